```python
import jax, jax.numpy as jnp
from jax import lax
import numpy as np

D_MODEL = 1024
BATCH = 8
SEQ = 4096
DEPTH = 2

N_MIXERS = 2
D_FF = 2816
CONV_WIDTH = 31
ATTN_GROUPS = ((128, 1), (512, 4), (2048, 16))
N_GROUPS = len(ATTN_GROUPS)
HEADS_PER_GROUP = 8
HEAD_DIM = 128
QKV_COLS = N_GROUPS * 3 * HEADS_PER_GROUP * HEAD_DIM
ATTN_OUT = HEADS_PER_GROUP * HEAD_DIM
ATTN_BLOCK = 128
NORM_EPS = 1e-6
N_CONV_LAYERS = (DEPTH + N_MIXERS - 1) // N_MIXERS
N_ATTN_LAYERS = DEPTH // N_MIXERS

kernel_name = "hybrid_conformer_conv_dilated_attn_macaron"


def rmsnorm(x, g):
    xf = x.astype(jnp.float32)
    y = xf * lax.rsqrt(jnp.mean(xf * xf, axis=-1, keepdims=True) + NORM_EPS)
    return (y * g.astype(jnp.float32)).astype(x.dtype)


def swiglu_ffn(h, w_in, w_out):
    gate, up = jnp.split(h @ w_in, 2, axis=-1)
    return (jax.nn.silu(gate) * up) @ w_out


def conformer_conv_module(h, w_pw1, b_pw1, w_dw, b_dw, norm_g, w_pw2, b_pw2):
    a, gate = jnp.split(h @ w_pw1 + b_pw1, 2, axis=-1)
    u = a * jax.nn.sigmoid(gate)
    u = lax.conv_general_dilated(
        u, w_dw[:, None, :].astype(u.dtype), window_strides=(1,),
        padding=((CONV_WIDTH - 1, 0),),
        dimension_numbers=("NWC", "WIO", "NWC"),
        feature_group_count=D_MODEL) + b_dw
    u = jax.nn.silu(rmsnorm(u, norm_g))
    return u @ w_pw2 + b_pw2


def _band_mask(n_blocks, steps):
    i = jnp.arange(ATTN_BLOCK)[:, None]
    j = jnp.arange(2 * ATTN_BLOCK)[None, :]
    diff = ATTN_BLOCK + i - j
    band = (diff >= 0) & (diff <= steps)
    key_pos = jnp.arange(n_blocks)[:, None, None] * ATTN_BLOCK - ATTN_BLOCK + j[None]
    return band[None] & (key_pos >= 0)


def dilated_window_attention(q, k, v, window, dilation):
    B, S, H, E = q.shape
    steps = window // dilation
    L = S // dilation
    nb = -(-L // ATTN_BLOCK)
    Lp = nb * ATTN_BLOCK

    def to_streams(t):
        t = t.reshape(B, L, dilation, H, E).transpose(0, 2, 1, 3, 4)
        t = jnp.pad(t, ((0, 0), (0, 0), (0, Lp - L), (0, 0), (0, 0)))
        return t.reshape(B, dilation, nb, ATTN_BLOCK, H, E)

    def with_prev(t):
        prev = jnp.concatenate([jnp.zeros_like(t[:, :, :1]), t[:, :, :-1]], axis=2)
        return jnp.concatenate([prev, t], axis=3)

    qb = to_streams(q)
    kw = with_prev(to_streams(k))
    vw = with_prev(to_streams(v))
    s = jnp.einsum("bdnqhe,bdnkhe->bdnhqk", qb, kw).astype(jnp.float32) * (HEAD_DIM ** -0.5)
    valid = _band_mask(nb, steps)[None, None, :, None]
    s = jnp.where(valid, s, -jnp.inf)
    m = jnp.max(s, axis=-1, keepdims=True)
    p = jnp.exp(s - m)
    l = jnp.sum(p, axis=-1, keepdims=True)
    o = jnp.einsum("bdnhqk,bdnkhe->bdnqhe", p / l, vw.astype(jnp.float32))
    lse = (m + jnp.log(l))[..., 0]
    o = o.reshape(B, dilation, Lp, H, E)[:, :, :L].transpose(0, 2, 1, 3, 4).reshape(B, S, H, E)
    lse = lse.transpose(0, 1, 2, 4, 3).reshape(B, dilation, Lp, H)[:, :, :L]
    lse = lse.transpose(0, 2, 1, 3).reshape(B, S, H)
    return o, lse


def dilated_attention_mixer(h, w_qkv, q_norm, k_norm, w_o):
    B, S, _ = h.shape
    qkv = (h @ w_qkv).reshape(B, S, N_GROUPS, 3, HEADS_PER_GROUP, HEAD_DIM)
    outs, lses = [], []
    for g, (window, dilation) in enumerate(ATTN_GROUPS):
        q = rmsnorm(qkv[:, :, g, 0], q_norm[g])
        k = rmsnorm(qkv[:, :, g, 1], k_norm[g])
        o, lse = dilated_window_attention(q, k, qkv[:, :, g, 2], window, dilation)
        outs.append(o)
        lses.append(lse)
    wts = jax.nn.softmax(jnp.stack(lses), axis=0)
    o = jnp.einsum("gbsh,gbshe->bshe", wts, jnp.stack(outs))
    return o.reshape(B, S, ATTN_OUT).astype(h.dtype) @ w_o


def _fwd_setup_inputs(seed: int = 0) -> dict:
    key = jax.random.key(seed)
    ks = jax.random.split(key, 16)
    nrm = lambda k, shape, scale: jax.random.normal(k, shape, jnp.float32) * scale
    return {
        "x": nrm(ks[0], (BATCH, SEQ, D_MODEL), 1.0),
        "norm_g": 1.0 + nrm(ks[1], (DEPTH, 3, D_MODEL), 0.02),
        "ffn_w_in": nrm(ks[2], (DEPTH, 2, D_MODEL, 2 * D_FF), D_MODEL ** -0.5),
        "ffn_w_out": nrm(ks[3], (DEPTH, 2, D_FF, D_MODEL), D_FF ** -0.5),
        "conv_w_pw1": nrm(ks[4], (N_CONV_LAYERS, D_MODEL, 2 * D_MODEL), D_MODEL ** -0.5),
        "conv_b_pw1": nrm(ks[5], (N_CONV_LAYERS, 2 * D_MODEL), 0.01),
        "conv_w_dw": nrm(ks[6], (N_CONV_LAYERS, CONV_WIDTH, D_MODEL), CONV_WIDTH ** -0.5),
        "conv_b_dw": nrm(ks[7], (N_CONV_LAYERS, D_MODEL), 0.01),
        "conv_norm_g": 1.0 + nrm(ks[8], (N_CONV_LAYERS, D_MODEL), 0.02),
        "conv_w_pw2": nrm(ks[9], (N_CONV_LAYERS, D_MODEL, D_MODEL), D_MODEL ** -0.5),
        "conv_b_pw2": nrm(ks[10], (N_CONV_LAYERS, D_MODEL), 0.01),
        "attn_w_qkv": nrm(ks[11], (N_ATTN_LAYERS, D_MODEL, QKV_COLS), D_MODEL ** -0.5),
        "attn_q_norm": 1.0 + nrm(ks[12], (N_ATTN_LAYERS, N_GROUPS, HEAD_DIM), 0.02),
        "attn_k_norm": 1.0 + nrm(ks[13], (N_ATTN_LAYERS, N_GROUPS, HEAD_DIM), 0.02),
        "attn_w_o": nrm(ks[14], (N_ATTN_LAYERS, ATTN_OUT, D_MODEL), ATTN_OUT ** -0.5),
    }


def _fwd_reference(x, norm_g, ffn_w_in, ffn_w_out, conv_w_pw1, conv_b_pw1, conv_w_dw, conv_b_dw,
              conv_norm_g, conv_w_pw2, conv_b_pw2, attn_w_qkv, attn_q_norm, attn_k_norm, attn_w_o):
    for layer in range(DEPTH):
        x = x + 0.5 * swiglu_ffn(rmsnorm(x, norm_g[layer, 0]), ffn_w_in[layer, 0], ffn_w_out[layer, 0])
        h = rmsnorm(x, norm_g[layer, 1])
        idx = layer // N_MIXERS
        if layer % N_MIXERS == 0:
            mix = conformer_conv_module(h, conv_w_pw1[idx], conv_b_pw1[idx], conv_w_dw[idx], conv_b_dw[idx],
                                        conv_norm_g[idx], conv_w_pw2[idx], conv_b_pw2[idx])
        else:
            mix = dilated_attention_mixer(h, attn_w_qkv[idx], attn_q_norm[idx], attn_k_norm[idx], attn_w_o[idx])
        x = x + mix
        x = x + 0.5 * swiglu_ffn(rmsnorm(x, norm_g[layer, 2]), ffn_w_in[layer, 1], ffn_w_out[layer, 1])
    return x


import jax as _jax
import jax.numpy as _jnp

TWIN_FORMAT = 'train_step'
FWD_PARAMS = ['x', 'norm_g', 'ffn_w_in', 'ffn_w_out', 'conv_w_pw1', 'conv_b_pw1', 'conv_w_dw', 'conv_b_dw', 'conv_norm_g', 'conv_w_pw2', 'conv_b_pw2', 'attn_w_qkv', 'attn_q_norm', 'attn_k_norm', 'attn_w_o']
TWIN_WEIGHTS = ['norm_g', 'ffn_w_in', 'ffn_w_out', 'conv_w_pw1', 'conv_b_pw1', 'conv_w_dw', 'conv_b_dw', 'conv_norm_g', 'conv_w_pw2', 'conv_b_pw2', 'attn_w_qkv', 'attn_q_norm', 'attn_k_norm', 'attn_w_o']
TWIN_DIFF_INPUT = 'x'
TWIN_INPUTS = ['x', 'norm_g', 'ffn_w_in', 'ffn_w_out', 'conv_w_pw1', 'conv_b_pw1', 'conv_w_dw', 'conv_b_dw', 'conv_norm_g', 'conv_w_pw2', 'conv_b_pw2', 'attn_w_qkv', 'attn_q_norm', 'attn_k_norm', 'attn_w_o', 'loss_target', 'm_norm_g', 'm_ffn_w_in', 'm_ffn_w_out', 'm_conv_w_pw1', 'm_conv_b_pw1', 'm_conv_w_dw', 'm_conv_b_dw', 'm_conv_norm_g', 'm_conv_w_pw2', 'm_conv_b_pw2', 'm_attn_w_qkv', 'm_attn_q_norm', 'm_attn_k_norm', 'm_attn_w_o', 'v_norm_g', 'v_ffn_w_in', 'v_ffn_w_out', 'v_conv_w_pw1', 'v_conv_b_pw1', 'v_conv_w_dw', 'v_conv_b_dw', 'v_conv_norm_g', 'v_conv_w_pw2', 'v_conv_b_pw2', 'v_attn_w_qkv', 'v_attn_q_norm', 'v_attn_k_norm', 'v_attn_w_o']
TWIN_OUTPUTS = ['loss', 'grad_x', 'grad_norm_g', 'grad_ffn_w_in', 'grad_ffn_w_out', 'grad_conv_w_pw1', 'grad_conv_b_pw1', 'grad_conv_w_dw', 'grad_conv_b_dw', 'grad_conv_norm_g', 'grad_conv_w_pw2', 'grad_conv_b_pw2', 'grad_attn_w_qkv', 'grad_attn_q_norm', 'grad_attn_k_norm', 'grad_attn_w_o', 'delta_norm_g', 'delta_ffn_w_in', 'delta_ffn_w_out', 'delta_conv_w_pw1', 'delta_conv_b_pw1', 'delta_conv_w_dw', 'delta_conv_b_dw', 'delta_conv_norm_g', 'delta_conv_w_pw2', 'delta_conv_b_pw2', 'delta_attn_w_qkv', 'delta_attn_q_norm', 'delta_attn_k_norm', 'delta_attn_w_o', 'new_m_norm_g', 'new_m_ffn_w_in', 'new_m_ffn_w_out', 'new_m_conv_w_pw1', 'new_m_conv_b_pw1', 'new_m_conv_w_dw', 'new_m_conv_b_dw', 'new_m_conv_norm_g', 'new_m_conv_w_pw2', 'new_m_conv_b_pw2', 'new_m_attn_w_qkv', 'new_m_attn_q_norm', 'new_m_attn_k_norm', 'new_m_attn_w_o', 'new_v_norm_g', 'new_v_ffn_w_in', 'new_v_ffn_w_out', 'new_v_conv_w_pw1', 'new_v_conv_b_pw1', 'new_v_conv_w_dw', 'new_v_conv_b_dw', 'new_v_conv_norm_g', 'new_v_conv_w_pw2', 'new_v_conv_b_pw2', 'new_v_attn_w_qkv', 'new_v_attn_q_norm', 'new_v_attn_k_norm', 'new_v_attn_w_o']
TWIN_LEAF_KINDS = {'loss': 'loss', 'grad_x': 'grad_x', 'grad_norm_g': 'grad_w', 'grad_ffn_w_in': 'grad_w', 'grad_ffn_w_out': 'grad_w', 'grad_conv_w_pw1': 'grad_w', 'grad_conv_b_pw1': 'grad_w', 'grad_conv_w_dw': 'grad_w', 'grad_conv_b_dw': 'grad_w', 'grad_conv_norm_g': 'grad_w', 'grad_conv_w_pw2': 'grad_w', 'grad_conv_b_pw2': 'grad_w', 'grad_attn_w_qkv': 'grad_w', 'grad_attn_q_norm': 'grad_w', 'grad_attn_k_norm': 'grad_w', 'grad_attn_w_o': 'grad_w', 'delta_norm_g': 'delta_w', 'delta_ffn_w_in': 'delta_w', 'delta_ffn_w_out': 'delta_w', 'delta_conv_w_pw1': 'delta_w', 'delta_conv_b_pw1': 'delta_w', 'delta_conv_w_dw': 'delta_w', 'delta_conv_b_dw': 'delta_w', 'delta_conv_norm_g': 'delta_w', 'delta_conv_w_pw2': 'delta_w', 'delta_conv_b_pw2': 'delta_w', 'delta_attn_w_qkv': 'delta_w', 'delta_attn_q_norm': 'delta_w', 'delta_attn_k_norm': 'delta_w', 'delta_attn_w_o': 'delta_w', 'new_m_norm_g': 'new_m', 'new_m_ffn_w_in': 'new_m', 'new_m_ffn_w_out': 'new_m', 'new_m_conv_w_pw1': 'new_m', 'new_m_conv_b_pw1': 'new_m', 'new_m_conv_w_dw': 'new_m', 'new_m_conv_b_dw': 'new_m', 'new_m_conv_norm_g': 'new_m', 'new_m_conv_w_pw2': 'new_m', 'new_m_conv_b_pw2': 'new_m', 'new_m_attn_w_qkv': 'new_m', 'new_m_attn_q_norm': 'new_m', 'new_m_attn_k_norm': 'new_m', 'new_m_attn_w_o': 'new_m', 'new_v_norm_g': 'new_v', 'new_v_ffn_w_in': 'new_v', 'new_v_ffn_w_out': 'new_v', 'new_v_conv_w_pw1': 'new_v', 'new_v_conv_b_pw1': 'new_v', 'new_v_conv_w_dw': 'new_v', 'new_v_conv_b_dw': 'new_v', 'new_v_conv_norm_g': 'new_v', 'new_v_conv_w_pw2': 'new_v', 'new_v_conv_b_pw2': 'new_v', 'new_v_attn_w_qkv': 'new_v', 'new_v_attn_q_norm': 'new_v', 'new_v_attn_k_norm': 'new_v', 'new_v_attn_w_o': 'new_v'}


def _forward(args):
    return _fwd_reference(*[args[k] for k in FWD_PARAMS])


def _output_shape():
    out = _jax.eval_shape(lambda: _forward(_fwd_setup_inputs(0)))
    return out.shape, out.dtype

N_MICROBATCH = 1
ADAM_LR = 0.001
ADAM_B1 = 0.9
ADAM_B2 = 0.999
ADAM_EPS = 1e-08
ADAM_WD = 0.01
ADAM_STEP = 10
PER_EXAMPLE_BATCH_AXIS = {'x': 0, 'loss_target': 0}
SHARED_INPUTS = []
_WEIGHT_DTYPES = {'norm_g': _jnp.float32, 'ffn_w_in': _jnp.float32, 'ffn_w_out': _jnp.float32, 'conv_w_pw1': _jnp.float32, 'conv_b_pw1': _jnp.float32, 'conv_w_dw': _jnp.float32, 'conv_b_dw': _jnp.float32, 'conv_norm_g': _jnp.float32, 'conv_w_pw2': _jnp.float32, 'conv_b_pw2': _jnp.float32, 'attn_w_qkv': _jnp.float32, 'attn_q_norm': _jnp.float32, 'attn_k_norm': _jnp.float32, 'attn_w_o': _jnp.float32}
MOMENT_SCALE = {'norm_g': 4.897823e+00, 'ffn_w_in': 1.088448e-01, 'ffn_w_out': 1.742121e-01, 'conv_w_pw1': 1.985507e-01, 'conv_b_pw1': 7.350514e+00, 'conv_w_dw': 4.917500e-01, 'conv_b_dw': 1.811628e+01, 'conv_norm_g': 1.483390e+01, 'conv_w_pw2': 1.845822e+00, 'conv_b_pw2': 9.315008e+00, 'attn_w_qkv': 1.965605e-01, 'attn_q_norm': 4.769978e-01, 'attn_k_norm': 4.781005e-01, 'attn_w_o': 5.993947e-01}


def _to_microbatches(a, axis):
    t = _jnp.moveaxis(a, axis, 0)
    t = t.reshape((N_MICROBATCH, t.shape[0] // N_MICROBATCH) + t.shape[1:])
    return _jnp.moveaxis(t, 1, axis + 1)


def setup_inputs(seed: int = 0) -> dict:
    inp = _fwd_setup_inputs(seed)
    key = _jax.random.fold_in(_jax.random.key(seed), 7919)
    shape, _ = _output_shape()
    out = dict(inp)
    out["loss_target"] = _jax.random.normal(_jax.random.fold_in(key, 0), shape, _jnp.float32)
    for i, name in enumerate(TWIN_WEIGHTS):
        w = inp[name].astype(_jnp.float32)
        if MOMENT_SCALE is None:
            s = _jnp.sqrt(_jnp.mean(_jnp.square(w)) + 1e-30)
        else:
            s = MOMENT_SCALE[name]
        km, kv = _jax.random.split(_jax.random.fold_in(key, i + 1))
        out[name] = w
        out["m_" + name] = s * _jax.random.normal(km, w.shape, _jnp.float32)
        out["v_" + name] = (s * s) * _jax.random.uniform(kv, w.shape, _jnp.float32, 0.5, 1.5)
    if N_MICROBATCH > 1:
        for name, axis in PER_EXAMPLE_BATCH_AXIS.items():
            out[name] = _to_microbatches(out[name], axis)
    return {'x': out['x'], 'norm_g': out['norm_g'], 'ffn_w_in': out['ffn_w_in'], 'ffn_w_out': out['ffn_w_out'], 'conv_w_pw1': out['conv_w_pw1'], 'conv_b_pw1': out['conv_b_pw1'], 'conv_w_dw': out['conv_w_dw'], 'conv_b_dw': out['conv_b_dw'], 'conv_norm_g': out['conv_norm_g'], 'conv_w_pw2': out['conv_w_pw2'], 'conv_b_pw2': out['conv_b_pw2'], 'attn_w_qkv': out['attn_w_qkv'], 'attn_q_norm': out['attn_q_norm'], 'attn_k_norm': out['attn_k_norm'], 'attn_w_o': out['attn_w_o'], 'loss_target': out['loss_target'], 'm_norm_g': out['m_norm_g'], 'm_ffn_w_in': out['m_ffn_w_in'], 'm_ffn_w_out': out['m_ffn_w_out'], 'm_conv_w_pw1': out['m_conv_w_pw1'], 'm_conv_b_pw1': out['m_conv_b_pw1'], 'm_conv_w_dw': out['m_conv_w_dw'], 'm_conv_b_dw': out['m_conv_b_dw'], 'm_conv_norm_g': out['m_conv_norm_g'], 'm_conv_w_pw2': out['m_conv_w_pw2'], 'm_conv_b_pw2': out['m_conv_b_pw2'], 'm_attn_w_qkv': out['m_attn_w_qkv'], 'm_attn_q_norm': out['m_attn_q_norm'], 'm_attn_k_norm': out['m_attn_k_norm'], 'm_attn_w_o': out['m_attn_w_o'], 'v_norm_g': out['v_norm_g'], 'v_ffn_w_in': out['v_ffn_w_in'], 'v_ffn_w_out': out['v_ffn_w_out'], 'v_conv_w_pw1': out['v_conv_w_pw1'], 'v_conv_b_pw1': out['v_conv_b_pw1'], 'v_conv_w_dw': out['v_conv_w_dw'], 'v_conv_b_dw': out['v_conv_b_dw'], 'v_conv_norm_g': out['v_conv_norm_g'], 'v_conv_w_pw2': out['v_conv_w_pw2'], 'v_conv_b_pw2': out['v_conv_b_pw2'], 'v_attn_w_qkv': out['v_attn_w_qkv'], 'v_attn_q_norm': out['v_attn_q_norm'], 'v_attn_k_norm': out['v_attn_k_norm'], 'v_attn_w_o': out['v_attn_w_o']}


def _loss(weights, diff, rest, loss_target):
    with _jax.named_scope("forward"):
        args = {**rest, TWIN_DIFF_INPUT: diff, **{k: w.astype(_WEIGHT_DTYPES[k]) for k, w in weights.items()}}
        y = _forward(args)
    with _jax.named_scope("loss_head"):
        err = _jnp.square(y.astype(_jnp.float32) - loss_target)
        return 0.5 * _jnp.sum(_jnp.mean(err, axis=-1)) if err.ndim else 0.5 * err


def _adamw(w, g, m, v):
    m = ADAM_B1 * m + (1.0 - ADAM_B1) * g
    v = ADAM_B2 * v + (1.0 - ADAM_B2) * _jnp.square(g)
    m_hat = m / (1.0 - ADAM_B1 ** ADAM_STEP)
    v_hat = v / (1.0 - ADAM_B2 ** ADAM_STEP)
    delta = -ADAM_LR * (m_hat / (_jnp.sqrt(v_hat) + ADAM_EPS) + ADAM_WD * w)
    return delta, m, v


def reference(x, norm_g, ffn_w_in, ffn_w_out, conv_w_pw1, conv_b_pw1, conv_w_dw, conv_b_dw, conv_norm_g, conv_w_pw2, conv_b_pw2, attn_w_qkv, attn_q_norm, attn_k_norm, attn_w_o, loss_target, m_norm_g, m_ffn_w_in, m_ffn_w_out, m_conv_w_pw1, m_conv_b_pw1, m_conv_w_dw, m_conv_b_dw, m_conv_norm_g, m_conv_w_pw2, m_conv_b_pw2, m_attn_w_qkv, m_attn_q_norm, m_attn_k_norm, m_attn_w_o, v_norm_g, v_ffn_w_in, v_ffn_w_out, v_conv_w_pw1, v_conv_b_pw1, v_conv_w_dw, v_conv_b_dw, v_conv_norm_g, v_conv_w_pw2, v_conv_b_pw2, v_attn_w_qkv, v_attn_q_norm, v_attn_k_norm, v_attn_w_o):
    given = dict(x=x, norm_g=norm_g, ffn_w_in=ffn_w_in, ffn_w_out=ffn_w_out, conv_w_pw1=conv_w_pw1, conv_b_pw1=conv_b_pw1, conv_w_dw=conv_w_dw, conv_b_dw=conv_b_dw, conv_norm_g=conv_norm_g, conv_w_pw2=conv_w_pw2, conv_b_pw2=conv_b_pw2, attn_w_qkv=attn_w_qkv, attn_q_norm=attn_q_norm, attn_k_norm=attn_k_norm, attn_w_o=attn_w_o, loss_target=loss_target, m_norm_g=m_norm_g, m_ffn_w_in=m_ffn_w_in, m_ffn_w_out=m_ffn_w_out, m_conv_w_pw1=m_conv_w_pw1, m_conv_b_pw1=m_conv_b_pw1, m_conv_w_dw=m_conv_w_dw, m_conv_b_dw=m_conv_b_dw, m_conv_norm_g=m_conv_norm_g, m_conv_w_pw2=m_conv_w_pw2, m_conv_b_pw2=m_conv_b_pw2, m_attn_w_qkv=m_attn_w_qkv, m_attn_q_norm=m_attn_q_norm, m_attn_k_norm=m_attn_k_norm, m_attn_w_o=m_attn_w_o, v_norm_g=v_norm_g, v_ffn_w_in=v_ffn_w_in, v_ffn_w_out=v_ffn_w_out, v_conv_w_pw1=v_conv_w_pw1, v_conv_b_pw1=v_conv_b_pw1, v_conv_w_dw=v_conv_w_dw, v_conv_b_dw=v_conv_b_dw, v_conv_norm_g=v_conv_norm_g, v_conv_w_pw2=v_conv_w_pw2, v_conv_b_pw2=v_conv_b_pw2, v_attn_w_qkv=v_attn_w_qkv, v_attn_q_norm=v_attn_q_norm, v_attn_k_norm=v_attn_k_norm, v_attn_w_o=v_attn_w_o)
    weights = {n: given[n] for n in TWIN_WEIGHTS}
    shared = {n: given[n] for n in SHARED_INPUTS}
    per_example = {n: given[n] for n in ['x']}
    grad_fn = _jax.value_and_grad(_loss, argnums=(0, 1))

    def one_microbatch(ex, loss_target):
        ex = dict(ex)
        diff = ex.pop(TWIN_DIFF_INPUT)
        return grad_fn(weights, diff, {**shared, **ex}, loss_target)

    if N_MICROBATCH == 1:
        loss, (grad_w, grad_x) = one_microbatch(per_example, given["loss_target"])
    else:
        def body(carry, xs):
            loss_sum, grad_sum = carry
            l_k, (gw_k, gx_k) = one_microbatch(xs[0], xs[1])
            with _jax.named_scope("update"):
                return (loss_sum + l_k, _jax.tree.map(_jnp.add, grad_sum, gw_k)), gx_k

        init = (_jnp.zeros((), _jnp.float32), _jax.tree.map(_jnp.zeros_like, weights))
        (loss, grad_w), grad_x = _jax.lax.scan(body, init, (per_example, given["loss_target"]))
    with _jax.named_scope("update"):
        delta_w, new_m, new_v = {}, {}, {}
        for n in TWIN_WEIGHTS:
            delta_w[n], new_m[n], new_v[n] = _adamw(weights[n], grad_w[n], given["m_" + n], given["v_" + n])
    return (loss, grad_x, *[grad_w[n] for n in TWIN_WEIGHTS], *[delta_w[n] for n in TWIN_WEIGHTS],
            *[new_m[n] for n in TWIN_WEIGHTS], *[new_v[n] for n in TWIN_WEIGHTS])
```

```python
import functools

import jax
import jax.numpy as jnp
from jax import lax
from jax.experimental import pallas as pl
from jax.experimental.pallas import tpu as pltpu

F32 = jnp.float32
BF16 = jnp.bfloat16
I32 = jnp.int32

NORM_EPS = 1e-6
HEAD_DIM = 128
ATTN_BLOCK = 128
ATTN_DILATIONS = (1, 4, 16)
CONV_WIDTH = 31
CONV_HALO = 32
ADAM_LR, ADAM_B1, ADAM_B2, ADAM_EPS, ADAM_WD, ADAM_STEP = 0.001, 0.9, 0.999, 1e-08, 0.01, 10
N_DEV = 8
NEG_BIG = -1e30
V7X_VMEM_BYTES = 64 * 1024 * 1024
VMEM_LIMIT = V7X_VMEM_BYTES - 8 * 1024 * 1024
MESH = pl.DeviceIdType.MESH


def _params(*sem):
    return pltpu.CompilerParams(dimension_semantics=sem or None, vmem_limit_bytes=VMEM_LIMIT)


def _dot_nn(a, b):
    return lax.dot_general(a, b, (((1,), (0,)), ((), ())), preferred_element_type=F32)


def _dot_nt(a, b):
    return lax.dot_general(a, b, (((1,), (1,)), ((), ())), preferred_element_type=F32)


def _dot_tn(a, b):
    return lax.dot_general(a, b, (((0,), (0,)), ((), ())), preferred_element_type=F32)


def _sigmoid(x):
    return 1.0 / (1.0 + jnp.exp(-x))


def _tile(n, pref):
    if n <= pref:
        return n
    for t in range(pref - pref % 128, 0, -128):
        if n % t == 0:
            return t
    return n


def _rows8(v):
    tm, c = v.shape
    return v.reshape(tm // 8, 8, c).sum(axis=0)


def _rmsnorm(x, g, name):
    T, D = x.shape
    tm = min(T, 512)

    def body(x_ref, g_ref, h_ref):
        xv = x_ref[...]
        r = lax.rsqrt(jnp.mean(xv * xv, axis=-1, keepdims=True) + NORM_EPS)
        h_ref[...] = (xv * r * g_ref[...]).astype(BF16)

    return pl.pallas_call(
        body, name=name, grid=(T // tm,),
        in_specs=[pl.BlockSpec((tm, D), lambda i: (i, 0)), pl.BlockSpec((1, D), lambda i: (0, 0))],
        out_specs=pl.BlockSpec((tm, D), lambda i: (i, 0)),
        out_shape=jax.ShapeDtypeStruct((T, D), BF16),
        compiler_params=_params("parallel"),
    )(x, g)


def _gated_in(h, wt, bias, act, u_dtype, name):
    T, D = h.shape
    F = wt.shape[0] // 2
    tn = _tile(F, 1408)
    tm = min(T, 512)
    nn = F // tn

    def body(*refs):
        if bias is None:
            h_ref, w0_ref, w1_ref, z_ref, u_ref = refs
        else:
            h_ref, w0_ref, w1_ref, b_ref, z_ref, u_ref = refs
        hv = h_ref[...]
        z0 = _dot_nt(hv, w0_ref[...])
        z1 = _dot_nt(hv, w1_ref[...])
        if bias is not None:
            z0 = z0 + b_ref[0:1, :]
            z1 = z1 + b_ref[1:2, :]
        z_ref[0] = z0.astype(BF16)
        z_ref[1] = z1.astype(BF16)
        if act == "swiglu":
            u = z0 * _sigmoid(z0) * z1
        else:
            u = z0 * _sigmoid(z1)
        u_ref[...] = u.astype(u_dtype)

    in_specs = [pl.BlockSpec((tm, D), lambda n, m: (m, 0)),
                pl.BlockSpec((tn, D), lambda n, m: (n, 0)),
                pl.BlockSpec((tn, D), lambda n, m: (n + nn, 0))]
    args = [h, wt, wt]
    if bias is not None:
        in_specs.append(pl.BlockSpec((2, tn), lambda n, m: (0, n)))
        args.append(bias)
    return pl.pallas_call(
        body, name=name, grid=(nn, T // tm), in_specs=in_specs,
        out_specs=[pl.BlockSpec((2, tm, tn), lambda n, m: (0, m, n)), pl.BlockSpec((tm, tn), lambda n, m: (m, n))],
        out_shape=[jax.ShapeDtypeStruct((2, T, F), BF16), jax.ShapeDtypeStruct((T, F), u_dtype)],
        compiler_params=_params("parallel", "parallel"),
    )(*args)


def _swiglu_bwd(dxo, wout, z3, scale, name):
    T, D = dxo.shape
    F = wout.shape[0]
    tn = _tile(F, 1408)
    tm = min(T, 512)

    def body(d_ref, w_ref, z_ref, dz_ref):
        dy = (d_ref[...] * scale).astype(BF16)
        da = _dot_nt(dy, w_ref[...])
        gate = z_ref[0].astype(F32)
        up = z_ref[1].astype(F32)
        sg = _sigmoid(gate)
        dz_ref[0] = (da * up * (sg * (1.0 + gate * (1.0 - sg)))).astype(BF16)
        dz_ref[1] = (da * gate * sg).astype(BF16)

    return pl.pallas_call(
        body, name=name, grid=(F // tn, T // tm),
        in_specs=[pl.BlockSpec((tm, D), lambda n, m: (m, 0)),
                  pl.BlockSpec((tn, D), lambda n, m: (n, 0)),
                  pl.BlockSpec((2, tm, tn), lambda n, m: (0, m, n))],
        out_specs=pl.BlockSpec((2, tm, tn), lambda n, m: (0, m, n)),
        out_shape=jax.ShapeDtypeStruct((2, T, F), BF16),
        compiler_params=_params("parallel", "parallel"),
    )(dxo, wout, z3)


def _mm_nt(a, wt, out_dtype, name):
    T, K = a.shape
    N = wt.shape[0]
    tn = _tile(N, 1152)
    tm = min(T, 512)

    def body(a_ref, w_ref, o_ref):
        o_ref[...] = _dot_nt(a_ref[...].astype(BF16), w_ref[...]).astype(out_dtype)

    return pl.pallas_call(
        body, name=name, grid=(N // tn, T // tm),
        in_specs=[pl.BlockSpec((tm, K), lambda n, m: (m, 0)), pl.BlockSpec((tn, K), lambda n, m: (n, 0))],
        out_specs=pl.BlockSpec((tm, tn), lambda n, m: (m, n)),
        out_shape=jax.ShapeDtypeStruct((T, N), out_dtype),
        compiler_params=_params("parallel", "parallel"),
    )(a, wt)


def _mm_tn(a, b, b_scale, name):
    if a.ndim == 3:
        J, T, F = a.shape
    else:
        (T, F), J = a.shape, 1
    N = b.shape[1]
    tmm = _tile(F, 1408)
    tk = min(T, 1024)
    npj = F // tmm
    nk = T // tk

    def body(a_ref, b_ref, o_ref, acc_ref):
        k = pl.program_id(1)
        bv = b_ref[...]
        if b_scale != 1.0:
            bv = bv * b_scale
        part = _dot_tn(a_ref[...].astype(BF16), bv.astype(BF16))

        @pl.when(k == 0)
        def _():
            acc_ref[...] = part

        @pl.when(k > 0)
        def _():
            acc_ref[...] += part

        @pl.when(k == nk - 1)
        def _():
            o_ref[...] = acc_ref[...].astype(BF16)

    if a.ndim == 3:
        a_spec = pl.BlockSpec((None, tk, tmm), lambda i, k: (i // npj, k, i % npj))
    else:
        a_spec = pl.BlockSpec((tk, tmm), lambda i, k: (k, i))
    return pl.pallas_call(
        body, name=name, grid=(J * npj, nk),
        in_specs=[a_spec, pl.BlockSpec((tk, N), lambda i, k: (k, 0))],
        out_specs=pl.BlockSpec((tmm, N), lambda i, k: (i, 0)),
        out_shape=jax.ShapeDtypeStruct((J * F, N), BF16),
        scratch_shapes=[pltpu.VMEM((tmm, N), F32)],
        compiler_params=_params("parallel", "arbitrary"),
    )(a, b)


def _rows_mm(a, b, mode, name, *, x=None, scale=1.0, bias=None, tgt=None, g=None, dxo=None):
    if a.ndim == 3:
        J, T, F = a.shape
    else:
        (T, F), J = a.shape, 1
    D = b.shape[1]
    tk = _tile(F, 1408)
    kpj = F // tk
    nk = J * kpj
    tm = min(T, 512)
    nm = T // tm

    def body(*refs):
        refs = list(refs)
        a_ref, b_ref = refs[:2]
        rest = refs[2:]
        acc_ref = rest.pop()
        m, k = pl.program_id(0), pl.program_id(1)
        part = _dot_nn(a_ref[...], b_ref[...])

        @pl.when(k == 0)
        def _():
            acc_ref[...] = part

        @pl.when(k > 0)
        def _():
            acc_ref[...] += part

        @pl.when(k == nk - 1)
        def _():
            acc = acc_ref[...]
            if mode == "res":
                if bias is None:
                    x_ref, o_ref = rest
                else:
                    x_ref, bias_ref, o_ref = rest
                    acc = acc + bias_ref[...]
                o_ref[...] = x_ref[...] + scale * acc
            elif mode == "loss":
                x_ref, t_ref, o_ref, l_ref = rest
                e = x_ref[...] + scale * acc - t_ref[...]
                o_ref[...] = e / D

                @pl.when(m == 0)
                def _():
                    l_ref[...] = jnp.zeros_like(l_ref)

                l_ref[...] += _rows8(e * e)
            else:
                x_ref, g_ref, d_ref, o_ref, dg_ref = rest
                xv = x_ref[...]
                r = lax.rsqrt(jnp.mean(xv * xv, axis=-1, keepdims=True) + NORM_EPS)
                xh = xv * r
                dxh = acc * g_ref[...]
                o_ref[...] = d_ref[...] + r * (dxh - xh * jnp.mean(dxh * xh, axis=-1, keepdims=True))

                @pl.when(m == 0)
                def _():
                    dg_ref[...] = jnp.zeros_like(dg_ref)

                dg_ref[...] += _rows8(acc * xh)

    if a.ndim == 3:
        a_spec = pl.BlockSpec((None, tm, tk), lambda m, k: (k // kpj, m, k % kpj))
    else:
        a_spec = pl.BlockSpec((tm, tk), lambda m, k: (m, k))
    row = pl.BlockSpec((tm, D), lambda m, k: (m, 0))
    vec = pl.BlockSpec((1, D), lambda m, k: (0, 0))
    part8 = pl.BlockSpec((8, D), lambda m, k: (0, 0))
    in_specs = [a_spec, pl.BlockSpec((tk, D), lambda m, k: (k, 0))]
    args = [a, b]
    full = jax.ShapeDtypeStruct((T, D), F32)
    small = jax.ShapeDtypeStruct((8, D), F32)
    if mode == "res":
        in_specs.append(row)
        args.append(x)
        if bias is not None:
            in_specs.append(vec)
            args.append(bias)
        out_specs, out_shape = row, full
    elif mode == "loss":
        in_specs += [row, row]
        args += [x, tgt]
        out_specs, out_shape = [row, part8], [full, small]
    else:
        in_specs += [row, vec, row]
        args += [x, g, dxo]
        out_specs, out_shape = [row, part8], [full, small]
    return pl.pallas_call(
        body, name=name, grid=(nm, nk), in_specs=in_specs, out_specs=out_specs, out_shape=out_shape,
        scratch_shapes=[pltpu.VMEM((tm, D), F32)],
        compiler_params=_params("arbitrary", "arbitrary"),
    )(*args)


def _dwconv_fwd(u, w, b_dw, cg, name):
    T, C = u.shape
    tm = min(T, 256)
    hb = tm // CONV_HALO
    pad = CONV_HALO - (CONV_WIDTH - 1)

    def body(u_ref, halo_ref, w_ref, b_ref, g_ref, v_ref, s_ref, buf):
        i = pl.program_id(0)
        buf[pl.ds(0, CONV_HALO), :] = jnp.where(i > 0, halo_ref[...], 0.0)
        buf[pl.ds(CONV_HALO, tm), :] = u_ref[...]
        acc = jnp.zeros((tm, C), F32) + b_ref[...]
        for k in range(CONV_WIDTH):
            acc = acc + w_ref[k:k + 1, :] * buf[pl.ds(pad + k, tm), :]
        v_ref[...] = acc
        r = lax.rsqrt(jnp.mean(acc * acc, axis=-1, keepdims=True) + NORM_EPS)
        n = acc * r * g_ref[...]
        s_ref[...] = (n * _sigmoid(n)).astype(BF16)

    row = pl.BlockSpec((tm, C), lambda i: (i, 0))
    vec = pl.BlockSpec((1, C), lambda i: (0, 0))
    return pl.pallas_call(
        body, name=name, grid=(T // tm,),
        in_specs=[row, pl.BlockSpec((CONV_HALO, C), lambda i: (jnp.maximum(i * hb - 1, 0), 0)),
                  pl.BlockSpec((CONV_WIDTH, C), lambda i: (0, 0)), vec, vec],
        out_specs=[row, row],
        out_shape=[jax.ShapeDtypeStruct((T, C), F32), jax.ShapeDtypeStruct((T, C), BF16)],
        scratch_shapes=[pltpu.VMEM((CONV_HALO + tm, C), F32)],
        compiler_params=_params("parallel"),
    )(u, u, w, b_dw, cg)


def _conv_post_bwd(ds, v, cg, dxo, name):
    T, C = v.shape
    tm = min(T, 512)

    def body(ds_ref, v_ref, g_ref, d_ref, dv_ref, dcg_ref, dbdw_ref, dbpw2_ref):
        @pl.when(pl.program_id(0) == 0)
        def _():
            dcg_ref[...] = jnp.zeros_like(dcg_ref)
            dbdw_ref[...] = jnp.zeros_like(dbdw_ref)
            dbpw2_ref[...] = jnp.zeros_like(dbpw2_ref)

        vv = v_ref[...]
        r = lax.rsqrt(jnp.mean(vv * vv, axis=-1, keepdims=True) + NORM_EPS)
        vh = vv * r
        n = vh * g_ref[...]
        sg = _sigmoid(n)
        dn = ds_ref[...] * (sg * (1.0 + n * (1.0 - sg)))
        dvh = dn * g_ref[...]
        dv = r * (dvh - vh * jnp.mean(dvh * vh, axis=-1, keepdims=True))
        dv_ref[...] = dv
        dcg_ref[...] += _rows8(dn * vh)
        dbdw_ref[...] += _rows8(dv)
        dbpw2_ref[...] += _rows8(d_ref[...])

    row = pl.BlockSpec((tm, C), lambda i: (i, 0))
    part8 = pl.BlockSpec((8, C), lambda i: (0, 0))
    small = jax.ShapeDtypeStruct((8, C), F32)
    return pl.pallas_call(
        body, name=name, grid=(T // tm,),
        in_specs=[row, row, pl.BlockSpec((1, C), lambda i: (0, 0)), row],
        out_specs=[row, part8, part8, part8],
        out_shape=[jax.ShapeDtypeStruct((T, C), F32), small, small, small],
        compiler_params=_params("arbitrary"),
    )(ds, v, cg, dxo)


def _dwconv_bwd(dv, u, w, z3, name):
    T, C = u.shape
    tm = min(T, 256)
    hb = tm // CONV_HALO
    nt = T // tm
    pad = CONV_HALO - (CONV_WIDTH - 1)

    def body(dv_ref, dvn_ref, u_ref, up_ref, w_ref, z_ref, dz_ref, dw_ref, db_ref, dvbuf, ubuf, dwacc):
        i = pl.program_id(0)

        @pl.when(i == 0)
        def _():
            dwacc[...] = jnp.zeros_like(dwacc)
            db_ref[...] = jnp.zeros_like(db_ref)

        dvv = dv_ref[...]
        dvbuf[pl.ds(0, tm), :] = dvv
        dvbuf[pl.ds(tm, CONV_HALO), :] = jnp.where(i < nt - 1, dvn_ref[...], 0.0)
        ubuf[pl.ds(0, CONV_HALO), :] = jnp.where(i > 0, up_ref[...], 0.0)
        ubuf[pl.ds(CONV_HALO, tm), :] = u_ref[...]
        du = jnp.zeros((tm, C), F32)
        for k in range(CONV_WIDTH):
            du = du + w_ref[k:k + 1, :] * dvbuf[pl.ds(CONV_WIDTH - 1 - k, tm), :]
            dwacc[k] += _rows8(dvv * ubuf[pl.ds(pad + k, tm), :])
        a = z_ref[0].astype(F32)
        gate = z_ref[1].astype(F32)
        sg = _sigmoid(gate)
        da = du * sg
        dgate = du * a * sg * (1.0 - sg)
        dz_ref[0] = da.astype(BF16)
        dz_ref[1] = dgate.astype(BF16)
        db_ref[0:8, :] += _rows8(da)
        db_ref[8:16, :] += _rows8(dgate)

        @pl.when(i == nt - 1)
        def _():
            dw_ref[...] = jnp.zeros_like(dw_ref)
            for k in range(CONV_WIDTH):
                dw_ref[k:k + 1, :] = jnp.sum(dwacc[k], axis=0, keepdims=True)

    row = pl.BlockSpec((tm, C), lambda i: (i, 0))
    last_halo = T // CONV_HALO - 1
    return pl.pallas_call(
        body, name=name, grid=(nt,),
        in_specs=[row, pl.BlockSpec((CONV_HALO, C), lambda i: (jnp.minimum((i + 1) * hb, last_halo), 0)),
                  row, pl.BlockSpec((CONV_HALO, C), lambda i: (jnp.maximum(i * hb - 1, 0), 0)),
                  pl.BlockSpec((CONV_WIDTH, C), lambda i: (0, 0)),
                  pl.BlockSpec((2, tm, C), lambda i: (0, i, 0))],
        out_specs=[pl.BlockSpec((2, tm, C), lambda i: (0, i, 0)),
                   pl.BlockSpec((CONV_HALO, C), lambda i: (0, 0)),
                   pl.BlockSpec((16, C), lambda i: (0, 0))],
        out_shape=[jax.ShapeDtypeStruct((2, T, C), BF16), jax.ShapeDtypeStruct((CONV_HALO, C), F32),
                   jax.ShapeDtypeStruct((16, C), F32)],
        scratch_shapes=[pltpu.VMEM((tm + CONV_HALO, C), F32), pltpu.VMEM((CONV_HALO + tm, C), F32),
                        pltpu.VMEM((CONV_WIDTH, 8, C), F32)],
        compiler_params=_params("arbitrary"),
    )(dv, dv, u, u, w, z3)


def _head_norm(raw, gain):
    xv = raw.astype(F32)
    r = lax.rsqrt(jnp.mean(xv * xv, axis=-1, keepdims=True) + NORM_EPS)
    xh = xv * r
    return (xh * gain).astype(BF16), xh, r


def _band_masks(n):
    row = lax.broadcasted_iota(I32, (ATTN_BLOCK, ATTN_BLOCK), 0)
    col = lax.broadcasted_iota(I32, (ATTN_BLOCK, ATTN_BLOCK), 1)
    return (col >= row) & (n > 0), col <= row


def _attn_fwd(qkv, qg, kg, g, name):
    T, W = qkv.shape
    HD = W // 9
    H = HD // HEAD_DIM
    d = ATTN_DILATIONS[g]
    L = T // d
    nb = L // ATTN_BLOCK
    scale = HEAD_DIM ** -0.5

    def body(q_ref, kp_ref, kc_ref, vp_ref, vc_ref, qg_ref, kg_ref, o_ref, lse_ref):
        mask_p, mask_c = _band_masks(pl.program_id(1))
        lane = lax.broadcasted_iota(I32, (ATTN_BLOCK, 128), 1)
        gq = qg_ref[g:g + 1, :]
        gk = kg_ref[g:g + 1, :]
        lse_tile = jnp.zeros((ATTN_BLOCK, 128), F32)
        for h in range(H):
            sl = slice(h * HEAD_DIM, (h + 1) * HEAD_DIM)
            q = _head_norm(q_ref[:, sl], gq)[0]
            kp = _head_norm(kp_ref[:, sl], gk)[0]
            kc = _head_norm(kc_ref[:, sl], gk)[0]
            sp = jnp.where(mask_p, _dot_nt(q, kp) * scale, NEG_BIG)
            sc = jnp.where(mask_c, _dot_nt(q, kc) * scale, NEG_BIG)
            m = jnp.maximum(jnp.max(sp, axis=-1, keepdims=True), jnp.max(sc, axis=-1, keepdims=True))
            pp = jnp.exp(sp - m)
            pc = jnp.exp(sc - m)
            l = jnp.sum(pp, axis=-1, keepdims=True) + jnp.sum(pc, axis=-1, keepdims=True)
            o = _dot_nn(pp.astype(BF16), vp_ref[:, sl]) + _dot_nn(pc.astype(BF16), vc_ref[:, sl])
            o_ref[:, sl] = o / l
            lse_tile = jnp.where(lane == h, m + jnp.log(l), lse_tile)
        lse_ref[...] = lse_tile

    def blk(which, prev):
        if prev:
            return pl.BlockSpec((ATTN_BLOCK, HD), lambda r, n: (jnp.maximum(n - 1, 0), r * 9 + 3 * g + which))
        return pl.BlockSpec((ATTN_BLOCK, HD), lambda r, n: (n, r * 9 + 3 * g + which))

    gains = pl.BlockSpec((3, HEAD_DIM), lambda r, n: (0, 0))
    view = qkv.reshape(L, d * W)
    o, lse = pl.pallas_call(
        body, name=name, grid=(d, nb),
        in_specs=[blk(0, False), blk(1, True), blk(1, False), blk(2, True), blk(2, False), gains, gains],
        out_specs=[pl.BlockSpec((ATTN_BLOCK, HD), lambda r, n: (n, r)),
                   pl.BlockSpec((ATTN_BLOCK, 128), lambda r, n: (n, r))],
        out_shape=[jax.ShapeDtypeStruct((L, d * HD), F32), jax.ShapeDtypeStruct((L, d * 128), F32)],
        compiler_params=_params("parallel", "parallel"),
    )(view, view, view, view, view, qg, kg)
    return o.reshape(T, HD), lse.reshape(T, 128)


def _attn_merge(os, lses, name):
    T, HD = os[0].shape
    H = HD // HEAD_DIM
    tm = min(T, 256)

    def body(o0, o1, o2, l0, l1, l2, ob_ref, of_ref, lse_ref):
        ls = [l0[...], l1[...], l2[...]]
        m = jnp.maximum(jnp.maximum(ls[0], ls[1]), ls[2])
        tot = m + jnp.log(jnp.exp(ls[0] - m) + jnp.exp(ls[1] - m) + jnp.exp(ls[2] - m))
        lse_ref[...] = tot
        ws = [jnp.exp(l - tot) for l in ls]
        for h in range(H):
            sl = slice(h * HEAD_DIM, (h + 1) * HEAD_DIM)
            out = (ws[0][:, h:h + 1] * o0[:, sl] + ws[1][:, h:h + 1] * o1[:, sl]) + ws[2][:, h:h + 1] * o2[:, sl]
            of_ref[:, sl] = out
            ob_ref[:, sl] = out.astype(BF16)

    row = pl.BlockSpec((tm, HD), lambda i: (i, 0))
    nar = pl.BlockSpec((tm, 128), lambda i: (i, 0))
    return pl.pallas_call(
        body, name=name, grid=(T // tm,), in_specs=[row, row, row, nar, nar, nar],
        out_specs=[row, row, nar],
        out_shape=[jax.ShapeDtypeStruct((T, HD), BF16), jax.ShapeDtypeStruct((T, HD), F32),
                   jax.ShapeDtypeStruct((T, 128), F32)],
        compiler_params=_params("parallel"),
    )(*os, *lses)


def _attn_delta(out, dout, name):
    T, HD = out.shape
    H = HD // HEAD_DIM
    tm = min(T, 512)

    def body(o_ref, d_ref, dl_ref):
        lane = lax.broadcasted_iota(I32, (tm, 128), 1)
        tile = jnp.zeros((tm, 128), F32)
        for h in range(H):
            sl = slice(h * HEAD_DIM, (h + 1) * HEAD_DIM)
            tile = jnp.where(lane == h, jnp.sum(o_ref[:, sl] * d_ref[:, sl], axis=-1, keepdims=True), tile)
        dl_ref[...] = tile

    row = pl.BlockSpec((tm, HD), lambda i: (i, 0))
    return pl.pallas_call(
        body, name=name, grid=(T // tm,), in_specs=[row, row],
        out_specs=pl.BlockSpec((tm, 128), lambda i: (i, 0)),
        out_shape=jax.ShapeDtypeStruct((T, 128), F32),
        compiler_params=_params("parallel"),
    )(out, dout)


def _attn_bwd(qkv, dout, lse, delta, qg, kg, g, dqkv_prev, name):
    T, W = qkv.shape
    HD = W // 9
    H = HD // HEAD_DIM
    d = ATTN_DILATIONS[g]
    L = T // d
    nb = L // ATTN_BLOCK
    scale = HEAD_DIM ** -0.5

    def body(*refs):
        if dqkv_prev is None:
            (q_ref, kp_ref, kc_ref, vp_ref, vc_ref, do_ref, lse_ref, dl_ref, qg_ref, kg_ref,
             out_ref, gg_ref, dq_a, dk_a, dv_a, dq_b, dk_b, dv_b) = refs
        else:
            (q_ref, kp_ref, kc_ref, vp_ref, vc_ref, do_ref, lse_ref, dl_ref, qg_ref, kg_ref, _,
             out_ref, gg_ref, dq_a, dk_a, dv_a, dq_b, dk_b, dv_b) = refs
        r_, n = pl.program_id(0), pl.program_id(1)
        gq = qg_ref[g:g + 1, :]
        gk = kg_ref[g:g + 1, :]

        @pl.when((r_ == 0) & (n == 0))
        def _():
            gg_ref[...] = jnp.zeros_like(gg_ref)

        @pl.when(n == 0)
        def _():
            dk_a[...] = jnp.zeros_like(dk_a)
            dv_a[...] = jnp.zeros_like(dv_a)

        @pl.when(n < nb)
        def _():
            mask_p, mask_c = _band_masks(n)
            gq_sum = jnp.zeros((1, HEAD_DIM), F32)
            for h in range(H):
                sl = slice(h * HEAD_DIM, (h + 1) * HEAD_DIM)
                qn, qh, rq = _head_norm(q_ref[:, sl], gq)
                kpn = _head_norm(kp_ref[:, sl], gk)[0]
                kcn = _head_norm(kc_ref[:, sl], gk)[0]
                lse_h = lse_ref[:, h:h + 1]
                dl_h = dl_ref[:, h:h + 1]
                pp = jnp.where(mask_p, jnp.exp(_dot_nt(qn, kpn) * scale - lse_h), 0.0)
                pc = jnp.where(mask_c, jnp.exp(_dot_nt(qn, kcn) * scale - lse_h), 0.0)
                do = do_ref[:, sl].astype(BF16)
                dsp = (pp * (_dot_nt(do, vp_ref[:, sl]) - dl_h) * scale).astype(BF16)
                dsc = (pc * (_dot_nt(do, vc_ref[:, sl]) - dl_h) * scale).astype(BF16)
                dqn = _dot_nn(dsp, kpn) + _dot_nn(dsc, kcn)
                dqh = dqn * gq
                dq_b[:, sl] = rq * (dqh - qh * jnp.mean(dqh * qh, axis=-1, keepdims=True))
                gq_sum = gq_sum + jnp.sum(dqn * qh, axis=0, keepdims=True)
                dk_a[:, sl] += _dot_tn(dsp, qn)
                dv_a[:, sl] += _dot_tn(pp.astype(BF16), do)
                dk_b[:, sl] = _dot_tn(dsc, qn)
                dv_b[:, sl] = _dot_tn(pc.astype(BF16), do)
            gg_ref[0:1, :] += gq_sum

        @pl.when(n > 0)
        def _():
            out_ref[:, 0:HD] = dq_a[...].astype(BF16)
            gk_sum = jnp.zeros((1, HEAD_DIM), F32)
            for h in range(H):
                sl = slice(h * HEAD_DIM, (h + 1) * HEAD_DIM)
                _, kh, rk = _head_norm(kp_ref[:, sl], gk)
                dkn = dk_a[:, sl]
                dkh = dkn * gk
                dk = rk * (dkh - kh * jnp.mean(dkh * kh, axis=-1, keepdims=True))
                out_ref[:, HD + h * HEAD_DIM:HD + (h + 1) * HEAD_DIM] = dk.astype(BF16)
                gk_sum = gk_sum + jnp.sum(dkn * kh, axis=0, keepdims=True)
            out_ref[:, 2 * HD:3 * HD] = dv_a[...].astype(BF16)
            gg_ref[1:2, :] += gk_sum

        @pl.when(n < nb)
        def _():
            dq_a[...] = dq_b[...]
            dk_a[...] = dk_b[...]
            dv_a[...] = dv_b[...]

    last = nb - 1

    def blk(which, prev):
        if prev:
            return pl.BlockSpec((ATTN_BLOCK, HD), lambda r, n: (jnp.maximum(n - 1, 0), r * 9 + 3 * g + which))
        return pl.BlockSpec((ATTN_BLOCK, HD), lambda r, n: (jnp.minimum(n, last), r * 9 + 3 * g + which))

    gains = pl.BlockSpec((3, HEAD_DIM), lambda r, n: (0, 0))
    nar = pl.BlockSpec((ATTN_BLOCK, 128), lambda r, n: (jnp.minimum(n, last), r))
    view = qkv.reshape(L, d * W)
    in_specs = [blk(0, False), blk(1, True), blk(1, False), blk(2, True), blk(2, False),
                pl.BlockSpec((ATTN_BLOCK, HD), lambda r, n: (jnp.minimum(n, last), r)), nar, nar, gains, gains]
    args = [view, view, view, view, view, dout.reshape(L, d * HD), lse.reshape(L, d * 128),
            delta.reshape(L, d * 128), qg, kg]
    aliases = {}
    if dqkv_prev is not None:
        in_specs.append(pl.BlockSpec(memory_space=pl.ANY))
        args.append(dqkv_prev.reshape(L, d * W))
        aliases = {len(args) - 1: 0}
    acc = pltpu.VMEM((ATTN_BLOCK, HD), F32)
    dqkv, gg = pl.pallas_call(
        body, name=name, grid=(d, nb + 1), in_specs=in_specs,
        out_specs=[pl.BlockSpec((ATTN_BLOCK, 3 * HD), lambda r, n: (jnp.maximum(n - 1, 0), r * 3 + g)),
                   pl.BlockSpec((8, HEAD_DIM), lambda r, n: (0, 0))],
        out_shape=[jax.ShapeDtypeStruct((L, d * W), BF16), jax.ShapeDtypeStruct((8, HEAD_DIM), F32)],
        scratch_shapes=[acc, acc, acc, acc, acc, acc],
        input_output_aliases=aliases,
        compiler_params=_params("arbitrary", "arbitrary"),
    )(*args)
    return dqkv.reshape(T, W), gg


def _adamw(w, grad, m, v, name):
    R, C = w.shape
    tr = _row_tile(R, 512)

    def body(w_ref, g_ref, m_ref, v_ref, d_ref, nm_ref, nv_ref):
        gv = g_ref[...]
        nm = ADAM_B1 * m_ref[...] + (1.0 - ADAM_B1) * gv
        nv = ADAM_B2 * v_ref[...] + (1.0 - ADAM_B2) * (gv * gv)
        m_hat = nm / (1.0 - ADAM_B1 ** ADAM_STEP)
        v_hat = nv / (1.0 - ADAM_B2 ** ADAM_STEP)
        d_ref[...] = -ADAM_LR * (m_hat / (jnp.sqrt(v_hat) + ADAM_EPS) + ADAM_WD * w_ref[...])
        nm_ref[...] = nm
        nv_ref[...] = nv

    blk = pl.BlockSpec((tr, C), lambda i: (i, 0))
    shp = jax.ShapeDtypeStruct((R, C), F32)
    return pl.pallas_call(
        body, name=name, grid=(R // tr,), in_specs=[blk] * 4, out_specs=[blk] * 3, out_shape=[shp] * 3,
        compiler_params=_params("parallel"),
    )(w, grad, m, v)


def _ffn_fwd(x, g, win_t, wout, tag, tgt=None):
    h = _rmsnorm(x, g, f"rmsnorm_{tag}")
    z3, a = _gated_in(h, win_t, None, "swiglu", BF16, f"ffn_in_{tag}")
    saved = (x, h, z3, a)
    if tgt is None:
        return _rows_mm(a, wout, "res", f"ffn_out_{tag}", x=x, scale=0.5), saved
    return _rows_mm(a, wout, "loss", f"ffn_out_loss_{tag}", x=x, scale=0.5, tgt=tgt), saved


def _ffn_bwd(dxo, saved, g, win_t, wout, tag):
    x, h, z3, a = saved
    dz3 = _swiglu_bwd(dxo, wout, z3, 0.5, f"ffn_out_bwd_{tag}")
    d_wout = _mm_tn(a, dxo, 0.5, f"ffn_dwout_{tag}")
    d_win_t = _mm_tn(dz3, h, 1.0, f"ffn_dwin_{tag}")
    dx, dg8 = _rows_mm(dz3, win_t, "rmsbwd", f"ffn_in_bwd_{tag}", x=x, g=g, dxo=dxo)
    return dx, dg8, d_win_t, d_wout


def _local_step(x, tgt, wts):
    norm_g = wts["norm_g"]
    gr = {}

    def gvec(i):
        return norm_g[i:i + 1]

    x1, sv_a0 = _ffn_fwd(x, gvec(0), wts["win_t"][0], wts["wout"][0], "l0a")
    h_c = _rmsnorm(x1, gvec(1), "rmsnorm_conv")
    zc3, u = _gated_in(h_c, wts["pw1_t"], wts["b_pw1"], "glu", F32, "conv_pw1")
    v, s = _dwconv_fwd(u, wts["w_dw"], wts["b_dw"], wts["cg"], "conv_dw")
    x2 = _rows_mm(s, wts["pw2"], "res", "conv_pw2", x=x1, bias=wts["b_pw2"])
    x3, sv_b0 = _ffn_fwd(x2, gvec(2), wts["win_t"][1], wts["wout"][1], "l0b")
    x4, sv_a1 = _ffn_fwd(x3, gvec(3), wts["win_t"][2], wts["wout"][2], "l1a")
    h_t = _rmsnorm(x4, gvec(4), "rmsnorm_attn")
    qkv = _mm_nt(h_t, wts["qkv_t"], BF16, "attn_qkv")
    os, lses = [], []
    for g in range(3):
        o, lse = _attn_fwd(qkv, wts["q_norm"], wts["k_norm"], g, f"attn_fwd_g{g}")
        os.append(o)
        lses.append(lse)
    out_b, out_f, lse_tot = _attn_merge(os, lses, "attn_merge")
    x5 = _rows_mm(out_b, wts["wo"], "res", "attn_wo", x=x4)
    (dy, loss8), sv_b1 = _ffn_fwd(x5, gvec(5), wts["win_t"][3], wts["wout"][3], "l1b", tgt=tgt)
    loss = 0.5 * jnp.sum(loss8) / x.shape[1]

    dg = [None] * 6
    d_win, d_wout = [None] * 4, [None] * 4
    dx, dg[5], d_win[3], d_wout[3] = _ffn_bwd(dy, sv_b1, gvec(5), wts["win_t"][3], wts["wout"][3], "l1b")
    dout = _mm_nt(dx, wts["wo"], F32, "attn_wo_bwd")
    gr["wo"] = _mm_tn(out_b, dx, 1.0, "attn_dwo")
    delta = _attn_delta(out_f, dout, "attn_delta")
    dqkv, ggs = None, []
    for g in range(3):
        dqkv, gg = _attn_bwd(qkv, dout, lse_tot, delta, wts["q_norm"], wts["k_norm"], g, dqkv, f"attn_bwd_g{g}")
        ggs.append(gg)
    gr["qkv_t"] = _mm_tn(dqkv, h_t, 1.0, "attn_dwqkv")
    dx, dg[4] = _rows_mm(dqkv, wts["qkv_t"], "rmsbwd", "attn_qkv_bwd", x=x4, g=gvec(4), dxo=dx)
    dx, dg[3], d_win[2], d_wout[2] = _ffn_bwd(dx, sv_a1, gvec(3), wts["win_t"][2], wts["wout"][2], "l1a")
    dx, dg[2], d_win[1], d_wout[1] = _ffn_bwd(dx, sv_b0, gvec(2), wts["win_t"][1], wts["wout"][1], "l0b")
    ds = _mm_nt(dx, wts["pw2"], F32, "conv_pw2_bwd")
    gr["pw2"] = _mm_tn(s, dx, 1.0, "conv_dwpw2")
    dv, dcg8, dbdw8, dbpw2_8 = _conv_post_bwd(ds, v, wts["cg"], dx, "conv_post_bwd")
    dzc3, d_wdw, db1_16 = _dwconv_bwd(dv, u, wts["w_dw"], zc3, "conv_dw_bwd")
    gr["pw1_t"] = _mm_tn(dzc3, h_c, 1.0, "conv_dwpw1")
    dx, dg[1] = _rows_mm(dzc3, wts["pw1_t"], "rmsbwd", "conv_pw1_bwd", x=x1, g=gvec(1), dxo=dx)
    dx, dg[0], d_win[0], d_wout[0] = _ffn_bwd(dx, sv_a0, gvec(0), wts["win_t"][0], wts["wout"][0], "l0a")
    gr["win_t"], gr["wout"] = d_win, d_wout

    D = x.shape[1]
    small = {
        "norm_g": jnp.stack([p.sum(axis=0) for p in dg]),
        "b_pw1": db1_16.reshape(2, 8, D).sum(axis=1),
        "w_dw": d_wdw[:CONV_WIDTH],
        "b_dw": dbdw8.sum(axis=0, keepdims=True),
        "cg": dcg8.sum(axis=0, keepdims=True),
        "b_pw2": dbpw2_8.sum(axis=0, keepdims=True),
        "q_norm": jnp.stack([gg[0] for gg in ggs]),
        "k_norm": jnp.stack([gg[1] for gg in ggs]),
    }
    return loss, dx, gr, small


ANY = pl.BlockSpec(memory_space=pl.ANY)


def _mesh_pos():
    return lax.axis_index("x"), lax.axis_index("y"), lax.axis_index("c")


def _other_chips(x, y):
    return [(1 - x, y), (x, 1 - y), (1 - x, 1 - y)]


def _all_gather(shards, name):
    n = len(shards)

    def body(*refs):
        ins, outs = refs[:n], refs[n:2 * n]
        send_sems, recv_sems, local_sems = refs[2 * n:]
        x, y, c = _mesh_pos()
        me, sibling = (x, y, c), (x, y, 1 - c)
        chips = _other_chips(x, y)

        def blk(i, px, py, pc):
            return outs[i].at[4 * px + 2 * py + pc]

        def copy(i, k, block, to, src=None):
            return pltpu.make_async_remote_copy(
                src_ref=blk(i, *block) if src is None else src, dst_ref=blk(i, *block),
                send_sem=send_sems.at[i, k], recv_sem=recv_sems.at[i, k], device_id=to, device_id_type=MESH)

        local = [pltpu.make_async_copy(ins[i], blk(i, *me), local_sems.at[i]) for i in range(n)]
        for cp in local:
            cp.start()
        first = []
        for i in range(n):
            first.append(copy(i, 0, me, sibling, src=ins[i]))
            first += [copy(i, 1 + j, me, (*chip, c), src=ins[i]) for j, chip in enumerate(chips)]
        for cp in first:
            cp.start()
        passed = []
        for j, chip in enumerate(chips):
            for i in range(n):
                copy(i, 1 + j, (*chip, c), me).wait_recv()
                fwd = copy(i, 4 + j, (*chip, c), sibling)
                fwd.start()
                passed.append(fwd)
        for i in range(n):
            copy(i, 0, sibling, me).wait_recv()
            for j, chip in enumerate(chips):
                copy(i, 4 + j, (*chip, 1 - c), me).wait_recv()
        for cp in first + passed:
            cp.wait_send()
        for cp in local:
            cp.wait()

    return pl.pallas_call(
        body, name=name, in_specs=[ANY] * n, out_specs=[ANY] * n,
        out_shape=[jax.ShapeDtypeStruct((N_DEV,) + s.shape, s.dtype) for s in shards],
        scratch_shapes=[pltpu.SemaphoreType.DMA((n, 7)), pltpu.SemaphoreType.DMA((n, 7)),
                        pltpu.SemaphoreType.DMA((n,))],
    )(*shards)


def _all_sum(p, name):
    R, C = p.shape

    def body(p_ref, o_ref, buf, send_sems, recv_sems):
        x, y, c = _mesh_pos()
        me = 4 * x + 2 * y + c
        buf[me] = p_ref[...]
        copies = []
        for rel in range(1, N_DEV):
            peer = (1 - x if rel & 4 else x, 1 - y if rel & 2 else y, 1 - c if rel & 1 else c)
            cp = pltpu.make_async_remote_copy(
                src_ref=p_ref, dst_ref=buf.at[me], send_sem=send_sems.at[rel - 1], recv_sem=recv_sems.at[rel - 1],
                device_id=peer, device_id_type=MESH)
            cp.start()
            copies.append(cp)
        for cp in copies:
            cp.wait()
        acc = buf[0]
        for dev in range(1, N_DEV):
            acc = acc + buf[dev]
        o_ref[...] = acc

    vm = pl.BlockSpec(memory_space=pltpu.VMEM)
    return pl.pallas_call(
        body, name=name, in_specs=[vm], out_specs=vm, out_shape=jax.ShapeDtypeStruct((R, C), F32),
        scratch_shapes=[pltpu.VMEM((N_DEV, R, C), F32), pltpu.SemaphoreType.DMA((N_DEV - 1,)),
                        pltpu.SemaphoreType.DMA((N_DEV - 1,))],
    )(p)


def _swap_with_sibling(gs, name):
    n = len(gs)

    def body(*refs):
        ins, outs = refs[:n], refs[n:2 * n]
        send_sems, recv_sems = refs[2 * n:]
        x, y, c = _mesh_pos()
        copies = []
        for i in range(n):
            for k in range(4):
                cp = pltpu.make_async_remote_copy(
                    src_ref=ins[i].at[2 * k + (1 - c)], dst_ref=outs[i].at[k],
                    send_sem=send_sems.at[i, k], recv_sem=recv_sems.at[i, k],
                    device_id=(x, y, 1 - c), device_id_type=MESH)
                cp.start()
                copies.append(cp)
        for cp in copies:
            cp.wait()

    return pl.pallas_call(
        body, name=name, in_specs=[ANY] * n, out_specs=[ANY] * n,
        out_shape=[jax.ShapeDtypeStruct((4,) + g.shape[1:], g.dtype) for g in gs],
        scratch_shapes=[pltpu.SemaphoreType.DMA((n, 4)), pltpu.SemaphoreType.DMA((n, 4))],
    )(*gs)


def _swap_with_chips(ps, name):
    n = len(ps)

    def body(*refs):
        ins, outs = refs[:n], refs[n:2 * n]
        send_sems, recv_sems = refs[2 * n:]
        x, y, c = _mesh_pos()
        copies = []
        for i in range(n):
            for j, (px, py) in enumerate(_other_chips(x, y)):
                cp = pltpu.make_async_remote_copy(
                    src_ref=ins[i].at[2 * px + py], dst_ref=outs[i].at[j],
                    send_sem=send_sems.at[i, j], recv_sem=recv_sems.at[i, j],
                    device_id=(px, py, c), device_id_type=MESH)
                cp.start()
                copies.append(cp)
        for cp in copies:
            cp.wait()

    return pl.pallas_call(
        body, name=name, in_specs=[ANY] * n, out_specs=[ANY] * n,
        out_shape=[jax.ShapeDtypeStruct((3,) + p.shape[1:], p.dtype) for p in ps],
        scratch_shapes=[pltpu.SemaphoreType.DMA((n, 3)), pltpu.SemaphoreType.DMA((n, 3))],
    )(*ps)


def _row_tile(r, pref):
    if r <= pref:
        return r
    for t in range(pref - pref % 16, 0, -16):
        if r % t == 0:
            return t
    return r


def _add_sibling(g, r1, pos, name):
    _, r, D = g.shape
    tr = _row_tile(r, 512)

    def body(pos_ref, g_ref, r_ref, o_ref):
        o_ref[...] = (g_ref[...].astype(F32) + r_ref[...].astype(F32)).astype(BF16)

    return pl.pallas_call(
        body, name=name,
        grid_spec=pltpu.PrefetchScalarGridSpec(
            num_scalar_prefetch=1, grid=(4, r // tr),
            in_specs=[pl.BlockSpec((None, None, tr, D), lambda k, j, pos: (k, pos[0], j, 0)),
                      pl.BlockSpec((None, tr, D), lambda k, j, pos: (k, j, 0))],
            out_specs=pl.BlockSpec((None, tr, D), lambda k, j, pos: (k, j, 0))),
        out_shape=jax.ShapeDtypeStruct((4, r, D), BF16),
        compiler_params=_params("parallel", "parallel"),
    )(pos, g.reshape(4, 2, r, D), r1)


def _add_chips(p, r2, pos, name):
    _, r, D = p.shape
    tr = _row_tile(r, 512)

    def body(pos_ref, p_ref, r_ref, o_ref):
        acc = p_ref[...].astype(F32)
        for j in range(3):
            acc = acc + r_ref[j].astype(F32)
        o_ref[...] = acc

    return pl.pallas_call(
        body, name=name,
        grid_spec=pltpu.PrefetchScalarGridSpec(
            num_scalar_prefetch=1, grid=(r // tr,),
            in_specs=[pl.BlockSpec((None, tr, D), lambda j, pos: (pos[1], j, 0)),
                      pl.BlockSpec((3, tr, D), lambda j, pos: (0, j, 0))],
            out_specs=pl.BlockSpec((tr, D), lambda j, pos: (j, 0))),
        out_shape=jax.ShapeDtypeStruct((r, D), F32),
        compiler_params=_params("parallel"),
    )(pos, p, r2)


def _reduce_scatter(gs, pos, tag):
    g3 = [g.reshape(N_DEV, g.shape[0] // N_DEV, g.shape[1]) for g in gs]
    r1 = _swap_with_sibling(g3, f"rs_sibling_{tag}")
    ps = [_add_sibling(g, r, pos, f"rs_add_sibling_{tag}{i}") for i, (g, r) in enumerate(zip(g3, r1))]
    r2 = _swap_with_chips(ps, f"rs_chips_{tag}")
    return [_add_chips(p, r, pos, f"rs_add_chips_{tag}{i}") for i, (p, r) in enumerate(zip(ps, r2))]


def _place_cols(local, me, width):
    R, w = local.shape
    return lax.dynamic_update_slice(jnp.zeros((R, width), F32), local, (0, me * w))


def _pad_rows(a, rows):
    return jnp.pad(a, ((0, rows - a.shape[0]), (0, 0)))


def kernel(x, norm_g, ffn_w_in, ffn_w_out, conv_w_pw1, conv_b_pw1, conv_w_dw, conv_b_dw, conv_norm_g, conv_w_pw2, conv_b_pw2, attn_w_qkv, attn_q_norm, attn_k_norm, attn_w_o, loss_target, m_norm_g, m_ffn_w_in, m_ffn_w_out, m_conv_w_pw1, m_conv_b_pw1, m_conv_w_dw, m_conv_b_dw, m_conv_norm_g, m_conv_w_pw2, m_conv_b_pw2, m_attn_w_qkv, m_attn_q_norm, m_attn_k_norm, m_attn_w_o, v_norm_g, v_ffn_w_in, v_ffn_w_out, v_conv_w_pw1, v_conv_b_pw1, v_conv_w_dw, v_conv_b_dw, v_conv_norm_g, v_conv_w_pw2, v_conv_b_pw2, v_attn_w_qkv, v_attn_q_norm, v_attn_k_norm, v_attn_w_o):
    T, D = x.shape[1], x.shape[2]
    xi, yi, ci = _mesh_pos()
    me = 4 * xi + 2 * yi + ci
    pos = jnp.stack([ci, 2 * xi + yi]).astype(I32)
    lf = [(l, f) for l in range(2) for f in range(2)]

    shards = ([ffn_w_in[l, f].T.astype(BF16) for l, f in lf] + [ffn_w_out[l, f].astype(BF16) for l, f in lf]
              + [conv_w_pw1[0].T.astype(BF16), conv_w_pw2[0].astype(BF16),
                 attn_w_qkv[0].T.astype(BF16), attn_w_o[0].astype(BF16)])
    full = [w.reshape(-1, D) for w in _all_gather(shards, "gather_weights")]
    w_cols = norm_g.shape[2]
    vec = _all_sum(_pad_rows(jnp.concatenate([_place_cols(norm_g.reshape(6, w_cols), me, D),
                                              _place_cols(conv_w_dw[0], me, D)]), 40), "gather_vectors")
    wts = {
        "win_t": full[0:4], "wout": full[4:8], "pw1_t": full[8], "pw2": full[9], "qkv_t": full[10], "wo": full[11],
        "norm_g": vec[0:6], "w_dw": vec[6:6 + CONV_WIDTH],
        "b_pw1": conv_b_pw1.reshape(2, D), "b_dw": conv_b_dw, "cg": conv_norm_g, "b_pw2": conv_b_pw2,
        "q_norm": attn_q_norm[0], "k_norm": attn_k_norm[0],
    }

    loss_local, dx, gr, small = _local_step(x[0], loss_target[0], wts)
    loss = lax.psum(loss_local, ("x", "y", "c"))

    mats = _reduce_scatter(gr["win_t"] + gr["wout"] + [gr["pw1_t"], gr["pw2"], gr["qkv_t"], gr["wo"]], pos, "w")
    hd3 = 3 * HEAD_DIM
    packed = jnp.concatenate([
        small["norm_g"], small["b_pw1"], small["w_dw"], small["b_dw"], small["cg"], small["b_pw2"],
        jnp.pad(small["q_norm"].reshape(1, hd3), ((0, 0), (0, D - hd3))),
        jnp.pad(small["k_norm"].reshape(1, hd3), ((0, 0), (0, D - hd3)))])
    red = _all_sum(_pad_rows(packed, 48), "reduce_vectors")
    col0 = me * w_cols
    grads = {
        "norm_g": lax.dynamic_slice(red[0:6], (0, col0), (6, w_cols)),
        "ffn_w_in": jnp.concatenate([mats[i].T for i in range(4)]),
        "ffn_w_out": jnp.concatenate(mats[4:8]),
        "conv_w_pw1": mats[8].T,
        "conv_b_pw1": red[6:8].reshape(1, 2 * D),
        "conv_w_dw": lax.dynamic_slice(red[8:8 + CONV_WIDTH], (0, col0), (CONV_WIDTH, w_cols)),
        "conv_b_dw": red[39:40], "conv_norm_g": red[40:41], "conv_w_pw2": mats[9], "conv_b_pw2": red[41:42],
        "attn_w_qkv": mats[10].T,
        "attn_q_norm": red[42, :hd3].reshape(3, HEAD_DIM), "attn_k_norm": red[43, :hd3].reshape(3, HEAD_DIM),
        "attn_w_o": mats[11],
    }

    names = ["norm_g", "ffn_w_in", "ffn_w_out", "conv_w_pw1", "conv_b_pw1", "conv_w_dw", "conv_b_dw",
             "conv_norm_g", "conv_w_pw2", "conv_b_pw2", "attn_w_qkv", "attn_q_norm", "attn_k_norm", "attn_w_o"]
    ws = [norm_g, ffn_w_in, ffn_w_out, conv_w_pw1, conv_b_pw1, conv_w_dw, conv_b_dw, conv_norm_g, conv_w_pw2,
          conv_b_pw2, attn_w_qkv, attn_q_norm, attn_k_norm, attn_w_o]
    ms = [m_norm_g, m_ffn_w_in, m_ffn_w_out, m_conv_w_pw1, m_conv_b_pw1, m_conv_w_dw, m_conv_b_dw, m_conv_norm_g,
          m_conv_w_pw2, m_conv_b_pw2, m_attn_w_qkv, m_attn_q_norm, m_attn_k_norm, m_attn_w_o]
    vs = [v_norm_g, v_ffn_w_in, v_ffn_w_out, v_conv_w_pw1, v_conv_b_pw1, v_conv_w_dw, v_conv_b_dw, v_conv_norm_g,
          v_conv_w_pw2, v_conv_b_pw2, v_attn_w_qkv, v_attn_q_norm, v_attn_k_norm, v_attn_w_o]
    g_out, d_out, m_out, v_out = [], [], [], []
    for name, w, m, v in zip(names, ws, ms, vs):
        shape2 = (-1, w.shape[-1])
        g2 = grads[name].reshape(shape2)
        delta, new_m, new_v = _adamw(w.reshape(shape2), g2, m.reshape(shape2), v.reshape(shape2), f"adamw_{name}")
        g_out.append(g2.reshape(w.shape))
        d_out.append(delta.reshape(w.shape))
        m_out.append(new_m.reshape(w.shape))
        v_out.append(new_v.reshape(w.shape))
    return (loss, dx.reshape(x.shape), *g_out, *d_out, *m_out, *v_out)
```

```python
import functools

import jax
import jax.numpy as jnp
from jax import lax
from jax.experimental import pallas as pl
from jax.experimental.pallas import tpu as pltpu

F32 = jnp.float32
BF16 = jnp.bfloat16
I32 = jnp.int32

NORM_EPS = 1e-6
LANES = 128
HEAD_DIM = 128
ATTN_BLOCK = 128
ATTN_DILATIONS = (1, 4, 16)
CONV_WIDTH = 31
CONV_HALO = 32
STREAM_TILE = 512
ADAM_LR, ADAM_B1, ADAM_B2, ADAM_EPS, ADAM_WD, ADAM_STEP = 0.001, 0.9, 0.999, 1e-08, 0.01, 10
N_DEV = 8
NEG_BIG = -1e30
V7X_VMEM_BYTES = 64 * 1024 * 1024
VMEM_LIMIT = V7X_VMEM_BYTES - 8 * 1024 * 1024
MESH = pl.DeviceIdType.MESH


def _params(*sem):
    return pltpu.CompilerParams(dimension_semantics=sem or None, vmem_limit_bytes=VMEM_LIMIT)


def _dot_nn(a, b):
    return lax.dot_general(a, b, (((1,), (0,)), ((), ())), preferred_element_type=F32)


def _dot_nt(a, b):
    return lax.dot_general(a, b, (((1,), (1,)), ((), ())), preferred_element_type=F32)


def _dot_tn(a, b):
    return lax.dot_general(a, b, (((0,), (0,)), ((), ())), preferred_element_type=F32)


def _sigmoid(x):
    return 1.0 / (1.0 + jnp.exp(-x))


def _tile(n, pref):
    if n <= pref:
        return n
    for t in range(pref - pref % 128, 0, -128):
        if n % t == 0:
            return t
    return n


def _to_lane_chunks(dst, val):
    for c in range(dst.shape[0]):
        dst[c] = val[:, c * LANES:(c + 1) * LANES]


def _rows8(v):
    tm, c = v.shape
    return v.reshape(tm // 8, 8, c).sum(axis=0)


def _rmsnorm(x, g, name):
    T, D = x.shape
    tm = min(T, 512)

    def body(x_ref, g_ref, h_ref):
        xv = x_ref[...]
        r = lax.rsqrt(jnp.mean(xv * xv, axis=-1, keepdims=True) + NORM_EPS)
        h_ref[...] = (xv * r * g_ref[...]).astype(BF16)

    return pl.pallas_call(
        body, name=name, grid=(T // tm,),
        in_specs=[pl.BlockSpec((tm, D), lambda i: (i, 0)), pl.BlockSpec((1, D), lambda i: (0, 0))],
        out_specs=pl.BlockSpec((tm, D), lambda i: (i, 0)),
        out_shape=jax.ShapeDtypeStruct((T, D), BF16),
        compiler_params=_params("parallel"),
    )(x, g)


def _rmsnorm_streams(x, g, name):
    T, D = x.shape
    tm = min(T, STREAM_TILE)
    dils = [d for d in ATTN_DILATIONS if d > 1]

    def body(x_ref, g_ref, h_ref, *rest):
        outs, hs = rest[:-1], rest[-1]
        xv = x_ref[...]
        r = lax.rsqrt(jnp.mean(xv * xv, axis=-1, keepdims=True) + NORM_EPS)
        hf = xv * r * g_ref[...]
        h_ref[...] = hf.astype(BF16)
        _to_lane_chunks(hs, hf)
        for d, o_ref in zip(dils, outs):
            for s in range(d):
                for c in range(D // LANES):
                    o_ref[s, :, c * LANES:(c + 1) * LANES] = hs.at[c][pl.ds(s, tm // d, stride=d), :].astype(BF16)

    res = pl.pallas_call(
        body, name=name, grid=(T // tm,),
        in_specs=[pl.BlockSpec((tm, D), lambda i: (i, 0)), pl.BlockSpec((1, D), lambda i: (0, 0))],
        out_specs=[pl.BlockSpec((tm, D), lambda i: (i, 0))]
        + [pl.BlockSpec((d, tm // d, D), lambda i: (0, i, 0)) for d in dils],
        out_shape=[jax.ShapeDtypeStruct((T, D), BF16)] + [jax.ShapeDtypeStruct((d, T // d, D), BF16) for d in dils],
        scratch_shapes=[pltpu.VMEM((D // LANES, tm, LANES), F32)],
        compiler_params=_params("parallel"),
    )(x, g)
    return [res[0]] + [a.reshape(T, D) for a in res[1:]]


def _gated_in(h, wt, bias, act, u_dtype, name):
    T, D = h.shape
    F = wt.shape[0] // 2
    tn = _tile(F, 1408)
    tm = min(T, 512)
    nn = F // tn

    def body(*refs):
        if bias is None:
            h_ref, w0_ref, w1_ref, z_ref, u_ref = refs
        else:
            h_ref, w0_ref, w1_ref, b_ref, z_ref, u_ref = refs
        hv = h_ref[...]
        z0 = _dot_nt(hv, w0_ref[...])
        z1 = _dot_nt(hv, w1_ref[...])
        if bias is not None:
            z0 = z0 + b_ref[0:1, :]
            z1 = z1 + b_ref[1:2, :]
        z_ref[0] = z0.astype(BF16)
        z_ref[1] = z1.astype(BF16)
        if act == "swiglu":
            u = z0 * _sigmoid(z0) * z1
        else:
            u = z0 * _sigmoid(z1)
        u_ref[...] = u.astype(u_dtype)

    in_specs = [pl.BlockSpec((tm, D), lambda n, m: (m, 0)),
                pl.BlockSpec((tn, D), lambda n, m: (n, 0)),
                pl.BlockSpec((tn, D), lambda n, m: (n + nn, 0))]
    args = [h, wt, wt]
    if bias is not None:
        in_specs.append(pl.BlockSpec((2, tn), lambda n, m: (0, n)))
        args.append(bias)
    return pl.pallas_call(
        body, name=name, grid=(nn, T // tm), in_specs=in_specs,
        out_specs=[pl.BlockSpec((2, tm, tn), lambda n, m: (0, m, n)), pl.BlockSpec((tm, tn), lambda n, m: (m, n))],
        out_shape=[jax.ShapeDtypeStruct((2, T, F), BF16), jax.ShapeDtypeStruct((T, F), u_dtype)],
        compiler_params=_params("parallel", "parallel"),
    )(*args)


def _swiglu_bwd(dxo, wout, z3, scale, name):
    T, D = dxo.shape
    F = wout.shape[0]
    tn = _tile(F, 1408)
    tm = min(T, 512)

    def body(d_ref, w_ref, z_ref, dz_ref):
        dy = (d_ref[...] * scale).astype(BF16)
        da = _dot_nt(dy, w_ref[...])
        gate = z_ref[0].astype(F32)
        up = z_ref[1].astype(F32)
        sg = _sigmoid(gate)
        dz_ref[0] = (da * up * (sg * (1.0 + gate * (1.0 - sg)))).astype(BF16)
        dz_ref[1] = (da * gate * sg).astype(BF16)

    return pl.pallas_call(
        body, name=name, grid=(F // tn, T // tm),
        in_specs=[pl.BlockSpec((tm, D), lambda n, m: (m, 0)),
                  pl.BlockSpec((tn, D), lambda n, m: (n, 0)),
                  pl.BlockSpec((2, tm, tn), lambda n, m: (0, m, n))],
        out_specs=pl.BlockSpec((2, tm, tn), lambda n, m: (0, m, n)),
        out_shape=jax.ShapeDtypeStruct((2, T, F), BF16),
        compiler_params=_params("parallel", "parallel"),
    )(dxo, wout, z3)


def _mm_nt(a, wt, out_dtype, name, *, w_row0=0, n_rows=None, streams=1):
    T, K = a.shape
    N = wt.shape[0] if n_rows is None else n_rows
    L = T // streams
    tn = _tile(N, 1152)
    tm = min(L, 512)
    mps = L // tm
    npn = N // tn
    assert w_row0 % tn == 0
    wb0 = w_row0 // tn

    def body(a_ref, w_ref, o_ref):
        o_ref[...] = _dot_nt(a_ref[...].astype(BF16), w_ref[...]).astype(out_dtype)

    return pl.pallas_call(
        body, name=name, grid=(npn, T // tm),
        in_specs=[pl.BlockSpec((tm, K), lambda n, m: (m, 0)), pl.BlockSpec((tn, K), lambda n, m: (wb0 + n, 0))],
        out_specs=pl.BlockSpec((tm, tn), lambda n, m: (m % mps, (m // mps) * npn + n)),
        out_shape=jax.ShapeDtypeStruct((L, streams * N), out_dtype),
        compiler_params=_params("parallel", "parallel"),
    )(a, wt)


def _mm_tn(a, b, b_scale, name, *, streams=1):
    if a.ndim == 3:
        J, T, F = a.shape
    elif streams > 1:
        J, T, F = 1, a.shape[0] * streams, a.shape[1] // streams
    else:
        (T, F), J = a.shape, 1
    N = b.shape[1]
    tmm = _tile(F, 1408)
    tk = min(T // streams, 1024)
    npj = F // tmm
    nk = T // tk
    kps = nk // streams

    def body(a_ref, b_ref, o_ref, acc_ref):
        k = pl.program_id(1)
        bv = b_ref[...]
        if b_scale != 1.0:
            bv = bv * b_scale
        part = _dot_tn(a_ref[...].astype(BF16), bv.astype(BF16))

        @pl.when(k == 0)
        def _():
            acc_ref[...] = part

        @pl.when(k > 0)
        def _():
            acc_ref[...] += part

        @pl.when(k == nk - 1)
        def _():
            o_ref[...] = acc_ref[...].astype(BF16)

    if a.ndim == 3:
        a_spec = pl.BlockSpec((None, tk, tmm), lambda i, k: (i // npj, k, i % npj))
    elif streams > 1:
        a_spec = pl.BlockSpec((tk, tmm), lambda i, k: (k % kps, (k // kps) * npj + i))
    else:
        a_spec = pl.BlockSpec((tk, tmm), lambda i, k: (k, i))
    return pl.pallas_call(
        body, name=name, grid=(J * npj, nk),
        in_specs=[a_spec, pl.BlockSpec((tk, N), lambda i, k: (k, 0))],
        out_specs=pl.BlockSpec((tmm, N), lambda i, k: (i, 0)),
        out_shape=jax.ShapeDtypeStruct((J * F, N), BF16),
        scratch_shapes=[pltpu.VMEM((tmm, N), F32)],
        compiler_params=_params("parallel", "arbitrary"),
    )(a, b)


def _rows_mm(a, b, mode, name, *, x=None, scale=1.0, bias=None, tgt=None, g=None, dxo=None,
             b_row0=0, streams=1, extras=()):
    if a.ndim == 3:
        J, T, F = a.shape
    elif streams > 1:
        J, T, F = 1, a.shape[0] * streams, a.shape[1] // streams
    else:
        (T, F), J = a.shape, 1
    D = b.shape[1]
    tk = _tile(F, 1408)
    kpj = F // tk
    nk = J * kpj
    tm = min(T // streams, STREAM_TILE)
    nm = T // tm
    mps = nm // streams
    assert b_row0 % tk == 0
    kb0 = b_row0 // tk
    n_ex = len(extras)

    def body(*refs):
        refs = list(refs)
        a_ref, b_ref = refs[:2]
        rest = refs[2:]
        if n_ex:
            tmp_ref = rest.pop()
        acc_ref = rest.pop()
        ex_refs, rest = rest[:n_ex], rest[n_ex:]
        m, k = pl.program_id(0), pl.program_id(1)
        part = _dot_nn(a_ref[...], b_ref[...])

        @pl.when(k == 0)
        def _():
            acc_ref[...] = part

        @pl.when(k > 0)
        def _():
            acc_ref[...] += part

        @pl.when(k == nk - 1)
        def _():
            acc = acc_ref[...]
            for (d, _), e_ref in zip(extras, ex_refs):
                for s in range(d):
                    for c in range(D // LANES):
                        tmp_ref.at[c][pl.ds(s, tm // d, stride=d), :] = e_ref[s, :, c * LANES:(c + 1) * LANES]
                acc = acc + jnp.concatenate([tmp_ref[c] for c in range(D // LANES)], axis=1)
            if mode == "plain":
                (o_ref,) = rest
                o_ref[...] = acc
            elif mode == "res":
                if bias is None:
                    x_ref, o_ref = rest
                else:
                    x_ref, bias_ref, o_ref = rest
                    acc = acc + bias_ref[...]
                o_ref[...] = x_ref[...] + scale * acc
            elif mode == "loss":
                x_ref, t_ref, o_ref, l_ref = rest
                e = x_ref[...] + scale * acc - t_ref[...]
                o_ref[...] = e / D

                @pl.when(m == 0)
                def _():
                    l_ref[...] = jnp.zeros_like(l_ref)

                l_ref[...] += _rows8(e * e)
            else:
                x_ref, g_ref, d_ref, o_ref, dg_ref = rest
                xv = x_ref[...]
                r = lax.rsqrt(jnp.mean(xv * xv, axis=-1, keepdims=True) + NORM_EPS)
                xh = xv * r
                dxh = acc * g_ref[...]
                o_ref[...] = d_ref[...] + r * (dxh - xh * jnp.mean(dxh * xh, axis=-1, keepdims=True))

                @pl.when(m == 0)
                def _():
                    dg_ref[...] = jnp.zeros_like(dg_ref)

                dg_ref[...] += _rows8(acc * xh)

    if a.ndim == 3:
        a_spec = pl.BlockSpec((None, tm, tk), lambda m, k: (k // kpj, m, k % kpj))
    elif streams > 1:
        a_spec = pl.BlockSpec((tm, tk), lambda m, k: (m % mps, (m // mps) * kpj + k))
    else:
        a_spec = pl.BlockSpec((tm, tk), lambda m, k: (m, k))
    row = pl.BlockSpec((tm, D), lambda m, k: (m, 0))
    vec = pl.BlockSpec((1, D), lambda m, k: (0, 0))
    part8 = pl.BlockSpec((8, D), lambda m, k: (0, 0))
    in_specs = [a_spec, pl.BlockSpec((tk, D), lambda m, k: (kb0 + k, 0))]
    args = [a, b]
    for d, e in extras:
        in_specs.append(pl.BlockSpec((d, tm // d, D), lambda m, k: (0, m, 0)))
        args.append(e.reshape(d, T // d, D))
    full = jax.ShapeDtypeStruct((T, D), F32)
    small = jax.ShapeDtypeStruct((8, D), F32)
    if mode == "plain":
        out_specs, out_shape = row, full
    elif mode == "res":
        in_specs.append(row)
        args.append(x)
        if bias is not None:
            in_specs.append(vec)
            args.append(bias)
        out_specs, out_shape = row, full
    elif mode == "loss":
        in_specs += [row, row]
        args += [x, tgt]
        out_specs, out_shape = [row, part8], [full, small]
    else:
        in_specs += [row, vec, row]
        args += [x, g, dxo]
        out_specs, out_shape = [row, part8], [full, small]
    return pl.pallas_call(
        body, name=name, grid=(nm, nk), in_specs=in_specs, out_specs=out_specs, out_shape=out_shape,
        scratch_shapes=[pltpu.VMEM((tm, D), F32)] + ([pltpu.VMEM((D // LANES, tm, LANES), F32)] if n_ex else []),
        compiler_params=_params("arbitrary", "arbitrary"),
    )(*args)


def _dwconv_fwd(u, w, b_dw, cg, name):
    T, C = u.shape
    tm = min(T, 256)
    hb = tm // CONV_HALO
    pad = CONV_HALO - (CONV_WIDTH - 1)

    def body(u_ref, halo_ref, w_ref, b_ref, g_ref, v_ref, s_ref, buf):
        i = pl.program_id(0)
        buf[pl.ds(0, CONV_HALO), :] = jnp.where(i > 0, halo_ref[...], 0.0)
        buf[pl.ds(CONV_HALO, tm), :] = u_ref[...]
        acc = jnp.zeros((tm, C), F32) + b_ref[...]
        for k in range(CONV_WIDTH):
            acc = acc + w_ref[k:k + 1, :] * buf[pl.ds(pad + k, tm), :]
        v_ref[...] = acc
        r = lax.rsqrt(jnp.mean(acc * acc, axis=-1, keepdims=True) + NORM_EPS)
        n = acc * r * g_ref[...]
        s_ref[...] = (n * _sigmoid(n)).astype(BF16)

    row = pl.BlockSpec((tm, C), lambda i: (i, 0))
    vec = pl.BlockSpec((1, C), lambda i: (0, 0))
    return pl.pallas_call(
        body, name=name, grid=(T // tm,),
        in_specs=[row, pl.BlockSpec((CONV_HALO, C), lambda i: (jnp.maximum(i * hb - 1, 0), 0)),
                  pl.BlockSpec((CONV_WIDTH, C), lambda i: (0, 0)), vec, vec],
        out_specs=[row, row],
        out_shape=[jax.ShapeDtypeStruct((T, C), F32), jax.ShapeDtypeStruct((T, C), BF16)],
        scratch_shapes=[pltpu.VMEM((CONV_HALO + tm, C), F32)],
        compiler_params=_params("parallel"),
    )(u, u, w, b_dw, cg)


def _conv_post_bwd(ds, v, cg, dxo, name):
    T, C = v.shape
    tm = min(T, 512)

    def body(ds_ref, v_ref, g_ref, d_ref, dv_ref, dcg_ref, dbdw_ref, dbpw2_ref):
        @pl.when(pl.program_id(0) == 0)
        def _():
            dcg_ref[...] = jnp.zeros_like(dcg_ref)
            dbdw_ref[...] = jnp.zeros_like(dbdw_ref)
            dbpw2_ref[...] = jnp.zeros_like(dbpw2_ref)

        vv = v_ref[...]
        r = lax.rsqrt(jnp.mean(vv * vv, axis=-1, keepdims=True) + NORM_EPS)
        vh = vv * r
        n = vh * g_ref[...]
        sg = _sigmoid(n)
        dn = ds_ref[...] * (sg * (1.0 + n * (1.0 - sg)))
        dvh = dn * g_ref[...]
        dv = r * (dvh - vh * jnp.mean(dvh * vh, axis=-1, keepdims=True))
        dv_ref[...] = dv
        dcg_ref[...] += _rows8(dn * vh)
        dbdw_ref[...] += _rows8(dv)
        dbpw2_ref[...] += _rows8(d_ref[...])

    row = pl.BlockSpec((tm, C), lambda i: (i, 0))
    part8 = pl.BlockSpec((8, C), lambda i: (0, 0))
    small = jax.ShapeDtypeStruct((8, C), F32)
    return pl.pallas_call(
        body, name=name, grid=(T // tm,),
        in_specs=[row, row, pl.BlockSpec((1, C), lambda i: (0, 0)), row],
        out_specs=[row, part8, part8, part8],
        out_shape=[jax.ShapeDtypeStruct((T, C), F32), small, small, small],
        compiler_params=_params("arbitrary"),
    )(ds, v, cg, dxo)


def _dwconv_bwd(dv, u, w, z3, name):
    T, C = u.shape
    tm = min(T, 256)
    hb = tm // CONV_HALO
    nt = T // tm
    pad = CONV_HALO - (CONV_WIDTH - 1)

    def body(dv_ref, dvn_ref, u_ref, up_ref, w_ref, z_ref, dz_ref, dw_ref, db_ref, dvbuf, ubuf, dwacc):
        i = pl.program_id(0)

        @pl.when(i == 0)
        def _():
            dwacc[...] = jnp.zeros_like(dwacc)
            db_ref[...] = jnp.zeros_like(db_ref)

        dvv = dv_ref[...]
        dvbuf[pl.ds(0, tm), :] = dvv
        dvbuf[pl.ds(tm, CONV_HALO), :] = jnp.where(i < nt - 1, dvn_ref[...], 0.0)
        ubuf[pl.ds(0, CONV_HALO), :] = jnp.where(i > 0, up_ref[...], 0.0)
        ubuf[pl.ds(CONV_HALO, tm), :] = u_ref[...]
        du = jnp.zeros((tm, C), F32)
        for k in range(CONV_WIDTH):
            du = du + w_ref[k:k + 1, :] * dvbuf[pl.ds(CONV_WIDTH - 1 - k, tm), :]
            dwacc[k] += _rows8(dvv * ubuf[pl.ds(pad + k, tm), :])
        a = z_ref[0].astype(F32)
        gate = z_ref[1].astype(F32)
        sg = _sigmoid(gate)
        da = du * sg
        dgate = du * a * sg * (1.0 - sg)
        dz_ref[0] = da.astype(BF16)
        dz_ref[1] = dgate.astype(BF16)
        db_ref[0:8, :] += _rows8(da)
        db_ref[8:16, :] += _rows8(dgate)

        @pl.when(i == nt - 1)
        def _():
            dw_ref[...] = jnp.zeros_like(dw_ref)
            for k in range(CONV_WIDTH):
                dw_ref[k:k + 1, :] = jnp.sum(dwacc[k], axis=0, keepdims=True)

    row = pl.BlockSpec((tm, C), lambda i: (i, 0))
    last_halo = T // CONV_HALO - 1
    return pl.pallas_call(
        body, name=name, grid=(nt,),
        in_specs=[row, pl.BlockSpec((CONV_HALO, C), lambda i: (jnp.minimum((i + 1) * hb, last_halo), 0)),
                  row, pl.BlockSpec((CONV_HALO, C), lambda i: (jnp.maximum(i * hb - 1, 0), 0)),
                  pl.BlockSpec((CONV_WIDTH, C), lambda i: (0, 0)),
                  pl.BlockSpec((2, tm, C), lambda i: (0, i, 0))],
        out_specs=[pl.BlockSpec((2, tm, C), lambda i: (0, i, 0)),
                   pl.BlockSpec((CONV_HALO, C), lambda i: (0, 0)),
                   pl.BlockSpec((16, C), lambda i: (0, 0))],
        out_shape=[jax.ShapeDtypeStruct((2, T, C), BF16), jax.ShapeDtypeStruct((CONV_HALO, C), F32),
                   jax.ShapeDtypeStruct((16, C), F32)],
        scratch_shapes=[pltpu.VMEM((tm + CONV_HALO, C), F32), pltpu.VMEM((CONV_HALO + tm, C), F32),
                        pltpu.VMEM((CONV_WIDTH, 8, C), F32)],
        compiler_params=_params("arbitrary"),
    )(dv, dv, u, u, w, z3)


def _head_norm(raw, gain):
    xv = raw.astype(F32)
    r = lax.rsqrt(jnp.mean(xv * xv, axis=-1, keepdims=True) + NORM_EPS)
    xh = xv * r
    return (xh * gain).astype(BF16), xh, r


def _band_masks(n):
    row = lax.broadcasted_iota(I32, (ATTN_BLOCK, ATTN_BLOCK), 0)
    col = lax.broadcasted_iota(I32, (ATTN_BLOCK, ATTN_BLOCK), 1)
    return (col >= row) & (n > 0), col <= row


def _attn_fwd(qkv, qg, kg, g, name):
    d = ATTN_DILATIONS[g]
    L = qkv.shape[0]
    HD = qkv.shape[1] // (3 * d)
    H = HD // HEAD_DIM
    nb = L // ATTN_BLOCK
    scale = HEAD_DIM ** -0.5

    def body(q_ref, kp_ref, kc_ref, vp_ref, vc_ref, qg_ref, kg_ref, o_ref, lse_ref):
        mask_p, mask_c = _band_masks(pl.program_id(1))
        lane = lax.broadcasted_iota(I32, (ATTN_BLOCK, 128), 1)
        gq = qg_ref[g:g + 1, :]
        gk = kg_ref[g:g + 1, :]
        lse_tile = jnp.zeros((ATTN_BLOCK, 128), F32)
        for h in range(H):
            sl = slice(h * HEAD_DIM, (h + 1) * HEAD_DIM)
            q = _head_norm(q_ref[:, sl], gq)[0]
            kp = _head_norm(kp_ref[:, sl], gk)[0]
            kc = _head_norm(kc_ref[:, sl], gk)[0]
            sp = jnp.where(mask_p, _dot_nt(q, kp) * scale, NEG_BIG)
            sc = jnp.where(mask_c, _dot_nt(q, kc) * scale, NEG_BIG)
            m = jnp.maximum(jnp.max(sp, axis=-1, keepdims=True), jnp.max(sc, axis=-1, keepdims=True))
            pp = jnp.exp(sp - m)
            pc = jnp.exp(sc - m)
            l = jnp.sum(pp, axis=-1, keepdims=True) + jnp.sum(pc, axis=-1, keepdims=True)
            o = _dot_nn(pp.astype(BF16), vp_ref[:, sl]) + _dot_nn(pc.astype(BF16), vc_ref[:, sl])
            o_ref[:, sl] = o / l
            lse_tile = jnp.where(lane == h, m + jnp.log(l), lse_tile)
        lse_ref[...] = lse_tile

    def blk(which, prev):
        if prev:
            return pl.BlockSpec((ATTN_BLOCK, HD), lambda r, n: (jnp.maximum(n - 1, 0), r * 3 + which))
        return pl.BlockSpec((ATTN_BLOCK, HD), lambda r, n: (n, r * 3 + which))

    gains = pl.BlockSpec((3, HEAD_DIM), lambda r, n: (0, 0))
    return pl.pallas_call(
        body, name=name, grid=(d, nb),
        in_specs=[blk(0, False), blk(1, True), blk(1, False), blk(2, True), blk(2, False), gains, gains],
        out_specs=[pl.BlockSpec((ATTN_BLOCK, HD), lambda r, n: (n, r)),
                   pl.BlockSpec((ATTN_BLOCK, 128), lambda r, n: (n, r))],
        out_shape=[jax.ShapeDtypeStruct((L, d * HD), F32), jax.ShapeDtypeStruct((L, d * 128), F32)],
        compiler_params=_params("parallel", "parallel"),
    )(qkv, qkv, qkv, qkv, qkv, qg, kg)


def _attn_merge(os, lses, name):
    T = os[0].shape[0]
    HD = os[0].shape[1]
    H = HD // HEAD_DIM
    tm = min(T, STREAM_TILE)
    dils = ATTN_DILATIONS[1:]

    def body(o0, o1, o2, l0, l1, l2, ob_ref, of_ref, lse_ref, n1, n2, m1, m2):
        for d, src, dst, w in ((dils[0], o1, n1, HD), (dils[1], o2, n2, HD), (dils[0], l1, m1, 128), (dils[1], l2, m2, 128)):
            for s in range(d):
                for c in range(w // LANES):
                    dst.at[c][pl.ds(s, tm // d, stride=d), :] = src[:, s * w + c * LANES:s * w + (c + 1) * LANES]
        ls = [l0[...], m1[0], m2[0]]
        m = jnp.maximum(jnp.maximum(ls[0], ls[1]), ls[2])
        tot = m + jnp.log(jnp.exp(ls[0] - m) + jnp.exp(ls[1] - m) + jnp.exp(ls[2] - m))
        lse_ref[...] = tot
        ws = [jnp.exp(l - tot) for l in ls]
        for h in range(H):
            sl = slice(h * HEAD_DIM, (h + 1) * HEAD_DIM)
            out = (ws[0][:, h:h + 1] * o0[:, sl] + ws[1][:, h:h + 1] * n1[h]) + ws[2][:, h:h + 1] * n2[h]
            of_ref[:, sl] = out
            ob_ref[:, sl] = out.astype(BF16)

    def wide(d, w):
        return pl.BlockSpec((tm // d, d * w), lambda i: (i, 0))

    row = pl.BlockSpec((tm, HD), lambda i: (i, 0))
    nar = pl.BlockSpec((tm, 128), lambda i: (i, 0))
    return pl.pallas_call(
        body, name=name, grid=(T // tm,),
        in_specs=[row, wide(dils[0], HD), wide(dils[1], HD), nar, wide(dils[0], 128), wide(dils[1], 128)],
        out_specs=[row, row, nar],
        out_shape=[jax.ShapeDtypeStruct((T, HD), BF16), jax.ShapeDtypeStruct((T, HD), F32),
                   jax.ShapeDtypeStruct((T, 128), F32)],
        scratch_shapes=[pltpu.VMEM((H, tm, HEAD_DIM), F32), pltpu.VMEM((H, tm, HEAD_DIM), F32),
                        pltpu.VMEM((1, tm, 128), F32), pltpu.VMEM((1, tm, 128), F32)],
        compiler_params=_params("parallel"),
    )(*os, *lses)


def _attn_delta(out, dout, lse, name):
    T, HD = out.shape
    H = HD // HEAD_DIM
    tm = min(T, STREAM_TILE)
    dils = ATTN_DILATIONS[1:]

    def body(o_ref, d_ref, l_ref, dl_ref, *rest):
        dch = rest[-1]
        lane = lax.broadcasted_iota(I32, (tm, 128), 1)
        tile = jnp.zeros((tm, 128), F32)
        for h in range(H):
            sl = slice(h * HEAD_DIM, (h + 1) * HEAD_DIM)
            dch[h] = d_ref[:, sl]
            tile = jnp.where(lane == h, jnp.sum(o_ref[:, sl] * d_ref[:, sl], axis=-1, keepdims=True), tile)
        dl_ref[...] = tile
        for i, d in enumerate(dils):
            dw_ref, lw_ref, dlw_ref = rest[3 * i:3 * i + 3]
            for s in range(d):
                rows = pl.ds(s, tm // d, stride=d)
                for h in range(H):
                    dw_ref[:, s * HD + h * HEAD_DIM:s * HD + (h + 1) * HEAD_DIM] = dch.at[h][rows, :].astype(BF16)
                lw_ref[:, s * 128:(s + 1) * 128] = l_ref[rows, :]
                dlw_ref[:, s * 128:(s + 1) * 128] = dl_ref[rows, :]

    row = pl.BlockSpec((tm, HD), lambda i: (i, 0))
    nar = pl.BlockSpec((tm, 128), lambda i: (i, 0))
    out_specs, out_shape = [nar], [jax.ShapeDtypeStruct((T, 128), F32)]
    for d in dils:
        out_specs += [pl.BlockSpec((tm // d, d * w), lambda i: (i, 0)) for w in (HD, 128, 128)]
        out_shape += [jax.ShapeDtypeStruct((T // d, d * HD), BF16), jax.ShapeDtypeStruct((T // d, d * 128), F32),
                      jax.ShapeDtypeStruct((T // d, d * 128), F32)]
    return pl.pallas_call(
        body, name=name, grid=(T // tm,), in_specs=[row, row, nar], out_specs=out_specs, out_shape=out_shape,
        scratch_shapes=[pltpu.VMEM((H, tm, HEAD_DIM), F32)],
        compiler_params=_params("parallel"),
    )(out, dout, lse)


def _attn_bwd(qkv, dout, lse, delta, qg, kg, g, name):
    d = ATTN_DILATIONS[g]
    L = qkv.shape[0]
    HD = qkv.shape[1] // (3 * d)
    H = HD // HEAD_DIM
    nb = L // ATTN_BLOCK
    scale = HEAD_DIM ** -0.5

    def body(*refs):
        (q_ref, kp_ref, kc_ref, vp_ref, vc_ref, do_ref, lse_ref, dl_ref, qg_ref, kg_ref,
         out_ref, gg_ref, dq_a, dk_a, dv_a, dq_b, dk_b, dv_b) = refs
        r_, n = pl.program_id(0), pl.program_id(1)
        gq = qg_ref[g:g + 1, :]
        gk = kg_ref[g:g + 1, :]

        @pl.when((r_ == 0) & (n == 0))
        def _():
            gg_ref[...] = jnp.zeros_like(gg_ref)

        @pl.when(n == 0)
        def _():
            dk_a[...] = jnp.zeros_like(dk_a)
            dv_a[...] = jnp.zeros_like(dv_a)

        @pl.when(n < nb)
        def _():
            mask_p, mask_c = _band_masks(n)
            gq_sum = jnp.zeros((1, HEAD_DIM), F32)
            for h in range(H):
                sl = slice(h * HEAD_DIM, (h + 1) * HEAD_DIM)
                qn, qh, rq = _head_norm(q_ref[:, sl], gq)
                kpn = _head_norm(kp_ref[:, sl], gk)[0]
                kcn = _head_norm(kc_ref[:, sl], gk)[0]
                lse_h = lse_ref[:, h:h + 1]
                dl_h = dl_ref[:, h:h + 1]
                pp = jnp.where(mask_p, jnp.exp(_dot_nt(qn, kpn) * scale - lse_h), 0.0)
                pc = jnp.where(mask_c, jnp.exp(_dot_nt(qn, kcn) * scale - lse_h), 0.0)
                do = do_ref[:, sl].astype(BF16)
                dsp = (pp * (_dot_nt(do, vp_ref[:, sl]) - dl_h) * scale).astype(BF16)
                dsc = (pc * (_dot_nt(do, vc_ref[:, sl]) - dl_h) * scale).astype(BF16)
                dqn = _dot_nn(dsp, kpn) + _dot_nn(dsc, kcn)
                dqh = dqn * gq
                dq_b[:, sl] = rq * (dqh - qh * jnp.mean(dqh * qh, axis=-1, keepdims=True))
                gq_sum = gq_sum + jnp.sum(dqn * qh, axis=0, keepdims=True)
                dk_a[:, sl] += _dot_tn(dsp, qn)
                dv_a[:, sl] += _dot_tn(pp.astype(BF16), do)
                dk_b[:, sl] = _dot_tn(dsc, qn)
                dv_b[:, sl] = _dot_tn(pc.astype(BF16), do)
            gg_ref[0:1, :] += gq_sum

        @pl.when(n > 0)
        def _():
            out_ref[:, 0:HD] = dq_a[...].astype(BF16)
            gk_sum = jnp.zeros((1, HEAD_DIM), F32)
            for h in range(H):
                sl = slice(h * HEAD_DIM, (h + 1) * HEAD_DIM)
                _, kh, rk = _head_norm(kp_ref[:, sl], gk)
                dkn = dk_a[:, sl]
                dkh = dkn * gk
                dk = rk * (dkh - kh * jnp.mean(dkh * kh, axis=-1, keepdims=True))
                out_ref[:, HD + h * HEAD_DIM:HD + (h + 1) * HEAD_DIM] = dk.astype(BF16)
                gk_sum = gk_sum + jnp.sum(dkn * kh, axis=0, keepdims=True)
            out_ref[:, 2 * HD:3 * HD] = dv_a[...].astype(BF16)
            gg_ref[1:2, :] += gk_sum

        @pl.when(n < nb)
        def _():
            dq_a[...] = dq_b[...]
            dk_a[...] = dk_b[...]
            dv_a[...] = dv_b[...]

    last = nb - 1

    def blk(which, prev):
        if prev:
            return pl.BlockSpec((ATTN_BLOCK, HD), lambda r, n: (jnp.maximum(n - 1, 0), r * 3 + which))
        return pl.BlockSpec((ATTN_BLOCK, HD), lambda r, n: (jnp.minimum(n, last), r * 3 + which))

    gains = pl.BlockSpec((3, HEAD_DIM), lambda r, n: (0, 0))
    nar = pl.BlockSpec((ATTN_BLOCK, 128), lambda r, n: (jnp.minimum(n, last), r))
    acc = pltpu.VMEM((ATTN_BLOCK, HD), F32)
    return pl.pallas_call(
        body, name=name, grid=(d, nb + 1),
        in_specs=[blk(0, False), blk(1, True), blk(1, False), blk(2, True), blk(2, False),
                  pl.BlockSpec((ATTN_BLOCK, HD), lambda r, n: (jnp.minimum(n, last), r)), nar, nar, gains, gains],
        out_specs=[pl.BlockSpec((ATTN_BLOCK, 3 * HD), lambda r, n: (jnp.maximum(n - 1, 0), r)),
                   pl.BlockSpec((8, HEAD_DIM), lambda r, n: (0, 0))],
        out_shape=[jax.ShapeDtypeStruct((L, d * 3 * HD), BF16), jax.ShapeDtypeStruct((8, HEAD_DIM), F32)],
        scratch_shapes=[acc, acc, acc, acc, acc, acc],
        compiler_params=_params("arbitrary", "arbitrary"),
    )(qkv, qkv, qkv, qkv, qkv, dout, lse, delta, qg, kg)


def _adamw(w, grad, m, v, name):
    R, C = w.shape
    tr = _row_tile(R, 512)

    def body(w_ref, g_ref, m_ref, v_ref, d_ref, nm_ref, nv_ref):
        gv = g_ref[...]
        nm = ADAM_B1 * m_ref[...] + (1.0 - ADAM_B1) * gv
        nv = ADAM_B2 * v_ref[...] + (1.0 - ADAM_B2) * (gv * gv)
        m_hat = nm / (1.0 - ADAM_B1 ** ADAM_STEP)
        v_hat = nv / (1.0 - ADAM_B2 ** ADAM_STEP)
        d_ref[...] = -ADAM_LR * (m_hat / (jnp.sqrt(v_hat) + ADAM_EPS) + ADAM_WD * w_ref[...])
        nm_ref[...] = nm
        nv_ref[...] = nv

    blk = pl.BlockSpec((tr, C), lambda i: (i, 0))
    shp = jax.ShapeDtypeStruct((R, C), F32)
    return pl.pallas_call(
        body, name=name, grid=(R // tr,), in_specs=[blk] * 4, out_specs=[blk] * 3, out_shape=[shp] * 3,
        compiler_params=_params("parallel"),
    )(w, grad, m, v)


def _ffn_fwd(x, g, win_t, wout, tag, tgt=None):
    h = _rmsnorm(x, g, f"rmsnorm_{tag}")
    z3, a = _gated_in(h, win_t, None, "swiglu", BF16, f"ffn_in_{tag}")
    saved = (x, h, z3, a)
    if tgt is None:
        return _rows_mm(a, wout, "res", f"ffn_out_{tag}", x=x, scale=0.5), saved
    return _rows_mm(a, wout, "loss", f"ffn_out_loss_{tag}", x=x, scale=0.5, tgt=tgt), saved


def _ffn_bwd(dxo, saved, g, win_t, wout, tag):
    x, h, z3, a = saved
    dz3 = _swiglu_bwd(dxo, wout, z3, 0.5, f"ffn_out_bwd_{tag}")
    d_wout = _mm_tn(a, dxo, 0.5, f"ffn_dwout_{tag}")
    d_win_t = _mm_tn(dz3, h, 1.0, f"ffn_dwin_{tag}")
    dx, dg8 = _rows_mm(dz3, win_t, "rmsbwd", f"ffn_in_bwd_{tag}", x=x, g=g, dxo=dxo)
    return dx, dg8, d_win_t, d_wout


def _local_step(x, tgt, wts):
    norm_g = wts["norm_g"]
    gr = {}

    def gvec(i):
        return norm_g[i:i + 1]

    x1, sv_a0 = _ffn_fwd(x, gvec(0), wts["win_t"][0], wts["wout"][0], "l0a")
    h_c = _rmsnorm(x1, gvec(1), "rmsnorm_conv")
    zc3, u = _gated_in(h_c, wts["pw1_t"], wts["b_pw1"], "glu", F32, "conv_pw1")
    v, s = _dwconv_fwd(u, wts["w_dw"], wts["b_dw"], wts["cg"], "conv_dw")
    x2 = _rows_mm(s, wts["pw2"], "res", "conv_pw2", x=x1, bias=wts["b_pw2"])
    x3, sv_b0 = _ffn_fwd(x2, gvec(2), wts["win_t"][1], wts["wout"][1], "l0b")
    x4, sv_a1 = _ffn_fwd(x3, gvec(3), wts["win_t"][2], wts["wout"][2], "l1a")
    hd3 = wts["qkv_t"].shape[0] // 3
    h_s = _rmsnorm_streams(x4, gvec(4), "rmsnorm_attn")
    qkvs, os, lses = [], [], []
    for g, d in enumerate(ATTN_DILATIONS):
        qkv = _mm_nt(h_s[g], wts["qkv_t"], BF16, f"attn_qkv_g{g}", w_row0=g * hd3, n_rows=hd3, streams=d)
        o, lse = _attn_fwd(qkv, wts["q_norm"], wts["k_norm"], g, f"attn_fwd_g{g}")
        qkvs.append(qkv)
        os.append(o)
        lses.append(lse)
    out_b, out_f, lse_tot = _attn_merge(os, lses, "attn_merge")
    x5 = _rows_mm(out_b, wts["wo"], "res", "attn_wo", x=x4)
    (dy, loss8), sv_b1 = _ffn_fwd(x5, gvec(5), wts["win_t"][3], wts["wout"][3], "l1b", tgt=tgt)
    loss = 0.5 * jnp.sum(loss8) / x.shape[1]

    dg = [None] * 6
    d_win, d_wout = [None] * 4, [None] * 4
    dx, dg[5], d_win[3], d_wout[3] = _ffn_bwd(dy, sv_b1, gvec(5), wts["win_t"][3], wts["wout"][3], "l1b")
    dout = _mm_nt(dx, wts["wo"], F32, "attn_wo_bwd")
    gr["wo"] = _mm_tn(out_b, dx, 1.0, "attn_dwo")
    delta, *wide = _attn_delta(out_f, dout, lse_tot, "attn_delta")
    per_group = [(dout, lse_tot, delta), tuple(wide[0:3]), tuple(wide[3:6])]
    ggs, d_qkv_t, dhs = [], [], []
    for g, d in enumerate(ATTN_DILATIONS):
        dqkv, gg = _attn_bwd(qkvs[g], *per_group[g], wts["q_norm"], wts["k_norm"], g, f"attn_bwd_g{g}")
        ggs.append(gg)
        d_qkv_t.append(_mm_tn(dqkv, h_s[g], 1.0, f"attn_dwqkv_g{g}", streams=d))
        if d > 1:
            dhs.append((d, _rows_mm(dqkv, wts["qkv_t"], "plain", f"attn_qkv_bwd_g{g}", b_row0=g * hd3, streams=d)))
        else:
            dqkv0 = dqkv
    gr["qkv_t"] = jnp.concatenate(d_qkv_t)
    dx, dg[4] = _rows_mm(dqkv0, wts["qkv_t"], "rmsbwd", "attn_qkv_bwd_g0", x=x4, g=gvec(4), dxo=dx, extras=dhs)
    dx, dg[3], d_win[2], d_wout[2] = _ffn_bwd(dx, sv_a1, gvec(3), wts["win_t"][2], wts["wout"][2], "l1a")
    dx, dg[2], d_win[1], d_wout[1] = _ffn_bwd(dx, sv_b0, gvec(2), wts["win_t"][1], wts["wout"][1], "l0b")
    ds = _mm_nt(dx, wts["pw2"], F32, "conv_pw2_bwd")
    gr["pw2"] = _mm_tn(s, dx, 1.0, "conv_dwpw2")
    dv, dcg8, dbdw8, dbpw2_8 = _conv_post_bwd(ds, v, wts["cg"], dx, "conv_post_bwd")
    dzc3, d_wdw, db1_16 = _dwconv_bwd(dv, u, wts["w_dw"], zc3, "conv_dw_bwd")
    gr["pw1_t"] = _mm_tn(dzc3, h_c, 1.0, "conv_dwpw1")
    dx, dg[1] = _rows_mm(dzc3, wts["pw1_t"], "rmsbwd", "conv_pw1_bwd", x=x1, g=gvec(1), dxo=dx)
    dx, dg[0], d_win[0], d_wout[0] = _ffn_bwd(dx, sv_a0, gvec(0), wts["win_t"][0], wts["wout"][0], "l0a")
    gr["win_t"], gr["wout"] = d_win, d_wout

    D = x.shape[1]
    small = {
        "norm_g": jnp.stack([p.sum(axis=0) for p in dg]),
        "b_pw1": db1_16.reshape(2, 8, D).sum(axis=1),
        "w_dw": d_wdw[:CONV_WIDTH],
        "b_dw": dbdw8.sum(axis=0, keepdims=True),
        "cg": dcg8.sum(axis=0, keepdims=True),
        "b_pw2": dbpw2_8.sum(axis=0, keepdims=True),
        "q_norm": jnp.stack([gg[0] for gg in ggs]),
        "k_norm": jnp.stack([gg[1] for gg in ggs]),
    }
    return loss, dx, gr, small


ANY = pl.BlockSpec(memory_space=pl.ANY)


def _mesh_pos():
    return lax.axis_index("x"), lax.axis_index("y"), lax.axis_index("c")


def _other_chips(x, y):
    return [(1 - x, y), (x, 1 - y), (1 - x, 1 - y)]


def _all_gather(shards, name):
    n = len(shards)

    def body(*refs):
        ins, outs = refs[:n], refs[n:2 * n]
        send_sems, recv_sems, local_sems = refs[2 * n:]
        x, y, c = _mesh_pos()
        me, sibling = (x, y, c), (x, y, 1 - c)
        chips = _other_chips(x, y)

        def blk(i, px, py, pc):
            return outs[i].at[4 * px + 2 * py + pc]

        def copy(i, k, block, to, src=None):
            return pltpu.make_async_remote_copy(
                src_ref=blk(i, *block) if src is None else src, dst_ref=blk(i, *block),
                send_sem=send_sems.at[i, k], recv_sem=recv_sems.at[i, k], device_id=to, device_id_type=MESH)

        local = [pltpu.make_async_copy(ins[i], blk(i, *me), local_sems.at[i]) for i in range(n)]
        for cp in local:
            cp.start()
        first = []
        for i in range(n):
            first.append(copy(i, 0, me, sibling, src=ins[i]))
            first += [copy(i, 1 + j, me, (*chip, c), src=ins[i]) for j, chip in enumerate(chips)]
        for cp in first:
            cp.start()
        passed = []
        for j, chip in enumerate(chips):
            for i in range(n):
                copy(i, 1 + j, (*chip, c), me).wait_recv()
                fwd = copy(i, 4 + j, (*chip, c), sibling)
                fwd.start()
                passed.append(fwd)
        for i in range(n):
            copy(i, 0, sibling, me).wait_recv()
            for j, chip in enumerate(chips):
                copy(i, 4 + j, (*chip, 1 - c), me).wait_recv()
        for cp in first + passed:
            cp.wait_send()
        for cp in local:
            cp.wait()

    return pl.pallas_call(
        body, name=name, in_specs=[ANY] * n, out_specs=[ANY] * n,
        out_shape=[jax.ShapeDtypeStruct((N_DEV,) + s.shape, s.dtype) for s in shards],
        scratch_shapes=[pltpu.SemaphoreType.DMA((n, 7)), pltpu.SemaphoreType.DMA((n, 7)),
                        pltpu.SemaphoreType.DMA((n,))],
    )(*shards)


def _all_sum(p, name):
    R, C = p.shape

    def body(p_ref, o_ref, buf, send_sems, recv_sems):
        x, y, c = _mesh_pos()
        me = 4 * x + 2 * y + c
        buf[me] = p_ref[...]
        copies = []
        for rel in range(1, N_DEV):
            peer = (1 - x if rel & 4 else x, 1 - y if rel & 2 else y, 1 - c if rel & 1 else c)
            cp = pltpu.make_async_remote_copy(
                src_ref=p_ref, dst_ref=buf.at[me], send_sem=send_sems.at[rel - 1], recv_sem=recv_sems.at[rel - 1],
                device_id=peer, device_id_type=MESH)
            cp.start()
            copies.append(cp)
        for cp in copies:
            cp.wait()
        acc = buf[0]
        for dev in range(1, N_DEV):
            acc = acc + buf[dev]
        o_ref[...] = acc

    vm = pl.BlockSpec(memory_space=pltpu.VMEM)
    return pl.pallas_call(
        body, name=name, in_specs=[vm], out_specs=vm, out_shape=jax.ShapeDtypeStruct((R, C), F32),
        scratch_shapes=[pltpu.VMEM((N_DEV, R, C), F32), pltpu.SemaphoreType.DMA((N_DEV - 1,)),
                        pltpu.SemaphoreType.DMA((N_DEV - 1,))],
    )(p)


def _swap_with_sibling(gs, name):
    n = len(gs)

    def body(*refs):
        ins, outs = refs[:n], refs[n:2 * n]
        send_sems, recv_sems = refs[2 * n:]
        x, y, c = _mesh_pos()
        copies = []
        for i in range(n):
            for k in range(4):
                cp = pltpu.make_async_remote_copy(
                    src_ref=ins[i].at[2 * k + (1 - c)], dst_ref=outs[i].at[k],
                    send_sem=send_sems.at[i, k], recv_sem=recv_sems.at[i, k],
                    device_id=(x, y, 1 - c), device_id_type=MESH)
                cp.start()
                copies.append(cp)
        for cp in copies:
            cp.wait()

    return pl.pallas_call(
        body, name=name, in_specs=[ANY] * n, out_specs=[ANY] * n,
        out_shape=[jax.ShapeDtypeStruct((4,) + g.shape[1:], g.dtype) for g in gs],
        scratch_shapes=[pltpu.SemaphoreType.DMA((n, 4)), pltpu.SemaphoreType.DMA((n, 4))],
    )(*gs)


def _swap_with_chips(ps, name):
    n = len(ps)

    def body(*refs):
        ins, outs = refs[:n], refs[n:2 * n]
        send_sems, recv_sems = refs[2 * n:]
        x, y, c = _mesh_pos()
        copies = []
        for i in range(n):
            for j, (px, py) in enumerate(_other_chips(x, y)):
                cp = pltpu.make_async_remote_copy(
                    src_ref=ins[i].at[2 * px + py], dst_ref=outs[i].at[j],
                    send_sem=send_sems.at[i, j], recv_sem=recv_sems.at[i, j],
                    device_id=(px, py, c), device_id_type=MESH)
                cp.start()
                copies.append(cp)
        for cp in copies:
            cp.wait()

    return pl.pallas_call(
        body, name=name, in_specs=[ANY] * n, out_specs=[ANY] * n,
        out_shape=[jax.ShapeDtypeStruct((3,) + p.shape[1:], p.dtype) for p in ps],
        scratch_shapes=[pltpu.SemaphoreType.DMA((n, 3)), pltpu.SemaphoreType.DMA((n, 3))],
    )(*ps)


def _row_tile(r, pref):
    if r <= pref:
        return r
    for t in range(pref - pref % 16, 0, -16):
        if r % t == 0:
            return t
    return r


def _add_sibling(g, r1, pos, name):
    _, r, D = g.shape
    tr = _row_tile(r, 512)

    def body(pos_ref, g_ref, r_ref, o_ref):
        o_ref[...] = (g_ref[...].astype(F32) + r_ref[...].astype(F32)).astype(BF16)

    return pl.pallas_call(
        body, name=name,
        grid_spec=pltpu.PrefetchScalarGridSpec(
            num_scalar_prefetch=1, grid=(4, r // tr),
            in_specs=[pl.BlockSpec((None, None, tr, D), lambda k, j, pos: (k, pos[0], j, 0)),
                      pl.BlockSpec((None, tr, D), lambda k, j, pos: (k, j, 0))],
            out_specs=pl.BlockSpec((None, tr, D), lambda k, j, pos: (k, j, 0))),
        out_shape=jax.ShapeDtypeStruct((4, r, D), BF16),
        compiler_params=_params("parallel", "parallel"),
    )(pos, g.reshape(4, 2, r, D), r1)


def _add_chips(p, r2, pos, name):
    _, r, D = p.shape
    tr = _row_tile(r, 512)

    def body(pos_ref, p_ref, r_ref, o_ref):
        acc = p_ref[...].astype(F32)
        for j in range(3):
            acc = acc + r_ref[j].astype(F32)
        o_ref[...] = acc

    return pl.pallas_call(
        body, name=name,
        grid_spec=pltpu.PrefetchScalarGridSpec(
            num_scalar_prefetch=1, grid=(r // tr,),
            in_specs=[pl.BlockSpec((None, tr, D), lambda j, pos: (pos[1], j, 0)),
                      pl.BlockSpec((3, tr, D), lambda j, pos: (0, j, 0))],
            out_specs=pl.BlockSpec((tr, D), lambda j, pos: (j, 0))),
        out_shape=jax.ShapeDtypeStruct((r, D), F32),
        compiler_params=_params("parallel"),
    )(pos, p, r2)


def _reduce_scatter(gs, pos, tag):
    g3 = [g.reshape(N_DEV, g.shape[0] // N_DEV, g.shape[1]) for g in gs]
    r1 = _swap_with_sibling(g3, f"rs_sibling_{tag}")
    ps = [_add_sibling(g, r, pos, f"rs_add_sibling_{tag}{i}") for i, (g, r) in enumerate(zip(g3, r1))]
    r2 = _swap_with_chips(ps, f"rs_chips_{tag}")
    return [_add_chips(p, r, pos, f"rs_add_chips_{tag}{i}") for i, (p, r) in enumerate(zip(ps, r2))]


def _place_cols(local, me, width):
    R, w = local.shape
    return lax.dynamic_update_slice(jnp.zeros((R, width), F32), local, (0, me * w))


def _pad_rows(a, rows):
    return jnp.pad(a, ((0, rows - a.shape[0]), (0, 0)))


def kernel(x, norm_g, ffn_w_in, ffn_w_out, conv_w_pw1, conv_b_pw1, conv_w_dw, conv_b_dw, conv_norm_g, conv_w_pw2, conv_b_pw2, attn_w_qkv, attn_q_norm, attn_k_norm, attn_w_o, loss_target, m_norm_g, m_ffn_w_in, m_ffn_w_out, m_conv_w_pw1, m_conv_b_pw1, m_conv_w_dw, m_conv_b_dw, m_conv_norm_g, m_conv_w_pw2, m_conv_b_pw2, m_attn_w_qkv, m_attn_q_norm, m_attn_k_norm, m_attn_w_o, v_norm_g, v_ffn_w_in, v_ffn_w_out, v_conv_w_pw1, v_conv_b_pw1, v_conv_w_dw, v_conv_b_dw, v_conv_norm_g, v_conv_w_pw2, v_conv_b_pw2, v_attn_w_qkv, v_attn_q_norm, v_attn_k_norm, v_attn_w_o):
    T, D = x.shape[1], x.shape[2]
    xi, yi, ci = _mesh_pos()
    me = 4 * xi + 2 * yi + ci
    pos = jnp.stack([ci, 2 * xi + yi]).astype(I32)
    lf = [(l, f) for l in range(2) for f in range(2)]

    shards = ([ffn_w_in[l, f].T.astype(BF16) for l, f in lf] + [ffn_w_out[l, f].astype(BF16) for l, f in lf]
              + [conv_w_pw1[0].T.astype(BF16), conv_w_pw2[0].astype(BF16),
                 attn_w_qkv[0].T.astype(BF16), attn_w_o[0].astype(BF16)])
    full = [w.reshape(-1, D) for w in _all_gather(shards, "gather_weights")]
    w_cols = norm_g.shape[2]
    vec = _all_sum(_pad_rows(jnp.concatenate([_place_cols(norm_g.reshape(6, w_cols), me, D),
                                              _place_cols(conv_w_dw[0], me, D)]), 40), "gather_vectors")
    wts = {
        "win_t": full[0:4], "wout": full[4:8], "pw1_t": full[8], "pw2": full[9], "qkv_t": full[10], "wo": full[11],
        "norm_g": vec[0:6], "w_dw": vec[6:6 + CONV_WIDTH],
        "b_pw1": conv_b_pw1.reshape(2, D), "b_dw": conv_b_dw, "cg": conv_norm_g, "b_pw2": conv_b_pw2,
        "q_norm": attn_q_norm[0], "k_norm": attn_k_norm[0],
    }

    loss_local, dx, gr, small = _local_step(x[0], loss_target[0], wts)
    loss = lax.psum(loss_local, ("x", "y", "c"))

    mats = _reduce_scatter(gr["win_t"] + gr["wout"] + [gr["pw1_t"], gr["pw2"], gr["qkv_t"], gr["wo"]], pos, "w")
    hd3 = 3 * HEAD_DIM
    packed = jnp.concatenate([
        small["norm_g"], small["b_pw1"], small["w_dw"], small["b_dw"], small["cg"], small["b_pw2"],
        jnp.pad(small["q_norm"].reshape(1, hd3), ((0, 0), (0, D - hd3))),
        jnp.pad(small["k_norm"].reshape(1, hd3), ((0, 0), (0, D - hd3)))])
    red = _all_sum(_pad_rows(packed, 48), "reduce_vectors")
    col0 = me * w_cols
    grads = {
        "norm_g": lax.dynamic_slice(red[0:6], (0, col0), (6, w_cols)),
        "ffn_w_in": jnp.concatenate([mats[i].T for i in range(4)]),
        "ffn_w_out": jnp.concatenate(mats[4:8]),
        "conv_w_pw1": mats[8].T,
        "conv_b_pw1": red[6:8].reshape(1, 2 * D),
        "conv_w_dw": lax.dynamic_slice(red[8:8 + CONV_WIDTH], (0, col0), (CONV_WIDTH, w_cols)),
        "conv_b_dw": red[39:40], "conv_norm_g": red[40:41], "conv_w_pw2": mats[9], "conv_b_pw2": red[41:42],
        "attn_w_qkv": mats[10].T,
        "attn_q_norm": red[42, :hd3].reshape(3, HEAD_DIM), "attn_k_norm": red[43, :hd3].reshape(3, HEAD_DIM),
        "attn_w_o": mats[11],
    }

    names = ["norm_g", "ffn_w_in", "ffn_w_out", "conv_w_pw1", "conv_b_pw1", "conv_w_dw", "conv_b_dw",
             "conv_norm_g", "conv_w_pw2", "conv_b_pw2", "attn_w_qkv", "attn_q_norm", "attn_k_norm", "attn_w_o"]
    ws = [norm_g, ffn_w_in, ffn_w_out, conv_w_pw1, conv_b_pw1, conv_w_dw, conv_b_dw, conv_norm_g, conv_w_pw2,
          conv_b_pw2, attn_w_qkv, attn_q_norm, attn_k_norm, attn_w_o]
    ms = [m_norm_g, m_ffn_w_in, m_ffn_w_out, m_conv_w_pw1, m_conv_b_pw1, m_conv_w_dw, m_conv_b_dw, m_conv_norm_g,
          m_conv_w_pw2, m_conv_b_pw2, m_attn_w_qkv, m_attn_q_norm, m_attn_k_norm, m_attn_w_o]
    vs = [v_norm_g, v_ffn_w_in, v_ffn_w_out, v_conv_w_pw1, v_conv_b_pw1, v_conv_w_dw, v_conv_b_dw, v_conv_norm_g,
          v_conv_w_pw2, v_conv_b_pw2, v_attn_w_qkv, v_attn_q_norm, v_attn_k_norm, v_attn_w_o]
    g_out, d_out, m_out, v_out = [], [], [], []
    for name, w, m, v in zip(names, ws, ms, vs):
        shape2 = (-1, w.shape[-1])
        g2 = grads[name].reshape(shape2)
        delta, new_m, new_v = _adamw(w.reshape(shape2), g2, m.reshape(shape2), v.reshape(shape2), f"adamw_{name}")
        g_out.append(g2.reshape(w.shape))
        d_out.append(delta.reshape(w.shape))
        m_out.append(new_m.reshape(w.shape))
        v_out.append(new_v.reshape(w.shape))
    return (loss, dx.reshape(x.shape), *g_out, *d_out, *m_out, *v_out)
```

```python
import functools

import jax
import jax.numpy as jnp
from jax import lax
from jax.experimental import pallas as pl
from jax.experimental.pallas import tpu as pltpu

F32 = jnp.float32
BF16 = jnp.bfloat16
I32 = jnp.int32

NORM_EPS = 1e-6
LANES = 128
HEAD_DIM = 128
ATTN_BLOCK = 128
ATTN_DILATIONS = (1, 4, 16)
CONV_WIDTH = 31
CONV_HALO = 32
STREAM_TILE = 512
ADAM_LR, ADAM_B1, ADAM_B2, ADAM_EPS, ADAM_WD, ADAM_STEP = 0.001, 0.9, 0.999, 1e-08, 0.01, 10
N_DEV = 8
NEG_BIG = -1e30
V7X_VMEM_BYTES = 64 * 1024 * 1024
VMEM_LIMIT = V7X_VMEM_BYTES - 8 * 1024 * 1024
MESH = pl.DeviceIdType.MESH
ANY = pl.BlockSpec(memory_space=pl.ANY)


def _params(*sem):
    return pltpu.CompilerParams(dimension_semantics=sem or None, vmem_limit_bytes=VMEM_LIMIT)


def _dot_nn(a, b):
    return lax.dot_general(a, b, (((1,), (0,)), ((), ())), preferred_element_type=F32)


def _dot_nt(a, b):
    return lax.dot_general(a, b, (((1,), (1,)), ((), ())), preferred_element_type=F32)


def _dot_tn(a, b):
    return lax.dot_general(a, b, (((0,), (0,)), ((), ())), preferred_element_type=F32)


def _sigmoid(x):
    return 1.0 / (1.0 + jnp.exp(-x))


def _tile(n, pref):
    if n <= pref:
        return n
    for t in range(pref - pref % 128, 0, -128):
        if n % t == 0:
            return t
    return n


def _to_lane_chunks(dst, val):
    for c in range(dst.shape[0]):
        dst[c] = val[:, c * LANES:(c + 1) * LANES]


def _rows8(v):
    tm, c = v.shape
    return v.reshape(tm // 8, 8, c).sum(axis=0)


def _rmsnorm(x, g, name):
    T, D = x.shape
    tm = min(T, 512)

    def body(x_ref, g_ref, h_ref):
        xv = x_ref[...]
        r = lax.rsqrt(jnp.mean(xv * xv, axis=-1, keepdims=True) + NORM_EPS)
        h_ref[...] = (xv * r * g_ref[...]).astype(BF16)

    return pl.pallas_call(
        body, name=name, grid=(T // tm,),
        in_specs=[pl.BlockSpec((tm, D), lambda i: (i, 0)), pl.BlockSpec((1, D), lambda i: (0, 0))],
        out_specs=pl.BlockSpec((tm, D), lambda i: (i, 0)),
        out_shape=jax.ShapeDtypeStruct((T, D), BF16),
        compiler_params=_params("parallel"),
    )(x, g)


def _rmsnorm_streams(x, g, name):
    T, D = x.shape
    tm = min(T, STREAM_TILE)
    dils = [d for d in ATTN_DILATIONS if d > 1]

    def body(x_ref, g_ref, h_ref, *rest):
        outs, hs = rest[:-1], rest[-1]
        xv = x_ref[...]
        r = lax.rsqrt(jnp.mean(xv * xv, axis=-1, keepdims=True) + NORM_EPS)
        hf = xv * r * g_ref[...]
        h_ref[...] = hf.astype(BF16)
        _to_lane_chunks(hs, hf)
        for d, o_ref in zip(dils, outs):
            for s in range(d):
                for c in range(D // LANES):
                    o_ref[s, :, c * LANES:(c + 1) * LANES] = hs.at[c][pl.ds(s, tm // d, stride=d), :].astype(BF16)

    res = pl.pallas_call(
        body, name=name, grid=(T // tm,),
        in_specs=[pl.BlockSpec((tm, D), lambda i: (i, 0)), pl.BlockSpec((1, D), lambda i: (0, 0))],
        out_specs=[pl.BlockSpec((tm, D), lambda i: (i, 0))]
        + [pl.BlockSpec((d, tm // d, D), lambda i: (0, i, 0)) for d in dils],
        out_shape=[jax.ShapeDtypeStruct((T, D), BF16)] + [jax.ShapeDtypeStruct((d, T // d, D), BF16) for d in dils],
        scratch_shapes=[pltpu.VMEM((D // LANES, tm, LANES), F32)],
        compiler_params=_params("parallel"),
    )(x, g)
    return [res[0]] + [a.reshape(T, D) for a in res[1:]]


def _gated_in(h, wt, bias, act, u_dtype, name):
    T, D = h.shape
    F = wt.shape[0] // 2
    tn = _tile(F, 1408)
    tm = min(T, 512)
    nn = F // tn

    def body(*refs):
        if bias is None:
            h_ref, w0_ref, w1_ref, z_ref, u_ref = refs
        else:
            h_ref, w0_ref, w1_ref, b_ref, z_ref, u_ref = refs
        hv = h_ref[...]
        z0 = _dot_nt(hv, w0_ref[...])
        z1 = _dot_nt(hv, w1_ref[...])
        if bias is not None:
            z0 = z0 + b_ref[0:1, :]
            z1 = z1 + b_ref[1:2, :]
        z_ref[0] = z0.astype(BF16)
        z_ref[1] = z1.astype(BF16)
        if act == "swiglu":
            u = z0 * _sigmoid(z0) * z1
        else:
            u = z0 * _sigmoid(z1)
        u_ref[...] = u.astype(u_dtype)

    in_specs = [pl.BlockSpec((tm, D), lambda n, m: (m, 0)),
                pl.BlockSpec((tn, D), lambda n, m: (n, 0)),
                pl.BlockSpec((tn, D), lambda n, m: (n + nn, 0))]
    args = [h, wt, wt]
    if bias is not None:
        in_specs.append(pl.BlockSpec((2, tn), lambda n, m: (0, n)))
        args.append(bias)
    return pl.pallas_call(
        body, name=name, grid=(nn, T // tm), in_specs=in_specs,
        out_specs=[pl.BlockSpec((2, tm, tn), lambda n, m: (0, m, n)), pl.BlockSpec((tm, tn), lambda n, m: (m, n))],
        out_shape=[jax.ShapeDtypeStruct((2, T, F), BF16), jax.ShapeDtypeStruct((T, F), u_dtype)],
        compiler_params=_params("parallel", "parallel"),
    )(*args)


def _swiglu_bwd(dxo, wout, z3, scale, name, after=()):
    T, D = dxo.shape
    F = wout.shape[0]
    tn = _tile(F, 1408)
    tm = min(T, 512)

    def body(d_ref, w_ref, z_ref, *rest):
        dz_ref = rest[-1]
        dy = (d_ref[...] * scale).astype(BF16)
        da = _dot_nt(dy, w_ref[...])
        gate = z_ref[0].astype(F32)
        up = z_ref[1].astype(F32)
        sg = _sigmoid(gate)
        dz_ref[0] = (da * up * (sg * (1.0 + gate * (1.0 - sg)))).astype(BF16)
        dz_ref[1] = (da * gate * sg).astype(BF16)

    return pl.pallas_call(
        body, name=name, grid=(F // tn, T // tm),
        in_specs=[pl.BlockSpec((tm, D), lambda n, m: (m, 0)),
                  pl.BlockSpec((tn, D), lambda n, m: (n, 0)),
                  pl.BlockSpec((2, tm, tn), lambda n, m: (0, m, n))] + [ANY] * len(after),
        out_specs=pl.BlockSpec((2, tm, tn), lambda n, m: (0, m, n)),
        out_shape=jax.ShapeDtypeStruct((2, T, F), BF16),
        compiler_params=_params("parallel", "parallel"),
    )(dxo, wout, z3, *after)


def _mm_nt(a, wt, out_dtype, name, *, w_row0=0, n_rows=None, streams=1, after=()):
    T, K = a.shape
    N = wt.shape[0] if n_rows is None else n_rows
    L = T // streams
    tn = _tile(N, 1152)
    tm = min(L, 512)
    mps = L // tm
    npn = N // tn
    assert w_row0 % tn == 0
    wb0 = w_row0 // tn

    def body(a_ref, w_ref, *rest):
        o_ref = rest[-1]
        o_ref[...] = _dot_nt(a_ref[...].astype(BF16), w_ref[...]).astype(out_dtype)

    return pl.pallas_call(
        body, name=name, grid=(npn, T // tm),
        in_specs=[pl.BlockSpec((tm, K), lambda n, m: (m, 0)), pl.BlockSpec((tn, K), lambda n, m: (wb0 + n, 0))]
        + [ANY] * len(after),
        out_specs=pl.BlockSpec((tm, tn), lambda n, m: (m % mps, (m // mps) * npn + n)),
        out_shape=jax.ShapeDtypeStruct((L, streams * N), out_dtype),
        compiler_params=_params("parallel", "parallel"),
    )(a, wt, *after)


def _mm_tn(a, b, b_scale, name, *, streams=1):
    if a.ndim == 3:
        J, T, F = a.shape
    elif streams > 1:
        J, T, F = 1, a.shape[0] * streams, a.shape[1] // streams
    else:
        (T, F), J = a.shape, 1
    N = b.shape[1]
    tmm = _tile(F, 1408)
    tk = min(T // streams, 1024)
    npj = F // tmm
    nk = T // tk
    kps = nk // streams

    def body(a_ref, b_ref, o_ref, acc_ref):
        k = pl.program_id(1)
        bv = b_ref[...]
        if b_scale != 1.0:
            bv = bv * b_scale
        part = _dot_tn(a_ref[...].astype(BF16), bv.astype(BF16))

        @pl.when(k == 0)
        def _():
            acc_ref[...] = part

        @pl.when(k > 0)
        def _():
            acc_ref[...] += part

        @pl.when(k == nk - 1)
        def _():
            o_ref[...] = acc_ref[...].astype(BF16)

    if a.ndim == 3:
        a_spec = pl.BlockSpec((None, tk, tmm), lambda i, k: (i // npj, k, i % npj))
    elif streams > 1:
        a_spec = pl.BlockSpec((tk, tmm), lambda i, k: (k % kps, (k // kps) * npj + i))
    else:
        a_spec = pl.BlockSpec((tk, tmm), lambda i, k: (k, i))
    return pl.pallas_call(
        body, name=name, grid=(J * npj, nk),
        in_specs=[a_spec, pl.BlockSpec((tk, N), lambda i, k: (k, 0))],
        out_specs=pl.BlockSpec((tmm, N), lambda i, k: (i, 0)),
        out_shape=jax.ShapeDtypeStruct((J * F, N), BF16),
        scratch_shapes=[pltpu.VMEM((tmm, N), F32)],
        compiler_params=_params("parallel", "arbitrary"),
    )(a, b)


def _rows_mm(a, b, mode, name, *, x=None, scale=1.0, bias=None, tgt=None, g=None, dxo=None,
             b_row0=0, streams=1, extras=()):
    if a.ndim == 3:
        J, T, F = a.shape
    elif streams > 1:
        J, T, F = 1, a.shape[0] * streams, a.shape[1] // streams
    else:
        (T, F), J = a.shape, 1
    D = b.shape[1]
    tk = _tile(F, 1408)
    kpj = F // tk
    nk = J * kpj
    tm = min(T // streams, STREAM_TILE)
    nm = T // tm
    mps = nm // streams
    assert b_row0 % tk == 0
    kb0 = b_row0 // tk
    n_ex = len(extras)

    def body(*refs):
        refs = list(refs)
        a_ref, b_ref = refs[:2]
        rest = refs[2:]
        if n_ex:
            tmp_ref = rest.pop()
        acc_ref = rest.pop()
        ex_refs, rest = rest[:n_ex], rest[n_ex:]
        m, k = pl.program_id(0), pl.program_id(1)
        part = _dot_nn(a_ref[...], b_ref[...])

        @pl.when(k == 0)
        def _():
            acc_ref[...] = part

        @pl.when(k > 0)
        def _():
            acc_ref[...] += part

        @pl.when(k == nk - 1)
        def _():
            acc = acc_ref[...]
            for (d, _), e_ref in zip(extras, ex_refs):
                for s in range(d):
                    for c in range(D // LANES):
                        tmp_ref.at[c][pl.ds(s, tm // d, stride=d), :] = e_ref[s, :, c * LANES:(c + 1) * LANES]
                acc = acc + jnp.concatenate([tmp_ref[c] for c in range(D // LANES)], axis=1)
            if mode == "plain":
                (o_ref,) = rest
                o_ref[...] = acc
            elif mode == "res":
                if bias is None:
                    x_ref, o_ref = rest
                else:
                    x_ref, bias_ref, o_ref = rest
                    acc = acc + bias_ref[...]
                o_ref[...] = x_ref[...] + scale * acc
            elif mode == "loss":
                x_ref, t_ref, o_ref, l_ref = rest
                e = x_ref[...] + scale * acc - t_ref[...]
                o_ref[...] = e / D

                @pl.when(m == 0)
                def _():
                    l_ref[...] = jnp.zeros_like(l_ref)

                l_ref[...] += _rows8(e * e)
            else:
                x_ref, g_ref, d_ref, o_ref, dg_ref = rest
                xv = x_ref[...]
                r = lax.rsqrt(jnp.mean(xv * xv, axis=-1, keepdims=True) + NORM_EPS)
                xh = xv * r
                dxh = acc * g_ref[...]
                o_ref[...] = d_ref[...] + r * (dxh - xh * jnp.mean(dxh * xh, axis=-1, keepdims=True))

                @pl.when(m == 0)
                def _():
                    dg_ref[...] = jnp.zeros_like(dg_ref)

                dg_ref[...] += _rows8(acc * xh)

    if a.ndim == 3:
        a_spec = pl.BlockSpec((None, tm, tk), lambda m, k: (k // kpj, m, k % kpj))
    elif streams > 1:
        a_spec = pl.BlockSpec((tm, tk), lambda m, k: (m % mps, (m // mps) * kpj + k))
    else:
        a_spec = pl.BlockSpec((tm, tk), lambda m, k: (m, k))
    row = pl.BlockSpec((tm, D), lambda m, k: (m, 0))
    vec = pl.BlockSpec((1, D), lambda m, k: (0, 0))
    part8 = pl.BlockSpec((8, D), lambda m, k: (0, 0))
    in_specs = [a_spec, pl.BlockSpec((tk, D), lambda m, k: (kb0 + k, 0))]
    args = [a, b]
    for d, e in extras:
        in_specs.append(pl.BlockSpec((d, tm // d, D), lambda m, k: (0, m, 0)))
        args.append(e.reshape(d, T // d, D))
    full = jax.ShapeDtypeStruct((T, D), F32)
    small = jax.ShapeDtypeStruct((8, D), F32)
    if mode == "plain":
        out_specs, out_shape = row, full
    elif mode == "res":
        in_specs.append(row)
        args.append(x)
        if bias is not None:
            in_specs.append(vec)
            args.append(bias)
        out_specs, out_shape = row, full
    elif mode == "loss":
        in_specs += [row, row]
        args += [x, tgt]
        out_specs, out_shape = [row, part8], [full, small]
    else:
        in_specs += [row, vec, row]
        args += [x, g, dxo]
        out_specs, out_shape = [row, part8], [full, small]
    return pl.pallas_call(
        body, name=name, grid=(nm, nk), in_specs=in_specs, out_specs=out_specs, out_shape=out_shape,
        scratch_shapes=[pltpu.VMEM((tm, D), F32)] + ([pltpu.VMEM((D // LANES, tm, LANES), F32)] if n_ex else []),
        compiler_params=_params("arbitrary", "arbitrary"),
    )(*args)


def _dwconv_fwd(u, w, b_dw, cg, name):
    T, C = u.shape
    tm = min(T, 256)
    hb = tm // CONV_HALO
    pad = CONV_HALO - (CONV_WIDTH - 1)

    def body(u_ref, halo_ref, w_ref, b_ref, g_ref, v_ref, s_ref, buf):
        i = pl.program_id(0)
        buf[pl.ds(0, CONV_HALO), :] = jnp.where(i > 0, halo_ref[...], 0.0)
        buf[pl.ds(CONV_HALO, tm), :] = u_ref[...]
        acc = jnp.zeros((tm, C), F32) + b_ref[...]
        for k in range(CONV_WIDTH):
            acc = acc + w_ref[k:k + 1, :] * buf[pl.ds(pad + k, tm), :]
        v_ref[...] = acc
        r = lax.rsqrt(jnp.mean(acc * acc, axis=-1, keepdims=True) + NORM_EPS)
        n = acc * r * g_ref[...]
        s_ref[...] = (n * _sigmoid(n)).astype(BF16)

    row = pl.BlockSpec((tm, C), lambda i: (i, 0))
    vec = pl.BlockSpec((1, C), lambda i: (0, 0))
    return pl.pallas_call(
        body, name=name, grid=(T // tm,),
        in_specs=[row, pl.BlockSpec((CONV_HALO, C), lambda i: (jnp.maximum(i * hb - 1, 0), 0)),
                  pl.BlockSpec((CONV_WIDTH, C), lambda i: (0, 0)), vec, vec],
        out_specs=[row, row],
        out_shape=[jax.ShapeDtypeStruct((T, C), F32), jax.ShapeDtypeStruct((T, C), BF16)],
        scratch_shapes=[pltpu.VMEM((CONV_HALO + tm, C), F32)],
        compiler_params=_params("parallel"),
    )(u, u, w, b_dw, cg)


def _conv_post_bwd(ds, v, cg, dxo, name):
    T, C = v.shape
    tm = min(T, 512)

    def body(ds_ref, v_ref, g_ref, d_ref, dv_ref, dcg_ref, dbdw_ref, dbpw2_ref):
        @pl.when(pl.program_id(0) == 0)
        def _():
            dcg_ref[...] = jnp.zeros_like(dcg_ref)
            dbdw_ref[...] = jnp.zeros_like(dbdw_ref)
            dbpw2_ref[...] = jnp.zeros_like(dbpw2_ref)

        vv = v_ref[...]
        r = lax.rsqrt(jnp.mean(vv * vv, axis=-1, keepdims=True) + NORM_EPS)
        vh = vv * r
        n = vh * g_ref[...]
        sg = _sigmoid(n)
        dn = ds_ref[...] * (sg * (1.0 + n * (1.0 - sg)))
        dvh = dn * g_ref[...]
        dv = r * (dvh - vh * jnp.mean(dvh * vh, axis=-1, keepdims=True))
        dv_ref[...] = dv
        dcg_ref[...] += _rows8(dn * vh)
        dbdw_ref[...] += _rows8(dv)
        dbpw2_ref[...] += _rows8(d_ref[...])

    row = pl.BlockSpec((tm, C), lambda i: (i, 0))
    part8 = pl.BlockSpec((8, C), lambda i: (0, 0))
    small = jax.ShapeDtypeStruct((8, C), F32)
    return pl.pallas_call(
        body, name=name, grid=(T // tm,),
        in_specs=[row, row, pl.BlockSpec((1, C), lambda i: (0, 0)), row],
        out_specs=[row, part8, part8, part8],
        out_shape=[jax.ShapeDtypeStruct((T, C), F32), small, small, small],
        compiler_params=_params("arbitrary"),
    )(ds, v, cg, dxo)


def _dwconv_bwd(dv, u, w, z3, name):
    T, C = u.shape
    tm = min(T, 256)
    hb = tm // CONV_HALO
    nt = T // tm
    pad = CONV_HALO - (CONV_WIDTH - 1)

    def body(dv_ref, dvn_ref, u_ref, up_ref, w_ref, z_ref, dz_ref, dw_ref, db_ref, dvbuf, ubuf, dwacc):
        i = pl.program_id(0)

        @pl.when(i == 0)
        def _():
            dwacc[...] = jnp.zeros_like(dwacc)
            db_ref[...] = jnp.zeros_like(db_ref)

        dvv = dv_ref[...]
        dvbuf[pl.ds(0, tm), :] = dvv
        dvbuf[pl.ds(tm, CONV_HALO), :] = jnp.where(i < nt - 1, dvn_ref[...], 0.0)
        ubuf[pl.ds(0, CONV_HALO), :] = jnp.where(i > 0, up_ref[...], 0.0)
        ubuf[pl.ds(CONV_HALO, tm), :] = u_ref[...]
        du = jnp.zeros((tm, C), F32)
        for k in range(CONV_WIDTH):
            du = du + w_ref[k:k + 1, :] * dvbuf[pl.ds(CONV_WIDTH - 1 - k, tm), :]
            dwacc[k] += _rows8(dvv * ubuf[pl.ds(pad + k, tm), :])
        a = z_ref[0].astype(F32)
        gate = z_ref[1].astype(F32)
        sg = _sigmoid(gate)
        da = du * sg
        dgate = du * a * sg * (1.0 - sg)
        dz_ref[0] = da.astype(BF16)
        dz_ref[1] = dgate.astype(BF16)
        db_ref[0:8, :] += _rows8(da)
        db_ref[8:16, :] += _rows8(dgate)

        @pl.when(i == nt - 1)
        def _():
            dw_ref[...] = jnp.zeros_like(dw_ref)
            for k in range(CONV_WIDTH):
                dw_ref[k:k + 1, :] = jnp.sum(dwacc[k], axis=0, keepdims=True)

    row = pl.BlockSpec((tm, C), lambda i: (i, 0))
    last_halo = T // CONV_HALO - 1
    return pl.pallas_call(
        body, name=name, grid=(nt,),
        in_specs=[row, pl.BlockSpec((CONV_HALO, C), lambda i: (jnp.minimum((i + 1) * hb, last_halo), 0)),
                  row, pl.BlockSpec((CONV_HALO, C), lambda i: (jnp.maximum(i * hb - 1, 0), 0)),
                  pl.BlockSpec((CONV_WIDTH, C), lambda i: (0, 0)),
                  pl.BlockSpec((2, tm, C), lambda i: (0, i, 0))],
        out_specs=[pl.BlockSpec((2, tm, C), lambda i: (0, i, 0)),
                   pl.BlockSpec((CONV_HALO, C), lambda i: (0, 0)),
                   pl.BlockSpec((16, C), lambda i: (0, 0))],
        out_shape=[jax.ShapeDtypeStruct((2, T, C), BF16), jax.ShapeDtypeStruct((CONV_HALO, C), F32),
                   jax.ShapeDtypeStruct((16, C), F32)],
        scratch_shapes=[pltpu.VMEM((tm + CONV_HALO, C), F32), pltpu.VMEM((CONV_HALO + tm, C), F32),
                        pltpu.VMEM((CONV_WIDTH, 8, C), F32)],
        compiler_params=_params("arbitrary"),
    )(dv, dv, u, u, w, z3)


def _head_norm(raw, gain):
    xv = raw.astype(F32)
    r = lax.rsqrt(jnp.mean(xv * xv, axis=-1, keepdims=True) + NORM_EPS)
    xh = xv * r
    return (xh * gain).astype(BF16), xh, r


def _band_masks(n):
    row = lax.broadcasted_iota(I32, (ATTN_BLOCK, ATTN_BLOCK), 0)
    col = lax.broadcasted_iota(I32, (ATTN_BLOCK, ATTN_BLOCK), 1)
    return (col >= row) & (n > 0), col <= row


def _attn_fwd(qkv, qg, kg, g, name):
    d = ATTN_DILATIONS[g]
    L = qkv.shape[0]
    HD = qkv.shape[1] // (3 * d)
    H = HD // HEAD_DIM
    nb = L // ATTN_BLOCK
    scale = HEAD_DIM ** -0.5

    def body(q_ref, kp_ref, kc_ref, vp_ref, vc_ref, qg_ref, kg_ref, o_ref, lse_ref):
        mask_p, mask_c = _band_masks(pl.program_id(1))
        lane = lax.broadcasted_iota(I32, (ATTN_BLOCK, 128), 1)
        gq = qg_ref[g:g + 1, :]
        gk = kg_ref[g:g + 1, :]
        lse_tile = jnp.zeros((ATTN_BLOCK, 128), F32)
        for h in range(H):
            sl = slice(h * HEAD_DIM, (h + 1) * HEAD_DIM)
            q = _head_norm(q_ref[:, sl], gq)[0]
            kp = _head_norm(kp_ref[:, sl], gk)[0]
            kc = _head_norm(kc_ref[:, sl], gk)[0]
            sp = jnp.where(mask_p, _dot_nt(q, kp) * scale, NEG_BIG)
            sc = jnp.where(mask_c, _dot_nt(q, kc) * scale, NEG_BIG)
            m = jnp.maximum(jnp.max(sp, axis=-1, keepdims=True), jnp.max(sc, axis=-1, keepdims=True))
            pp = jnp.exp(sp - m)
            pc = jnp.exp(sc - m)
            l = jnp.sum(pp, axis=-1, keepdims=True) + jnp.sum(pc, axis=-1, keepdims=True)
            o = _dot_nn(pp.astype(BF16), vp_ref[:, sl]) + _dot_nn(pc.astype(BF16), vc_ref[:, sl])
            o_ref[:, sl] = o / l
            lse_tile = jnp.where(lane == h, m + jnp.log(l), lse_tile)
        lse_ref[...] = lse_tile

    def blk(which, prev):
        if prev:
            return pl.BlockSpec((ATTN_BLOCK, HD), lambda r, n: (jnp.maximum(n - 1, 0), r * 3 + which))
        return pl.BlockSpec((ATTN_BLOCK, HD), lambda r, n: (n, r * 3 + which))

    gains = pl.BlockSpec((3, HEAD_DIM), lambda r, n: (0, 0))
    return pl.pallas_call(
        body, name=name, grid=(d, nb),
        in_specs=[blk(0, False), blk(1, True), blk(1, False), blk(2, True), blk(2, False), gains, gains],
        out_specs=[pl.BlockSpec((ATTN_BLOCK, HD), lambda r, n: (n, r)),
                   pl.BlockSpec((ATTN_BLOCK, 128), lambda r, n: (n, r))],
        out_shape=[jax.ShapeDtypeStruct((L, d * HD), F32), jax.ShapeDtypeStruct((L, d * 128), F32)],
        compiler_params=_params("parallel", "parallel"),
    )(qkv, qkv, qkv, qkv, qkv, qg, kg)


def _attn_merge(os, lses, name):
    T = os[0].shape[0]
    HD = os[0].shape[1]
    H = HD // HEAD_DIM
    tm = min(T, STREAM_TILE)
    dils = ATTN_DILATIONS[1:]

    def body(o0, o1, o2, l0, l1, l2, ob_ref, of_ref, lse_ref, n1, n2, m1, m2):
        for d, src, dst, w in ((dils[0], o1, n1, HD), (dils[1], o2, n2, HD), (dils[0], l1, m1, 128), (dils[1], l2, m2, 128)):
            for s in range(d):
                for c in range(w // LANES):
                    dst.at[c][pl.ds(s, tm // d, stride=d), :] = src[:, s * w + c * LANES:s * w + (c + 1) * LANES]
        ls = [l0[...], m1[0], m2[0]]
        m = jnp.maximum(jnp.maximum(ls[0], ls[1]), ls[2])
        tot = m + jnp.log(jnp.exp(ls[0] - m) + jnp.exp(ls[1] - m) + jnp.exp(ls[2] - m))
        lse_ref[...] = tot
        ws = [jnp.exp(l - tot) for l in ls]
        for h in range(H):
            sl = slice(h * HEAD_DIM, (h + 1) * HEAD_DIM)
            out = (ws[0][:, h:h + 1] * o0[:, sl] + ws[1][:, h:h + 1] * n1[h]) + ws[2][:, h:h + 1] * n2[h]
            of_ref[:, sl] = out
            ob_ref[:, sl] = out.astype(BF16)

    def wide(d, w):
        return pl.BlockSpec((tm // d, d * w), lambda i: (i, 0))

    row = pl.BlockSpec((tm, HD), lambda i: (i, 0))
    nar = pl.BlockSpec((tm, 128), lambda i: (i, 0))
    return pl.pallas_call(
        body, name=name, grid=(T // tm,),
        in_specs=[row, wide(dils[0], HD), wide(dils[1], HD), nar, wide(dils[0], 128), wide(dils[1], 128)],
        out_specs=[row, row, nar],
        out_shape=[jax.ShapeDtypeStruct((T, HD), BF16), jax.ShapeDtypeStruct((T, HD), F32),
                   jax.ShapeDtypeStruct((T, 128), F32)],
        scratch_shapes=[pltpu.VMEM((H, tm, HEAD_DIM), F32), pltpu.VMEM((H, tm, HEAD_DIM), F32),
                        pltpu.VMEM((1, tm, 128), F32), pltpu.VMEM((1, tm, 128), F32)],
        compiler_params=_params("parallel"),
    )(*os, *lses)


def _attn_delta(out, dout, lse, name):
    T, HD = out.shape
    H = HD // HEAD_DIM
    tm = min(T, STREAM_TILE)
    dils = ATTN_DILATIONS[1:]

    def body(o_ref, d_ref, l_ref, dl_ref, *rest):
        dch = rest[-1]
        lane = lax.broadcasted_iota(I32, (tm, 128), 1)
        tile = jnp.zeros((tm, 128), F32)
        for h in range(H):
            sl = slice(h * HEAD_DIM, (h + 1) * HEAD_DIM)
            dch[h] = d_ref[:, sl]
            tile = jnp.where(lane == h, jnp.sum(o_ref[:, sl] * d_ref[:, sl], axis=-1, keepdims=True), tile)
        dl_ref[...] = tile
        for i, d in enumerate(dils):
            dw_ref, lw_ref, dlw_ref = rest[3 * i:3 * i + 3]
            for s in range(d):
                rows = pl.ds(s, tm // d, stride=d)
                for h in range(H):
                    dw_ref[:, s * HD + h * HEAD_DIM:s * HD + (h + 1) * HEAD_DIM] = dch.at[h][rows, :].astype(BF16)
                lw_ref[:, s * 128:(s + 1) * 128] = l_ref[rows, :]
                dlw_ref[:, s * 128:(s + 1) * 128] = dl_ref[rows, :]

    row = pl.BlockSpec((tm, HD), lambda i: (i, 0))
    nar = pl.BlockSpec((tm, 128), lambda i: (i, 0))
    out_specs, out_shape = [nar], [jax.ShapeDtypeStruct((T, 128), F32)]
    for d in dils:
        out_specs += [pl.BlockSpec((tm // d, d * w), lambda i: (i, 0)) for w in (HD, 128, 128)]
        out_shape += [jax.ShapeDtypeStruct((T // d, d * HD), BF16), jax.ShapeDtypeStruct((T // d, d * 128), F32),
                      jax.ShapeDtypeStruct((T // d, d * 128), F32)]
    return pl.pallas_call(
        body, name=name, grid=(T // tm,), in_specs=[row, row, nar], out_specs=out_specs, out_shape=out_shape,
        scratch_shapes=[pltpu.VMEM((H, tm, HEAD_DIM), F32)],
        compiler_params=_params("parallel"),
    )(out, dout, lse)


def _attn_bwd(qkv, dout, lse, delta, qg, kg, g, name):
    d = ATTN_DILATIONS[g]
    L = qkv.shape[0]
    HD = qkv.shape[1] // (3 * d)
    H = HD // HEAD_DIM
    nb = L // ATTN_BLOCK
    scale = HEAD_DIM ** -0.5

    def body(*refs):
        (q_ref, kp_ref, kc_ref, vp_ref, vc_ref, do_ref, lse_ref, dl_ref, qg_ref, kg_ref,
         out_ref, gg_ref, dq_a, dk_a, dv_a, dq_b, dk_b, dv_b) = refs
        r_, n = pl.program_id(0), pl.program_id(1)
        gq = qg_ref[g:g + 1, :]
        gk = kg_ref[g:g + 1, :]

        @pl.when((r_ == 0) & (n == 0))
        def _():
            gg_ref[...] = jnp.zeros_like(gg_ref)

        @pl.when(n == 0)
        def _():
            dk_a[...] = jnp.zeros_like(dk_a)
            dv_a[...] = jnp.zeros_like(dv_a)

        @pl.when(n < nb)
        def _():
            mask_p, mask_c = _band_masks(n)
            gq_sum = jnp.zeros((1, HEAD_DIM), F32)
            for h in range(H):
                sl = slice(h * HEAD_DIM, (h + 1) * HEAD_DIM)
                qn, qh, rq = _head_norm(q_ref[:, sl], gq)
                kpn = _head_norm(kp_ref[:, sl], gk)[0]
                kcn = _head_norm(kc_ref[:, sl], gk)[0]
                lse_h = lse_ref[:, h:h + 1]
                dl_h = dl_ref[:, h:h + 1]
                pp = jnp.where(mask_p, jnp.exp(_dot_nt(qn, kpn) * scale - lse_h), 0.0)
                pc = jnp.where(mask_c, jnp.exp(_dot_nt(qn, kcn) * scale - lse_h), 0.0)
                do = do_ref[:, sl].astype(BF16)
                dsp = (pp * (_dot_nt(do, vp_ref[:, sl]) - dl_h) * scale).astype(BF16)
                dsc = (pc * (_dot_nt(do, vc_ref[:, sl]) - dl_h) * scale).astype(BF16)
                dqn = _dot_nn(dsp, kpn) + _dot_nn(dsc, kcn)
                dqh = dqn * gq
                dq_b[:, sl] = rq * (dqh - qh * jnp.mean(dqh * qh, axis=-1, keepdims=True))
                gq_sum = gq_sum + jnp.sum(dqn * qh, axis=0, keepdims=True)
                dk_a[:, sl] += _dot_tn(dsp, qn)
                dv_a[:, sl] += _dot_tn(pp.astype(BF16), do)
                dk_b[:, sl] = _dot_tn(dsc, qn)
                dv_b[:, sl] = _dot_tn(pc.astype(BF16), do)
            gg_ref[0:1, :] += gq_sum

        @pl.when(n > 0)
        def _():
            out_ref[:, 0:HD] = dq_a[...].astype(BF16)
            gk_sum = jnp.zeros((1, HEAD_DIM), F32)
            for h in range(H):
                sl = slice(h * HEAD_DIM, (h + 1) * HEAD_DIM)
                _, kh, rk = _head_norm(kp_ref[:, sl], gk)
                dkn = dk_a[:, sl]
                dkh = dkn * gk
                dk = rk * (dkh - kh * jnp.mean(dkh * kh, axis=-1, keepdims=True))
                out_ref[:, HD + h * HEAD_DIM:HD + (h + 1) * HEAD_DIM] = dk.astype(BF16)
                gk_sum = gk_sum + jnp.sum(dkn * kh, axis=0, keepdims=True)
            out_ref[:, 2 * HD:3 * HD] = dv_a[...].astype(BF16)
            gg_ref[1:2, :] += gk_sum

        @pl.when(n < nb)
        def _():
            dq_a[...] = dq_b[...]
            dk_a[...] = dk_b[...]
            dv_a[...] = dv_b[...]

    last = nb - 1

    def blk(which, prev):
        if prev:
            return pl.BlockSpec((ATTN_BLOCK, HD), lambda r, n: (jnp.maximum(n - 1, 0), r * 3 + which))
        return pl.BlockSpec((ATTN_BLOCK, HD), lambda r, n: (jnp.minimum(n, last), r * 3 + which))

    gains = pl.BlockSpec((3, HEAD_DIM), lambda r, n: (0, 0))
    nar = pl.BlockSpec((ATTN_BLOCK, 128), lambda r, n: (jnp.minimum(n, last), r))
    acc = pltpu.VMEM((ATTN_BLOCK, HD), F32)
    return pl.pallas_call(
        body, name=name, grid=(d, nb + 1),
        in_specs=[blk(0, False), blk(1, True), blk(1, False), blk(2, True), blk(2, False),
                  pl.BlockSpec((ATTN_BLOCK, HD), lambda r, n: (jnp.minimum(n, last), r)), nar, nar, gains, gains],
        out_specs=[pl.BlockSpec((ATTN_BLOCK, 3 * HD), lambda r, n: (jnp.maximum(n - 1, 0), r)),
                   pl.BlockSpec((8, HEAD_DIM), lambda r, n: (0, 0))],
        out_shape=[jax.ShapeDtypeStruct((L, d * 3 * HD), BF16), jax.ShapeDtypeStruct((8, HEAD_DIM), F32)],
        scratch_shapes=[acc, acc, acc, acc, acc, acc],
        compiler_params=_params("arbitrary", "arbitrary"),
    )(qkv, qkv, qkv, qkv, qkv, dout, lse, delta, qg, kg)


def _adamw(w, grad, m, v, name):
    R, C = w.shape
    tr = _row_tile(R, 512)

    def body(w_ref, g_ref, m_ref, v_ref, d_ref, nm_ref, nv_ref):
        gv = g_ref[...]
        nm = ADAM_B1 * m_ref[...] + (1.0 - ADAM_B1) * gv
        nv = ADAM_B2 * v_ref[...] + (1.0 - ADAM_B2) * (gv * gv)
        m_hat = nm / (1.0 - ADAM_B1 ** ADAM_STEP)
        v_hat = nv / (1.0 - ADAM_B2 ** ADAM_STEP)
        d_ref[...] = -ADAM_LR * (m_hat / (jnp.sqrt(v_hat) + ADAM_EPS) + ADAM_WD * w_ref[...])
        nm_ref[...] = nm
        nv_ref[...] = nv

    blk = pl.BlockSpec((tr, C), lambda i: (i, 0))
    shp = jax.ShapeDtypeStruct((R, C), F32)
    return pl.pallas_call(
        body, name=name, grid=(R // tr,), in_specs=[blk] * 4, out_specs=[blk] * 3, out_shape=[shp] * 3,
        compiler_params=_params("parallel"),
    )(w, grad, m, v)


def _ffn_fwd(x, g, comm, s, tag, tgt=None):
    h = _rmsnorm(x, g, f"rmsnorm_{tag}")
    win_t, wout = comm.weights(s, h)
    z3, a = _gated_in(h, win_t, None, "swiglu", BF16, f"ffn_in_{tag}")
    saved = (x, h, z3, a, win_t, wout)
    if tgt is None:
        return _rows_mm(a, wout, "res", f"ffn_out_{tag}", x=x, scale=0.5), saved
    return _rows_mm(a, wout, "loss", f"ffn_out_loss_{tag}", x=x, scale=0.5, tgt=tgt), saved


def _ffn_bwd(dxo, saved, g, comm, s, tag, after):
    x, h, z3, a, win_t, wout = saved
    dz3 = _swiglu_bwd(dxo, wout, z3, 0.5, f"ffn_out_bwd_{tag}", after=after)
    d_wout = _mm_tn(a, dxo, 0.5, f"ffn_dwout_{tag}")
    d_win_t = _mm_tn(dz3, h, 1.0, f"ffn_dwin_{tag}")
    dx, dg8 = _rows_mm(dz3, win_t, "rmsbwd", f"ffn_in_bwd_{tag}", x=x, g=g, dxo=dxo)
    return dx, dg8, comm.grads(s, [d_win_t, d_wout])


def _local_step(x, tgt, vecs, comm):
    norm_g = vecs["norm_g"]

    def gvec(i):
        return norm_g[i:i + 1]

    x1, sv_a0 = _ffn_fwd(x, gvec(0), comm, 0, "l0a")
    h_c = _rmsnorm(x1, gvec(1), "rmsnorm_conv")
    pw1_t, pw2 = comm.weights(1, h_c)
    zc3, u = _gated_in(h_c, pw1_t, vecs["b_pw1"], "glu", F32, "conv_pw1")
    v, s = _dwconv_fwd(u, vecs["w_dw"], vecs["b_dw"], vecs["cg"], "conv_dw")
    x2 = _rows_mm(s, pw2, "res", "conv_pw2", x=x1, bias=vecs["b_pw2"])
    x3, sv_b0 = _ffn_fwd(x2, gvec(2), comm, 2, "l0b")
    x4, sv_a1 = _ffn_fwd(x3, gvec(3), comm, 3, "l1a")
    h_s = _rmsnorm_streams(x4, gvec(4), "rmsnorm_attn")
    qkv_t, wo = comm.weights(4, h_s[0])
    hd3 = qkv_t.shape[0] // 3
    qkvs, os, lses = [], [], []
    for g, d in enumerate(ATTN_DILATIONS):
        qkv = _mm_nt(h_s[g], qkv_t, BF16, f"attn_qkv_g{g}", w_row0=g * hd3, n_rows=hd3, streams=d)
        o, lse = _attn_fwd(qkv, vecs["q_norm"], vecs["k_norm"], g, f"attn_fwd_g{g}")
        qkvs.append(qkv)
        os.append(o)
        lses.append(lse)
    out_b, out_f, lse_tot = _attn_merge(os, lses, "attn_merge")
    x5 = _rows_mm(out_b, wo, "res", "attn_wo", x=x4)
    (dy, loss8), sv_b1 = _ffn_fwd(x5, gvec(5), comm, 5, "l1b", tgt=tgt)
    loss = 0.5 * jnp.sum(loss8) / x.shape[1]

    dg = [None] * 6
    dx, dg[5], tok = _ffn_bwd(dy, sv_b1, gvec(5), comm, 5, "l1b", ())
    dout = _mm_nt(dx, wo, F32, "attn_wo_bwd", after=tok)
    d_wo = _mm_tn(out_b, dx, 1.0, "attn_dwo")
    delta, *wide = _attn_delta(out_f, dout, lse_tot, "attn_delta")
    per_group = [(dout, lse_tot, delta), tuple(wide[0:3]), tuple(wide[3:6])]
    ggs, d_qkv_t, dhs = [], [], []
    for g, d in enumerate(ATTN_DILATIONS):
        dqkv, gg = _attn_bwd(qkvs[g], *per_group[g], vecs["q_norm"], vecs["k_norm"], g, f"attn_bwd_g{g}")
        ggs.append(gg)
        d_qkv_t.append(_mm_tn(dqkv, h_s[g], 1.0, f"attn_dwqkv_g{g}", streams=d))
        if d > 1:
            dhs.append((d, _rows_mm(dqkv, qkv_t, "plain", f"attn_qkv_bwd_g{g}", b_row0=g * hd3, streams=d)))
        else:
            dqkv0 = dqkv
    dx, dg[4] = _rows_mm(dqkv0, qkv_t, "rmsbwd", "attn_qkv_bwd_g0", x=x4, g=gvec(4), dxo=dx, extras=dhs)
    tok = comm.grads(4, [jnp.concatenate(d_qkv_t), d_wo])
    dx, dg[3], tok = _ffn_bwd(dx, sv_a1, gvec(3), comm, 3, "l1a", tok)
    dx, dg[2], tok = _ffn_bwd(dx, sv_b0, gvec(2), comm, 2, "l0b", tok)
    ds = _mm_nt(dx, pw2, F32, "conv_pw2_bwd", after=tok)
    d_pw2 = _mm_tn(s, dx, 1.0, "conv_dwpw2")
    dv, dcg8, dbdw8, dbpw2_8 = _conv_post_bwd(ds, v, vecs["cg"], dx, "conv_post_bwd")
    dzc3, d_wdw, db1_16 = _dwconv_bwd(dv, u, vecs["w_dw"], zc3, "conv_dw_bwd")
    d_pw1_t = _mm_tn(dzc3, h_c, 1.0, "conv_dwpw1")
    dx, dg[1] = _rows_mm(dzc3, pw1_t, "rmsbwd", "conv_pw1_bwd", x=x1, g=gvec(1), dxo=dx)
    tok = comm.grads(1, [d_pw1_t, d_pw2])
    dx, dg[0], tok = _ffn_bwd(dx, sv_a0, gvec(0), comm, 0, "l0a", tok)

    D = x.shape[1]
    small = {
        "norm_g": jnp.stack([p.sum(axis=0) for p in dg]),
        "b_pw1": db1_16.reshape(2, 8, D).sum(axis=1),
        "w_dw": d_wdw[:CONV_WIDTH],
        "b_dw": dbdw8.sum(axis=0, keepdims=True),
        "cg": dcg8.sum(axis=0, keepdims=True),
        "b_pw2": dbpw2_8.sum(axis=0, keepdims=True),
        "q_norm": jnp.stack([gg[0] for gg in ggs]),
        "k_norm": jnp.stack([gg[1] for gg in ggs]),
    }
    return loss, dx, small, tok


def _mesh_pos():
    return lax.axis_index("x"), lax.axis_index("y"), lax.axis_index("c")


def _other_chips(x, y):
    return [(1 - x, y), (x, 1 - y), (1 - x, 1 - y)]


HBM = pl.BlockSpec(memory_space=pltpu.HBM)
SEM = pl.BlockSpec(memory_space=pltpu.SEMAPHORE)
SPLIT_COPY = pltpu.SideEffectType.DATAFLOW_SIDE_EFFECTING


def _hbm(a):
    return pltpu.with_memory_space_constraint(a, pltpu.HBM)


def _gather_copies(j, src, out, send, recv, x, y, c):
    me = 4 * x + 2 * y + c
    peers = [(x, y, 1 - c)] + [(px, py, c) for px, py in _other_chips(x, y)]
    return [pltpu.make_async_remote_copy(src_ref=src, dst_ref=out.at[me], send_sem=send.at[4 * j + k],
                                         recv_sem=recv.at[4 * j + k], device_id=peer, device_id_type=MESH)
            for k, peer in enumerate(peers)]


def _gather_start(shards, groups, name):
    n = len(shards)

    def body(*refs):
        ins, outs = refs[:n], refs[n:2 * n]
        sems, local_sems = refs[3 * n:-1], refs[-1]
        x, y, c = _mesh_pos()
        local = [pltpu.make_async_copy(ins[i], outs[i].at[4 * x + 2 * y + c], local_sems.at[i]) for i in range(n)]
        for cp in local:
            cp.start()
        for gi, grp in enumerate(groups):
            for j, i in enumerate(grp):
                for cp in _gather_copies(j, ins[i], outs[i], sems[2 * gi], sems[2 * gi + 1], x, y, c):
                    cp.start()
        for cp in local:
            cp.wait()

    out_shape = [pltpu.HBM((N_DEV,) + s.shape, s.dtype) for s in shards] + [pltpu.HBM(s.shape, s.dtype) for s in shards]
    for grp in groups:
        out_shape += [pltpu.SemaphoreType.DMA((4 * len(grp),)), pltpu.SemaphoreType.DMA((4 * len(grp),))]
    res = pl.pallas_call(
        body, name=name, in_specs=[HBM] * n, out_specs=[HBM] * (2 * n) + [SEM] * (2 * len(groups)),
        out_shape=out_shape, input_output_aliases={i: n + i for i in range(n)},
        scratch_shapes=[pltpu.SemaphoreType.DMA((n,))],
        compiler_params=pltpu.CompilerParams(has_side_effects=SPLIT_COPY),
    )(*[_hbm(s) for s in shards])
    sems = res[2 * n:]
    return res[:n], res[n:2 * n], [(sems[2 * gi], sems[2 * gi + 1]) for gi in range(len(groups))]


def _gather_wait(outs, thru, send, recv, after, name):
    n = len(outs)

    def body(*refs):
        out_refs, thru_refs, send_ref, recv_ref = refs[:n], refs[n:2 * n], refs[2 * n], refs[2 * n + 1]
        x, y, c = _mesh_pos()
        for j in range(n):
            for cp in _gather_copies(j, thru_refs[j], out_refs[j], send_ref, recv_ref, x, y, c):
                cp.wait_send()
                cp.wait_recv()

    res = pl.pallas_call(
        body, name=name, in_specs=[HBM] * (2 * n) + [SEM, SEM, ANY], out_specs=[HBM] * (2 * n),
        out_shape=[pltpu.HBM(a.shape, a.dtype) for a in list(outs) + list(thru)],
        input_output_aliases={i: i for i in range(2 * n)},
        compiler_params=pltpu.CompilerParams(has_side_effects=SPLIT_COPY),
    )(*outs, *thru, send, recv, after)
    return res[:n]


def _gather_forward(outs, name):
    n = len(outs)

    def body(*refs):
        o = refs[n:2 * n]
        send_sems, recv_sems = refs[2 * n:]
        x, y, c = _mesh_pos()
        copies = []
        for j in range(n):
            for k, (px, py) in enumerate(_other_chips(x, y)):
                blk = o[j].at[4 * px + 2 * py + c]
                cp = pltpu.make_async_remote_copy(src_ref=blk, dst_ref=blk, send_sem=send_sems.at[j, k],
                                                  recv_sem=recv_sems.at[j, k], device_id=(x, y, 1 - c), device_id_type=MESH)
                cp.start()
                copies.append(cp)
        for cp in copies:
            cp.wait()

    return pl.pallas_call(
        body, name=name, in_specs=[ANY] * n, out_specs=[ANY] * n,
        out_shape=[jax.ShapeDtypeStruct(a.shape, a.dtype) for a in outs],
        input_output_aliases={i: i for i in range(n)},
        scratch_shapes=[pltpu.SemaphoreType.DMA((n, 3)), pltpu.SemaphoreType.DMA((n, 3))],
    )(*outs)


def _all_sum(p, name):
    R, C = p.shape

    def body(p_ref, o_ref, buf, send_sems, recv_sems):
        x, y, c = _mesh_pos()
        me = 4 * x + 2 * y + c
        buf[me] = p_ref[...]
        copies = []
        for rel in range(1, N_DEV):
            peer = (1 - x if rel & 4 else x, 1 - y if rel & 2 else y, 1 - c if rel & 1 else c)
            cp = pltpu.make_async_remote_copy(
                src_ref=p_ref, dst_ref=buf.at[me], send_sem=send_sems.at[rel - 1], recv_sem=recv_sems.at[rel - 1],
                device_id=peer, device_id_type=MESH)
            cp.start()
            copies.append(cp)
        for cp in copies:
            cp.wait()
        acc = buf[0]
        for dev in range(1, N_DEV):
            acc = acc + buf[dev]
        o_ref[...] = acc

    vm = pl.BlockSpec(memory_space=pltpu.VMEM)
    return pl.pallas_call(
        body, name=name, in_specs=[vm], out_specs=vm, out_shape=jax.ShapeDtypeStruct((R, C), F32),
        scratch_shapes=[pltpu.VMEM((N_DEV, R, C), F32), pltpu.SemaphoreType.DMA((N_DEV - 1,)),
                        pltpu.SemaphoreType.DMA((N_DEV - 1,))],
    )(p)


def _swap_with_sibling(gs, name):
    n = len(gs)

    def body(*refs):
        ins, outs = refs[:n], refs[n:2 * n]
        send_sems, recv_sems = refs[2 * n:]
        x, y, c = _mesh_pos()
        copies = []
        for i in range(n):
            for k in range(4):
                cp = pltpu.make_async_remote_copy(
                    src_ref=ins[i].at[2 * k + (1 - c)], dst_ref=outs[i].at[k],
                    send_sem=send_sems.at[i, k], recv_sem=recv_sems.at[i, k],
                    device_id=(x, y, 1 - c), device_id_type=MESH)
                cp.start()
                copies.append(cp)
        for cp in copies:
            cp.wait()

    return pl.pallas_call(
        body, name=name, in_specs=[ANY] * n, out_specs=[ANY] * n,
        out_shape=[jax.ShapeDtypeStruct((4,) + g.shape[1:], g.dtype) for g in gs],
        scratch_shapes=[pltpu.SemaphoreType.DMA((n, 4)), pltpu.SemaphoreType.DMA((n, 4))],
    )(*gs)


def _scatter_copies(j, src, land, send, recv, x, y, c):
    return [pltpu.make_async_remote_copy(src_ref=src.at[2 * px + py], dst_ref=land.at[k], send_sem=send.at[3 * j + k],
                                         recv_sem=recv.at[3 * j + k], device_id=(px, py, c), device_id_type=MESH)
            for k, (px, py) in enumerate(_other_chips(x, y))]


def _scatter_start(ps, name):
    n = len(ps)

    def body(*refs):
        ins, lands = refs[:n], refs[2 * n:3 * n]
        send, recv, token = refs[3 * n:]
        x, y, c = _mesh_pos()
        for j in range(n):
            for cp in _scatter_copies(j, ins[j], lands[j], send, recv, x, y, c):
                cp.start()
        token[...] = jnp.zeros_like(token)

    res = pl.pallas_call(
        body, name=name, in_specs=[HBM] * n,
        out_specs=[HBM] * (2 * n) + [SEM, SEM, pl.BlockSpec(memory_space=pltpu.VMEM)],
        out_shape=[pltpu.HBM(p.shape, p.dtype) for p in ps] + [pltpu.HBM((3,) + p.shape[1:], p.dtype) for p in ps]
        + [pltpu.SemaphoreType.DMA((3 * n,)), pltpu.SemaphoreType.DMA((3 * n,)), jax.ShapeDtypeStruct((8, 128), F32)],
        input_output_aliases={i: i for i in range(n)},
        compiler_params=pltpu.CompilerParams(has_side_effects=SPLIT_COPY),
    )(*[_hbm(p) for p in ps])
    return res[:n], res[n:2 * n], res[2 * n], res[2 * n + 1], res[2 * n + 2]


def _scatter_wait(thru, lands, send, recv, after, name):
    n = len(thru)

    def body(*refs):
        thru_refs, land_refs, send_ref, recv_ref = refs[:n], refs[n:2 * n], refs[2 * n], refs[2 * n + 1]
        x, y, c = _mesh_pos()
        for j in range(n):
            for cp in _scatter_copies(j, thru_refs[j], land_refs[j], send_ref, recv_ref, x, y, c):
                cp.wait_send()
                cp.wait_recv()

    res = pl.pallas_call(
        body, name=name, in_specs=[HBM] * (2 * n) + [SEM, SEM, ANY], out_specs=[HBM] * (2 * n),
        out_shape=[pltpu.HBM(a.shape, a.dtype) for a in list(thru) + list(lands)],
        input_output_aliases={i: i for i in range(2 * n)},
        compiler_params=pltpu.CompilerParams(has_side_effects=SPLIT_COPY),
    )(*thru, *lands, send, recv, after)
    return res[:n], res[n:]


def _row_tile(r, pref):
    if r <= pref:
        return r
    for t in range(pref - pref % 16, 0, -16):
        if r % t == 0:
            return t
    return r


def _add_sibling(g, r1, pos, name):
    _, r, D = g.shape
    tr = _row_tile(r, 512)

    def body(pos_ref, g_ref, r_ref, o_ref):
        o_ref[...] = (g_ref[...].astype(F32) + r_ref[...].astype(F32)).astype(BF16)

    return pl.pallas_call(
        body, name=name,
        grid_spec=pltpu.PrefetchScalarGridSpec(
            num_scalar_prefetch=1, grid=(4, r // tr),
            in_specs=[pl.BlockSpec((None, None, tr, D), lambda k, j, pos: (k, pos[0], j, 0)),
                      pl.BlockSpec((None, tr, D), lambda k, j, pos: (k, j, 0))],
            out_specs=pl.BlockSpec((None, tr, D), lambda k, j, pos: (k, j, 0))),
        out_shape=jax.ShapeDtypeStruct((4, r, D), BF16),
        compiler_params=_params("parallel", "parallel"),
    )(pos, g.reshape(4, 2, r, D), r1)


def _add_chips(p, r2, pos, name):
    _, r, D = p.shape
    tr = _row_tile(r, 512)

    def body(pos_ref, p_ref, r_ref, o_ref):
        acc = p_ref[...].astype(F32)
        for j in range(3):
            acc = acc + r_ref[j].astype(F32)
        o_ref[...] = acc

    return pl.pallas_call(
        body, name=name,
        grid_spec=pltpu.PrefetchScalarGridSpec(
            num_scalar_prefetch=1, grid=(r // tr,),
            in_specs=[pl.BlockSpec((None, tr, D), lambda j, pos: (pos[1], j, 0)),
                      pl.BlockSpec((3, tr, D), lambda j, pos: (0, j, 0))],
            out_specs=pl.BlockSpec((tr, D), lambda j, pos: (j, 0))),
        out_shape=jax.ShapeDtypeStruct((r, D), F32),
        compiler_params=_params("parallel"),
    )(pos, p, r2)


def _place_cols(local, me, width):
    R, w = local.shape
    return lax.dynamic_update_slice(jnp.zeros((R, width), F32), local, (0, me * w))


def _pad_rows(a, rows):
    return jnp.pad(a, ((0, rows - a.shape[0]), (0, 0)))


SUBLAYER_SHARDS = ((0, 4), (8, 9), (1, 5), (2, 6), (10, 11), (3, 7))


class _StepComm:
    def __init__(self, shards, pos):
        self.pos = pos
        self.outs, self.thru, self.sems = _gather_start(shards, SUBLAYER_SHARDS, "gather_start")
        self.pending = {}

    def weights(self, s, after):
        idx = SUBLAYER_SHARDS[s]
        outs = _gather_wait([self.outs[i] for i in idx], [self.thru[i] for i in idx], *self.sems[s], after,
                            f"gather_wait_{s}")
        outs = _gather_forward(outs, f"gather_forward_{s}")
        return [w.reshape(-1, w.shape[-1]) for w in outs]

    def grads(self, s, mats):
        g3 = [g.reshape(N_DEV, g.shape[0] // N_DEV, g.shape[1]) for g in mats]
        r1 = _swap_with_sibling(g3, f"rs_sibling_{s}")
        ps = [_add_sibling(g, r, self.pos, f"rs_add_sibling_{s}_{i}") for i, (g, r) in enumerate(zip(g3, r1))]
        thru, lands, send, recv, token = _scatter_start(ps, f"rs_start_{s}")
        self.pending[s] = (thru, lands, send, recv)
        return (token,)

    def finish(self, after):
        out = {}
        for s in sorted(self.pending, reverse=True):
            ps, lands = _scatter_wait(*self.pending[s], after, f"rs_wait_{s}")
            for i, p, r2 in zip(SUBLAYER_SHARDS[s], ps, lands):
                out[i] = _add_chips(p, r2, self.pos, f"rs_add_chips_{i}")
        return out


def kernel(x, norm_g, ffn_w_in, ffn_w_out, conv_w_pw1, conv_b_pw1, conv_w_dw, conv_b_dw, conv_norm_g, conv_w_pw2, conv_b_pw2, attn_w_qkv, attn_q_norm, attn_k_norm, attn_w_o, loss_target, m_norm_g, m_ffn_w_in, m_ffn_w_out, m_conv_w_pw1, m_conv_b_pw1, m_conv_w_dw, m_conv_b_dw, m_conv_norm_g, m_conv_w_pw2, m_conv_b_pw2, m_attn_w_qkv, m_attn_q_norm, m_attn_k_norm, m_attn_w_o, v_norm_g, v_ffn_w_in, v_ffn_w_out, v_conv_w_pw1, v_conv_b_pw1, v_conv_w_dw, v_conv_b_dw, v_conv_norm_g, v_conv_w_pw2, v_conv_b_pw2, v_attn_w_qkv, v_attn_q_norm, v_attn_k_norm, v_attn_w_o):
    T, D = x.shape[1], x.shape[2]
    xi, yi, ci = _mesh_pos()
    me = 4 * xi + 2 * yi + ci
    pos = jnp.stack([ci, 2 * xi + yi]).astype(I32)
    lf = [(l, f) for l in range(2) for f in range(2)]

    shards = ([ffn_w_in[l, f].T.astype(BF16) for l, f in lf] + [ffn_w_out[l, f].astype(BF16) for l, f in lf]
              + [conv_w_pw1[0].T.astype(BF16), conv_w_pw2[0].astype(BF16),
                 attn_w_qkv[0].T.astype(BF16), attn_w_o[0].astype(BF16)])
    comm = _StepComm(shards, pos)
    w_cols = norm_g.shape[2]
    vec = _all_sum(_pad_rows(jnp.concatenate([_place_cols(norm_g.reshape(6, w_cols), me, D),
                                              _place_cols(conv_w_dw[0], me, D)]), 40), "gather_vectors")
    vecs = {
        "norm_g": vec[0:6], "w_dw": vec[6:6 + CONV_WIDTH],
        "b_pw1": conv_b_pw1.reshape(2, D), "b_dw": conv_b_dw, "cg": conv_norm_g, "b_pw2": conv_b_pw2,
        "q_norm": attn_q_norm[0], "k_norm": attn_k_norm[0],
    }

    loss_local, dx, small, tok = _local_step(x[0], loss_target[0], vecs, comm)
    loss = lax.psum(loss_local, ("x", "y", "c"))

    mats = comm.finish(tok[0])
    hd3 = 3 * HEAD_DIM
    packed = jnp.concatenate([
        small["norm_g"], small["b_pw1"], small["w_dw"], small["b_dw"], small["cg"], small["b_pw2"],
        jnp.pad(small["q_norm"].reshape(1, hd3), ((0, 0), (0, D - hd3))),
        jnp.pad(small["k_norm"].reshape(1, hd3), ((0, 0), (0, D - hd3)))])
    red = _all_sum(_pad_rows(packed, 48), "reduce_vectors")
    col0 = me * w_cols
    grads = {
        "norm_g": lax.dynamic_slice(red[0:6], (0, col0), (6, w_cols)),
        "ffn_w_in": jnp.concatenate([mats[i].T for i in range(4)]),
        "ffn_w_out": jnp.concatenate([mats[i] for i in range(4, 8)]),
        "conv_w_pw1": mats[8].T,
        "conv_b_pw1": red[6:8].reshape(1, 2 * D),
        "conv_w_dw": lax.dynamic_slice(red[8:8 + CONV_WIDTH], (0, col0), (CONV_WIDTH, w_cols)),
        "conv_b_dw": red[39:40], "conv_norm_g": red[40:41], "conv_w_pw2": mats[9], "conv_b_pw2": red[41:42],
        "attn_w_qkv": mats[10].T,
        "attn_q_norm": red[42, :hd3].reshape(3, HEAD_DIM), "attn_k_norm": red[43, :hd3].reshape(3, HEAD_DIM),
        "attn_w_o": mats[11],
    }

    names = ["norm_g", "ffn_w_in", "ffn_w_out", "conv_w_pw1", "conv_b_pw1", "conv_w_dw", "conv_b_dw",
             "conv_norm_g", "conv_w_pw2", "conv_b_pw2", "attn_w_qkv", "attn_q_norm", "attn_k_norm", "attn_w_o"]
    ws = [norm_g, ffn_w_in, ffn_w_out, conv_w_pw1, conv_b_pw1, conv_w_dw, conv_b_dw, conv_norm_g, conv_w_pw2,
          conv_b_pw2, attn_w_qkv, attn_q_norm, attn_k_norm, attn_w_o]
    ms = [m_norm_g, m_ffn_w_in, m_ffn_w_out, m_conv_w_pw1, m_conv_b_pw1, m_conv_w_dw, m_conv_b_dw, m_conv_norm_g,
          m_conv_w_pw2, m_conv_b_pw2, m_attn_w_qkv, m_attn_q_norm, m_attn_k_norm, m_attn_w_o]
    vs = [v_norm_g, v_ffn_w_in, v_ffn_w_out, v_conv_w_pw1, v_conv_b_pw1, v_conv_w_dw, v_conv_b_dw, v_conv_norm_g,
          v_conv_w_pw2, v_conv_b_pw2, v_attn_w_qkv, v_attn_q_norm, v_attn_k_norm, v_attn_w_o]
    g_out, d_out, m_out, v_out = [], [], [], []
    for name, w, m, v in zip(names, ws, ms, vs):
        shape2 = (-1, w.shape[-1])
        g2 = grads[name].reshape(shape2)
        delta, new_m, new_v = _adamw(w.reshape(shape2), g2, m.reshape(shape2), v.reshape(shape2), f"adamw_{name}")
        g_out.append(g2.reshape(w.shape))
        d_out.append(delta.reshape(w.shape))
        m_out.append(new_m.reshape(w.shape))
        v_out.append(new_v.reshape(w.shape))
    return (loss, dx.reshape(x.shape), *g_out, *d_out, *m_out, *v_out)
```

```python
import functools

import jax
import jax.numpy as jnp
from jax import lax
from jax.experimental import pallas as pl
from jax.experimental.pallas import tpu as pltpu

F32 = jnp.float32
BF16 = jnp.bfloat16
I32 = jnp.int32

NORM_EPS = 1e-6
LANES = 128
SUBLANES = 8
HEAD_DIM = 128
ATTN_BLOCK = 128
ATTN_DILATIONS = (1, 4, 16)
CONV_WIDTH = 31
CONV_HALO = 32
CONV_CHUNK = 16
STREAM_TILE = 512
ADAM_LR, ADAM_B1, ADAM_B2, ADAM_EPS, ADAM_WD, ADAM_STEP = 0.001, 0.9, 0.999, 1e-08, 0.01, 10
N_DEV = 8
NEG_BIG = -1e30
V7X_VMEM_BYTES = 64 * 1024 * 1024
VMEM_LIMIT = V7X_VMEM_BYTES - 8 * 1024 * 1024
MESH = pl.DeviceIdType.MESH
ANY = pl.BlockSpec(memory_space=pl.ANY)


def _params(*sem):
    return pltpu.CompilerParams(dimension_semantics=sem or None, vmem_limit_bytes=VMEM_LIMIT)


def _dot_nn(a, b):
    return lax.dot_general(a, b, (((1,), (0,)), ((), ())), preferred_element_type=F32)


def _dot_nt(a, b):
    return lax.dot_general(a, b, (((1,), (1,)), ((), ())), preferred_element_type=F32)


def _dot_tn(a, b):
    return lax.dot_general(a, b, (((0,), (0,)), ((), ())), preferred_element_type=F32)


def _sigmoid(x):
    return 1.0 / (1.0 + jnp.exp(-x))


def _tile(n, pref):
    if n <= pref:
        return n
    for t in range(pref - pref % 128, 0, -128):
        if n % t == 0:
            return t
    return n


def _to_lane_chunks(dst, val):
    for c in range(dst.shape[0]):
        dst[c] = val[:, c * LANES:(c + 1) * LANES]


def _rows8(v):
    tm, c = v.shape
    return v.reshape(tm // 8, 8, c).sum(axis=0)


def _rmsnorm(x, g, name):
    T, D = x.shape
    tm = min(T, 512)

    def body(x_ref, g_ref, h_ref):
        xv = x_ref[...]
        r = lax.rsqrt(jnp.mean(xv * xv, axis=-1, keepdims=True) + NORM_EPS)
        h_ref[...] = (xv * r * g_ref[...]).astype(BF16)

    return pl.pallas_call(
        body, name=name, grid=(T // tm,),
        in_specs=[pl.BlockSpec((tm, D), lambda i: (i, 0)), pl.BlockSpec((1, D), lambda i: (0, 0))],
        out_specs=pl.BlockSpec((tm, D), lambda i: (i, 0)),
        out_shape=jax.ShapeDtypeStruct((T, D), BF16),
        compiler_params=_params("parallel"),
    )(x, g)


def _rmsnorm_streams(x, g, name):
    T, D = x.shape
    tm = min(T, STREAM_TILE)
    dils = [d for d in ATTN_DILATIONS if d > 1]

    def body(x_ref, g_ref, h_ref, *rest):
        outs, hs = rest[:-1], rest[-1]
        xv = x_ref[...]
        r = lax.rsqrt(jnp.mean(xv * xv, axis=-1, keepdims=True) + NORM_EPS)
        hf = xv * r * g_ref[...]
        h_ref[...] = hf.astype(BF16)
        _to_lane_chunks(hs, hf)
        for d, o_ref in zip(dils, outs):
            for s in range(d):
                for c in range(D // LANES):
                    o_ref[s, :, c * LANES:(c + 1) * LANES] = hs.at[c][pl.ds(s, tm // d, stride=d), :].astype(BF16)

    res = pl.pallas_call(
        body, name=name, grid=(T // tm,),
        in_specs=[pl.BlockSpec((tm, D), lambda i: (i, 0)), pl.BlockSpec((1, D), lambda i: (0, 0))],
        out_specs=[pl.BlockSpec((tm, D), lambda i: (i, 0))]
        + [pl.BlockSpec((d, tm // d, D), lambda i: (0, i, 0)) for d in dils],
        out_shape=[jax.ShapeDtypeStruct((T, D), BF16)] + [jax.ShapeDtypeStruct((d, T // d, D), BF16) for d in dils],
        scratch_shapes=[pltpu.VMEM((D // LANES, tm, LANES), F32)],
        compiler_params=_params("parallel"),
    )(x, g)
    return [res[0]] + [a.reshape(T, D) for a in res[1:]]


def _gated_in(h, wt, bias, act, u_dtype, name, after=()):
    T, D = h.shape
    F = wt.shape[0] // 2
    tn = _tile(F, 1408)
    tm = min(T, 512)
    nn = F // tn

    def body(*refs):
        z_ref, u_ref = refs[-2:]
        if bias is None:
            h_ref, w0_ref, w1_ref = refs[:3]
        else:
            h_ref, w0_ref, w1_ref, b_ref = refs[:4]
        hv = h_ref[...]
        z0 = _dot_nt(hv, w0_ref[...])
        z1 = _dot_nt(hv, w1_ref[...])
        if bias is not None:
            z0 = z0 + b_ref[0:1, :]
            z1 = z1 + b_ref[1:2, :]
        z_ref[0] = z0.astype(BF16)
        z_ref[1] = z1.astype(BF16)
        if act == "swiglu":
            u = z0 * _sigmoid(z0) * z1
        else:
            u = z0 * _sigmoid(z1)
        u_ref[...] = u.astype(u_dtype)

    in_specs = [pl.BlockSpec((tm, D), lambda n, m: (m, 0)),
                pl.BlockSpec((tn, D), lambda n, m: (n, 0)),
                pl.BlockSpec((tn, D), lambda n, m: (n + nn, 0))]
    args = [h, wt, wt]
    if bias is not None:
        in_specs.append(pl.BlockSpec((2, tn), lambda n, m: (0, n)))
        args.append(bias)
    in_specs += [ANY] * len(after)
    args += list(after)
    return pl.pallas_call(
        body, name=name, grid=(nn, T // tm), in_specs=in_specs,
        out_specs=[pl.BlockSpec((2, tm, tn), lambda n, m: (0, m, n)), pl.BlockSpec((tm, tn), lambda n, m: (m, n))],
        out_shape=[jax.ShapeDtypeStruct((2, T, F), BF16), jax.ShapeDtypeStruct((T, F), u_dtype)],
        compiler_params=_params("parallel", "parallel"),
    )(*args)


def _swiglu_bwd(dxo, wout, z3, scale, name, after=()):
    T, D = dxo.shape
    F = wout.shape[0]
    tn = _tile(F, 1408)
    tm = min(T, 512)

    def body(d_ref, w_ref, z_ref, *rest):
        dz_ref = rest[-1]
        dy = (d_ref[...] * scale).astype(BF16)
        da = _dot_nt(dy, w_ref[...])
        gate = z_ref[0].astype(F32)
        up = z_ref[1].astype(F32)
        sg = _sigmoid(gate)
        dz_ref[0] = (da * up * (sg * (1.0 + gate * (1.0 - sg)))).astype(BF16)
        dz_ref[1] = (da * gate * sg).astype(BF16)

    return pl.pallas_call(
        body, name=name, grid=(F // tn, T // tm),
        in_specs=[pl.BlockSpec((tm, D), lambda n, m: (m, 0)),
                  pl.BlockSpec((tn, D), lambda n, m: (n, 0)),
                  pl.BlockSpec((2, tm, tn), lambda n, m: (0, m, n))] + [ANY] * len(after),
        out_specs=pl.BlockSpec((2, tm, tn), lambda n, m: (0, m, n)),
        out_shape=jax.ShapeDtypeStruct((2, T, F), BF16),
        compiler_params=_params("parallel", "parallel"),
    )(dxo, wout, z3, *after)


def _mm_nt(a, wt, out_dtype, name, *, w_row0=0, n_rows=None, streams=1, after=()):
    T, K = a.shape
    N = wt.shape[0] if n_rows is None else n_rows
    L = T // streams
    tn = _tile(N, 1152)
    tm = min(L, 512)
    mps = L // tm
    npn = N // tn
    assert w_row0 % tn == 0
    wb0 = w_row0 // tn

    def body(a_ref, w_ref, *rest):
        o_ref = rest[-1]
        o_ref[...] = _dot_nt(a_ref[...].astype(BF16), w_ref[...]).astype(out_dtype)

    return pl.pallas_call(
        body, name=name, grid=(npn, T // tm),
        in_specs=[pl.BlockSpec((tm, K), lambda n, m: (m, 0)), pl.BlockSpec((tn, K), lambda n, m: (wb0 + n, 0))]
        + [ANY] * len(after),
        out_specs=pl.BlockSpec((tm, tn), lambda n, m: (m % mps, (m // mps) * npn + n)),
        out_shape=jax.ShapeDtypeStruct((L, streams * N), out_dtype),
        compiler_params=_params("parallel", "parallel"),
    )(a, wt, *after)


def _mm_tn(a, b, b_scale, name, *, streams=1):
    if a.ndim == 3:
        J, T, F = a.shape
    elif streams > 1:
        J, T, F = 1, a.shape[0] * streams, a.shape[1] // streams
    else:
        (T, F), J = a.shape, 1
    N = b.shape[1]
    tmm = _tile(F, 1408)
    tk = min(T // streams, 1024)
    npj = F // tmm
    nk = T // tk
    kps = nk // streams

    def body(a_ref, b_ref, o_ref, acc_ref):
        k = pl.program_id(1)
        bv = b_ref[...]
        if b_scale != 1.0:
            bv = bv * b_scale
        part = _dot_tn(a_ref[...].astype(BF16), bv.astype(BF16))

        @pl.when(k == 0)
        def _():
            acc_ref[...] = part

        @pl.when(k > 0)
        def _():
            acc_ref[...] += part

        @pl.when(k == nk - 1)
        def _():
            o_ref[...] = acc_ref[...].astype(BF16)

    if a.ndim == 3:
        a_spec = pl.BlockSpec((None, tk, tmm), lambda i, k: (i // npj, k, i % npj))
    elif streams > 1:
        a_spec = pl.BlockSpec((tk, tmm), lambda i, k: (k % kps, (k // kps) * npj + i))
    else:
        a_spec = pl.BlockSpec((tk, tmm), lambda i, k: (k, i))
    return pl.pallas_call(
        body, name=name, grid=(J * npj, nk),
        in_specs=[a_spec, pl.BlockSpec((tk, N), lambda i, k: (k, 0))],
        out_specs=pl.BlockSpec((tmm, N), lambda i, k: (i, 0)),
        out_shape=jax.ShapeDtypeStruct((J * F, N), BF16),
        scratch_shapes=[pltpu.VMEM((tmm, N), F32)],
        compiler_params=_params("parallel", "arbitrary"),
    )(a, b)


def _rows_mm(a, b, mode, name, *, x=None, scale=1.0, bias=None, tgt=None, g=None, dxo=None,
             b_row0=0, streams=1, extras=()):
    if a.ndim == 3:
        J, T, F = a.shape
    elif streams > 1:
        J, T, F = 1, a.shape[0] * streams, a.shape[1] // streams
    else:
        (T, F), J = a.shape, 1
    D = b.shape[1]
    tk = _tile(F, 1408)
    kpj = F // tk
    nk = J * kpj
    tm = min(T // streams, STREAM_TILE)
    nm = T // tm
    mps = nm // streams
    assert b_row0 % tk == 0
    kb0 = b_row0 // tk
    n_ex = len(extras)

    def body(*refs):
        refs = list(refs)
        a_ref, b_ref = refs[:2]
        rest = refs[2:]
        if n_ex:
            tmp_ref = rest.pop()
        acc_ref = rest.pop()
        ex_refs, rest = rest[:n_ex], rest[n_ex:]
        m, k = pl.program_id(0), pl.program_id(1)
        part = _dot_nn(a_ref[...], b_ref[...])

        @pl.when(k == 0)
        def _():
            acc_ref[...] = part

        @pl.when(k > 0)
        def _():
            acc_ref[...] += part

        @pl.when(k == nk - 1)
        def _():
            acc = acc_ref[...]
            for (d, _), e_ref in zip(extras, ex_refs):
                for s in range(d):
                    for c in range(D // LANES):
                        tmp_ref.at[c][pl.ds(s, tm // d, stride=d), :] = e_ref[s, :, c * LANES:(c + 1) * LANES]
                acc = acc + jnp.concatenate([tmp_ref[c] for c in range(D // LANES)], axis=1)
            if mode == "plain":
                (o_ref,) = rest
                o_ref[...] = acc
            elif mode == "res":
                if bias is None:
                    x_ref, o_ref = rest
                else:
                    x_ref, bias_ref, o_ref = rest
                    acc = acc + bias_ref[...]
                o_ref[...] = x_ref[...] + scale * acc
            elif mode == "loss":
                x_ref, t_ref, o_ref, l_ref = rest
                e = x_ref[...] + scale * acc - t_ref[...]
                o_ref[...] = e / D

                @pl.when(m == 0)
                def _():
                    l_ref[...] = jnp.zeros_like(l_ref)

                l_ref[...] += _rows8(e * e)
            else:
                x_ref, g_ref, d_ref, o_ref, dg_ref = rest
                xv = x_ref[...]
                r = lax.rsqrt(jnp.mean(xv * xv, axis=-1, keepdims=True) + NORM_EPS)
                xh = xv * r
                dxh = acc * g_ref[...]
                o_ref[...] = d_ref[...] + r * (dxh - xh * jnp.mean(dxh * xh, axis=-1, keepdims=True))

                @pl.when(m == 0)
                def _():
                    dg_ref[...] = jnp.zeros_like(dg_ref)

                dg_ref[...] += _rows8(acc * xh)

    if a.ndim == 3:
        a_spec = pl.BlockSpec((None, tm, tk), lambda m, k: (k // kpj, m, k % kpj))
    elif streams > 1:
        a_spec = pl.BlockSpec((tm, tk), lambda m, k: (m % mps, (m // mps) * kpj + k))
    else:
        a_spec = pl.BlockSpec((tm, tk), lambda m, k: (m, k))
    row = pl.BlockSpec((tm, D), lambda m, k: (m, 0))
    vec = pl.BlockSpec((1, D), lambda m, k: (0, 0))
    part8 = pl.BlockSpec((8, D), lambda m, k: (0, 0))
    in_specs = [a_spec, pl.BlockSpec((tk, D), lambda m, k: (kb0 + k, 0))]
    args = [a, b]
    for d, e in extras:
        in_specs.append(pl.BlockSpec((d, tm // d, D), lambda m, k: (0, m, 0)))
        args.append(e.reshape(d, T // d, D))
    full = jax.ShapeDtypeStruct((T, D), F32)
    small = jax.ShapeDtypeStruct((8, D), F32)
    if mode == "plain":
        out_specs, out_shape = row, full
    elif mode == "res":
        in_specs.append(row)
        args.append(x)
        if bias is not None:
            in_specs.append(vec)
            args.append(bias)
        out_specs, out_shape = row, full
    elif mode == "loss":
        in_specs += [row, row]
        args += [x, tgt]
        out_specs, out_shape = [row, part8], [full, small]
    else:
        in_specs += [row, vec, row]
        args += [x, g, dxo]
        out_specs, out_shape = [row, part8], [full, small]
    return pl.pallas_call(
        body, name=name, grid=(nm, nk), in_specs=in_specs, out_specs=out_specs, out_shape=out_shape,
        scratch_shapes=[pltpu.VMEM((tm, D), F32)] + ([pltpu.VMEM((D // LANES, tm, LANES), F32)] if n_ex else []),
        compiler_params=_params("arbitrary", "arbitrary"),
    )(*args)


def _sublane_phases(sh):
    rows = sh.shape[1] - SUBLANES
    for p in range(1, SUBLANES):
        sh[p, pl.ds(0, rows), :] = sh[0, pl.ds(p, rows), :]


def _shifted(sh, offset, j, rows=SUBLANES):
    start = pl.multiple_of(j * rows + (offset - offset % SUBLANES), SUBLANES)
    return sh[offset % SUBLANES, pl.ds(start, rows), :]


def _conv_taps(wb, sh, offsets, j, init):
    groups = CONV_CHUNK // SUBLANES
    acc = init
    for k, off in enumerate(offsets):
        rows = _shifted(sh, off, j, CONV_CHUNK).reshape(groups, SUBLANES, -1)
        acc = acc + (wb[k] * rows).reshape(CONV_CHUNK, -1)
    return acc


def _broadcast_rows(dst, src):
    for r in range(dst.shape[0]):
        dst[r] = jnp.zeros(dst.shape[1:], F32) + src[r:r + 1, :]


def _dwconv_fwd(u, w, b_dw, cg, name):
    T, C = u.shape
    tm = min(T, 256)
    hb = tm // CONV_HALO
    pad = CONV_HALO - (CONV_WIDTH - 1)

    def body(u_ref, halo_ref, w_ref, b_ref, g_ref, v_ref, s_ref, sh, wb):
        i = pl.program_id(0)
        sh[0, pl.ds(0, CONV_HALO), :] = jnp.where(i > 0, halo_ref[...], 0.0)
        sh[0, pl.ds(CONV_HALO, tm), :] = u_ref[...]
        _sublane_phases(sh)
        _broadcast_rows(wb, w_ref)
        bias = jnp.zeros((CONV_CHUNK, C), F32) + b_ref[...]

        def chunk(j, carry):
            acc = _conv_taps(wb, sh, [pad + k for k in range(CONV_WIDTH)], j, bias)
            v_ref[pl.ds(pl.multiple_of(j * CONV_CHUNK, CONV_CHUNK), CONV_CHUNK), :] = acc
            return carry

        lax.fori_loop(0, tm // CONV_CHUNK, chunk, 0)
        acc = v_ref[...]
        r = lax.rsqrt(jnp.mean(acc * acc, axis=-1, keepdims=True) + NORM_EPS)
        n = acc * r * g_ref[...]
        s_ref[...] = (n * _sigmoid(n)).astype(BF16)

    row = pl.BlockSpec((tm, C), lambda i: (i, 0))
    vec = pl.BlockSpec((1, C), lambda i: (0, 0))
    return pl.pallas_call(
        body, name=name, grid=(T // tm,),
        in_specs=[row, pl.BlockSpec((CONV_HALO, C), lambda i: (jnp.maximum(i * hb - 1, 0), 0)),
                  pl.BlockSpec((CONV_WIDTH, C), lambda i: (0, 0)), vec, vec],
        out_specs=[row, row],
        out_shape=[jax.ShapeDtypeStruct((T, C), F32), jax.ShapeDtypeStruct((T, C), BF16)],
        scratch_shapes=[pltpu.VMEM((SUBLANES, CONV_HALO + tm, C), F32), pltpu.VMEM((CONV_WIDTH, SUBLANES, C), F32)],
        compiler_params=_params("parallel"),
    )(u, u, w, b_dw, cg)


def _conv_post_bwd(ds, v, cg, dxo, name):
    T, C = v.shape
    tm = min(T, 512)

    def body(ds_ref, v_ref, g_ref, d_ref, dv_ref, dcg_ref, dbdw_ref, dbpw2_ref):
        @pl.when(pl.program_id(0) == 0)
        def _():
            dcg_ref[...] = jnp.zeros_like(dcg_ref)
            dbdw_ref[...] = jnp.zeros_like(dbdw_ref)
            dbpw2_ref[...] = jnp.zeros_like(dbpw2_ref)

        vv = v_ref[...]
        r = lax.rsqrt(jnp.mean(vv * vv, axis=-1, keepdims=True) + NORM_EPS)
        vh = vv * r
        n = vh * g_ref[...]
        sg = _sigmoid(n)
        dn = ds_ref[...] * (sg * (1.0 + n * (1.0 - sg)))
        dvh = dn * g_ref[...]
        dv = r * (dvh - vh * jnp.mean(dvh * vh, axis=-1, keepdims=True))
        dv_ref[...] = dv
        dcg_ref[...] += _rows8(dn * vh)
        dbdw_ref[...] += _rows8(dv)
        dbpw2_ref[...] += _rows8(d_ref[...])

    row = pl.BlockSpec((tm, C), lambda i: (i, 0))
    part8 = pl.BlockSpec((8, C), lambda i: (0, 0))
    small = jax.ShapeDtypeStruct((8, C), F32)
    return pl.pallas_call(
        body, name=name, grid=(T // tm,),
        in_specs=[row, row, pl.BlockSpec((1, C), lambda i: (0, 0)), row],
        out_specs=[row, part8, part8, part8],
        out_shape=[jax.ShapeDtypeStruct((T, C), F32), small, small, small],
        compiler_params=_params("arbitrary"),
    )(ds, v, cg, dxo)


def _dwconv_bwd(dv, u, w, z3, name):
    T, C = u.shape
    tm = min(T, 256)
    hb = tm // CONV_HALO
    nt = T // tm
    pad = CONV_HALO - (CONV_WIDTH - 1)

    taps_at_once = 4

    def body(dv_ref, dvn_ref, u_ref, up_ref, w_ref, z_ref, dz_ref, dw_ref, db_ref, dvsh, ush, wb, du_ref, dwacc):
        i = pl.program_id(0)

        @pl.when(i == 0)
        def _():
            dwacc[...] = jnp.zeros_like(dwacc)
            db_ref[...] = jnp.zeros_like(db_ref)

        dvsh[0, pl.ds(0, tm), :] = dv_ref[...]
        dvsh[0, pl.ds(tm, CONV_HALO), :] = jnp.where(i < nt - 1, dvn_ref[...], 0.0)
        ush[0, pl.ds(0, CONV_HALO), :] = jnp.where(i > 0, up_ref[...], 0.0)
        ush[0, pl.ds(CONV_HALO, tm), :] = u_ref[...]
        _sublane_phases(dvsh)
        _sublane_phases(ush)
        _broadcast_rows(wb, w_ref)

        def du_chunk(j, carry):
            acc = _conv_taps(wb, dvsh, [CONV_WIDTH - 1 - k for k in range(CONV_WIDTH)], j,
                             jnp.zeros((CONV_CHUNK, C), F32))
            du_ref[pl.ds(pl.multiple_of(j * CONV_CHUNK, CONV_CHUNK), CONV_CHUNK), :] = acc
            return carry

        lax.fori_loop(0, tm // CONV_CHUNK, du_chunk, 0)

        for k0 in range(0, CONV_WIDTH, taps_at_once):
            taps = range(k0, min(k0 + taps_at_once, CONV_WIDTH))

            def dw_chunk(j, accs, taps=taps):
                dvc = dv_ref[pl.ds(pl.multiple_of(j * SUBLANES, SUBLANES), SUBLANES), :]
                return tuple(acc + dvc * _shifted(ush, pad + k, j) for acc, k in zip(accs, taps))

            accs = lax.fori_loop(0, tm // SUBLANES, dw_chunk, tuple(jnp.zeros((SUBLANES, C), F32) for _ in taps),
                                 unroll=4)
            for k, acc in zip(taps, accs):
                dwacc[k] += acc

        du = du_ref[...]
        a = z_ref[0].astype(F32)
        gate = z_ref[1].astype(F32)
        sg = _sigmoid(gate)
        da = du * sg
        dgate = du * a * sg * (1.0 - sg)
        dz_ref[0] = da.astype(BF16)
        dz_ref[1] = dgate.astype(BF16)
        db_ref[0:8, :] += _rows8(da)
        db_ref[8:16, :] += _rows8(dgate)

        @pl.when(i == nt - 1)
        def _():
            dw_ref[...] = jnp.zeros_like(dw_ref)
            for k in range(CONV_WIDTH):
                dw_ref[k:k + 1, :] = jnp.sum(dwacc[k], axis=0, keepdims=True)

    row = pl.BlockSpec((tm, C), lambda i: (i, 0))
    last_halo = T // CONV_HALO - 1
    return pl.pallas_call(
        body, name=name, grid=(nt,),
        in_specs=[row, pl.BlockSpec((CONV_HALO, C), lambda i: (jnp.minimum((i + 1) * hb, last_halo), 0)),
                  row, pl.BlockSpec((CONV_HALO, C), lambda i: (jnp.maximum(i * hb - 1, 0), 0)),
                  pl.BlockSpec((CONV_WIDTH, C), lambda i: (0, 0)),
                  pl.BlockSpec((2, tm, C), lambda i: (0, i, 0))],
        out_specs=[pl.BlockSpec((2, tm, C), lambda i: (0, i, 0)),
                   pl.BlockSpec((CONV_HALO, C), lambda i: (0, 0)),
                   pl.BlockSpec((16, C), lambda i: (0, 0))],
        out_shape=[jax.ShapeDtypeStruct((2, T, C), BF16), jax.ShapeDtypeStruct((CONV_HALO, C), F32),
                   jax.ShapeDtypeStruct((16, C), F32)],
        scratch_shapes=[pltpu.VMEM((SUBLANES, tm + CONV_HALO, C), F32), pltpu.VMEM((SUBLANES, CONV_HALO + tm, C), F32),
                        pltpu.VMEM((CONV_WIDTH, SUBLANES, C), F32), pltpu.VMEM((tm, C), F32),
                        pltpu.VMEM((CONV_WIDTH, SUBLANES, C), F32)],
        compiler_params=_params("arbitrary"),
    )(dv, dv, u, u, w, z3)


def _head_norm(raw, gain):
    xv = raw.astype(F32)
    r = lax.rsqrt(jnp.mean(xv * xv, axis=-1, keepdims=True) + NORM_EPS)
    xh = xv * r
    return (xh * gain).astype(BF16), xh, r


def _band_masks(n):
    row = lax.broadcasted_iota(I32, (ATTN_BLOCK, ATTN_BLOCK), 0)
    col = lax.broadcasted_iota(I32, (ATTN_BLOCK, ATTN_BLOCK), 1)
    return (col >= row) & (n > 0), col <= row


def _attn_fwd(qkv, qg, kg, g, name):
    d = ATTN_DILATIONS[g]
    L = qkv.shape[0]
    HD = qkv.shape[1] // (3 * d)
    H = HD // HEAD_DIM
    nb = L // ATTN_BLOCK
    scale = HEAD_DIM ** -0.5

    def body(q_ref, kp_ref, kc_ref, vp_ref, vc_ref, qg_ref, kg_ref, o_ref, lse_ref):
        mask_p, mask_c = _band_masks(pl.program_id(1))
        lane = lax.broadcasted_iota(I32, (ATTN_BLOCK, 128), 1)
        gq = qg_ref[g:g + 1, :]
        gk = kg_ref[g:g + 1, :]
        lse_tile = jnp.zeros((ATTN_BLOCK, 128), F32)
        for h in range(H):
            sl = slice(h * HEAD_DIM, (h + 1) * HEAD_DIM)
            q = _head_norm(q_ref[:, sl], gq)[0]
            kp = _head_norm(kp_ref[:, sl], gk)[0]
            kc = _head_norm(kc_ref[:, sl], gk)[0]
            sp = jnp.where(mask_p, _dot_nt(q, kp) * scale, NEG_BIG)
            sc = jnp.where(mask_c, _dot_nt(q, kc) * scale, NEG_BIG)
            m = jnp.maximum(jnp.max(sp, axis=-1, keepdims=True), jnp.max(sc, axis=-1, keepdims=True))
            pp = jnp.exp(sp - m)
            pc = jnp.exp(sc - m)
            l = jnp.sum(pp, axis=-1, keepdims=True) + jnp.sum(pc, axis=-1, keepdims=True)
            o = _dot_nn(pp.astype(BF16), vp_ref[:, sl]) + _dot_nn(pc.astype(BF16), vc_ref[:, sl])
            o_ref[:, sl] = o / l
            lse_tile = jnp.where(lane == h, m + jnp.log(l), lse_tile)
        lse_ref[...] = lse_tile

    def blk(which, prev):
        if prev:
            return pl.BlockSpec((ATTN_BLOCK, HD), lambda r, n: (jnp.maximum(n - 1, 0), r * 3 + which))
        return pl.BlockSpec((ATTN_BLOCK, HD), lambda r, n: (n, r * 3 + which))

    gains = pl.BlockSpec((3, HEAD_DIM), lambda r, n: (0, 0))
    return pl.pallas_call(
        body, name=name, grid=(d, nb),
        in_specs=[blk(0, False), blk(1, True), blk(1, False), blk(2, True), blk(2, False), gains, gains],
        out_specs=[pl.BlockSpec((ATTN_BLOCK, HD), lambda r, n: (n, r)),
                   pl.BlockSpec((ATTN_BLOCK, 128), lambda r, n: (n, r))],
        out_shape=[jax.ShapeDtypeStruct((L, d * HD), F32), jax.ShapeDtypeStruct((L, d * 128), F32)],
        compiler_params=_params("parallel", "parallel"),
    )(qkv, qkv, qkv, qkv, qkv, qg, kg)


def _attn_merge(os, lses, name):
    T = os[0].shape[0]
    HD = os[0].shape[1]
    H = HD // HEAD_DIM
    tm = min(T, STREAM_TILE)
    dils = ATTN_DILATIONS[1:]

    def body(o0, o1, o2, l0, l1, l2, ob_ref, of_ref, lse_ref, n1, n2, m1, m2):
        for d, src, dst, w in ((dils[0], o1, n1, HD), (dils[1], o2, n2, HD), (dils[0], l1, m1, 128), (dils[1], l2, m2, 128)):
            for s in range(d):
                for c in range(w // LANES):
                    dst.at[c][pl.ds(s, tm // d, stride=d), :] = src[:, s * w + c * LANES:s * w + (c + 1) * LANES]
        ls = [l0[...], m1[0], m2[0]]
        m = jnp.maximum(jnp.maximum(ls[0], ls[1]), ls[2])
        tot = m + jnp.log(jnp.exp(ls[0] - m) + jnp.exp(ls[1] - m) + jnp.exp(ls[2] - m))
        lse_ref[...] = tot
        ws = [jnp.exp(l - tot) for l in ls]
        for h in range(H):
            sl = slice(h * HEAD_DIM, (h + 1) * HEAD_DIM)
            out = (ws[0][:, h:h + 1] * o0[:, sl] + ws[1][:, h:h + 1] * n1[h]) + ws[2][:, h:h + 1] * n2[h]
            of_ref[:, sl] = out
            ob_ref[:, sl] = out.astype(BF16)

    def wide(d, w):
        return pl.BlockSpec((tm // d, d * w), lambda i: (i, 0))

    row = pl.BlockSpec((tm, HD), lambda i: (i, 0))
    nar = pl.BlockSpec((tm, 128), lambda i: (i, 0))
    return pl.pallas_call(
        body, name=name, grid=(T // tm,),
        in_specs=[row, wide(dils[0], HD), wide(dils[1], HD), nar, wide(dils[0], 128), wide(dils[1], 128)],
        out_specs=[row, row, nar],
        out_shape=[jax.ShapeDtypeStruct((T, HD), BF16), jax.ShapeDtypeStruct((T, HD), F32),
                   jax.ShapeDtypeStruct((T, 128), F32)],
        scratch_shapes=[pltpu.VMEM((H, tm, HEAD_DIM), F32), pltpu.VMEM((H, tm, HEAD_DIM), F32),
                        pltpu.VMEM((1, tm, 128), F32), pltpu.VMEM((1, tm, 128), F32)],
        compiler_params=_params("parallel"),
    )(*os, *lses)


def _attn_delta(out, dout, lse, name):
    T, HD = out.shape
    H = HD // HEAD_DIM
    tm = min(T, STREAM_TILE)
    dils = ATTN_DILATIONS[1:]

    def body(o_ref, d_ref, l_ref, dl_ref, *rest):
        dch = rest[-1]
        lane = lax.broadcasted_iota(I32, (tm, 128), 1)
        tile = jnp.zeros((tm, 128), F32)
        for h in range(H):
            sl = slice(h * HEAD_DIM, (h + 1) * HEAD_DIM)
            dch[h] = d_ref[:, sl]
            tile = jnp.where(lane == h, jnp.sum(o_ref[:, sl] * d_ref[:, sl], axis=-1, keepdims=True), tile)
        dl_ref[...] = tile
        for i, d in enumerate(dils):
            dw_ref, lw_ref, dlw_ref = rest[3 * i:3 * i + 3]
            for s in range(d):
                rows = pl.ds(s, tm // d, stride=d)
                for h in range(H):
                    dw_ref[:, s * HD + h * HEAD_DIM:s * HD + (h + 1) * HEAD_DIM] = dch.at[h][rows, :].astype(BF16)
                lw_ref[:, s * 128:(s + 1) * 128] = l_ref[rows, :]
                dlw_ref[:, s * 128:(s + 1) * 128] = dl_ref[rows, :]

    row = pl.BlockSpec((tm, HD), lambda i: (i, 0))
    nar = pl.BlockSpec((tm, 128), lambda i: (i, 0))
    out_specs, out_shape = [nar], [jax.ShapeDtypeStruct((T, 128), F32)]
    for d in dils:
        out_specs += [pl.BlockSpec((tm // d, d * w), lambda i: (i, 0)) for w in (HD, 128, 128)]
        out_shape += [jax.ShapeDtypeStruct((T // d, d * HD), BF16), jax.ShapeDtypeStruct((T // d, d * 128), F32),
                      jax.ShapeDtypeStruct((T // d, d * 128), F32)]
    return pl.pallas_call(
        body, name=name, grid=(T // tm,), in_specs=[row, row, nar], out_specs=out_specs, out_shape=out_shape,
        scratch_shapes=[pltpu.VMEM((H, tm, HEAD_DIM), F32)],
        compiler_params=_params("parallel"),
    )(out, dout, lse)


def _attn_bwd(qkv, dout, lse, delta, qg, kg, g, name):
    d = ATTN_DILATIONS[g]
    L = qkv.shape[0]
    HD = qkv.shape[1] // (3 * d)
    H = HD // HEAD_DIM
    nb = L // ATTN_BLOCK
    scale = HEAD_DIM ** -0.5

    def body(*refs):
        (q_ref, kp_ref, kc_ref, vp_ref, vc_ref, do_ref, lse_ref, dl_ref, qg_ref, kg_ref,
         out_ref, gg_ref, dq_a, dk_a, dv_a, dq_b, dk_b, dv_b) = refs
        r_, n = pl.program_id(0), pl.program_id(1)
        gq = qg_ref[g:g + 1, :]
        gk = kg_ref[g:g + 1, :]

        @pl.when((r_ == 0) & (n == 0))
        def _():
            gg_ref[...] = jnp.zeros_like(gg_ref)

        @pl.when(n == 0)
        def _():
            dk_a[...] = jnp.zeros_like(dk_a)
            dv_a[...] = jnp.zeros_like(dv_a)

        @pl.when(n < nb)
        def _():
            mask_p, mask_c = _band_masks(n)
            gq_sum = jnp.zeros((1, HEAD_DIM), F32)
            for h in range(H):
                sl = slice(h * HEAD_DIM, (h + 1) * HEAD_DIM)
                qn, qh, rq = _head_norm(q_ref[:, sl], gq)
                kpn = _head_norm(kp_ref[:, sl], gk)[0]
                kcn = _head_norm(kc_ref[:, sl], gk)[0]
                lse_h = lse_ref[:, h:h + 1]
                dl_h = dl_ref[:, h:h + 1]
                pp = jnp.where(mask_p, jnp.exp(_dot_nt(qn, kpn) * scale - lse_h), 0.0)
                pc = jnp.where(mask_c, jnp.exp(_dot_nt(qn, kcn) * scale - lse_h), 0.0)
                do = do_ref[:, sl].astype(BF16)
                dsp = (pp * (_dot_nt(do, vp_ref[:, sl]) - dl_h) * scale).astype(BF16)
                dsc = (pc * (_dot_nt(do, vc_ref[:, sl]) - dl_h) * scale).astype(BF16)
                dqn = _dot_nn(dsp, kpn) + _dot_nn(dsc, kcn)
                dqh = dqn * gq
                dq_b[:, sl] = rq * (dqh - qh * jnp.mean(dqh * qh, axis=-1, keepdims=True))
                gq_sum = gq_sum + jnp.sum(dqn * qh, axis=0, keepdims=True)
                dk_a[:, sl] += _dot_tn(dsp, qn)
                dv_a[:, sl] += _dot_tn(pp.astype(BF16), do)
                dk_b[:, sl] = _dot_tn(dsc, qn)
                dv_b[:, sl] = _dot_tn(pc.astype(BF16), do)
            gg_ref[0:1, :] += gq_sum

        @pl.when(n > 0)
        def _():
            out_ref[:, 0:HD] = dq_a[...].astype(BF16)
            gk_sum = jnp.zeros((1, HEAD_DIM), F32)
            for h in range(H):
                sl = slice(h * HEAD_DIM, (h + 1) * HEAD_DIM)
                _, kh, rk = _head_norm(kp_ref[:, sl], gk)
                dkn = dk_a[:, sl]
                dkh = dkn * gk
                dk = rk * (dkh - kh * jnp.mean(dkh * kh, axis=-1, keepdims=True))
                out_ref[:, HD + h * HEAD_DIM:HD + (h + 1) * HEAD_DIM] = dk.astype(BF16)
                gk_sum = gk_sum + jnp.sum(dkn * kh, axis=0, keepdims=True)
            out_ref[:, 2 * HD:3 * HD] = dv_a[...].astype(BF16)
            gg_ref[1:2, :] += gk_sum

        @pl.when(n < nb)
        def _():
            dq_a[...] = dq_b[...]
            dk_a[...] = dk_b[...]
            dv_a[...] = dv_b[...]

    last = nb - 1

    def blk(which, prev):
        if prev:
            return pl.BlockSpec((ATTN_BLOCK, HD), lambda r, n: (jnp.maximum(n - 1, 0), r * 3 + which))
        return pl.BlockSpec((ATTN_BLOCK, HD), lambda r, n: (jnp.minimum(n, last), r * 3 + which))

    gains = pl.BlockSpec((3, HEAD_DIM), lambda r, n: (0, 0))
    nar = pl.BlockSpec((ATTN_BLOCK, 128), lambda r, n: (jnp.minimum(n, last), r))
    acc = pltpu.VMEM((ATTN_BLOCK, HD), F32)
    return pl.pallas_call(
        body, name=name, grid=(d, nb + 1),
        in_specs=[blk(0, False), blk(1, True), blk(1, False), blk(2, True), blk(2, False),
                  pl.BlockSpec((ATTN_BLOCK, HD), lambda r, n: (jnp.minimum(n, last), r)), nar, nar, gains, gains],
        out_specs=[pl.BlockSpec((ATTN_BLOCK, 3 * HD), lambda r, n: (jnp.maximum(n - 1, 0), r)),
                   pl.BlockSpec((8, HEAD_DIM), lambda r, n: (0, 0))],
        out_shape=[jax.ShapeDtypeStruct((L, d * 3 * HD), BF16), jax.ShapeDtypeStruct((8, HEAD_DIM), F32)],
        scratch_shapes=[acc, acc, acc, acc, acc, acc],
        compiler_params=_params("arbitrary", "arbitrary"),
    )(qkv, qkv, qkv, qkv, qkv, dout, lse, delta, qg, kg)


def _adamw(w, grad, m, v, name):
    R, C = w.shape
    tr = _row_tile(R, 512)

    def body(w_ref, g_ref, m_ref, v_ref, d_ref, nm_ref, nv_ref):
        gv = g_ref[...]
        nm = ADAM_B1 * m_ref[...] + (1.0 - ADAM_B1) * gv
        nv = ADAM_B2 * v_ref[...] + (1.0 - ADAM_B2) * (gv * gv)
        m_hat = nm / (1.0 - ADAM_B1 ** ADAM_STEP)
        v_hat = nv / (1.0 - ADAM_B2 ** ADAM_STEP)
        d_ref[...] = -ADAM_LR * (m_hat / (jnp.sqrt(v_hat) + ADAM_EPS) + ADAM_WD * w_ref[...])
        nm_ref[...] = nm
        nv_ref[...] = nv

    blk = pl.BlockSpec((tr, C), lambda i: (i, 0))
    shp = jax.ShapeDtypeStruct((R, C), F32)
    return pl.pallas_call(
        body, name=name, grid=(R // tr,), in_specs=[blk] * 4, out_specs=[blk] * 3, out_shape=[shp] * 3,
        compiler_params=_params("parallel"),
    )(w, grad, m, v)


def _ffn_fwd(x, g, comm, s, tag, tgt=None):
    h = _rmsnorm(x, g, f"rmsnorm_{tag}")
    (win_t, wout), tok = comm.weights(s, h)
    z3, a = _gated_in(h, win_t, None, "swiglu", BF16, f"ffn_in_{tag}", after=tok)
    saved = (x, h, z3, a, win_t, wout)
    if tgt is None:
        return _rows_mm(a, wout, "res", f"ffn_out_{tag}", x=x, scale=0.5), saved
    return _rows_mm(a, wout, "loss", f"ffn_out_loss_{tag}", x=x, scale=0.5, tgt=tgt), saved


def _ffn_bwd(dxo, saved, g, comm, s, tag, after):
    x, h, z3, a, win_t, wout = saved
    dz3 = _swiglu_bwd(dxo, wout, z3, 0.5, f"ffn_out_bwd_{tag}", after=after)
    d_wout = _mm_tn(a, dxo, 0.5, f"ffn_dwout_{tag}")
    d_win_t = _mm_tn(dz3, h, 1.0, f"ffn_dwin_{tag}")
    dx, dg8 = _rows_mm(dz3, win_t, "rmsbwd", f"ffn_in_bwd_{tag}", x=x, g=g, dxo=dxo)
    return dx, dg8, comm.grads(s, [d_win_t, d_wout])


def _local_step(x, tgt, vecs, comm):
    norm_g = vecs["norm_g"]

    def gvec(i):
        return norm_g[i:i + 1]

    x1, sv_a0 = _ffn_fwd(x, gvec(0), comm, 0, "l0a")
    h_c = _rmsnorm(x1, gvec(1), "rmsnorm_conv")
    (pw1_t, pw2), tok = comm.weights(1, h_c)
    zc3, u = _gated_in(h_c, pw1_t, vecs["b_pw1"], "glu", F32, "conv_pw1", after=tok)
    v, s = _dwconv_fwd(u, vecs["w_dw"], vecs["b_dw"], vecs["cg"], "conv_dw")
    x2 = _rows_mm(s, pw2, "res", "conv_pw2", x=x1, bias=vecs["b_pw2"])
    x3, sv_b0 = _ffn_fwd(x2, gvec(2), comm, 2, "l0b")
    x4, sv_a1 = _ffn_fwd(x3, gvec(3), comm, 3, "l1a")
    h_s = _rmsnorm_streams(x4, gvec(4), "rmsnorm_attn")
    (qkv_t, wo), tok = comm.weights(4, h_s[0])
    hd3 = qkv_t.shape[0] // 3
    qkvs, os, lses = [], [], []
    for g, d in enumerate(ATTN_DILATIONS):
        qkv = _mm_nt(h_s[g], qkv_t, BF16, f"attn_qkv_g{g}", w_row0=g * hd3, n_rows=hd3, streams=d, after=tok)
        o, lse = _attn_fwd(qkv, vecs["q_norm"], vecs["k_norm"], g, f"attn_fwd_g{g}")
        qkvs.append(qkv)
        os.append(o)
        lses.append(lse)
    out_b, out_f, lse_tot = _attn_merge(os, lses, "attn_merge")
    x5 = _rows_mm(out_b, wo, "res", "attn_wo", x=x4)
    (dy, loss8), sv_b1 = _ffn_fwd(x5, gvec(5), comm, 5, "l1b", tgt=tgt)
    loss = 0.5 * jnp.sum(loss8) / x.shape[1]

    dg = [None] * 6
    dx, dg[5], tok = _ffn_bwd(dy, sv_b1, gvec(5), comm, 5, "l1b", ())
    dout = _mm_nt(dx, wo, F32, "attn_wo_bwd", after=tok)
    d_wo = _mm_tn(out_b, dx, 1.0, "attn_dwo")
    delta, *wide = _attn_delta(out_f, dout, lse_tot, "attn_delta")
    per_group = [(dout, lse_tot, delta), tuple(wide[0:3]), tuple(wide[3:6])]
    ggs, d_qkv_t, dhs = [], [], []
    for g, d in enumerate(ATTN_DILATIONS):
        dqkv, gg = _attn_bwd(qkvs[g], *per_group[g], vecs["q_norm"], vecs["k_norm"], g, f"attn_bwd_g{g}")
        ggs.append(gg)
        d_qkv_t.append(_mm_tn(dqkv, h_s[g], 1.0, f"attn_dwqkv_g{g}", streams=d))
        if d > 1:
            dhs.append((d, _rows_mm(dqkv, qkv_t, "plain", f"attn_qkv_bwd_g{g}", b_row0=g * hd3, streams=d)))
        else:
            dqkv0 = dqkv
    dx, dg[4] = _rows_mm(dqkv0, qkv_t, "rmsbwd", "attn_qkv_bwd_g0", x=x4, g=gvec(4), dxo=dx, extras=dhs)
    tok = comm.grads(4, [jnp.concatenate(d_qkv_t), d_wo])
    dx, dg[3], tok = _ffn_bwd(dx, sv_a1, gvec(3), comm, 3, "l1a", tok)
    dx, dg[2], tok = _ffn_bwd(dx, sv_b0, gvec(2), comm, 2, "l0b", tok)
    ds = _mm_nt(dx, pw2, F32, "conv_pw2_bwd", after=tok)
    d_pw2 = _mm_tn(s, dx, 1.0, "conv_dwpw2")
    dv, dcg8, dbdw8, dbpw2_8 = _conv_post_bwd(ds, v, vecs["cg"], dx, "conv_post_bwd")
    dzc3, d_wdw, db1_16 = _dwconv_bwd(dv, u, vecs["w_dw"], zc3, "conv_dw_bwd")
    d_pw1_t = _mm_tn(dzc3, h_c, 1.0, "conv_dwpw1")
    dx, dg[1] = _rows_mm(dzc3, pw1_t, "rmsbwd", "conv_pw1_bwd", x=x1, g=gvec(1), dxo=dx)
    tok = comm.grads(1, [d_pw1_t, d_pw2])
    dx, dg[0], tok = _ffn_bwd(dx, sv_a0, gvec(0), comm, 0, "l0a", tok)

    D = x.shape[1]
    small = {
        "norm_g": jnp.stack([p.sum(axis=0) for p in dg]),
        "b_pw1": db1_16.reshape(2, 8, D).sum(axis=1),
        "w_dw": d_wdw[:CONV_WIDTH],
        "b_dw": dbdw8.sum(axis=0, keepdims=True),
        "cg": dcg8.sum(axis=0, keepdims=True),
        "b_pw2": dbpw2_8.sum(axis=0, keepdims=True),
        "q_norm": jnp.stack([gg[0] for gg in ggs]),
        "k_norm": jnp.stack([gg[1] for gg in ggs]),
    }
    return loss, dx, small, tok


def _mesh_pos():
    return lax.axis_index("x"), lax.axis_index("y"), lax.axis_index("c")


def _other_chips(x, y):
    return [(1 - x, y), (x, 1 - y), (1 - x, 1 - y)]


HBM = pl.BlockSpec(memory_space=pltpu.HBM)
SEM = pl.BlockSpec(memory_space=pltpu.SEMAPHORE)
SPLIT_COPY = pltpu.SideEffectType.DATAFLOW_SIDE_EFFECTING


def _hbm(a):
    return pltpu.with_memory_space_constraint(a, pltpu.HBM)


def _gather_copies(j, src, out, send, recv, x, y, c):
    me = 4 * x + 2 * y + c
    peers = [(x, y, 1 - c)] + [(px, py, c) for px, py in _other_chips(x, y)]
    return [pltpu.make_async_remote_copy(src_ref=src, dst_ref=out.at[me], send_sem=send.at[4 * j + k],
                                         recv_sem=recv.at[4 * j + k], device_id=peer, device_id_type=MESH)
            for k, peer in enumerate(peers)]


def _gather_start(shards, after, name):
    n = len(shards)

    def body(*refs):
        ins, outs = refs[:n], refs[n + len(after):2 * n + len(after)]
        send, recv, token, local_sems = refs[-4:]
        x, y, c = _mesh_pos()
        local = [pltpu.make_async_copy(ins[i], outs[i].at[4 * x + 2 * y + c], local_sems.at[i]) for i in range(n)]
        for cp in local:
            cp.start()
        for j in range(n):
            for cp in _gather_copies(j, ins[j], outs[j], send, recv, x, y, c):
                cp.start()
        for cp in local:
            cp.wait()
        token[...] = jnp.zeros_like(token)

    res = pl.pallas_call(
        body, name=name, in_specs=[HBM] * n + [ANY] * len(after),
        out_specs=[HBM] * (2 * n) + [SEM, SEM, pl.BlockSpec(memory_space=pltpu.VMEM)],
        out_shape=[pltpu.HBM((N_DEV,) + s.shape, s.dtype) for s in shards] + [pltpu.HBM(s.shape, s.dtype) for s in shards]
        + [pltpu.SemaphoreType.DMA((4 * n,)), pltpu.SemaphoreType.DMA((4 * n,)), jax.ShapeDtypeStruct((8, 128), F32)],
        input_output_aliases={i: n + i for i in range(n)},
        scratch_shapes=[pltpu.SemaphoreType.DMA((n,))],
        compiler_params=pltpu.CompilerParams(has_side_effects=SPLIT_COPY),
    )(*[_hbm(s) for s in shards], *after)
    return res[:n], res[n:2 * n], res[2 * n], res[2 * n + 1], res[2 * n + 2]


def _gather_wait(outs, thru, send, recv, after, name):
    n = len(outs)

    def body(*refs):
        out_refs, thru_refs, send_ref, recv_ref = refs[:n], refs[n:2 * n], refs[2 * n], refs[2 * n + 1]
        x, y, c = _mesh_pos()
        for j in range(n):
            for cp in _gather_copies(j, thru_refs[j], out_refs[j], send_ref, recv_ref, x, y, c):
                cp.wait_send()
                cp.wait_recv()

    res = pl.pallas_call(
        body, name=name, in_specs=[HBM] * (2 * n) + [SEM, SEM, ANY], out_specs=[HBM] * (2 * n),
        out_shape=[pltpu.HBM(a.shape, a.dtype) for a in list(outs) + list(thru)],
        input_output_aliases={i: i for i in range(2 * n)},
        compiler_params=pltpu.CompilerParams(has_side_effects=SPLIT_COPY),
    )(*outs, *thru, send, recv, after)
    return res[:n]


def _gather_forward(outs, name):
    n = len(outs)

    def body(*refs):
        o = refs[n:2 * n]
        send_sems, recv_sems = refs[2 * n:]
        x, y, c = _mesh_pos()
        copies = []
        for j in range(n):
            for k, (px, py) in enumerate(_other_chips(x, y)):
                blk = o[j].at[4 * px + 2 * py + c]
                cp = pltpu.make_async_remote_copy(src_ref=blk, dst_ref=blk, send_sem=send_sems.at[j, k],
                                                  recv_sem=recv_sems.at[j, k], device_id=(x, y, 1 - c), device_id_type=MESH)
                cp.start()
                copies.append(cp)
        for cp in copies:
            cp.wait()

    return pl.pallas_call(
        body, name=name, in_specs=[ANY] * n, out_specs=[ANY] * n,
        out_shape=[jax.ShapeDtypeStruct(a.shape, a.dtype) for a in outs],
        input_output_aliases={i: i for i in range(n)},
        scratch_shapes=[pltpu.SemaphoreType.DMA((n, 3)), pltpu.SemaphoreType.DMA((n, 3))],
    )(*outs)


def _all_sum(p, name):
    R, C = p.shape

    def body(p_ref, o_ref, buf, send_sems, recv_sems):
        x, y, c = _mesh_pos()
        me = 4 * x + 2 * y + c
        buf[me] = p_ref[...]
        copies = []
        for rel in range(1, N_DEV):
            peer = (1 - x if rel & 4 else x, 1 - y if rel & 2 else y, 1 - c if rel & 1 else c)
            cp = pltpu.make_async_remote_copy(
                src_ref=p_ref, dst_ref=buf.at[me], send_sem=send_sems.at[rel - 1], recv_sem=recv_sems.at[rel - 1],
                device_id=peer, device_id_type=MESH)
            cp.start()
            copies.append(cp)
        for cp in copies:
            cp.wait()
        acc = buf[0]
        for dev in range(1, N_DEV):
            acc = acc + buf[dev]
        o_ref[...] = acc

    vm = pl.BlockSpec(memory_space=pltpu.VMEM)
    return pl.pallas_call(
        body, name=name, in_specs=[vm], out_specs=vm, out_shape=jax.ShapeDtypeStruct((R, C), F32),
        scratch_shapes=[pltpu.VMEM((N_DEV, R, C), F32), pltpu.SemaphoreType.DMA((N_DEV - 1,)),
                        pltpu.SemaphoreType.DMA((N_DEV - 1,))],
    )(p)


def _swap_with_sibling(gs, name):
    n = len(gs)

    def body(*refs):
        ins, outs = refs[:n], refs[n:2 * n]
        send_sems, recv_sems = refs[2 * n:]
        x, y, c = _mesh_pos()
        copies = []
        for i in range(n):
            for k in range(4):
                cp = pltpu.make_async_remote_copy(
                    src_ref=ins[i].at[2 * k + (1 - c)], dst_ref=outs[i].at[k],
                    send_sem=send_sems.at[i, k], recv_sem=recv_sems.at[i, k],
                    device_id=(x, y, 1 - c), device_id_type=MESH)
                cp.start()
                copies.append(cp)
        for cp in copies:
            cp.wait()

    return pl.pallas_call(
        body, name=name, in_specs=[ANY] * n, out_specs=[ANY] * n,
        out_shape=[jax.ShapeDtypeStruct((4,) + g.shape[1:], g.dtype) for g in gs],
        scratch_shapes=[pltpu.SemaphoreType.DMA((n, 4)), pltpu.SemaphoreType.DMA((n, 4))],
    )(*gs)


def _scatter_copies(j, src, land, send, recv, x, y, c):
    return [pltpu.make_async_remote_copy(src_ref=src.at[2 * px + py], dst_ref=land.at[k], send_sem=send.at[3 * j + k],
                                         recv_sem=recv.at[3 * j + k], device_id=(px, py, c), device_id_type=MESH)
            for k, (px, py) in enumerate(_other_chips(x, y))]


def _scatter_start(ps, name):
    n = len(ps)

    def body(*refs):
        ins, lands = refs[:n], refs[2 * n:3 * n]
        send, recv, token = refs[3 * n:]
        x, y, c = _mesh_pos()
        for j in range(n):
            for cp in _scatter_copies(j, ins[j], lands[j], send, recv, x, y, c):
                cp.start()
        token[...] = jnp.zeros_like(token)

    res = pl.pallas_call(
        body, name=name, in_specs=[HBM] * n,
        out_specs=[HBM] * (2 * n) + [SEM, SEM, pl.BlockSpec(memory_space=pltpu.VMEM)],
        out_shape=[pltpu.HBM(p.shape, p.dtype) for p in ps] + [pltpu.HBM((3,) + p.shape[1:], p.dtype) for p in ps]
        + [pltpu.SemaphoreType.DMA((3 * n,)), pltpu.SemaphoreType.DMA((3 * n,)), jax.ShapeDtypeStruct((8, 128), F32)],
        input_output_aliases={i: i for i in range(n)},
        compiler_params=pltpu.CompilerParams(has_side_effects=SPLIT_COPY),
    )(*[_hbm(p) for p in ps])
    return res[:n], res[n:2 * n], res[2 * n], res[2 * n + 1], res[2 * n + 2]


def _scatter_wait(thru, lands, send, recv, after, name):
    n = len(thru)

    def body(*refs):
        thru_refs, land_refs, send_ref, recv_ref = refs[:n], refs[n:2 * n], refs[2 * n], refs[2 * n + 1]
        x, y, c = _mesh_pos()
        for j in range(n):
            for cp in _scatter_copies(j, thru_refs[j], land_refs[j], send_ref, recv_ref, x, y, c):
                cp.wait_send()
                cp.wait_recv()

    res = pl.pallas_call(
        body, name=name, in_specs=[HBM] * (2 * n) + [SEM, SEM, ANY], out_specs=[HBM] * (2 * n),
        out_shape=[pltpu.HBM(a.shape, a.dtype) for a in list(thru) + list(lands)],
        input_output_aliases={i: i for i in range(2 * n)},
        compiler_params=pltpu.CompilerParams(has_side_effects=SPLIT_COPY),
    )(*thru, *lands, send, recv, after)
    return res[:n], res[n:]


def _row_tile(r, pref):
    if r <= pref:
        return r
    for t in range(pref - pref % 16, 0, -16):
        if r % t == 0:
            return t
    return r


def _add_sibling(g, r1, pos, name):
    _, r, D = g.shape
    tr = _row_tile(r, 512)

    def body(pos_ref, g_ref, r_ref, o_ref):
        o_ref[...] = (g_ref[...].astype(F32) + r_ref[...].astype(F32)).astype(BF16)

    return pl.pallas_call(
        body, name=name,
        grid_spec=pltpu.PrefetchScalarGridSpec(
            num_scalar_prefetch=1, grid=(4, r // tr),
            in_specs=[pl.BlockSpec((None, None, tr, D), lambda k, j, pos: (k, pos[0], j, 0)),
                      pl.BlockSpec((None, tr, D), lambda k, j, pos: (k, j, 0))],
            out_specs=pl.BlockSpec((None, tr, D), lambda k, j, pos: (k, j, 0))),
        out_shape=jax.ShapeDtypeStruct((4, r, D), BF16),
        compiler_params=_params("parallel", "parallel"),
    )(pos, g.reshape(4, 2, r, D), r1)


def _add_chips(p, r2, pos, name):
    _, r, D = p.shape
    tr = _row_tile(r, 512)

    def body(pos_ref, p_ref, r_ref, o_ref):
        acc = p_ref[...].astype(F32)
        for j in range(3):
            acc = acc + r_ref[j].astype(F32)
        o_ref[...] = acc

    return pl.pallas_call(
        body, name=name,
        grid_spec=pltpu.PrefetchScalarGridSpec(
            num_scalar_prefetch=1, grid=(r // tr,),
            in_specs=[pl.BlockSpec((None, tr, D), lambda j, pos: (pos[1], j, 0)),
                      pl.BlockSpec((3, tr, D), lambda j, pos: (0, j, 0))],
            out_specs=pl.BlockSpec((tr, D), lambda j, pos: (j, 0))),
        out_shape=jax.ShapeDtypeStruct((r, D), F32),
        compiler_params=_params("parallel"),
    )(pos, p, r2)


def _place_cols(local, me, width):
    R, w = local.shape
    return lax.dynamic_update_slice(jnp.zeros((R, width), F32), local, (0, me * w))


def _pad_rows(a, rows):
    return jnp.pad(a, ((0, rows - a.shape[0]), (0, 0)))


SUBLAYER_SHARDS = ((0, 4), (8, 9), (1, 5), (2, 6), (10, 11), (3, 7))


class _StepComm:
    def __init__(self, shards, pos):
        self.pos = pos
        self.shards = shards
        self.gathers = {}
        tok = self._start_gather(0, ())
        self._start_gather(1, tok)
        self.pending = {}

    def _start_gather(self, s, after):
        outs, thru, send, recv, token = _gather_start([self.shards[i] for i in SUBLAYER_SHARDS[s]], after,
                                                      f"gather_start_{s}")
        self.gathers[s] = (outs, thru, send, recv)
        return (token,)

    def weights(self, s, after):
        outs = _gather_wait(*self.gathers.pop(s), after, f"gather_wait_{s}")
        outs = _gather_forward(outs, f"gather_forward_{s}")
        tok = self._start_gather(s + 2, (outs[0],)) if s + 2 < len(SUBLAYER_SHARDS) else ()
        return [w.reshape(-1, w.shape[-1]) for w in outs], tok

    def grads(self, s, mats):
        g3 = [g.reshape(N_DEV, g.shape[0] // N_DEV, g.shape[1]) for g in mats]
        r1 = _swap_with_sibling(g3, f"rs_sibling_{s}")
        ps = [_add_sibling(g, r, self.pos, f"rs_add_sibling_{s}_{i}") for i, (g, r) in enumerate(zip(g3, r1))]
        thru, lands, send, recv, token = _scatter_start(ps, f"rs_start_{s}")
        self.pending[s] = (thru, lands, send, recv)
        return (token,)

    def finish(self, after):
        out = {}
        for s in sorted(self.pending, reverse=True):
            ps, lands = _scatter_wait(*self.pending[s], after, f"rs_wait_{s}")
            for i, p, r2 in zip(SUBLAYER_SHARDS[s], ps, lands):
                out[i] = _add_chips(p, r2, self.pos, f"rs_add_chips_{i}")
        return out


def kernel(x, norm_g, ffn_w_in, ffn_w_out, conv_w_pw1, conv_b_pw1, conv_w_dw, conv_b_dw, conv_norm_g, conv_w_pw2, conv_b_pw2, attn_w_qkv, attn_q_norm, attn_k_norm, attn_w_o, loss_target, m_norm_g, m_ffn_w_in, m_ffn_w_out, m_conv_w_pw1, m_conv_b_pw1, m_conv_w_dw, m_conv_b_dw, m_conv_norm_g, m_conv_w_pw2, m_conv_b_pw2, m_attn_w_qkv, m_attn_q_norm, m_attn_k_norm, m_attn_w_o, v_norm_g, v_ffn_w_in, v_ffn_w_out, v_conv_w_pw1, v_conv_b_pw1, v_conv_w_dw, v_conv_b_dw, v_conv_norm_g, v_conv_w_pw2, v_conv_b_pw2, v_attn_w_qkv, v_attn_q_norm, v_attn_k_norm, v_attn_w_o):
    T, D = x.shape[1], x.shape[2]
    xi, yi, ci = _mesh_pos()
    me = 4 * xi + 2 * yi + ci
    pos = jnp.stack([ci, 2 * xi + yi]).astype(I32)
    lf = [(l, f) for l in range(2) for f in range(2)]

    shards = ([ffn_w_in[l, f].T.astype(BF16) for l, f in lf] + [ffn_w_out[l, f].astype(BF16) for l, f in lf]
              + [conv_w_pw1[0].T.astype(BF16), conv_w_pw2[0].astype(BF16),
                 attn_w_qkv[0].T.astype(BF16), attn_w_o[0].astype(BF16)])
    comm = _StepComm(shards, pos)
    w_cols = norm_g.shape[2]
    vec = _all_sum(_pad_rows(jnp.concatenate([_place_cols(norm_g.reshape(6, w_cols), me, D),
                                              _place_cols(conv_w_dw[0], me, D)]), 40), "gather_vectors")
    vecs = {
        "norm_g": vec[0:6], "w_dw": vec[6:6 + CONV_WIDTH],
        "b_pw1": conv_b_pw1.reshape(2, D), "b_dw": conv_b_dw, "cg": conv_norm_g, "b_pw2": conv_b_pw2,
        "q_norm": attn_q_norm[0], "k_norm": attn_k_norm[0],
    }

    loss_local, dx, small, tok = _local_step(x[0], loss_target[0], vecs, comm)
    loss = lax.psum(loss_local, ("x", "y", "c"))

    mats = comm.finish(tok[0])
    hd3 = 3 * HEAD_DIM
    packed = jnp.concatenate([
        small["norm_g"], small["b_pw1"], small["w_dw"], small["b_dw"], small["cg"], small["b_pw2"],
        jnp.pad(small["q_norm"].reshape(1, hd3), ((0, 0), (0, D - hd3))),
        jnp.pad(small["k_norm"].reshape(1, hd3), ((0, 0), (0, D - hd3)))])
    red = _all_sum(_pad_rows(packed, 48), "reduce_vectors")
    col0 = me * w_cols
    grads = {
        "norm_g": lax.dynamic_slice(red[0:6], (0, col0), (6, w_cols)),
        "ffn_w_in": jnp.concatenate([mats[i].T for i in range(4)]),
        "ffn_w_out": jnp.concatenate([mats[i] for i in range(4, 8)]),
        "conv_w_pw1": mats[8].T,
        "conv_b_pw1": red[6:8].reshape(1, 2 * D),
        "conv_w_dw": lax.dynamic_slice(red[8:8 + CONV_WIDTH], (0, col0), (CONV_WIDTH, w_cols)),
        "conv_b_dw": red[39:40], "conv_norm_g": red[40:41], "conv_w_pw2": mats[9], "conv_b_pw2": red[41:42],
        "attn_w_qkv": mats[10].T,
        "attn_q_norm": red[42, :hd3].reshape(3, HEAD_DIM), "attn_k_norm": red[43, :hd3].reshape(3, HEAD_DIM),
        "attn_w_o": mats[11],
    }

    names = ["norm_g", "ffn_w_in", "ffn_w_out", "conv_w_pw1", "conv_b_pw1", "conv_w_dw", "conv_b_dw",
             "conv_norm_g", "conv_w_pw2", "conv_b_pw2", "attn_w_qkv", "attn_q_norm", "attn_k_norm", "attn_w_o"]
    ws = [norm_g, ffn_w_in, ffn_w_out, conv_w_pw1, conv_b_pw1, conv_w_dw, conv_b_dw, conv_norm_g, conv_w_pw2,
          conv_b_pw2, attn_w_qkv, attn_q_norm, attn_k_norm, attn_w_o]
    ms = [m_norm_g, m_ffn_w_in, m_ffn_w_out, m_conv_w_pw1, m_conv_b_pw1, m_conv_w_dw, m_conv_b_dw, m_conv_norm_g,
          m_conv_w_pw2, m_conv_b_pw2, m_attn_w_qkv, m_attn_q_norm, m_attn_k_norm, m_attn_w_o]
    vs = [v_norm_g, v_ffn_w_in, v_ffn_w_out, v_conv_w_pw1, v_conv_b_pw1, v_conv_w_dw, v_conv_b_dw, v_conv_norm_g,
          v_conv_w_pw2, v_conv_b_pw2, v_attn_w_qkv, v_attn_q_norm, v_attn_k_norm, v_attn_w_o]
    g_out, d_out, m_out, v_out = [], [], [], []
    for name, w, m, v in zip(names, ws, ms, vs):
        shape2 = (-1, w.shape[-1])
        g2 = grads[name].reshape(shape2)
        delta, new_m, new_v = _adamw(w.reshape(shape2), g2, m.reshape(shape2), v.reshape(shape2), f"adamw_{name}")
        g_out.append(g2.reshape(w.shape))
        d_out.append(delta.reshape(w.shape))
        m_out.append(new_m.reshape(w.shape))
        v_out.append(new_v.reshape(w.shape))
    return (loss, dx.reshape(x.shape), *g_out, *d_out, *m_out, *v_out)
```

```python
import functools

import jax
import jax.numpy as jnp
from jax import lax
from jax.experimental import pallas as pl
from jax.experimental.pallas import tpu as pltpu

F32 = jnp.float32
BF16 = jnp.bfloat16
I32 = jnp.int32

NORM_EPS = 1e-6
LANES = 128
SUBLANES = 8
HEAD_DIM = 128
ATTN_BLOCK = 128
ATTN_DILATIONS = (1, 4, 16)
CONV_WIDTH = 31
CONV_HALO = 32
CONV_CHUNK = 16
STREAM_TILE = 512
ADAM_LR, ADAM_B1, ADAM_B2, ADAM_EPS, ADAM_WD, ADAM_STEP = 0.001, 0.9, 0.999, 1e-08, 0.01, 10
N_DEV = 8
NEG_BIG = -1e30
V7X_VMEM_BYTES = 64 * 1024 * 1024
VMEM_LIMIT = V7X_VMEM_BYTES - 8 * 1024 * 1024
MESH = pl.DeviceIdType.MESH
ANY = pl.BlockSpec(memory_space=pl.ANY)


def _params(*sem):
    return pltpu.CompilerParams(dimension_semantics=sem or None, vmem_limit_bytes=VMEM_LIMIT)


def _dot_nn(a, b):
    return lax.dot_general(a, b, (((1,), (0,)), ((), ())), preferred_element_type=F32)


def _dot_nt(a, b):
    return lax.dot_general(a, b, (((1,), (1,)), ((), ())), preferred_element_type=F32)


def _dot_tn(a, b):
    return lax.dot_general(a, b, (((0,), (0,)), ((), ())), preferred_element_type=F32)


def _sigmoid(x):
    return 1.0 / (1.0 + jnp.exp(-x))


def _tile(n, pref):
    if n <= pref:
        return n
    for t in range(pref - pref % 128, 0, -128):
        if n % t == 0:
            return t
    return n


def _to_lane_chunks(dst, val):
    for c in range(dst.shape[0]):
        dst[c] = val[:, c * LANES:(c + 1) * LANES]


def _rows8(v):
    tm, c = v.shape
    return v.reshape(tm // 8, 8, c).sum(axis=0)


def _rmsnorm(x, g, name):
    T, D = x.shape
    tm = min(T, 512)

    def body(x_ref, g_ref, h_ref):
        xv = x_ref[...]
        r = lax.rsqrt(jnp.mean(xv * xv, axis=-1, keepdims=True) + NORM_EPS)
        h_ref[...] = (xv * r * g_ref[...]).astype(BF16)

    return pl.pallas_call(
        body, name=name, grid=(T // tm,),
        in_specs=[pl.BlockSpec((tm, D), lambda i: (i, 0)), pl.BlockSpec((1, D), lambda i: (0, 0))],
        out_specs=pl.BlockSpec((tm, D), lambda i: (i, 0)),
        out_shape=jax.ShapeDtypeStruct((T, D), BF16),
        compiler_params=_params("parallel"),
    )(x, g)


def _rmsnorm_streams(x, g, name):
    T, D = x.shape
    tm = min(T, STREAM_TILE)
    dils = [d for d in ATTN_DILATIONS if d > 1]

    def body(x_ref, g_ref, h_ref, *rest):
        outs, hs = rest[:-1], rest[-1]
        xv = x_ref[...]
        r = lax.rsqrt(jnp.mean(xv * xv, axis=-1, keepdims=True) + NORM_EPS)
        hf = xv * r * g_ref[...]
        h_ref[...] = hf.astype(BF16)
        _to_lane_chunks(hs, hf)
        for d, o_ref in zip(dils, outs):
            for s in range(d):
                for c in range(D // LANES):
                    o_ref[s, :, c * LANES:(c + 1) * LANES] = hs.at[c][pl.ds(s, tm // d, stride=d), :].astype(BF16)

    res = pl.pallas_call(
        body, name=name, grid=(T // tm,),
        in_specs=[pl.BlockSpec((tm, D), lambda i: (i, 0)), pl.BlockSpec((1, D), lambda i: (0, 0))],
        out_specs=[pl.BlockSpec((tm, D), lambda i: (i, 0))]
        + [pl.BlockSpec((d, tm // d, D), lambda i: (0, i, 0)) for d in dils],
        out_shape=[jax.ShapeDtypeStruct((T, D), BF16)] + [jax.ShapeDtypeStruct((d, T // d, D), BF16) for d in dils],
        scratch_shapes=[pltpu.VMEM((D // LANES, tm, LANES), F32)],
        compiler_params=_params("parallel"),
    )(x, g)
    return [res[0]] + [a.reshape(T, D) for a in res[1:]]


def _gated_in(h, wt, bias, act, u_dtype, name, after=()):
    T, D = h.shape
    F = wt.shape[0] // 2
    tn = _tile(F, 1408)
    tm = min(T, 512)
    nn = F // tn

    def body(*refs):
        z_ref, u_ref = refs[-2:]
        if bias is None:
            h_ref, w0_ref, w1_ref = refs[:3]
        else:
            h_ref, w0_ref, w1_ref, b_ref = refs[:4]
        hv = h_ref[...]
        z0 = _dot_nt(hv, w0_ref[...])
        z1 = _dot_nt(hv, w1_ref[...])
        if bias is not None:
            z0 = z0 + b_ref[0:1, :]
            z1 = z1 + b_ref[1:2, :]
        z_ref[0] = z0.astype(BF16)
        z_ref[1] = z1.astype(BF16)
        if act == "swiglu":
            u = z0 * _sigmoid(z0) * z1
        else:
            u = z0 * _sigmoid(z1)
        u_ref[...] = u.astype(u_dtype)

    in_specs = [pl.BlockSpec((tm, D), lambda n, m: (m, 0)),
                pl.BlockSpec((tn, D), lambda n, m: (n, 0)),
                pl.BlockSpec((tn, D), lambda n, m: (n + nn, 0))]
    args = [h, wt, wt]
    if bias is not None:
        in_specs.append(pl.BlockSpec((2, tn), lambda n, m: (0, n)))
        args.append(bias)
    in_specs += [ANY] * len(after)
    args += list(after)
    return pl.pallas_call(
        body, name=name, grid=(nn, T // tm), in_specs=in_specs,
        out_specs=[pl.BlockSpec((2, tm, tn), lambda n, m: (0, m, n)), pl.BlockSpec((tm, tn), lambda n, m: (m, n))],
        out_shape=[jax.ShapeDtypeStruct((2, T, F), BF16), jax.ShapeDtypeStruct((T, F), u_dtype)],
        compiler_params=_params("parallel", "parallel"),
    )(*args)


def _swiglu_bwd(dxo, wout, z3, scale, name, after=()):
    T, D = dxo.shape
    F = wout.shape[0]
    tn = _tile(F, 1408)
    tm = min(T, 512)

    def body(d_ref, w_ref, z_ref, *rest):
        dz_ref = rest[-1]
        dy = (d_ref[...] * scale).astype(BF16)
        da = _dot_nt(dy, w_ref[...])
        gate = z_ref[0].astype(F32)
        up = z_ref[1].astype(F32)
        sg = _sigmoid(gate)
        dz_ref[0] = (da * up * (sg * (1.0 + gate * (1.0 - sg)))).astype(BF16)
        dz_ref[1] = (da * gate * sg).astype(BF16)

    return pl.pallas_call(
        body, name=name, grid=(F // tn, T // tm),
        in_specs=[pl.BlockSpec((tm, D), lambda n, m: (m, 0)),
                  pl.BlockSpec((tn, D), lambda n, m: (n, 0)),
                  pl.BlockSpec((2, tm, tn), lambda n, m: (0, m, n))] + [ANY] * len(after),
        out_specs=pl.BlockSpec((2, tm, tn), lambda n, m: (0, m, n)),
        out_shape=jax.ShapeDtypeStruct((2, T, F), BF16),
        compiler_params=_params("parallel", "parallel"),
    )(dxo, wout, z3, *after)


def _mm_nt(a, wt, out_dtype, name, *, w_row0=0, n_rows=None, streams=1, after=()):
    T, K = a.shape
    N = wt.shape[0] if n_rows is None else n_rows
    L = T // streams
    tn = _tile(N, 1152)
    tm = min(L, 512)
    mps = L // tm
    npn = N // tn
    assert w_row0 % tn == 0
    wb0 = w_row0 // tn

    def body(a_ref, w_ref, *rest):
        o_ref = rest[-1]
        o_ref[...] = _dot_nt(a_ref[...].astype(BF16), w_ref[...]).astype(out_dtype)

    return pl.pallas_call(
        body, name=name, grid=(npn, T // tm),
        in_specs=[pl.BlockSpec((tm, K), lambda n, m: (m, 0)), pl.BlockSpec((tn, K), lambda n, m: (wb0 + n, 0))]
        + [ANY] * len(after),
        out_specs=pl.BlockSpec((tm, tn), lambda n, m: (m % mps, (m // mps) * npn + n)),
        out_shape=jax.ShapeDtypeStruct((L, streams * N), out_dtype),
        compiler_params=_params("parallel", "parallel"),
    )(a, wt, *after)


def _mm_tn(a, b, b_scale, name, *, streams=1):
    if a.ndim == 3:
        J, T, F = a.shape
    elif streams > 1:
        J, T, F = 1, a.shape[0] * streams, a.shape[1] // streams
    else:
        (T, F), J = a.shape, 1
    N = b.shape[1]
    tmm = _tile(F, 1408)
    tk = min(T // streams, 1024)
    npj = F // tmm
    nk = T // tk
    kps = nk // streams

    def body(a_ref, b_ref, o_ref, acc_ref):
        k = pl.program_id(1)
        bv = b_ref[...]
        if b_scale != 1.0:
            bv = bv * b_scale
        part = _dot_tn(a_ref[...].astype(BF16), bv.astype(BF16))

        @pl.when(k == 0)
        def _():
            acc_ref[...] = part

        @pl.when(k > 0)
        def _():
            acc_ref[...] += part

        @pl.when(k == nk - 1)
        def _():
            o_ref[...] = acc_ref[...].astype(BF16)

    if a.ndim == 3:
        a_spec = pl.BlockSpec((None, tk, tmm), lambda i, k: (i // npj, k, i % npj))
    elif streams > 1:
        a_spec = pl.BlockSpec((tk, tmm), lambda i, k: (k % kps, (k // kps) * npj + i))
    else:
        a_spec = pl.BlockSpec((tk, tmm), lambda i, k: (k, i))
    return pl.pallas_call(
        body, name=name, grid=(J * npj, nk),
        in_specs=[a_spec, pl.BlockSpec((tk, N), lambda i, k: (k, 0))],
        out_specs=pl.BlockSpec((tmm, N), lambda i, k: (i, 0)),
        out_shape=jax.ShapeDtypeStruct((J * F, N), BF16),
        scratch_shapes=[pltpu.VMEM((tmm, N), F32)],
        compiler_params=_params("parallel", "arbitrary"),
    )(a, b)


def _rows_mm(a, b, mode, name, *, x=None, scale=1.0, bias=None, tgt=None, g=None, dxo=None,
             b_row0=0, streams=1, extras=()):
    if a.ndim == 3:
        J, T, F = a.shape
    elif streams > 1:
        J, T, F = 1, a.shape[0] * streams, a.shape[1] // streams
    else:
        (T, F), J = a.shape, 1
    D = b.shape[1]
    tk = _tile(F, 2816)
    kpj = F // tk
    nk = J * kpj
    tm = min(T // streams, STREAM_TILE)
    nm = T // tm
    mps = nm // streams
    assert b_row0 % tk == 0
    kb0 = b_row0 // tk
    n_ex = len(extras)

    def body(*refs):
        refs = list(refs)
        a_ref, b_ref = refs[:2]
        rest = refs[2:]
        if n_ex:
            tmp_ref = rest.pop()
        acc_ref = rest.pop()
        ex_refs, rest = rest[:n_ex], rest[n_ex:]
        m, k = pl.program_id(0), pl.program_id(1)
        part = _dot_nn(a_ref[...], b_ref[...])

        @pl.when(k == 0)
        def _():
            acc_ref[...] = part

        @pl.when(k > 0)
        def _():
            acc_ref[...] += part

        @pl.when(k == nk - 1)
        def _():
            acc = acc_ref[...]
            for (d, _), e_ref in zip(extras, ex_refs):
                for s in range(d):
                    for c in range(D // LANES):
                        tmp_ref.at[c][pl.ds(s, tm // d, stride=d), :] = e_ref[s, :, c * LANES:(c + 1) * LANES]
                acc = acc + jnp.concatenate([tmp_ref[c] for c in range(D // LANES)], axis=1)
            if mode == "plain":
                (o_ref,) = rest
                o_ref[...] = acc
            elif mode == "res":
                if bias is None:
                    x_ref, o_ref = rest
                else:
                    x_ref, bias_ref, o_ref = rest
                    acc = acc + bias_ref[...]
                o_ref[...] = x_ref[...] + scale * acc
            elif mode == "loss":
                x_ref, t_ref, o_ref, l_ref = rest
                e = x_ref[...] + scale * acc - t_ref[...]
                o_ref[...] = e / D

                @pl.when(m == 0)
                def _():
                    l_ref[...] = jnp.zeros_like(l_ref)

                l_ref[...] += _rows8(e * e)
            else:
                x_ref, g_ref, d_ref, o_ref, dg_ref = rest
                xv = x_ref[...]
                r = lax.rsqrt(jnp.mean(xv * xv, axis=-1, keepdims=True) + NORM_EPS)
                xh = xv * r
                dxh = acc * g_ref[...]
                o_ref[...] = d_ref[...] + r * (dxh - xh * jnp.mean(dxh * xh, axis=-1, keepdims=True))

                @pl.when(m == 0)
                def _():
                    dg_ref[...] = jnp.zeros_like(dg_ref)

                dg_ref[...] += _rows8(acc * xh)

    if a.ndim == 3:
        a_spec = pl.BlockSpec((None, tm, tk), lambda m, k: (k // kpj, m, k % kpj))
    elif streams > 1:
        a_spec = pl.BlockSpec((tm, tk), lambda m, k: (m % mps, (m // mps) * kpj + k))
    else:
        a_spec = pl.BlockSpec((tm, tk), lambda m, k: (m, k))
    row = pl.BlockSpec((tm, D), lambda m, k: (m, 0))
    vec = pl.BlockSpec((1, D), lambda m, k: (0, 0))
    part8 = pl.BlockSpec((8, D), lambda m, k: (0, 0))
    in_specs = [a_spec, pl.BlockSpec((tk, D), lambda m, k: (kb0 + k, 0))]
    args = [a, b]
    for d, e in extras:
        in_specs.append(pl.BlockSpec((d, tm // d, D), lambda m, k: (0, m, 0)))
        args.append(e.reshape(d, T // d, D))
    full = jax.ShapeDtypeStruct((T, D), F32)
    small = jax.ShapeDtypeStruct((8, D), F32)
    if mode == "plain":
        out_specs, out_shape = row, full
    elif mode == "res":
        in_specs.append(row)
        args.append(x)
        if bias is not None:
            in_specs.append(vec)
            args.append(bias)
        out_specs, out_shape = row, full
    elif mode == "loss":
        in_specs += [row, row]
        args += [x, tgt]
        out_specs, out_shape = [row, part8], [full, small]
    else:
        in_specs += [row, vec, row]
        args += [x, g, dxo]
        out_specs, out_shape = [row, part8], [full, small]
    return pl.pallas_call(
        body, name=name, grid=(nm, nk), in_specs=in_specs, out_specs=out_specs, out_shape=out_shape,
        scratch_shapes=[pltpu.VMEM((tm, D), F32)] + ([pltpu.VMEM((D // LANES, tm, LANES), F32)] if n_ex else []),
        compiler_params=_params("arbitrary", "arbitrary"),
    )(*args)


def _sublane_phases(sh):
    rows = sh.shape[1] - SUBLANES
    for p in range(1, SUBLANES):
        sh[p, pl.ds(0, rows), :] = sh[0, pl.ds(p, rows), :]


def _shifted(sh, offset, j, rows=SUBLANES):
    start = pl.multiple_of(j * rows + (offset - offset % SUBLANES), SUBLANES)
    return sh[offset % SUBLANES, pl.ds(start, rows), :]


def _conv_taps(wb, sh, offsets, j, init):
    groups = CONV_CHUNK // SUBLANES
    acc = init
    for k, off in enumerate(offsets):
        rows = _shifted(sh, off, j, CONV_CHUNK).reshape(groups, SUBLANES, -1)
        acc = acc + (wb[k] * rows).reshape(CONV_CHUNK, -1)
    return acc


def _broadcast_rows(dst, src):
    for r in range(dst.shape[0]):
        dst[r] = jnp.zeros(dst.shape[1:], F32) + src[r:r + 1, :]


def _dwconv_fwd(u, w, b_dw, cg, name):
    T, C = u.shape
    tm = min(T, 256)
    hb = tm // CONV_HALO
    pad = CONV_HALO - (CONV_WIDTH - 1)

    def body(u_ref, halo_ref, w_ref, b_ref, g_ref, v_ref, s_ref, sh, wb):
        i = pl.program_id(0)
        sh[0, pl.ds(0, CONV_HALO), :] = jnp.where(i > 0, halo_ref[...], 0.0)
        sh[0, pl.ds(CONV_HALO, tm), :] = u_ref[...]
        _sublane_phases(sh)
        _broadcast_rows(wb, w_ref)
        bias = jnp.zeros((CONV_CHUNK, C), F32) + b_ref[...]

        def chunk(j, carry):
            acc = _conv_taps(wb, sh, [pad + k for k in range(CONV_WIDTH)], j, bias)
            v_ref[pl.ds(pl.multiple_of(j * CONV_CHUNK, CONV_CHUNK), CONV_CHUNK), :] = acc
            return carry

        lax.fori_loop(0, tm // CONV_CHUNK, chunk, 0)
        acc = v_ref[...]
        r = lax.rsqrt(jnp.mean(acc * acc, axis=-1, keepdims=True) + NORM_EPS)
        n = acc * r * g_ref[...]
        s_ref[...] = (n * _sigmoid(n)).astype(BF16)

    row = pl.BlockSpec((tm, C), lambda i: (i, 0))
    vec = pl.BlockSpec((1, C), lambda i: (0, 0))
    return pl.pallas_call(
        body, name=name, grid=(T // tm,),
        in_specs=[row, pl.BlockSpec((CONV_HALO, C), lambda i: (jnp.maximum(i * hb - 1, 0), 0)),
                  pl.BlockSpec((CONV_WIDTH, C), lambda i: (0, 0)), vec, vec],
        out_specs=[row, row],
        out_shape=[jax.ShapeDtypeStruct((T, C), F32), jax.ShapeDtypeStruct((T, C), BF16)],
        scratch_shapes=[pltpu.VMEM((SUBLANES, CONV_HALO + tm, C), F32), pltpu.VMEM((CONV_WIDTH, SUBLANES, C), F32)],
        compiler_params=_params("parallel"),
    )(u, u, w, b_dw, cg)


def _conv_post_bwd(ds, v, cg, dxo, name):
    T, C = v.shape
    tm = min(T, 512)

    def body(ds_ref, v_ref, g_ref, d_ref, dv_ref, dcg_ref, dbdw_ref, dbpw2_ref):
        @pl.when(pl.program_id(0) == 0)
        def _():
            dcg_ref[...] = jnp.zeros_like(dcg_ref)
            dbdw_ref[...] = jnp.zeros_like(dbdw_ref)
            dbpw2_ref[...] = jnp.zeros_like(dbpw2_ref)

        vv = v_ref[...]
        r = lax.rsqrt(jnp.mean(vv * vv, axis=-1, keepdims=True) + NORM_EPS)
        vh = vv * r
        n = vh * g_ref[...]
        sg = _sigmoid(n)
        dn = ds_ref[...] * (sg * (1.0 + n * (1.0 - sg)))
        dvh = dn * g_ref[...]
        dv = r * (dvh - vh * jnp.mean(dvh * vh, axis=-1, keepdims=True))
        dv_ref[...] = dv
        dcg_ref[...] += _rows8(dn * vh)
        dbdw_ref[...] += _rows8(dv)
        dbpw2_ref[...] += _rows8(d_ref[...])

    row = pl.BlockSpec((tm, C), lambda i: (i, 0))
    part8 = pl.BlockSpec((8, C), lambda i: (0, 0))
    small = jax.ShapeDtypeStruct((8, C), F32)
    return pl.pallas_call(
        body, name=name, grid=(T // tm,),
        in_specs=[row, row, pl.BlockSpec((1, C), lambda i: (0, 0)), row],
        out_specs=[row, part8, part8, part8],
        out_shape=[jax.ShapeDtypeStruct((T, C), F32), small, small, small],
        compiler_params=_params("arbitrary"),
    )(ds, v, cg, dxo)


def _dwconv_bwd(dv, u, w, z3, name):
    T, C = u.shape
    tm = min(T, 256)
    hb = tm // CONV_HALO
    nt = T // tm
    pad = CONV_HALO - (CONV_WIDTH - 1)

    taps_at_once = 4

    def body(dv_ref, dvn_ref, u_ref, up_ref, w_ref, z_ref, dz_ref, dw_ref, db_ref, dvsh, ush, wb, du_ref, dwacc):
        i = pl.program_id(0)

        @pl.when(i == 0)
        def _():
            dwacc[...] = jnp.zeros_like(dwacc)
            db_ref[...] = jnp.zeros_like(db_ref)

        dvsh[0, pl.ds(0, tm), :] = dv_ref[...]
        dvsh[0, pl.ds(tm, CONV_HALO), :] = jnp.where(i < nt - 1, dvn_ref[...], 0.0)
        ush[0, pl.ds(0, CONV_HALO), :] = jnp.where(i > 0, up_ref[...], 0.0)
        ush[0, pl.ds(CONV_HALO, tm), :] = u_ref[...]
        _sublane_phases(dvsh)
        _sublane_phases(ush)
        _broadcast_rows(wb, w_ref)

        def du_chunk(j, carry):
            acc = _conv_taps(wb, dvsh, [CONV_WIDTH - 1 - k for k in range(CONV_WIDTH)], j,
                             jnp.zeros((CONV_CHUNK, C), F32))
            du_ref[pl.ds(pl.multiple_of(j * CONV_CHUNK, CONV_CHUNK), CONV_CHUNK), :] = acc
            return carry

        lax.fori_loop(0, tm // CONV_CHUNK, du_chunk, 0)

        for k0 in range(0, CONV_WIDTH, taps_at_once):
            taps = range(k0, min(k0 + taps_at_once, CONV_WIDTH))

            def dw_chunk(j, accs, taps=taps):
                dvc = dv_ref[pl.ds(pl.multiple_of(j * SUBLANES, SUBLANES), SUBLANES), :]
                return tuple(acc + dvc * _shifted(ush, pad + k, j) for acc, k in zip(accs, taps))

            accs = lax.fori_loop(0, tm // SUBLANES, dw_chunk, tuple(jnp.zeros((SUBLANES, C), F32) for _ in taps),
                                 unroll=4)
            for k, acc in zip(taps, accs):
                dwacc[k] += acc

        du = du_ref[...]
        a = z_ref[0].astype(F32)
        gate = z_ref[1].astype(F32)
        sg = _sigmoid(gate)
        da = du * sg
        dgate = du * a * sg * (1.0 - sg)
        dz_ref[0] = da.astype(BF16)
        dz_ref[1] = dgate.astype(BF16)
        db_ref[0:8, :] += _rows8(da)
        db_ref[8:16, :] += _rows8(dgate)

        @pl.when(i == nt - 1)
        def _():
            dw_ref[...] = jnp.zeros_like(dw_ref)
            for k in range(CONV_WIDTH):
                dw_ref[k:k + 1, :] = jnp.sum(dwacc[k], axis=0, keepdims=True)

    row = pl.BlockSpec((tm, C), lambda i: (i, 0))
    last_halo = T // CONV_HALO - 1
    return pl.pallas_call(
        body, name=name, grid=(nt,),
        in_specs=[row, pl.BlockSpec((CONV_HALO, C), lambda i: (jnp.minimum((i + 1) * hb, last_halo), 0)),
                  row, pl.BlockSpec((CONV_HALO, C), lambda i: (jnp.maximum(i * hb - 1, 0), 0)),
                  pl.BlockSpec((CONV_WIDTH, C), lambda i: (0, 0)),
                  pl.BlockSpec((2, tm, C), lambda i: (0, i, 0))],
        out_specs=[pl.BlockSpec((2, tm, C), lambda i: (0, i, 0)),
                   pl.BlockSpec((CONV_HALO, C), lambda i: (0, 0)),
                   pl.BlockSpec((16, C), lambda i: (0, 0))],
        out_shape=[jax.ShapeDtypeStruct((2, T, C), BF16), jax.ShapeDtypeStruct((CONV_HALO, C), F32),
                   jax.ShapeDtypeStruct((16, C), F32)],
        scratch_shapes=[pltpu.VMEM((SUBLANES, tm + CONV_HALO, C), F32), pltpu.VMEM((SUBLANES, CONV_HALO + tm, C), F32),
                        pltpu.VMEM((CONV_WIDTH, SUBLANES, C), F32), pltpu.VMEM((tm, C), F32),
                        pltpu.VMEM((CONV_WIDTH, SUBLANES, C), F32)],
        compiler_params=_params("arbitrary"),
    )(dv, dv, u, u, w, z3)


def _head_norm(raw, gain):
    xv = raw.astype(F32)
    r = lax.rsqrt(jnp.mean(xv * xv, axis=-1, keepdims=True) + NORM_EPS)
    xh = xv * r
    return (xh * gain).astype(BF16), xh, r


def _band_mask(n):
    row = lax.broadcasted_iota(I32, (ATTN_BLOCK, 2 * ATTN_BLOCK), 0)
    col = lax.broadcasted_iota(I32, (ATTN_BLOCK, 2 * ATTN_BLOCK), 1)
    return (col >= row) & (col <= row + ATTN_BLOCK) & ((col >= ATTN_BLOCK) | (n > 0))


def _attn_fwd(qkv, qg, kg, g, name):
    d = ATTN_DILATIONS[g]
    L = qkv.shape[0]
    HD = qkv.shape[1] // (3 * d)
    H = HD // HEAD_DIM
    nb = L // ATTN_BLOCK
    scale = HEAD_DIM ** -0.5

    def body(q_ref, kp_ref, kc_ref, vp_ref, vc_ref, qg_ref, kg_ref, o_ref, lse_ref, kk, vv):
        mask = _band_mask(pl.program_id(1))
        lane = lax.broadcasted_iota(I32, (ATTN_BLOCK, 128), 1)
        gq = qg_ref[g:g + 1, :]
        gk = kg_ref[g:g + 1, :]
        vv[0:ATTN_BLOCK, :] = vp_ref[...]
        vv[ATTN_BLOCK:, :] = vc_ref[...]
        lse_tile = jnp.zeros((ATTN_BLOCK, 128), F32)
        for h in range(H):
            sl = slice(h * HEAD_DIM, (h + 1) * HEAD_DIM)
            q = _head_norm(q_ref[:, sl], gq)[0]
            kk[0:ATTN_BLOCK, sl] = _head_norm(kp_ref[:, sl], gk)[0]
            kk[ATTN_BLOCK:, sl] = _head_norm(kc_ref[:, sl], gk)[0]
            s = jnp.where(mask, _dot_nt(q, kk[:, sl]) * scale, NEG_BIG)
            m = jnp.max(s, axis=-1, keepdims=True)
            p = jnp.exp(s - m)
            l = jnp.sum(p, axis=-1, keepdims=True)
            o_ref[:, sl] = _dot_nn(p.astype(BF16), vv[:, sl]) / l
            lse_tile = jnp.where(lane == h, m + jnp.log(l), lse_tile)
        lse_ref[...] = lse_tile

    def blk(which, prev):
        if prev:
            return pl.BlockSpec((ATTN_BLOCK, HD), lambda r, n: (jnp.maximum(n - 1, 0), r * 3 + which))
        return pl.BlockSpec((ATTN_BLOCK, HD), lambda r, n: (n, r * 3 + which))

    gains = pl.BlockSpec((3, HEAD_DIM), lambda r, n: (0, 0))
    return pl.pallas_call(
        body, name=name, grid=(d, nb),
        in_specs=[blk(0, False), blk(1, True), blk(1, False), blk(2, True), blk(2, False), gains, gains],
        out_specs=[pl.BlockSpec((ATTN_BLOCK, HD), lambda r, n: (n, r)),
                   pl.BlockSpec((ATTN_BLOCK, 128), lambda r, n: (n, r))],
        out_shape=[jax.ShapeDtypeStruct((L, d * HD), F32), jax.ShapeDtypeStruct((L, d * 128), F32)],
        scratch_shapes=[pltpu.VMEM((2 * ATTN_BLOCK, HD), BF16), pltpu.VMEM((2 * ATTN_BLOCK, HD), BF16)],
        compiler_params=_params("parallel", "parallel"),
    )(qkv, qkv, qkv, qkv, qkv, qg, kg)


def _attn_merge(os, lses, name):
    T = os[0].shape[0]
    HD = os[0].shape[1]
    H = HD // HEAD_DIM
    tm = min(T, STREAM_TILE)
    dils = ATTN_DILATIONS[1:]

    def body(o0, o1, o2, l0, l1, l2, ob_ref, of_ref, lse_ref, n1, n2, m1, m2):
        for d, src, dst, w in ((dils[0], o1, n1, HD), (dils[1], o2, n2, HD), (dils[0], l1, m1, 128), (dils[1], l2, m2, 128)):
            for s in range(d):
                for c in range(w // LANES):
                    dst.at[c][pl.ds(s, tm // d, stride=d), :] = src[:, s * w + c * LANES:s * w + (c + 1) * LANES]
        ls = [l0[...], m1[0], m2[0]]
        m = jnp.maximum(jnp.maximum(ls[0], ls[1]), ls[2])
        tot = m + jnp.log(jnp.exp(ls[0] - m) + jnp.exp(ls[1] - m) + jnp.exp(ls[2] - m))
        lse_ref[...] = tot
        ws = [jnp.exp(l - tot) for l in ls]
        for h in range(H):
            sl = slice(h * HEAD_DIM, (h + 1) * HEAD_DIM)
            out = (ws[0][:, h:h + 1] * o0[:, sl] + ws[1][:, h:h + 1] * n1[h]) + ws[2][:, h:h + 1] * n2[h]
            of_ref[:, sl] = out
            ob_ref[:, sl] = out.astype(BF16)

    def wide(d, w):
        return pl.BlockSpec((tm // d, d * w), lambda i: (i, 0))

    row = pl.BlockSpec((tm, HD), lambda i: (i, 0))
    nar = pl.BlockSpec((tm, 128), lambda i: (i, 0))
    return pl.pallas_call(
        body, name=name, grid=(T // tm,),
        in_specs=[row, wide(dils[0], HD), wide(dils[1], HD), nar, wide(dils[0], 128), wide(dils[1], 128)],
        out_specs=[row, row, nar],
        out_shape=[jax.ShapeDtypeStruct((T, HD), BF16), jax.ShapeDtypeStruct((T, HD), F32),
                   jax.ShapeDtypeStruct((T, 128), F32)],
        scratch_shapes=[pltpu.VMEM((H, tm, HEAD_DIM), F32), pltpu.VMEM((H, tm, HEAD_DIM), F32),
                        pltpu.VMEM((1, tm, 128), F32), pltpu.VMEM((1, tm, 128), F32)],
        compiler_params=_params("parallel"),
    )(*os, *lses)


def _attn_delta(out, dout, lse, name):
    T, HD = out.shape
    H = HD // HEAD_DIM
    tm = min(T, STREAM_TILE)
    dils = ATTN_DILATIONS[1:]

    def body(o_ref, d_ref, l_ref, dl_ref, *rest):
        dch = rest[-1]
        lane = lax.broadcasted_iota(I32, (tm, 128), 1)
        tile = jnp.zeros((tm, 128), F32)
        for h in range(H):
            sl = slice(h * HEAD_DIM, (h + 1) * HEAD_DIM)
            dch[h] = d_ref[:, sl]
            tile = jnp.where(lane == h, jnp.sum(o_ref[:, sl] * d_ref[:, sl], axis=-1, keepdims=True), tile)
        dl_ref[...] = tile
        for i, d in enumerate(dils):
            dw_ref, lw_ref, dlw_ref = rest[3 * i:3 * i + 3]
            for s in range(d):
                rows = pl.ds(s, tm // d, stride=d)
                for h in range(H):
                    dw_ref[:, s * HD + h * HEAD_DIM:s * HD + (h + 1) * HEAD_DIM] = dch.at[h][rows, :].astype(BF16)
                lw_ref[:, s * 128:(s + 1) * 128] = l_ref[rows, :]
                dlw_ref[:, s * 128:(s + 1) * 128] = dl_ref[rows, :]

    row = pl.BlockSpec((tm, HD), lambda i: (i, 0))
    nar = pl.BlockSpec((tm, 128), lambda i: (i, 0))
    out_specs, out_shape = [nar], [jax.ShapeDtypeStruct((T, 128), F32)]
    for d in dils:
        out_specs += [pl.BlockSpec((tm // d, d * w), lambda i: (i, 0)) for w in (HD, 128, 128)]
        out_shape += [jax.ShapeDtypeStruct((T // d, d * HD), BF16), jax.ShapeDtypeStruct((T // d, d * 128), F32),
                      jax.ShapeDtypeStruct((T // d, d * 128), F32)]
    return pl.pallas_call(
        body, name=name, grid=(T // tm,), in_specs=[row, row, nar], out_specs=out_specs, out_shape=out_shape,
        scratch_shapes=[pltpu.VMEM((H, tm, HEAD_DIM), F32)],
        compiler_params=_params("parallel"),
    )(out, dout, lse)


def _attn_bwd(qkv, dout, lse, delta, qg, kg, g, name):
    d = ATTN_DILATIONS[g]
    L = qkv.shape[0]
    HD = qkv.shape[1] // (3 * d)
    H = HD // HEAD_DIM
    nb = L // ATTN_BLOCK
    scale = HEAD_DIM ** -0.5

    def body(*refs):
        (q_ref, kp_ref, kc_ref, vp_ref, vc_ref, do_ref, lse_ref, dl_ref, qg_ref, kg_ref,
         out_ref, gg_ref, dq_a, dk_a, dv_a, dq_b, dk_b, dv_b, kk, vv) = refs
        r_, n = pl.program_id(0), pl.program_id(1)
        gq = qg_ref[g:g + 1, :]
        gk = kg_ref[g:g + 1, :]

        @pl.when((r_ == 0) & (n == 0))
        def _():
            gg_ref[...] = jnp.zeros_like(gg_ref)

        @pl.when(n == 0)
        def _():
            dk_a[...] = jnp.zeros_like(dk_a)
            dv_a[...] = jnp.zeros_like(dv_a)

        @pl.when(n < nb)
        def _():
            mask = _band_mask(n)
            vv[0:ATTN_BLOCK, :] = vp_ref[...]
            vv[ATTN_BLOCK:, :] = vc_ref[...]
            gq_sum = jnp.zeros((1, HEAD_DIM), F32)
            for h in range(H):
                sl = slice(h * HEAD_DIM, (h + 1) * HEAD_DIM)
                qn, qh, rq = _head_norm(q_ref[:, sl], gq)
                kk[0:ATTN_BLOCK, sl] = _head_norm(kp_ref[:, sl], gk)[0]
                kk[ATTN_BLOCK:, sl] = _head_norm(kc_ref[:, sl], gk)[0]
                kn = kk[:, sl]
                p = jnp.where(mask, jnp.exp(_dot_nt(qn, kn) * scale - lse_ref[:, h:h + 1]), 0.0)
                do = do_ref[:, sl].astype(BF16)
                ds = (p * (_dot_nt(do, vv[:, sl]) - dl_ref[:, h:h + 1]) * scale).astype(BF16)
                dqn = _dot_nn(ds, kn)
                dqh = dqn * gq
                dq_b[:, sl] = rq * (dqh - qh * jnp.mean(dqh * qh, axis=-1, keepdims=True))
                gq_sum = gq_sum + jnp.sum(dqn * qh, axis=0, keepdims=True)
                dkn = _dot_tn(ds, qn)
                dvn = _dot_tn(p.astype(BF16), do)
                dk_a[:, sl] += dkn[0:ATTN_BLOCK]
                dv_a[:, sl] += dvn[0:ATTN_BLOCK]
                dk_b[:, sl] = dkn[ATTN_BLOCK:]
                dv_b[:, sl] = dvn[ATTN_BLOCK:]
            gg_ref[0:1, :] += gq_sum

        @pl.when(n > 0)
        def _():
            out_ref[:, 0:HD] = dq_a[...].astype(BF16)
            gk_sum = jnp.zeros((1, HEAD_DIM), F32)
            for h in range(H):
                sl = slice(h * HEAD_DIM, (h + 1) * HEAD_DIM)
                _, kh, rk = _head_norm(kp_ref[:, sl], gk)
                dkn = dk_a[:, sl]
                dkh = dkn * gk
                dk = rk * (dkh - kh * jnp.mean(dkh * kh, axis=-1, keepdims=True))
                out_ref[:, HD + h * HEAD_DIM:HD + (h + 1) * HEAD_DIM] = dk.astype(BF16)
                gk_sum = gk_sum + jnp.sum(dkn * kh, axis=0, keepdims=True)
            out_ref[:, 2 * HD:3 * HD] = dv_a[...].astype(BF16)
            gg_ref[1:2, :] += gk_sum

        @pl.when(n < nb)
        def _():
            dq_a[...] = dq_b[...]
            dk_a[...] = dk_b[...]
            dv_a[...] = dv_b[...]

    last = nb - 1

    def blk(which, prev):
        if prev:
            return pl.BlockSpec((ATTN_BLOCK, HD), lambda r, n: (jnp.maximum(n - 1, 0), r * 3 + which))
        return pl.BlockSpec((ATTN_BLOCK, HD), lambda r, n: (jnp.minimum(n, last), r * 3 + which))

    gains = pl.BlockSpec((3, HEAD_DIM), lambda r, n: (0, 0))
    nar = pl.BlockSpec((ATTN_BLOCK, 128), lambda r, n: (jnp.minimum(n, last), r))
    acc = pltpu.VMEM((ATTN_BLOCK, HD), F32)
    return pl.pallas_call(
        body, name=name, grid=(d, nb + 1),
        in_specs=[blk(0, False), blk(1, True), blk(1, False), blk(2, True), blk(2, False),
                  pl.BlockSpec((ATTN_BLOCK, HD), lambda r, n: (jnp.minimum(n, last), r)), nar, nar, gains, gains],
        out_specs=[pl.BlockSpec((ATTN_BLOCK, 3 * HD), lambda r, n: (jnp.maximum(n - 1, 0), r)),
                   pl.BlockSpec((8, HEAD_DIM), lambda r, n: (0, 0))],
        out_shape=[jax.ShapeDtypeStruct((L, d * 3 * HD), BF16), jax.ShapeDtypeStruct((8, HEAD_DIM), F32)],
        scratch_shapes=[acc, acc, acc, acc, acc, acc,
                        pltpu.VMEM((2 * ATTN_BLOCK, HD), BF16), pltpu.VMEM((2 * ATTN_BLOCK, HD), BF16)],
        compiler_params=_params("arbitrary", "arbitrary"),
    )(qkv, qkv, qkv, qkv, qkv, dout, lse, delta, qg, kg)


def _adamw(w, grad, m, v, name):
    R, C = w.shape
    tr = _row_tile(R, 512)

    def body(w_ref, g_ref, m_ref, v_ref, d_ref, nm_ref, nv_ref):
        gv = g_ref[...]
        nm = ADAM_B1 * m_ref[...] + (1.0 - ADAM_B1) * gv
        nv = ADAM_B2 * v_ref[...] + (1.0 - ADAM_B2) * (gv * gv)
        m_hat = nm / (1.0 - ADAM_B1 ** ADAM_STEP)
        v_hat = nv / (1.0 - ADAM_B2 ** ADAM_STEP)
        d_ref[...] = -ADAM_LR * (m_hat / (jnp.sqrt(v_hat) + ADAM_EPS) + ADAM_WD * w_ref[...])
        nm_ref[...] = nm
        nv_ref[...] = nv

    blk = pl.BlockSpec((tr, C), lambda i: (i, 0))
    shp = jax.ShapeDtypeStruct((R, C), F32)
    return pl.pallas_call(
        body, name=name, grid=(R // tr,), in_specs=[blk] * 4, out_specs=[blk] * 3, out_shape=[shp] * 3,
        compiler_params=_params("parallel"),
    )(w, grad, m, v)


def _ffn_fwd(x, g, comm, s, tag, tgt=None):
    h = _rmsnorm(x, g, f"rmsnorm_{tag}")
    (win_t, wout), tok = comm.weights(s, h)
    z3, a = _gated_in(h, win_t, None, "swiglu", BF16, f"ffn_in_{tag}", after=tok)
    saved = (x, h, z3, a, win_t, wout)
    if tgt is None:
        return _rows_mm(a, wout, "res", f"ffn_out_{tag}", x=x, scale=0.5), saved
    return _rows_mm(a, wout, "loss", f"ffn_out_loss_{tag}", x=x, scale=0.5, tgt=tgt), saved


def _ffn_bwd(dxo, saved, g, comm, s, tag, after):
    x, h, z3, a, win_t, wout = saved
    dz3 = _swiglu_bwd(dxo, wout, z3, 0.5, f"ffn_out_bwd_{tag}", after=after)
    d_wout = _mm_tn(a, dxo, 0.5, f"ffn_dwout_{tag}")
    d_win_t = _mm_tn(dz3, h, 1.0, f"ffn_dwin_{tag}")
    dx, dg8 = _rows_mm(dz3, win_t, "rmsbwd", f"ffn_in_bwd_{tag}", x=x, g=g, dxo=dxo)
    return dx, dg8, comm.grads(s, [d_win_t, d_wout])


def _local_step(x, tgt, vecs, comm):
    norm_g = vecs["norm_g"]

    def gvec(i):
        return norm_g[i:i + 1]

    x1, sv_a0 = _ffn_fwd(x, gvec(0), comm, 0, "l0a")
    h_c = _rmsnorm(x1, gvec(1), "rmsnorm_conv")
    (pw1_t, pw2), tok = comm.weights(1, h_c)
    zc3, u = _gated_in(h_c, pw1_t, vecs["b_pw1"], "glu", F32, "conv_pw1", after=tok)
    v, s = _dwconv_fwd(u, vecs["w_dw"], vecs["b_dw"], vecs["cg"], "conv_dw")
    x2 = _rows_mm(s, pw2, "res", "conv_pw2", x=x1, bias=vecs["b_pw2"])
    x3, sv_b0 = _ffn_fwd(x2, gvec(2), comm, 2, "l0b")
    x4, sv_a1 = _ffn_fwd(x3, gvec(3), comm, 3, "l1a")
    h_s = _rmsnorm_streams(x4, gvec(4), "rmsnorm_attn")
    (qkv_t, wo), tok = comm.weights(4, h_s[0])
    hd3 = qkv_t.shape[0] // 3
    qkvs, os, lses = [], [], []
    for g, d in enumerate(ATTN_DILATIONS):
        qkv = _mm_nt(h_s[g], qkv_t, BF16, f"attn_qkv_g{g}", w_row0=g * hd3, n_rows=hd3, streams=d, after=tok)
        o, lse = _attn_fwd(qkv, vecs["q_norm"], vecs["k_norm"], g, f"attn_fwd_g{g}")
        qkvs.append(qkv)
        os.append(o)
        lses.append(lse)
    out_b, out_f, lse_tot = _attn_merge(os, lses, "attn_merge")
    x5 = _rows_mm(out_b, wo, "res", "attn_wo", x=x4)
    (dy, loss8), sv_b1 = _ffn_fwd(x5, gvec(5), comm, 5, "l1b", tgt=tgt)
    loss = 0.5 * jnp.sum(loss8) / x.shape[1]

    dg = [None] * 6
    dx, dg[5], tok = _ffn_bwd(dy, sv_b1, gvec(5), comm, 5, "l1b", ())
    dout = _mm_nt(dx, wo, F32, "attn_wo_bwd", after=tok)
    d_wo = _mm_tn(out_b, dx, 1.0, "attn_dwo")
    delta, *wide = _attn_delta(out_f, dout, lse_tot, "attn_delta")
    per_group = [(dout, lse_tot, delta), tuple(wide[0:3]), tuple(wide[3:6])]
    ggs, d_qkv_t, dhs = [], [], []
    for g, d in enumerate(ATTN_DILATIONS):
        dqkv, gg = _attn_bwd(qkvs[g], *per_group[g], vecs["q_norm"], vecs["k_norm"], g, f"attn_bwd_g{g}")
        ggs.append(gg)
        d_qkv_t.append(_mm_tn(dqkv, h_s[g], 1.0, f"attn_dwqkv_g{g}", streams=d))
        if d > 1:
            dhs.append((d, _rows_mm(dqkv, qkv_t, "plain", f"attn_qkv_bwd_g{g}", b_row0=g * hd3, streams=d)))
        else:
            dqkv0 = dqkv
    dx, dg[4] = _rows_mm(dqkv0, qkv_t, "rmsbwd", "attn_qkv_bwd_g0", x=x4, g=gvec(4), dxo=dx, extras=dhs)
    tok = comm.grads(4, [jnp.concatenate(d_qkv_t), d_wo])
    dx, dg[3], tok = _ffn_bwd(dx, sv_a1, gvec(3), comm, 3, "l1a", tok)
    dx, dg[2], tok = _ffn_bwd(dx, sv_b0, gvec(2), comm, 2, "l0b", tok)
    ds = _mm_nt(dx, pw2, F32, "conv_pw2_bwd", after=tok)
    d_pw2 = _mm_tn(s, dx, 1.0, "conv_dwpw2")
    dv, dcg8, dbdw8, dbpw2_8 = _conv_post_bwd(ds, v, vecs["cg"], dx, "conv_post_bwd")
    dzc3, d_wdw, db1_16 = _dwconv_bwd(dv, u, vecs["w_dw"], zc3, "conv_dw_bwd")
    d_pw1_t = _mm_tn(dzc3, h_c, 1.0, "conv_dwpw1")
    dx, dg[1] = _rows_mm(dzc3, pw1_t, "rmsbwd", "conv_pw1_bwd", x=x1, g=gvec(1), dxo=dx)
    tok = comm.grads(1, [d_pw1_t, d_pw2])
    dx, dg[0], tok = _ffn_bwd(dx, sv_a0, gvec(0), comm, 0, "l0a", tok)

    D = x.shape[1]
    small = {
        "norm_g": jnp.stack([p.sum(axis=0) for p in dg]),
        "b_pw1": db1_16.reshape(2, 8, D).sum(axis=1),
        "w_dw": d_wdw[:CONV_WIDTH],
        "b_dw": dbdw8.sum(axis=0, keepdims=True),
        "cg": dcg8.sum(axis=0, keepdims=True),
        "b_pw2": dbpw2_8.sum(axis=0, keepdims=True),
        "q_norm": jnp.stack([gg[0] for gg in ggs]),
        "k_norm": jnp.stack([gg[1] for gg in ggs]),
    }
    return loss, dx, small, tok


def _mesh_pos():
    return lax.axis_index("x"), lax.axis_index("y"), lax.axis_index("c")


def _other_chips(x, y):
    return [(1 - x, y), (x, 1 - y), (1 - x, 1 - y)]


HBM = pl.BlockSpec(memory_space=pltpu.HBM)
SEM = pl.BlockSpec(memory_space=pltpu.SEMAPHORE)
SPLIT_COPY = pltpu.SideEffectType.DATAFLOW_SIDE_EFFECTING


def _hbm(a):
    return pltpu.with_memory_space_constraint(a, pltpu.HBM)


def _gather_copies(j, src, out, send, recv, x, y, c):
    me = 4 * x + 2 * y + c
    peers = [(x, y, 1 - c)] + [(px, py, c) for px, py in _other_chips(x, y)]
    return [pltpu.make_async_remote_copy(src_ref=src, dst_ref=out.at[me], send_sem=send.at[4 * j + k],
                                         recv_sem=recv.at[4 * j + k], device_id=peer, device_id_type=MESH)
            for k, peer in enumerate(peers)]


def _gather_start(shards, after, name):
    n = len(shards)

    def body(*refs):
        ins, outs = refs[:n], refs[n + len(after):2 * n + len(after)]
        send, recv, token, local_sems = refs[-4:]
        x, y, c = _mesh_pos()
        local = [pltpu.make_async_copy(ins[i], outs[i].at[4 * x + 2 * y + c], local_sems.at[i]) for i in range(n)]
        for cp in local:
            cp.start()
        for cp in local:
            cp.wait()
        for j in range(n):
            for cp in _gather_copies(j, ins[j], outs[j], send, recv, x, y, c):
                cp.start()
        token[...] = jnp.zeros_like(token)

    res = pl.pallas_call(
        body, name=name, in_specs=[HBM] * n + [ANY] * len(after),
        out_specs=[HBM] * (2 * n) + [SEM, SEM, pl.BlockSpec(memory_space=pltpu.VMEM)],
        out_shape=[pltpu.HBM((N_DEV,) + s.shape, s.dtype) for s in shards] + [pltpu.HBM(s.shape, s.dtype) for s in shards]
        + [pltpu.SemaphoreType.DMA((4 * n,)), pltpu.SemaphoreType.DMA((4 * n,)), jax.ShapeDtypeStruct((8, 128), F32)],
        input_output_aliases={i: n + i for i in range(n)},
        scratch_shapes=[pltpu.SemaphoreType.DMA((n,))],
        compiler_params=pltpu.CompilerParams(has_side_effects=SPLIT_COPY),
    )(*[_hbm(s) for s in shards], *after)
    return res[:n], res[n:2 * n], res[2 * n], res[2 * n + 1], res[2 * n + 2]


def _gather_wait(outs, thru, send, recv, after, name):
    n = len(outs)

    def body(*refs):
        out_refs, thru_refs, send_ref, recv_ref = refs[:n], refs[n:2 * n], refs[2 * n], refs[2 * n + 1]
        x, y, c = _mesh_pos()
        for j in range(n):
            for cp in _gather_copies(j, thru_refs[j], out_refs[j], send_ref, recv_ref, x, y, c):
                cp.wait_send()
                cp.wait_recv()

    res = pl.pallas_call(
        body, name=name, in_specs=[HBM] * (2 * n) + [SEM, SEM, ANY], out_specs=[HBM] * (2 * n),
        out_shape=[pltpu.HBM(a.shape, a.dtype) for a in list(outs) + list(thru)],
        input_output_aliases={i: i for i in range(2 * n)},
        compiler_params=pltpu.CompilerParams(has_side_effects=SPLIT_COPY),
    )(*outs, *thru, send, recv, after)
    return res[:n]


def _gather_forward(outs, name):
    n = len(outs)

    def body(*refs):
        o = refs[n:2 * n]
        send_sems, recv_sems = refs[2 * n:]
        x, y, c = _mesh_pos()
        copies = []
        for j in range(n):
            for k, (px, py) in enumerate(_other_chips(x, y)):
                blk = o[j].at[4 * px + 2 * py + c]
                cp = pltpu.make_async_remote_copy(src_ref=blk, dst_ref=blk, send_sem=send_sems.at[j, k],
                                                  recv_sem=recv_sems.at[j, k], device_id=(x, y, 1 - c), device_id_type=MESH)
                cp.start()
                copies.append(cp)
        for cp in copies:
            cp.wait()

    return pl.pallas_call(
        body, name=name, in_specs=[ANY] * n, out_specs=[ANY] * n,
        out_shape=[jax.ShapeDtypeStruct(a.shape, a.dtype) for a in outs],
        input_output_aliases={i: i for i in range(n)},
        scratch_shapes=[pltpu.SemaphoreType.DMA((n, 3)), pltpu.SemaphoreType.DMA((n, 3))],
    )(*outs)


def _all_sum(p, name):
    R, C = p.shape

    def body(p_ref, o_ref, buf, send_sems, recv_sems):
        x, y, c = _mesh_pos()
        me = 4 * x + 2 * y + c
        buf[me] = p_ref[...]
        copies = []
        for rel in range(1, N_DEV):
            peer = (1 - x if rel & 4 else x, 1 - y if rel & 2 else y, 1 - c if rel & 1 else c)
            cp = pltpu.make_async_remote_copy(
                src_ref=p_ref, dst_ref=buf.at[me], send_sem=send_sems.at[rel - 1], recv_sem=recv_sems.at[rel - 1],
                device_id=peer, device_id_type=MESH)
            cp.start()
            copies.append(cp)
        for cp in copies:
            cp.wait()
        acc = buf[0]
        for dev in range(1, N_DEV):
            acc = acc + buf[dev]
        o_ref[...] = acc

    vm = pl.BlockSpec(memory_space=pltpu.VMEM)
    return pl.pallas_call(
        body, name=name, in_specs=[vm], out_specs=vm, out_shape=jax.ShapeDtypeStruct((R, C), F32),
        scratch_shapes=[pltpu.VMEM((N_DEV, R, C), F32), pltpu.SemaphoreType.DMA((N_DEV - 1,)),
                        pltpu.SemaphoreType.DMA((N_DEV - 1,))],
    )(p)


def _swap_with_sibling(gs, name):
    n = len(gs)

    def body(*refs):
        ins, outs = refs[:n], refs[n:2 * n]
        send_sems, recv_sems = refs[2 * n:]
        x, y, c = _mesh_pos()
        copies = []
        for i in range(n):
            for k in range(4):
                cp = pltpu.make_async_remote_copy(
                    src_ref=ins[i].at[2 * k + (1 - c)], dst_ref=outs[i].at[k],
                    send_sem=send_sems.at[i, k], recv_sem=recv_sems.at[i, k],
                    device_id=(x, y, 1 - c), device_id_type=MESH)
                cp.start()
                copies.append(cp)
        for cp in copies:
            cp.wait()

    return pl.pallas_call(
        body, name=name, in_specs=[ANY] * n, out_specs=[ANY] * n,
        out_shape=[jax.ShapeDtypeStruct((4,) + g.shape[1:], g.dtype) for g in gs],
        scratch_shapes=[pltpu.SemaphoreType.DMA((n, 4)), pltpu.SemaphoreType.DMA((n, 4))],
    )(*gs)


def _scatter_copies(j, src, land, send, recv, x, y, c):
    return [pltpu.make_async_remote_copy(src_ref=src.at[2 * px + py], dst_ref=land.at[k], send_sem=send.at[3 * j + k],
                                         recv_sem=recv.at[3 * j + k], device_id=(px, py, c), device_id_type=MESH)
            for k, (px, py) in enumerate(_other_chips(x, y))]


def _scatter_start(ps, name):
    n = len(ps)

    def body(*refs):
        ins, lands = refs[:n], refs[2 * n:3 * n]
        send, recv, token = refs[3 * n:]
        x, y, c = _mesh_pos()
        for j in range(n):
            for cp in _scatter_copies(j, ins[j], lands[j], send, recv, x, y, c):
                cp.start()
        token[...] = jnp.zeros_like(token)

    res = pl.pallas_call(
        body, name=name, in_specs=[HBM] * n,
        out_specs=[HBM] * (2 * n) + [SEM, SEM, pl.BlockSpec(memory_space=pltpu.VMEM)],
        out_shape=[pltpu.HBM(p.shape, p.dtype) for p in ps] + [pltpu.HBM((3,) + p.shape[1:], p.dtype) for p in ps]
        + [pltpu.SemaphoreType.DMA((3 * n,)), pltpu.SemaphoreType.DMA((3 * n,)), jax.ShapeDtypeStruct((8, 128), F32)],
        input_output_aliases={i: i for i in range(n)},
        compiler_params=pltpu.CompilerParams(has_side_effects=SPLIT_COPY),
    )(*[_hbm(p) for p in ps])
    return res[:n], res[n:2 * n], res[2 * n], res[2 * n + 1], res[2 * n + 2]


def _scatter_wait(thru, lands, send, recv, after, name):
    n = len(thru)

    def body(*refs):
        thru_refs, land_refs, send_ref, recv_ref = refs[:n], refs[n:2 * n], refs[2 * n], refs[2 * n + 1]
        x, y, c = _mesh_pos()
        for j in range(n):
            for cp in _scatter_copies(j, thru_refs[j], land_refs[j], send_ref, recv_ref, x, y, c):
                cp.wait_send()
                cp.wait_recv()

    res = pl.pallas_call(
        body, name=name, in_specs=[HBM] * (2 * n) + [SEM, SEM, ANY], out_specs=[HBM] * (2 * n),
        out_shape=[pltpu.HBM(a.shape, a.dtype) for a in list(thru) + list(lands)],
        input_output_aliases={i: i for i in range(2 * n)},
        compiler_params=pltpu.CompilerParams(has_side_effects=SPLIT_COPY),
    )(*thru, *lands, send, recv, after)
    return res[:n], res[n:]


def _row_tile(r, pref):
    if r <= pref:
        return r
    for t in range(pref - pref % 16, 0, -16):
        if r % t == 0:
            return t
    return r


def _add_sibling(g, r1, pos, name):
    _, r, D = g.shape
    tr = _row_tile(r, 512)

    def body(pos_ref, g_ref, r_ref, o_ref):
        o_ref[...] = (g_ref[...].astype(F32) + r_ref[...].astype(F32)).astype(BF16)

    return pl.pallas_call(
        body, name=name,
        grid_spec=pltpu.PrefetchScalarGridSpec(
            num_scalar_prefetch=1, grid=(4, r // tr),
            in_specs=[pl.BlockSpec((None, None, tr, D), lambda k, j, pos: (k, pos[0], j, 0)),
                      pl.BlockSpec((None, tr, D), lambda k, j, pos: (k, j, 0))],
            out_specs=pl.BlockSpec((None, tr, D), lambda k, j, pos: (k, j, 0))),
        out_shape=jax.ShapeDtypeStruct((4, r, D), BF16),
        compiler_params=_params("parallel", "parallel"),
    )(pos, g.reshape(4, 2, r, D), r1)


def _add_chips(p, r2, pos, name):
    _, r, D = p.shape
    tr = _row_tile(r, 512)

    def body(pos_ref, p_ref, r_ref, o_ref):
        acc = p_ref[...].astype(F32)
        for j in range(3):
            acc = acc + r_ref[j].astype(F32)
        o_ref[...] = acc

    return pl.pallas_call(
        body, name=name,
        grid_spec=pltpu.PrefetchScalarGridSpec(
            num_scalar_prefetch=1, grid=(r // tr,),
            in_specs=[pl.BlockSpec((None, tr, D), lambda j, pos: (pos[1], j, 0)),
                      pl.BlockSpec((3, tr, D), lambda j, pos: (0, j, 0))],
            out_specs=pl.BlockSpec((tr, D), lambda j, pos: (j, 0))),
        out_shape=jax.ShapeDtypeStruct((r, D), F32),
        compiler_params=_params("parallel"),
    )(pos, p, r2)


def _place_cols(local, me, width):
    R, w = local.shape
    return lax.dynamic_update_slice(jnp.zeros((R, width), F32), local, (0, me * w))


def _pad_rows(a, rows):
    return jnp.pad(a, ((0, rows - a.shape[0]), (0, 0)))


SUBLAYER_SHARDS = ((0, 4), (8, 9), (1, 5), (2, 6), (10, 11), (3, 7))


class _StepComm:
    def __init__(self, shards, pos):
        self.pos = pos
        self.shards = shards
        self.gathers = {}
        tok = self._start_gather(0, ())
        self._start_gather(1, tok)
        self.pending = {}

    def _start_gather(self, s, after):
        outs, thru, send, recv, token = _gather_start([self.shards[i] for i in SUBLAYER_SHARDS[s]], after,
                                                      f"gather_start_{s}")
        self.gathers[s] = (outs, thru, send, recv)
        return (token,)

    def weights(self, s, after):
        outs = _gather_wait(*self.gathers.pop(s), after, f"gather_wait_{s}")
        outs = _gather_forward(outs, f"gather_forward_{s}")
        tok = self._start_gather(s + 2, (outs[0],)) if s + 2 < len(SUBLAYER_SHARDS) else ()
        return [w.reshape(-1, w.shape[-1]) for w in outs], tok

    def grads(self, s, mats):
        g3 = [g.reshape(N_DEV, g.shape[0] // N_DEV, g.shape[1]) for g in mats]
        r1 = _swap_with_sibling(g3, f"rs_sibling_{s}")
        ps = [_add_sibling(g, r, self.pos, f"rs_add_sibling_{s}_{i}") for i, (g, r) in enumerate(zip(g3, r1))]
        thru, lands, send, recv, token = _scatter_start(ps, f"rs_start_{s}")
        self.pending[s] = (thru, lands, send, recv)
        return (token,)

    def finish(self, after):
        out = {}
        for s in sorted(self.pending, reverse=True):
            ps, lands = _scatter_wait(*self.pending[s], after, f"rs_wait_{s}")
            for i, p, r2 in zip(SUBLAYER_SHARDS[s], ps, lands):
                out[i] = _add_chips(p, r2, self.pos, f"rs_add_chips_{i}")
        return out


def kernel(x, norm_g, ffn_w_in, ffn_w_out, conv_w_pw1, conv_b_pw1, conv_w_dw, conv_b_dw, conv_norm_g, conv_w_pw2, conv_b_pw2, attn_w_qkv, attn_q_norm, attn_k_norm, attn_w_o, loss_target, m_norm_g, m_ffn_w_in, m_ffn_w_out, m_conv_w_pw1, m_conv_b_pw1, m_conv_w_dw, m_conv_b_dw, m_conv_norm_g, m_conv_w_pw2, m_conv_b_pw2, m_attn_w_qkv, m_attn_q_norm, m_attn_k_norm, m_attn_w_o, v_norm_g, v_ffn_w_in, v_ffn_w_out, v_conv_w_pw1, v_conv_b_pw1, v_conv_w_dw, v_conv_b_dw, v_conv_norm_g, v_conv_w_pw2, v_conv_b_pw2, v_attn_w_qkv, v_attn_q_norm, v_attn_k_norm, v_attn_w_o):
    T, D = x.shape[1], x.shape[2]
    xi, yi, ci = _mesh_pos()
    me = 4 * xi + 2 * yi + ci
    pos = jnp.stack([ci, 2 * xi + yi]).astype(I32)
    lf = [(l, f) for l in range(2) for f in range(2)]

    shards = ([ffn_w_in[l, f].T.astype(BF16) for l, f in lf] + [ffn_w_out[l, f].astype(BF16) for l, f in lf]
              + [conv_w_pw1[0].T.astype(BF16), conv_w_pw2[0].astype(BF16),
                 attn_w_qkv[0].T.astype(BF16), attn_w_o[0].astype(BF16)])
    comm = _StepComm(shards, pos)
    w_cols = norm_g.shape[2]
    vec = _all_sum(_pad_rows(jnp.concatenate([_place_cols(norm_g.reshape(6, w_cols), me, D),
                                              _place_cols(conv_w_dw[0], me, D)]), 40), "gather_vectors")
    vecs = {
        "norm_g": vec[0:6], "w_dw": vec[6:6 + CONV_WIDTH],
        "b_pw1": conv_b_pw1.reshape(2, D), "b_dw": conv_b_dw, "cg": conv_norm_g, "b_pw2": conv_b_pw2,
        "q_norm": attn_q_norm[0], "k_norm": attn_k_norm[0],
    }

    loss_local, dx, small, tok = _local_step(x[0], loss_target[0], vecs, comm)
    loss = lax.psum(loss_local, ("x", "y", "c"))

    mats = comm.finish(tok[0])
    hd3 = 3 * HEAD_DIM
    packed = jnp.concatenate([
        small["norm_g"], small["b_pw1"], small["w_dw"], small["b_dw"], small["cg"], small["b_pw2"],
        jnp.pad(small["q_norm"].reshape(1, hd3), ((0, 0), (0, D - hd3))),
        jnp.pad(small["k_norm"].reshape(1, hd3), ((0, 0), (0, D - hd3)))])
    red = _all_sum(_pad_rows(packed, 48), "reduce_vectors")
    col0 = me * w_cols
    grads = {
        "norm_g": lax.dynamic_slice(red[0:6], (0, col0), (6, w_cols)),
        "ffn_w_in": jnp.concatenate([mats[i].T for i in range(4)]),
        "ffn_w_out": jnp.concatenate([mats[i] for i in range(4, 8)]),
        "conv_w_pw1": mats[8].T,
        "conv_b_pw1": red[6:8].reshape(1, 2 * D),
        "conv_w_dw": lax.dynamic_slice(red[8:8 + CONV_WIDTH], (0, col0), (CONV_WIDTH, w_cols)),
        "conv_b_dw": red[39:40], "conv_norm_g": red[40:41], "conv_w_pw2": mats[9], "conv_b_pw2": red[41:42],
        "attn_w_qkv": mats[10].T,
        "attn_q_norm": red[42, :hd3].reshape(3, HEAD_DIM), "attn_k_norm": red[43, :hd3].reshape(3, HEAD_DIM),
        "attn_w_o": mats[11],
    }

    names = ["norm_g", "ffn_w_in", "ffn_w_out", "conv_w_pw1", "conv_b_pw1", "conv_w_dw", "conv_b_dw",
             "conv_norm_g", "conv_w_pw2", "conv_b_pw2", "attn_w_qkv", "attn_q_norm", "attn_k_norm", "attn_w_o"]
    ws = [norm_g, ffn_w_in, ffn_w_out, conv_w_pw1, conv_b_pw1, conv_w_dw, conv_b_dw, conv_norm_g, conv_w_pw2,
          conv_b_pw2, attn_w_qkv, attn_q_norm, attn_k_norm, attn_w_o]
    ms = [m_norm_g, m_ffn_w_in, m_ffn_w_out, m_conv_w_pw1, m_conv_b_pw1, m_conv_w_dw, m_conv_b_dw, m_conv_norm_g,
          m_conv_w_pw2, m_conv_b_pw2, m_attn_w_qkv, m_attn_q_norm, m_attn_k_norm, m_attn_w_o]
    vs = [v_norm_g, v_ffn_w_in, v_ffn_w_out, v_conv_w_pw1, v_conv_b_pw1, v_conv_w_dw, v_conv_b_dw, v_conv_norm_g,
          v_conv_w_pw2, v_conv_b_pw2, v_attn_w_qkv, v_attn_q_norm, v_attn_k_norm, v_attn_w_o]
    g_out, d_out, m_out, v_out = [], [], [], []
    for name, w, m, v in zip(names, ws, ms, vs):
        shape2 = (-1, w.shape[-1])
        g2 = grads[name].reshape(shape2)
        delta, new_m, new_v = _adamw(w.reshape(shape2), g2, m.reshape(shape2), v.reshape(shape2), f"adamw_{name}")
        g_out.append(g2.reshape(w.shape))
        d_out.append(delta.reshape(w.shape))
        m_out.append(new_m.reshape(w.shape))
        v_out.append(new_v.reshape(w.shape))
    return (loss, dx.reshape(x.shape), *g_out, *d_out, *m_out, *v_out)
```

```python
import functools

import jax
import jax.numpy as jnp
from jax import lax
from jax.experimental import pallas as pl
from jax.experimental.pallas import tpu as pltpu

F32 = jnp.float32
BF16 = jnp.bfloat16
I32 = jnp.int32

NORM_EPS = 1e-6
LANES = 128
SUBLANES = 8
HEAD_DIM = 128
ATTN_BLOCK = 128
ATTN_DILATIONS = (1, 4, 16)
CONV_WIDTH = 31
CONV_HALO = 32
CONV_CHUNK = 16
STREAM_TILE = 512
ADAM_LR, ADAM_B1, ADAM_B2, ADAM_EPS, ADAM_WD, ADAM_STEP = 0.001, 0.9, 0.999, 1e-08, 0.01, 10
N_DEV = 8
NEG_BIG = -1e30
V7X_VMEM_BYTES = 64 * 1024 * 1024
VMEM_LIMIT = V7X_VMEM_BYTES - 8 * 1024 * 1024
MESH = pl.DeviceIdType.MESH
ANY = pl.BlockSpec(memory_space=pl.ANY)


def _params(*sem):
    return pltpu.CompilerParams(dimension_semantics=sem or None, vmem_limit_bytes=VMEM_LIMIT)


def _dot_nn(a, b):
    return lax.dot_general(a, b, (((1,), (0,)), ((), ())), preferred_element_type=F32)


def _dot_nt(a, b):
    return lax.dot_general(a, b, (((1,), (1,)), ((), ())), preferred_element_type=F32)


def _dot_tn(a, b):
    return lax.dot_general(a, b, (((0,), (0,)), ((), ())), preferred_element_type=F32)


def _sigmoid(x):
    return 1.0 / (1.0 + jnp.exp(-x))


def _tile(n, pref):
    if n <= pref:
        return n
    for t in range(pref - pref % 128, 0, -128):
        if n % t == 0:
            return t
    return n


def _to_lane_chunks(dst, val):
    for c in range(dst.shape[0]):
        dst[c] = val[:, c * LANES:(c + 1) * LANES]


def _rows8(v):
    tm, c = v.shape
    return v.reshape(tm // 8, 8, c).sum(axis=0)


def _rmsnorm(x, g, name):
    T, D = x.shape
    tm = min(T, 512)

    def body(x_ref, g_ref, h_ref):
        xv = x_ref[...]
        r = lax.rsqrt(jnp.mean(xv * xv, axis=-1, keepdims=True) + NORM_EPS)
        h_ref[...] = (xv * r * g_ref[...]).astype(BF16)

    return pl.pallas_call(
        body, name=name, grid=(T // tm,),
        in_specs=[pl.BlockSpec((tm, D), lambda i: (i, 0)), pl.BlockSpec((1, D), lambda i: (0, 0))],
        out_specs=pl.BlockSpec((tm, D), lambda i: (i, 0)),
        out_shape=jax.ShapeDtypeStruct((T, D), BF16),
        compiler_params=_params("parallel"),
    )(x, g)


def _rmsnorm_streams(x, g, name):
    T, D = x.shape
    tm = min(T, STREAM_TILE)
    dils = [d for d in ATTN_DILATIONS if d > 1]

    def body(x_ref, g_ref, h_ref, *rest):
        outs, hs = rest[:-1], rest[-1]
        xv = x_ref[...]
        r = lax.rsqrt(jnp.mean(xv * xv, axis=-1, keepdims=True) + NORM_EPS)
        hf = xv * r * g_ref[...]
        h_ref[...] = hf.astype(BF16)
        _to_lane_chunks(hs, hf)
        for d, o_ref in zip(dils, outs):
            for s in range(d):
                for c in range(D // LANES):
                    o_ref[s, :, c * LANES:(c + 1) * LANES] = hs.at[c][pl.ds(s, tm // d, stride=d), :].astype(BF16)

    res = pl.pallas_call(
        body, name=name, grid=(T // tm,),
        in_specs=[pl.BlockSpec((tm, D), lambda i: (i, 0)), pl.BlockSpec((1, D), lambda i: (0, 0))],
        out_specs=[pl.BlockSpec((tm, D), lambda i: (i, 0))]
        + [pl.BlockSpec((d, tm // d, D), lambda i: (0, i, 0)) for d in dils],
        out_shape=[jax.ShapeDtypeStruct((T, D), BF16)] + [jax.ShapeDtypeStruct((d, T // d, D), BF16) for d in dils],
        scratch_shapes=[pltpu.VMEM((D // LANES, tm, LANES), F32)],
        compiler_params=_params("parallel"),
    )(x, g)
    return [res[0]] + [a.reshape(T, D) for a in res[1:]]


def _gated_in(h, wt, bias, act, u_dtype, name, after=()):
    T, D = h.shape
    F = wt.shape[0] // 2
    tn = _tile(F, 1408)
    tm = min(T, 512)
    nn = F // tn

    def body(*refs):
        z_ref, u_ref = refs[-2:]
        if bias is None:
            h_ref, w0_ref, w1_ref = refs[:3]
        else:
            h_ref, w0_ref, w1_ref, b_ref = refs[:4]
        hv = h_ref[...]
        z0 = _dot_nt(hv, w0_ref[...])
        z1 = _dot_nt(hv, w1_ref[...])
        if bias is not None:
            z0 = z0 + b_ref[0:1, :]
            z1 = z1 + b_ref[1:2, :]
        z_ref[0] = z0.astype(BF16)
        z_ref[1] = z1.astype(BF16)
        if act == "swiglu":
            u = z0 * _sigmoid(z0) * z1
        else:
            u = z0 * _sigmoid(z1)
        u_ref[...] = u.astype(u_dtype)

    in_specs = [pl.BlockSpec((tm, D), lambda n, m: (m, 0)),
                pl.BlockSpec((tn, D), lambda n, m: (n, 0)),
                pl.BlockSpec((tn, D), lambda n, m: (n + nn, 0))]
    args = [h, wt, wt]
    if bias is not None:
        in_specs.append(pl.BlockSpec((2, tn), lambda n, m: (0, n)))
        args.append(bias)
    in_specs += [ANY] * len(after)
    args += list(after)
    return pl.pallas_call(
        body, name=name, grid=(nn, T // tm), in_specs=in_specs,
        out_specs=[pl.BlockSpec((2, tm, tn), lambda n, m: (0, m, n)), pl.BlockSpec((tm, tn), lambda n, m: (m, n))],
        out_shape=[jax.ShapeDtypeStruct((2, T, F), BF16), jax.ShapeDtypeStruct((T, F), u_dtype)],
        compiler_params=_params("parallel", "parallel"),
    )(*args)


def _swiglu_bwd(dxo, wout, z3, scale, name, after=()):
    T, D = dxo.shape
    F = wout.shape[0]
    tn = _tile(F, 1408)
    tm = min(T, 512)

    def body(d_ref, w_ref, z_ref, *rest):
        dz_ref = rest[-1]
        dy = (d_ref[...] * scale).astype(BF16)
        da = _dot_nt(dy, w_ref[...])
        gate = z_ref[0].astype(F32)
        up = z_ref[1].astype(F32)
        sg = _sigmoid(gate)
        dz_ref[0] = (da * up * (sg * (1.0 + gate * (1.0 - sg)))).astype(BF16)
        dz_ref[1] = (da * gate * sg).astype(BF16)

    return pl.pallas_call(
        body, name=name, grid=(F // tn, T // tm),
        in_specs=[pl.BlockSpec((tm, D), lambda n, m: (m, 0)),
                  pl.BlockSpec((tn, D), lambda n, m: (n, 0)),
                  pl.BlockSpec((2, tm, tn), lambda n, m: (0, m, n))] + [ANY] * len(after),
        out_specs=pl.BlockSpec((2, tm, tn), lambda n, m: (0, m, n)),
        out_shape=jax.ShapeDtypeStruct((2, T, F), BF16),
        compiler_params=_params("parallel", "parallel"),
    )(dxo, wout, z3, *after)


def _mm_nt(a, wt, out_dtype, name, *, w_row0=0, n_rows=None, streams=1, after=()):
    T, K = a.shape
    N = wt.shape[0] if n_rows is None else n_rows
    L = T // streams
    tn = _tile(N, 1152)
    tm = min(L, 512)
    mps = L // tm
    npn = N // tn
    assert w_row0 % tn == 0
    wb0 = w_row0 // tn

    def body(a_ref, w_ref, *rest):
        o_ref = rest[-1]
        o_ref[...] = _dot_nt(a_ref[...].astype(BF16), w_ref[...]).astype(out_dtype)

    return pl.pallas_call(
        body, name=name, grid=(npn, T // tm),
        in_specs=[pl.BlockSpec((tm, K), lambda n, m: (m, 0)), pl.BlockSpec((tn, K), lambda n, m: (wb0 + n, 0))]
        + [ANY] * len(after),
        out_specs=pl.BlockSpec((tm, tn), lambda n, m: (m % mps, (m // mps) * npn + n)),
        out_shape=jax.ShapeDtypeStruct((L, streams * N), out_dtype),
        compiler_params=_params("parallel", "parallel"),
    )(a, wt, *after)


def _mm_tn(a, b, b_scale, name, *, streams=1):
    if a.ndim == 3:
        J, T, F = a.shape
    elif streams > 1:
        J, T, F = 1, a.shape[0] * streams, a.shape[1] // streams
    else:
        (T, F), J = a.shape, 1
    N = b.shape[1]
    tmm = _tile(F, 1408)
    tk = min(T // streams, 1024)
    npj = F // tmm
    nk = T // tk
    kps = nk // streams

    def body(a_ref, b_ref, o_ref, acc_ref):
        k = pl.program_id(1)
        bv = b_ref[...]
        if b_scale != 1.0:
            bv = bv * b_scale
        part = _dot_tn(a_ref[...].astype(BF16), bv.astype(BF16))

        @pl.when(k == 0)
        def _():
            acc_ref[...] = part

        @pl.when(k > 0)
        def _():
            acc_ref[...] += part

        @pl.when(k == nk - 1)
        def _():
            o_ref[...] = acc_ref[...].astype(BF16)

    if a.ndim == 3:
        a_spec = pl.BlockSpec((None, tk, tmm), lambda i, k: (i // npj, k, i % npj))
    elif streams > 1:
        a_spec = pl.BlockSpec((tk, tmm), lambda i, k: (k % kps, (k // kps) * npj + i))
    else:
        a_spec = pl.BlockSpec((tk, tmm), lambda i, k: (k, i))
    return pl.pallas_call(
        body, name=name, grid=(J * npj, nk),
        in_specs=[a_spec, pl.BlockSpec((tk, N), lambda i, k: (k, 0))],
        out_specs=pl.BlockSpec((tmm, N), lambda i, k: (i, 0)),
        out_shape=jax.ShapeDtypeStruct((J * F, N), BF16),
        scratch_shapes=[pltpu.VMEM((tmm, N), F32)],
        compiler_params=_params("parallel", "arbitrary"),
    )(a, b)


def _rows_mm(a, b, mode, name, *, x=None, scale=1.0, bias=None, tgt=None, g=None, dxo=None,
             b_row0=0, streams=1, extras=(), after=()):
    if a.ndim == 3:
        J, T, F = a.shape
    elif streams > 1:
        J, T, F = 1, a.shape[0] * streams, a.shape[1] // streams
    else:
        (T, F), J = a.shape, 1
    D = b.shape[1]
    tk = _tile(F, 2816)
    kpj = F // tk
    nk = J * kpj
    tm = min(T // streams, STREAM_TILE)
    nm = T // tm
    mps = nm // streams
    assert b_row0 % tk == 0
    kb0 = b_row0 // tk
    n_ex = len(extras)

    def body(*refs):
        refs = list(refs)
        a_ref, b_ref = refs[:2]
        rest = refs[2 + len(after):]
        if n_ex:
            tmp_ref = rest.pop()
        acc_ref = rest.pop()
        ex_refs, rest = rest[:n_ex], rest[n_ex:]
        m, k = pl.program_id(0), pl.program_id(1)
        part = _dot_nn(a_ref[...], b_ref[...])

        @pl.when(k == 0)
        def _():
            acc_ref[...] = part

        @pl.when(k > 0)
        def _():
            acc_ref[...] += part

        @pl.when(k == nk - 1)
        def _():
            acc = acc_ref[...]
            for (d, _), e_ref in zip(extras, ex_refs):
                for s in range(d):
                    for c in range(D // LANES):
                        tmp_ref.at[c][pl.ds(s, tm // d, stride=d), :] = e_ref[s, :, c * LANES:(c + 1) * LANES]
                acc = acc + jnp.concatenate([tmp_ref[c] for c in range(D // LANES)], axis=1)
            if mode == "plain":
                (o_ref,) = rest
                o_ref[...] = acc
            elif mode == "res":
                if bias is None:
                    x_ref, o_ref = rest
                else:
                    x_ref, bias_ref, o_ref = rest
                    acc = acc + bias_ref[...]
                o_ref[...] = x_ref[...] + scale * acc
            elif mode == "loss":
                x_ref, t_ref, o_ref, l_ref = rest
                e = x_ref[...] + scale * acc - t_ref[...]
                o_ref[...] = e / D

                @pl.when(m == 0)
                def _():
                    l_ref[...] = jnp.zeros_like(l_ref)

                l_ref[...] += _rows8(e * e)
            else:
                x_ref, g_ref, d_ref, o_ref, dg_ref = rest
                xv = x_ref[...]
                r = lax.rsqrt(jnp.mean(xv * xv, axis=-1, keepdims=True) + NORM_EPS)
                xh = xv * r
                dxh = acc * g_ref[...]
                o_ref[...] = d_ref[...] + r * (dxh - xh * jnp.mean(dxh * xh, axis=-1, keepdims=True))

                @pl.when(m == 0)
                def _():
                    dg_ref[...] = jnp.zeros_like(dg_ref)

                dg_ref[...] += _rows8(acc * xh)

    if a.ndim == 3:
        a_spec = pl.BlockSpec((None, tm, tk), lambda m, k: (k // kpj, m, k % kpj))
    elif streams > 1:
        a_spec = pl.BlockSpec((tm, tk), lambda m, k: (m % mps, (m // mps) * kpj + k))
    else:
        a_spec = pl.BlockSpec((tm, tk), lambda m, k: (m, k))
    row = pl.BlockSpec((tm, D), lambda m, k: (m, 0))
    vec = pl.BlockSpec((1, D), lambda m, k: (0, 0))
    part8 = pl.BlockSpec((8, D), lambda m, k: (0, 0))
    in_specs = [a_spec, pl.BlockSpec((tk, D), lambda m, k: (kb0 + k, 0))] + [ANY] * len(after)
    args = [a, b, *after]
    for d, e in extras:
        in_specs.append(pl.BlockSpec((d, tm // d, D), lambda m, k: (0, m, 0)))
        args.append(e.reshape(d, T // d, D))
    full = jax.ShapeDtypeStruct((T, D), F32)
    small = jax.ShapeDtypeStruct((8, D), F32)
    if mode == "plain":
        out_specs, out_shape = row, full
    elif mode == "res":
        in_specs.append(row)
        args.append(x)
        if bias is not None:
            in_specs.append(vec)
            args.append(bias)
        out_specs, out_shape = row, full
    elif mode == "loss":
        in_specs += [row, row]
        args += [x, tgt]
        out_specs, out_shape = [row, part8], [full, small]
    else:
        in_specs += [row, vec, row]
        args += [x, g, dxo]
        out_specs, out_shape = [row, part8], [full, small]
    return pl.pallas_call(
        body, name=name, grid=(nm, nk), in_specs=in_specs, out_specs=out_specs, out_shape=out_shape,
        scratch_shapes=[pltpu.VMEM((tm, D), F32)] + ([pltpu.VMEM((D // LANES, tm, LANES), F32)] if n_ex else []),
        compiler_params=_params("arbitrary", "arbitrary"),
    )(*args)


def _sublane_phases(sh):
    rows = sh.shape[1] - SUBLANES
    for p in range(1, SUBLANES):
        sh[p, pl.ds(0, rows), :] = sh[0, pl.ds(p, rows), :]


def _shifted(sh, offset, j, rows=SUBLANES):
    start = pl.multiple_of(j * rows + (offset - offset % SUBLANES), SUBLANES)
    return sh[offset % SUBLANES, pl.ds(start, rows), :]


def _conv_taps(wb, sh, offsets, j, init):
    groups = CONV_CHUNK // SUBLANES
    acc = init
    for k, off in enumerate(offsets):
        rows = _shifted(sh, off, j, CONV_CHUNK).reshape(groups, SUBLANES, -1)
        acc = acc + (wb[k] * rows).reshape(CONV_CHUNK, -1)
    return acc


def _broadcast_rows(dst, src):
    for r in range(dst.shape[0]):
        dst[r] = jnp.zeros(dst.shape[1:], F32) + src[r:r + 1, :]


def _dwconv_fwd(u, w, b_dw, cg, name):
    T, C = u.shape
    tm = min(T, 256)
    hb = tm // CONV_HALO
    pad = CONV_HALO - (CONV_WIDTH - 1)

    def body(u_ref, halo_ref, w_ref, b_ref, g_ref, v_ref, s_ref, sh, wb):
        i = pl.program_id(0)
        sh[0, pl.ds(0, CONV_HALO), :] = jnp.where(i > 0, halo_ref[...], 0.0)
        sh[0, pl.ds(CONV_HALO, tm), :] = u_ref[...]
        _sublane_phases(sh)
        _broadcast_rows(wb, w_ref)
        bias = jnp.zeros((CONV_CHUNK, C), F32) + b_ref[...]

        def chunk(j, carry):
            acc = _conv_taps(wb, sh, [pad + k for k in range(CONV_WIDTH)], j, bias)
            v_ref[pl.ds(pl.multiple_of(j * CONV_CHUNK, CONV_CHUNK), CONV_CHUNK), :] = acc
            return carry

        lax.fori_loop(0, tm // CONV_CHUNK, chunk, 0)
        acc = v_ref[...]
        r = lax.rsqrt(jnp.mean(acc * acc, axis=-1, keepdims=True) + NORM_EPS)
        n = acc * r * g_ref[...]
        s_ref[...] = (n * _sigmoid(n)).astype(BF16)

    row = pl.BlockSpec((tm, C), lambda i: (i, 0))
    vec = pl.BlockSpec((1, C), lambda i: (0, 0))
    return pl.pallas_call(
        body, name=name, grid=(T // tm,),
        in_specs=[row, pl.BlockSpec((CONV_HALO, C), lambda i: (jnp.maximum(i * hb - 1, 0), 0)),
                  pl.BlockSpec((CONV_WIDTH, C), lambda i: (0, 0)), vec, vec],
        out_specs=[row, row],
        out_shape=[jax.ShapeDtypeStruct((T, C), F32), jax.ShapeDtypeStruct((T, C), BF16)],
        scratch_shapes=[pltpu.VMEM((SUBLANES, CONV_HALO + tm, C), F32), pltpu.VMEM((CONV_WIDTH, SUBLANES, C), F32)],
        compiler_params=_params("parallel"),
    )(u, u, w, b_dw, cg)


def _conv_post_bwd(ds, v, cg, dxo, name):
    T, C = v.shape
    tm = min(T, 512)

    def body(ds_ref, v_ref, g_ref, d_ref, dv_ref, dcg_ref, dbdw_ref, dbpw2_ref):
        @pl.when(pl.program_id(0) == 0)
        def _():
            dcg_ref[...] = jnp.zeros_like(dcg_ref)
            dbdw_ref[...] = jnp.zeros_like(dbdw_ref)
            dbpw2_ref[...] = jnp.zeros_like(dbpw2_ref)

        vv = v_ref[...]
        r = lax.rsqrt(jnp.mean(vv * vv, axis=-1, keepdims=True) + NORM_EPS)
        vh = vv * r
        n = vh * g_ref[...]
        sg = _sigmoid(n)
        dn = ds_ref[...] * (sg * (1.0 + n * (1.0 - sg)))
        dvh = dn * g_ref[...]
        dv = r * (dvh - vh * jnp.mean(dvh * vh, axis=-1, keepdims=True))
        dv_ref[...] = dv
        dcg_ref[...] += _rows8(dn * vh)
        dbdw_ref[...] += _rows8(dv)
        dbpw2_ref[...] += _rows8(d_ref[...])

    row = pl.BlockSpec((tm, C), lambda i: (i, 0))
    part8 = pl.BlockSpec((8, C), lambda i: (0, 0))
    small = jax.ShapeDtypeStruct((8, C), F32)
    return pl.pallas_call(
        body, name=name, grid=(T // tm,),
        in_specs=[row, row, pl.BlockSpec((1, C), lambda i: (0, 0)), row],
        out_specs=[row, part8, part8, part8],
        out_shape=[jax.ShapeDtypeStruct((T, C), F32), small, small, small],
        compiler_params=_params("arbitrary"),
    )(ds, v, cg, dxo)


def _dwconv_bwd(dv, u, w, z3, name):
    T, C = u.shape
    tm = min(T, 256)
    hb = tm // CONV_HALO
    nt = T // tm
    pad = CONV_HALO - (CONV_WIDTH - 1)

    taps_at_once = 4

    def body(dv_ref, dvn_ref, u_ref, up_ref, w_ref, z_ref, dz_ref, dw_ref, db_ref, dvsh, ush, wb, du_ref, dwacc):
        i = pl.program_id(0)

        @pl.when(i == 0)
        def _():
            dwacc[...] = jnp.zeros_like(dwacc)
            db_ref[...] = jnp.zeros_like(db_ref)

        dvsh[0, pl.ds(0, tm), :] = dv_ref[...]
        dvsh[0, pl.ds(tm, CONV_HALO), :] = jnp.where(i < nt - 1, dvn_ref[...], 0.0)
        ush[0, pl.ds(0, CONV_HALO), :] = jnp.where(i > 0, up_ref[...], 0.0)
        ush[0, pl.ds(CONV_HALO, tm), :] = u_ref[...]
        _sublane_phases(dvsh)
        _sublane_phases(ush)
        _broadcast_rows(wb, w_ref)

        def du_chunk(j, carry):
            acc = _conv_taps(wb, dvsh, [CONV_WIDTH - 1 - k for k in range(CONV_WIDTH)], j,
                             jnp.zeros((CONV_CHUNK, C), F32))
            du_ref[pl.ds(pl.multiple_of(j * CONV_CHUNK, CONV_CHUNK), CONV_CHUNK), :] = acc
            return carry

        lax.fori_loop(0, tm // CONV_CHUNK, du_chunk, 0)

        for k0 in range(0, CONV_WIDTH, taps_at_once):
            taps = range(k0, min(k0 + taps_at_once, CONV_WIDTH))

            def dw_chunk(j, accs, taps=taps):
                dvc = dv_ref[pl.ds(pl.multiple_of(j * SUBLANES, SUBLANES), SUBLANES), :]
                return tuple(acc + dvc * _shifted(ush, pad + k, j) for acc, k in zip(accs, taps))

            accs = lax.fori_loop(0, tm // SUBLANES, dw_chunk, tuple(jnp.zeros((SUBLANES, C), F32) for _ in taps),
                                 unroll=4)
            for k, acc in zip(taps, accs):
                dwacc[k] += acc

        du = du_ref[...]
        a = z_ref[0].astype(F32)
        gate = z_ref[1].astype(F32)
        sg = _sigmoid(gate)
        da = du * sg
        dgate = du * a * sg * (1.0 - sg)
        dz_ref[0] = da.astype(BF16)
        dz_ref[1] = dgate.astype(BF16)
        db_ref[0:8, :] += _rows8(da)
        db_ref[8:16, :] += _rows8(dgate)

        @pl.when(i == nt - 1)
        def _():
            dw_ref[...] = jnp.zeros_like(dw_ref)
            for k in range(CONV_WIDTH):
                dw_ref[k:k + 1, :] = jnp.sum(dwacc[k], axis=0, keepdims=True)

    row = pl.BlockSpec((tm, C), lambda i: (i, 0))
    last_halo = T // CONV_HALO - 1
    return pl.pallas_call(
        body, name=name, grid=(nt,),
        in_specs=[row, pl.BlockSpec((CONV_HALO, C), lambda i: (jnp.minimum((i + 1) * hb, last_halo), 0)),
                  row, pl.BlockSpec((CONV_HALO, C), lambda i: (jnp.maximum(i * hb - 1, 0), 0)),
                  pl.BlockSpec((CONV_WIDTH, C), lambda i: (0, 0)),
                  pl.BlockSpec((2, tm, C), lambda i: (0, i, 0))],
        out_specs=[pl.BlockSpec((2, tm, C), lambda i: (0, i, 0)),
                   pl.BlockSpec((CONV_HALO, C), lambda i: (0, 0)),
                   pl.BlockSpec((16, C), lambda i: (0, 0))],
        out_shape=[jax.ShapeDtypeStruct((2, T, C), BF16), jax.ShapeDtypeStruct((CONV_HALO, C), F32),
                   jax.ShapeDtypeStruct((16, C), F32)],
        scratch_shapes=[pltpu.VMEM((SUBLANES, tm + CONV_HALO, C), F32), pltpu.VMEM((SUBLANES, CONV_HALO + tm, C), F32),
                        pltpu.VMEM((CONV_WIDTH, SUBLANES, C), F32), pltpu.VMEM((tm, C), F32),
                        pltpu.VMEM((CONV_WIDTH, SUBLANES, C), F32)],
        compiler_params=_params("arbitrary"),
    )(dv, dv, u, u, w, z3)


def _head_norm(raw, gain):
    xv = raw.astype(F32)
    r = lax.rsqrt(jnp.mean(xv * xv, axis=-1, keepdims=True) + NORM_EPS)
    xh = xv * r
    return (xh * gain).astype(BF16), xh, r


def _band_mask(n):
    row = lax.broadcasted_iota(I32, (ATTN_BLOCK, 2 * ATTN_BLOCK), 0)
    col = lax.broadcasted_iota(I32, (ATTN_BLOCK, 2 * ATTN_BLOCK), 1)
    return (col >= row) & (col <= row + ATTN_BLOCK) & ((col >= ATTN_BLOCK) | (n > 0))


def _attn_fwd(qkv, qg, kg, g, name):
    d = ATTN_DILATIONS[g]
    L = qkv.shape[0]
    HD = qkv.shape[1] // (3 * d)
    H = HD // HEAD_DIM
    nb = L // ATTN_BLOCK
    scale = HEAD_DIM ** -0.5

    def body(q_ref, kp_ref, kc_ref, vp_ref, vc_ref, qg_ref, kg_ref, o_ref, lse_ref, kk, vv):
        mask = _band_mask(pl.program_id(1))
        lane = lax.broadcasted_iota(I32, (ATTN_BLOCK, 128), 1)
        gq = qg_ref[g:g + 1, :]
        gk = kg_ref[g:g + 1, :]
        vv[0:ATTN_BLOCK, :] = vp_ref[...]
        vv[ATTN_BLOCK:, :] = vc_ref[...]
        lse_tile = jnp.zeros((ATTN_BLOCK, 128), F32)
        for h in range(H):
            sl = slice(h * HEAD_DIM, (h + 1) * HEAD_DIM)
            q = _head_norm(q_ref[:, sl], gq)[0]
            kk[0:ATTN_BLOCK, sl] = _head_norm(kp_ref[:, sl], gk)[0]
            kk[ATTN_BLOCK:, sl] = _head_norm(kc_ref[:, sl], gk)[0]
            s = jnp.where(mask, _dot_nt(q, kk[:, sl]) * scale, NEG_BIG)
            m = jnp.max(s, axis=-1, keepdims=True)
            p = jnp.exp(s - m)
            l = jnp.sum(p, axis=-1, keepdims=True)
            o_ref[:, sl] = _dot_nn(p.astype(BF16), vv[:, sl]) / l
            lse_tile = jnp.where(lane == h, m + jnp.log(l), lse_tile)
        lse_ref[...] = lse_tile

    def blk(which, prev):
        if prev:
            return pl.BlockSpec((ATTN_BLOCK, HD), lambda r, n: (jnp.maximum(n - 1, 0), r * 3 + which))
        return pl.BlockSpec((ATTN_BLOCK, HD), lambda r, n: (n, r * 3 + which))

    gains = pl.BlockSpec((3, HEAD_DIM), lambda r, n: (0, 0))
    return pl.pallas_call(
        body, name=name, grid=(d, nb),
        in_specs=[blk(0, False), blk(1, True), blk(1, False), blk(2, True), blk(2, False), gains, gains],
        out_specs=[pl.BlockSpec((ATTN_BLOCK, HD), lambda r, n: (n, r)),
                   pl.BlockSpec((ATTN_BLOCK, 128), lambda r, n: (n, r))],
        out_shape=[jax.ShapeDtypeStruct((L, d * HD), F32), jax.ShapeDtypeStruct((L, d * 128), F32)],
        scratch_shapes=[pltpu.VMEM((2 * ATTN_BLOCK, HD), BF16), pltpu.VMEM((2 * ATTN_BLOCK, HD), BF16)],
        compiler_params=_params("parallel", "parallel"),
    )(qkv, qkv, qkv, qkv, qkv, qg, kg)


def _attn_merge(os, lses, name):
    T = os[0].shape[0]
    HD = os[0].shape[1]
    H = HD // HEAD_DIM
    tm = min(T, STREAM_TILE)
    dils = ATTN_DILATIONS[1:]

    def body(o0, o1, o2, l0, l1, l2, ob_ref, of_ref, lse_ref, n1, n2, m1, m2):
        for d, src, dst, w in ((dils[0], o1, n1, HD), (dils[1], o2, n2, HD), (dils[0], l1, m1, 128), (dils[1], l2, m2, 128)):
            for s in range(d):
                for c in range(w // LANES):
                    dst.at[c][pl.ds(s, tm // d, stride=d), :] = src[:, s * w + c * LANES:s * w + (c + 1) * LANES]
        ls = [l0[...], m1[0], m2[0]]
        m = jnp.maximum(jnp.maximum(ls[0], ls[1]), ls[2])
        tot = m + jnp.log(jnp.exp(ls[0] - m) + jnp.exp(ls[1] - m) + jnp.exp(ls[2] - m))
        lse_ref[...] = tot
        ws = [jnp.exp(l - tot) for l in ls]
        for h in range(H):
            sl = slice(h * HEAD_DIM, (h + 1) * HEAD_DIM)
            out = (ws[0][:, h:h + 1] * o0[:, sl] + ws[1][:, h:h + 1] * n1[h]) + ws[2][:, h:h + 1] * n2[h]
            of_ref[:, sl] = out
            ob_ref[:, sl] = out.astype(BF16)

    def wide(d, w):
        return pl.BlockSpec((tm // d, d * w), lambda i: (i, 0))

    row = pl.BlockSpec((tm, HD), lambda i: (i, 0))
    nar = pl.BlockSpec((tm, 128), lambda i: (i, 0))
    return pl.pallas_call(
        body, name=name, grid=(T // tm,),
        in_specs=[row, wide(dils[0], HD), wide(dils[1], HD), nar, wide(dils[0], 128), wide(dils[1], 128)],
        out_specs=[row, row, nar],
        out_shape=[jax.ShapeDtypeStruct((T, HD), BF16), jax.ShapeDtypeStruct((T, HD), F32),
                   jax.ShapeDtypeStruct((T, 128), F32)],
        scratch_shapes=[pltpu.VMEM((H, tm, HEAD_DIM), F32), pltpu.VMEM((H, tm, HEAD_DIM), F32),
                        pltpu.VMEM((1, tm, 128), F32), pltpu.VMEM((1, tm, 128), F32)],
        compiler_params=_params("parallel"),
    )(*os, *lses)


def _attn_delta(out, dout, lse, name):
    T, HD = out.shape
    H = HD // HEAD_DIM
    tm = min(T, STREAM_TILE)
    dils = ATTN_DILATIONS[1:]

    def body(o_ref, d_ref, l_ref, dl_ref, *rest):
        dch = rest[-1]
        lane = lax.broadcasted_iota(I32, (tm, 128), 1)
        tile = jnp.zeros((tm, 128), F32)
        for h in range(H):
            sl = slice(h * HEAD_DIM, (h + 1) * HEAD_DIM)
            dch[h] = d_ref[:, sl]
            tile = jnp.where(lane == h, jnp.sum(o_ref[:, sl] * d_ref[:, sl], axis=-1, keepdims=True), tile)
        dl_ref[...] = tile
        for i, d in enumerate(dils):
            dw_ref, lw_ref, dlw_ref = rest[3 * i:3 * i + 3]
            for s in range(d):
                rows = pl.ds(s, tm // d, stride=d)
                for h in range(H):
                    dw_ref[:, s * HD + h * HEAD_DIM:s * HD + (h + 1) * HEAD_DIM] = dch.at[h][rows, :].astype(BF16)
                lw_ref[:, s * 128:(s + 1) * 128] = l_ref[rows, :]
                dlw_ref[:, s * 128:(s + 1) * 128] = dl_ref[rows, :]

    row = pl.BlockSpec((tm, HD), lambda i: (i, 0))
    nar = pl.BlockSpec((tm, 128), lambda i: (i, 0))
    out_specs, out_shape = [nar], [jax.ShapeDtypeStruct((T, 128), F32)]
    for d in dils:
        out_specs += [pl.BlockSpec((tm // d, d * w), lambda i: (i, 0)) for w in (HD, 128, 128)]
        out_shape += [jax.ShapeDtypeStruct((T // d, d * HD), BF16), jax.ShapeDtypeStruct((T // d, d * 128), F32),
                      jax.ShapeDtypeStruct((T // d, d * 128), F32)]
    return pl.pallas_call(
        body, name=name, grid=(T // tm,), in_specs=[row, row, nar], out_specs=out_specs, out_shape=out_shape,
        scratch_shapes=[pltpu.VMEM((H, tm, HEAD_DIM), F32)],
        compiler_params=_params("parallel"),
    )(out, dout, lse)


def _attn_bwd(qkv, dout, lse, delta, qg, kg, g, name):
    d = ATTN_DILATIONS[g]
    L = qkv.shape[0]
    HD = qkv.shape[1] // (3 * d)
    H = HD // HEAD_DIM
    nb = L // ATTN_BLOCK
    scale = HEAD_DIM ** -0.5

    def body(*refs):
        (q_ref, kp_ref, kc_ref, vp_ref, vc_ref, do_ref, lse_ref, dl_ref, qg_ref, kg_ref,
         out_ref, gg_ref, dq_a, dk_a, dv_a, dq_b, dk_b, dv_b, kk, vv) = refs
        r_, n = pl.program_id(0), pl.program_id(1)
        gq = qg_ref[g:g + 1, :]
        gk = kg_ref[g:g + 1, :]

        @pl.when((r_ == 0) & (n == 0))
        def _():
            gg_ref[...] = jnp.zeros_like(gg_ref)

        @pl.when(n == 0)
        def _():
            dk_a[...] = jnp.zeros_like(dk_a)
            dv_a[...] = jnp.zeros_like(dv_a)

        @pl.when(n < nb)
        def _():
            mask = _band_mask(n)
            vv[0:ATTN_BLOCK, :] = vp_ref[...]
            vv[ATTN_BLOCK:, :] = vc_ref[...]
            gq_sum = jnp.zeros((1, HEAD_DIM), F32)
            for h in range(H):
                sl = slice(h * HEAD_DIM, (h + 1) * HEAD_DIM)
                qn, qh, rq = _head_norm(q_ref[:, sl], gq)
                kk[0:ATTN_BLOCK, sl] = _head_norm(kp_ref[:, sl], gk)[0]
                kk[ATTN_BLOCK:, sl] = _head_norm(kc_ref[:, sl], gk)[0]
                kn = kk[:, sl]
                p = jnp.where(mask, jnp.exp(_dot_nt(qn, kn) * scale - lse_ref[:, h:h + 1]), 0.0)
                do = do_ref[:, sl].astype(BF16)
                ds = (p * (_dot_nt(do, vv[:, sl]) - dl_ref[:, h:h + 1]) * scale).astype(BF16)
                dqn = _dot_nn(ds, kn)
                dqh = dqn * gq
                dq_b[:, sl] = rq * (dqh - qh * jnp.mean(dqh * qh, axis=-1, keepdims=True))
                gq_sum = gq_sum + jnp.sum(dqn * qh, axis=0, keepdims=True)
                dkn = _dot_tn(ds, qn)
                dvn = _dot_tn(p.astype(BF16), do)
                dk_a[:, sl] += dkn[0:ATTN_BLOCK]
                dv_a[:, sl] += dvn[0:ATTN_BLOCK]
                dk_b[:, sl] = dkn[ATTN_BLOCK:]
                dv_b[:, sl] = dvn[ATTN_BLOCK:]
            gg_ref[0:1, :] += gq_sum

        @pl.when(n > 0)
        def _():
            out_ref[:, 0:HD] = dq_a[...].astype(BF16)
            gk_sum = jnp.zeros((1, HEAD_DIM), F32)
            for h in range(H):
                sl = slice(h * HEAD_DIM, (h + 1) * HEAD_DIM)
                _, kh, rk = _head_norm(kp_ref[:, sl], gk)
                dkn = dk_a[:, sl]
                dkh = dkn * gk
                dk = rk * (dkh - kh * jnp.mean(dkh * kh, axis=-1, keepdims=True))
                out_ref[:, HD + h * HEAD_DIM:HD + (h + 1) * HEAD_DIM] = dk.astype(BF16)
                gk_sum = gk_sum + jnp.sum(dkn * kh, axis=0, keepdims=True)
            out_ref[:, 2 * HD:3 * HD] = dv_a[...].astype(BF16)
            gg_ref[1:2, :] += gk_sum

        @pl.when(n < nb)
        def _():
            dq_a[...] = dq_b[...]
            dk_a[...] = dk_b[...]
            dv_a[...] = dv_b[...]

    last = nb - 1

    def blk(which, prev):
        if prev:
            return pl.BlockSpec((ATTN_BLOCK, HD), lambda r, n: (jnp.maximum(n - 1, 0), r * 3 + which))
        return pl.BlockSpec((ATTN_BLOCK, HD), lambda r, n: (jnp.minimum(n, last), r * 3 + which))

    gains = pl.BlockSpec((3, HEAD_DIM), lambda r, n: (0, 0))
    nar = pl.BlockSpec((ATTN_BLOCK, 128), lambda r, n: (jnp.minimum(n, last), r))
    acc = pltpu.VMEM((ATTN_BLOCK, HD), F32)
    return pl.pallas_call(
        body, name=name, grid=(d, nb + 1),
        in_specs=[blk(0, False), blk(1, True), blk(1, False), blk(2, True), blk(2, False),
                  pl.BlockSpec((ATTN_BLOCK, HD), lambda r, n: (jnp.minimum(n, last), r)), nar, nar, gains, gains],
        out_specs=[pl.BlockSpec((ATTN_BLOCK, 3 * HD), lambda r, n: (jnp.maximum(n - 1, 0), r)),
                   pl.BlockSpec((8, HEAD_DIM), lambda r, n: (0, 0))],
        out_shape=[jax.ShapeDtypeStruct((L, d * 3 * HD), BF16), jax.ShapeDtypeStruct((8, HEAD_DIM), F32)],
        scratch_shapes=[acc, acc, acc, acc, acc, acc,
                        pltpu.VMEM((2 * ATTN_BLOCK, HD), BF16), pltpu.VMEM((2 * ATTN_BLOCK, HD), BF16)],
        compiler_params=_params("arbitrary", "arbitrary"),
    )(qkv, qkv, qkv, qkv, qkv, dout, lse, delta, qg, kg)


def _adamw(w, grad, m, v, name):
    R, C = w.shape
    tr = _row_tile(R, 512)

    def body(w_ref, g_ref, m_ref, v_ref, d_ref, nm_ref, nv_ref):
        gv = g_ref[...]
        nm = ADAM_B1 * m_ref[...] + (1.0 - ADAM_B1) * gv
        nv = ADAM_B2 * v_ref[...] + (1.0 - ADAM_B2) * (gv * gv)
        m_hat = nm / (1.0 - ADAM_B1 ** ADAM_STEP)
        v_hat = nv / (1.0 - ADAM_B2 ** ADAM_STEP)
        d_ref[...] = -ADAM_LR * (m_hat / (jnp.sqrt(v_hat) + ADAM_EPS) + ADAM_WD * w_ref[...])
        nm_ref[...] = nm
        nv_ref[...] = nv

    blk = pl.BlockSpec((tr, C), lambda i: (i, 0))
    shp = jax.ShapeDtypeStruct((R, C), F32)
    return pl.pallas_call(
        body, name=name, grid=(R // tr,), in_specs=[blk] * 4, out_specs=[blk] * 3, out_shape=[shp] * 3,
        compiler_params=_params("parallel"),
    )(w, grad, m, v)


def _ffn_fwd(x, g, comm, s, tag, tgt=None):
    h = _rmsnorm(x, g, f"rmsnorm_{tag}")
    (win_t, wout), tok = comm.weights(s, h)
    z3, a = _gated_in(h, win_t, None, "swiglu", BF16, f"ffn_in_{tag}", after=tok)
    saved = (x, h, z3, a, win_t, wout)
    if tgt is None:
        return _rows_mm(a, wout, "res", f"ffn_out_{tag}", x=x, scale=0.5), saved
    return _rows_mm(a, wout, "loss", f"ffn_out_loss_{tag}", x=x, scale=0.5, tgt=tgt), saved


def _ffn_bwd(dxo, saved, g, comm, s, tag, after):
    x, h, z3, a, win_t, wout = saved
    dz3 = _swiglu_bwd(dxo, wout, z3, 0.5, f"ffn_out_bwd_{tag}", after=after)
    d_wout = _mm_tn(a, dxo, 0.5, f"ffn_dwout_{tag}")
    d_win_t = _mm_tn(dz3, h, 1.0, f"ffn_dwin_{tag}")
    tok = comm.grads(s, [d_win_t, d_wout])
    dx, dg8 = _rows_mm(dz3, win_t, "rmsbwd", f"ffn_in_bwd_{tag}", x=x, g=g, dxo=dxo, after=tok)
    return dx, dg8, ()


def _local_step(x, tgt, vecs, comm):
    norm_g = vecs["norm_g"]

    def gvec(i):
        return norm_g[i:i + 1]

    x1, sv_a0 = _ffn_fwd(x, gvec(0), comm, 0, "l0a")
    h_c = _rmsnorm(x1, gvec(1), "rmsnorm_conv")
    (pw1_t, pw2), tok = comm.weights(1, h_c)
    zc3, u = _gated_in(h_c, pw1_t, vecs["b_pw1"], "glu", F32, "conv_pw1", after=tok)
    v, s = _dwconv_fwd(u, vecs["w_dw"], vecs["b_dw"], vecs["cg"], "conv_dw")
    x2 = _rows_mm(s, pw2, "res", "conv_pw2", x=x1, bias=vecs["b_pw2"])
    x3, sv_b0 = _ffn_fwd(x2, gvec(2), comm, 2, "l0b")
    x4, sv_a1 = _ffn_fwd(x3, gvec(3), comm, 3, "l1a")
    h_s = _rmsnorm_streams(x4, gvec(4), "rmsnorm_attn")
    (qkv_t, wo), tok = comm.weights(4, h_s[0])
    hd3 = qkv_t.shape[0] // 3
    qkvs, os, lses = [], [], []
    for g, d in enumerate(ATTN_DILATIONS):
        qkv = _mm_nt(h_s[g], qkv_t, BF16, f"attn_qkv_g{g}", w_row0=g * hd3, n_rows=hd3, streams=d, after=tok)
        o, lse = _attn_fwd(qkv, vecs["q_norm"], vecs["k_norm"], g, f"attn_fwd_g{g}")
        qkvs.append(qkv)
        os.append(o)
        lses.append(lse)
    out_b, out_f, lse_tot = _attn_merge(os, lses, "attn_merge")
    x5 = _rows_mm(out_b, wo, "res", "attn_wo", x=x4)
    (dy, loss8), sv_b1 = _ffn_fwd(x5, gvec(5), comm, 5, "l1b", tgt=tgt)
    loss = 0.5 * jnp.sum(loss8) / x.shape[1]

    dg = [None] * 6
    dx, dg[5], tok = _ffn_bwd(dy, sv_b1, gvec(5), comm, 5, "l1b", ())
    dout = _mm_nt(dx, wo, F32, "attn_wo_bwd", after=tok)
    d_wo = _mm_tn(out_b, dx, 1.0, "attn_dwo")
    delta, *wide = _attn_delta(out_f, dout, lse_tot, "attn_delta")
    per_group = [(dout, lse_tot, delta), tuple(wide[0:3]), tuple(wide[3:6])]
    ggs, d_qkv_t, dhs = [], [], []
    for g, d in enumerate(ATTN_DILATIONS):
        dqkv, gg = _attn_bwd(qkvs[g], *per_group[g], vecs["q_norm"], vecs["k_norm"], g, f"attn_bwd_g{g}")
        ggs.append(gg)
        d_qkv_t.append(_mm_tn(dqkv, h_s[g], 1.0, f"attn_dwqkv_g{g}", streams=d))
        if d > 1:
            dhs.append((d, _rows_mm(dqkv, qkv_t, "plain", f"attn_qkv_bwd_g{g}", b_row0=g * hd3, streams=d)))
        else:
            dqkv0 = dqkv
    tok = comm.grads(4, [jnp.concatenate(d_qkv_t), d_wo])
    dx, dg[4] = _rows_mm(dqkv0, qkv_t, "rmsbwd", "attn_qkv_bwd_g0", x=x4, g=gvec(4), dxo=dx, extras=dhs, after=tok)
    dx, dg[3], tok = _ffn_bwd(dx, sv_a1, gvec(3), comm, 3, "l1a", ())
    dx, dg[2], tok = _ffn_bwd(dx, sv_b0, gvec(2), comm, 2, "l0b", tok)
    ds = _mm_nt(dx, pw2, F32, "conv_pw2_bwd", after=tok)
    d_pw2 = _mm_tn(s, dx, 1.0, "conv_dwpw2")
    dv, dcg8, dbdw8, dbpw2_8 = _conv_post_bwd(ds, v, vecs["cg"], dx, "conv_post_bwd")
    dzc3, d_wdw, db1_16 = _dwconv_bwd(dv, u, vecs["w_dw"], zc3, "conv_dw_bwd")
    d_pw1_t = _mm_tn(dzc3, h_c, 1.0, "conv_dwpw1")
    tok = comm.grads(1, [d_pw1_t, d_pw2])
    dx, dg[1] = _rows_mm(dzc3, pw1_t, "rmsbwd", "conv_pw1_bwd", x=x1, g=gvec(1), dxo=dx, after=tok)
    dx, dg[0], tok = _ffn_bwd(dx, sv_a0, gvec(0), comm, 0, "l0a", ())

    D = x.shape[1]
    small = {
        "norm_g": jnp.stack([p.sum(axis=0) for p in dg]),
        "b_pw1": db1_16.reshape(2, 8, D).sum(axis=1),
        "w_dw": d_wdw[:CONV_WIDTH],
        "b_dw": dbdw8.sum(axis=0, keepdims=True),
        "cg": dcg8.sum(axis=0, keepdims=True),
        "b_pw2": dbpw2_8.sum(axis=0, keepdims=True),
        "q_norm": jnp.stack([gg[0] for gg in ggs]),
        "k_norm": jnp.stack([gg[1] for gg in ggs]),
    }
    return loss, dx, small, tok


def _mesh_pos():
    return lax.axis_index("x"), lax.axis_index("y"), lax.axis_index("c")


def _other_chips(x, y):
    return [(1 - x, y), (x, 1 - y), (1 - x, 1 - y)]


HBM = pl.BlockSpec(memory_space=pltpu.HBM)
SEM = pl.BlockSpec(memory_space=pltpu.SEMAPHORE)
SPLIT_COPY = pltpu.SideEffectType.DATAFLOW_SIDE_EFFECTING


def _hbm(a):
    return pltpu.with_memory_space_constraint(a, pltpu.HBM)


def _gather_copies(j, src, out, send, recv, x, y, c):
    me = 4 * x + 2 * y + c
    peers = [(x, y, 1 - c)] + [(px, py, c) for px, py in _other_chips(x, y)]
    return [pltpu.make_async_remote_copy(src_ref=src, dst_ref=out.at[me], send_sem=send.at[4 * j + k],
                                         recv_sem=recv.at[4 * j + k], device_id=peer, device_id_type=MESH)
            for k, peer in enumerate(peers)]


def _gather_start(shards, after, name):
    n = len(shards)

    def body(*refs):
        ins, outs = refs[:n], refs[n + len(after):2 * n + len(after)]
        send, recv, token = refs[2 * n + len(after) + n:3 * n + len(after) + 3]
        bufs, local_sem = refs[-n - 1:-1], refs[-1]
        x, y, c = _mesh_pos()
        for i in range(n):
            cp = pltpu.make_async_copy(ins[i], bufs[i], local_sem)
            cp.start()
            cp.wait()
            cp = pltpu.make_async_copy(bufs[i], outs[i].at[4 * x + 2 * y + c], local_sem)
            cp.start()
            cp.wait()
        for j in range(n):
            for cp in _gather_copies(j, ins[j], outs[j], send, recv, x, y, c):
                cp.start()
        token[...] = jnp.zeros_like(token)

    res = pl.pallas_call(
        body, name=name, in_specs=[HBM] * n + [ANY] * len(after),
        out_specs=[HBM] * (2 * n) + [SEM, SEM, pl.BlockSpec(memory_space=pltpu.VMEM)],
        out_shape=[pltpu.HBM((N_DEV,) + s.shape, s.dtype) for s in shards] + [pltpu.HBM(s.shape, s.dtype) for s in shards]
        + [pltpu.SemaphoreType.DMA((4 * n,)), pltpu.SemaphoreType.DMA((4 * n,)), jax.ShapeDtypeStruct((8, 128), F32)],
        input_output_aliases={i: n + i for i in range(n)},
        scratch_shapes=[pltpu.VMEM(s.shape, s.dtype) for s in shards] + [pltpu.SemaphoreType.DMA(())],
        compiler_params=pltpu.CompilerParams(has_side_effects=SPLIT_COPY),
    )(*[_hbm(s) for s in shards], *after)
    return res[:n], res[n:2 * n], res[2 * n], res[2 * n + 1], res[2 * n + 2]


def _gather_wait(outs, thru, send, recv, after, name):
    n = len(outs)

    def body(*refs):
        out_refs, thru_refs, send_ref, recv_ref = refs[:n], refs[n:2 * n], refs[2 * n], refs[2 * n + 1]
        x, y, c = _mesh_pos()
        for j in range(n):
            for cp in _gather_copies(j, thru_refs[j], out_refs[j], send_ref, recv_ref, x, y, c):
                cp.wait_send()
                cp.wait_recv()

    res = pl.pallas_call(
        body, name=name, in_specs=[HBM] * (2 * n) + [SEM, SEM, ANY], out_specs=[HBM] * (2 * n),
        out_shape=[pltpu.HBM(a.shape, a.dtype) for a in list(outs) + list(thru)],
        input_output_aliases={i: i for i in range(2 * n)},
        compiler_params=pltpu.CompilerParams(has_side_effects=SPLIT_COPY),
    )(*outs, *thru, send, recv, after)
    return res[:n]


def _gather_forward(outs, name):
    n = len(outs)

    def body(*refs):
        o = refs[n:2 * n]
        send_sems, recv_sems = refs[2 * n:]
        x, y, c = _mesh_pos()
        copies = []
        for j in range(n):
            for k, (px, py) in enumerate(_other_chips(x, y)):
                blk = o[j].at[4 * px + 2 * py + c]
                cp = pltpu.make_async_remote_copy(src_ref=blk, dst_ref=blk, send_sem=send_sems.at[j, k],
                                                  recv_sem=recv_sems.at[j, k], device_id=(x, y, 1 - c), device_id_type=MESH)
                cp.start()
                copies.append(cp)
        for cp in copies:
            cp.wait()

    return pl.pallas_call(
        body, name=name, in_specs=[ANY] * n, out_specs=[ANY] * n,
        out_shape=[jax.ShapeDtypeStruct(a.shape, a.dtype) for a in outs],
        input_output_aliases={i: i for i in range(n)},
        scratch_shapes=[pltpu.SemaphoreType.DMA((n, 3)), pltpu.SemaphoreType.DMA((n, 3))],
    )(*outs)


def _all_sum(p, name):
    R, C = p.shape

    def body(p_ref, o_ref, buf, send_sems, recv_sems):
        x, y, c = _mesh_pos()
        me = 4 * x + 2 * y + c
        buf[me] = p_ref[...]
        copies = []
        for rel in range(1, N_DEV):
            peer = (1 - x if rel & 4 else x, 1 - y if rel & 2 else y, 1 - c if rel & 1 else c)
            cp = pltpu.make_async_remote_copy(
                src_ref=p_ref, dst_ref=buf.at[me], send_sem=send_sems.at[rel - 1], recv_sem=recv_sems.at[rel - 1],
                device_id=peer, device_id_type=MESH)
            cp.start()
            copies.append(cp)
        for cp in copies:
            cp.wait()
        acc = buf[0]
        for dev in range(1, N_DEV):
            acc = acc + buf[dev]
        o_ref[...] = acc

    vm = pl.BlockSpec(memory_space=pltpu.VMEM)
    return pl.pallas_call(
        body, name=name, in_specs=[vm], out_specs=vm, out_shape=jax.ShapeDtypeStruct((R, C), F32),
        scratch_shapes=[pltpu.VMEM((N_DEV, R, C), F32), pltpu.SemaphoreType.DMA((N_DEV - 1,)),
                        pltpu.SemaphoreType.DMA((N_DEV - 1,))],
    )(p)


def _swap_with_sibling(gs, name):
    n = len(gs)

    def body(*refs):
        ins, outs = refs[:n], refs[n:2 * n]
        send_sems, recv_sems = refs[2 * n:]
        x, y, c = _mesh_pos()
        copies = []
        for i in range(n):
            for k in range(4):
                cp = pltpu.make_async_remote_copy(
                    src_ref=ins[i].at[2 * k + (1 - c)], dst_ref=outs[i].at[k],
                    send_sem=send_sems.at[i, k], recv_sem=recv_sems.at[i, k],
                    device_id=(x, y, 1 - c), device_id_type=MESH)
                cp.start()
                copies.append(cp)
        for cp in copies:
            cp.wait()

    return pl.pallas_call(
        body, name=name, in_specs=[ANY] * n, out_specs=[ANY] * n,
        out_shape=[jax.ShapeDtypeStruct((4,) + g.shape[1:], g.dtype) for g in gs],
        scratch_shapes=[pltpu.SemaphoreType.DMA((n, 4)), pltpu.SemaphoreType.DMA((n, 4))],
    )(*gs)


def _scatter_copies(j, src, land, send, recv, x, y, c):
    return [pltpu.make_async_remote_copy(src_ref=src.at[2 * px + py], dst_ref=land.at[k], send_sem=send.at[3 * j + k],
                                         recv_sem=recv.at[3 * j + k], device_id=(px, py, c), device_id_type=MESH)
            for k, (px, py) in enumerate(_other_chips(x, y))]


def _scatter_start(ps, name):
    n = len(ps)

    def body(*refs):
        ins, lands = refs[:n], refs[2 * n:3 * n]
        send, recv, token = refs[3 * n:]
        x, y, c = _mesh_pos()
        for j in range(n):
            for cp in _scatter_copies(j, ins[j], lands[j], send, recv, x, y, c):
                cp.start()
        token[...] = jnp.zeros_like(token)

    res = pl.pallas_call(
        body, name=name, in_specs=[HBM] * n,
        out_specs=[HBM] * (2 * n) + [SEM, SEM, pl.BlockSpec(memory_space=pltpu.VMEM)],
        out_shape=[pltpu.HBM(p.shape, p.dtype) for p in ps] + [pltpu.HBM((3,) + p.shape[1:], p.dtype) for p in ps]
        + [pltpu.SemaphoreType.DMA((3 * n,)), pltpu.SemaphoreType.DMA((3 * n,)), jax.ShapeDtypeStruct((8, 128), F32)],
        input_output_aliases={i: i for i in range(n)},
        compiler_params=pltpu.CompilerParams(has_side_effects=SPLIT_COPY),
    )(*[_hbm(p) for p in ps])
    return res[:n], res[n:2 * n], res[2 * n], res[2 * n + 1], res[2 * n + 2]


def _scatter_wait(thru, lands, send, recv, after, name):
    n = len(thru)

    def body(*refs):
        thru_refs, land_refs, send_ref, recv_ref = refs[:n], refs[n:2 * n], refs[2 * n], refs[2 * n + 1]
        x, y, c = _mesh_pos()
        for j in range(n):
            for cp in _scatter_copies(j, thru_refs[j], land_refs[j], send_ref, recv_ref, x, y, c):
                cp.wait_send()
                cp.wait_recv()

    res = pl.pallas_call(
        body, name=name, in_specs=[HBM] * (2 * n) + [SEM, SEM, ANY], out_specs=[HBM] * (2 * n),
        out_shape=[pltpu.HBM(a.shape, a.dtype) for a in list(thru) + list(lands)],
        input_output_aliases={i: i for i in range(2 * n)},
        compiler_params=pltpu.CompilerParams(has_side_effects=SPLIT_COPY),
    )(*thru, *lands, send, recv, after)
    return res[:n], res[n:]


def _row_tile(r, pref):
    if r <= pref:
        return r
    for t in range(pref - pref % 16, 0, -16):
        if r % t == 0:
            return t
    return r


def _add_sibling(g, r1, pos, name):
    _, r, D = g.shape
    tr = _row_tile(r, 512)

    def body(pos_ref, g_ref, r_ref, o_ref):
        o_ref[...] = (g_ref[...].astype(F32) + r_ref[...].astype(F32)).astype(BF16)

    return pl.pallas_call(
        body, name=name,
        grid_spec=pltpu.PrefetchScalarGridSpec(
            num_scalar_prefetch=1, grid=(4, r // tr),
            in_specs=[pl.BlockSpec((None, None, tr, D), lambda k, j, pos: (k, pos[0], j, 0)),
                      pl.BlockSpec((None, tr, D), lambda k, j, pos: (k, j, 0))],
            out_specs=pl.BlockSpec((None, tr, D), lambda k, j, pos: (k, j, 0))),
        out_shape=jax.ShapeDtypeStruct((4, r, D), BF16),
        compiler_params=_params("parallel", "parallel"),
    )(pos, g.reshape(4, 2, r, D), r1)


def _add_chips(p, r2, pos, name):
    _, r, D = p.shape
    tr = _row_tile(r, 512)

    def body(pos_ref, p_ref, r_ref, o_ref):
        acc = p_ref[...].astype(F32)
        for j in range(3):
            acc = acc + r_ref[j].astype(F32)
        o_ref[...] = acc

    return pl.pallas_call(
        body, name=name,
        grid_spec=pltpu.PrefetchScalarGridSpec(
            num_scalar_prefetch=1, grid=(r // tr,),
            in_specs=[pl.BlockSpec((None, tr, D), lambda j, pos: (pos[1], j, 0)),
                      pl.BlockSpec((3, tr, D), lambda j, pos: (0, j, 0))],
            out_specs=pl.BlockSpec((tr, D), lambda j, pos: (j, 0))),
        out_shape=jax.ShapeDtypeStruct((r, D), F32),
        compiler_params=_params("parallel"),
    )(pos, p, r2)


def _place_cols(local, me, width):
    R, w = local.shape
    return lax.dynamic_update_slice(jnp.zeros((R, width), F32), local, (0, me * w))


def _pad_rows(a, rows):
    return jnp.pad(a, ((0, rows - a.shape[0]), (0, 0)))


SUBLAYER_SHARDS = ((0, 4), (8, 9), (1, 5), (2, 6), (10, 11), (3, 7))


class _StepComm:
    def __init__(self, shards, pos):
        self.pos = pos
        self.shards = shards
        self.gathers = {}
        tok = self._start_gather(0, ())
        self._start_gather(1, tok)
        self.pending = {}

    def _start_gather(self, s, after):
        outs, thru, send, recv, token = _gather_start([self.shards[i] for i in SUBLAYER_SHARDS[s]], after,
                                                      f"gather_start_{s}")
        self.gathers[s] = (outs, thru, send, recv)
        return (token,)

    def weights(self, s, after):
        outs = _gather_wait(*self.gathers.pop(s), after, f"gather_wait_{s}")
        outs = _gather_forward(outs, f"gather_forward_{s}")
        tok = self._start_gather(s + 2, (outs[0],)) if s + 2 < len(SUBLAYER_SHARDS) else ()
        return [w.reshape(-1, w.shape[-1]) for w in outs], tok

    def grads(self, s, mats):
        g3 = [g.reshape(N_DEV, g.shape[0] // N_DEV, g.shape[1]) for g in mats]
        r1 = _swap_with_sibling(g3, f"rs_sibling_{s}")
        ps = [_add_sibling(g, r, self.pos, f"rs_add_sibling_{s}_{i}") for i, (g, r) in enumerate(zip(g3, r1))]
        thru, lands, send, recv, token = _scatter_start(ps, f"rs_start_{s}")
        self.pending[s] = (thru, lands, send, recv)
        return (token,)

    def finish(self, after):
        out = {}
        for s in sorted(self.pending, reverse=True):
            ps, lands = _scatter_wait(*self.pending[s], after, f"rs_wait_{s}")
            for i, p, r2 in zip(SUBLAYER_SHARDS[s], ps, lands):
                out[i] = _add_chips(p, r2, self.pos, f"rs_add_chips_{i}")
        return out


def kernel(x, norm_g, ffn_w_in, ffn_w_out, conv_w_pw1, conv_b_pw1, conv_w_dw, conv_b_dw, conv_norm_g, conv_w_pw2, conv_b_pw2, attn_w_qkv, attn_q_norm, attn_k_norm, attn_w_o, loss_target, m_norm_g, m_ffn_w_in, m_ffn_w_out, m_conv_w_pw1, m_conv_b_pw1, m_conv_w_dw, m_conv_b_dw, m_conv_norm_g, m_conv_w_pw2, m_conv_b_pw2, m_attn_w_qkv, m_attn_q_norm, m_attn_k_norm, m_attn_w_o, v_norm_g, v_ffn_w_in, v_ffn_w_out, v_conv_w_pw1, v_conv_b_pw1, v_conv_w_dw, v_conv_b_dw, v_conv_norm_g, v_conv_w_pw2, v_conv_b_pw2, v_attn_w_qkv, v_attn_q_norm, v_attn_k_norm, v_attn_w_o):
    T, D = x.shape[1], x.shape[2]
    xi, yi, ci = _mesh_pos()
    me = 4 * xi + 2 * yi + ci
    pos = jnp.stack([ci, 2 * xi + yi]).astype(I32)
    lf = [(l, f) for l in range(2) for f in range(2)]

    shards = ([ffn_w_in[l, f].T.astype(BF16) for l, f in lf] + [ffn_w_out[l, f].astype(BF16) for l, f in lf]
              + [conv_w_pw1[0].T.astype(BF16), conv_w_pw2[0].astype(BF16),
                 attn_w_qkv[0].T.astype(BF16), attn_w_o[0].astype(BF16)])
    comm = _StepComm(shards, pos)
    w_cols = norm_g.shape[2]
    vec = _all_sum(_pad_rows(jnp.concatenate([_place_cols(norm_g.reshape(6, w_cols), me, D),
                                              _place_cols(conv_w_dw[0], me, D)]), 40), "gather_vectors")
    vecs = {
        "norm_g": vec[0:6], "w_dw": vec[6:6 + CONV_WIDTH],
        "b_pw1": conv_b_pw1.reshape(2, D), "b_dw": conv_b_dw, "cg": conv_norm_g, "b_pw2": conv_b_pw2,
        "q_norm": attn_q_norm[0], "k_norm": attn_k_norm[0],
    }

    loss_local, dx, small, tok = _local_step(x[0], loss_target[0], vecs, comm)
    loss = lax.psum(loss_local, ("x", "y", "c"))

    mats = comm.finish(dx)
    hd3 = 3 * HEAD_DIM
    packed = jnp.concatenate([
        small["norm_g"], small["b_pw1"], small["w_dw"], small["b_dw"], small["cg"], small["b_pw2"],
        jnp.pad(small["q_norm"].reshape(1, hd3), ((0, 0), (0, D - hd3))),
        jnp.pad(small["k_norm"].reshape(1, hd3), ((0, 0), (0, D - hd3)))])
    red = _all_sum(_pad_rows(packed, 48), "reduce_vectors")
    col0 = me * w_cols
    grads = {
        "norm_g": lax.dynamic_slice(red[0:6], (0, col0), (6, w_cols)),
        "ffn_w_in": jnp.concatenate([mats[i].T for i in range(4)]),
        "ffn_w_out": jnp.concatenate([mats[i] for i in range(4, 8)]),
        "conv_w_pw1": mats[8].T,
        "conv_b_pw1": red[6:8].reshape(1, 2 * D),
        "conv_w_dw": lax.dynamic_slice(red[8:8 + CONV_WIDTH], (0, col0), (CONV_WIDTH, w_cols)),
        "conv_b_dw": red[39:40], "conv_norm_g": red[40:41], "conv_w_pw2": mats[9], "conv_b_pw2": red[41:42],
        "attn_w_qkv": mats[10].T,
        "attn_q_norm": red[42, :hd3].reshape(3, HEAD_DIM), "attn_k_norm": red[43, :hd3].reshape(3, HEAD_DIM),
        "attn_w_o": mats[11],
    }

    names = ["norm_g", "ffn_w_in", "ffn_w_out", "conv_w_pw1", "conv_b_pw1", "conv_w_dw", "conv_b_dw",
             "conv_norm_g", "conv_w_pw2", "conv_b_pw2", "attn_w_qkv", "attn_q_norm", "attn_k_norm", "attn_w_o"]
    ws = [norm_g, ffn_w_in, ffn_w_out, conv_w_pw1, conv_b_pw1, conv_w_dw, conv_b_dw, conv_norm_g, conv_w_pw2,
          conv_b_pw2, attn_w_qkv, attn_q_norm, attn_k_norm, attn_w_o]
    ms = [m_norm_g, m_ffn_w_in, m_ffn_w_out, m_conv_w_pw1, m_conv_b_pw1, m_conv_w_dw, m_conv_b_dw, m_conv_norm_g,
          m_conv_w_pw2, m_conv_b_pw2, m_attn_w_qkv, m_attn_q_norm, m_attn_k_norm, m_attn_w_o]
    vs = [v_norm_g, v_ffn_w_in, v_ffn_w_out, v_conv_w_pw1, v_conv_b_pw1, v_conv_w_dw, v_conv_b_dw, v_conv_norm_g,
          v_conv_w_pw2, v_conv_b_pw2, v_attn_w_qkv, v_attn_q_norm, v_attn_k_norm, v_attn_w_o]
    g_out, d_out, m_out, v_out = [], [], [], []
    for name, w, m, v in zip(names, ws, ms, vs):
        shape2 = (-1, w.shape[-1])
        g2 = grads[name].reshape(shape2)
        delta, new_m, new_v = _adamw(w.reshape(shape2), g2, m.reshape(shape2), v.reshape(shape2), f"adamw_{name}")
        g_out.append(g2.reshape(w.shape))
        d_out.append(delta.reshape(w.shape))
        m_out.append(new_m.reshape(w.shape))
        v_out.append(new_v.reshape(w.shape))
    return (loss, dx.reshape(x.shape), *g_out, *d_out, *m_out, *v_out)
```

```python
import functools

import jax
import jax.numpy as jnp
from jax import lax
from jax.experimental import pallas as pl
from jax.experimental.pallas import tpu as pltpu

F32 = jnp.float32
BF16 = jnp.bfloat16
I32 = jnp.int32

NORM_EPS = 1e-6
LANES = 128
SUBLANES = 8
MXU_WIDTH = 256
HEAD_DIM = 128
ATTN_BLOCK = 128
ATTN_DILATIONS = (1, 4, 16)
CONV_WIDTH = 31
CONV_HALO = 32
CONV_CHUNK = 16
STREAM_TILE = 512
ADAM_LR, ADAM_B1, ADAM_B2, ADAM_EPS, ADAM_WD, ADAM_STEP = 0.001, 0.9, 0.999, 1e-08, 0.01, 10
N_DEV = 8
SMALL_PARAM = 8192
NEG_BIG = -1e30
V7X_VMEM_BYTES = 64 * 1024 * 1024
VMEM_LIMIT = V7X_VMEM_BYTES - 8 * 1024 * 1024
MESH = pl.DeviceIdType.MESH
ANY = pl.BlockSpec(memory_space=pl.ANY)


def _params(*sem):
    return pltpu.CompilerParams(dimension_semantics=sem or None, vmem_limit_bytes=VMEM_LIMIT)


def _dot_nn(a, b):
    return lax.dot_general(a, b, (((1,), (0,)), ((), ())), preferred_element_type=F32)


def _dot_nt(a, b):
    return lax.dot_general(a, b, (((1,), (1,)), ((), ())), preferred_element_type=F32)


def _dot_tn(a, b):
    return lax.dot_general(a, b, (((0,), (0,)), ((), ())), preferred_element_type=F32)


def _sigmoid(x):
    return 1.0 / (1.0 + jnp.exp(-x))


def _tile(n, pref):
    if n <= pref:
        return n
    for t in range(pref - pref % 128, 0, -128):
        if n % t == 0:
            return t
    return n


def _to_lane_chunks(dst, val):
    for c in range(dst.shape[0]):
        dst[c] = val[:, c * LANES:(c + 1) * LANES]


def _rows8(v):
    tm, c = v.shape
    return v.reshape(tm // 8, 8, c).sum(axis=0)


def _rmsnorm(x, g, name, after=()):
    T, D = x.shape
    tm = min(T, 512)

    def body(x_ref, g_ref, *rest):
        h_ref = rest[-1]
        xv = x_ref[...]
        r = lax.rsqrt(jnp.mean(xv * xv, axis=-1, keepdims=True) + NORM_EPS)
        h_ref[...] = (xv * r * g_ref[...]).astype(BF16)

    return pl.pallas_call(
        body, name=name, grid=(T // tm,),
        in_specs=[pl.BlockSpec((tm, D), lambda i: (i, 0)), pl.BlockSpec((1, D), lambda i: (0, 0))] + [ANY] * len(after),
        out_specs=pl.BlockSpec((tm, D), lambda i: (i, 0)),
        out_shape=jax.ShapeDtypeStruct((T, D), BF16),
        compiler_params=_params("parallel"),
    )(x, g, *after)


def _rmsnorm_streams(x, g, name):
    T, D = x.shape
    tm = min(T, STREAM_TILE)
    dils = [d for d in ATTN_DILATIONS if d > 1]

    def body(x_ref, g_ref, h_ref, *rest):
        outs, hs = rest[:-1], rest[-1]
        xv = x_ref[...]
        r = lax.rsqrt(jnp.mean(xv * xv, axis=-1, keepdims=True) + NORM_EPS)
        hf = xv * r * g_ref[...]
        h_ref[...] = hf.astype(BF16)
        _to_lane_chunks(hs, hf)
        for d, o_ref in zip(dils, outs):
            for s in range(d):
                for c in range(D // LANES):
                    o_ref[s, :, c * LANES:(c + 1) * LANES] = hs.at[c][pl.ds(s, tm // d, stride=d), :].astype(BF16)

    res = pl.pallas_call(
        body, name=name, grid=(T // tm,),
        in_specs=[pl.BlockSpec((tm, D), lambda i: (i, 0)), pl.BlockSpec((1, D), lambda i: (0, 0))],
        out_specs=[pl.BlockSpec((tm, D), lambda i: (i, 0))]
        + [pl.BlockSpec((d, tm // d, D), lambda i: (0, i, 0)) for d in dils],
        out_shape=[jax.ShapeDtypeStruct((T, D), BF16)] + [jax.ShapeDtypeStruct((d, T // d, D), BF16) for d in dils],
        scratch_shapes=[pltpu.VMEM((D // LANES, tm, LANES), F32)],
        compiler_params=_params("parallel"),
    )(x, g)
    return [res[0]] + [a.reshape(T, D) for a in res[1:]]


def _gated_in(h, wt, bias, act, u_dtype, name, after=()):
    T, D = h.shape
    F = wt.shape[0] // 2
    tn = _tile(F, 1408)
    tm = min(T, 512)
    nn = F // tn

    def body(*refs):
        z_ref, u_ref = refs[-2:]
        if bias is None:
            h_ref, w0_ref, w1_ref = refs[:3]
        else:
            h_ref, w0_ref, w1_ref, b_ref = refs[:4]
        hv = h_ref[...]
        for c0 in range(0, tn, MXU_WIDTH):
            cs = slice(c0, min(c0 + MXU_WIDTH, tn))
            z0 = _dot_nt(hv, w0_ref[cs, :])
            z1 = _dot_nt(hv, w1_ref[cs, :])
            if bias is not None:
                z0 = z0 + b_ref[0:1, cs]
                z1 = z1 + b_ref[1:2, cs]
            z_ref[0, :, cs] = z0.astype(BF16)
            z_ref[1, :, cs] = z1.astype(BF16)
            if act == "swiglu":
                u = z0 * _sigmoid(z0) * z1
            else:
                u = z0 * _sigmoid(z1)
            u_ref[:, cs] = u.astype(u_dtype)

    in_specs = [pl.BlockSpec((tm, D), lambda n, m: (m, 0)),
                pl.BlockSpec((tn, D), lambda n, m: (n, 0)),
                pl.BlockSpec((tn, D), lambda n, m: (n + nn, 0))]
    args = [h, wt, wt]
    if bias is not None:
        in_specs.append(pl.BlockSpec((2, tn), lambda n, m: (0, n)))
        args.append(bias)
    in_specs += [ANY] * len(after)
    args += list(after)
    return pl.pallas_call(
        body, name=name, grid=(nn, T // tm), in_specs=in_specs,
        out_specs=[pl.BlockSpec((2, tm, tn), lambda n, m: (0, m, n)), pl.BlockSpec((tm, tn), lambda n, m: (m, n))],
        out_shape=[jax.ShapeDtypeStruct((2, T, F), BF16), jax.ShapeDtypeStruct((T, F), u_dtype)],
        compiler_params=_params("parallel", "parallel"),
    )(*args)


def _swiglu_bwd(dxo, wout, z3, scale, name, after=()):
    T, D = dxo.shape
    F = wout.shape[0]
    tn = _tile(F, 1408)
    tm = min(T, 512)

    def body(d_ref, w_ref, z_ref, *rest):
        dz_ref = rest[-1]
        dy = (d_ref[...] * scale).astype(BF16)
        da = _dot_nt(dy, w_ref[...])
        gate = z_ref[0].astype(F32)
        up = z_ref[1].astype(F32)
        sg = _sigmoid(gate)
        dz_ref[0] = (da * up * (sg * (1.0 + gate * (1.0 - sg)))).astype(BF16)
        dz_ref[1] = (da * gate * sg).astype(BF16)

    return pl.pallas_call(
        body, name=name, grid=(F // tn, T // tm),
        in_specs=[pl.BlockSpec((tm, D), lambda n, m: (m, 0)),
                  pl.BlockSpec((tn, D), lambda n, m: (n, 0)),
                  pl.BlockSpec((2, tm, tn), lambda n, m: (0, m, n))] + [ANY] * len(after),
        out_specs=pl.BlockSpec((2, tm, tn), lambda n, m: (0, m, n)),
        out_shape=jax.ShapeDtypeStruct((2, T, F), BF16),
        compiler_params=_params("parallel", "parallel"),
    )(dxo, wout, z3, *after)


def _mm_nt(a, wt, out_dtype, name, *, w_row0=0, n_rows=None, streams=1, after=()):
    T, K = a.shape
    N = wt.shape[0] if n_rows is None else n_rows
    L = T // streams
    tn = _tile(N, 1152)
    tm = min(L, 512)
    mps = L // tm
    npn = N // tn
    assert w_row0 % tn == 0
    wb0 = w_row0 // tn

    def body(a_ref, w_ref, *rest):
        o_ref = rest[-1]
        o_ref[...] = _dot_nt(a_ref[...].astype(BF16), w_ref[...]).astype(out_dtype)

    return pl.pallas_call(
        body, name=name, grid=(npn, T // tm),
        in_specs=[pl.BlockSpec((tm, K), lambda n, m: (m, 0)), pl.BlockSpec((tn, K), lambda n, m: (wb0 + n, 0))]
        + [ANY] * len(after),
        out_specs=pl.BlockSpec((tm, tn), lambda n, m: (m % mps, (m // mps) * npn + n)),
        out_shape=jax.ShapeDtypeStruct((L, streams * N), out_dtype),
        compiler_params=_params("parallel", "parallel"),
    )(a, wt, *after)


def _mm_tn(a, b, b_scale, name, *, streams=1):
    if a.ndim == 3:
        J, T, F = a.shape
    elif streams > 1:
        J, T, F = 1, a.shape[0] * streams, a.shape[1] // streams
    else:
        (T, F), J = a.shape, 1
    N = b.shape[1]
    tmm = _tile(F, 1408)
    tk = min(T // streams, 1024)
    npj = F // tmm
    nk = T // tk
    kps = nk // streams

    def body(a_ref, b_ref, o_ref, acc_ref):
        k = pl.program_id(1)
        bv = b_ref[...]
        if b_scale != 1.0:
            bv = bv * b_scale
        part = _dot_tn(a_ref[...].astype(BF16), bv.astype(BF16))

        @pl.when(k == 0)
        def _():
            acc_ref[...] = part

        @pl.when(k > 0)
        def _():
            acc_ref[...] += part

        @pl.when(k == nk - 1)
        def _():
            o_ref[...] = acc_ref[...].astype(BF16)

    if a.ndim == 3:
        a_spec = pl.BlockSpec((None, tk, tmm), lambda i, k: (i // npj, k, i % npj))
    elif streams > 1:
        a_spec = pl.BlockSpec((tk, tmm), lambda i, k: (k % kps, (k // kps) * npj + i))
    else:
        a_spec = pl.BlockSpec((tk, tmm), lambda i, k: (k, i))
    return pl.pallas_call(
        body, name=name, grid=(J * npj, nk),
        in_specs=[a_spec, pl.BlockSpec((tk, N), lambda i, k: (k, 0))],
        out_specs=pl.BlockSpec((tmm, N), lambda i, k: (i, 0)),
        out_shape=jax.ShapeDtypeStruct((J * F, N), BF16),
        scratch_shapes=[pltpu.VMEM((tmm, N), F32)],
        compiler_params=_params("parallel", "arbitrary"),
    )(a, b)


def _rows_mm(a, b, mode, name, *, x=None, scale=1.0, bias=None, tgt=None, g=None, dxo=None,
             b_row0=0, streams=1, extras=(), after=()):
    if a.ndim == 3:
        J, T, F = a.shape
    elif streams > 1:
        J, T, F = 1, a.shape[0] * streams, a.shape[1] // streams
    else:
        (T, F), J = a.shape, 1
    D = b.shape[1]
    tk = _tile(F, 2816)
    kpj = F // tk
    nk = J * kpj
    tm = min(T // streams, STREAM_TILE)
    nm = T // tm
    mps = nm // streams
    assert b_row0 % tk == 0
    kb0 = b_row0 // tk
    n_ex = len(extras)

    def body(*refs):
        refs = list(refs)
        a_ref, b_ref = refs[:2]
        rest = refs[2 + len(after):]
        if n_ex:
            tmp_ref = rest.pop()
        acc_ref = rest.pop()
        ex_refs, rest = rest[:n_ex], rest[n_ex:]
        m, k = pl.program_id(0), pl.program_id(1)
        part = _dot_nn(a_ref[...], b_ref[...])

        @pl.when(k == 0)
        def _():
            acc_ref[...] = part

        @pl.when(k > 0)
        def _():
            acc_ref[...] += part

        @pl.when(k == nk - 1)
        def _():
            acc = acc_ref[...]
            for (d, _), e_ref in zip(extras, ex_refs):
                for s in range(d):
                    for c in range(D // LANES):
                        tmp_ref.at[c][pl.ds(s, tm // d, stride=d), :] = e_ref[s, :, c * LANES:(c + 1) * LANES]
                acc = acc + jnp.concatenate([tmp_ref[c] for c in range(D // LANES)], axis=1)
            if mode == "plain":
                (o_ref,) = rest
                o_ref[...] = acc
            elif mode == "res":
                ins = list(rest)
                x_ref = ins.pop(0)
                if bias is not None:
                    acc = acc + ins.pop(0)[...]
                xn = x_ref[...] + scale * acc
                if g is None:
                    (o_ref,) = ins
                else:
                    g_ref, o_ref, h_ref = ins
                    r = lax.rsqrt(jnp.mean(xn * xn, axis=-1, keepdims=True) + NORM_EPS)
                    h_ref[...] = (xn * r * g_ref[...]).astype(BF16)
                o_ref[...] = xn
            elif mode == "loss":
                x_ref, t_ref, o_ref, l_ref = rest
                e = x_ref[...] + scale * acc - t_ref[...]
                o_ref[...] = e / D

                @pl.when(m == 0)
                def _():
                    l_ref[...] = jnp.zeros_like(l_ref)

                l_ref[...] += _rows8(e * e)
            else:
                x_ref, g_ref, d_ref, o_ref, dg_ref = rest
                xv = x_ref[...]
                r = lax.rsqrt(jnp.mean(xv * xv, axis=-1, keepdims=True) + NORM_EPS)
                xh = xv * r
                dxh = acc * g_ref[...]
                o_ref[...] = d_ref[...] + r * (dxh - xh * jnp.mean(dxh * xh, axis=-1, keepdims=True))

                @pl.when(m == 0)
                def _():
                    dg_ref[...] = jnp.zeros_like(dg_ref)

                dg_ref[...] += _rows8(acc * xh)

    if a.ndim == 3:
        a_spec = pl.BlockSpec((None, tm, tk), lambda m, k: (k // kpj, m, k % kpj))
    elif streams > 1:
        a_spec = pl.BlockSpec((tm, tk), lambda m, k: (m % mps, (m // mps) * kpj + k))
    else:
        a_spec = pl.BlockSpec((tm, tk), lambda m, k: (m, k))
    row = pl.BlockSpec((tm, D), lambda m, k: (m, 0))
    vec = pl.BlockSpec((1, D), lambda m, k: (0, 0))
    part8 = pl.BlockSpec((8, D), lambda m, k: (0, 0))
    in_specs = [a_spec, pl.BlockSpec((tk, D), lambda m, k: (kb0 + k, 0))] + [ANY] * len(after)
    args = [a, b, *after]
    for d, e in extras:
        in_specs.append(pl.BlockSpec((d, tm // d, D), lambda m, k: (0, m, 0)))
        args.append(e.reshape(d, T // d, D))
    full = jax.ShapeDtypeStruct((T, D), F32)
    small = jax.ShapeDtypeStruct((8, D), F32)
    if mode == "plain":
        out_specs, out_shape = row, full
    elif mode == "res":
        in_specs.append(row)
        args.append(x)
        if bias is not None:
            in_specs.append(vec)
            args.append(bias)
        if g is None:
            out_specs, out_shape = row, full
        else:
            in_specs.append(vec)
            args.append(g)
            out_specs, out_shape = [row, row], [full, jax.ShapeDtypeStruct((T, D), BF16)]
    elif mode == "loss":
        in_specs += [row, row]
        args += [x, tgt]
        out_specs, out_shape = [row, part8], [full, small]
    else:
        in_specs += [row, vec, row]
        args += [x, g, dxo]
        out_specs, out_shape = [row, part8], [full, small]
    return pl.pallas_call(
        body, name=name, grid=(nm, nk), in_specs=in_specs, out_specs=out_specs, out_shape=out_shape,
        scratch_shapes=[pltpu.VMEM((tm, D), F32)] + ([pltpu.VMEM((D // LANES, tm, LANES), F32)] if n_ex else []),
        compiler_params=_params("arbitrary", "arbitrary"),
    )(*args)


def _sublane_phases(sh):
    rows = sh.shape[1] - SUBLANES
    for p in range(1, SUBLANES):
        sh[p, pl.ds(0, rows), :] = sh[0, pl.ds(p, rows), :]


def _shifted(sh, offset, j, rows=SUBLANES):
    start = pl.multiple_of(j * rows + (offset - offset % SUBLANES), SUBLANES)
    return sh[offset % SUBLANES, pl.ds(start, rows), :]


def _conv_taps(wb, sh, offsets, j, init):
    groups = CONV_CHUNK // SUBLANES
    acc = init
    for k, off in enumerate(offsets):
        rows = _shifted(sh, off, j, CONV_CHUNK).reshape(groups, SUBLANES, -1)
        acc = acc + (wb[k] * rows).reshape(CONV_CHUNK, -1)
    return acc


def _broadcast_rows(dst, src):
    for r in range(dst.shape[0]):
        dst[r] = jnp.zeros(dst.shape[1:], F32) + src[r:r + 1, :]


def _dwconv_fwd(u, w, b_dw, cg, name):
    T, C = u.shape
    tm = min(T, 256)
    hb = tm // CONV_HALO
    pad = CONV_HALO - (CONV_WIDTH - 1)

    def body(u_ref, halo_ref, w_ref, b_ref, g_ref, v_ref, s_ref, sh, wb):
        i = pl.program_id(0)
        sh[0, pl.ds(0, CONV_HALO), :] = jnp.where(i > 0, halo_ref[...], 0.0)
        sh[0, pl.ds(CONV_HALO, tm), :] = u_ref[...]
        _sublane_phases(sh)
        _broadcast_rows(wb, w_ref)
        bias = jnp.zeros((CONV_CHUNK, C), F32) + b_ref[...]

        def chunk(j, carry):
            acc = _conv_taps(wb, sh, [pad + k for k in range(CONV_WIDTH)], j, bias)
            v_ref[pl.ds(pl.multiple_of(j * CONV_CHUNK, CONV_CHUNK), CONV_CHUNK), :] = acc
            return carry

        lax.fori_loop(0, tm // CONV_CHUNK, chunk, 0)
        acc = v_ref[...]
        r = lax.rsqrt(jnp.mean(acc * acc, axis=-1, keepdims=True) + NORM_EPS)
        n = acc * r * g_ref[...]
        s_ref[...] = (n * _sigmoid(n)).astype(BF16)

    row = pl.BlockSpec((tm, C), lambda i: (i, 0))
    vec = pl.BlockSpec((1, C), lambda i: (0, 0))
    return pl.pallas_call(
        body, name=name, grid=(T // tm,),
        in_specs=[row, pl.BlockSpec((CONV_HALO, C), lambda i: (jnp.maximum(i * hb - 1, 0), 0)),
                  pl.BlockSpec((CONV_WIDTH, C), lambda i: (0, 0)), vec, vec],
        out_specs=[row, row],
        out_shape=[jax.ShapeDtypeStruct((T, C), F32), jax.ShapeDtypeStruct((T, C), BF16)],
        scratch_shapes=[pltpu.VMEM((SUBLANES, CONV_HALO + tm, C), F32), pltpu.VMEM((CONV_WIDTH, SUBLANES, C), F32)],
        compiler_params=_params("parallel"),
    )(u, u, w, b_dw, cg)


def _conv_post_bwd(ds, v, cg, dxo, name):
    T, C = v.shape
    tm = min(T, 512)

    def body(ds_ref, v_ref, g_ref, d_ref, dv_ref, dcg_ref, dbdw_ref, dbpw2_ref):
        @pl.when(pl.program_id(0) == 0)
        def _():
            dcg_ref[...] = jnp.zeros_like(dcg_ref)
            dbdw_ref[...] = jnp.zeros_like(dbdw_ref)
            dbpw2_ref[...] = jnp.zeros_like(dbpw2_ref)

        vv = v_ref[...]
        r = lax.rsqrt(jnp.mean(vv * vv, axis=-1, keepdims=True) + NORM_EPS)
        vh = vv * r
        n = vh * g_ref[...]
        sg = _sigmoid(n)
        dn = ds_ref[...] * (sg * (1.0 + n * (1.0 - sg)))
        dvh = dn * g_ref[...]
        dv = r * (dvh - vh * jnp.mean(dvh * vh, axis=-1, keepdims=True))
        dv_ref[...] = dv
        dcg_ref[...] += _rows8(dn * vh)
        dbdw_ref[...] += _rows8(dv)
        dbpw2_ref[...] += _rows8(d_ref[...])

    row = pl.BlockSpec((tm, C), lambda i: (i, 0))
    part8 = pl.BlockSpec((8, C), lambda i: (0, 0))
    small = jax.ShapeDtypeStruct((8, C), F32)
    return pl.pallas_call(
        body, name=name, grid=(T // tm,),
        in_specs=[row, row, pl.BlockSpec((1, C), lambda i: (0, 0)), row],
        out_specs=[row, part8, part8, part8],
        out_shape=[jax.ShapeDtypeStruct((T, C), F32), small, small, small],
        compiler_params=_params("arbitrary"),
    )(ds, v, cg, dxo)


def _dwconv_bwd(dv, u, w, z3, name):
    T, C = u.shape
    tm = min(T, 256)
    hb = tm // CONV_HALO
    nt = T // tm
    pad = CONV_HALO - (CONV_WIDTH - 1)

    taps_at_once = 4

    def body(dv_ref, dvn_ref, u_ref, up_ref, w_ref, z_ref, dz_ref, dw_ref, db_ref, dvsh, ush, wb, du_ref, dwacc):
        i = pl.program_id(0)

        @pl.when(i == 0)
        def _():
            dwacc[...] = jnp.zeros_like(dwacc)
            db_ref[...] = jnp.zeros_like(db_ref)

        dvsh[0, pl.ds(0, tm), :] = dv_ref[...]
        dvsh[0, pl.ds(tm, CONV_HALO), :] = jnp.where(i < nt - 1, dvn_ref[...], 0.0)
        ush[0, pl.ds(0, CONV_HALO), :] = jnp.where(i > 0, up_ref[...], 0.0)
        ush[0, pl.ds(CONV_HALO, tm), :] = u_ref[...]
        _sublane_phases(dvsh)
        _sublane_phases(ush)
        _broadcast_rows(wb, w_ref)

        def du_chunk(j, carry):
            acc = _conv_taps(wb, dvsh, [CONV_WIDTH - 1 - k for k in range(CONV_WIDTH)], j,
                             jnp.zeros((CONV_CHUNK, C), F32))
            du_ref[pl.ds(pl.multiple_of(j * CONV_CHUNK, CONV_CHUNK), CONV_CHUNK), :] = acc
            return carry

        lax.fori_loop(0, tm // CONV_CHUNK, du_chunk, 0)

        for k0 in range(0, CONV_WIDTH, taps_at_once):
            taps = range(k0, min(k0 + taps_at_once, CONV_WIDTH))

            def dw_chunk(j, accs, taps=taps):
                dvc = dv_ref[pl.ds(pl.multiple_of(j * SUBLANES, SUBLANES), SUBLANES), :]
                return tuple(acc + dvc * _shifted(ush, pad + k, j) for acc, k in zip(accs, taps))

            accs = lax.fori_loop(0, tm // SUBLANES, dw_chunk, tuple(jnp.zeros((SUBLANES, C), F32) for _ in taps),
                                 unroll=4)
            for k, acc in zip(taps, accs):
                dwacc[k] += acc

        du = du_ref[...]
        a = z_ref[0].astype(F32)
        gate = z_ref[1].astype(F32)
        sg = _sigmoid(gate)
        da = du * sg
        dgate = du * a * sg * (1.0 - sg)
        dz_ref[0] = da.astype(BF16)
        dz_ref[1] = dgate.astype(BF16)
        db_ref[0:8, :] += _rows8(da)
        db_ref[8:16, :] += _rows8(dgate)

        @pl.when(i == nt - 1)
        def _():
            dw_ref[...] = jnp.zeros_like(dw_ref)
            for k in range(CONV_WIDTH):
                dw_ref[k:k + 1, :] = jnp.sum(dwacc[k], axis=0, keepdims=True)

    row = pl.BlockSpec((tm, C), lambda i: (i, 0))
    last_halo = T // CONV_HALO - 1
    return pl.pallas_call(
        body, name=name, grid=(nt,),
        in_specs=[row, pl.BlockSpec((CONV_HALO, C), lambda i: (jnp.minimum((i + 1) * hb, last_halo), 0)),
                  row, pl.BlockSpec((CONV_HALO, C), lambda i: (jnp.maximum(i * hb - 1, 0), 0)),
                  pl.BlockSpec((CONV_WIDTH, C), lambda i: (0, 0)),
                  pl.BlockSpec((2, tm, C), lambda i: (0, i, 0))],
        out_specs=[pl.BlockSpec((2, tm, C), lambda i: (0, i, 0)),
                   pl.BlockSpec((CONV_HALO, C), lambda i: (0, 0)),
                   pl.BlockSpec((16, C), lambda i: (0, 0))],
        out_shape=[jax.ShapeDtypeStruct((2, T, C), BF16), jax.ShapeDtypeStruct((CONV_HALO, C), F32),
                   jax.ShapeDtypeStruct((16, C), F32)],
        scratch_shapes=[pltpu.VMEM((SUBLANES, tm + CONV_HALO, C), F32), pltpu.VMEM((SUBLANES, CONV_HALO + tm, C), F32),
                        pltpu.VMEM((CONV_WIDTH, SUBLANES, C), F32), pltpu.VMEM((tm, C), F32),
                        pltpu.VMEM((CONV_WIDTH, SUBLANES, C), F32)],
        compiler_params=_params("arbitrary"),
    )(dv, dv, u, u, w, z3)


def _head_norm(raw, gain):
    xv = raw.astype(F32)
    r = lax.rsqrt(jnp.mean(xv * xv, axis=-1, keepdims=True) + NORM_EPS)
    xh = xv * r
    return (xh * gain).astype(BF16), xh, r


def _band_mask(n):
    row = lax.broadcasted_iota(I32, (ATTN_BLOCK, 2 * ATTN_BLOCK), 0)
    col = lax.broadcasted_iota(I32, (ATTN_BLOCK, 2 * ATTN_BLOCK), 1)
    return (col >= row) & (col <= row + ATTN_BLOCK) & ((col >= ATTN_BLOCK) | (n > 0))


def _attn_fwd(qkv, qg, kg, g, name):
    d = ATTN_DILATIONS[g]
    L = qkv.shape[0]
    HD = qkv.shape[1] // (3 * d)
    H = HD // HEAD_DIM
    nb = L // ATTN_BLOCK
    scale = HEAD_DIM ** -0.5

    def body(q_ref, kp_ref, kc_ref, vp_ref, vc_ref, qg_ref, kg_ref, o_ref, lse_ref, kk, vv):
        mask = _band_mask(pl.program_id(1))
        lane = lax.broadcasted_iota(I32, (ATTN_BLOCK, 128), 1)
        gq = qg_ref[g:g + 1, :]
        gk = kg_ref[g:g + 1, :]
        vv[0:ATTN_BLOCK, :] = vp_ref[...]
        vv[ATTN_BLOCK:, :] = vc_ref[...]
        lse_tile = jnp.zeros((ATTN_BLOCK, 128), F32)
        for h in range(H):
            sl = slice(h * HEAD_DIM, (h + 1) * HEAD_DIM)
            q = _head_norm(q_ref[:, sl], gq)[0]
            kk[0:ATTN_BLOCK, sl] = _head_norm(kp_ref[:, sl], gk)[0]
            kk[ATTN_BLOCK:, sl] = _head_norm(kc_ref[:, sl], gk)[0]
            s = jnp.where(mask, _dot_nt(q, kk[:, sl]) * scale, NEG_BIG)
            m = jnp.max(s, axis=-1, keepdims=True)
            p = jnp.exp(s - m)
            l = jnp.sum(p, axis=-1, keepdims=True)
            o_ref[:, sl] = _dot_nn(p.astype(BF16), vv[:, sl]) / l
            lse_tile = jnp.where(lane == h, m + jnp.log(l), lse_tile)
        lse_ref[...] = lse_tile

    def blk(which, prev):
        if prev:
            return pl.BlockSpec((ATTN_BLOCK, HD), lambda r, n: (jnp.maximum(n - 1, 0), r * 3 + which))
        return pl.BlockSpec((ATTN_BLOCK, HD), lambda r, n: (n, r * 3 + which))

    gains = pl.BlockSpec((3, HEAD_DIM), lambda r, n: (0, 0))
    return pl.pallas_call(
        body, name=name, grid=(d, nb),
        in_specs=[blk(0, False), blk(1, True), blk(1, False), blk(2, True), blk(2, False), gains, gains],
        out_specs=[pl.BlockSpec((ATTN_BLOCK, HD), lambda r, n: (n, r)),
                   pl.BlockSpec((ATTN_BLOCK, 128), lambda r, n: (n, r))],
        out_shape=[jax.ShapeDtypeStruct((L, d * HD), F32), jax.ShapeDtypeStruct((L, d * 128), F32)],
        scratch_shapes=[pltpu.VMEM((2 * ATTN_BLOCK, HD), BF16), pltpu.VMEM((2 * ATTN_BLOCK, HD), BF16)],
        compiler_params=_params("parallel", "parallel"),
    )(qkv, qkv, qkv, qkv, qkv, qg, kg)


def _attn_merge(os, lses, name):
    T = os[0].shape[0]
    HD = os[0].shape[1]
    H = HD // HEAD_DIM
    tm = min(T, STREAM_TILE)
    dils = ATTN_DILATIONS[1:]

    def body(o0, o1, o2, l0, l1, l2, ob_ref, of_ref, lse_ref, n1, n2, m1, m2):
        for d, src, dst, w in ((dils[0], o1, n1, HD), (dils[1], o2, n2, HD), (dils[0], l1, m1, 128), (dils[1], l2, m2, 128)):
            for s in range(d):
                for c in range(w // LANES):
                    dst.at[c][pl.ds(s, tm // d, stride=d), :] = src[:, s * w + c * LANES:s * w + (c + 1) * LANES]
        ls = [l0[...], m1[0], m2[0]]
        m = jnp.maximum(jnp.maximum(ls[0], ls[1]), ls[2])
        tot = m + jnp.log(jnp.exp(ls[0] - m) + jnp.exp(ls[1] - m) + jnp.exp(ls[2] - m))
        lse_ref[...] = tot
        ws = [jnp.exp(l - tot) for l in ls]
        for h in range(H):
            sl = slice(h * HEAD_DIM, (h + 1) * HEAD_DIM)
            out = (ws[0][:, h:h + 1] * o0[:, sl] + ws[1][:, h:h + 1] * n1[h]) + ws[2][:, h:h + 1] * n2[h]
            of_ref[:, sl] = out
            ob_ref[:, sl] = out.astype(BF16)

    def wide(d, w):
        return pl.BlockSpec((tm // d, d * w), lambda i: (i, 0))

    row = pl.BlockSpec((tm, HD), lambda i: (i, 0))
    nar = pl.BlockSpec((tm, 128), lambda i: (i, 0))
    return pl.pallas_call(
        body, name=name, grid=(T // tm,),
        in_specs=[row, wide(dils[0], HD), wide(dils[1], HD), nar, wide(dils[0], 128), wide(dils[1], 128)],
        out_specs=[row, row, nar],
        out_shape=[jax.ShapeDtypeStruct((T, HD), BF16), jax.ShapeDtypeStruct((T, HD), F32),
                   jax.ShapeDtypeStruct((T, 128), F32)],
        scratch_shapes=[pltpu.VMEM((H, tm, HEAD_DIM), F32), pltpu.VMEM((H, tm, HEAD_DIM), F32),
                        pltpu.VMEM((1, tm, 128), F32), pltpu.VMEM((1, tm, 128), F32)],
        compiler_params=_params("parallel"),
    )(*os, *lses)


def _attn_delta(out, dout, lse, name):
    T, HD = out.shape
    H = HD // HEAD_DIM
    tm = min(T, STREAM_TILE)
    dils = ATTN_DILATIONS[1:]

    def body(o_ref, d_ref, l_ref, dl_ref, *rest):
        dch = rest[-1]
        lane = lax.broadcasted_iota(I32, (tm, 128), 1)
        tile = jnp.zeros((tm, 128), F32)
        for h in range(H):
            sl = slice(h * HEAD_DIM, (h + 1) * HEAD_DIM)
            dch[h] = d_ref[:, sl]
            tile = jnp.where(lane == h, jnp.sum(o_ref[:, sl] * d_ref[:, sl], axis=-1, keepdims=True), tile)
        dl_ref[...] = tile
        for i, d in enumerate(dils):
            dw_ref, lw_ref, dlw_ref = rest[3 * i:3 * i + 3]
            for s in range(d):
                rows = pl.ds(s, tm // d, stride=d)
                for h in range(H):
                    dw_ref[:, s * HD + h * HEAD_DIM:s * HD + (h + 1) * HEAD_DIM] = dch.at[h][rows, :].astype(BF16)
                lw_ref[:, s * 128:(s + 1) * 128] = l_ref[rows, :]
                dlw_ref[:, s * 128:(s + 1) * 128] = dl_ref[rows, :]

    row = pl.BlockSpec((tm, HD), lambda i: (i, 0))
    nar = pl.BlockSpec((tm, 128), lambda i: (i, 0))
    out_specs, out_shape = [nar], [jax.ShapeDtypeStruct((T, 128), F32)]
    for d in dils:
        out_specs += [pl.BlockSpec((tm // d, d * w), lambda i: (i, 0)) for w in (HD, 128, 128)]
        out_shape += [jax.ShapeDtypeStruct((T // d, d * HD), BF16), jax.ShapeDtypeStruct((T // d, d * 128), F32),
                      jax.ShapeDtypeStruct((T // d, d * 128), F32)]
    return pl.pallas_call(
        body, name=name, grid=(T // tm,), in_specs=[row, row, nar], out_specs=out_specs, out_shape=out_shape,
        scratch_shapes=[pltpu.VMEM((H, tm, HEAD_DIM), F32)],
        compiler_params=_params("parallel"),
    )(out, dout, lse)


def _attn_bwd(qkv, dout, lse, delta, qg, kg, g, name):
    d = ATTN_DILATIONS[g]
    L = qkv.shape[0]
    HD = qkv.shape[1] // (3 * d)
    H = HD // HEAD_DIM
    nb = L // ATTN_BLOCK
    scale = HEAD_DIM ** -0.5

    def body(*refs):
        (q_ref, kp_ref, kc_ref, vp_ref, vc_ref, do_ref, lse_ref, dl_ref, qg_ref, kg_ref,
         out_ref, gg_ref, dq_a, dk_a, dv_a, dq_b, dk_b, dv_b, kk, vv) = refs
        r_, n = pl.program_id(0), pl.program_id(1)
        gq = qg_ref[g:g + 1, :]
        gk = kg_ref[g:g + 1, :]

        @pl.when((r_ == 0) & (n == 0))
        def _():
            gg_ref[...] = jnp.zeros_like(gg_ref)

        @pl.when(n == 0)
        def _():
            dk_a[...] = jnp.zeros_like(dk_a)
            dv_a[...] = jnp.zeros_like(dv_a)

        @pl.when(n < nb)
        def _():
            mask = _band_mask(n)
            vv[0:ATTN_BLOCK, :] = vp_ref[...]
            vv[ATTN_BLOCK:, :] = vc_ref[...]
            gq_sum = jnp.zeros((1, HEAD_DIM), F32)
            for h in range(H):
                sl = slice(h * HEAD_DIM, (h + 1) * HEAD_DIM)
                qn, qh, rq = _head_norm(q_ref[:, sl], gq)
                kk[0:ATTN_BLOCK, sl] = _head_norm(kp_ref[:, sl], gk)[0]
                kk[ATTN_BLOCK:, sl] = _head_norm(kc_ref[:, sl], gk)[0]
                kn = kk[:, sl]
                p = jnp.where(mask, jnp.exp(_dot_nt(qn, kn) * scale - lse_ref[:, h:h + 1]), 0.0)
                do = do_ref[:, sl].astype(BF16)
                ds = (p * (_dot_nt(do, vv[:, sl]) - dl_ref[:, h:h + 1]) * scale).astype(BF16)
                dqn = _dot_nn(ds, kn)
                dqh = dqn * gq
                dq_b[:, sl] = rq * (dqh - qh * jnp.mean(dqh * qh, axis=-1, keepdims=True))
                gq_sum = gq_sum + jnp.sum(dqn * qh, axis=0, keepdims=True)
                dkn = _dot_tn(ds, qn)
                dvn = _dot_tn(p.astype(BF16), do)
                dk_a[:, sl] += dkn[0:ATTN_BLOCK]
                dv_a[:, sl] += dvn[0:ATTN_BLOCK]
                dk_b[:, sl] = dkn[ATTN_BLOCK:]
                dv_b[:, sl] = dvn[ATTN_BLOCK:]
            gg_ref[0:1, :] += gq_sum

        @pl.when(n > 0)
        def _():
            out_ref[:, 0:HD] = dq_a[...].astype(BF16)
            gk_sum = jnp.zeros((1, HEAD_DIM), F32)
            for h in range(H):
                sl = slice(h * HEAD_DIM, (h + 1) * HEAD_DIM)
                _, kh, rk = _head_norm(kp_ref[:, sl], gk)
                dkn = dk_a[:, sl]
                dkh = dkn * gk
                dk = rk * (dkh - kh * jnp.mean(dkh * kh, axis=-1, keepdims=True))
                out_ref[:, HD + h * HEAD_DIM:HD + (h + 1) * HEAD_DIM] = dk.astype(BF16)
                gk_sum = gk_sum + jnp.sum(dkn * kh, axis=0, keepdims=True)
            out_ref[:, 2 * HD:3 * HD] = dv_a[...].astype(BF16)
            gg_ref[1:2, :] += gk_sum

        @pl.when(n < nb)
        def _():
            dq_a[...] = dq_b[...]
            dk_a[...] = dk_b[...]
            dv_a[...] = dv_b[...]

    last = nb - 1

    def blk(which, prev):
        if prev:
            return pl.BlockSpec((ATTN_BLOCK, HD), lambda r, n: (jnp.maximum(n - 1, 0), r * 3 + which))
        return pl.BlockSpec((ATTN_BLOCK, HD), lambda r, n: (jnp.minimum(n, last), r * 3 + which))

    gains = pl.BlockSpec((3, HEAD_DIM), lambda r, n: (0, 0))
    nar = pl.BlockSpec((ATTN_BLOCK, 128), lambda r, n: (jnp.minimum(n, last), r))
    acc = pltpu.VMEM((ATTN_BLOCK, HD), F32)
    return pl.pallas_call(
        body, name=name, grid=(d, nb + 1),
        in_specs=[blk(0, False), blk(1, True), blk(1, False), blk(2, True), blk(2, False),
                  pl.BlockSpec((ATTN_BLOCK, HD), lambda r, n: (jnp.minimum(n, last), r)), nar, nar, gains, gains],
        out_specs=[pl.BlockSpec((ATTN_BLOCK, 3 * HD), lambda r, n: (jnp.maximum(n - 1, 0), r)),
                   pl.BlockSpec((8, HEAD_DIM), lambda r, n: (0, 0))],
        out_shape=[jax.ShapeDtypeStruct((L, d * 3 * HD), BF16), jax.ShapeDtypeStruct((8, HEAD_DIM), F32)],
        scratch_shapes=[acc, acc, acc, acc, acc, acc,
                        pltpu.VMEM((2 * ATTN_BLOCK, HD), BF16), pltpu.VMEM((2 * ATTN_BLOCK, HD), BF16)],
        compiler_params=_params("arbitrary", "arbitrary"),
    )(qkv, qkv, qkv, qkv, qkv, dout, lse, delta, qg, kg)


def _adamw(w, grad, m, v, name):
    if w.ndim == 2:
        R, C = w.shape
        tr = _row_tile(R, 512)
        blk, steps = pl.BlockSpec((tr, C), lambda i: (i, 0)), R // tr
    else:
        zeros = (0,) * w.ndim
        blk, steps = pl.BlockSpec(w.shape, lambda i: zeros), 1

    def body(w_ref, g_ref, m_ref, v_ref, d_ref, nm_ref, nv_ref):
        gv = g_ref[...]
        nm = ADAM_B1 * m_ref[...] + (1.0 - ADAM_B1) * gv
        nv = ADAM_B2 * v_ref[...] + (1.0 - ADAM_B2) * (gv * gv)
        m_hat = nm / (1.0 - ADAM_B1 ** ADAM_STEP)
        v_hat = nv / (1.0 - ADAM_B2 ** ADAM_STEP)
        d_ref[...] = -ADAM_LR * (m_hat / (jnp.sqrt(v_hat) + ADAM_EPS) + ADAM_WD * w_ref[...])
        nm_ref[...] = nm
        nv_ref[...] = nv

    shp = jax.ShapeDtypeStruct(w.shape, F32)
    return pl.pallas_call(
        body, name=name, grid=(steps,), in_specs=[blk] * 4, out_specs=[blk] * 3, out_shape=[shp] * 3,
        compiler_params=_params("parallel"),
    )(w, grad, m, v)


def _ffn_fwd(x, h, comm, s, tag, tgt=None, next_g=None):
    (win_t, wout), tok = comm.weights(s, h)
    z3, a = _gated_in(h, win_t, None, "swiglu", BF16, f"ffn_in_{tag}", after=tok)
    saved = (x, h, z3, a, win_t, wout)
    if tgt is None:
        return _rows_mm(a, wout, "res", f"ffn_out_{tag}", x=x, scale=0.5, g=next_g), saved
    return _rows_mm(a, wout, "loss", f"ffn_out_loss_{tag}", x=x, scale=0.5, tgt=tgt), saved


def _ffn_bwd(dxo, saved, g, comm, s, tag, after):
    x, h, z3, a, win_t, wout = saved
    dz3 = _swiglu_bwd(dxo, wout, z3, 0.5, f"ffn_out_bwd_{tag}", after=after)
    d_wout = _mm_tn(a, dxo, 0.5, f"ffn_dwout_{tag}")
    d_win_t = _mm_tn(dz3, h, 1.0, f"ffn_dwin_{tag}")
    tok = comm.grads(s, [d_win_t, d_wout])
    dx, dg8 = _rows_mm(dz3, win_t, "rmsbwd", f"ffn_in_bwd_{tag}", x=x, g=g, dxo=dxo, after=tok)
    return dx, dg8, ()


def _local_step(x, tgt, vecs, comm):
    norm_g = vecs["norm_g"]

    def gvec(i):
        return norm_g[i:i + 1]

    h0 = _rmsnorm(x, gvec(0), "rmsnorm_l0a", after=comm.started)
    (x1, h_c), sv_a0 = _ffn_fwd(x, h0, comm, 0, "l0a", next_g=gvec(1))
    (pw1_t, pw2), tok = comm.weights(1, h_c)
    zc3, u = _gated_in(h_c, pw1_t, vecs["b_pw1"], "glu", F32, "conv_pw1", after=tok)
    v, s = _dwconv_fwd(u, vecs["w_dw"], vecs["b_dw"], vecs["cg"], "conv_dw")
    x2, h_b0 = _rows_mm(s, pw2, "res", "conv_pw2", x=x1, bias=vecs["b_pw2"], g=gvec(2))
    (x3, h_a1), sv_b0 = _ffn_fwd(x2, h_b0, comm, 2, "l0b", next_g=gvec(3))
    x4, sv_a1 = _ffn_fwd(x3, h_a1, comm, 3, "l1a")
    h_s = _rmsnorm_streams(x4, gvec(4), "rmsnorm_attn")
    (qkv_t, wo), tok = comm.weights(4, h_s[0])
    hd3 = qkv_t.shape[0] // 3
    qkvs, os, lses = [], [], []
    for g, d in enumerate(ATTN_DILATIONS):
        qkv = _mm_nt(h_s[g], qkv_t, BF16, f"attn_qkv_g{g}", w_row0=g * hd3, n_rows=hd3, streams=d, after=tok)
        o, lse = _attn_fwd(qkv, vecs["q_norm"], vecs["k_norm"], g, f"attn_fwd_g{g}")
        qkvs.append(qkv)
        os.append(o)
        lses.append(lse)
    out_b, out_f, lse_tot = _attn_merge(os, lses, "attn_merge")
    x5, h_b1 = _rows_mm(out_b, wo, "res", "attn_wo", x=x4, g=gvec(5))
    (dy, loss8), sv_b1 = _ffn_fwd(x5, h_b1, comm, 5, "l1b", tgt=tgt)
    loss = 0.5 * jnp.sum(loss8) / x.shape[1]

    dg = [None] * 6
    dx, dg[5], tok = _ffn_bwd(dy, sv_b1, gvec(5), comm, 5, "l1b", ())
    dout = _mm_nt(dx, wo, F32, "attn_wo_bwd", after=tok)
    d_wo = _mm_tn(out_b, dx, 1.0, "attn_dwo")
    delta, *wide = _attn_delta(out_f, dout, lse_tot, "attn_delta")
    per_group = [(dout, lse_tot, delta), tuple(wide[0:3]), tuple(wide[3:6])]
    ggs, d_qkv_t, dhs = [], [], []
    for g, d in enumerate(ATTN_DILATIONS):
        dqkv, gg = _attn_bwd(qkvs[g], *per_group[g], vecs["q_norm"], vecs["k_norm"], g, f"attn_bwd_g{g}")
        ggs.append(gg)
        d_qkv_t.append(_mm_tn(dqkv, h_s[g], 1.0, f"attn_dwqkv_g{g}", streams=d))
        if d > 1:
            dhs.append((d, _rows_mm(dqkv, qkv_t, "plain", f"attn_qkv_bwd_g{g}", b_row0=g * hd3, streams=d)))
        else:
            dqkv0 = dqkv
    tok = comm.grads(4, [jnp.concatenate(d_qkv_t), d_wo])
    dx, dg[4] = _rows_mm(dqkv0, qkv_t, "rmsbwd", "attn_qkv_bwd_g0", x=x4, g=gvec(4), dxo=dx, extras=dhs, after=tok)
    dx, dg[3], tok = _ffn_bwd(dx, sv_a1, gvec(3), comm, 3, "l1a", ())
    dx, dg[2], tok = _ffn_bwd(dx, sv_b0, gvec(2), comm, 2, "l0b", tok)
    ds = _mm_nt(dx, pw2, F32, "conv_pw2_bwd", after=tok)
    d_pw2 = _mm_tn(s, dx, 1.0, "conv_dwpw2")
    dv, dcg8, dbdw8, dbpw2_8 = _conv_post_bwd(ds, v, vecs["cg"], dx, "conv_post_bwd")
    dzc3, d_wdw, db1_16 = _dwconv_bwd(dv, u, vecs["w_dw"], zc3, "conv_dw_bwd")
    d_pw1_t = _mm_tn(dzc3, h_c, 1.0, "conv_dwpw1")
    tok = comm.grads(1, [d_pw1_t, d_pw2])
    dx, dg[1] = _rows_mm(dzc3, pw1_t, "rmsbwd", "conv_pw1_bwd", x=x1, g=gvec(1), dxo=dx, after=tok)
    dx, dg[0], tok = _ffn_bwd(dx, sv_a0, gvec(0), comm, 0, "l0a", ())

    D = x.shape[1]
    small = {
        "norm_g": jnp.stack([p.sum(axis=0) for p in dg]),
        "b_pw1": db1_16.reshape(2, 8, D).sum(axis=1),
        "w_dw": d_wdw[:CONV_WIDTH],
        "b_dw": dbdw8.sum(axis=0, keepdims=True),
        "cg": dcg8.sum(axis=0, keepdims=True),
        "b_pw2": dbpw2_8.sum(axis=0, keepdims=True),
        "q_norm": jnp.stack([gg[0] for gg in ggs]),
        "k_norm": jnp.stack([gg[1] for gg in ggs]),
    }
    return loss, dx, small, tok


def _mesh_pos():
    return lax.axis_index("x"), lax.axis_index("y"), lax.axis_index("c")


def _other_chips(x, y):
    return [(1 - x, y), (x, 1 - y), (1 - x, 1 - y)]


HBM = pl.BlockSpec(memory_space=pltpu.HBM)
SEM = pl.BlockSpec(memory_space=pltpu.SEMAPHORE)
SPLIT_COPY = pltpu.SideEffectType.DATAFLOW_SIDE_EFFECTING


def _hbm(a):
    return pltpu.with_memory_space_constraint(a, pltpu.HBM)


def _gather_copies(j, src, out, send, recv, x, y, c):
    me = 4 * x + 2 * y + c
    peers = [(x, y, 1 - c)] + [(px, py, c) for px, py in _other_chips(x, y)]
    return [pltpu.make_async_remote_copy(src_ref=src, dst_ref=out.at[me], send_sem=send.at[4 * j + k],
                                         recv_sem=recv.at[4 * j + k], device_id=peer, device_id_type=MESH)
            for k, peer in enumerate(peers)]


def _gather_start(shards, after, name):
    n = len(shards)

    def body(*refs):
        ins, outs = refs[:n], refs[n + len(after):2 * n + len(after)]
        send, recv, token = refs[2 * n + len(after) + n:3 * n + len(after) + 3]
        bufs, local_sem = refs[-n - 1:-1], refs[-1]
        x, y, c = _mesh_pos()
        for i in range(n):
            cp = pltpu.make_async_copy(ins[i], bufs[i], local_sem)
            cp.start()
            cp.wait()
            cp = pltpu.make_async_copy(bufs[i], outs[i].at[4 * x + 2 * y + c], local_sem)
            cp.start()
            cp.wait()
        for j in range(n):
            for cp in _gather_copies(j, ins[j], outs[j], send, recv, x, y, c):
                cp.start()
        token[...] = jnp.zeros_like(token)

    res = pl.pallas_call(
        body, name=name, in_specs=[HBM] * n + [ANY] * len(after),
        out_specs=[HBM] * (2 * n) + [SEM, SEM, pl.BlockSpec(memory_space=pltpu.VMEM)],
        out_shape=[pltpu.HBM((N_DEV,) + s.shape, s.dtype) for s in shards] + [pltpu.HBM(s.shape, s.dtype) for s in shards]
        + [pltpu.SemaphoreType.DMA((4 * n,)), pltpu.SemaphoreType.DMA((4 * n,)), jax.ShapeDtypeStruct((8, 128), F32)],
        input_output_aliases={i: n + i for i in range(n)},
        scratch_shapes=[pltpu.VMEM(s.shape, s.dtype) for s in shards] + [pltpu.SemaphoreType.DMA(())],
        compiler_params=pltpu.CompilerParams(has_side_effects=SPLIT_COPY),
    )(*[_hbm(s) for s in shards], *after)
    return res[:n], res[n:2 * n], res[2 * n], res[2 * n + 1], res[2 * n + 2]


def _gather_wait(outs, thru, send, recv, after, name):
    n = len(outs)

    def body(*refs):
        out_refs, thru_refs, send_ref, recv_ref = refs[:n], refs[n:2 * n], refs[2 * n], refs[2 * n + 1]
        x, y, c = _mesh_pos()
        for j in range(n):
            for cp in _gather_copies(j, thru_refs[j], out_refs[j], send_ref, recv_ref, x, y, c):
                cp.wait_send()
                cp.wait_recv()

    res = pl.pallas_call(
        body, name=name, in_specs=[HBM] * (2 * n) + [SEM, SEM] + [ANY] * len(after), out_specs=[HBM] * (2 * n),
        out_shape=[pltpu.HBM(a.shape, a.dtype) for a in list(outs) + list(thru)],
        input_output_aliases={i: i for i in range(2 * n)},
        compiler_params=pltpu.CompilerParams(has_side_effects=SPLIT_COPY),
    )(*outs, *thru, send, recv, *after)
    return res[:n]


def _gather_forward(outs, name):
    n = len(outs)

    def body(*refs):
        o = refs[n:2 * n]
        send_sems, recv_sems = refs[2 * n:]
        x, y, c = _mesh_pos()
        copies = []
        for j in range(n):
            for k, (px, py) in enumerate(_other_chips(x, y)):
                blk = o[j].at[4 * px + 2 * py + c]
                cp = pltpu.make_async_remote_copy(src_ref=blk, dst_ref=blk, send_sem=send_sems.at[j, k],
                                                  recv_sem=recv_sems.at[j, k], device_id=(x, y, 1 - c), device_id_type=MESH)
                cp.start()
                copies.append(cp)
        for cp in copies:
            cp.wait()

    return pl.pallas_call(
        body, name=name, in_specs=[ANY] * n, out_specs=[ANY] * n,
        out_shape=[jax.ShapeDtypeStruct(a.shape, a.dtype) for a in outs],
        input_output_aliases={i: i for i in range(n)},
        scratch_shapes=[pltpu.SemaphoreType.DMA((n, 3)), pltpu.SemaphoreType.DMA((n, 3))],
    )(*outs)


def _all_sum(p, name):
    R, C = p.shape

    def body(p_ref, o_ref, buf, send_sems, recv_sems):
        x, y, c = _mesh_pos()
        me = 4 * x + 2 * y + c
        buf[me] = p_ref[...]
        copies = []
        for rel in range(1, N_DEV):
            peer = (1 - x if rel & 4 else x, 1 - y if rel & 2 else y, 1 - c if rel & 1 else c)
            cp = pltpu.make_async_remote_copy(
                src_ref=p_ref, dst_ref=buf.at[me], send_sem=send_sems.at[rel - 1], recv_sem=recv_sems.at[rel - 1],
                device_id=peer, device_id_type=MESH)
            cp.start()
            copies.append(cp)
        for cp in copies:
            cp.wait()
        acc = buf[0]
        for dev in range(1, N_DEV):
            acc = acc + buf[dev]
        o_ref[...] = acc

    vm = pl.BlockSpec(memory_space=pltpu.VMEM)
    return pl.pallas_call(
        body, name=name, in_specs=[vm], out_specs=vm, out_shape=jax.ShapeDtypeStruct((R, C), F32),
        scratch_shapes=[pltpu.VMEM((N_DEV, R, C), F32), pltpu.SemaphoreType.DMA((N_DEV - 1,)),
                        pltpu.SemaphoreType.DMA((N_DEV - 1,))],
    )(p)


def _swap_with_sibling(gs, name):
    n = len(gs)

    def body(*refs):
        ins, outs = refs[:n], refs[n:2 * n]
        send_sems, recv_sems = refs[2 * n:]
        x, y, c = _mesh_pos()
        copies = []
        for i in range(n):
            for k in range(4):
                cp = pltpu.make_async_remote_copy(
                    src_ref=ins[i].at[2 * k + (1 - c)], dst_ref=outs[i].at[k],
                    send_sem=send_sems.at[i, k], recv_sem=recv_sems.at[i, k],
                    device_id=(x, y, 1 - c), device_id_type=MESH)
                cp.start()
                copies.append(cp)
        for cp in copies:
            cp.wait()

    return pl.pallas_call(
        body, name=name, in_specs=[ANY] * n, out_specs=[ANY] * n,
        out_shape=[jax.ShapeDtypeStruct((4,) + g.shape[1:], g.dtype) for g in gs],
        scratch_shapes=[pltpu.SemaphoreType.DMA((n, 4)), pltpu.SemaphoreType.DMA((n, 4))],
    )(*gs)


def _scatter_copies(j, src, land, send, recv, x, y, c):
    return [pltpu.make_async_remote_copy(src_ref=src.at[2 * px + py], dst_ref=land.at[k], send_sem=send.at[3 * j + k],
                                         recv_sem=recv.at[3 * j + k], device_id=(px, py, c), device_id_type=MESH)
            for k, (px, py) in enumerate(_other_chips(x, y))]


def _scatter_start(ps, name):
    n = len(ps)

    def body(*refs):
        ins, lands = refs[:n], refs[2 * n:3 * n]
        send, recv, token = refs[3 * n:]
        x, y, c = _mesh_pos()
        for j in range(n):
            for cp in _scatter_copies(j, ins[j], lands[j], send, recv, x, y, c):
                cp.start()
        token[...] = jnp.zeros_like(token)

    res = pl.pallas_call(
        body, name=name, in_specs=[HBM] * n,
        out_specs=[HBM] * (2 * n) + [SEM, SEM, pl.BlockSpec(memory_space=pltpu.VMEM)],
        out_shape=[pltpu.HBM(p.shape, p.dtype) for p in ps] + [pltpu.HBM((3,) + p.shape[1:], p.dtype) for p in ps]
        + [pltpu.SemaphoreType.DMA((3 * n,)), pltpu.SemaphoreType.DMA((3 * n,)), jax.ShapeDtypeStruct((8, 128), F32)],
        input_output_aliases={i: i for i in range(n)},
        compiler_params=pltpu.CompilerParams(has_side_effects=SPLIT_COPY),
    )(*[_hbm(p) for p in ps])
    return res[:n], res[n:2 * n], res[2 * n], res[2 * n + 1], res[2 * n + 2]


def _scatter_wait(thru, lands, send, recv, after, name):
    n = len(thru)

    def body(*refs):
        thru_refs, land_refs, send_ref, recv_ref = refs[:n], refs[n:2 * n], refs[2 * n], refs[2 * n + 1]
        x, y, c = _mesh_pos()
        for j in range(n):
            for cp in _scatter_copies(j, thru_refs[j], land_refs[j], send_ref, recv_ref, x, y, c):
                cp.wait_send()
                cp.wait_recv()

    res = pl.pallas_call(
        body, name=name, in_specs=[HBM] * (2 * n) + [SEM, SEM, ANY], out_specs=[HBM] * (2 * n),
        out_shape=[pltpu.HBM(a.shape, a.dtype) for a in list(thru) + list(lands)],
        input_output_aliases={i: i for i in range(2 * n)},
        compiler_params=pltpu.CompilerParams(has_side_effects=SPLIT_COPY),
    )(*thru, *lands, send, recv, after)
    return res[:n], res[n:]


def _row_tile(r, pref):
    if r <= pref:
        return r
    for t in range(pref - pref % 16, 0, -16):
        if r % t == 0:
            return t
    return r


def _add_sibling(g, r1, pos, name):
    _, r, D = g.shape
    tr = _row_tile(r, 512)

    def body(pos_ref, g_ref, r_ref, o_ref):
        o_ref[...] = (g_ref[...].astype(F32) + r_ref[...].astype(F32)).astype(BF16)

    return pl.pallas_call(
        body, name=name,
        grid_spec=pltpu.PrefetchScalarGridSpec(
            num_scalar_prefetch=1, grid=(4, r // tr),
            in_specs=[pl.BlockSpec((None, None, tr, D), lambda k, j, pos: (k, pos[0], j, 0)),
                      pl.BlockSpec((None, tr, D), lambda k, j, pos: (k, j, 0))],
            out_specs=pl.BlockSpec((None, tr, D), lambda k, j, pos: (k, j, 0))),
        out_shape=jax.ShapeDtypeStruct((4, r, D), BF16),
        compiler_params=_params("parallel", "parallel"),
    )(pos, g.reshape(4, 2, r, D), r1)


def _add_chips(p, r2, pos, name):
    _, r, D = p.shape
    tr = _row_tile(r, 512)

    def body(pos_ref, p_ref, r_ref, o_ref):
        acc = p_ref[...].astype(F32)
        for j in range(3):
            acc = acc + r_ref[j].astype(F32)
        o_ref[...] = acc

    return pl.pallas_call(
        body, name=name,
        grid_spec=pltpu.PrefetchScalarGridSpec(
            num_scalar_prefetch=1, grid=(r // tr,),
            in_specs=[pl.BlockSpec((None, tr, D), lambda j, pos: (pos[1], j, 0)),
                      pl.BlockSpec((3, tr, D), lambda j, pos: (0, j, 0))],
            out_specs=pl.BlockSpec((tr, D), lambda j, pos: (j, 0))),
        out_shape=jax.ShapeDtypeStruct((r, D), F32),
        compiler_params=_params("parallel"),
    )(pos, p, r2)


def _place_cols(local, me, width):
    R, w = local.shape
    return lax.dynamic_update_slice(jnp.zeros((R, width), F32), local, (0, me * w))


def _pad_rows(a, rows):
    return jnp.pad(a, ((0, rows - a.shape[0]), (0, 0)))


SUBLAYER_SHARDS = ((0, 4), (8, 9), (1, 5), (2, 6), (10, 11), (3, 7))


class _StepComm:
    def __init__(self, shards, pos):
        self.pos = pos
        self.shards = shards
        self.gathers = {}
        self.started = self._start_gather(1, self._start_gather(0, ()))
        self.pending = {}

    def _start_gather(self, s, after):
        outs, thru, send, recv, token = _gather_start([self.shards[i] for i in SUBLAYER_SHARDS[s]], after,
                                                      f"gather_start_{s}")
        self.gathers[s] = (outs, thru, send, recv)
        return (token,)

    def weights(self, s, after):
        outs = _gather_wait(*self.gathers.pop(s), (after,), f"gather_wait_{s}")
        outs = _gather_forward(outs, f"gather_forward_{s}")
        tok = self._start_gather(s + 2, (outs[0],)) if s + 2 < len(SUBLAYER_SHARDS) else ()
        return [w.reshape(-1, w.shape[-1]) for w in outs], tok

    def grads(self, s, mats):
        g3 = [g.reshape(N_DEV, g.shape[0] // N_DEV, g.shape[1]) for g in mats]
        r1 = _swap_with_sibling(g3, f"rs_sibling_{s}")
        ps = [_add_sibling(g, r, self.pos, f"rs_add_sibling_{s}_{i}") for i, (g, r) in enumerate(zip(g3, r1))]
        thru, lands, send, recv, token = _scatter_start(ps, f"rs_start_{s}")
        self.pending[s] = (thru, lands, send, recv)
        return (token,)

    def finish(self, after):
        out = {}
        for s in sorted(self.pending, reverse=True):
            ps, lands = _scatter_wait(*self.pending[s], after, f"rs_wait_{s}")
            for i, p, r2 in zip(SUBLAYER_SHARDS[s], ps, lands):
                out[i] = _add_chips(p, r2, self.pos, f"rs_add_chips_{i}")
        return out


def kernel(x, norm_g, ffn_w_in, ffn_w_out, conv_w_pw1, conv_b_pw1, conv_w_dw, conv_b_dw, conv_norm_g, conv_w_pw2, conv_b_pw2, attn_w_qkv, attn_q_norm, attn_k_norm, attn_w_o, loss_target, m_norm_g, m_ffn_w_in, m_ffn_w_out, m_conv_w_pw1, m_conv_b_pw1, m_conv_w_dw, m_conv_b_dw, m_conv_norm_g, m_conv_w_pw2, m_conv_b_pw2, m_attn_w_qkv, m_attn_q_norm, m_attn_k_norm, m_attn_w_o, v_norm_g, v_ffn_w_in, v_ffn_w_out, v_conv_w_pw1, v_conv_b_pw1, v_conv_w_dw, v_conv_b_dw, v_conv_norm_g, v_conv_w_pw2, v_conv_b_pw2, v_attn_w_qkv, v_attn_q_norm, v_attn_k_norm, v_attn_w_o):
    T, D = x.shape[1], x.shape[2]
    xi, yi, ci = _mesh_pos()
    me = 4 * xi + 2 * yi + ci
    pos = jnp.stack([ci, 2 * xi + yi]).astype(I32)
    lf = [(l, f) for l in range(2) for f in range(2)]

    shards = ([ffn_w_in[l, f].T.astype(BF16) for l, f in lf] + [ffn_w_out[l, f].astype(BF16) for l, f in lf]
              + [conv_w_pw1[0].T.astype(BF16), conv_w_pw2[0].astype(BF16),
                 attn_w_qkv[0].T.astype(BF16), attn_w_o[0].astype(BF16)])
    comm = _StepComm(shards, pos)
    w_cols = norm_g.shape[2]
    vec = _all_sum(_pad_rows(jnp.concatenate([_place_cols(norm_g.reshape(6, w_cols), me, D),
                                              _place_cols(conv_w_dw[0], me, D)]), 40), "gather_vectors")
    vecs = {
        "norm_g": vec[0:6], "w_dw": vec[6:6 + CONV_WIDTH],
        "b_pw1": conv_b_pw1.reshape(2, D), "b_dw": conv_b_dw, "cg": conv_norm_g, "b_pw2": conv_b_pw2,
        "q_norm": attn_q_norm[0], "k_norm": attn_k_norm[0],
    }

    loss_local, dx, small, tok = _local_step(x[0], loss_target[0], vecs, comm)
    loss = lax.psum(loss_local, ("x", "y", "c"))

    mats = comm.finish(dx)
    hd3 = 3 * HEAD_DIM
    packed = jnp.concatenate([
        small["norm_g"], small["b_pw1"], small["w_dw"], small["b_dw"], small["cg"], small["b_pw2"],
        jnp.pad(small["q_norm"].reshape(1, hd3), ((0, 0), (0, D - hd3))),
        jnp.pad(small["k_norm"].reshape(1, hd3), ((0, 0), (0, D - hd3)))])
    red = _all_sum(_pad_rows(packed, 48), "reduce_vectors")
    col0 = me * w_cols
    grads = {
        "norm_g": lax.dynamic_slice(red[0:6], (0, col0), (6, w_cols)),
        "ffn_w_in": jnp.concatenate([mats[i].T for i in range(4)]),
        "ffn_w_out": jnp.concatenate([mats[i] for i in range(4, 8)]),
        "conv_w_pw1": mats[8].T,
        "conv_b_pw1": red[6:8].reshape(1, 2 * D),
        "conv_w_dw": lax.dynamic_slice(red[8:8 + CONV_WIDTH], (0, col0), (CONV_WIDTH, w_cols)),
        "conv_b_dw": red[39:40], "conv_norm_g": red[40:41], "conv_w_pw2": mats[9], "conv_b_pw2": red[41:42],
        "attn_w_qkv": mats[10].T,
        "attn_q_norm": red[42, :hd3].reshape(3, HEAD_DIM), "attn_k_norm": red[43, :hd3].reshape(3, HEAD_DIM),
        "attn_w_o": mats[11],
    }

    names = ["norm_g", "ffn_w_in", "ffn_w_out", "conv_w_pw1", "conv_b_pw1", "conv_w_dw", "conv_b_dw",
             "conv_norm_g", "conv_w_pw2", "conv_b_pw2", "attn_w_qkv", "attn_q_norm", "attn_k_norm", "attn_w_o"]
    ws = [norm_g, ffn_w_in, ffn_w_out, conv_w_pw1, conv_b_pw1, conv_w_dw, conv_b_dw, conv_norm_g, conv_w_pw2,
          conv_b_pw2, attn_w_qkv, attn_q_norm, attn_k_norm, attn_w_o]
    ms = [m_norm_g, m_ffn_w_in, m_ffn_w_out, m_conv_w_pw1, m_conv_b_pw1, m_conv_w_dw, m_conv_b_dw, m_conv_norm_g,
          m_conv_w_pw2, m_conv_b_pw2, m_attn_w_qkv, m_attn_q_norm, m_attn_k_norm, m_attn_w_o]
    vs = [v_norm_g, v_ffn_w_in, v_ffn_w_out, v_conv_w_pw1, v_conv_b_pw1, v_conv_w_dw, v_conv_b_dw, v_conv_norm_g,
          v_conv_w_pw2, v_conv_b_pw2, v_attn_w_qkv, v_attn_q_norm, v_attn_k_norm, v_attn_w_o]
    g_out, d_out, m_out, v_out = [], [], [], []
    for name, w, m, v in zip(names, ws, ms, vs):
        shape2 = (-1, w.shape[-1]) if w.size > SMALL_PARAM else w.shape
        g2 = grads[name].reshape(shape2)
        delta, new_m, new_v = _adamw(w.reshape(shape2), g2, m.reshape(shape2), v.reshape(shape2), f"adamw_{name}")
        g_out.append(g2.reshape(w.shape))
        d_out.append(delta.reshape(w.shape))
        m_out.append(new_m.reshape(w.shape))
        v_out.append(new_v.reshape(w.shape))
    return (loss, dx.reshape(x.shape), *g_out, *d_out, *m_out, *v_out)
```

```python
import functools

import jax
import jax.numpy as jnp
from jax import lax
from jax.experimental import pallas as pl
from jax.experimental.pallas import tpu as pltpu

F32 = jnp.float32
BF16 = jnp.bfloat16
I32 = jnp.int32

NORM_EPS = 1e-6
LANES = 128
SUBLANES = 8
MXU_WIDTH = 256
HEAD_DIM = 128
ATTN_BLOCK = 128
ATTN_DILATIONS = (1, 4, 16)
CONV_WIDTH = 31
CONV_HALO = 32
CONV_CHUNK = 16
STREAM_TILE = 512
ADAM_LR, ADAM_B1, ADAM_B2, ADAM_EPS, ADAM_WD, ADAM_STEP = 0.001, 0.9, 0.999, 1e-08, 0.01, 10
N_DEV = 8
SMALL_PARAM = 8192
NEG_BIG = -1e30
V7X_VMEM_BYTES = 64 * 1024 * 1024
VMEM_LIMIT = V7X_VMEM_BYTES - 8 * 1024 * 1024
MESH = pl.DeviceIdType.MESH
ANY = pl.BlockSpec(memory_space=pl.ANY)


def _params(*sem):
    return pltpu.CompilerParams(dimension_semantics=sem or None, vmem_limit_bytes=VMEM_LIMIT)


def _dot_nn(a, b):
    return lax.dot_general(a, b, (((1,), (0,)), ((), ())), preferred_element_type=F32)


def _dot_nt(a, b):
    return lax.dot_general(a, b, (((1,), (1,)), ((), ())), preferred_element_type=F32)


def _dot_tn(a, b):
    return lax.dot_general(a, b, (((0,), (0,)), ((), ())), preferred_element_type=F32)


def _sigmoid(x):
    return 1.0 / (1.0 + jnp.exp(-x))


def _tile(n, pref):
    if n <= pref:
        return n
    for t in range(pref - pref % 128, 0, -128):
        if n % t == 0:
            return t
    return n


def _to_lane_chunks(dst, val):
    for c in range(dst.shape[0]):
        dst[c] = val[:, c * LANES:(c + 1) * LANES]


def _rows8(v):
    tm, c = v.shape
    return v.reshape(tm // 8, 8, c).sum(axis=0)


def _rmsnorm(x, g, name, after=()):
    T, D = x.shape
    tm = min(T, 512)

    def body(x_ref, g_ref, *rest):
        h_ref = rest[-1]
        xv = x_ref[...]
        r = lax.rsqrt(jnp.mean(xv * xv, axis=-1, keepdims=True) + NORM_EPS)
        h_ref[...] = (xv * r * g_ref[...]).astype(BF16)

    return pl.pallas_call(
        body, name=name, grid=(T // tm,),
        in_specs=[pl.BlockSpec((tm, D), lambda i: (i, 0)), pl.BlockSpec((1, D), lambda i: (0, 0))] + [ANY] * len(after),
        out_specs=pl.BlockSpec((tm, D), lambda i: (i, 0)),
        out_shape=jax.ShapeDtypeStruct((T, D), BF16),
        compiler_params=_params("parallel"),
    )(x, g, *after)


def _rmsnorm_streams(x, g, name):
    T, D = x.shape
    tm = min(T, STREAM_TILE)
    dils = [d for d in ATTN_DILATIONS if d > 1]

    def body(x_ref, g_ref, h_ref, *rest):
        outs, hs = rest[:-1], rest[-1]
        xv = x_ref[...]
        r = lax.rsqrt(jnp.mean(xv * xv, axis=-1, keepdims=True) + NORM_EPS)
        hf = xv * r * g_ref[...]
        h_ref[...] = hf.astype(BF16)
        _to_lane_chunks(hs, hf)
        for d, o_ref in zip(dils, outs):
            for s in range(d):
                for c in range(D // LANES):
                    o_ref[s, :, c * LANES:(c + 1) * LANES] = hs.at[c][pl.ds(s, tm // d, stride=d), :].astype(BF16)

    res = pl.pallas_call(
        body, name=name, grid=(T // tm,),
        in_specs=[pl.BlockSpec((tm, D), lambda i: (i, 0)), pl.BlockSpec((1, D), lambda i: (0, 0))],
        out_specs=[pl.BlockSpec((tm, D), lambda i: (i, 0))]
        + [pl.BlockSpec((d, tm // d, D), lambda i: (0, i, 0)) for d in dils],
        out_shape=[jax.ShapeDtypeStruct((T, D), BF16)] + [jax.ShapeDtypeStruct((d, T // d, D), BF16) for d in dils],
        scratch_shapes=[pltpu.VMEM((D // LANES, tm, LANES), F32)],
        compiler_params=_params("parallel"),
    )(x, g)
    return [res[0]] + [a.reshape(T, D) for a in res[1:]]


def _gated_in(h, wt, bias, act, u_dtype, name, after=()):
    T, D = h.shape
    F = wt.shape[0] // 2
    tn = _tile(F, 1408)
    tm = min(T, 1024)
    nn = F // tn

    def body(*refs):
        z_ref, u_ref = refs[-2:]
        if bias is None:
            h_ref, w0_ref, w1_ref = refs[:3]
        else:
            h_ref, w0_ref, w1_ref, b_ref = refs[:4]
        hv = h_ref[...]
        for c0 in range(0, tn, MXU_WIDTH):
            cs = slice(c0, min(c0 + MXU_WIDTH, tn))
            z0 = _dot_nt(hv, w0_ref[cs, :])
            z1 = _dot_nt(hv, w1_ref[cs, :])
            if bias is not None:
                z0 = z0 + b_ref[0:1, cs]
                z1 = z1 + b_ref[1:2, cs]
            z_ref[0, :, cs] = z0.astype(BF16)
            z_ref[1, :, cs] = z1.astype(BF16)
            if act == "swiglu":
                u = z0 * _sigmoid(z0) * z1
            else:
                u = z0 * _sigmoid(z1)
            u_ref[:, cs] = u.astype(u_dtype)

    in_specs = [pl.BlockSpec((tm, D), lambda n, m: (m, 0)),
                pl.BlockSpec((tn, D), lambda n, m: (n, 0)),
                pl.BlockSpec((tn, D), lambda n, m: (n + nn, 0))]
    args = [h, wt, wt]
    if bias is not None:
        in_specs.append(pl.BlockSpec((2, tn), lambda n, m: (0, n)))
        args.append(bias)
    in_specs += [ANY] * len(after)
    args += list(after)
    return pl.pallas_call(
        body, name=name, grid=(nn, T // tm), in_specs=in_specs,
        out_specs=[pl.BlockSpec((2, tm, tn), lambda n, m: (0, m, n)), pl.BlockSpec((tm, tn), lambda n, m: (m, n))],
        out_shape=[jax.ShapeDtypeStruct((2, T, F), BF16), jax.ShapeDtypeStruct((T, F), u_dtype)],
        compiler_params=_params("parallel", "parallel"),
    )(*args)


def _swiglu_bwd(dxo, wout, z3, scale, name, after=()):
    T, D = dxo.shape
    F = wout.shape[0]
    tn = _tile(F, 1408)
    tm = min(T, 512)

    def body(d_ref, w_ref, z_ref, *rest):
        dz_ref = rest[-1]
        dy = (d_ref[...] * scale).astype(BF16)
        da = _dot_nt(dy, w_ref[...])
        gate = z_ref[0].astype(F32)
        up = z_ref[1].astype(F32)
        sg = _sigmoid(gate)
        dz_ref[0] = (da * up * (sg * (1.0 + gate * (1.0 - sg)))).astype(BF16)
        dz_ref[1] = (da * gate * sg).astype(BF16)

    return pl.pallas_call(
        body, name=name, grid=(F // tn, T // tm),
        in_specs=[pl.BlockSpec((tm, D), lambda n, m: (m, 0)),
                  pl.BlockSpec((tn, D), lambda n, m: (n, 0)),
                  pl.BlockSpec((2, tm, tn), lambda n, m: (0, m, n))] + [ANY] * len(after),
        out_specs=pl.BlockSpec((2, tm, tn), lambda n, m: (0, m, n)),
        out_shape=jax.ShapeDtypeStruct((2, T, F), BF16),
        compiler_params=_params("parallel", "parallel"),
    )(dxo, wout, z3, *after)


def _mm_nt(a, wt, out_dtype, name, *, w_row0=0, n_rows=None, streams=1, after=()):
    T, K = a.shape
    N = wt.shape[0] if n_rows is None else n_rows
    L = T // streams
    tn = _tile(N, 1152)
    tm = min(L, 512)
    mps = L // tm
    npn = N // tn
    assert w_row0 % tn == 0
    wb0 = w_row0 // tn

    def body(a_ref, w_ref, *rest):
        o_ref = rest[-1]
        o_ref[...] = _dot_nt(a_ref[...].astype(BF16), w_ref[...]).astype(out_dtype)

    return pl.pallas_call(
        body, name=name, grid=(npn, T // tm),
        in_specs=[pl.BlockSpec((tm, K), lambda n, m: (m, 0)), pl.BlockSpec((tn, K), lambda n, m: (wb0 + n, 0))]
        + [ANY] * len(after),
        out_specs=pl.BlockSpec((tm, tn), lambda n, m: (m % mps, (m // mps) * npn + n)),
        out_shape=jax.ShapeDtypeStruct((L, streams * N), out_dtype),
        compiler_params=_params("parallel", "parallel"),
    )(a, wt, *after)


def _mm_tn(a, b, b_scale, name, *, streams=1):
    if a.ndim == 3:
        J, T, F = a.shape
    elif streams > 1:
        J, T, F = 1, a.shape[0] * streams, a.shape[1] // streams
    else:
        (T, F), J = a.shape, 1
    N = b.shape[1]
    tmm = _tile(F, 1408)
    tk = min(T // streams, 1024)
    npj = F // tmm
    nk = T // tk
    kps = nk // streams

    def body(a_ref, b_ref, o_ref, acc_ref):
        k = pl.program_id(1)
        bv = b_ref[...]
        if b_scale != 1.0:
            bv = bv * b_scale
        part = _dot_tn(a_ref[...].astype(BF16), bv.astype(BF16))

        @pl.when(k == 0)
        def _():
            acc_ref[...] = part

        @pl.when(k > 0)
        def _():
            acc_ref[...] += part

        @pl.when(k == nk - 1)
        def _():
            o_ref[...] = acc_ref[...].astype(BF16)

    if a.ndim == 3:
        a_spec = pl.BlockSpec((None, tk, tmm), lambda i, k: (i // npj, k, i % npj))
    elif streams > 1:
        a_spec = pl.BlockSpec((tk, tmm), lambda i, k: (k % kps, (k // kps) * npj + i))
    else:
        a_spec = pl.BlockSpec((tk, tmm), lambda i, k: (k, i))
    return pl.pallas_call(
        body, name=name, grid=(J * npj, nk),
        in_specs=[a_spec, pl.BlockSpec((tk, N), lambda i, k: (k, 0))],
        out_specs=pl.BlockSpec((tmm, N), lambda i, k: (i, 0)),
        out_shape=jax.ShapeDtypeStruct((J * F, N), BF16),
        scratch_shapes=[pltpu.VMEM((tmm, N), F32)],
        compiler_params=_params("parallel", "arbitrary"),
    )(a, b)


def _rows_mm(a, b, mode, name, *, x=None, scale=1.0, bias=None, tgt=None, g=None, dxo=None,
             b_row0=0, streams=1, extras=(), after=()):
    if a.ndim == 3:
        J, T, F = a.shape
    elif streams > 1:
        J, T, F = 1, a.shape[0] * streams, a.shape[1] // streams
    else:
        (T, F), J = a.shape, 1
    D = b.shape[1]
    tk = _tile(F, 2816)
    kpj = F // tk
    nk = J * kpj
    tm = min(T // streams, STREAM_TILE)
    nm = T // tm
    mps = nm // streams
    assert b_row0 % tk == 0
    kb0 = b_row0 // tk
    n_ex = len(extras)

    def body(*refs):
        refs = list(refs)
        a_ref, b_ref = refs[:2]
        rest = refs[2 + len(after):]
        if n_ex:
            tmp_ref = rest.pop()
        acc_ref = rest.pop()
        ex_refs, rest = rest[:n_ex], rest[n_ex:]
        m, k = pl.program_id(0), pl.program_id(1)
        part = _dot_nn(a_ref[...], b_ref[...])

        @pl.when(k == 0)
        def _():
            acc_ref[...] = part

        @pl.when(k > 0)
        def _():
            acc_ref[...] += part

        @pl.when(k == nk - 1)
        def _():
            acc = acc_ref[...]
            for (d, _), e_ref in zip(extras, ex_refs):
                for s in range(d):
                    for c in range(D // LANES):
                        tmp_ref.at[c][pl.ds(s, tm // d, stride=d), :] = e_ref[s, :, c * LANES:(c + 1) * LANES]
                acc = acc + jnp.concatenate([tmp_ref[c] for c in range(D // LANES)], axis=1)
            if mode == "plain":
                (o_ref,) = rest
                o_ref[...] = acc
            elif mode == "res":
                ins = list(rest)
                x_ref = ins.pop(0)
                if bias is not None:
                    acc = acc + ins.pop(0)[...]
                xn = x_ref[...] + scale * acc
                if g is None:
                    (o_ref,) = ins
                else:
                    g_ref, o_ref, h_ref = ins
                    r = lax.rsqrt(jnp.mean(xn * xn, axis=-1, keepdims=True) + NORM_EPS)
                    h_ref[...] = (xn * r * g_ref[...]).astype(BF16)
                o_ref[...] = xn
            elif mode == "loss":
                x_ref, t_ref, o_ref, l_ref = rest
                e = x_ref[...] + scale * acc - t_ref[...]
                o_ref[...] = e / D

                @pl.when(m == 0)
                def _():
                    l_ref[...] = jnp.zeros_like(l_ref)

                l_ref[...] += _rows8(e * e)
            else:
                x_ref, g_ref, d_ref, o_ref, dg_ref = rest
                xv = x_ref[...]
                r = lax.rsqrt(jnp.mean(xv * xv, axis=-1, keepdims=True) + NORM_EPS)
                xh = xv * r
                dxh = acc * g_ref[...]
                o_ref[...] = d_ref[...] + r * (dxh - xh * jnp.mean(dxh * xh, axis=-1, keepdims=True))

                @pl.when(m == 0)
                def _():
                    dg_ref[...] = jnp.zeros_like(dg_ref)

                dg_ref[...] += _rows8(acc * xh)

    if a.ndim == 3:
        a_spec = pl.BlockSpec((None, tm, tk), lambda m, k: (k // kpj, m, k % kpj))
    elif streams > 1:
        a_spec = pl.BlockSpec((tm, tk), lambda m, k: (m % mps, (m // mps) * kpj + k))
    else:
        a_spec = pl.BlockSpec((tm, tk), lambda m, k: (m, k))
    row = pl.BlockSpec((tm, D), lambda m, k: (m, 0))
    vec = pl.BlockSpec((1, D), lambda m, k: (0, 0))
    part8 = pl.BlockSpec((8, D), lambda m, k: (0, 0))
    in_specs = [a_spec, pl.BlockSpec((tk, D), lambda m, k: (kb0 + k, 0))] + [ANY] * len(after)
    args = [a, b, *after]
    for d, e in extras:
        in_specs.append(pl.BlockSpec((d, tm // d, D), lambda m, k: (0, m, 0)))
        args.append(e.reshape(d, T // d, D))
    full = jax.ShapeDtypeStruct((T, D), F32)
    small = jax.ShapeDtypeStruct((8, D), F32)
    if mode == "plain":
        out_specs, out_shape = row, full
    elif mode == "res":
        in_specs.append(row)
        args.append(x)
        if bias is not None:
            in_specs.append(vec)
            args.append(bias)
        if g is None:
            out_specs, out_shape = row, full
        else:
            in_specs.append(vec)
            args.append(g)
            out_specs, out_shape = [row, row], [full, jax.ShapeDtypeStruct((T, D), BF16)]
    elif mode == "loss":
        in_specs += [row, row]
        args += [x, tgt]
        out_specs, out_shape = [row, part8], [full, small]
    else:
        in_specs += [row, vec, row]
        args += [x, g, dxo]
        out_specs, out_shape = [row, part8], [full, small]
    return pl.pallas_call(
        body, name=name, grid=(nm, nk), in_specs=in_specs, out_specs=out_specs, out_shape=out_shape,
        scratch_shapes=[pltpu.VMEM((tm, D), F32)] + ([pltpu.VMEM((D // LANES, tm, LANES), F32)] if n_ex else []),
        compiler_params=_params("arbitrary", "arbitrary"),
    )(*args)


def _sublane_phases(sh):
    rows = sh.shape[1] - SUBLANES
    for p in range(1, SUBLANES):
        sh[p, pl.ds(0, rows), :] = sh[0, pl.ds(p, rows), :]


def _shifted(sh, offset, j, rows=SUBLANES):
    start = pl.multiple_of(j * rows + (offset - offset % SUBLANES), SUBLANES)
    return sh[offset % SUBLANES, pl.ds(start, rows), :]


def _conv_taps(wb, sh, offsets, j, init):
    groups = CONV_CHUNK // SUBLANES
    accs = [init[g * SUBLANES:(g + 1) * SUBLANES] for g in range(groups)]
    for k, off in enumerate(offsets):
        w = wb[k]
        start = pl.multiple_of(j * CONV_CHUNK + (off - off % SUBLANES), SUBLANES)
        for g in range(groups):
            accs[g] = accs[g] + w * sh[off % SUBLANES, pl.ds(start + g * SUBLANES, SUBLANES), :]
    return jnp.concatenate(accs, axis=0)


def _broadcast_rows(dst, src):
    for r in range(dst.shape[0]):
        dst[r] = jnp.zeros(dst.shape[1:], F32) + src[r:r + 1, :]


def _dwconv_fwd(u, w, b_dw, cg, name):
    T, C = u.shape
    tm = min(T, 256)
    hb = tm // CONV_HALO
    pad = CONV_HALO - (CONV_WIDTH - 1)

    def body(u_ref, halo_ref, w_ref, b_ref, g_ref, v_ref, s_ref, sh, wb):
        i = pl.program_id(0)
        sh[0, pl.ds(0, CONV_HALO), :] = jnp.where(i > 0, halo_ref[...], 0.0)
        sh[0, pl.ds(CONV_HALO, tm), :] = u_ref[...]
        _sublane_phases(sh)
        _broadcast_rows(wb, w_ref)
        bias = jnp.zeros((CONV_CHUNK, C), F32) + b_ref[...]

        def chunk(j, carry):
            acc = _conv_taps(wb, sh, [pad + k for k in range(CONV_WIDTH)], j, bias)
            v_ref[pl.ds(pl.multiple_of(j * CONV_CHUNK, CONV_CHUNK), CONV_CHUNK), :] = acc
            return carry

        lax.fori_loop(0, tm // CONV_CHUNK, chunk, 0)
        acc = v_ref[...]
        r = lax.rsqrt(jnp.mean(acc * acc, axis=-1, keepdims=True) + NORM_EPS)
        n = acc * r * g_ref[...]
        s_ref[...] = (n * _sigmoid(n)).astype(BF16)

    row = pl.BlockSpec((tm, C), lambda i: (i, 0))
    vec = pl.BlockSpec((1, C), lambda i: (0, 0))
    return pl.pallas_call(
        body, name=name, grid=(T // tm,),
        in_specs=[row, pl.BlockSpec((CONV_HALO, C), lambda i: (jnp.maximum(i * hb - 1, 0), 0)),
                  pl.BlockSpec((CONV_WIDTH, C), lambda i: (0, 0)), vec, vec],
        out_specs=[row, row],
        out_shape=[jax.ShapeDtypeStruct((T, C), F32), jax.ShapeDtypeStruct((T, C), BF16)],
        scratch_shapes=[pltpu.VMEM((SUBLANES, CONV_HALO + tm, C), F32), pltpu.VMEM((CONV_WIDTH, SUBLANES, C), F32)],
        compiler_params=_params("parallel"),
    )(u, u, w, b_dw, cg)


def _conv_post_bwd(ds, v, cg, dxo, name):
    T, C = v.shape
    tm = min(T, 512)

    def body(ds_ref, v_ref, g_ref, d_ref, dv_ref, dcg_ref, dbdw_ref, dbpw2_ref):
        @pl.when(pl.program_id(0) == 0)
        def _():
            dcg_ref[...] = jnp.zeros_like(dcg_ref)
            dbdw_ref[...] = jnp.zeros_like(dbdw_ref)
            dbpw2_ref[...] = jnp.zeros_like(dbpw2_ref)

        vv = v_ref[...]
        r = lax.rsqrt(jnp.mean(vv * vv, axis=-1, keepdims=True) + NORM_EPS)
        vh = vv * r
        n = vh * g_ref[...]
        sg = _sigmoid(n)
        dn = ds_ref[...] * (sg * (1.0 + n * (1.0 - sg)))
        dvh = dn * g_ref[...]
        dv = r * (dvh - vh * jnp.mean(dvh * vh, axis=-1, keepdims=True))
        dv_ref[...] = dv
        dcg_ref[...] += _rows8(dn * vh)
        dbdw_ref[...] += _rows8(dv)
        dbpw2_ref[...] += _rows8(d_ref[...])

    row = pl.BlockSpec((tm, C), lambda i: (i, 0))
    part8 = pl.BlockSpec((8, C), lambda i: (0, 0))
    small = jax.ShapeDtypeStruct((8, C), F32)
    return pl.pallas_call(
        body, name=name, grid=(T // tm,),
        in_specs=[row, row, pl.BlockSpec((1, C), lambda i: (0, 0)), row],
        out_specs=[row, part8, part8, part8],
        out_shape=[jax.ShapeDtypeStruct((T, C), F32), small, small, small],
        compiler_params=_params("arbitrary"),
    )(ds, v, cg, dxo)


def _dwconv_bwd(dv, u, w, z3, name):
    T, C = u.shape
    tm = min(T, 256)
    hb = tm // CONV_HALO
    nt = T // tm
    pad = CONV_HALO - (CONV_WIDTH - 1)

    taps_at_once = 4
    chunks_at_once = 8

    def body(dv_ref, dvn_ref, u_ref, up_ref, w_ref, z_ref, dz_ref, dw_ref, db_ref, dvsh, ush, wb, du_ref, dwacc):
        i = pl.program_id(0)

        @pl.when(i == 0)
        def _():
            dwacc[...] = jnp.zeros_like(dwacc)
            db_ref[...] = jnp.zeros_like(db_ref)

        dvsh[0, pl.ds(0, tm), :] = dv_ref[...]
        dvsh[0, pl.ds(tm, CONV_HALO), :] = jnp.where(i < nt - 1, dvn_ref[...], 0.0)
        ush[0, pl.ds(0, CONV_HALO), :] = jnp.where(i > 0, up_ref[...], 0.0)
        ush[0, pl.ds(CONV_HALO, tm), :] = u_ref[...]
        _sublane_phases(dvsh)
        _sublane_phases(ush)
        _broadcast_rows(wb, w_ref)

        def du_chunk(j, carry):
            acc = _conv_taps(wb, dvsh, [CONV_WIDTH - 1 - k for k in range(CONV_WIDTH)], j,
                             jnp.zeros((CONV_CHUNK, C), F32))
            du_ref[pl.ds(pl.multiple_of(j * CONV_CHUNK, CONV_CHUNK), CONV_CHUNK), :] = acc
            return carry

        lax.fori_loop(0, tm // CONV_CHUNK, du_chunk, 0)

        for k0 in range(0, CONV_WIDTH, taps_at_once):
            taps = range(k0, min(k0 + taps_at_once, CONV_WIDTH))

            def dw_chunks(jj, carry, taps=taps):
                accs = [jnp.zeros((SUBLANES, C), F32) for _ in taps]
                for u_ in range(chunks_at_once):
                    j = jj * chunks_at_once + u_
                    dvc = dv_ref[pl.ds(pl.multiple_of(j * SUBLANES, SUBLANES), SUBLANES), :]
                    accs = [acc + dvc * _shifted(ush, pad + k, j) for acc, k in zip(accs, taps)]
                for k, acc in zip(taps, accs):
                    dwacc[k] += acc
                return carry

            lax.fori_loop(0, tm // (SUBLANES * chunks_at_once), dw_chunks, 0)

        du = du_ref[...]
        a = z_ref[0].astype(F32)
        gate = z_ref[1].astype(F32)
        sg = _sigmoid(gate)
        da = du * sg
        dgate = du * a * sg * (1.0 - sg)
        dz_ref[0] = da.astype(BF16)
        dz_ref[1] = dgate.astype(BF16)
        db_ref[0:8, :] += _rows8(da)
        db_ref[8:16, :] += _rows8(dgate)

        @pl.when(i == nt - 1)
        def _():
            dw_ref[...] = jnp.zeros_like(dw_ref)
            for k in range(CONV_WIDTH):
                dw_ref[k:k + 1, :] = jnp.sum(dwacc[k], axis=0, keepdims=True)

    row = pl.BlockSpec((tm, C), lambda i: (i, 0))
    last_halo = T // CONV_HALO - 1
    return pl.pallas_call(
        body, name=name, grid=(nt,),
        in_specs=[row, pl.BlockSpec((CONV_HALO, C), lambda i: (jnp.minimum((i + 1) * hb, last_halo), 0)),
                  row, pl.BlockSpec((CONV_HALO, C), lambda i: (jnp.maximum(i * hb - 1, 0), 0)),
                  pl.BlockSpec((CONV_WIDTH, C), lambda i: (0, 0)),
                  pl.BlockSpec((2, tm, C), lambda i: (0, i, 0))],
        out_specs=[pl.BlockSpec((2, tm, C), lambda i: (0, i, 0)),
                   pl.BlockSpec((CONV_HALO, C), lambda i: (0, 0)),
                   pl.BlockSpec((16, C), lambda i: (0, 0))],
        out_shape=[jax.ShapeDtypeStruct((2, T, C), BF16), jax.ShapeDtypeStruct((CONV_HALO, C), F32),
                   jax.ShapeDtypeStruct((16, C), F32)],
        scratch_shapes=[pltpu.VMEM((SUBLANES, tm + CONV_HALO, C), F32), pltpu.VMEM((SUBLANES, CONV_HALO + tm, C), F32),
                        pltpu.VMEM((CONV_WIDTH, SUBLANES, C), F32), pltpu.VMEM((tm, C), F32),
                        pltpu.VMEM((CONV_WIDTH, SUBLANES, C), F32)],
        compiler_params=_params("arbitrary"),
    )(dv, dv, u, u, w, z3)


def _head_norm(raw, gain):
    xv = raw.astype(F32)
    r = lax.rsqrt(jnp.mean(xv * xv, axis=-1, keepdims=True) + NORM_EPS)
    xh = xv * r
    return (xh * gain).astype(BF16), xh, r


def _band_mask(n):
    row = lax.broadcasted_iota(I32, (ATTN_BLOCK, 2 * ATTN_BLOCK), 0)
    col = lax.broadcasted_iota(I32, (ATTN_BLOCK, 2 * ATTN_BLOCK), 1)
    return (col >= row) & (col <= row + ATTN_BLOCK) & ((col >= ATTN_BLOCK) | (n > 0))


def _carry_previous(n, bufs):
    @pl.when(n == 0)
    def _():
        for b in bufs:
            b[0:ATTN_BLOCK, :] = jnp.zeros((ATTN_BLOCK, b.shape[1]), b.dtype)

    @pl.when(n > 0)
    def _():
        for b in bufs:
            b[0:ATTN_BLOCK, :] = b[ATTN_BLOCK:, :]


def _attn_fwd(qkv, qg, kg, g, name):
    d = ATTN_DILATIONS[g]
    L = qkv.shape[0]
    HD = qkv.shape[1] // (3 * d)
    H = HD // HEAD_DIM
    nb = L // ATTN_BLOCK
    scale = HEAD_DIM ** -0.5

    def body(q_ref, kc_ref, vc_ref, qg_ref, kg_ref, o_ref, lse_ref, kk, vv):
        n = pl.program_id(1)
        _carry_previous(n, [kk, vv])
        mask = _band_mask(n)
        lane = lax.broadcasted_iota(I32, (ATTN_BLOCK, 128), 1)
        gq = qg_ref[g:g + 1, :]
        gk = kg_ref[g:g + 1, :]
        vv[ATTN_BLOCK:, :] = vc_ref[...]
        lse_tile = jnp.zeros((ATTN_BLOCK, 128), F32)
        for h in range(H):
            sl = slice(h * HEAD_DIM, (h + 1) * HEAD_DIM)
            q = _head_norm(q_ref[:, sl], gq)[0]
            kk[ATTN_BLOCK:, sl] = _head_norm(kc_ref[:, sl], gk)[0]
            s = jnp.where(mask, _dot_nt(q, kk[:, sl]) * scale, NEG_BIG)
            m = jnp.max(s, axis=-1, keepdims=True)
            p = jnp.exp(s - m)
            l = jnp.sum(p, axis=-1, keepdims=True)
            o_ref[:, sl] = _dot_nn(p.astype(BF16), vv[:, sl]) / l
            lse_tile = jnp.where(lane == h, m + jnp.log(l), lse_tile)
        lse_ref[...] = lse_tile

    def blk(which):
        return pl.BlockSpec((ATTN_BLOCK, HD), lambda r, n: (n, r * 3 + which))

    gains = pl.BlockSpec((3, HEAD_DIM), lambda r, n: (0, 0))
    return pl.pallas_call(
        body, name=name, grid=(d, nb),
        in_specs=[blk(0), blk(1), blk(2), gains, gains],
        out_specs=[pl.BlockSpec((ATTN_BLOCK, HD), lambda r, n: (n, r)),
                   pl.BlockSpec((ATTN_BLOCK, 128), lambda r, n: (n, r))],
        out_shape=[jax.ShapeDtypeStruct((L, d * HD), F32), jax.ShapeDtypeStruct((L, d * 128), F32)],
        scratch_shapes=[pltpu.VMEM((2 * ATTN_BLOCK, HD), BF16), pltpu.VMEM((2 * ATTN_BLOCK, HD), BF16)],
        compiler_params=_params("arbitrary", "arbitrary"),
    )(qkv, qkv, qkv, qg, kg)


def _attn_merge(os, lses, name):
    T = os[0].shape[0]
    HD = os[0].shape[1]
    H = HD // HEAD_DIM
    tm = min(T, STREAM_TILE)
    dils = ATTN_DILATIONS[1:]

    def body(o0, o1, o2, l0, l1, l2, ob_ref, of_ref, lse_ref, n1, n2, m1, m2):
        for d, src, dst, w in ((dils[0], o1, n1, HD), (dils[1], o2, n2, HD), (dils[0], l1, m1, 128), (dils[1], l2, m2, 128)):
            for s in range(d):
                for c in range(w // LANES):
                    dst.at[c][pl.ds(s, tm // d, stride=d), :] = src[:, s * w + c * LANES:s * w + (c + 1) * LANES]
        ls = [l0[...], m1[0], m2[0]]
        m = jnp.maximum(jnp.maximum(ls[0], ls[1]), ls[2])
        tot = m + jnp.log(jnp.exp(ls[0] - m) + jnp.exp(ls[1] - m) + jnp.exp(ls[2] - m))
        lse_ref[...] = tot
        ws = [jnp.exp(l - tot) for l in ls]
        for h in range(H):
            sl = slice(h * HEAD_DIM, (h + 1) * HEAD_DIM)
            out = (ws[0][:, h:h + 1] * o0[:, sl] + ws[1][:, h:h + 1] * n1[h]) + ws[2][:, h:h + 1] * n2[h]
            of_ref[:, sl] = out
            ob_ref[:, sl] = out.astype(BF16)

    def wide(d, w):
        return pl.BlockSpec((tm // d, d * w), lambda i: (i, 0))

    row = pl.BlockSpec((tm, HD), lambda i: (i, 0))
    nar = pl.BlockSpec((tm, 128), lambda i: (i, 0))
    return pl.pallas_call(
        body, name=name, grid=(T // tm,),
        in_specs=[row, wide(dils[0], HD), wide(dils[1], HD), nar, wide(dils[0], 128), wide(dils[1], 128)],
        out_specs=[row, row, nar],
        out_shape=[jax.ShapeDtypeStruct((T, HD), BF16), jax.ShapeDtypeStruct((T, HD), F32),
                   jax.ShapeDtypeStruct((T, 128), F32)],
        scratch_shapes=[pltpu.VMEM((H, tm, HEAD_DIM), F32), pltpu.VMEM((H, tm, HEAD_DIM), F32),
                        pltpu.VMEM((1, tm, 128), F32), pltpu.VMEM((1, tm, 128), F32)],
        compiler_params=_params("parallel"),
    )(*os, *lses)


def _attn_delta(out, dout, lse, name):
    T, HD = out.shape
    H = HD // HEAD_DIM
    tm = min(T, STREAM_TILE)
    dils = ATTN_DILATIONS[1:]

    def body(o_ref, d_ref, l_ref, dl_ref, *rest):
        dch = rest[-1]
        lane = lax.broadcasted_iota(I32, (tm, 128), 1)
        tile = jnp.zeros((tm, 128), F32)
        for h in range(H):
            sl = slice(h * HEAD_DIM, (h + 1) * HEAD_DIM)
            dch[h] = d_ref[:, sl]
            tile = jnp.where(lane == h, jnp.sum(o_ref[:, sl] * d_ref[:, sl], axis=-1, keepdims=True), tile)
        dl_ref[...] = tile
        for i, d in enumerate(dils):
            dw_ref, lw_ref, dlw_ref = rest[3 * i:3 * i + 3]
            for s in range(d):
                rows = pl.ds(s, tm // d, stride=d)
                for h in range(H):
                    dw_ref[:, s * HD + h * HEAD_DIM:s * HD + (h + 1) * HEAD_DIM] = dch.at[h][rows, :].astype(BF16)
                lw_ref[:, s * 128:(s + 1) * 128] = l_ref[rows, :]
                dlw_ref[:, s * 128:(s + 1) * 128] = dl_ref[rows, :]

    row = pl.BlockSpec((tm, HD), lambda i: (i, 0))
    nar = pl.BlockSpec((tm, 128), lambda i: (i, 0))
    out_specs, out_shape = [nar], [jax.ShapeDtypeStruct((T, 128), F32)]
    for d in dils:
        out_specs += [pl.BlockSpec((tm // d, d * w), lambda i: (i, 0)) for w in (HD, 128, 128)]
        out_shape += [jax.ShapeDtypeStruct((T // d, d * HD), BF16), jax.ShapeDtypeStruct((T // d, d * 128), F32),
                      jax.ShapeDtypeStruct((T // d, d * 128), F32)]
    return pl.pallas_call(
        body, name=name, grid=(T // tm,), in_specs=[row, row, nar], out_specs=out_specs, out_shape=out_shape,
        scratch_shapes=[pltpu.VMEM((H, tm, HEAD_DIM), F32)],
        compiler_params=_params("parallel"),
    )(out, dout, lse)


def _attn_bwd(qkv, dout, lse, delta, qg, kg, g, name):
    d = ATTN_DILATIONS[g]
    L = qkv.shape[0]
    HD = qkv.shape[1] // (3 * d)
    H = HD // HEAD_DIM
    nb = L // ATTN_BLOCK
    scale = HEAD_DIM ** -0.5

    def body(*refs):
        (q_ref, kc_ref, vc_ref, do_ref, lse_ref, dl_ref, qg_ref, kg_ref,
         out_ref, gg_ref, dq_a, dk_a, dv_a, dq_b, dk_b, dv_b, kk, vv, khh, rkk) = refs
        r_, n = pl.program_id(0), pl.program_id(1)
        gq = qg_ref[g:g + 1, :]
        gk = kg_ref[g:g + 1, :]
        _carry_previous(n, [kk, vv, khh, rkk])

        @pl.when((r_ == 0) & (n == 0))
        def _():
            gg_ref[...] = jnp.zeros_like(gg_ref)

        @pl.when(n == 0)
        def _():
            dk_a[...] = jnp.zeros_like(dk_a)
            dv_a[...] = jnp.zeros_like(dv_a)

        @pl.when(n < nb)
        def _():
            mask = _band_mask(n)
            vv[ATTN_BLOCK:, :] = vc_ref[...]
            gq_sum = jnp.zeros((1, HEAD_DIM), F32)
            for h in range(H):
                sl = slice(h * HEAD_DIM, (h + 1) * HEAD_DIM)
                qn, qh, rq = _head_norm(q_ref[:, sl], gq)
                kcn, kch, rkc = _head_norm(kc_ref[:, sl], gk)
                kk[ATTN_BLOCK:, sl] = kcn
                khh[ATTN_BLOCK:, sl] = kch
                rkk[ATTN_BLOCK:, sl] = jnp.broadcast_to(rkc, (ATTN_BLOCK, HEAD_DIM))
                kn = kk[:, sl]
                p = jnp.where(mask, jnp.exp(_dot_nt(qn, kn) * scale - lse_ref[:, h:h + 1]), 0.0)
                do = do_ref[:, sl].astype(BF16)
                ds = (p * (_dot_nt(do, vv[:, sl]) - dl_ref[:, h:h + 1]) * scale).astype(BF16)
                dqn = _dot_nn(ds, kn)
                dqh = dqn * gq
                dq_b[:, sl] = rq * (dqh - qh * jnp.mean(dqh * qh, axis=-1, keepdims=True))
                gq_sum = gq_sum + jnp.sum(dqn * qh, axis=0, keepdims=True)
                dkn = _dot_tn(ds, qn)
                dvn = _dot_tn(p.astype(BF16), do)
                dk_a[:, sl] += dkn[0:ATTN_BLOCK]
                dv_a[:, sl] += dvn[0:ATTN_BLOCK]
                dk_b[:, sl] = dkn[ATTN_BLOCK:]
                dv_b[:, sl] = dvn[ATTN_BLOCK:]
            gg_ref[0:1, :] += gq_sum

        @pl.when(n > 0)
        def _():
            out_ref[:, 0:HD] = dq_a[...].astype(BF16)
            gk_sum = jnp.zeros((1, HEAD_DIM), F32)
            for h in range(H):
                sl = slice(h * HEAD_DIM, (h + 1) * HEAD_DIM)
                kh = khh[0:ATTN_BLOCK, sl]
                rk = rkk[0:ATTN_BLOCK, sl]
                dkn = dk_a[:, sl]
                dkh = dkn * gk
                dk = rk * (dkh - kh * jnp.mean(dkh * kh, axis=-1, keepdims=True))
                out_ref[:, HD + h * HEAD_DIM:HD + (h + 1) * HEAD_DIM] = dk.astype(BF16)
                gk_sum = gk_sum + jnp.sum(dkn * kh, axis=0, keepdims=True)
            out_ref[:, 2 * HD:3 * HD] = dv_a[...].astype(BF16)
            gg_ref[1:2, :] += gk_sum

        @pl.when(n < nb)
        def _():
            dq_a[...] = dq_b[...]
            dk_a[...] = dk_b[...]
            dv_a[...] = dv_b[...]

    last = nb - 1

    def blk(which):
        return pl.BlockSpec((ATTN_BLOCK, HD), lambda r, n: (jnp.minimum(n, last), r * 3 + which))

    gains = pl.BlockSpec((3, HEAD_DIM), lambda r, n: (0, 0))
    nar = pl.BlockSpec((ATTN_BLOCK, 128), lambda r, n: (jnp.minimum(n, last), r))
    acc = pltpu.VMEM((ATTN_BLOCK, HD), F32)
    pair16 = pltpu.VMEM((2 * ATTN_BLOCK, HD), BF16)
    pair32 = pltpu.VMEM((2 * ATTN_BLOCK, HD), F32)
    return pl.pallas_call(
        body, name=name, grid=(d, nb + 1),
        in_specs=[blk(0), blk(1), blk(2),
                  pl.BlockSpec((ATTN_BLOCK, HD), lambda r, n: (jnp.minimum(n, last), r)), nar, nar, gains, gains],
        out_specs=[pl.BlockSpec((ATTN_BLOCK, 3 * HD), lambda r, n: (jnp.maximum(n - 1, 0), r)),
                   pl.BlockSpec((8, HEAD_DIM), lambda r, n: (0, 0))],
        out_shape=[jax.ShapeDtypeStruct((L, d * 3 * HD), BF16), jax.ShapeDtypeStruct((8, HEAD_DIM), F32)],
        scratch_shapes=[acc, acc, acc, acc, acc, acc, pair16, pair16, pair32, pair32],
        compiler_params=_params("arbitrary", "arbitrary"),
    )(qkv, qkv, qkv, dout, lse, delta, qg, kg)


def _adamw(w, grad, m, v, name):
    if w.ndim == 2:
        R, C = w.shape
        tr = _row_tile(R, 512)
        blk, steps = pl.BlockSpec((tr, C), lambda i: (i, 0)), R // tr
    else:
        zeros = (0,) * w.ndim
        blk, steps = pl.BlockSpec(w.shape, lambda i: zeros), 1

    def body(w_ref, g_ref, m_ref, v_ref, d_ref, nm_ref, nv_ref):
        gv = g_ref[...]
        nm = ADAM_B1 * m_ref[...] + (1.0 - ADAM_B1) * gv
        nv = ADAM_B2 * v_ref[...] + (1.0 - ADAM_B2) * (gv * gv)
        m_hat = nm / (1.0 - ADAM_B1 ** ADAM_STEP)
        v_hat = nv / (1.0 - ADAM_B2 ** ADAM_STEP)
        d_ref[...] = -ADAM_LR * (m_hat / (jnp.sqrt(v_hat) + ADAM_EPS) + ADAM_WD * w_ref[...])
        nm_ref[...] = nm
        nv_ref[...] = nv

    shp = jax.ShapeDtypeStruct(w.shape, F32)
    return pl.pallas_call(
        body, name=name, grid=(steps,), in_specs=[blk] * 4, out_specs=[blk] * 3, out_shape=[shp] * 3,
        compiler_params=_params("parallel"),
    )(w, grad, m, v)


def _ffn_fwd(x, h, comm, s, tag, tgt=None, next_g=None):
    (win_t, wout), tok = comm.weights(s, h)
    z3, a = _gated_in(h, win_t, None, "swiglu", BF16, f"ffn_in_{tag}", after=tok)
    saved = (x, h, z3, a, win_t, wout)
    if tgt is None:
        return _rows_mm(a, wout, "res", f"ffn_out_{tag}", x=x, scale=0.5, g=next_g), saved
    return _rows_mm(a, wout, "loss", f"ffn_out_loss_{tag}", x=x, scale=0.5, tgt=tgt), saved


def _ffn_bwd(dxo, saved, g, comm, s, tag, after):
    x, h, z3, a, win_t, wout = saved
    dz3 = _swiglu_bwd(dxo, wout, z3, 0.5, f"ffn_out_bwd_{tag}", after=after)
    d_wout = _mm_tn(a, dxo, 0.5, f"ffn_dwout_{tag}")
    d_win_t = _mm_tn(dz3, h, 1.0, f"ffn_dwin_{tag}")
    tok = comm.grads(s, [d_win_t, d_wout])
    dx, dg8 = _rows_mm(dz3, win_t, "rmsbwd", f"ffn_in_bwd_{tag}", x=x, g=g, dxo=dxo, after=tok)
    return dx, dg8, ()


def _local_step(x, tgt, vecs, comm):
    norm_g = vecs["norm_g"]

    def gvec(i):
        return norm_g[i:i + 1]

    h0 = _rmsnorm(x, gvec(0), "rmsnorm_l0a", after=comm.started)
    (x1, h_c), sv_a0 = _ffn_fwd(x, h0, comm, 0, "l0a", next_g=gvec(1))
    (pw1_t, pw2), tok = comm.weights(1, h_c)
    zc3, u = _gated_in(h_c, pw1_t, vecs["b_pw1"], "glu", F32, "conv_pw1", after=tok)
    v, s = _dwconv_fwd(u, vecs["w_dw"], vecs["b_dw"], vecs["cg"], "conv_dw")
    x2, h_b0 = _rows_mm(s, pw2, "res", "conv_pw2", x=x1, bias=vecs["b_pw2"], g=gvec(2))
    (x3, h_a1), sv_b0 = _ffn_fwd(x2, h_b0, comm, 2, "l0b", next_g=gvec(3))
    x4, sv_a1 = _ffn_fwd(x3, h_a1, comm, 3, "l1a")
    h_s = _rmsnorm_streams(x4, gvec(4), "rmsnorm_attn")
    (qkv_t, wo), tok = comm.weights(4, h_s[0])
    hd3 = qkv_t.shape[0] // 3
    qkvs, os, lses = [], [], []
    for g, d in enumerate(ATTN_DILATIONS):
        qkv = _mm_nt(h_s[g], qkv_t, BF16, f"attn_qkv_g{g}", w_row0=g * hd3, n_rows=hd3, streams=d, after=tok)
        o, lse = _attn_fwd(qkv, vecs["q_norm"], vecs["k_norm"], g, f"attn_fwd_g{g}")
        qkvs.append(qkv)
        os.append(o)
        lses.append(lse)
    out_b, out_f, lse_tot = _attn_merge(os, lses, "attn_merge")
    x5, h_b1 = _rows_mm(out_b, wo, "res", "attn_wo", x=x4, g=gvec(5))
    (dy, loss8), sv_b1 = _ffn_fwd(x5, h_b1, comm, 5, "l1b", tgt=tgt)
    loss = 0.5 * jnp.sum(loss8) / x.shape[1]

    dg = [None] * 6
    dx, dg[5], tok = _ffn_bwd(dy, sv_b1, gvec(5), comm, 5, "l1b", ())
    dout = _mm_nt(dx, wo, F32, "attn_wo_bwd", after=tok)
    d_wo = _mm_tn(out_b, dx, 1.0, "attn_dwo")
    delta, *wide = _attn_delta(out_f, dout, lse_tot, "attn_delta")
    per_group = [(dout, lse_tot, delta), tuple(wide[0:3]), tuple(wide[3:6])]
    ggs, d_qkv_t, dhs = [], [], []
    for g, d in enumerate(ATTN_DILATIONS):
        dqkv, gg = _attn_bwd(qkvs[g], *per_group[g], vecs["q_norm"], vecs["k_norm"], g, f"attn_bwd_g{g}")
        ggs.append(gg)
        d_qkv_t.append(_mm_tn(dqkv, h_s[g], 1.0, f"attn_dwqkv_g{g}", streams=d))
        if d > 1:
            dhs.append((d, _rows_mm(dqkv, qkv_t, "plain", f"attn_qkv_bwd_g{g}", b_row0=g * hd3, streams=d)))
        else:
            dqkv0 = dqkv
    tok = comm.grads(4, [jnp.concatenate(d_qkv_t), d_wo])
    dx, dg[4] = _rows_mm(dqkv0, qkv_t, "rmsbwd", "attn_qkv_bwd_g0", x=x4, g=gvec(4), dxo=dx, extras=dhs, after=tok)
    dx, dg[3], tok = _ffn_bwd(dx, sv_a1, gvec(3), comm, 3, "l1a", ())
    dx, dg[2], tok = _ffn_bwd(dx, sv_b0, gvec(2), comm, 2, "l0b", tok)
    ds = _mm_nt(dx, pw2, F32, "conv_pw2_bwd", after=tok)
    d_pw2 = _mm_tn(s, dx, 1.0, "conv_dwpw2")
    dv, dcg8, dbdw8, dbpw2_8 = _conv_post_bwd(ds, v, vecs["cg"], dx, "conv_post_bwd")
    dzc3, d_wdw, db1_16 = _dwconv_bwd(dv, u, vecs["w_dw"], zc3, "conv_dw_bwd")
    d_pw1_t = _mm_tn(dzc3, h_c, 1.0, "conv_dwpw1")
    tok = comm.grads(1, [d_pw1_t, d_pw2])
    dx, dg[1] = _rows_mm(dzc3, pw1_t, "rmsbwd", "conv_pw1_bwd", x=x1, g=gvec(1), dxo=dx, after=tok)
    dx, dg[0], tok = _ffn_bwd(dx, sv_a0, gvec(0), comm, 0, "l0a", ())

    D = x.shape[1]
    small = {
        "norm_g": jnp.stack([p.sum(axis=0) for p in dg]),
        "b_pw1": db1_16.reshape(2, 8, D).sum(axis=1),
        "w_dw": d_wdw[:CONV_WIDTH],
        "b_dw": dbdw8.sum(axis=0, keepdims=True),
        "cg": dcg8.sum(axis=0, keepdims=True),
        "b_pw2": dbpw2_8.sum(axis=0, keepdims=True),
        "q_norm": jnp.stack([gg[0] for gg in ggs]),
        "k_norm": jnp.stack([gg[1] for gg in ggs]),
    }
    return loss, dx, small, tok


def _mesh_pos():
    return lax.axis_index("x"), lax.axis_index("y"), lax.axis_index("c")


def _other_chips(x, y):
    return [(1 - x, y), (x, 1 - y), (1 - x, 1 - y)]


HBM = pl.BlockSpec(memory_space=pltpu.HBM)
SEM = pl.BlockSpec(memory_space=pltpu.SEMAPHORE)
SPLIT_COPY = pltpu.SideEffectType.DATAFLOW_SIDE_EFFECTING


def _hbm(a):
    return pltpu.with_memory_space_constraint(a, pltpu.HBM)


def _gather_copies(j, src, out, send, recv, x, y, c):
    me = 4 * x + 2 * y + c
    peers = [(x, y, 1 - c)] + [(px, py, c) for px, py in _other_chips(x, y)]
    return [pltpu.make_async_remote_copy(src_ref=src, dst_ref=out.at[me], send_sem=send.at[4 * j + k],
                                         recv_sem=recv.at[4 * j + k], device_id=peer, device_id_type=MESH)
            for k, peer in enumerate(peers)]


def _gather_start(shards, after, name):
    n = len(shards)

    def body(*refs):
        ins, outs = refs[:n], refs[n + len(after):2 * n + len(after)]
        send, recv, token = refs[2 * n + len(after) + n:3 * n + len(after) + 3]
        bufs, local_sem = refs[-n - 1:-1], refs[-1]
        x, y, c = _mesh_pos()
        for i in range(n):
            cp = pltpu.make_async_copy(ins[i], bufs[i], local_sem)
            cp.start()
            cp.wait()
            cp = pltpu.make_async_copy(bufs[i], outs[i].at[4 * x + 2 * y + c], local_sem)
            cp.start()
            cp.wait()
        for j in range(n):
            for cp in _gather_copies(j, ins[j], outs[j], send, recv, x, y, c):
                cp.start()
        token[...] = jnp.zeros_like(token)

    res = pl.pallas_call(
        body, name=name, in_specs=[HBM] * n + [ANY] * len(after),
        out_specs=[HBM] * (2 * n) + [SEM, SEM, pl.BlockSpec(memory_space=pltpu.VMEM)],
        out_shape=[pltpu.HBM((N_DEV,) + s.shape, s.dtype) for s in shards] + [pltpu.HBM(s.shape, s.dtype) for s in shards]
        + [pltpu.SemaphoreType.DMA((4 * n,)), pltpu.SemaphoreType.DMA((4 * n,)), jax.ShapeDtypeStruct((8, 128), F32)],
        input_output_aliases={i: n + i for i in range(n)},
        scratch_shapes=[pltpu.VMEM(s.shape, s.dtype) for s in shards] + [pltpu.SemaphoreType.DMA(())],
        compiler_params=pltpu.CompilerParams(has_side_effects=SPLIT_COPY),
    )(*[_hbm(s) for s in shards], *after)
    return res[:n], res[n:2 * n], res[2 * n], res[2 * n + 1], res[2 * n + 2]


def _gather_wait(outs, thru, send, recv, after, name):
    n = len(outs)

    def body(*refs):
        out_refs, thru_refs, send_ref, recv_ref = refs[:n], refs[n:2 * n], refs[2 * n], refs[2 * n + 1]
        x, y, c = _mesh_pos()
        for j in range(n):
            for cp in _gather_copies(j, thru_refs[j], out_refs[j], send_ref, recv_ref, x, y, c):
                cp.wait_send()
                cp.wait_recv()

    res = pl.pallas_call(
        body, name=name, in_specs=[HBM] * (2 * n) + [SEM, SEM] + [ANY] * len(after), out_specs=[HBM] * (2 * n),
        out_shape=[pltpu.HBM(a.shape, a.dtype) for a in list(outs) + list(thru)],
        input_output_aliases={i: i for i in range(2 * n)},
        compiler_params=pltpu.CompilerParams(has_side_effects=SPLIT_COPY),
    )(*outs, *thru, send, recv, *after)
    return res[:n]


def _gather_forward(outs, name):
    n = len(outs)

    def body(*refs):
        o = refs[n:2 * n]
        send_sems, recv_sems = refs[2 * n:]
        x, y, c = _mesh_pos()
        copies = []
        for j in range(n):
            for k, (px, py) in enumerate(_other_chips(x, y)):
                blk = o[j].at[4 * px + 2 * py + c]
                cp = pltpu.make_async_remote_copy(src_ref=blk, dst_ref=blk, send_sem=send_sems.at[j, k],
                                                  recv_sem=recv_sems.at[j, k], device_id=(x, y, 1 - c), device_id_type=MESH)
                cp.start()
                copies.append(cp)
        for cp in copies:
            cp.wait()

    return pl.pallas_call(
        body, name=name, in_specs=[ANY] * n, out_specs=[ANY] * n,
        out_shape=[jax.ShapeDtypeStruct(a.shape, a.dtype) for a in outs],
        input_output_aliases={i: i for i in range(n)},
        scratch_shapes=[pltpu.SemaphoreType.DMA((n, 3)), pltpu.SemaphoreType.DMA((n, 3))],
    )(*outs)


def _all_sum(p, name):
    R, C = p.shape

    def body(p_ref, o_ref, buf, send_sems, recv_sems):
        x, y, c = _mesh_pos()
        me = 4 * x + 2 * y + c
        buf[me] = p_ref[...]
        copies = []
        for rel in range(1, N_DEV):
            peer = (1 - x if rel & 4 else x, 1 - y if rel & 2 else y, 1 - c if rel & 1 else c)
            cp = pltpu.make_async_remote_copy(
                src_ref=p_ref, dst_ref=buf.at[me], send_sem=send_sems.at[rel - 1], recv_sem=recv_sems.at[rel - 1],
                device_id=peer, device_id_type=MESH)
            cp.start()
            copies.append(cp)
        for cp in copies:
            cp.wait()
        acc = buf[0]
        for dev in range(1, N_DEV):
            acc = acc + buf[dev]
        o_ref[...] = acc

    vm = pl.BlockSpec(memory_space=pltpu.VMEM)
    return pl.pallas_call(
        body, name=name, in_specs=[vm], out_specs=vm, out_shape=jax.ShapeDtypeStruct((R, C), F32),
        scratch_shapes=[pltpu.VMEM((N_DEV, R, C), F32), pltpu.SemaphoreType.DMA((N_DEV - 1,)),
                        pltpu.SemaphoreType.DMA((N_DEV - 1,))],
    )(p)


def _swap_with_sibling(gs, name):
    n = len(gs)

    def body(*refs):
        ins, outs = refs[:n], refs[n:2 * n]
        send_sems, recv_sems = refs[2 * n:]
        x, y, c = _mesh_pos()
        copies = []
        for i in range(n):
            for k in range(4):
                cp = pltpu.make_async_remote_copy(
                    src_ref=ins[i].at[2 * k + (1 - c)], dst_ref=outs[i].at[k],
                    send_sem=send_sems.at[i, k], recv_sem=recv_sems.at[i, k],
                    device_id=(x, y, 1 - c), device_id_type=MESH)
                cp.start()
                copies.append(cp)
        for cp in copies:
            cp.wait()

    return pl.pallas_call(
        body, name=name, in_specs=[ANY] * n, out_specs=[ANY] * n,
        out_shape=[jax.ShapeDtypeStruct((4,) + g.shape[1:], g.dtype) for g in gs],
        scratch_shapes=[pltpu.SemaphoreType.DMA((n, 4)), pltpu.SemaphoreType.DMA((n, 4))],
    )(*gs)


def _scatter_copies(j, src, land, send, recv, x, y, c):
    return [pltpu.make_async_remote_copy(src_ref=src.at[2 * px + py], dst_ref=land.at[k], send_sem=send.at[3 * j + k],
                                         recv_sem=recv.at[3 * j + k], device_id=(px, py, c), device_id_type=MESH)
            for k, (px, py) in enumerate(_other_chips(x, y))]


def _scatter_start(ps, name):
    n = len(ps)

    def body(*refs):
        ins, lands = refs[:n], refs[2 * n:3 * n]
        send, recv, token = refs[3 * n:]
        x, y, c = _mesh_pos()
        for j in range(n):
            for cp in _scatter_copies(j, ins[j], lands[j], send, recv, x, y, c):
                cp.start()
        token[...] = jnp.zeros_like(token)

    res = pl.pallas_call(
        body, name=name, in_specs=[HBM] * n,
        out_specs=[HBM] * (2 * n) + [SEM, SEM, pl.BlockSpec(memory_space=pltpu.VMEM)],
        out_shape=[pltpu.HBM(p.shape, p.dtype) for p in ps] + [pltpu.HBM((3,) + p.shape[1:], p.dtype) for p in ps]
        + [pltpu.SemaphoreType.DMA((3 * n,)), pltpu.SemaphoreType.DMA((3 * n,)), jax.ShapeDtypeStruct((8, 128), F32)],
        input_output_aliases={i: i for i in range(n)},
        compiler_params=pltpu.CompilerParams(has_side_effects=SPLIT_COPY),
    )(*[_hbm(p) for p in ps])
    return res[:n], res[n:2 * n], res[2 * n], res[2 * n + 1], res[2 * n + 2]


def _scatter_wait(thru, lands, send, recv, after, name):
    n = len(thru)

    def body(*refs):
        thru_refs, land_refs, send_ref, recv_ref = refs[:n], refs[n:2 * n], refs[2 * n], refs[2 * n + 1]
        x, y, c = _mesh_pos()
        for j in range(n):
            for cp in _scatter_copies(j, thru_refs[j], land_refs[j], send_ref, recv_ref, x, y, c):
                cp.wait_send()
                cp.wait_recv()

    res = pl.pallas_call(
        body, name=name, in_specs=[HBM] * (2 * n) + [SEM, SEM, ANY], out_specs=[HBM] * (2 * n),
        out_shape=[pltpu.HBM(a.shape, a.dtype) for a in list(thru) + list(lands)],
        input_output_aliases={i: i for i in range(2 * n)},
        compiler_params=pltpu.CompilerParams(has_side_effects=SPLIT_COPY),
    )(*thru, *lands, send, recv, after)
    return res[:n], res[n:]


def _row_tile(r, pref):
    if r <= pref:
        return r
    for t in range(pref - pref % 16, 0, -16):
        if r % t == 0:
            return t
    return r


def _add_sibling(g, r1, pos, name):
    _, r, D = g.shape
    tr = _row_tile(r, 512)

    def body(pos_ref, g_ref, r_ref, o_ref):
        o_ref[...] = (g_ref[...].astype(F32) + r_ref[...].astype(F32)).astype(BF16)

    return pl.pallas_call(
        body, name=name,
        grid_spec=pltpu.PrefetchScalarGridSpec(
            num_scalar_prefetch=1, grid=(4, r // tr),
            in_specs=[pl.BlockSpec((None, None, tr, D), lambda k, j, pos: (k, pos[0], j, 0)),
                      pl.BlockSpec((None, tr, D), lambda k, j, pos: (k, j, 0))],
            out_specs=pl.BlockSpec((None, tr, D), lambda k, j, pos: (k, j, 0))),
        out_shape=jax.ShapeDtypeStruct((4, r, D), BF16),
        compiler_params=_params("parallel", "parallel"),
    )(pos, g.reshape(4, 2, r, D), r1)


def _add_chips(p, r2, pos, name):
    _, r, D = p.shape
    tr = _row_tile(r, 512)

    def body(pos_ref, p_ref, r_ref, o_ref):
        acc = p_ref[...].astype(F32)
        for j in range(3):
            acc = acc + r_ref[j].astype(F32)
        o_ref[...] = acc

    return pl.pallas_call(
        body, name=name,
        grid_spec=pltpu.PrefetchScalarGridSpec(
            num_scalar_prefetch=1, grid=(r // tr,),
            in_specs=[pl.BlockSpec((None, tr, D), lambda j, pos: (pos[1], j, 0)),
                      pl.BlockSpec((3, tr, D), lambda j, pos: (0, j, 0))],
            out_specs=pl.BlockSpec((tr, D), lambda j, pos: (j, 0))),
        out_shape=jax.ShapeDtypeStruct((r, D), F32),
        compiler_params=_params("parallel"),
    )(pos, p, r2)


def _place_cols(local, me, width):
    R, w = local.shape
    return lax.dynamic_update_slice(jnp.zeros((R, width), F32), local, (0, me * w))


def _pad_rows(a, rows):
    return jnp.pad(a, ((0, rows - a.shape[0]), (0, 0)))


SUBLAYER_SHARDS = ((0, 4), (8, 9), (1, 5), (2, 6), (10, 11), (3, 7))


class _StepComm:
    def __init__(self, shards, pos):
        self.pos = pos
        self.shards = shards
        self.gathers = {}
        self.started = self._start_gather(1, self._start_gather(0, ()))
        self.pending = {}

    def _start_gather(self, s, after):
        outs, thru, send, recv, token = _gather_start([self.shards[i] for i in SUBLAYER_SHARDS[s]], after,
                                                      f"gather_start_{s}")
        self.gathers[s] = (outs, thru, send, recv)
        return (token,)

    def weights(self, s, after):
        outs = _gather_wait(*self.gathers.pop(s), (after,), f"gather_wait_{s}")
        outs = _gather_forward(outs, f"gather_forward_{s}")
        tok = self._start_gather(s + 2, (outs[0],)) if s + 2 < len(SUBLAYER_SHARDS) else ()
        return [w.reshape(-1, w.shape[-1]) for w in outs], tok

    def grads(self, s, mats):
        g3 = [g.reshape(N_DEV, g.shape[0] // N_DEV, g.shape[1]) for g in mats]
        r1 = _swap_with_sibling(g3, f"rs_sibling_{s}")
        ps = [_add_sibling(g, r, self.pos, f"rs_add_sibling_{s}_{i}") for i, (g, r) in enumerate(zip(g3, r1))]
        thru, lands, send, recv, token = _scatter_start(ps, f"rs_start_{s}")
        self.pending[s] = (thru, lands, send, recv)
        return (token,)

    def finish(self, after):
        out = {}
        for s in sorted(self.pending, reverse=True):
            ps, lands = _scatter_wait(*self.pending[s], after, f"rs_wait_{s}")
            for i, p, r2 in zip(SUBLAYER_SHARDS[s], ps, lands):
                out[i] = _add_chips(p, r2, self.pos, f"rs_add_chips_{i}")
        return out


def kernel(x, norm_g, ffn_w_in, ffn_w_out, conv_w_pw1, conv_b_pw1, conv_w_dw, conv_b_dw, conv_norm_g, conv_w_pw2, conv_b_pw2, attn_w_qkv, attn_q_norm, attn_k_norm, attn_w_o, loss_target, m_norm_g, m_ffn_w_in, m_ffn_w_out, m_conv_w_pw1, m_conv_b_pw1, m_conv_w_dw, m_conv_b_dw, m_conv_norm_g, m_conv_w_pw2, m_conv_b_pw2, m_attn_w_qkv, m_attn_q_norm, m_attn_k_norm, m_attn_w_o, v_norm_g, v_ffn_w_in, v_ffn_w_out, v_conv_w_pw1, v_conv_b_pw1, v_conv_w_dw, v_conv_b_dw, v_conv_norm_g, v_conv_w_pw2, v_conv_b_pw2, v_attn_w_qkv, v_attn_q_norm, v_attn_k_norm, v_attn_w_o):
    T, D = x.shape[1], x.shape[2]
    xi, yi, ci = _mesh_pos()
    me = 4 * xi + 2 * yi + ci
    pos = jnp.stack([ci, 2 * xi + yi]).astype(I32)
    lf = [(l, f) for l in range(2) for f in range(2)]

    shards = ([ffn_w_in[l, f].T.astype(BF16) for l, f in lf] + [ffn_w_out[l, f].astype(BF16) for l, f in lf]
              + [conv_w_pw1[0].T.astype(BF16), conv_w_pw2[0].astype(BF16),
                 attn_w_qkv[0].T.astype(BF16), attn_w_o[0].astype(BF16)])
    comm = _StepComm(shards, pos)
    w_cols = norm_g.shape[2]
    vec = _all_sum(_pad_rows(jnp.concatenate([_place_cols(norm_g.reshape(6, w_cols), me, D),
                                              _place_cols(conv_w_dw[0], me, D)]), 40), "gather_vectors")
    vecs = {
        "norm_g": vec[0:6], "w_dw": vec[6:6 + CONV_WIDTH],
        "b_pw1": conv_b_pw1.reshape(2, D), "b_dw": conv_b_dw, "cg": conv_norm_g, "b_pw2": conv_b_pw2,
        "q_norm": attn_q_norm[0], "k_norm": attn_k_norm[0],
    }

    loss_local, dx, small, tok = _local_step(x[0], loss_target[0], vecs, comm)
    loss = lax.psum(loss_local, ("x", "y", "c"))

    mats = comm.finish(dx)
    hd3 = 3 * HEAD_DIM
    packed = jnp.concatenate([
        small["norm_g"], small["b_pw1"], small["w_dw"], small["b_dw"], small["cg"], small["b_pw2"],
        jnp.pad(small["q_norm"].reshape(1, hd3), ((0, 0), (0, D - hd3))),
        jnp.pad(small["k_norm"].reshape(1, hd3), ((0, 0), (0, D - hd3)))])
    red = _all_sum(_pad_rows(packed, 48), "reduce_vectors")
    col0 = me * w_cols
    grads = {
        "norm_g": lax.dynamic_slice(red[0:6], (0, col0), (6, w_cols)),
        "ffn_w_in": jnp.concatenate([mats[i].T for i in range(4)]),
        "ffn_w_out": jnp.concatenate([mats[i] for i in range(4, 8)]),
        "conv_w_pw1": mats[8].T,
        "conv_b_pw1": red[6:8].reshape(1, 2 * D),
        "conv_w_dw": lax.dynamic_slice(red[8:8 + CONV_WIDTH], (0, col0), (CONV_WIDTH, w_cols)),
        "conv_b_dw": red[39:40], "conv_norm_g": red[40:41], "conv_w_pw2": mats[9], "conv_b_pw2": red[41:42],
        "attn_w_qkv": mats[10].T,
        "attn_q_norm": red[42, :hd3].reshape(3, HEAD_DIM), "attn_k_norm": red[43, :hd3].reshape(3, HEAD_DIM),
        "attn_w_o": mats[11],
    }

    names = ["norm_g", "ffn_w_in", "ffn_w_out", "conv_w_pw1", "conv_b_pw1", "conv_w_dw", "conv_b_dw",
             "conv_norm_g", "conv_w_pw2", "conv_b_pw2", "attn_w_qkv", "attn_q_norm", "attn_k_norm", "attn_w_o"]
    ws = [norm_g, ffn_w_in, ffn_w_out, conv_w_pw1, conv_b_pw1, conv_w_dw, conv_b_dw, conv_norm_g, conv_w_pw2,
          conv_b_pw2, attn_w_qkv, attn_q_norm, attn_k_norm, attn_w_o]
    ms = [m_norm_g, m_ffn_w_in, m_ffn_w_out, m_conv_w_pw1, m_conv_b_pw1, m_conv_w_dw, m_conv_b_dw, m_conv_norm_g,
          m_conv_w_pw2, m_conv_b_pw2, m_attn_w_qkv, m_attn_q_norm, m_attn_k_norm, m_attn_w_o]
    vs = [v_norm_g, v_ffn_w_in, v_ffn_w_out, v_conv_w_pw1, v_conv_b_pw1, v_conv_w_dw, v_conv_b_dw, v_conv_norm_g,
          v_conv_w_pw2, v_conv_b_pw2, v_attn_w_qkv, v_attn_q_norm, v_attn_k_norm, v_attn_w_o]
    g_out, d_out, m_out, v_out = [], [], [], []
    for name, w, m, v in zip(names, ws, ms, vs):
        shape2 = (-1, w.shape[-1]) if w.size > SMALL_PARAM else w.shape
        g2 = grads[name].reshape(shape2)
        delta, new_m, new_v = _adamw(w.reshape(shape2), g2, m.reshape(shape2), v.reshape(shape2), f"adamw_{name}")
        g_out.append(g2.reshape(w.shape))
        d_out.append(delta.reshape(w.shape))
        m_out.append(new_m.reshape(w.shape))
        v_out.append(new_v.reshape(w.shape))
    return (loss, dx.reshape(x.shape), *g_out, *d_out, *m_out, *v_out)
```

```python
import functools

import jax
import jax.numpy as jnp
from jax import lax
from jax.experimental import pallas as pl
from jax.experimental.pallas import tpu as pltpu

F32 = jnp.float32
BF16 = jnp.bfloat16
I32 = jnp.int32

NORM_EPS = 1e-6
LANES = 128
SUBLANES = 8
MXU_WIDTH = 256
HEAD_DIM = 128
ATTN_BLOCK = 128
ATTN_DILATIONS = (1, 4, 16)
CONV_WIDTH = 31
CONV_HALO = 32
CONV_CHUNK = 16
STREAM_TILE = 512
ADAM_LR, ADAM_B1, ADAM_B2, ADAM_EPS, ADAM_WD, ADAM_STEP = 0.001, 0.9, 0.999, 1e-08, 0.01, 10
N_DEV = 8
SMALL_PARAM = 8192
NEG_BIG = -1e30
V7X_VMEM_BYTES = 64 * 1024 * 1024
VMEM_LIMIT = V7X_VMEM_BYTES - 8 * 1024 * 1024
MESH = pl.DeviceIdType.MESH
ANY = pl.BlockSpec(memory_space=pl.ANY)


def _params(*sem):
    return pltpu.CompilerParams(dimension_semantics=sem or None, vmem_limit_bytes=VMEM_LIMIT)


def _dot_nn(a, b):
    return lax.dot_general(a, b, (((1,), (0,)), ((), ())), preferred_element_type=F32)


def _dot_nt(a, b):
    return lax.dot_general(a, b, (((1,), (1,)), ((), ())), preferred_element_type=F32)


def _dot_tn(a, b):
    return lax.dot_general(a, b, (((0,), (0,)), ((), ())), preferred_element_type=F32)


def _sigmoid(x):
    return 1.0 / (1.0 + jnp.exp(-x))


def _tile(n, pref):
    if n <= pref:
        return n
    for t in range(pref - pref % 128, 0, -128):
        if n % t == 0:
            return t
    return n


def _to_lane_chunks(dst, val):
    for c in range(dst.shape[0]):
        dst[c] = val[:, c * LANES:(c + 1) * LANES]


def _rows8(v):
    tm, c = v.shape
    return v.reshape(tm // 8, 8, c).sum(axis=0)


def _rmsnorm(x, g, name, after=()):
    T, D = x.shape
    tm = min(T, 512)

    def body(x_ref, g_ref, *rest):
        h_ref = rest[-1]
        xv = x_ref[...]
        r = lax.rsqrt(jnp.mean(xv * xv, axis=-1, keepdims=True) + NORM_EPS)
        h_ref[...] = (xv * r * g_ref[...]).astype(BF16)

    return pl.pallas_call(
        body, name=name, grid=(T // tm,),
        in_specs=[pl.BlockSpec((tm, D), lambda i: (i, 0)), pl.BlockSpec((1, D), lambda i: (0, 0))] + [ANY] * len(after),
        out_specs=pl.BlockSpec((tm, D), lambda i: (i, 0)),
        out_shape=jax.ShapeDtypeStruct((T, D), BF16),
        compiler_params=_params("parallel"),
    )(x, g, *after)


def _rmsnorm_streams(x, g, name):
    T, D = x.shape
    tm = min(T, STREAM_TILE)
    dils = [d for d in ATTN_DILATIONS if d > 1]

    def body(x_ref, g_ref, h_ref, *rest):
        outs, hs = rest[:-1], rest[-1]
        xv = x_ref[...]
        r = lax.rsqrt(jnp.mean(xv * xv, axis=-1, keepdims=True) + NORM_EPS)
        hf = xv * r * g_ref[...]
        h_ref[...] = hf.astype(BF16)
        _to_lane_chunks(hs, hf)
        for d, o_ref in zip(dils, outs):
            for s in range(d):
                for c in range(D // LANES):
                    o_ref[s, :, c * LANES:(c + 1) * LANES] = hs.at[c][pl.ds(s, tm // d, stride=d), :].astype(BF16)

    res = pl.pallas_call(
        body, name=name, grid=(T // tm,),
        in_specs=[pl.BlockSpec((tm, D), lambda i: (i, 0)), pl.BlockSpec((1, D), lambda i: (0, 0))],
        out_specs=[pl.BlockSpec((tm, D), lambda i: (i, 0))]
        + [pl.BlockSpec((d, tm // d, D), lambda i: (0, i, 0)) for d in dils],
        out_shape=[jax.ShapeDtypeStruct((T, D), BF16)] + [jax.ShapeDtypeStruct((d, T // d, D), BF16) for d in dils],
        scratch_shapes=[pltpu.VMEM((D // LANES, tm, LANES), F32)],
        compiler_params=_params("parallel"),
    )(x, g)
    return [res[0]] + [a.reshape(T, D) for a in res[1:]]


def _gated_in(h, wt, bias, act, u_dtype, name, after=()):
    T, D = h.shape
    F = wt.shape[0] // 2
    tn = _tile(F, 1408)
    tm = min(T, 1024)
    nn = F // tn

    def body(*refs):
        z_ref, u_ref = refs[-2:]
        if bias is None:
            h_ref, w0_ref, w1_ref = refs[:3]
        else:
            h_ref, w0_ref, w1_ref, b_ref = refs[:4]
        hv = h_ref[...]
        for c0 in range(0, tn, MXU_WIDTH):
            cs = slice(c0, min(c0 + MXU_WIDTH, tn))
            z0 = _dot_nt(hv, w0_ref[cs, :])
            z1 = _dot_nt(hv, w1_ref[cs, :])
            if bias is not None:
                z0 = z0 + b_ref[0:1, cs]
                z1 = z1 + b_ref[1:2, cs]
            z_ref[0, :, cs] = z0.astype(BF16)
            z_ref[1, :, cs] = z1.astype(BF16)
            if act == "swiglu":
                u = z0 * _sigmoid(z0) * z1
            else:
                u = z0 * _sigmoid(z1)
            u_ref[:, cs] = u.astype(u_dtype)

    in_specs = [pl.BlockSpec((tm, D), lambda n, m: (m, 0)),
                pl.BlockSpec((tn, D), lambda n, m: (n, 0)),
                pl.BlockSpec((tn, D), lambda n, m: (n + nn, 0))]
    args = [h, wt, wt]
    if bias is not None:
        in_specs.append(pl.BlockSpec((2, tn), lambda n, m: (0, n)))
        args.append(bias)
    in_specs += [ANY] * len(after)
    args += list(after)
    return pl.pallas_call(
        body, name=name, grid=(nn, T // tm), in_specs=in_specs,
        out_specs=[pl.BlockSpec((2, tm, tn), lambda n, m: (0, m, n)), pl.BlockSpec((tm, tn), lambda n, m: (m, n))],
        out_shape=[jax.ShapeDtypeStruct((2, T, F), BF16), jax.ShapeDtypeStruct((T, F), u_dtype)],
        compiler_params=_params("parallel", "parallel"),
    )(*args)


def _swiglu_bwd(dxo, wout, z3, scale, name, after=()):
    T, D = dxo.shape
    F = wout.shape[0]
    tn = _tile(F, 1408)
    tm = min(T, 512)

    def body(d_ref, w_ref, z_ref, *rest):
        dz_ref = rest[-1]
        dy = (d_ref[...] * scale).astype(BF16)
        da = _dot_nt(dy, w_ref[...])
        gate = z_ref[0].astype(F32)
        up = z_ref[1].astype(F32)
        sg = _sigmoid(gate)
        dz_ref[0] = (da * up * (sg * (1.0 + gate * (1.0 - sg)))).astype(BF16)
        dz_ref[1] = (da * gate * sg).astype(BF16)

    return pl.pallas_call(
        body, name=name, grid=(F // tn, T // tm),
        in_specs=[pl.BlockSpec((tm, D), lambda n, m: (m, 0)),
                  pl.BlockSpec((tn, D), lambda n, m: (n, 0)),
                  pl.BlockSpec((2, tm, tn), lambda n, m: (0, m, n))] + [ANY] * len(after),
        out_specs=pl.BlockSpec((2, tm, tn), lambda n, m: (0, m, n)),
        out_shape=jax.ShapeDtypeStruct((2, T, F), BF16),
        compiler_params=_params("parallel", "parallel"),
    )(dxo, wout, z3, *after)


def _mm_nt(a, wt, out_dtype, name, *, w_row0=0, n_rows=None, streams=1, after=()):
    T, K = a.shape
    N = wt.shape[0] if n_rows is None else n_rows
    L = T // streams
    tn = _tile(N, 1152)
    tm = min(L, 512)
    mps = L // tm
    npn = N // tn
    assert w_row0 % tn == 0
    wb0 = w_row0 // tn

    def body(a_ref, w_ref, *rest):
        o_ref = rest[-1]
        o_ref[...] = _dot_nt(a_ref[...].astype(BF16), w_ref[...]).astype(out_dtype)

    return pl.pallas_call(
        body, name=name, grid=(npn, T // tm),
        in_specs=[pl.BlockSpec((tm, K), lambda n, m: (m, 0)), pl.BlockSpec((tn, K), lambda n, m: (wb0 + n, 0))]
        + [ANY] * len(after),
        out_specs=pl.BlockSpec((tm, tn), lambda n, m: (m % mps, (m // mps) * npn + n)),
        out_shape=jax.ShapeDtypeStruct((L, streams * N), out_dtype),
        compiler_params=_params("parallel", "parallel"),
    )(a, wt, *after)


def _mm_tn(a, b, b_scale, name, *, streams=1):
    if a.ndim == 3:
        J, T, F = a.shape
    elif streams > 1:
        J, T, F = 1, a.shape[0] * streams, a.shape[1] // streams
    else:
        (T, F), J = a.shape, 1
    N = b.shape[1]
    tmm = _tile(F, 1408)
    tk = min(T // streams, 2048 if b.dtype == BF16 else 1024)
    npj = F // tmm
    nk = T // tk
    kps = nk // streams

    def body(a_ref, b_ref, o_ref, acc_ref):
        k = pl.program_id(1)
        bv = b_ref[...]
        if b_scale != 1.0:
            bv = bv * b_scale
        part = _dot_tn(a_ref[...].astype(BF16), bv.astype(BF16))

        @pl.when(k == 0)
        def _():
            acc_ref[...] = part

        @pl.when(k > 0)
        def _():
            acc_ref[...] += part

        @pl.when(k == nk - 1)
        def _():
            o_ref[...] = acc_ref[...].astype(BF16)

    if a.ndim == 3:
        a_spec = pl.BlockSpec((None, tk, tmm), lambda i, k: (i // npj, k, i % npj))
    elif streams > 1:
        a_spec = pl.BlockSpec((tk, tmm), lambda i, k: (k % kps, (k // kps) * npj + i))
    else:
        a_spec = pl.BlockSpec((tk, tmm), lambda i, k: (k, i))
    return pl.pallas_call(
        body, name=name, grid=(J * npj, nk),
        in_specs=[a_spec, pl.BlockSpec((tk, N), lambda i, k: (k, 0))],
        out_specs=pl.BlockSpec((tmm, N), lambda i, k: (i, 0)),
        out_shape=jax.ShapeDtypeStruct((J * F, N), BF16),
        scratch_shapes=[pltpu.VMEM((tmm, N), F32)],
        compiler_params=_params("parallel", "arbitrary"),
    )(a, b)


def _rows_mm(a, b, mode, name, *, x=None, scale=1.0, bias=None, tgt=None, g=None, dxo=None,
             b_row0=0, streams=1, extras=(), after=()):
    if a.ndim == 3:
        J, T, F = a.shape
    elif streams > 1:
        J, T, F = 1, a.shape[0] * streams, a.shape[1] // streams
    else:
        (T, F), J = a.shape, 1
    D = b.shape[1]
    tk = _tile(F, 2816)
    kpj = F // tk
    nk = J * kpj
    tm = min(T // streams, STREAM_TILE)
    nm = T // tm
    mps = nm // streams
    assert b_row0 % tk == 0
    kb0 = b_row0 // tk
    n_ex = len(extras)

    def body(*refs):
        refs = list(refs)
        a_ref, b_ref = refs[:2]
        rest = refs[2 + len(after):]
        if n_ex:
            tmp_ref = rest.pop()
        acc_ref = rest.pop()
        ex_refs, rest = rest[:n_ex], rest[n_ex:]
        m, k = pl.program_id(0), pl.program_id(1)
        part = _dot_nn(a_ref[...], b_ref[...])

        @pl.when(k == 0)
        def _():
            acc_ref[...] = part

        @pl.when(k > 0)
        def _():
            acc_ref[...] += part

        @pl.when(k == nk - 1)
        def _():
            acc = acc_ref[...]
            for (d, _), e_ref in zip(extras, ex_refs):
                for s in range(d):
                    for c in range(D // LANES):
                        tmp_ref.at[c][pl.ds(s, tm // d, stride=d), :] = e_ref[s, :, c * LANES:(c + 1) * LANES]
                acc = acc + jnp.concatenate([tmp_ref[c] for c in range(D // LANES)], axis=1)
            if mode == "plain":
                (o_ref,) = rest
                o_ref[...] = acc
            elif mode == "res":
                ins = list(rest)
                x_ref = ins.pop(0)
                if bias is not None:
                    acc = acc + ins.pop(0)[...]
                xn = x_ref[...] + scale * acc
                if g is None:
                    (o_ref,) = ins
                else:
                    g_ref, o_ref, h_ref = ins
                    r = lax.rsqrt(jnp.mean(xn * xn, axis=-1, keepdims=True) + NORM_EPS)
                    h_ref[...] = (xn * r * g_ref[...]).astype(BF16)
                o_ref[...] = xn
            elif mode == "loss":
                x_ref, t_ref, o_ref, l_ref = rest
                e = x_ref[...] + scale * acc - t_ref[...]
                o_ref[...] = e / D

                @pl.when(m == 0)
                def _():
                    l_ref[...] = jnp.zeros_like(l_ref)

                l_ref[...] += _rows8(e * e)
            else:
                x_ref, g_ref, d_ref, o_ref, dg_ref = rest
                xv = x_ref[...]
                r = lax.rsqrt(jnp.mean(xv * xv, axis=-1, keepdims=True) + NORM_EPS)
                xh = xv * r
                dxh = acc * g_ref[...]
                o_ref[...] = d_ref[...] + r * (dxh - xh * jnp.mean(dxh * xh, axis=-1, keepdims=True))

                @pl.when(m == 0)
                def _():
                    dg_ref[...] = jnp.zeros_like(dg_ref)

                dg_ref[...] += _rows8(acc * xh)

    if a.ndim == 3:
        a_spec = pl.BlockSpec((None, tm, tk), lambda m, k: (k // kpj, m, k % kpj))
    elif streams > 1:
        a_spec = pl.BlockSpec((tm, tk), lambda m, k: (m % mps, (m // mps) * kpj + k))
    else:
        a_spec = pl.BlockSpec((tm, tk), lambda m, k: (m, k))
    row = pl.BlockSpec((tm, D), lambda m, k: (m, 0))
    vec = pl.BlockSpec((1, D), lambda m, k: (0, 0))
    part8 = pl.BlockSpec((8, D), lambda m, k: (0, 0))
    in_specs = [a_spec, pl.BlockSpec((tk, D), lambda m, k: (kb0 + k, 0))] + [ANY] * len(after)
    args = [a, b, *after]
    for d, e in extras:
        in_specs.append(pl.BlockSpec((d, tm // d, D), lambda m, k: (0, m, 0)))
        args.append(e.reshape(d, T // d, D))
    full = jax.ShapeDtypeStruct((T, D), F32)
    small = jax.ShapeDtypeStruct((8, D), F32)
    if mode == "plain":
        out_specs, out_shape = row, full
    elif mode == "res":
        in_specs.append(row)
        args.append(x)
        if bias is not None:
            in_specs.append(vec)
            args.append(bias)
        if g is None:
            out_specs, out_shape = row, full
        else:
            in_specs.append(vec)
            args.append(g)
            out_specs, out_shape = [row, row], [full, jax.ShapeDtypeStruct((T, D), BF16)]
    elif mode == "loss":
        in_specs += [row, row]
        args += [x, tgt]
        out_specs, out_shape = [row, part8], [full, small]
    else:
        in_specs += [row, vec, row]
        args += [x, g, dxo]
        out_specs, out_shape = [row, part8], [full, small]
    return pl.pallas_call(
        body, name=name, grid=(nm, nk), in_specs=in_specs, out_specs=out_specs, out_shape=out_shape,
        scratch_shapes=[pltpu.VMEM((tm, D), F32)] + ([pltpu.VMEM((D // LANES, tm, LANES), F32)] if n_ex else []),
        compiler_params=_params("arbitrary", "arbitrary"),
    )(*args)


def _sublane_phases(sh):
    rows = sh.shape[1] - SUBLANES
    for p in range(1, SUBLANES):
        sh[p, pl.ds(0, rows), :] = sh[0, pl.ds(p, rows), :]


def _shifted(sh, offset, j, rows=SUBLANES):
    start = pl.multiple_of(j * rows + (offset - offset % SUBLANES), SUBLANES)
    return sh[offset % SUBLANES, pl.ds(start, rows), :]


def _conv_taps(wb, sh, offsets, j, init):
    groups = CONV_CHUNK // SUBLANES
    accs = [init[g * SUBLANES:(g + 1) * SUBLANES] for g in range(groups)]
    for k, off in enumerate(offsets):
        w = wb[k]
        start = pl.multiple_of(j * CONV_CHUNK + (off - off % SUBLANES), SUBLANES)
        for g in range(groups):
            accs[g] = accs[g] + w * sh[off % SUBLANES, pl.ds(start + g * SUBLANES, SUBLANES), :]
    return jnp.concatenate(accs, axis=0)


def _broadcast_rows(dst, src):
    for r in range(dst.shape[0]):
        dst[r] = jnp.zeros(dst.shape[1:], F32) + src[r:r + 1, :]


def _dwconv_fwd(u, w, b_dw, cg, name):
    T, C = u.shape
    tm = min(T, 256)
    hb = tm // CONV_HALO
    pad = CONV_HALO - (CONV_WIDTH - 1)

    def body(u_ref, halo_ref, w_ref, b_ref, g_ref, v_ref, s_ref, sh, wb):
        i = pl.program_id(0)
        sh[0, pl.ds(0, CONV_HALO), :] = jnp.where(i > 0, halo_ref[...], 0.0)
        sh[0, pl.ds(CONV_HALO, tm), :] = u_ref[...]
        _sublane_phases(sh)
        _broadcast_rows(wb, w_ref)
        bias = jnp.zeros((CONV_CHUNK, C), F32) + b_ref[...]

        def chunk(j, carry):
            acc = _conv_taps(wb, sh, [pad + k for k in range(CONV_WIDTH)], j, bias)
            v_ref[pl.ds(pl.multiple_of(j * CONV_CHUNK, CONV_CHUNK), CONV_CHUNK), :] = acc
            return carry

        lax.fori_loop(0, tm // CONV_CHUNK, chunk, 0)
        acc = v_ref[...]
        r = lax.rsqrt(jnp.mean(acc * acc, axis=-1, keepdims=True) + NORM_EPS)
        n = acc * r * g_ref[...]
        s_ref[...] = (n * _sigmoid(n)).astype(BF16)

    row = pl.BlockSpec((tm, C), lambda i: (i, 0))
    vec = pl.BlockSpec((1, C), lambda i: (0, 0))
    return pl.pallas_call(
        body, name=name, grid=(T // tm,),
        in_specs=[row, pl.BlockSpec((CONV_HALO, C), lambda i: (jnp.maximum(i * hb - 1, 0), 0)),
                  pl.BlockSpec((CONV_WIDTH, C), lambda i: (0, 0)), vec, vec],
        out_specs=[row, row],
        out_shape=[jax.ShapeDtypeStruct((T, C), F32), jax.ShapeDtypeStruct((T, C), BF16)],
        scratch_shapes=[pltpu.VMEM((SUBLANES, CONV_HALO + tm, C), F32), pltpu.VMEM((CONV_WIDTH, SUBLANES, C), F32)],
        compiler_params=_params("parallel"),
    )(u, u, w, b_dw, cg)


def _conv_post_bwd(ds, v, cg, dxo, name):
    T, C = v.shape
    tm = min(T, 512)

    def body(ds_ref, v_ref, g_ref, d_ref, dv_ref, dcg_ref, dbdw_ref, dbpw2_ref):
        @pl.when(pl.program_id(0) == 0)
        def _():
            dcg_ref[...] = jnp.zeros_like(dcg_ref)
            dbdw_ref[...] = jnp.zeros_like(dbdw_ref)
            dbpw2_ref[...] = jnp.zeros_like(dbpw2_ref)

        vv = v_ref[...]
        r = lax.rsqrt(jnp.mean(vv * vv, axis=-1, keepdims=True) + NORM_EPS)
        vh = vv * r
        n = vh * g_ref[...]
        sg = _sigmoid(n)
        dn = ds_ref[...] * (sg * (1.0 + n * (1.0 - sg)))
        dvh = dn * g_ref[...]
        dv = r * (dvh - vh * jnp.mean(dvh * vh, axis=-1, keepdims=True))
        dv_ref[...] = dv
        dcg_ref[...] += _rows8(dn * vh)
        dbdw_ref[...] += _rows8(dv)
        dbpw2_ref[...] += _rows8(d_ref[...])

    row = pl.BlockSpec((tm, C), lambda i: (i, 0))
    part8 = pl.BlockSpec((8, C), lambda i: (0, 0))
    small = jax.ShapeDtypeStruct((8, C), F32)
    return pl.pallas_call(
        body, name=name, grid=(T // tm,),
        in_specs=[row, row, pl.BlockSpec((1, C), lambda i: (0, 0)), row],
        out_specs=[row, part8, part8, part8],
        out_shape=[jax.ShapeDtypeStruct((T, C), F32), small, small, small],
        compiler_params=_params("arbitrary"),
    )(ds, v, cg, dxo)


def _dwconv_bwd(dv, u, w, z3, name):
    T, C = u.shape
    tm = min(T, 256)
    hb = tm // CONV_HALO
    nt = T // tm
    pad = CONV_HALO - (CONV_WIDTH - 1)

    taps_at_once = 4
    chunks_at_once = 8

    def body(dv_ref, dvn_ref, u_ref, up_ref, w_ref, z_ref, dz_ref, dw_ref, db_ref, dvsh, ush, wb, du_ref, dwacc):
        i = pl.program_id(0)

        @pl.when(i == 0)
        def _():
            dwacc[...] = jnp.zeros_like(dwacc)
            db_ref[...] = jnp.zeros_like(db_ref)

        dvsh[0, pl.ds(0, tm), :] = dv_ref[...]
        dvsh[0, pl.ds(tm, CONV_HALO), :] = jnp.where(i < nt - 1, dvn_ref[...], 0.0)
        ush[0, pl.ds(0, CONV_HALO), :] = jnp.where(i > 0, up_ref[...], 0.0)
        ush[0, pl.ds(CONV_HALO, tm), :] = u_ref[...]
        _sublane_phases(dvsh)
        _sublane_phases(ush)
        _broadcast_rows(wb, w_ref)

        def du_chunk(j, carry):
            acc = _conv_taps(wb, dvsh, [CONV_WIDTH - 1 - k for k in range(CONV_WIDTH)], j,
                             jnp.zeros((CONV_CHUNK, C), F32))
            du_ref[pl.ds(pl.multiple_of(j * CONV_CHUNK, CONV_CHUNK), CONV_CHUNK), :] = acc
            return carry

        lax.fori_loop(0, tm // CONV_CHUNK, du_chunk, 0)

        for k0 in range(0, CONV_WIDTH, taps_at_once):
            taps = range(k0, min(k0 + taps_at_once, CONV_WIDTH))

            def dw_chunks(jj, carry, taps=taps):
                accs = [jnp.zeros((SUBLANES, C), F32) for _ in taps]
                for u_ in range(chunks_at_once):
                    j = jj * chunks_at_once + u_
                    dvc = dv_ref[pl.ds(pl.multiple_of(j * SUBLANES, SUBLANES), SUBLANES), :]
                    accs = [acc + dvc * _shifted(ush, pad + k, j) for acc, k in zip(accs, taps)]
                for k, acc in zip(taps, accs):
                    dwacc[k] += acc
                return carry

            lax.fori_loop(0, tm // (SUBLANES * chunks_at_once), dw_chunks, 0)

        du = du_ref[...]
        a = z_ref[0].astype(F32)
        gate = z_ref[1].astype(F32)
        sg = _sigmoid(gate)
        da = du * sg
        dgate = du * a * sg * (1.0 - sg)
        dz_ref[0] = da.astype(BF16)
        dz_ref[1] = dgate.astype(BF16)
        db_ref[0:8, :] += _rows8(da)
        db_ref[8:16, :] += _rows8(dgate)

        @pl.when(i == nt - 1)
        def _():
            dw_ref[...] = jnp.zeros_like(dw_ref)
            for k in range(CONV_WIDTH):
                dw_ref[k:k + 1, :] = jnp.sum(dwacc[k], axis=0, keepdims=True)

    row = pl.BlockSpec((tm, C), lambda i: (i, 0))
    last_halo = T // CONV_HALO - 1
    return pl.pallas_call(
        body, name=name, grid=(nt,),
        in_specs=[row, pl.BlockSpec((CONV_HALO, C), lambda i: (jnp.minimum((i + 1) * hb, last_halo), 0)),
                  row, pl.BlockSpec((CONV_HALO, C), lambda i: (jnp.maximum(i * hb - 1, 0), 0)),
                  pl.BlockSpec((CONV_WIDTH, C), lambda i: (0, 0)),
                  pl.BlockSpec((2, tm, C), lambda i: (0, i, 0))],
        out_specs=[pl.BlockSpec((2, tm, C), lambda i: (0, i, 0)),
                   pl.BlockSpec((CONV_HALO, C), lambda i: (0, 0)),
                   pl.BlockSpec((16, C), lambda i: (0, 0))],
        out_shape=[jax.ShapeDtypeStruct((2, T, C), BF16), jax.ShapeDtypeStruct((CONV_HALO, C), F32),
                   jax.ShapeDtypeStruct((16, C), F32)],
        scratch_shapes=[pltpu.VMEM((SUBLANES, tm + CONV_HALO, C), F32), pltpu.VMEM((SUBLANES, CONV_HALO + tm, C), F32),
                        pltpu.VMEM((CONV_WIDTH, SUBLANES, C), F32), pltpu.VMEM((tm, C), F32),
                        pltpu.VMEM((CONV_WIDTH, SUBLANES, C), F32)],
        compiler_params=_params("arbitrary"),
    )(dv, dv, u, u, w, z3)


def _head_norm(raw, gain):
    xv = raw.astype(F32)
    r = lax.rsqrt(jnp.mean(xv * xv, axis=-1, keepdims=True) + NORM_EPS)
    xh = xv * r
    return (xh * gain).astype(BF16), xh, r


def _band_mask(n):
    row = lax.broadcasted_iota(I32, (ATTN_BLOCK, 2 * ATTN_BLOCK), 0)
    col = lax.broadcasted_iota(I32, (ATTN_BLOCK, 2 * ATTN_BLOCK), 1)
    return (col >= row) & (col <= row + ATTN_BLOCK) & ((col >= ATTN_BLOCK) | (n > 0))


def _carry_previous(n, bufs):
    @pl.when(n == 0)
    def _():
        for b in bufs:
            b[0:ATTN_BLOCK, :] = jnp.zeros((ATTN_BLOCK, b.shape[1]), b.dtype)

    @pl.when(n > 0)
    def _():
        for b in bufs:
            b[0:ATTN_BLOCK, :] = b[ATTN_BLOCK:, :]


def _attn_fwd(qkv, qg, kg, g, name):
    d = ATTN_DILATIONS[g]
    L = qkv.shape[0]
    HD = qkv.shape[1] // (3 * d)
    H = HD // HEAD_DIM
    nb = L // ATTN_BLOCK
    scale = HEAD_DIM ** -0.5

    def body(q_ref, kc_ref, vc_ref, qg_ref, kg_ref, o_ref, lse_ref, kk, vv):
        n = pl.program_id(1)
        _carry_previous(n, [kk, vv])
        mask = _band_mask(n)
        lane = lax.broadcasted_iota(I32, (ATTN_BLOCK, 128), 1)
        gq = qg_ref[g:g + 1, :]
        gk = kg_ref[g:g + 1, :]
        vv[ATTN_BLOCK:, :] = vc_ref[...]
        lse_tile = jnp.zeros((ATTN_BLOCK, 128), F32)
        for h in range(H):
            sl = slice(h * HEAD_DIM, (h + 1) * HEAD_DIM)
            q = _head_norm(q_ref[:, sl], gq)[0]
            kk[ATTN_BLOCK:, sl] = _head_norm(kc_ref[:, sl], gk)[0]
            s = jnp.where(mask, _dot_nt(q, kk[:, sl]) * scale, NEG_BIG)
            m = jnp.max(s, axis=-1, keepdims=True)
            p = jnp.exp(s - m)
            l = jnp.sum(p, axis=-1, keepdims=True)
            o_ref[:, sl] = _dot_nn(p.astype(BF16), vv[:, sl]) / l
            lse_tile = jnp.where(lane == h, m + jnp.log(l), lse_tile)
        lse_ref[...] = lse_tile

    def blk(which):
        return pl.BlockSpec((ATTN_BLOCK, HD), lambda r, n: (n, r * 3 + which))

    gains = pl.BlockSpec((3, HEAD_DIM), lambda r, n: (0, 0))
    return pl.pallas_call(
        body, name=name, grid=(d, nb),
        in_specs=[blk(0), blk(1), blk(2), gains, gains],
        out_specs=[pl.BlockSpec((ATTN_BLOCK, HD), lambda r, n: (n, r)),
                   pl.BlockSpec((ATTN_BLOCK, 128), lambda r, n: (n, r))],
        out_shape=[jax.ShapeDtypeStruct((L, d * HD), F32), jax.ShapeDtypeStruct((L, d * 128), F32)],
        scratch_shapes=[pltpu.VMEM((2 * ATTN_BLOCK, HD), BF16), pltpu.VMEM((2 * ATTN_BLOCK, HD), BF16)],
        compiler_params=_params("arbitrary", "arbitrary"),
    )(qkv, qkv, qkv, qg, kg)


def _attn_merge(os, lses, name):
    T = os[0].shape[0]
    HD = os[0].shape[1]
    H = HD // HEAD_DIM
    tm = min(T, STREAM_TILE)
    dils = ATTN_DILATIONS[1:]

    def body(o0, o1, o2, l0, l1, l2, ob_ref, of_ref, lse_ref, n1, n2, m1, m2):
        for d, src, dst, w in ((dils[0], o1, n1, HD), (dils[1], o2, n2, HD), (dils[0], l1, m1, 128), (dils[1], l2, m2, 128)):
            for s in range(d):
                for c in range(w // LANES):
                    dst.at[c][pl.ds(s, tm // d, stride=d), :] = src[:, s * w + c * LANES:s * w + (c + 1) * LANES]
        ls = [l0[...], m1[0], m2[0]]
        m = jnp.maximum(jnp.maximum(ls[0], ls[1]), ls[2])
        tot = m + jnp.log(jnp.exp(ls[0] - m) + jnp.exp(ls[1] - m) + jnp.exp(ls[2] - m))
        lse_ref[...] = tot
        ws = [jnp.exp(l - tot) for l in ls]
        for h in range(H):
            sl = slice(h * HEAD_DIM, (h + 1) * HEAD_DIM)
            out = (ws[0][:, h:h + 1] * o0[:, sl] + ws[1][:, h:h + 1] * n1[h]) + ws[2][:, h:h + 1] * n2[h]
            of_ref[:, sl] = out
            ob_ref[:, sl] = out.astype(BF16)

    def wide(d, w):
        return pl.BlockSpec((tm // d, d * w), lambda i: (i, 0))

    row = pl.BlockSpec((tm, HD), lambda i: (i, 0))
    nar = pl.BlockSpec((tm, 128), lambda i: (i, 0))
    return pl.pallas_call(
        body, name=name, grid=(T // tm,),
        in_specs=[row, wide(dils[0], HD), wide(dils[1], HD), nar, wide(dils[0], 128), wide(dils[1], 128)],
        out_specs=[row, row, nar],
        out_shape=[jax.ShapeDtypeStruct((T, HD), BF16), jax.ShapeDtypeStruct((T, HD), F32),
                   jax.ShapeDtypeStruct((T, 128), F32)],
        scratch_shapes=[pltpu.VMEM((H, tm, HEAD_DIM), F32), pltpu.VMEM((H, tm, HEAD_DIM), F32),
                        pltpu.VMEM((1, tm, 128), F32), pltpu.VMEM((1, tm, 128), F32)],
        compiler_params=_params("parallel"),
    )(*os, *lses)


def _attn_delta(out, dout, lse, name):
    T, HD = out.shape
    H = HD // HEAD_DIM
    tm = min(T, STREAM_TILE)
    dils = ATTN_DILATIONS[1:]

    def body(o_ref, d_ref, l_ref, dl_ref, *rest):
        dch = rest[-1]
        lane = lax.broadcasted_iota(I32, (tm, 128), 1)
        tile = jnp.zeros((tm, 128), F32)
        for h in range(H):
            sl = slice(h * HEAD_DIM, (h + 1) * HEAD_DIM)
            dch[h] = d_ref[:, sl]
            tile = jnp.where(lane == h, jnp.sum(o_ref[:, sl] * d_ref[:, sl], axis=-1, keepdims=True), tile)
        dl_ref[...] = tile
        for i, d in enumerate(dils):
            dw_ref, lw_ref, dlw_ref = rest[3 * i:3 * i + 3]
            for s in range(d):
                rows = pl.ds(s, tm // d, stride=d)
                for h in range(H):
                    dw_ref[:, s * HD + h * HEAD_DIM:s * HD + (h + 1) * HEAD_DIM] = dch.at[h][rows, :].astype(BF16)
                lw_ref[:, s * 128:(s + 1) * 128] = l_ref[rows, :]
                dlw_ref[:, s * 128:(s + 1) * 128] = dl_ref[rows, :]

    row = pl.BlockSpec((tm, HD), lambda i: (i, 0))
    nar = pl.BlockSpec((tm, 128), lambda i: (i, 0))
    out_specs, out_shape = [nar], [jax.ShapeDtypeStruct((T, 128), F32)]
    for d in dils:
        out_specs += [pl.BlockSpec((tm // d, d * w), lambda i: (i, 0)) for w in (HD, 128, 128)]
        out_shape += [jax.ShapeDtypeStruct((T // d, d * HD), BF16), jax.ShapeDtypeStruct((T // d, d * 128), F32),
                      jax.ShapeDtypeStruct((T // d, d * 128), F32)]
    return pl.pallas_call(
        body, name=name, grid=(T // tm,), in_specs=[row, row, nar], out_specs=out_specs, out_shape=out_shape,
        scratch_shapes=[pltpu.VMEM((H, tm, HEAD_DIM), F32)],
        compiler_params=_params("parallel"),
    )(out, dout, lse)


def _attn_bwd(qkv, dout, lse, delta, qg, kg, g, name):
    d = ATTN_DILATIONS[g]
    L = qkv.shape[0]
    HD = qkv.shape[1] // (3 * d)
    H = HD // HEAD_DIM
    nb = L // ATTN_BLOCK
    scale = HEAD_DIM ** -0.5

    def body(*refs):
        (q_ref, kc_ref, vc_ref, do_ref, lse_ref, dl_ref, qg_ref, kg_ref,
         out_ref, gg_ref, dq_a, dk_a, dv_a, dq_b, dk_b, dv_b, kk, vv, khh, rkk) = refs
        r_, n = pl.program_id(0), pl.program_id(1)
        gq = qg_ref[g:g + 1, :]
        gk = kg_ref[g:g + 1, :]
        _carry_previous(n, [kk, vv, khh, rkk])

        @pl.when((r_ == 0) & (n == 0))
        def _():
            gg_ref[...] = jnp.zeros_like(gg_ref)

        @pl.when(n == 0)
        def _():
            dk_a[...] = jnp.zeros_like(dk_a)
            dv_a[...] = jnp.zeros_like(dv_a)

        @pl.when(n < nb)
        def _():
            mask = _band_mask(n)
            vv[ATTN_BLOCK:, :] = vc_ref[...]
            gq_sum = jnp.zeros((1, HEAD_DIM), F32)
            for h in range(H):
                sl = slice(h * HEAD_DIM, (h + 1) * HEAD_DIM)
                qn, qh, rq = _head_norm(q_ref[:, sl], gq)
                kcn, kch, rkc = _head_norm(kc_ref[:, sl], gk)
                kk[ATTN_BLOCK:, sl] = kcn
                khh[ATTN_BLOCK:, sl] = kch
                rkk[ATTN_BLOCK:, sl] = jnp.broadcast_to(rkc, (ATTN_BLOCK, HEAD_DIM))
                kn = kk[:, sl]
                p = jnp.where(mask, jnp.exp(_dot_nt(qn, kn) * scale - lse_ref[:, h:h + 1]), 0.0)
                do = do_ref[:, sl].astype(BF16)
                ds = (p * (_dot_nt(do, vv[:, sl]) - dl_ref[:, h:h + 1]) * scale).astype(BF16)
                dqn = _dot_nn(ds, kn)
                dqh = dqn * gq
                dq_b[:, sl] = rq * (dqh - qh * jnp.mean(dqh * qh, axis=-1, keepdims=True))
                gq_sum = gq_sum + jnp.sum(dqn * qh, axis=0, keepdims=True)
                dkn = _dot_tn(ds, qn)
                dvn = _dot_tn(p.astype(BF16), do)
                dk_a[:, sl] += dkn[0:ATTN_BLOCK]
                dv_a[:, sl] += dvn[0:ATTN_BLOCK]
                dk_b[:, sl] = dkn[ATTN_BLOCK:]
                dv_b[:, sl] = dvn[ATTN_BLOCK:]
            gg_ref[0:1, :] += gq_sum

        @pl.when(n > 0)
        def _():
            out_ref[:, 0:HD] = dq_a[...].astype(BF16)
            gk_sum = jnp.zeros((1, HEAD_DIM), F32)
            for h in range(H):
                sl = slice(h * HEAD_DIM, (h + 1) * HEAD_DIM)
                kh = khh[0:ATTN_BLOCK, sl]
                rk = rkk[0:ATTN_BLOCK, sl]
                dkn = dk_a[:, sl]
                dkh = dkn * gk
                dk = rk * (dkh - kh * jnp.mean(dkh * kh, axis=-1, keepdims=True))
                out_ref[:, HD + h * HEAD_DIM:HD + (h + 1) * HEAD_DIM] = dk.astype(BF16)
                gk_sum = gk_sum + jnp.sum(dkn * kh, axis=0, keepdims=True)
            out_ref[:, 2 * HD:3 * HD] = dv_a[...].astype(BF16)
            gg_ref[1:2, :] += gk_sum

        @pl.when(n < nb)
        def _():
            dq_a[...] = dq_b[...]
            dk_a[...] = dk_b[...]
            dv_a[...] = dv_b[...]

    last = nb - 1

    def blk(which):
        return pl.BlockSpec((ATTN_BLOCK, HD), lambda r, n: (jnp.minimum(n, last), r * 3 + which))

    gains = pl.BlockSpec((3, HEAD_DIM), lambda r, n: (0, 0))
    nar = pl.BlockSpec((ATTN_BLOCK, 128), lambda r, n: (jnp.minimum(n, last), r))
    acc = pltpu.VMEM((ATTN_BLOCK, HD), F32)
    pair16 = pltpu.VMEM((2 * ATTN_BLOCK, HD), BF16)
    pair32 = pltpu.VMEM((2 * ATTN_BLOCK, HD), F32)
    return pl.pallas_call(
        body, name=name, grid=(d, nb + 1),
        in_specs=[blk(0), blk(1), blk(2),
                  pl.BlockSpec((ATTN_BLOCK, HD), lambda r, n: (jnp.minimum(n, last), r)), nar, nar, gains, gains],
        out_specs=[pl.BlockSpec((ATTN_BLOCK, 3 * HD), lambda r, n: (jnp.maximum(n - 1, 0), r)),
                   pl.BlockSpec((8, HEAD_DIM), lambda r, n: (0, 0))],
        out_shape=[jax.ShapeDtypeStruct((L, d * 3 * HD), BF16), jax.ShapeDtypeStruct((8, HEAD_DIM), F32)],
        scratch_shapes=[acc, acc, acc, acc, acc, acc, pair16, pair16, pair32, pair32],
        compiler_params=_params("arbitrary", "arbitrary"),
    )(qkv, qkv, qkv, dout, lse, delta, qg, kg)


def _adamw(w, grad, m, v, name):
    if w.ndim == 2:
        R, C = w.shape
        tr = _row_tile(R, 512)
        blk, steps = pl.BlockSpec((tr, C), lambda i: (i, 0)), R // tr
    else:
        zeros = (0,) * w.ndim
        blk, steps = pl.BlockSpec(w.shape, lambda i: zeros), 1

    def body(w_ref, g_ref, m_ref, v_ref, d_ref, nm_ref, nv_ref):
        gv = g_ref[...]
        nm = ADAM_B1 * m_ref[...] + (1.0 - ADAM_B1) * gv
        nv = ADAM_B2 * v_ref[...] + (1.0 - ADAM_B2) * (gv * gv)
        m_hat = nm / (1.0 - ADAM_B1 ** ADAM_STEP)
        v_hat = nv / (1.0 - ADAM_B2 ** ADAM_STEP)
        d_ref[...] = -ADAM_LR * (m_hat / (jnp.sqrt(v_hat) + ADAM_EPS) + ADAM_WD * w_ref[...])
        nm_ref[...] = nm
        nv_ref[...] = nv

    shp = jax.ShapeDtypeStruct(w.shape, F32)
    return pl.pallas_call(
        body, name=name, grid=(steps,), in_specs=[blk] * 4, out_specs=[blk] * 3, out_shape=[shp] * 3,
        compiler_params=_params("parallel"),
    )(w, grad, m, v)


def _ffn_fwd(x, h, comm, s, tag, tgt=None, next_g=None):
    (win_t, wout), tok = comm.weights(s, h)
    z3, a = _gated_in(h, win_t, None, "swiglu", BF16, f"ffn_in_{tag}", after=tok)
    saved = (x, h, z3, a, win_t, wout)
    if tgt is None:
        return _rows_mm(a, wout, "res", f"ffn_out_{tag}", x=x, scale=0.5, g=next_g), saved
    return _rows_mm(a, wout, "loss", f"ffn_out_loss_{tag}", x=x, scale=0.5, tgt=tgt), saved


def _ffn_bwd(dxo, saved, g, comm, s, tag, after):
    x, h, z3, a, win_t, wout = saved
    dz3 = _swiglu_bwd(dxo, wout, z3, 0.5, f"ffn_out_bwd_{tag}", after=after)
    d_wout = _mm_tn(a, dxo, 0.5, f"ffn_dwout_{tag}")
    d_win_t = _mm_tn(dz3, h, 1.0, f"ffn_dwin_{tag}")
    tok = comm.grads(s, [d_win_t, d_wout])
    dx, dg8 = _rows_mm(dz3, win_t, "rmsbwd", f"ffn_in_bwd_{tag}", x=x, g=g, dxo=dxo, after=tok)
    return dx, dg8, ()


def _local_step(x, tgt, vecs, comm):
    norm_g = vecs["norm_g"]

    def gvec(i):
        return norm_g[i:i + 1]

    h0 = _rmsnorm(x, gvec(0), "rmsnorm_l0a", after=comm.started)
    (x1, h_c), sv_a0 = _ffn_fwd(x, h0, comm, 0, "l0a", next_g=gvec(1))
    (pw1_t, pw2), tok = comm.weights(1, h_c)
    zc3, u = _gated_in(h_c, pw1_t, vecs["b_pw1"], "glu", F32, "conv_pw1", after=tok)
    v, s = _dwconv_fwd(u, vecs["w_dw"], vecs["b_dw"], vecs["cg"], "conv_dw")
    x2, h_b0 = _rows_mm(s, pw2, "res", "conv_pw2", x=x1, bias=vecs["b_pw2"], g=gvec(2))
    (x3, h_a1), sv_b0 = _ffn_fwd(x2, h_b0, comm, 2, "l0b", next_g=gvec(3))
    x4, sv_a1 = _ffn_fwd(x3, h_a1, comm, 3, "l1a")
    h_s = _rmsnorm_streams(x4, gvec(4), "rmsnorm_attn")
    (qkv_t, wo), tok = comm.weights(4, h_s[0])
    hd3 = qkv_t.shape[0] // 3
    qkvs, os, lses = [], [], []
    for g, d in enumerate(ATTN_DILATIONS):
        qkv = _mm_nt(h_s[g], qkv_t, BF16, f"attn_qkv_g{g}", w_row0=g * hd3, n_rows=hd3, streams=d, after=tok)
        o, lse = _attn_fwd(qkv, vecs["q_norm"], vecs["k_norm"], g, f"attn_fwd_g{g}")
        qkvs.append(qkv)
        os.append(o)
        lses.append(lse)
    out_b, out_f, lse_tot = _attn_merge(os, lses, "attn_merge")
    x5, h_b1 = _rows_mm(out_b, wo, "res", "attn_wo", x=x4, g=gvec(5))
    (dy, loss8), sv_b1 = _ffn_fwd(x5, h_b1, comm, 5, "l1b", tgt=tgt)
    loss = 0.5 * jnp.sum(loss8) / x.shape[1]

    dg = [None] * 6
    dx, dg[5], tok = _ffn_bwd(dy, sv_b1, gvec(5), comm, 5, "l1b", ())
    dout = _mm_nt(dx, wo, F32, "attn_wo_bwd", after=tok)
    d_wo = _mm_tn(out_b, dx, 1.0, "attn_dwo")
    delta, *wide = _attn_delta(out_f, dout, lse_tot, "attn_delta")
    per_group = [(dout, lse_tot, delta), tuple(wide[0:3]), tuple(wide[3:6])]
    ggs, d_qkv_t, dhs = [], [], []
    for g, d in enumerate(ATTN_DILATIONS):
        dqkv, gg = _attn_bwd(qkvs[g], *per_group[g], vecs["q_norm"], vecs["k_norm"], g, f"attn_bwd_g{g}")
        ggs.append(gg)
        d_qkv_t.append(_mm_tn(dqkv, h_s[g], 1.0, f"attn_dwqkv_g{g}", streams=d))
        if d > 1:
            dhs.append((d, _rows_mm(dqkv, qkv_t, "plain", f"attn_qkv_bwd_g{g}", b_row0=g * hd3, streams=d)))
        else:
            dqkv0 = dqkv
    tok = comm.grads(4, [jnp.concatenate(d_qkv_t), d_wo])
    dx, dg[4] = _rows_mm(dqkv0, qkv_t, "rmsbwd", "attn_qkv_bwd_g0", x=x4, g=gvec(4), dxo=dx, extras=dhs, after=tok)
    dx, dg[3], tok = _ffn_bwd(dx, sv_a1, gvec(3), comm, 3, "l1a", ())
    dx, dg[2], tok = _ffn_bwd(dx, sv_b0, gvec(2), comm, 2, "l0b", tok)
    ds = _mm_nt(dx, pw2, F32, "conv_pw2_bwd", after=tok)
    d_pw2 = _mm_tn(s, dx, 1.0, "conv_dwpw2")
    dv, dcg8, dbdw8, dbpw2_8 = _conv_post_bwd(ds, v, vecs["cg"], dx, "conv_post_bwd")
    dzc3, d_wdw, db1_16 = _dwconv_bwd(dv, u, vecs["w_dw"], zc3, "conv_dw_bwd")
    d_pw1_t = _mm_tn(dzc3, h_c, 1.0, "conv_dwpw1")
    tok = comm.grads(1, [d_pw1_t, d_pw2])
    dx, dg[1] = _rows_mm(dzc3, pw1_t, "rmsbwd", "conv_pw1_bwd", x=x1, g=gvec(1), dxo=dx, after=tok)
    dx, dg[0], tok = _ffn_bwd(dx, sv_a0, gvec(0), comm, 0, "l0a", ())

    D = x.shape[1]
    small = {
        "norm_g": jnp.stack([p.sum(axis=0) for p in dg]),
        "b_pw1": db1_16.reshape(2, 8, D).sum(axis=1),
        "w_dw": d_wdw[:CONV_WIDTH],
        "b_dw": dbdw8.sum(axis=0, keepdims=True),
        "cg": dcg8.sum(axis=0, keepdims=True),
        "b_pw2": dbpw2_8.sum(axis=0, keepdims=True),
        "q_norm": jnp.stack([gg[0] for gg in ggs]),
        "k_norm": jnp.stack([gg[1] for gg in ggs]),
    }
    return loss, dx, small, tok


def _mesh_pos():
    return lax.axis_index("x"), lax.axis_index("y"), lax.axis_index("c")


def _other_chips(x, y):
    return [(1 - x, y), (x, 1 - y), (1 - x, 1 - y)]


HBM = pl.BlockSpec(memory_space=pltpu.HBM)
SEM = pl.BlockSpec(memory_space=pltpu.SEMAPHORE)
SPLIT_COPY = pltpu.SideEffectType.DATAFLOW_SIDE_EFFECTING


def _hbm(a):
    return pltpu.with_memory_space_constraint(a, pltpu.HBM)


def _gather_copies(j, src, out, send, recv, x, y, c):
    me = 4 * x + 2 * y + c
    peers = [(x, y, 1 - c)] + [(px, py, c) for px, py in _other_chips(x, y)]
    return [pltpu.make_async_remote_copy(src_ref=src, dst_ref=out.at[me], send_sem=send.at[4 * j + k],
                                         recv_sem=recv.at[4 * j + k], device_id=peer, device_id_type=MESH)
            for k, peer in enumerate(peers)]


def _gather_start(shards, after, name):
    n = len(shards)

    def body(*refs):
        ins, outs = refs[:n], refs[n + len(after):2 * n + len(after)]
        send, recv, token = refs[2 * n + len(after) + n:3 * n + len(after) + 3]
        bufs, local_sem = refs[-n - 1:-1], refs[-1]
        x, y, c = _mesh_pos()
        for i in range(n):
            cp = pltpu.make_async_copy(ins[i], bufs[i], local_sem)
            cp.start()
            cp.wait()
            cp = pltpu.make_async_copy(bufs[i], outs[i].at[4 * x + 2 * y + c], local_sem)
            cp.start()
            cp.wait()
        for j in range(n):
            for cp in _gather_copies(j, ins[j], outs[j], send, recv, x, y, c):
                cp.start()
        token[...] = jnp.zeros_like(token)

    res = pl.pallas_call(
        body, name=name, in_specs=[HBM] * n + [ANY] * len(after),
        out_specs=[HBM] * (2 * n) + [SEM, SEM, pl.BlockSpec(memory_space=pltpu.VMEM)],
        out_shape=[pltpu.HBM((N_DEV,) + s.shape, s.dtype) for s in shards] + [pltpu.HBM(s.shape, s.dtype) for s in shards]
        + [pltpu.SemaphoreType.DMA((4 * n,)), pltpu.SemaphoreType.DMA((4 * n,)), jax.ShapeDtypeStruct((8, 128), F32)],
        input_output_aliases={i: n + i for i in range(n)},
        scratch_shapes=[pltpu.VMEM(s.shape, s.dtype) for s in shards] + [pltpu.SemaphoreType.DMA(())],
        compiler_params=pltpu.CompilerParams(has_side_effects=SPLIT_COPY),
    )(*[_hbm(s) for s in shards], *after)
    return res[:n], res[n:2 * n], res[2 * n], res[2 * n + 1], res[2 * n + 2]


def _gather_wait(outs, thru, send, recv, after, name):
    n = len(outs)

    def body(*refs):
        out_refs, thru_refs, send_ref, recv_ref = refs[:n], refs[n:2 * n], refs[2 * n], refs[2 * n + 1]
        x, y, c = _mesh_pos()
        for j in range(n):
            for cp in _gather_copies(j, thru_refs[j], out_refs[j], send_ref, recv_ref, x, y, c):
                cp.wait_send()
                cp.wait_recv()

    res = pl.pallas_call(
        body, name=name, in_specs=[HBM] * (2 * n) + [SEM, SEM] + [ANY] * len(after), out_specs=[HBM] * (2 * n),
        out_shape=[pltpu.HBM(a.shape, a.dtype) for a in list(outs) + list(thru)],
        input_output_aliases={i: i for i in range(2 * n)},
        compiler_params=pltpu.CompilerParams(has_side_effects=SPLIT_COPY),
    )(*outs, *thru, send, recv, *after)
    return res[:n]


def _gather_forward(outs, name):
    n = len(outs)

    def body(*refs):
        o = refs[n:2 * n]
        send_sems, recv_sems = refs[2 * n:]
        x, y, c = _mesh_pos()
        copies = []
        for j in range(n):
            for k, (px, py) in enumerate(_other_chips(x, y)):
                blk = o[j].at[4 * px + 2 * py + c]
                cp = pltpu.make_async_remote_copy(src_ref=blk, dst_ref=blk, send_sem=send_sems.at[j, k],
                                                  recv_sem=recv_sems.at[j, k], device_id=(x, y, 1 - c), device_id_type=MESH)
                cp.start()
                copies.append(cp)
        for cp in copies:
            cp.wait()

    return pl.pallas_call(
        body, name=name, in_specs=[ANY] * n, out_specs=[ANY] * n,
        out_shape=[jax.ShapeDtypeStruct(a.shape, a.dtype) for a in outs],
        input_output_aliases={i: i for i in range(n)},
        scratch_shapes=[pltpu.SemaphoreType.DMA((n, 3)), pltpu.SemaphoreType.DMA((n, 3))],
    )(*outs)


def _all_sum(p, name):
    R, C = p.shape

    def body(p_ref, o_ref, buf, send_sems, recv_sems):
        x, y, c = _mesh_pos()
        me = 4 * x + 2 * y + c
        buf[me] = p_ref[...]
        copies = []
        for rel in range(1, N_DEV):
            peer = (1 - x if rel & 4 else x, 1 - y if rel & 2 else y, 1 - c if rel & 1 else c)
            cp = pltpu.make_async_remote_copy(
                src_ref=p_ref, dst_ref=buf.at[me], send_sem=send_sems.at[rel - 1], recv_sem=recv_sems.at[rel - 1],
                device_id=peer, device_id_type=MESH)
            cp.start()
            copies.append(cp)
        for cp in copies:
            cp.wait()
        acc = buf[0]
        for dev in range(1, N_DEV):
            acc = acc + buf[dev]
        o_ref[...] = acc

    vm = pl.BlockSpec(memory_space=pltpu.VMEM)
    return pl.pallas_call(
        body, name=name, in_specs=[vm], out_specs=vm, out_shape=jax.ShapeDtypeStruct((R, C), F32),
        scratch_shapes=[pltpu.VMEM((N_DEV, R, C), F32), pltpu.SemaphoreType.DMA((N_DEV - 1,)),
                        pltpu.SemaphoreType.DMA((N_DEV - 1,))],
    )(p)


def _swap_with_sibling(gs, name):
    n = len(gs)

    def body(*refs):
        ins, outs = refs[:n], refs[n:2 * n]
        send_sems, recv_sems = refs[2 * n:]
        x, y, c = _mesh_pos()
        copies = []
        for i in range(n):
            for k in range(4):
                cp = pltpu.make_async_remote_copy(
                    src_ref=ins[i].at[2 * k + (1 - c)], dst_ref=outs[i].at[k],
                    send_sem=send_sems.at[i, k], recv_sem=recv_sems.at[i, k],
                    device_id=(x, y, 1 - c), device_id_type=MESH)
                cp.start()
                copies.append(cp)
        for cp in copies:
            cp.wait()

    return pl.pallas_call(
        body, name=name, in_specs=[ANY] * n, out_specs=[ANY] * n,
        out_shape=[jax.ShapeDtypeStruct((4,) + g.shape[1:], g.dtype) for g in gs],
        scratch_shapes=[pltpu.SemaphoreType.DMA((n, 4)), pltpu.SemaphoreType.DMA((n, 4))],
    )(*gs)


def _scatter_copies(j, src, land, send, recv, x, y, c):
    return [pltpu.make_async_remote_copy(src_ref=src.at[2 * px + py], dst_ref=land.at[k], send_sem=send.at[3 * j + k],
                                         recv_sem=recv.at[3 * j + k], device_id=(px, py, c), device_id_type=MESH)
            for k, (px, py) in enumerate(_other_chips(x, y))]


def _scatter_start(ps, name):
    n = len(ps)

    def body(*refs):
        ins, lands = refs[:n], refs[2 * n:3 * n]
        send, recv, token = refs[3 * n:]
        x, y, c = _mesh_pos()
        for j in range(n):
            for cp in _scatter_copies(j, ins[j], lands[j], send, recv, x, y, c):
                cp.start()
        token[...] = jnp.zeros_like(token)

    res = pl.pallas_call(
        body, name=name, in_specs=[HBM] * n,
        out_specs=[HBM] * (2 * n) + [SEM, SEM, pl.BlockSpec(memory_space=pltpu.VMEM)],
        out_shape=[pltpu.HBM(p.shape, p.dtype) for p in ps] + [pltpu.HBM((3,) + p.shape[1:], p.dtype) for p in ps]
        + [pltpu.SemaphoreType.DMA((3 * n,)), pltpu.SemaphoreType.DMA((3 * n,)), jax.ShapeDtypeStruct((8, 128), F32)],
        input_output_aliases={i: i for i in range(n)},
        compiler_params=pltpu.CompilerParams(has_side_effects=SPLIT_COPY),
    )(*[_hbm(p) for p in ps])
    return res[:n], res[n:2 * n], res[2 * n], res[2 * n + 1], res[2 * n + 2]


def _scatter_wait(thru, lands, send, recv, after, name):
    n = len(thru)

    def body(*refs):
        thru_refs, land_refs, send_ref, recv_ref = refs[:n], refs[n:2 * n], refs[2 * n], refs[2 * n + 1]
        x, y, c = _mesh_pos()
        for j in range(n):
            for cp in _scatter_copies(j, thru_refs[j], land_refs[j], send_ref, recv_ref, x, y, c):
                cp.wait_send()
                cp.wait_recv()

    res = pl.pallas_call(
        body, name=name, in_specs=[HBM] * (2 * n) + [SEM, SEM, ANY], out_specs=[HBM] * (2 * n),
        out_shape=[pltpu.HBM(a.shape, a.dtype) for a in list(thru) + list(lands)],
        input_output_aliases={i: i for i in range(2 * n)},
        compiler_params=pltpu.CompilerParams(has_side_effects=SPLIT_COPY),
    )(*thru, *lands, send, recv, after)
    return res[:n], res[n:]


def _row_tile(r, pref):
    if r <= pref:
        return r
    for t in range(pref - pref % 16, 0, -16):
        if r % t == 0:
            return t
    return r


def _add_sibling(g, r1, pos, name):
    _, r, D = g.shape
    tr = _row_tile(r, 512)

    def body(pos_ref, g_ref, r_ref, o_ref):
        o_ref[...] = (g_ref[...].astype(F32) + r_ref[...].astype(F32)).astype(BF16)

    return pl.pallas_call(
        body, name=name,
        grid_spec=pltpu.PrefetchScalarGridSpec(
            num_scalar_prefetch=1, grid=(4, r // tr),
            in_specs=[pl.BlockSpec((None, None, tr, D), lambda k, j, pos: (k, pos[0], j, 0)),
                      pl.BlockSpec((None, tr, D), lambda k, j, pos: (k, j, 0))],
            out_specs=pl.BlockSpec((None, tr, D), lambda k, j, pos: (k, j, 0))),
        out_shape=jax.ShapeDtypeStruct((4, r, D), BF16),
        compiler_params=_params("parallel", "parallel"),
    )(pos, g.reshape(4, 2, r, D), r1)


def _add_chips(p, r2, pos, name):
    _, r, D = p.shape
    tr = _row_tile(r, 512)

    def body(pos_ref, p_ref, r_ref, o_ref):
        acc = p_ref[...].astype(F32)
        for j in range(3):
            acc = acc + r_ref[j].astype(F32)
        o_ref[...] = acc

    return pl.pallas_call(
        body, name=name,
        grid_spec=pltpu.PrefetchScalarGridSpec(
            num_scalar_prefetch=1, grid=(r // tr,),
            in_specs=[pl.BlockSpec((None, tr, D), lambda j, pos: (pos[1], j, 0)),
                      pl.BlockSpec((3, tr, D), lambda j, pos: (0, j, 0))],
            out_specs=pl.BlockSpec((tr, D), lambda j, pos: (j, 0))),
        out_shape=jax.ShapeDtypeStruct((r, D), F32),
        compiler_params=_params("parallel"),
    )(pos, p, r2)


def _place_cols(local, me, width):
    R, w = local.shape
    return lax.dynamic_update_slice(jnp.zeros((R, width), F32), local, (0, me * w))


def _pad_rows(a, rows):
    return jnp.pad(a, ((0, rows - a.shape[0]), (0, 0)))


SUBLAYER_SHARDS = ((0, 4), (8, 9), (1, 5), (2, 6), (10, 11), (3, 7))


class _StepComm:
    def __init__(self, shards, pos):
        self.pos = pos
        self.shards = shards
        self.gathers = {}
        self.started = self._start_gather(1, self._start_gather(0, ()))
        self.pending = {}

    def _start_gather(self, s, after):
        outs, thru, send, recv, token = _gather_start([self.shards[i] for i in SUBLAYER_SHARDS[s]], after,
                                                      f"gather_start_{s}")
        self.gathers[s] = (outs, thru, send, recv)
        return (token,)

    def weights(self, s, after):
        outs = _gather_wait(*self.gathers.pop(s), (after,), f"gather_wait_{s}")
        outs = _gather_forward(outs, f"gather_forward_{s}")
        tok = self._start_gather(s + 2, (outs[0],)) if s + 2 < len(SUBLAYER_SHARDS) else ()
        return [w.reshape(-1, w.shape[-1]) for w in outs], tok

    def grads(self, s, mats):
        g3 = [g.reshape(N_DEV, g.shape[0] // N_DEV, g.shape[1]) for g in mats]
        r1 = _swap_with_sibling(g3, f"rs_sibling_{s}")
        ps = [_add_sibling(g, r, self.pos, f"rs_add_sibling_{s}_{i}") for i, (g, r) in enumerate(zip(g3, r1))]
        thru, lands, send, recv, token = _scatter_start(ps, f"rs_start_{s}")
        self.pending[s] = (thru, lands, send, recv)
        return (token,)

    def finish(self, after):
        out = {}
        for s in sorted(self.pending, reverse=True):
            ps, lands = _scatter_wait(*self.pending[s], after, f"rs_wait_{s}")
            for i, p, r2 in zip(SUBLAYER_SHARDS[s], ps, lands):
                out[i] = _add_chips(p, r2, self.pos, f"rs_add_chips_{i}")
        return out


def kernel(x, norm_g, ffn_w_in, ffn_w_out, conv_w_pw1, conv_b_pw1, conv_w_dw, conv_b_dw, conv_norm_g, conv_w_pw2, conv_b_pw2, attn_w_qkv, attn_q_norm, attn_k_norm, attn_w_o, loss_target, m_norm_g, m_ffn_w_in, m_ffn_w_out, m_conv_w_pw1, m_conv_b_pw1, m_conv_w_dw, m_conv_b_dw, m_conv_norm_g, m_conv_w_pw2, m_conv_b_pw2, m_attn_w_qkv, m_attn_q_norm, m_attn_k_norm, m_attn_w_o, v_norm_g, v_ffn_w_in, v_ffn_w_out, v_conv_w_pw1, v_conv_b_pw1, v_conv_w_dw, v_conv_b_dw, v_conv_norm_g, v_conv_w_pw2, v_conv_b_pw2, v_attn_w_qkv, v_attn_q_norm, v_attn_k_norm, v_attn_w_o):
    T, D = x.shape[1], x.shape[2]
    xi, yi, ci = _mesh_pos()
    me = 4 * xi + 2 * yi + ci
    pos = jnp.stack([ci, 2 * xi + yi]).astype(I32)
    lf = [(l, f) for l in range(2) for f in range(2)]

    shards = ([ffn_w_in[l, f].T.astype(BF16) for l, f in lf] + [ffn_w_out[l, f].astype(BF16) for l, f in lf]
              + [conv_w_pw1[0].T.astype(BF16), conv_w_pw2[0].astype(BF16),
                 attn_w_qkv[0].T.astype(BF16), attn_w_o[0].astype(BF16)])
    comm = _StepComm(shards, pos)
    w_cols = norm_g.shape[2]
    vec = _all_sum(_pad_rows(jnp.concatenate([_place_cols(norm_g.reshape(6, w_cols), me, D),
                                              _place_cols(conv_w_dw[0], me, D)]), 40), "gather_vectors")
    vecs = {
        "norm_g": vec[0:6], "w_dw": vec[6:6 + CONV_WIDTH],
        "b_pw1": conv_b_pw1.reshape(2, D), "b_dw": conv_b_dw, "cg": conv_norm_g, "b_pw2": conv_b_pw2,
        "q_norm": attn_q_norm[0], "k_norm": attn_k_norm[0],
    }

    loss_local, dx, small, tok = _local_step(x[0], loss_target[0], vecs, comm)
    loss = lax.psum(loss_local, ("x", "y", "c"))

    mats = comm.finish(dx)
    hd3 = 3 * HEAD_DIM
    packed = jnp.concatenate([
        small["norm_g"], small["b_pw1"], small["w_dw"], small["b_dw"], small["cg"], small["b_pw2"],
        jnp.pad(small["q_norm"].reshape(1, hd3), ((0, 0), (0, D - hd3))),
        jnp.pad(small["k_norm"].reshape(1, hd3), ((0, 0), (0, D - hd3)))])
    red = _all_sum(_pad_rows(packed, 48), "reduce_vectors")
    col0 = me * w_cols
    grads = {
        "norm_g": lax.dynamic_slice(red[0:6], (0, col0), (6, w_cols)),
        "ffn_w_in": jnp.concatenate([mats[i].T for i in range(4)]),
        "ffn_w_out": jnp.concatenate([mats[i] for i in range(4, 8)]),
        "conv_w_pw1": mats[8].T,
        "conv_b_pw1": red[6:8].reshape(1, 2 * D),
        "conv_w_dw": lax.dynamic_slice(red[8:8 + CONV_WIDTH], (0, col0), (CONV_WIDTH, w_cols)),
        "conv_b_dw": red[39:40], "conv_norm_g": red[40:41], "conv_w_pw2": mats[9], "conv_b_pw2": red[41:42],
        "attn_w_qkv": mats[10].T,
        "attn_q_norm": red[42, :hd3].reshape(3, HEAD_DIM), "attn_k_norm": red[43, :hd3].reshape(3, HEAD_DIM),
        "attn_w_o": mats[11],
    }

    names = ["norm_g", "ffn_w_in", "ffn_w_out", "conv_w_pw1", "conv_b_pw1", "conv_w_dw", "conv_b_dw",
             "conv_norm_g", "conv_w_pw2", "conv_b_pw2", "attn_w_qkv", "attn_q_norm", "attn_k_norm", "attn_w_o"]
    ws = [norm_g, ffn_w_in, ffn_w_out, conv_w_pw1, conv_b_pw1, conv_w_dw, conv_b_dw, conv_norm_g, conv_w_pw2,
          conv_b_pw2, attn_w_qkv, attn_q_norm, attn_k_norm, attn_w_o]
    ms = [m_norm_g, m_ffn_w_in, m_ffn_w_out, m_conv_w_pw1, m_conv_b_pw1, m_conv_w_dw, m_conv_b_dw, m_conv_norm_g,
          m_conv_w_pw2, m_conv_b_pw2, m_attn_w_qkv, m_attn_q_norm, m_attn_k_norm, m_attn_w_o]
    vs = [v_norm_g, v_ffn_w_in, v_ffn_w_out, v_conv_w_pw1, v_conv_b_pw1, v_conv_w_dw, v_conv_b_dw, v_conv_norm_g,
          v_conv_w_pw2, v_conv_b_pw2, v_attn_w_qkv, v_attn_q_norm, v_attn_k_norm, v_attn_w_o]
    g_out, d_out, m_out, v_out = [], [], [], []
    for name, w, m, v in zip(names, ws, ms, vs):
        shape2 = (-1, w.shape[-1]) if w.size > SMALL_PARAM else w.shape
        g2 = grads[name].reshape(shape2)
        delta, new_m, new_v = _adamw(w.reshape(shape2), g2, m.reshape(shape2), v.reshape(shape2), f"adamw_{name}")
        g_out.append(g2.reshape(w.shape))
        d_out.append(delta.reshape(w.shape))
        m_out.append(new_m.reshape(w.shape))
        v_out.append(new_v.reshape(w.shape))
    return (loss, dx.reshape(x.shape), *g_out, *d_out, *m_out, *v_out)
```

```python
import functools

import jax
import jax.numpy as jnp
from jax import lax
from jax.experimental import pallas as pl
from jax.experimental.pallas import tpu as pltpu

F32 = jnp.float32
BF16 = jnp.bfloat16
I32 = jnp.int32

NORM_EPS = 1e-6
LANES = 128
SUBLANES = 8
MXU_WIDTH = 256
HEAD_DIM = 128
ATTN_BLOCK = 128
ATTN_DILATIONS = (1, 4, 16)
CONV_WIDTH = 31
CONV_HALO = 32
CONV_CHUNK = 16
STREAM_TILE = 512
ADAM_LR, ADAM_B1, ADAM_B2, ADAM_EPS, ADAM_WD, ADAM_STEP = 0.001, 0.9, 0.999, 1e-08, 0.01, 10
N_DEV = 8
SMALL_PARAM = 8192
NEG_BIG = -1e30
V7X_VMEM_BYTES = 64 * 1024 * 1024
VMEM_LIMIT = V7X_VMEM_BYTES - 8 * 1024 * 1024
MESH = pl.DeviceIdType.MESH
ANY = pl.BlockSpec(memory_space=pl.ANY)


def _params(*sem):
    return pltpu.CompilerParams(dimension_semantics=sem or None, vmem_limit_bytes=VMEM_LIMIT)


def _dot_nn(a, b):
    return lax.dot_general(a, b, (((1,), (0,)), ((), ())), preferred_element_type=F32)


def _dot_nt(a, b):
    return lax.dot_general(a, b, (((1,), (1,)), ((), ())), preferred_element_type=F32)


def _dot_tn(a, b):
    return lax.dot_general(a, b, (((0,), (0,)), ((), ())), preferred_element_type=F32)


def _sigmoid(x):
    return 1.0 / (1.0 + jnp.exp(-x))


def _tile(n, pref):
    if n <= pref:
        return n
    for t in range(pref - pref % 128, 0, -128):
        if n % t == 0:
            return t
    return n


def _to_lane_chunks(dst, val):
    for c in range(dst.shape[0]):
        dst[c] = val[:, c * LANES:(c + 1) * LANES]


def _rows8(v):
    tm, c = v.shape
    return v.reshape(tm // 8, 8, c).sum(axis=0)


def _rmsnorm(x, g, name, after=()):
    T, D = x.shape
    tm = min(T, 512)

    def body(x_ref, g_ref, *rest):
        h_ref = rest[-1]
        xv = x_ref[...]
        r = lax.rsqrt(jnp.mean(xv * xv, axis=-1, keepdims=True) + NORM_EPS)
        h_ref[...] = (xv * r * g_ref[...]).astype(BF16)

    return pl.pallas_call(
        body, name=name, grid=(T // tm,),
        in_specs=[pl.BlockSpec((tm, D), lambda i: (i, 0)), pl.BlockSpec((1, D), lambda i: (0, 0))] + [ANY] * len(after),
        out_specs=pl.BlockSpec((tm, D), lambda i: (i, 0)),
        out_shape=jax.ShapeDtypeStruct((T, D), BF16),
        compiler_params=_params("parallel"),
    )(x, g, *after)


def _rmsnorm_streams(x, g, name):
    T, D = x.shape
    tm = min(T, STREAM_TILE)
    dils = [d for d in ATTN_DILATIONS if d > 1]

    def body(x_ref, g_ref, h_ref, *rest):
        outs, hs = rest[:-1], rest[-1]
        xv = x_ref[...]
        r = lax.rsqrt(jnp.mean(xv * xv, axis=-1, keepdims=True) + NORM_EPS)
        hf = xv * r * g_ref[...]
        h_ref[...] = hf.astype(BF16)
        _to_lane_chunks(hs, hf)
        for d, o_ref in zip(dils, outs):
            for s in range(d):
                for c in range(D // LANES):
                    o_ref[s, :, c * LANES:(c + 1) * LANES] = hs.at[c][pl.ds(s, tm // d, stride=d), :].astype(BF16)

    res = pl.pallas_call(
        body, name=name, grid=(T // tm,),
        in_specs=[pl.BlockSpec((tm, D), lambda i: (i, 0)), pl.BlockSpec((1, D), lambda i: (0, 0))],
        out_specs=[pl.BlockSpec((tm, D), lambda i: (i, 0))]
        + [pl.BlockSpec((d, tm // d, D), lambda i: (0, i, 0)) for d in dils],
        out_shape=[jax.ShapeDtypeStruct((T, D), BF16)] + [jax.ShapeDtypeStruct((d, T // d, D), BF16) for d in dils],
        scratch_shapes=[pltpu.VMEM((D // LANES, tm, LANES), F32)],
        compiler_params=_params("parallel"),
    )(x, g)
    return [res[0]] + [a.reshape(T, D) for a in res[1:]]


def _gated_in(h, wt, bias, act, u_dtype, name, after=()):
    T, D = h.shape
    F = wt.shape[0] // 2
    tn = _tile(F, 1408)
    tm = min(T, 1024)
    nn = F // tn

    def body(*refs):
        z_ref, u_ref = refs[-2:]
        if bias is None:
            h_ref, w0_ref, w1_ref = refs[:3]
        else:
            h_ref, w0_ref, w1_ref, b_ref = refs[:4]
        hv = h_ref[...]
        for c0 in range(0, tn, MXU_WIDTH):
            cs = slice(c0, min(c0 + MXU_WIDTH, tn))
            z0 = _dot_nt(hv, w0_ref[cs, :])
            z1 = _dot_nt(hv, w1_ref[cs, :])
            if bias is not None:
                z0 = z0 + b_ref[0:1, cs]
                z1 = z1 + b_ref[1:2, cs]
            z_ref[0, :, cs] = z0.astype(BF16)
            z_ref[1, :, cs] = z1.astype(BF16)
            if act == "swiglu":
                u = z0 * _sigmoid(z0) * z1
            else:
                u = z0 * _sigmoid(z1)
            u_ref[:, cs] = u.astype(u_dtype)

    in_specs = [pl.BlockSpec((tm, D), lambda n, m: (m, 0)),
                pl.BlockSpec((tn, D), lambda n, m: (n, 0)),
                pl.BlockSpec((tn, D), lambda n, m: (n + nn, 0))]
    args = [h, wt, wt]
    if bias is not None:
        in_specs.append(pl.BlockSpec((2, tn), lambda n, m: (0, n)))
        args.append(bias)
    in_specs += [ANY] * len(after)
    args += list(after)
    return pl.pallas_call(
        body, name=name, grid=(nn, T // tm), in_specs=in_specs,
        out_specs=[pl.BlockSpec((2, tm, tn), lambda n, m: (0, m, n)), pl.BlockSpec((tm, tn), lambda n, m: (m, n))],
        out_shape=[jax.ShapeDtypeStruct((2, T, F), BF16), jax.ShapeDtypeStruct((T, F), u_dtype)],
        compiler_params=_params("parallel", "parallel"),
    )(*args)


def _swiglu_bwd(dxo, wout, z3, scale, name, after=()):
    T, D = dxo.shape
    F = wout.shape[0]
    tn = _tile(F, 1408)
    tm = min(T, 512)

    def body(d_ref, w_ref, z_ref, *rest):
        dz_ref = rest[-1]
        dy = (d_ref[...] * scale).astype(BF16)
        da = _dot_nt(dy, w_ref[...])
        gate = z_ref[0].astype(F32)
        up = z_ref[1].astype(F32)
        sg = _sigmoid(gate)
        dz_ref[0] = (da * up * (sg * (1.0 + gate * (1.0 - sg)))).astype(BF16)
        dz_ref[1] = (da * gate * sg).astype(BF16)

    return pl.pallas_call(
        body, name=name, grid=(F // tn, T // tm),
        in_specs=[pl.BlockSpec((tm, D), lambda n, m: (m, 0)),
                  pl.BlockSpec((tn, D), lambda n, m: (n, 0)),
                  pl.BlockSpec((2, tm, tn), lambda n, m: (0, m, n))] + [ANY] * len(after),
        out_specs=pl.BlockSpec((2, tm, tn), lambda n, m: (0, m, n)),
        out_shape=jax.ShapeDtypeStruct((2, T, F), BF16),
        compiler_params=_params("parallel", "parallel"),
    )(dxo, wout, z3, *after)


def _mm_nt(a, wt, out_dtype, name, *, w_row0=0, n_rows=None, streams=1, after=()):
    T, K = a.shape
    N = wt.shape[0] if n_rows is None else n_rows
    L = T // streams
    tn = _tile(N, 1152)
    tm = min(L, 512)
    mps = L // tm
    npn = N // tn
    assert w_row0 % tn == 0
    wb0 = w_row0 // tn

    def body(a_ref, w_ref, *rest):
        o_ref = rest[-1]
        o_ref[...] = _dot_nt(a_ref[...].astype(BF16), w_ref[...]).astype(out_dtype)

    return pl.pallas_call(
        body, name=name, grid=(npn, T // tm),
        in_specs=[pl.BlockSpec((tm, K), lambda n, m: (m, 0)), pl.BlockSpec((tn, K), lambda n, m: (wb0 + n, 0))]
        + [ANY] * len(after),
        out_specs=pl.BlockSpec((tm, tn), lambda n, m: (m % mps, (m // mps) * npn + n)),
        out_shape=jax.ShapeDtypeStruct((L, streams * N), out_dtype),
        compiler_params=_params("parallel", "parallel"),
    )(a, wt, *after)


def _mm_tn(a, b, b_scale, name, *, streams=1):
    if a.ndim == 3:
        J, T, F = a.shape
    elif streams > 1:
        J, T, F = 1, a.shape[0] * streams, a.shape[1] // streams
    else:
        (T, F), J = a.shape, 1
    N = b.shape[1]
    tmm = _tile(F, 1408)
    tk = min(T // streams, 2048 if b.dtype == BF16 else 1024)
    npj = F // tmm
    nk = T // tk
    kps = nk // streams

    def body(a_ref, b_ref, o_ref, acc_ref):
        k = pl.program_id(1)
        bv = b_ref[...]
        if b_scale != 1.0:
            bv = bv * b_scale
        part = _dot_tn(a_ref[...].astype(BF16), bv.astype(BF16))

        @pl.when(k == 0)
        def _():
            acc_ref[...] = part

        @pl.when(k > 0)
        def _():
            acc_ref[...] += part

        @pl.when(k == nk - 1)
        def _():
            o_ref[...] = acc_ref[...].astype(BF16)

    if a.ndim == 3:
        a_spec = pl.BlockSpec((None, tk, tmm), lambda i, k: (i // npj, k, i % npj))
    elif streams > 1:
        a_spec = pl.BlockSpec((tk, tmm), lambda i, k: (k % kps, (k // kps) * npj + i))
    else:
        a_spec = pl.BlockSpec((tk, tmm), lambda i, k: (k, i))
    return pl.pallas_call(
        body, name=name, grid=(J * npj, nk),
        in_specs=[a_spec, pl.BlockSpec((tk, N), lambda i, k: (k, 0))],
        out_specs=pl.BlockSpec((tmm, N), lambda i, k: (i, 0)),
        out_shape=jax.ShapeDtypeStruct((J * F, N), BF16),
        scratch_shapes=[pltpu.VMEM((tmm, N), F32)],
        compiler_params=_params("parallel", "arbitrary"),
    )(a, b)


def _rows_mm(a, b, mode, name, *, x=None, scale=1.0, bias=None, tgt=None, g=None, dxo=None,
             b_row0=0, streams=1, extras=(), after=()):
    if a.ndim == 3:
        J, T, F = a.shape
    elif streams > 1:
        J, T, F = 1, a.shape[0] * streams, a.shape[1] // streams
    else:
        (T, F), J = a.shape, 1
    D = b.shape[1]
    tk = _tile(F, 2816)
    kpj = F // tk
    nk = J * kpj
    tm = min(T // streams, STREAM_TILE)
    nm = T // tm
    mps = nm // streams
    assert b_row0 % tk == 0
    kb0 = b_row0 // tk
    n_ex = len(extras)

    def body(*refs):
        refs = list(refs)
        a_ref, b_ref = refs[:2]
        rest = refs[2 + len(after):]
        if n_ex:
            tmp_ref = rest.pop()
        acc_ref = rest.pop()
        ex_refs, rest = rest[:n_ex], rest[n_ex:]
        m, k = pl.program_id(0), pl.program_id(1)
        part = _dot_nn(a_ref[...], b_ref[...])

        @pl.when(k == 0)
        def _():
            acc_ref[...] = part

        @pl.when(k > 0)
        def _():
            acc_ref[...] += part

        @pl.when(k == nk - 1)
        def _():
            acc = acc_ref[...]
            for (d, _), e_ref in zip(extras, ex_refs):
                for s in range(d):
                    for c in range(D // LANES):
                        tmp_ref.at[c][pl.ds(s, tm // d, stride=d), :] = e_ref[s, :, c * LANES:(c + 1) * LANES]
                acc = acc + jnp.concatenate([tmp_ref[c] for c in range(D // LANES)], axis=1)
            if mode == "plain":
                (o_ref,) = rest
                o_ref[...] = acc
            elif mode == "res":
                ins = list(rest)
                x_ref = ins.pop(0)
                if bias is not None:
                    acc = acc + ins.pop(0)[...]
                xn = x_ref[...] + scale * acc
                if g is None:
                    (o_ref,) = ins
                else:
                    g_ref, o_ref, h_ref = ins
                    r = lax.rsqrt(jnp.mean(xn * xn, axis=-1, keepdims=True) + NORM_EPS)
                    h_ref[...] = (xn * r * g_ref[...]).astype(BF16)
                o_ref[...] = xn
            elif mode == "loss":
                x_ref, t_ref, o_ref, l_ref = rest
                e = x_ref[...] + scale * acc - t_ref[...]
                o_ref[...] = e / D

                @pl.when(m == 0)
                def _():
                    l_ref[...] = jnp.zeros_like(l_ref)

                l_ref[...] += _rows8(e * e)
            else:
                x_ref, g_ref, d_ref, o_ref, dg_ref = rest
                xv = x_ref[...]
                r = lax.rsqrt(jnp.mean(xv * xv, axis=-1, keepdims=True) + NORM_EPS)
                xh = xv * r
                dxh = acc * g_ref[...]
                o_ref[...] = d_ref[...] + r * (dxh - xh * jnp.mean(dxh * xh, axis=-1, keepdims=True))

                @pl.when(m == 0)
                def _():
                    dg_ref[...] = jnp.zeros_like(dg_ref)

                dg_ref[...] += _rows8(acc * xh)

    if a.ndim == 3:
        a_spec = pl.BlockSpec((None, tm, tk), lambda m, k: (k // kpj, m, k % kpj))
    elif streams > 1:
        a_spec = pl.BlockSpec((tm, tk), lambda m, k: (m % mps, (m // mps) * kpj + k))
    else:
        a_spec = pl.BlockSpec((tm, tk), lambda m, k: (m, k))
    row = pl.BlockSpec((tm, D), lambda m, k: (m, 0))
    vec = pl.BlockSpec((1, D), lambda m, k: (0, 0))
    part8 = pl.BlockSpec((8, D), lambda m, k: (0, 0))
    in_specs = [a_spec, pl.BlockSpec((tk, D), lambda m, k: (kb0 + k, 0))] + [ANY] * len(after)
    args = [a, b, *after]
    for d, e in extras:
        in_specs.append(pl.BlockSpec((d, tm // d, D), lambda m, k: (0, m, 0)))
        args.append(e.reshape(d, T // d, D))
    full = jax.ShapeDtypeStruct((T, D), F32)
    small = jax.ShapeDtypeStruct((8, D), F32)
    if mode == "plain":
        out_specs, out_shape = row, full
    elif mode == "res":
        in_specs.append(row)
        args.append(x)
        if bias is not None:
            in_specs.append(vec)
            args.append(bias)
        if g is None:
            out_specs, out_shape = row, full
        else:
            in_specs.append(vec)
            args.append(g)
            out_specs, out_shape = [row, row], [full, jax.ShapeDtypeStruct((T, D), BF16)]
    elif mode == "loss":
        in_specs += [row, row]
        args += [x, tgt]
        out_specs, out_shape = [row, part8], [full, small]
    else:
        in_specs += [row, vec, row]
        args += [x, g, dxo]
        out_specs, out_shape = [row, part8], [full, small]
    return pl.pallas_call(
        body, name=name, grid=(nm, nk), in_specs=in_specs, out_specs=out_specs, out_shape=out_shape,
        scratch_shapes=[pltpu.VMEM((tm, D), F32)] + ([pltpu.VMEM((D // LANES, tm, LANES), F32)] if n_ex else []),
        compiler_params=_params("arbitrary", "arbitrary"),
    )(*args)


def _sublane_phases(sh):
    rows = sh.shape[1] - SUBLANES
    for p in range(1, SUBLANES):
        sh[p, pl.ds(0, rows), :] = sh[0, pl.ds(p, rows), :]


def _shifted(sh, offset, j, rows=SUBLANES):
    start = pl.multiple_of(j * rows + (offset - offset % SUBLANES), SUBLANES)
    return sh[offset % SUBLANES, pl.ds(start, rows), :]


def _conv_taps(wb, sh, offsets, j, init):
    groups = CONV_CHUNK // SUBLANES
    accs = [init[g * SUBLANES:(g + 1) * SUBLANES] for g in range(groups)]
    for k, off in enumerate(offsets):
        w = wb[k]
        start = pl.multiple_of(j * CONV_CHUNK + (off - off % SUBLANES), SUBLANES)
        for g in range(groups):
            accs[g] = accs[g] + w * sh[off % SUBLANES, pl.ds(start + g * SUBLANES, SUBLANES), :]
    return jnp.concatenate(accs, axis=0)


def _broadcast_rows(dst, src):
    for r in range(dst.shape[0]):
        dst[r] = jnp.zeros(dst.shape[1:], F32) + src[r:r + 1, :]


def _dwconv_fwd(u, w, b_dw, cg, name):
    T, C = u.shape
    tm = min(T, 256)
    hb = tm // CONV_HALO
    pad = CONV_HALO - (CONV_WIDTH - 1)

    def body(u_ref, halo_ref, w_ref, b_ref, g_ref, v_ref, s_ref, sh, wb):
        i = pl.program_id(0)
        sh[0, pl.ds(0, CONV_HALO), :] = jnp.where(i > 0, halo_ref[...], 0.0)
        sh[0, pl.ds(CONV_HALO, tm), :] = u_ref[...]
        _sublane_phases(sh)
        _broadcast_rows(wb, w_ref)
        bias = jnp.zeros((CONV_CHUNK, C), F32) + b_ref[...]

        def chunk(j, carry):
            acc = _conv_taps(wb, sh, [pad + k for k in range(CONV_WIDTH)], j, bias)
            v_ref[pl.ds(pl.multiple_of(j * CONV_CHUNK, CONV_CHUNK), CONV_CHUNK), :] = acc
            return carry

        lax.fori_loop(0, tm // CONV_CHUNK, chunk, 0)
        acc = v_ref[...]
        r = lax.rsqrt(jnp.mean(acc * acc, axis=-1, keepdims=True) + NORM_EPS)
        n = acc * r * g_ref[...]
        s_ref[...] = (n * _sigmoid(n)).astype(BF16)

    row = pl.BlockSpec((tm, C), lambda i: (i, 0))
    vec = pl.BlockSpec((1, C), lambda i: (0, 0))
    return pl.pallas_call(
        body, name=name, grid=(T // tm,),
        in_specs=[row, pl.BlockSpec((CONV_HALO, C), lambda i: (jnp.maximum(i * hb - 1, 0), 0)),
                  pl.BlockSpec((CONV_WIDTH, C), lambda i: (0, 0)), vec, vec],
        out_specs=[row, row],
        out_shape=[jax.ShapeDtypeStruct((T, C), F32), jax.ShapeDtypeStruct((T, C), BF16)],
        scratch_shapes=[pltpu.VMEM((SUBLANES, CONV_HALO + tm, C), F32), pltpu.VMEM((CONV_WIDTH, SUBLANES, C), F32)],
        compiler_params=_params("parallel"),
    )(u, u, w, b_dw, cg)


def _conv_post_bwd(ds, v, cg, dxo, name):
    T, C = v.shape
    tm = min(T, 512)

    def body(ds_ref, v_ref, g_ref, d_ref, dv_ref, dcg_ref, dbdw_ref, dbpw2_ref):
        @pl.when(pl.program_id(0) == 0)
        def _():
            dcg_ref[...] = jnp.zeros_like(dcg_ref)
            dbdw_ref[...] = jnp.zeros_like(dbdw_ref)
            dbpw2_ref[...] = jnp.zeros_like(dbpw2_ref)

        vv = v_ref[...]
        r = lax.rsqrt(jnp.mean(vv * vv, axis=-1, keepdims=True) + NORM_EPS)
        vh = vv * r
        n = vh * g_ref[...]
        sg = _sigmoid(n)
        dn = ds_ref[...] * (sg * (1.0 + n * (1.0 - sg)))
        dvh = dn * g_ref[...]
        dv = r * (dvh - vh * jnp.mean(dvh * vh, axis=-1, keepdims=True))
        dv_ref[...] = dv
        dcg_ref[...] += _rows8(dn * vh)
        dbdw_ref[...] += _rows8(dv)
        dbpw2_ref[...] += _rows8(d_ref[...])

    row = pl.BlockSpec((tm, C), lambda i: (i, 0))
    part8 = pl.BlockSpec((8, C), lambda i: (0, 0))
    small = jax.ShapeDtypeStruct((8, C), F32)
    return pl.pallas_call(
        body, name=name, grid=(T // tm,),
        in_specs=[row, row, pl.BlockSpec((1, C), lambda i: (0, 0)), row],
        out_specs=[row, part8, part8, part8],
        out_shape=[jax.ShapeDtypeStruct((T, C), F32), small, small, small],
        compiler_params=_params("arbitrary"),
    )(ds, v, cg, dxo)


def _dwconv_bwd(dv, u, w, z3, name):
    T, C = u.shape
    tm = min(T, 256)
    hb = tm // CONV_HALO
    nt = T // tm
    pad = CONV_HALO - (CONV_WIDTH - 1)

    taps_at_once = 4
    chunks_at_once = 8

    def body(dv_ref, dvn_ref, u_ref, up_ref, w_ref, z_ref, dz_ref, dw_ref, db_ref, dvsh, ush, wb, du_ref, dwacc):
        i = pl.program_id(0)

        @pl.when(i == 0)
        def _():
            dwacc[...] = jnp.zeros_like(dwacc)
            db_ref[...] = jnp.zeros_like(db_ref)

        dvsh[0, pl.ds(0, tm), :] = dv_ref[...]
        dvsh[0, pl.ds(tm, CONV_HALO), :] = jnp.where(i < nt - 1, dvn_ref[...], 0.0)
        ush[0, pl.ds(0, CONV_HALO), :] = jnp.where(i > 0, up_ref[...], 0.0)
        ush[0, pl.ds(CONV_HALO, tm), :] = u_ref[...]
        _sublane_phases(dvsh)
        _sublane_phases(ush)
        _broadcast_rows(wb, w_ref)

        def du_chunk(j, carry):
            acc = _conv_taps(wb, dvsh, [CONV_WIDTH - 1 - k for k in range(CONV_WIDTH)], j,
                             jnp.zeros((CONV_CHUNK, C), F32))
            du_ref[pl.ds(pl.multiple_of(j * CONV_CHUNK, CONV_CHUNK), CONV_CHUNK), :] = acc
            return carry

        lax.fori_loop(0, tm // CONV_CHUNK, du_chunk, 0)

        for k0 in range(0, CONV_WIDTH, taps_at_once):
            taps = range(k0, min(k0 + taps_at_once, CONV_WIDTH))

            def dw_chunks(jj, carry, taps=taps):
                accs = [jnp.zeros((SUBLANES, C), F32) for _ in taps]
                for u_ in range(chunks_at_once):
                    j = jj * chunks_at_once + u_
                    dvc = dv_ref[pl.ds(pl.multiple_of(j * SUBLANES, SUBLANES), SUBLANES), :]
                    accs = [acc + dvc * _shifted(ush, pad + k, j) for acc, k in zip(accs, taps)]
                for k, acc in zip(taps, accs):
                    dwacc[k] += acc
                return carry

            lax.fori_loop(0, tm // (SUBLANES * chunks_at_once), dw_chunks, 0)

        du = du_ref[...]
        a = z_ref[0].astype(F32)
        gate = z_ref[1].astype(F32)
        sg = _sigmoid(gate)
        da = du * sg
        dgate = du * a * sg * (1.0 - sg)
        dz_ref[0] = da.astype(BF16)
        dz_ref[1] = dgate.astype(BF16)
        db_ref[0:8, :] += _rows8(da)
        db_ref[8:16, :] += _rows8(dgate)

        @pl.when(i == nt - 1)
        def _():
            dw_ref[...] = jnp.zeros_like(dw_ref)
            for k in range(CONV_WIDTH):
                dw_ref[k:k + 1, :] = jnp.sum(dwacc[k], axis=0, keepdims=True)

    row = pl.BlockSpec((tm, C), lambda i: (i, 0))
    last_halo = T // CONV_HALO - 1
    return pl.pallas_call(
        body, name=name, grid=(nt,),
        in_specs=[row, pl.BlockSpec((CONV_HALO, C), lambda i: (jnp.minimum((i + 1) * hb, last_halo), 0)),
                  row, pl.BlockSpec((CONV_HALO, C), lambda i: (jnp.maximum(i * hb - 1, 0), 0)),
                  pl.BlockSpec((CONV_WIDTH, C), lambda i: (0, 0)),
                  pl.BlockSpec((2, tm, C), lambda i: (0, i, 0))],
        out_specs=[pl.BlockSpec((2, tm, C), lambda i: (0, i, 0)),
                   pl.BlockSpec((CONV_HALO, C), lambda i: (0, 0)),
                   pl.BlockSpec((16, C), lambda i: (0, 0))],
        out_shape=[jax.ShapeDtypeStruct((2, T, C), BF16), jax.ShapeDtypeStruct((CONV_HALO, C), F32),
                   jax.ShapeDtypeStruct((16, C), F32)],
        scratch_shapes=[pltpu.VMEM((SUBLANES, tm + CONV_HALO, C), F32), pltpu.VMEM((SUBLANES, CONV_HALO + tm, C), F32),
                        pltpu.VMEM((CONV_WIDTH, SUBLANES, C), F32), pltpu.VMEM((tm, C), F32),
                        pltpu.VMEM((CONV_WIDTH, SUBLANES, C), F32)],
        compiler_params=_params("arbitrary"),
    )(dv, dv, u, u, w, z3)


def _head_norm(raw, gain):
    xv = raw.astype(F32)
    r = lax.rsqrt(jnp.mean(xv * xv, axis=-1, keepdims=True) + NORM_EPS)
    xh = xv * r
    return (xh * gain).astype(BF16), xh, r


def _band_mask(n):
    row = lax.broadcasted_iota(I32, (ATTN_BLOCK, 2 * ATTN_BLOCK), 0)
    col = lax.broadcasted_iota(I32, (ATTN_BLOCK, 2 * ATTN_BLOCK), 1)
    return (col >= row) & (col <= row + ATTN_BLOCK) & ((col >= ATTN_BLOCK) | (n > 0))


def _carry_previous(n, bufs):
    @pl.when(n == 0)
    def _():
        for b in bufs:
            b[0:ATTN_BLOCK, :] = jnp.zeros((ATTN_BLOCK, b.shape[1]), b.dtype)

    @pl.when(n > 0)
    def _():
        for b in bufs:
            b[0:ATTN_BLOCK, :] = b[ATTN_BLOCK:, :]


def _attn_fwd(qkv, qg, kg, g, name):
    d = ATTN_DILATIONS[g]
    L = qkv.shape[0]
    HD = qkv.shape[1] // (3 * d)
    H = HD // HEAD_DIM
    nb = L // ATTN_BLOCK
    scale = HEAD_DIM ** -0.5

    def body(q_ref, kc_ref, vc_ref, qg_ref, kg_ref, o_ref, lse_ref, kk, vv):
        n = pl.program_id(1)
        _carry_previous(n, [kk, vv])
        mask = _band_mask(n)
        lane = lax.broadcasted_iota(I32, (ATTN_BLOCK, 128), 1)
        gq = qg_ref[g:g + 1, :]
        gk = kg_ref[g:g + 1, :]
        vv[ATTN_BLOCK:, :] = vc_ref[...]
        lse_tile = jnp.zeros((ATTN_BLOCK, 128), F32)
        for h in range(H):
            sl = slice(h * HEAD_DIM, (h + 1) * HEAD_DIM)
            q = _head_norm(q_ref[:, sl], gq)[0]
            kk[ATTN_BLOCK:, sl] = _head_norm(kc_ref[:, sl], gk)[0]
            s = jnp.where(mask, _dot_nt(q, kk[:, sl]) * scale, NEG_BIG)
            m = jnp.max(s, axis=-1, keepdims=True)
            p = jnp.exp(s - m)
            l = jnp.sum(p, axis=-1, keepdims=True)
            o_ref[:, sl] = _dot_nn(p.astype(BF16), vv[:, sl]) / l
            lse_tile = jnp.where(lane == h, m + jnp.log(l), lse_tile)
        lse_ref[...] = lse_tile

    def blk(which):
        return pl.BlockSpec((ATTN_BLOCK, HD), lambda r, n: (n, r * 3 + which))

    gains = pl.BlockSpec((3, HEAD_DIM), lambda r, n: (0, 0))
    return pl.pallas_call(
        body, name=name, grid=(d, nb),
        in_specs=[blk(0), blk(1), blk(2), gains, gains],
        out_specs=[pl.BlockSpec((ATTN_BLOCK, HD), lambda r, n: (n, r)),
                   pl.BlockSpec((ATTN_BLOCK, 128), lambda r, n: (n, r))],
        out_shape=[jax.ShapeDtypeStruct((L, d * HD), F32), jax.ShapeDtypeStruct((L, d * 128), F32)],
        scratch_shapes=[pltpu.VMEM((2 * ATTN_BLOCK, HD), BF16), pltpu.VMEM((2 * ATTN_BLOCK, HD), BF16)],
        compiler_params=_params("arbitrary", "arbitrary"),
    )(qkv, qkv, qkv, qg, kg)


def _attn_merge(os, lses, name):
    T = os[0].shape[0]
    HD = os[0].shape[1]
    H = HD // HEAD_DIM
    tm = min(T, STREAM_TILE)
    dils = ATTN_DILATIONS[1:]

    def body(o0, o1, o2, l0, l1, l2, ob_ref, of_ref, lse_ref, n1, n2, m1, m2):
        for d, src, dst, w in ((dils[0], o1, n1, HD), (dils[1], o2, n2, HD), (dils[0], l1, m1, 128), (dils[1], l2, m2, 128)):
            for s in range(d):
                for c in range(w // LANES):
                    dst.at[c][pl.ds(s, tm // d, stride=d), :] = src[:, s * w + c * LANES:s * w + (c + 1) * LANES]
        ls = [l0[...], m1[0], m2[0]]
        m = jnp.maximum(jnp.maximum(ls[0], ls[1]), ls[2])
        tot = m + jnp.log(jnp.exp(ls[0] - m) + jnp.exp(ls[1] - m) + jnp.exp(ls[2] - m))
        lse_ref[...] = tot
        ws = [jnp.exp(l - tot) for l in ls]
        for h in range(H):
            sl = slice(h * HEAD_DIM, (h + 1) * HEAD_DIM)
            out = (ws[0][:, h:h + 1] * o0[:, sl] + ws[1][:, h:h + 1] * n1[h]) + ws[2][:, h:h + 1] * n2[h]
            of_ref[:, sl] = out
            ob_ref[:, sl] = out.astype(BF16)

    def wide(d, w):
        return pl.BlockSpec((tm // d, d * w), lambda i: (i, 0))

    row = pl.BlockSpec((tm, HD), lambda i: (i, 0))
    nar = pl.BlockSpec((tm, 128), lambda i: (i, 0))
    return pl.pallas_call(
        body, name=name, grid=(T // tm,),
        in_specs=[row, wide(dils[0], HD), wide(dils[1], HD), nar, wide(dils[0], 128), wide(dils[1], 128)],
        out_specs=[row, row, nar],
        out_shape=[jax.ShapeDtypeStruct((T, HD), BF16), jax.ShapeDtypeStruct((T, HD), F32),
                   jax.ShapeDtypeStruct((T, 128), F32)],
        scratch_shapes=[pltpu.VMEM((H, tm, HEAD_DIM), F32), pltpu.VMEM((H, tm, HEAD_DIM), F32),
                        pltpu.VMEM((1, tm, 128), F32), pltpu.VMEM((1, tm, 128), F32)],
        compiler_params=_params("parallel"),
    )(*os, *lses)


def _attn_delta(out, dout, lse, name):
    T, HD = out.shape
    H = HD // HEAD_DIM
    tm = min(T, STREAM_TILE)
    dils = ATTN_DILATIONS[1:]

    def body(o_ref, d_ref, l_ref, dl_ref, *rest):
        dch = rest[-1]
        lane = lax.broadcasted_iota(I32, (tm, 128), 1)
        tile = jnp.zeros((tm, 128), F32)
        for h in range(H):
            sl = slice(h * HEAD_DIM, (h + 1) * HEAD_DIM)
            dch[h] = d_ref[:, sl]
            tile = jnp.where(lane == h, jnp.sum(o_ref[:, sl] * d_ref[:, sl], axis=-1, keepdims=True), tile)
        dl_ref[...] = tile
        for i, d in enumerate(dils):
            dw_ref, lw_ref, dlw_ref = rest[3 * i:3 * i + 3]
            for s in range(d):
                rows = pl.ds(s, tm // d, stride=d)
                for h in range(H):
                    dw_ref[:, s * HD + h * HEAD_DIM:s * HD + (h + 1) * HEAD_DIM] = dch.at[h][rows, :].astype(BF16)
                lw_ref[:, s * 128:(s + 1) * 128] = l_ref[rows, :]
                dlw_ref[:, s * 128:(s + 1) * 128] = dl_ref[rows, :]

    row = pl.BlockSpec((tm, HD), lambda i: (i, 0))
    nar = pl.BlockSpec((tm, 128), lambda i: (i, 0))
    out_specs, out_shape = [nar], [jax.ShapeDtypeStruct((T, 128), F32)]
    for d in dils:
        out_specs += [pl.BlockSpec((tm // d, d * w), lambda i: (i, 0)) for w in (HD, 128, 128)]
        out_shape += [jax.ShapeDtypeStruct((T // d, d * HD), BF16), jax.ShapeDtypeStruct((T // d, d * 128), F32),
                      jax.ShapeDtypeStruct((T // d, d * 128), F32)]
    return pl.pallas_call(
        body, name=name, grid=(T // tm,), in_specs=[row, row, nar], out_specs=out_specs, out_shape=out_shape,
        scratch_shapes=[pltpu.VMEM((H, tm, HEAD_DIM), F32)],
        compiler_params=_params("parallel"),
    )(out, dout, lse)


def _attn_bwd(qkv, dout, lse, delta, qg, kg, g, name):
    d = ATTN_DILATIONS[g]
    L = qkv.shape[0]
    HD = qkv.shape[1] // (3 * d)
    H = HD // HEAD_DIM
    nb = L // ATTN_BLOCK
    total = d * nb
    scale = HEAD_DIM ** -0.5

    def body(*refs):
        (q_ref, kc_ref, vc_ref, do_ref, lse_ref, dl_ref, qg_ref, kg_ref,
         out_ref, gg_ref, dq_a, dk_a, dv_a, dq_b, dk_b, dv_b, kk, vv, khh, rkk) = refs
        t = pl.program_id(0)
        n = t % nb
        gq = qg_ref[g:g + 1, :]
        gk = kg_ref[g:g + 1, :]
        _carry_previous(t, [kk, vv, khh, rkk])

        @pl.when(t == 0)
        def _():
            gg_ref[...] = jnp.zeros_like(gg_ref)
            dk_a[...] = jnp.zeros_like(dk_a)
            dv_a[...] = jnp.zeros_like(dv_a)

        @pl.when(t < total)
        def _():
            mask = _band_mask(n)
            vv[ATTN_BLOCK:, :] = vc_ref[...]
            gq_sum = jnp.zeros((1, HEAD_DIM), F32)
            for h in range(H):
                sl = slice(h * HEAD_DIM, (h + 1) * HEAD_DIM)
                qn, qh, rq = _head_norm(q_ref[:, sl], gq)
                kcn, kch, rkc = _head_norm(kc_ref[:, sl], gk)
                kk[ATTN_BLOCK:, sl] = kcn
                khh[ATTN_BLOCK:, sl] = kch
                rkk[ATTN_BLOCK:, sl] = jnp.broadcast_to(rkc, (ATTN_BLOCK, HEAD_DIM))
                kn = kk[:, sl]
                p = jnp.where(mask, jnp.exp(_dot_nt(qn, kn) * scale - lse_ref[:, h:h + 1]), 0.0)
                do = do_ref[:, sl].astype(BF16)
                ds = (p * (_dot_nt(do, vv[:, sl]) - dl_ref[:, h:h + 1]) * scale).astype(BF16)
                dqn = _dot_nn(ds, kn)
                dqh = dqn * gq
                dq_b[:, sl] = rq * (dqh - qh * jnp.mean(dqh * qh, axis=-1, keepdims=True))
                gq_sum = gq_sum + jnp.sum(dqn * qh, axis=0, keepdims=True)
                dkn = _dot_tn(ds, qn)
                dvn = _dot_tn(p.astype(BF16), do)
                dk_a[:, sl] += dkn[0:ATTN_BLOCK]
                dv_a[:, sl] += dvn[0:ATTN_BLOCK]
                dk_b[:, sl] = dkn[ATTN_BLOCK:]
                dv_b[:, sl] = dvn[ATTN_BLOCK:]
            gg_ref[0:1, :] += gq_sum

        @pl.when(t > 0)
        def _():
            out_ref[:, 0:HD] = dq_a[...].astype(BF16)
            gk_sum = jnp.zeros((1, HEAD_DIM), F32)
            for h in range(H):
                sl = slice(h * HEAD_DIM, (h + 1) * HEAD_DIM)
                kh = khh[0:ATTN_BLOCK, sl]
                rk = rkk[0:ATTN_BLOCK, sl]
                dkn = dk_a[:, sl]
                dkh = dkn * gk
                dk = rk * (dkh - kh * jnp.mean(dkh * kh, axis=-1, keepdims=True))
                out_ref[:, HD + h * HEAD_DIM:HD + (h + 1) * HEAD_DIM] = dk.astype(BF16)
                gk_sum = gk_sum + jnp.sum(dkn * kh, axis=0, keepdims=True)
            out_ref[:, 2 * HD:3 * HD] = dv_a[...].astype(BF16)
            gg_ref[1:2, :] += gk_sum

        @pl.when(t < total)
        def _():
            dq_a[...] = dq_b[...]
            dk_a[...] = dk_b[...]
            dv_a[...] = dv_b[...]

    def at(t):
        t = jnp.minimum(t, total - 1)
        return t % nb, t // nb

    def blk(which):
        return pl.BlockSpec((ATTN_BLOCK, HD), lambda t: (at(t)[0], at(t)[1] * 3 + which))

    gains = pl.BlockSpec((3, HEAD_DIM), lambda t: (0, 0))
    nar = pl.BlockSpec((ATTN_BLOCK, 128), lambda t: at(t))
    acc = pltpu.VMEM((ATTN_BLOCK, HD), F32)
    pair16 = pltpu.VMEM((2 * ATTN_BLOCK, HD), BF16)
    pair32 = pltpu.VMEM((2 * ATTN_BLOCK, HD), F32)
    return pl.pallas_call(
        body, name=name, grid=(total + 1,),
        in_specs=[blk(0), blk(1), blk(2), pl.BlockSpec((ATTN_BLOCK, HD), lambda t: at(t)), nar, nar, gains, gains],
        out_specs=[pl.BlockSpec((ATTN_BLOCK, 3 * HD), lambda t: at(jnp.maximum(t - 1, 0))),
                   pl.BlockSpec((8, HEAD_DIM), lambda t: (0, 0))],
        out_shape=[jax.ShapeDtypeStruct((L, d * 3 * HD), BF16), jax.ShapeDtypeStruct((8, HEAD_DIM), F32)],
        scratch_shapes=[acc, acc, acc, acc, acc, acc, pair16, pair16, pair32, pair32],
        compiler_params=_params("arbitrary"),
    )(qkv, qkv, qkv, dout, lse, delta, qg, kg)


def _adamw(w, grad, m, v, name):
    if w.ndim == 2:
        R, C = w.shape
        tr = _row_tile(R, 512)
        blk, steps = pl.BlockSpec((tr, C), lambda i: (i, 0)), R // tr
    else:
        zeros = (0,) * w.ndim
        blk, steps = pl.BlockSpec(w.shape, lambda i: zeros), 1

    def body(w_ref, g_ref, m_ref, v_ref, d_ref, nm_ref, nv_ref):
        gv = g_ref[...]
        nm = ADAM_B1 * m_ref[...] + (1.0 - ADAM_B1) * gv
        nv = ADAM_B2 * v_ref[...] + (1.0 - ADAM_B2) * (gv * gv)
        m_hat = nm / (1.0 - ADAM_B1 ** ADAM_STEP)
        v_hat = nv / (1.0 - ADAM_B2 ** ADAM_STEP)
        d_ref[...] = -ADAM_LR * (m_hat / (jnp.sqrt(v_hat) + ADAM_EPS) + ADAM_WD * w_ref[...])
        nm_ref[...] = nm
        nv_ref[...] = nv

    shp = jax.ShapeDtypeStruct(w.shape, F32)
    return pl.pallas_call(
        body, name=name, grid=(steps,), in_specs=[blk] * 4, out_specs=[blk] * 3, out_shape=[shp] * 3,
        compiler_params=_params("parallel"),
    )(w, grad, m, v)


def _ffn_fwd(x, h, comm, s, tag, tgt=None, next_g=None):
    (win_t, wout), tok = comm.weights(s, h)
    z3, a = _gated_in(h, win_t, None, "swiglu", BF16, f"ffn_in_{tag}", after=tok)
    saved = (x, h, z3, a, win_t, wout)
    if tgt is None:
        return _rows_mm(a, wout, "res", f"ffn_out_{tag}", x=x, scale=0.5, g=next_g), saved
    return _rows_mm(a, wout, "loss", f"ffn_out_loss_{tag}", x=x, scale=0.5, tgt=tgt), saved


def _ffn_bwd(dxo, saved, g, comm, s, tag, after):
    x, h, z3, a, win_t, wout = saved
    dz3 = _swiglu_bwd(dxo, wout, z3, 0.5, f"ffn_out_bwd_{tag}", after=after)
    d_wout = _mm_tn(a, dxo, 0.5, f"ffn_dwout_{tag}")
    d_win_t = _mm_tn(dz3, h, 1.0, f"ffn_dwin_{tag}")
    tok = comm.grads(s, [d_win_t, d_wout])
    dx, dg8 = _rows_mm(dz3, win_t, "rmsbwd", f"ffn_in_bwd_{tag}", x=x, g=g, dxo=dxo, after=tok)
    return dx, dg8, ()


def _local_step(x, tgt, vecs, comm):
    norm_g = vecs["norm_g"]

    def gvec(i):
        return norm_g[i:i + 1]

    h0 = _rmsnorm(x, gvec(0), "rmsnorm_l0a", after=comm.started)
    (x1, h_c), sv_a0 = _ffn_fwd(x, h0, comm, 0, "l0a", next_g=gvec(1))
    (pw1_t, pw2), tok = comm.weights(1, h_c)
    zc3, u = _gated_in(h_c, pw1_t, vecs["b_pw1"], "glu", F32, "conv_pw1", after=tok)
    v, s = _dwconv_fwd(u, vecs["w_dw"], vecs["b_dw"], vecs["cg"], "conv_dw")
    x2, h_b0 = _rows_mm(s, pw2, "res", "conv_pw2", x=x1, bias=vecs["b_pw2"], g=gvec(2))
    (x3, h_a1), sv_b0 = _ffn_fwd(x2, h_b0, comm, 2, "l0b", next_g=gvec(3))
    x4, sv_a1 = _ffn_fwd(x3, h_a1, comm, 3, "l1a")
    h_s = _rmsnorm_streams(x4, gvec(4), "rmsnorm_attn")
    (qkv_t, wo), tok = comm.weights(4, h_s[0])
    hd3 = qkv_t.shape[0] // 3
    qkvs, os, lses = [], [], []
    for g, d in enumerate(ATTN_DILATIONS):
        qkv = _mm_nt(h_s[g], qkv_t, BF16, f"attn_qkv_g{g}", w_row0=g * hd3, n_rows=hd3, streams=d, after=tok)
        o, lse = _attn_fwd(qkv, vecs["q_norm"], vecs["k_norm"], g, f"attn_fwd_g{g}")
        qkvs.append(qkv)
        os.append(o)
        lses.append(lse)
    out_b, out_f, lse_tot = _attn_merge(os, lses, "attn_merge")
    x5, h_b1 = _rows_mm(out_b, wo, "res", "attn_wo", x=x4, g=gvec(5))
    (dy, loss8), sv_b1 = _ffn_fwd(x5, h_b1, comm, 5, "l1b", tgt=tgt)
    loss = 0.5 * jnp.sum(loss8) / x.shape[1]

    dg = [None] * 6
    dx, dg[5], tok = _ffn_bwd(dy, sv_b1, gvec(5), comm, 5, "l1b", ())
    dout = _mm_nt(dx, wo, F32, "attn_wo_bwd", after=tok)
    d_wo = _mm_tn(out_b, dx, 1.0, "attn_dwo")
    delta, *wide = _attn_delta(out_f, dout, lse_tot, "attn_delta")
    per_group = [(dout, lse_tot, delta), tuple(wide[0:3]), tuple(wide[3:6])]
    ggs, d_qkv_t, dhs = [], [], []
    for g, d in enumerate(ATTN_DILATIONS):
        dqkv, gg = _attn_bwd(qkvs[g], *per_group[g], vecs["q_norm"], vecs["k_norm"], g, f"attn_bwd_g{g}")
        ggs.append(gg)
        d_qkv_t.append(_mm_tn(dqkv, h_s[g], 1.0, f"attn_dwqkv_g{g}", streams=d))
        if d > 1:
            dhs.append((d, _rows_mm(dqkv, qkv_t, "plain", f"attn_qkv_bwd_g{g}", b_row0=g * hd3, streams=d)))
        else:
            dqkv0 = dqkv
    tok = comm.grads(4, [jnp.concatenate(d_qkv_t), d_wo])
    dx, dg[4] = _rows_mm(dqkv0, qkv_t, "rmsbwd", "attn_qkv_bwd_g0", x=x4, g=gvec(4), dxo=dx, extras=dhs, after=tok)
    dx, dg[3], tok = _ffn_bwd(dx, sv_a1, gvec(3), comm, 3, "l1a", ())
    dx, dg[2], tok = _ffn_bwd(dx, sv_b0, gvec(2), comm, 2, "l0b", tok)
    ds = _mm_nt(dx, pw2, F32, "conv_pw2_bwd", after=tok)
    d_pw2 = _mm_tn(s, dx, 1.0, "conv_dwpw2")
    dv, dcg8, dbdw8, dbpw2_8 = _conv_post_bwd(ds, v, vecs["cg"], dx, "conv_post_bwd")
    dzc3, d_wdw, db1_16 = _dwconv_bwd(dv, u, vecs["w_dw"], zc3, "conv_dw_bwd")
    d_pw1_t = _mm_tn(dzc3, h_c, 1.0, "conv_dwpw1")
    tok = comm.grads(1, [d_pw1_t, d_pw2])
    dx, dg[1] = _rows_mm(dzc3, pw1_t, "rmsbwd", "conv_pw1_bwd", x=x1, g=gvec(1), dxo=dx, after=tok)
    dx, dg[0], tok = _ffn_bwd(dx, sv_a0, gvec(0), comm, 0, "l0a", ())

    D = x.shape[1]
    small = {
        "norm_g": jnp.stack([p.sum(axis=0) for p in dg]),
        "b_pw1": db1_16.reshape(2, 8, D).sum(axis=1),
        "w_dw": d_wdw[:CONV_WIDTH],
        "b_dw": dbdw8.sum(axis=0, keepdims=True),
        "cg": dcg8.sum(axis=0, keepdims=True),
        "b_pw2": dbpw2_8.sum(axis=0, keepdims=True),
        "q_norm": jnp.stack([gg[0] for gg in ggs]),
        "k_norm": jnp.stack([gg[1] for gg in ggs]),
    }
    return loss, dx, small, tok


def _mesh_pos():
    return lax.axis_index("x"), lax.axis_index("y"), lax.axis_index("c")


def _other_chips(x, y):
    return [(1 - x, y), (x, 1 - y), (1 - x, 1 - y)]


HBM = pl.BlockSpec(memory_space=pltpu.HBM)
SEM = pl.BlockSpec(memory_space=pltpu.SEMAPHORE)
SPLIT_COPY = pltpu.SideEffectType.DATAFLOW_SIDE_EFFECTING


def _hbm(a):
    return pltpu.with_memory_space_constraint(a, pltpu.HBM)


def _gather_copies(j, src, out, send, recv, x, y, c):
    me = 4 * x + 2 * y + c
    peers = [(x, y, 1 - c)] + [(px, py, c) for px, py in _other_chips(x, y)]
    return [pltpu.make_async_remote_copy(src_ref=src, dst_ref=out.at[me], send_sem=send.at[4 * j + k],
                                         recv_sem=recv.at[4 * j + k], device_id=peer, device_id_type=MESH)
            for k, peer in enumerate(peers)]


def _gather_start(shards, me, after, name):
    n = len(shards)
    lands = [lax.dynamic_update_slice(lax.empty((N_DEV,) + s.shape, s.dtype), s[None], (me, 0, 0)) for s in shards]

    def body(*refs):
        ins, outs = refs[:n], refs[2 * n + len(after):3 * n + len(after)]
        send, recv, token = refs[4 * n + len(after):]
        x, y, c = _mesh_pos()
        for j in range(n):
            for cp in _gather_copies(j, ins[j], outs[j], send, recv, x, y, c):
                cp.start()
        token[...] = jnp.zeros_like(token)

    res = pl.pallas_call(
        body, name=name, in_specs=[HBM] * (2 * n) + [ANY] * len(after),
        out_specs=[HBM] * (2 * n) + [SEM, SEM, pl.BlockSpec(memory_space=pltpu.VMEM)],
        out_shape=[pltpu.HBM((N_DEV,) + s.shape, s.dtype) for s in shards] + [pltpu.HBM(s.shape, s.dtype) for s in shards]
        + [pltpu.SemaphoreType.DMA((4 * n,)), pltpu.SemaphoreType.DMA((4 * n,)), jax.ShapeDtypeStruct((8, 128), F32)],
        input_output_aliases={**{i: n + i for i in range(n)}, **{n + i: i for i in range(n)}},
        compiler_params=pltpu.CompilerParams(has_side_effects=SPLIT_COPY),
    )(*[_hbm(s) for s in shards], *[_hbm(l) for l in lands], *after)
    return res[:n], res[n:2 * n], res[2 * n], res[2 * n + 1], res[2 * n + 2]


def _gather_wait(outs, thru, send, recv, after, name):
    n = len(outs)

    def body(*refs):
        out_refs, thru_refs, send_ref, recv_ref = refs[:n], refs[n:2 * n], refs[2 * n], refs[2 * n + 1]
        x, y, c = _mesh_pos()
        for j in range(n):
            for cp in _gather_copies(j, thru_refs[j], out_refs[j], send_ref, recv_ref, x, y, c):
                cp.wait_send()
                cp.wait_recv()

    res = pl.pallas_call(
        body, name=name, in_specs=[HBM] * (2 * n) + [SEM, SEM] + [ANY] * len(after), out_specs=[HBM] * (2 * n),
        out_shape=[pltpu.HBM(a.shape, a.dtype) for a in list(outs) + list(thru)],
        input_output_aliases={i: i for i in range(2 * n)},
        compiler_params=pltpu.CompilerParams(has_side_effects=SPLIT_COPY),
    )(*outs, *thru, send, recv, *after)
    return res[:n]


def _gather_forward(outs, name):
    n = len(outs)

    def body(*refs):
        o = refs[n:2 * n]
        send_sems, recv_sems = refs[2 * n:]
        x, y, c = _mesh_pos()
        copies = []
        for j in range(n):
            for k, (px, py) in enumerate(_other_chips(x, y)):
                blk = o[j].at[4 * px + 2 * py + c]
                cp = pltpu.make_async_remote_copy(src_ref=blk, dst_ref=blk, send_sem=send_sems.at[j, k],
                                                  recv_sem=recv_sems.at[j, k], device_id=(x, y, 1 - c), device_id_type=MESH)
                cp.start()
                copies.append(cp)
        for cp in copies:
            cp.wait()

    return pl.pallas_call(
        body, name=name, in_specs=[ANY] * n, out_specs=[ANY] * n,
        out_shape=[jax.ShapeDtypeStruct(a.shape, a.dtype) for a in outs],
        input_output_aliases={i: i for i in range(n)},
        scratch_shapes=[pltpu.SemaphoreType.DMA((n, 3)), pltpu.SemaphoreType.DMA((n, 3))],
    )(*outs)


def _all_sum(p, name):
    R, C = p.shape

    def body(p_ref, o_ref, buf, send_sems, recv_sems):
        x, y, c = _mesh_pos()
        me = 4 * x + 2 * y + c
        buf[me] = p_ref[...]
        copies = []
        for rel in range(1, N_DEV):
            peer = (1 - x if rel & 4 else x, 1 - y if rel & 2 else y, 1 - c if rel & 1 else c)
            cp = pltpu.make_async_remote_copy(
                src_ref=p_ref, dst_ref=buf.at[me], send_sem=send_sems.at[rel - 1], recv_sem=recv_sems.at[rel - 1],
                device_id=peer, device_id_type=MESH)
            cp.start()
            copies.append(cp)
        for cp in copies:
            cp.wait()
        acc = buf[0]
        for dev in range(1, N_DEV):
            acc = acc + buf[dev]
        o_ref[...] = acc

    vm = pl.BlockSpec(memory_space=pltpu.VMEM)
    return pl.pallas_call(
        body, name=name, in_specs=[vm], out_specs=vm, out_shape=jax.ShapeDtypeStruct((R, C), F32),
        scratch_shapes=[pltpu.VMEM((N_DEV, R, C), F32), pltpu.SemaphoreType.DMA((N_DEV - 1,)),
                        pltpu.SemaphoreType.DMA((N_DEV - 1,))],
    )(p)


def _swap_with_sibling(gs, name):
    n = len(gs)

    def body(*refs):
        ins, outs = refs[:n], refs[n:2 * n]
        send_sems, recv_sems = refs[2 * n:]
        x, y, c = _mesh_pos()
        copies = []
        for i in range(n):
            for k in range(4):
                cp = pltpu.make_async_remote_copy(
                    src_ref=ins[i].at[2 * k + (1 - c)], dst_ref=outs[i].at[k],
                    send_sem=send_sems.at[i, k], recv_sem=recv_sems.at[i, k],
                    device_id=(x, y, 1 - c), device_id_type=MESH)
                cp.start()
                copies.append(cp)
        for cp in copies:
            cp.wait()

    return pl.pallas_call(
        body, name=name, in_specs=[ANY] * n, out_specs=[ANY] * n,
        out_shape=[jax.ShapeDtypeStruct((4,) + g.shape[1:], g.dtype) for g in gs],
        scratch_shapes=[pltpu.SemaphoreType.DMA((n, 4)), pltpu.SemaphoreType.DMA((n, 4))],
    )(*gs)


def _scatter_copies(j, src, land, send, recv, x, y, c):
    return [pltpu.make_async_remote_copy(src_ref=src.at[2 * px + py], dst_ref=land.at[k], send_sem=send.at[3 * j + k],
                                         recv_sem=recv.at[3 * j + k], device_id=(px, py, c), device_id_type=MESH)
            for k, (px, py) in enumerate(_other_chips(x, y))]


def _scatter_start(ps, name):
    n = len(ps)

    def body(*refs):
        ins, lands = refs[:n], refs[2 * n:3 * n]
        send, recv, token = refs[3 * n:]
        x, y, c = _mesh_pos()
        for j in range(n):
            for cp in _scatter_copies(j, ins[j], lands[j], send, recv, x, y, c):
                cp.start()
        token[...] = jnp.zeros_like(token)

    res = pl.pallas_call(
        body, name=name, in_specs=[HBM] * n,
        out_specs=[HBM] * (2 * n) + [SEM, SEM, pl.BlockSpec(memory_space=pltpu.VMEM)],
        out_shape=[pltpu.HBM(p.shape, p.dtype) for p in ps] + [pltpu.HBM((3,) + p.shape[1:], p.dtype) for p in ps]
        + [pltpu.SemaphoreType.DMA((3 * n,)), pltpu.SemaphoreType.DMA((3 * n,)), jax.ShapeDtypeStruct((8, 128), F32)],
        input_output_aliases={i: i for i in range(n)},
        compiler_params=pltpu.CompilerParams(has_side_effects=SPLIT_COPY),
    )(*[_hbm(p) for p in ps])
    return res[:n], res[n:2 * n], res[2 * n], res[2 * n + 1], res[2 * n + 2]


def _scatter_wait(thru, lands, send, recv, after, name):
    n = len(thru)

    def body(*refs):
        thru_refs, land_refs, send_ref, recv_ref = refs[:n], refs[n:2 * n], refs[2 * n], refs[2 * n + 1]
        x, y, c = _mesh_pos()
        for j in range(n):
            for cp in _scatter_copies(j, thru_refs[j], land_refs[j], send_ref, recv_ref, x, y, c):
                cp.wait_send()
                cp.wait_recv()

    res = pl.pallas_call(
        body, name=name, in_specs=[HBM] * (2 * n) + [SEM, SEM, ANY], out_specs=[HBM] * (2 * n),
        out_shape=[pltpu.HBM(a.shape, a.dtype) for a in list(thru) + list(lands)],
        input_output_aliases={i: i for i in range(2 * n)},
        compiler_params=pltpu.CompilerParams(has_side_effects=SPLIT_COPY),
    )(*thru, *lands, send, recv, after)
    return res[:n], res[n:]


def _row_tile(r, pref):
    if r <= pref:
        return r
    for t in range(pref - pref % 16, 0, -16):
        if r % t == 0:
            return t
    return r


def _add_sibling(g, r1, pos, name):
    _, r, D = g.shape
    tr = _row_tile(r, 512)

    def body(pos_ref, g_ref, r_ref, o_ref):
        o_ref[...] = (g_ref[...].astype(F32) + r_ref[...].astype(F32)).astype(BF16)

    return pl.pallas_call(
        body, name=name,
        grid_spec=pltpu.PrefetchScalarGridSpec(
            num_scalar_prefetch=1, grid=(4, r // tr),
            in_specs=[pl.BlockSpec((None, None, tr, D), lambda k, j, pos: (k, pos[0], j, 0)),
                      pl.BlockSpec((None, tr, D), lambda k, j, pos: (k, j, 0))],
            out_specs=pl.BlockSpec((None, tr, D), lambda k, j, pos: (k, j, 0))),
        out_shape=jax.ShapeDtypeStruct((4, r, D), BF16),
        compiler_params=_params("parallel", "parallel"),
    )(pos, g.reshape(4, 2, r, D), r1)


def _add_chips(p, r2, pos, name):
    _, r, D = p.shape
    tr = _row_tile(r, 512)

    def body(pos_ref, p_ref, r_ref, o_ref):
        acc = p_ref[...].astype(F32)
        for j in range(3):
            acc = acc + r_ref[j].astype(F32)
        o_ref[...] = acc

    return pl.pallas_call(
        body, name=name,
        grid_spec=pltpu.PrefetchScalarGridSpec(
            num_scalar_prefetch=1, grid=(r // tr,),
            in_specs=[pl.BlockSpec((None, tr, D), lambda j, pos: (pos[1], j, 0)),
                      pl.BlockSpec((3, tr, D), lambda j, pos: (0, j, 0))],
            out_specs=pl.BlockSpec((tr, D), lambda j, pos: (j, 0))),
        out_shape=jax.ShapeDtypeStruct((r, D), F32),
        compiler_params=_params("parallel"),
    )(pos, p, r2)


def _place_cols(local, me, width):
    R, w = local.shape
    return lax.dynamic_update_slice(jnp.zeros((R, width), F32), local, (0, me * w))


def _pad_rows(a, rows):
    return jnp.pad(a, ((0, rows - a.shape[0]), (0, 0)))


SUBLAYER_SHARDS = ((0, 4), (8, 9), (1, 5), (2, 6), (10, 11), (3, 7))


class _StepComm:
    def __init__(self, shards, pos):
        self.pos = pos
        self.shards = shards
        self.gathers = {}
        self.started = self._start_gather(1, self._start_gather(0, ()))
        self.pending = {}

    def _start_gather(self, s, after):
        outs, thru, send, recv, token = _gather_start([self.shards[i] for i in SUBLAYER_SHARDS[s]],
                                                      2 * self.pos[1] + self.pos[0], after, f"gather_start_{s}")
        self.gathers[s] = (outs, thru, send, recv)
        return (token,)

    def weights(self, s, after):
        outs = _gather_wait(*self.gathers.pop(s), (after,), f"gather_wait_{s}")
        outs = _gather_forward(outs, f"gather_forward_{s}")
        tok = self._start_gather(s + 2, (outs[0],)) if s + 2 < len(SUBLAYER_SHARDS) else ()
        return [w.reshape(-1, w.shape[-1]) for w in outs], tok

    def grads(self, s, mats):
        g3 = [g.reshape(N_DEV, g.shape[0] // N_DEV, g.shape[1]) for g in mats]
        r1 = _swap_with_sibling(g3, f"rs_sibling_{s}")
        ps = [_add_sibling(g, r, self.pos, f"rs_add_sibling_{s}_{i}") for i, (g, r) in enumerate(zip(g3, r1))]
        thru, lands, send, recv, token = _scatter_start(ps, f"rs_start_{s}")
        self.pending[s] = (thru, lands, send, recv)
        return (token,)

    def finish(self, after):
        out = {}
        for s in sorted(self.pending, reverse=True):
            ps, lands = _scatter_wait(*self.pending[s], after, f"rs_wait_{s}")
            for i, p, r2 in zip(SUBLAYER_SHARDS[s], ps, lands):
                out[i] = _add_chips(p, r2, self.pos, f"rs_add_chips_{i}")
        return out


def kernel(x, norm_g, ffn_w_in, ffn_w_out, conv_w_pw1, conv_b_pw1, conv_w_dw, conv_b_dw, conv_norm_g, conv_w_pw2, conv_b_pw2, attn_w_qkv, attn_q_norm, attn_k_norm, attn_w_o, loss_target, m_norm_g, m_ffn_w_in, m_ffn_w_out, m_conv_w_pw1, m_conv_b_pw1, m_conv_w_dw, m_conv_b_dw, m_conv_norm_g, m_conv_w_pw2, m_conv_b_pw2, m_attn_w_qkv, m_attn_q_norm, m_attn_k_norm, m_attn_w_o, v_norm_g, v_ffn_w_in, v_ffn_w_out, v_conv_w_pw1, v_conv_b_pw1, v_conv_w_dw, v_conv_b_dw, v_conv_norm_g, v_conv_w_pw2, v_conv_b_pw2, v_attn_w_qkv, v_attn_q_norm, v_attn_k_norm, v_attn_w_o):
    T, D = x.shape[1], x.shape[2]
    xi, yi, ci = _mesh_pos()
    me = 4 * xi + 2 * yi + ci
    pos = jnp.stack([ci, 2 * xi + yi]).astype(I32)
    lf = [(l, f) for l in range(2) for f in range(2)]

    shards = ([ffn_w_in[l, f].T.astype(BF16) for l, f in lf] + [ffn_w_out[l, f].astype(BF16) for l, f in lf]
              + [conv_w_pw1[0].T.astype(BF16), conv_w_pw2[0].astype(BF16),
                 attn_w_qkv[0].T.astype(BF16), attn_w_o[0].astype(BF16)])
    comm = _StepComm(shards, pos)
    w_cols = norm_g.shape[2]
    vec = _all_sum(_pad_rows(jnp.concatenate([_place_cols(norm_g.reshape(6, w_cols), me, D),
                                              _place_cols(conv_w_dw[0], me, D)]), 40), "gather_vectors")
    vecs = {
        "norm_g": vec[0:6], "w_dw": vec[6:6 + CONV_WIDTH],
        "b_pw1": conv_b_pw1.reshape(2, D), "b_dw": conv_b_dw, "cg": conv_norm_g, "b_pw2": conv_b_pw2,
        "q_norm": attn_q_norm[0], "k_norm": attn_k_norm[0],
    }

    loss_local, dx, small, tok = _local_step(x[0], loss_target[0], vecs, comm)
    loss = lax.psum(loss_local, ("x", "y", "c"))

    mats = comm.finish(dx)
    hd3 = 3 * HEAD_DIM
    packed = jnp.concatenate([
        small["norm_g"], small["b_pw1"], small["w_dw"], small["b_dw"], small["cg"], small["b_pw2"],
        jnp.pad(small["q_norm"].reshape(1, hd3), ((0, 0), (0, D - hd3))),
        jnp.pad(small["k_norm"].reshape(1, hd3), ((0, 0), (0, D - hd3)))])
    red = _all_sum(_pad_rows(packed, 48), "reduce_vectors")
    col0 = me * w_cols
    grads = {
        "norm_g": lax.dynamic_slice(red[0:6], (0, col0), (6, w_cols)),
        "ffn_w_in": jnp.concatenate([mats[i].T for i in range(4)]),
        "ffn_w_out": jnp.concatenate([mats[i] for i in range(4, 8)]),
        "conv_w_pw1": mats[8].T,
        "conv_b_pw1": red[6:8].reshape(1, 2 * D),
        "conv_w_dw": lax.dynamic_slice(red[8:8 + CONV_WIDTH], (0, col0), (CONV_WIDTH, w_cols)),
        "conv_b_dw": red[39:40], "conv_norm_g": red[40:41], "conv_w_pw2": mats[9], "conv_b_pw2": red[41:42],
        "attn_w_qkv": mats[10].T,
        "attn_q_norm": red[42, :hd3].reshape(3, HEAD_DIM), "attn_k_norm": red[43, :hd3].reshape(3, HEAD_DIM),
        "attn_w_o": mats[11],
    }

    names = ["norm_g", "ffn_w_in", "ffn_w_out", "conv_w_pw1", "conv_b_pw1", "conv_w_dw", "conv_b_dw",
             "conv_norm_g", "conv_w_pw2", "conv_b_pw2", "attn_w_qkv", "attn_q_norm", "attn_k_norm", "attn_w_o"]
    ws = [norm_g, ffn_w_in, ffn_w_out, conv_w_pw1, conv_b_pw1, conv_w_dw, conv_b_dw, conv_norm_g, conv_w_pw2,
          conv_b_pw2, attn_w_qkv, attn_q_norm, attn_k_norm, attn_w_o]
    ms = [m_norm_g, m_ffn_w_in, m_ffn_w_out, m_conv_w_pw1, m_conv_b_pw1, m_conv_w_dw, m_conv_b_dw, m_conv_norm_g,
          m_conv_w_pw2, m_conv_b_pw2, m_attn_w_qkv, m_attn_q_norm, m_attn_k_norm, m_attn_w_o]
    vs = [v_norm_g, v_ffn_w_in, v_ffn_w_out, v_conv_w_pw1, v_conv_b_pw1, v_conv_w_dw, v_conv_b_dw, v_conv_norm_g,
          v_conv_w_pw2, v_conv_b_pw2, v_attn_w_qkv, v_attn_q_norm, v_attn_k_norm, v_attn_w_o]
    g_out, d_out, m_out, v_out = [], [], [], []
    for name, w, m, v in zip(names, ws, ms, vs):
        shape2 = (-1, w.shape[-1]) if w.size > SMALL_PARAM else w.shape
        g2 = grads[name].reshape(shape2)
        delta, new_m, new_v = _adamw(w.reshape(shape2), g2, m.reshape(shape2), v.reshape(shape2), f"adamw_{name}")
        g_out.append(g2.reshape(w.shape))
        d_out.append(delta.reshape(w.shape))
        m_out.append(new_m.reshape(w.shape))
        v_out.append(new_v.reshape(w.shape))
    return (loss, dx.reshape(x.shape), *g_out, *d_out, *m_out, *v_out)
```

```python
import functools

import jax
import jax.numpy as jnp
from jax import lax
from jax.experimental import pallas as pl
from jax.experimental.pallas import tpu as pltpu

F32 = jnp.float32
BF16 = jnp.bfloat16
I32 = jnp.int32

NORM_EPS = 1e-6
LANES = 128
SUBLANES = 8
MXU_WIDTH = 256
HEAD_DIM = 128
ATTN_BLOCK = 128
ATTN_DILATIONS = (1, 4, 16)
CONV_WIDTH = 31
CONV_HALO = 32
CONV_CHUNK = 16
MM_TN_TOKENS = 2048
STREAM_TILE = 512
ADAM_LR, ADAM_B1, ADAM_B2, ADAM_EPS, ADAM_WD, ADAM_STEP = 0.001, 0.9, 0.999, 1e-08, 0.01, 10
N_DEV = 8
SMALL_PARAM = 8192
NEG_BIG = -1e30
V7X_VMEM_BYTES = 64 * 1024 * 1024
VMEM_LIMIT = V7X_VMEM_BYTES - 8 * 1024 * 1024
MESH = pl.DeviceIdType.MESH
ANY = pl.BlockSpec(memory_space=pl.ANY)


def _params(*sem):
    return pltpu.CompilerParams(dimension_semantics=sem or None, vmem_limit_bytes=VMEM_LIMIT)


def _dot_nn(a, b):
    return lax.dot_general(a, b, (((1,), (0,)), ((), ())), preferred_element_type=F32)


def _dot_nt(a, b):
    return lax.dot_general(a, b, (((1,), (1,)), ((), ())), preferred_element_type=F32)


def _dot_tn(a, b):
    return lax.dot_general(a, b, (((0,), (0,)), ((), ())), preferred_element_type=F32)


def _sigmoid(x):
    return 1.0 / (1.0 + jnp.exp(-x))


def _tile(n, pref):
    if n <= pref:
        return n
    for t in range(pref - pref % 128, 0, -128):
        if n % t == 0:
            return t
    return n


def _to_lane_chunks(dst, val):
    for c in range(dst.shape[0]):
        dst[c] = val[:, c * LANES:(c + 1) * LANES]


def _rows8(v):
    tm, c = v.shape
    return v.reshape(tm // 8, 8, c).sum(axis=0)


def _rmsnorm(x, g, name, after=()):
    T, D = x.shape
    tm = min(T, 512)

    def body(x_ref, g_ref, *rest):
        h_ref = rest[-1]
        xv = x_ref[...]
        r = lax.rsqrt(jnp.mean(xv * xv, axis=-1, keepdims=True) + NORM_EPS)
        h_ref[...] = (xv * r * g_ref[...]).astype(BF16)

    return pl.pallas_call(
        body, name=name, grid=(T // tm,),
        in_specs=[pl.BlockSpec((tm, D), lambda i: (i, 0)), pl.BlockSpec((1, D), lambda i: (0, 0))] + [ANY] * len(after),
        out_specs=pl.BlockSpec((tm, D), lambda i: (i, 0)),
        out_shape=jax.ShapeDtypeStruct((T, D), BF16),
        compiler_params=_params("parallel"),
    )(x, g, *after)


def _rmsnorm_streams(x, g, name):
    T, D = x.shape
    tm = min(T, STREAM_TILE)
    dils = [d for d in ATTN_DILATIONS if d > 1]

    def body(x_ref, g_ref, h_ref, *rest):
        outs, hs = rest[:-1], rest[-1]
        xv = x_ref[...]
        r = lax.rsqrt(jnp.mean(xv * xv, axis=-1, keepdims=True) + NORM_EPS)
        hf = xv * r * g_ref[...]
        h_ref[...] = hf.astype(BF16)
        _to_lane_chunks(hs, hf)
        for d, o_ref in zip(dils, outs):
            for s in range(d):
                for c in range(D // LANES):
                    o_ref[s, :, c * LANES:(c + 1) * LANES] = hs.at[c][pl.ds(s, tm // d, stride=d), :].astype(BF16)

    res = pl.pallas_call(
        body, name=name, grid=(T // tm,),
        in_specs=[pl.BlockSpec((tm, D), lambda i: (i, 0)), pl.BlockSpec((1, D), lambda i: (0, 0))],
        out_specs=[pl.BlockSpec((tm, D), lambda i: (i, 0))]
        + [pl.BlockSpec((d, tm // d, D), lambda i: (0, i, 0)) for d in dils],
        out_shape=[jax.ShapeDtypeStruct((T, D), BF16)] + [jax.ShapeDtypeStruct((d, T // d, D), BF16) for d in dils],
        scratch_shapes=[pltpu.VMEM((D // LANES, tm, LANES), F32)],
        compiler_params=_params("parallel"),
    )(x, g)
    return [res[0]] + [a.reshape(T, D) for a in res[1:]]


def _gated_in(h, wt, bias, act, u_dtype, name, after=()):
    T, D = h.shape
    F = wt.shape[0] // 2
    tn = _tile(F, 1408)
    tm = min(T, 1024)
    nn = F // tn

    def body(*refs):
        z_ref, u_ref = refs[-2:]
        if bias is None:
            h_ref, w0_ref, w1_ref = refs[:3]
        else:
            h_ref, w0_ref, w1_ref, b_ref = refs[:4]
        hv = h_ref[...]
        for c0 in range(0, tn, MXU_WIDTH):
            cs = slice(c0, min(c0 + MXU_WIDTH, tn))
            z0 = _dot_nt(hv, w0_ref[cs, :])
            z1 = _dot_nt(hv, w1_ref[cs, :])
            if bias is not None:
                z0 = z0 + b_ref[0:1, cs]
                z1 = z1 + b_ref[1:2, cs]
            z_ref[0, :, cs] = z0.astype(BF16)
            z_ref[1, :, cs] = z1.astype(BF16)
            if act == "swiglu":
                u = z0 * _sigmoid(z0) * z1
            else:
                u = z0 * _sigmoid(z1)
            u_ref[:, cs] = u.astype(u_dtype)

    in_specs = [pl.BlockSpec((tm, D), lambda n, m: (m, 0)),
                pl.BlockSpec((tn, D), lambda n, m: (n, 0)),
                pl.BlockSpec((tn, D), lambda n, m: (n + nn, 0))]
    args = [h, wt, wt]
    if bias is not None:
        in_specs.append(pl.BlockSpec((2, tn), lambda n, m: (0, n)))
        args.append(bias)
    in_specs += [ANY] * len(after)
    args += list(after)
    return pl.pallas_call(
        body, name=name, grid=(nn, T // tm), in_specs=in_specs,
        out_specs=[pl.BlockSpec((2, tm, tn), lambda n, m: (0, m, n)), pl.BlockSpec((tm, tn), lambda n, m: (m, n))],
        out_shape=[jax.ShapeDtypeStruct((2, T, F), BF16), jax.ShapeDtypeStruct((T, F), u_dtype)],
        compiler_params=_params("parallel", "parallel"),
    )(*args)


def _swiglu_bwd(dxo, wout, z3, scale, name, after=()):
    T, D = dxo.shape
    F = wout.shape[0]
    tn = _tile(F, 1408)
    tm = min(T, 512)

    def body(d_ref, w_ref, z_ref, *rest):
        dz_ref = rest[-1]
        dy = (d_ref[...] * scale).astype(BF16)
        da = _dot_nt(dy, w_ref[...])
        gate = z_ref[0].astype(F32)
        up = z_ref[1].astype(F32)
        sg = _sigmoid(gate)
        dz_ref[0] = (da * up * (sg * (1.0 + gate * (1.0 - sg)))).astype(BF16)
        dz_ref[1] = (da * gate * sg).astype(BF16)

    return pl.pallas_call(
        body, name=name, grid=(F // tn, T // tm),
        in_specs=[pl.BlockSpec((tm, D), lambda n, m: (m, 0)),
                  pl.BlockSpec((tn, D), lambda n, m: (n, 0)),
                  pl.BlockSpec((2, tm, tn), lambda n, m: (0, m, n))] + [ANY] * len(after),
        out_specs=pl.BlockSpec((2, tm, tn), lambda n, m: (0, m, n)),
        out_shape=jax.ShapeDtypeStruct((2, T, F), BF16),
        compiler_params=_params("parallel", "parallel"),
    )(dxo, wout, z3, *after)


def _mm_nt(a, wt, out_dtype, name, *, w_row0=0, n_rows=None, streams=1, after=()):
    T, K = a.shape
    N = wt.shape[0] if n_rows is None else n_rows
    L = T // streams
    rows = min(L, STREAM_TILE)
    sp = STREAM_TILE // rows
    mps = L // rows
    assert streams % sp == 0 and w_row0 % N == 0

    def body(a_ref, w_ref, *rest):
        o_ref = rest[-1]
        for s in range(sp):
            a_s = a_ref[s * rows:(s + 1) * rows, :].astype(BF16)
            for c0 in range(0, N, MXU_WIDTH):
                c1 = min(c0 + MXU_WIDTH, N)
                o_ref[:, s * N + c0:s * N + c1] = _dot_nt(a_s, w_ref[c0:c1, :]).astype(out_dtype)

    return pl.pallas_call(
        body, name=name, grid=(T // (rows * sp),),
        in_specs=[pl.BlockSpec((rows * sp, K), lambda m: (m, 0)), pl.BlockSpec((N, K), lambda m: (w_row0 // N, 0))]
        + [ANY] * len(after),
        out_specs=pl.BlockSpec((rows, sp * N), lambda m: (m % mps, m // mps)),
        out_shape=jax.ShapeDtypeStruct((L, streams * N), out_dtype),
        compiler_params=_params("parallel"),
    )(a, wt, *after)


def _mm_tn(a, b, b_scale, name, *, streams=1):
    if a.ndim == 3:
        J, T, F = a.shape
    elif streams > 1:
        J, T, F = 1, a.shape[0] * streams, a.shape[1] // streams
    else:
        (T, F), J = a.shape, 1
    N = b.shape[1]
    tmm = _tile(F, 1408)
    tk = min(T // streams, MM_TN_TOKENS if b.dtype == BF16 else MM_TN_TOKENS // 2)
    npj = F // tmm
    sp = max(1, MM_TN_TOKENS // tk) if streams > 1 and tk == T // streams else 1
    assert streams % sp == 0
    nk = T // (tk * sp)
    kps = nk // streams if sp == 1 else 1

    def body(*refs):
        a_refs, (b_ref, o_ref, acc_ref) = refs[:sp], refs[sp:]
        k = pl.program_id(1)
        part = None
        for s, a_ref in enumerate(a_refs):
            bv = b_ref[s * tk:(s + 1) * tk, :]
            if b_scale != 1.0:
                bv = bv * b_scale
            dot = _dot_tn(a_ref[...].astype(BF16), bv.astype(BF16))
            part = dot if part is None else part + dot

        @pl.when(k == 0)
        def _():
            acc_ref[...] = part

        @pl.when(k > 0)
        def _():
            acc_ref[...] += part

        @pl.when(k == nk - 1)
        def _():
            o_ref[...] = acc_ref[...].astype(BF16)

    if a.ndim == 3:
        a_specs = [pl.BlockSpec((None, tk, tmm), lambda i, k: (i // npj, k, i % npj))]
    elif sp > 1:
        a_specs = [pl.BlockSpec((tk, tmm), lambda i, k, s=s: (0, (k * sp + s) * npj + i)) for s in range(sp)]
    elif streams > 1:
        a_specs = [pl.BlockSpec((tk, tmm), lambda i, k: (k % kps, (k // kps) * npj + i))]
    else:
        a_specs = [pl.BlockSpec((tk, tmm), lambda i, k: (k, i))]
    return pl.pallas_call(
        body, name=name, grid=(J * npj, nk),
        in_specs=a_specs + [pl.BlockSpec((tk * sp, N), lambda i, k: (k, 0))],
        out_specs=pl.BlockSpec((tmm, N), lambda i, k: (i, 0)),
        out_shape=jax.ShapeDtypeStruct((J * F, N), BF16),
        scratch_shapes=[pltpu.VMEM((tmm, N), F32)],
        compiler_params=_params("parallel", "arbitrary"),
    )(*[a] * sp, b)


def _rows_mm(a, b, mode, name, *, x=None, scale=1.0, bias=None, tgt=None, g=None, dxo=None,
             b_row0=0, streams=1, extras=(), after=()):
    if a.ndim == 3:
        J, T, F = a.shape
    elif streams > 1:
        J, T, F = 1, a.shape[0] * streams, a.shape[1] // streams
    else:
        (T, F), J = a.shape, 1
    D = b.shape[1]
    if streams > 1 and T // streams < STREAM_TILE:
        L = T // streams
        sp = STREAM_TILE // L
        assert mode == "plain" and streams % sp == 0 and b_row0 % F == 0 and not after

        def body_short(a_ref, b_ref, o_ref):
            for s in range(sp):
                o_ref[s * L:(s + 1) * L, :] = _dot_nn(a_ref[:, s * F:(s + 1) * F], b_ref[...])

        return pl.pallas_call(
            body_short, name=name, grid=(streams // sp,),
            in_specs=[pl.BlockSpec((L, sp * F), lambda m: (0, m)), pl.BlockSpec((F, D), lambda m: (b_row0 // F, 0))],
            out_specs=pl.BlockSpec((sp * L, D), lambda m: (m, 0)),
            out_shape=jax.ShapeDtypeStruct((T, D), F32),
            compiler_params=_params("parallel"),
        )(a, b)
    tk = _tile(F, 3072)
    kpj = F // tk
    nk = J * kpj
    tm = min(T // streams, STREAM_TILE)
    nm = T // tm
    mps = nm // streams
    assert b_row0 % tk == 0
    kb0 = b_row0 // tk
    n_ex = len(extras)

    def body(*refs):
        refs = list(refs)
        a_ref, b_ref = refs[:2]
        rest = refs[2 + len(after):]
        if n_ex:
            tmp_ref = rest.pop()
        acc_ref = rest.pop()
        ex_refs, rest = rest[:n_ex], rest[n_ex:]
        m, k = pl.program_id(0), pl.program_id(1)
        part = _dot_nn(a_ref[...], b_ref[...])

        @pl.when(k == 0)
        def _():
            acc_ref[...] = part

        @pl.when(k > 0)
        def _():
            acc_ref[...] += part

        @pl.when(k == nk - 1)
        def _():
            acc = acc_ref[...]
            for (d, _), e_ref in zip(extras, ex_refs):
                for s in range(d):
                    for c in range(D // LANES):
                        tmp_ref.at[c][pl.ds(s, tm // d, stride=d), :] = e_ref[s, :, c * LANES:(c + 1) * LANES]
                acc = acc + jnp.concatenate([tmp_ref[c] for c in range(D // LANES)], axis=1)
            if mode == "plain":
                (o_ref,) = rest
                o_ref[...] = acc
            elif mode == "res":
                ins = list(rest)
                x_ref = ins.pop(0)
                if bias is not None:
                    acc = acc + ins.pop(0)[...]
                xn = x_ref[...] + scale * acc
                if g is None:
                    (o_ref,) = ins
                else:
                    g_ref, o_ref, h_ref = ins
                    r = lax.rsqrt(jnp.mean(xn * xn, axis=-1, keepdims=True) + NORM_EPS)
                    h_ref[...] = (xn * r * g_ref[...]).astype(BF16)
                o_ref[...] = xn
            elif mode == "loss":
                x_ref, t_ref, o_ref, l_ref = rest
                e = x_ref[...] + scale * acc - t_ref[...]
                o_ref[...] = e / D

                @pl.when(m == 0)
                def _():
                    l_ref[...] = jnp.zeros_like(l_ref)

                l_ref[...] += _rows8(e * e)
            else:
                x_ref, g_ref, d_ref, o_ref, dg_ref = rest
                xv = x_ref[...]
                r = lax.rsqrt(jnp.mean(xv * xv, axis=-1, keepdims=True) + NORM_EPS)
                xh = xv * r
                dxh = acc * g_ref[...]
                o_ref[...] = d_ref[...] + r * (dxh - xh * jnp.mean(dxh * xh, axis=-1, keepdims=True))

                @pl.when(m == 0)
                def _():
                    dg_ref[...] = jnp.zeros_like(dg_ref)

                dg_ref[...] += _rows8(acc * xh)

    if a.ndim == 3:
        a_spec = pl.BlockSpec((None, tm, tk), lambda m, k: (k // kpj, m, k % kpj))
    elif streams > 1:
        a_spec = pl.BlockSpec((tm, tk), lambda m, k: (m % mps, (m // mps) * kpj + k))
    else:
        a_spec = pl.BlockSpec((tm, tk), lambda m, k: (m, k))
    row = pl.BlockSpec((tm, D), lambda m, k: (m, 0))
    vec = pl.BlockSpec((1, D), lambda m, k: (0, 0))
    part8 = pl.BlockSpec((8, D), lambda m, k: (0, 0))
    in_specs = [a_spec, pl.BlockSpec((tk, D), lambda m, k: (kb0 + k, 0))] + [ANY] * len(after)
    args = [a, b, *after]
    for d, e in extras:
        in_specs.append(pl.BlockSpec((d, tm // d, D), lambda m, k: (0, m, 0)))
        args.append(e.reshape(d, T // d, D))
    full = jax.ShapeDtypeStruct((T, D), F32)
    small = jax.ShapeDtypeStruct((8, D), F32)
    if mode == "plain":
        out_specs, out_shape = row, full
    elif mode == "res":
        in_specs.append(row)
        args.append(x)
        if bias is not None:
            in_specs.append(vec)
            args.append(bias)
        if g is None:
            out_specs, out_shape = row, full
        else:
            in_specs.append(vec)
            args.append(g)
            out_specs, out_shape = [row, row], [full, jax.ShapeDtypeStruct((T, D), BF16)]
    elif mode == "loss":
        in_specs += [row, row]
        args += [x, tgt]
        out_specs, out_shape = [row, part8], [full, small]
    else:
        in_specs += [row, vec, row]
        args += [x, g, dxo]
        out_specs, out_shape = [row, part8], [full, small]
    return pl.pallas_call(
        body, name=name, grid=(nm, nk), in_specs=in_specs, out_specs=out_specs, out_shape=out_shape,
        scratch_shapes=[pltpu.VMEM((tm, D), F32)] + ([pltpu.VMEM((D // LANES, tm, LANES), F32)] if n_ex else []),
        compiler_params=_params("arbitrary", "arbitrary"),
    )(*args)


def _sublane_phases(sh):
    rows = sh.shape[1] - SUBLANES
    for p in range(1, SUBLANES):
        sh[p, pl.ds(0, rows), :] = sh[0, pl.ds(p, rows), :]


def _shifted(sh, offset, j, rows=SUBLANES):
    start = pl.multiple_of(j * rows + (offset - offset % SUBLANES), SUBLANES)
    return sh[offset % SUBLANES, pl.ds(start, rows), :]


def _conv_taps(wb, sh, offsets, j, init):
    groups = CONV_CHUNK // SUBLANES
    accs = [init[g * SUBLANES:(g + 1) * SUBLANES] for g in range(groups)]
    for k, off in enumerate(offsets):
        w = wb[k]
        start = pl.multiple_of(j * CONV_CHUNK + (off - off % SUBLANES), SUBLANES)
        for g in range(groups):
            accs[g] = accs[g] + w * sh[off % SUBLANES, pl.ds(start + g * SUBLANES, SUBLANES), :]
    return jnp.concatenate(accs, axis=0)


def _broadcast_rows(dst, src):
    for r in range(dst.shape[0]):
        dst[r] = jnp.zeros(dst.shape[1:], F32) + src[r:r + 1, :]


def _dwconv_fwd(u, w, b_dw, cg, name):
    T, C = u.shape
    tm = min(T, 256)
    hb = tm // CONV_HALO
    pad = CONV_HALO - (CONV_WIDTH - 1)

    def body(u_ref, halo_ref, w_ref, b_ref, g_ref, v_ref, s_ref, sh, wb):
        i = pl.program_id(0)
        sh[0, pl.ds(0, CONV_HALO), :] = jnp.where(i > 0, halo_ref[...], 0.0)
        sh[0, pl.ds(CONV_HALO, tm), :] = u_ref[...]
        _sublane_phases(sh)
        _broadcast_rows(wb, w_ref)
        bias = jnp.zeros((CONV_CHUNK, C), F32) + b_ref[...]

        def chunk(j, carry):
            acc = _conv_taps(wb, sh, [pad + k for k in range(CONV_WIDTH)], j, bias)
            v_ref[pl.ds(pl.multiple_of(j * CONV_CHUNK, CONV_CHUNK), CONV_CHUNK), :] = acc
            return carry

        lax.fori_loop(0, tm // CONV_CHUNK, chunk, 0)
        acc = v_ref[...]
        r = lax.rsqrt(jnp.mean(acc * acc, axis=-1, keepdims=True) + NORM_EPS)
        n = acc * r * g_ref[...]
        s_ref[...] = (n * _sigmoid(n)).astype(BF16)

    row = pl.BlockSpec((tm, C), lambda i: (i, 0))
    vec = pl.BlockSpec((1, C), lambda i: (0, 0))
    return pl.pallas_call(
        body, name=name, grid=(T // tm,),
        in_specs=[row, pl.BlockSpec((CONV_HALO, C), lambda i: (jnp.maximum(i * hb - 1, 0), 0)),
                  pl.BlockSpec((CONV_WIDTH, C), lambda i: (0, 0)), vec, vec],
        out_specs=[row, row],
        out_shape=[jax.ShapeDtypeStruct((T, C), F32), jax.ShapeDtypeStruct((T, C), BF16)],
        scratch_shapes=[pltpu.VMEM((SUBLANES, CONV_HALO + tm, C), F32), pltpu.VMEM((CONV_WIDTH, SUBLANES, C), F32)],
        compiler_params=_params("parallel"),
    )(u, u, w, b_dw, cg)


def _conv_post_bwd(ds, v, cg, dxo, name):
    T, C = v.shape
    tm = min(T, 512)

    def body(ds_ref, v_ref, g_ref, d_ref, dv_ref, dcg_ref, dbdw_ref, dbpw2_ref):
        @pl.when(pl.program_id(0) == 0)
        def _():
            dcg_ref[...] = jnp.zeros_like(dcg_ref)
            dbdw_ref[...] = jnp.zeros_like(dbdw_ref)
            dbpw2_ref[...] = jnp.zeros_like(dbpw2_ref)

        vv = v_ref[...]
        r = lax.rsqrt(jnp.mean(vv * vv, axis=-1, keepdims=True) + NORM_EPS)
        vh = vv * r
        n = vh * g_ref[...]
        sg = _sigmoid(n)
        dn = ds_ref[...] * (sg * (1.0 + n * (1.0 - sg)))
        dvh = dn * g_ref[...]
        dv = r * (dvh - vh * jnp.mean(dvh * vh, axis=-1, keepdims=True))
        dv_ref[...] = dv
        dcg_ref[...] += _rows8(dn * vh)
        dbdw_ref[...] += _rows8(dv)
        dbpw2_ref[...] += _rows8(d_ref[...])

    row = pl.BlockSpec((tm, C), lambda i: (i, 0))
    part8 = pl.BlockSpec((8, C), lambda i: (0, 0))
    small = jax.ShapeDtypeStruct((8, C), F32)
    return pl.pallas_call(
        body, name=name, grid=(T // tm,),
        in_specs=[row, row, pl.BlockSpec((1, C), lambda i: (0, 0)), row],
        out_specs=[row, part8, part8, part8],
        out_shape=[jax.ShapeDtypeStruct((T, C), F32), small, small, small],
        compiler_params=_params("arbitrary"),
    )(ds, v, cg, dxo)


def _dwconv_bwd(dv, u, w, z3, name):
    T, C = u.shape
    tm = min(T, 256)
    hb = tm // CONV_HALO
    nt = T // tm
    pad = CONV_HALO - (CONV_WIDTH - 1)

    taps_at_once = 4
    chunks_at_once = 8

    def body(dv_ref, dvn_ref, u_ref, up_ref, w_ref, z_ref, dz_ref, dw_ref, db_ref, dvsh, ush, wb, du_ref, dwacc):
        i = pl.program_id(0)

        @pl.when(i == 0)
        def _():
            dwacc[...] = jnp.zeros_like(dwacc)
            db_ref[...] = jnp.zeros_like(db_ref)

        dvsh[0, pl.ds(0, tm), :] = dv_ref[...]
        dvsh[0, pl.ds(tm, CONV_HALO), :] = jnp.where(i < nt - 1, dvn_ref[...], 0.0)
        ush[0, pl.ds(0, CONV_HALO), :] = jnp.where(i > 0, up_ref[...], 0.0)
        ush[0, pl.ds(CONV_HALO, tm), :] = u_ref[...]
        _sublane_phases(dvsh)
        _sublane_phases(ush)
        _broadcast_rows(wb, w_ref)

        def du_chunk(j, carry):
            acc = _conv_taps(wb, dvsh, [CONV_WIDTH - 1 - k for k in range(CONV_WIDTH)], j,
                             jnp.zeros((CONV_CHUNK, C), F32))
            du_ref[pl.ds(pl.multiple_of(j * CONV_CHUNK, CONV_CHUNK), CONV_CHUNK), :] = acc
            return carry

        lax.fori_loop(0, tm // CONV_CHUNK, du_chunk, 0)

        for k0 in range(0, CONV_WIDTH, taps_at_once):
            taps = range(k0, min(k0 + taps_at_once, CONV_WIDTH))

            def dw_chunks(jj, carry, taps=taps):
                accs = [jnp.zeros((SUBLANES, C), F32) for _ in taps]
                for u_ in range(chunks_at_once):
                    j = jj * chunks_at_once + u_
                    dvc = dv_ref[pl.ds(pl.multiple_of(j * SUBLANES, SUBLANES), SUBLANES), :]
                    accs = [acc + dvc * _shifted(ush, pad + k, j) for acc, k in zip(accs, taps)]
                for k, acc in zip(taps, accs):
                    dwacc[k] += acc
                return carry

            lax.fori_loop(0, tm // (SUBLANES * chunks_at_once), dw_chunks, 0)

        du = du_ref[...]
        a = z_ref[0].astype(F32)
        gate = z_ref[1].astype(F32)
        sg = _sigmoid(gate)
        da = du * sg
        dgate = du * a * sg * (1.0 - sg)
        dz_ref[0] = da.astype(BF16)
        dz_ref[1] = dgate.astype(BF16)
        db_ref[0:8, :] += _rows8(da)
        db_ref[8:16, :] += _rows8(dgate)

        @pl.when(i == nt - 1)
        def _():
            dw_ref[...] = jnp.zeros_like(dw_ref)
            for k in range(CONV_WIDTH):
                dw_ref[k:k + 1, :] = jnp.sum(dwacc[k], axis=0, keepdims=True)

    row = pl.BlockSpec((tm, C), lambda i: (i, 0))
    last_halo = T // CONV_HALO - 1
    return pl.pallas_call(
        body, name=name, grid=(nt,),
        in_specs=[row, pl.BlockSpec((CONV_HALO, C), lambda i: (jnp.minimum((i + 1) * hb, last_halo), 0)),
                  row, pl.BlockSpec((CONV_HALO, C), lambda i: (jnp.maximum(i * hb - 1, 0), 0)),
                  pl.BlockSpec((CONV_WIDTH, C), lambda i: (0, 0)),
                  pl.BlockSpec((2, tm, C), lambda i: (0, i, 0))],
        out_specs=[pl.BlockSpec((2, tm, C), lambda i: (0, i, 0)),
                   pl.BlockSpec((CONV_HALO, C), lambda i: (0, 0)),
                   pl.BlockSpec((16, C), lambda i: (0, 0))],
        out_shape=[jax.ShapeDtypeStruct((2, T, C), BF16), jax.ShapeDtypeStruct((CONV_HALO, C), F32),
                   jax.ShapeDtypeStruct((16, C), F32)],
        scratch_shapes=[pltpu.VMEM((SUBLANES, tm + CONV_HALO, C), F32), pltpu.VMEM((SUBLANES, CONV_HALO + tm, C), F32),
                        pltpu.VMEM((CONV_WIDTH, SUBLANES, C), F32), pltpu.VMEM((tm, C), F32),
                        pltpu.VMEM((CONV_WIDTH, SUBLANES, C), F32)],
        compiler_params=_params("arbitrary"),
    )(dv, dv, u, u, w, z3)


def _head_norm(raw, gain):
    xv = raw.astype(F32)
    r = lax.rsqrt(jnp.mean(xv * xv, axis=-1, keepdims=True) + NORM_EPS)
    xh = xv * r
    return (xh * gain).astype(BF16), xh, r


def _band_mask(n):
    row = lax.broadcasted_iota(I32, (ATTN_BLOCK, 2 * ATTN_BLOCK), 0)
    col = lax.broadcasted_iota(I32, (ATTN_BLOCK, 2 * ATTN_BLOCK), 1)
    return (col >= row) & (col <= row + ATTN_BLOCK) & ((col >= ATTN_BLOCK) | (n > 0))


def _carry_previous(n, bufs):
    @pl.when(n == 0)
    def _():
        for b in bufs:
            b[0:ATTN_BLOCK, :] = jnp.zeros((ATTN_BLOCK, b.shape[1]), b.dtype)

    @pl.when(n > 0)
    def _():
        for b in bufs:
            b[0:ATTN_BLOCK, :] = b[ATTN_BLOCK:, :]


def _attn_fwd(qkv, qg, kg, g, name):
    d = ATTN_DILATIONS[g]
    L = qkv.shape[0]
    HD = qkv.shape[1] // (3 * d)
    H = HD // HEAD_DIM
    nb = L // ATTN_BLOCK
    scale = HEAD_DIM ** -0.5

    def body(q_ref, kc_ref, vc_ref, qg_ref, kg_ref, o_ref, lse_ref, kk, vv):
        n = pl.program_id(1)
        _carry_previous(n, [kk, vv])
        mask = _band_mask(n)
        lane = lax.broadcasted_iota(I32, (ATTN_BLOCK, 128), 1)
        gq = qg_ref[g:g + 1, :]
        gk = kg_ref[g:g + 1, :]
        vv[ATTN_BLOCK:, :] = vc_ref[...]
        lse_tile = jnp.zeros((ATTN_BLOCK, 128), F32)
        for h in range(H):
            sl = slice(h * HEAD_DIM, (h + 1) * HEAD_DIM)
            q = _head_norm(q_ref[:, sl], gq)[0]
            kk[ATTN_BLOCK:, sl] = _head_norm(kc_ref[:, sl], gk)[0]
            s = jnp.where(mask, _dot_nt(q, kk[:, sl]) * scale, NEG_BIG)
            m = jnp.max(s, axis=-1, keepdims=True)
            p = jnp.exp(s - m)
            l = jnp.sum(p, axis=-1, keepdims=True)
            o_ref[:, sl] = _dot_nn(p.astype(BF16), vv[:, sl]) / l
            lse_tile = jnp.where(lane == h, m + jnp.log(l), lse_tile)
        lse_ref[...] = lse_tile

    def blk(which):
        return pl.BlockSpec((ATTN_BLOCK, HD), lambda r, n: (n, r * 3 + which))

    gains = pl.BlockSpec((3, HEAD_DIM), lambda r, n: (0, 0))
    return pl.pallas_call(
        body, name=name, grid=(d, nb),
        in_specs=[blk(0), blk(1), blk(2), gains, gains],
        out_specs=[pl.BlockSpec((ATTN_BLOCK, HD), lambda r, n: (n, r)),
                   pl.BlockSpec((ATTN_BLOCK, 128), lambda r, n: (n, r))],
        out_shape=[jax.ShapeDtypeStruct((L, d * HD), F32), jax.ShapeDtypeStruct((L, d * 128), F32)],
        scratch_shapes=[pltpu.VMEM((2 * ATTN_BLOCK, HD), BF16), pltpu.VMEM((2 * ATTN_BLOCK, HD), BF16)],
        compiler_params=_params("arbitrary", "arbitrary"),
    )(qkv, qkv, qkv, qg, kg)


def _attn_merge(os, lses, name):
    T = os[0].shape[0]
    HD = os[0].shape[1]
    H = HD // HEAD_DIM
    tm = min(T, STREAM_TILE)
    dils = ATTN_DILATIONS[1:]

    def body(o0, o1, o2, l0, l1, l2, ob_ref, of_ref, lse_ref, n1, n2, m1, m2):
        for d, src, dst, w in ((dils[0], o1, n1, HD), (dils[1], o2, n2, HD), (dils[0], l1, m1, 128), (dils[1], l2, m2, 128)):
            for s in range(d):
                for c in range(w // LANES):
                    dst.at[c][pl.ds(s, tm // d, stride=d), :] = src[:, s * w + c * LANES:s * w + (c + 1) * LANES]
        ls = [l0[...], m1[0], m2[0]]
        m = jnp.maximum(jnp.maximum(ls[0], ls[1]), ls[2])
        tot = m + jnp.log(jnp.exp(ls[0] - m) + jnp.exp(ls[1] - m) + jnp.exp(ls[2] - m))
        lse_ref[...] = tot
        ws = [jnp.exp(l - tot) for l in ls]
        for h in range(H):
            sl = slice(h * HEAD_DIM, (h + 1) * HEAD_DIM)
            out = (ws[0][:, h:h + 1] * o0[:, sl] + ws[1][:, h:h + 1] * n1[h]) + ws[2][:, h:h + 1] * n2[h]
            of_ref[:, sl] = out
            ob_ref[:, sl] = out.astype(BF16)

    def wide(d, w):
        return pl.BlockSpec((tm // d, d * w), lambda i: (i, 0))

    row = pl.BlockSpec((tm, HD), lambda i: (i, 0))
    nar = pl.BlockSpec((tm, 128), lambda i: (i, 0))
    return pl.pallas_call(
        body, name=name, grid=(T // tm,),
        in_specs=[row, wide(dils[0], HD), wide(dils[1], HD), nar, wide(dils[0], 128), wide(dils[1], 128)],
        out_specs=[row, row, nar],
        out_shape=[jax.ShapeDtypeStruct((T, HD), BF16), jax.ShapeDtypeStruct((T, HD), F32),
                   jax.ShapeDtypeStruct((T, 128), F32)],
        scratch_shapes=[pltpu.VMEM((H, tm, HEAD_DIM), F32), pltpu.VMEM((H, tm, HEAD_DIM), F32),
                        pltpu.VMEM((1, tm, 128), F32), pltpu.VMEM((1, tm, 128), F32)],
        compiler_params=_params("parallel"),
    )(*os, *lses)


def _attn_delta(out, dout, lse, name):
    T, HD = out.shape
    H = HD // HEAD_DIM
    tm = min(T, STREAM_TILE)
    dils = ATTN_DILATIONS[1:]

    def body(o_ref, d_ref, l_ref, dl_ref, *rest):
        dch = rest[-1]
        lane = lax.broadcasted_iota(I32, (tm, 128), 1)
        tile = jnp.zeros((tm, 128), F32)
        for h in range(H):
            sl = slice(h * HEAD_DIM, (h + 1) * HEAD_DIM)
            dch[h] = d_ref[:, sl]
            tile = jnp.where(lane == h, jnp.sum(o_ref[:, sl] * d_ref[:, sl], axis=-1, keepdims=True), tile)
        dl_ref[...] = tile
        for i, d in enumerate(dils):
            dw_ref, lw_ref, dlw_ref = rest[3 * i:3 * i + 3]
            for s in range(d):
                rows = pl.ds(s, tm // d, stride=d)
                for h in range(H):
                    dw_ref[:, s * HD + h * HEAD_DIM:s * HD + (h + 1) * HEAD_DIM] = dch.at[h][rows, :].astype(BF16)
                lw_ref[:, s * 128:(s + 1) * 128] = l_ref[rows, :]
                dlw_ref[:, s * 128:(s + 1) * 128] = dl_ref[rows, :]

    row = pl.BlockSpec((tm, HD), lambda i: (i, 0))
    nar = pl.BlockSpec((tm, 128), lambda i: (i, 0))
    out_specs, out_shape = [nar], [jax.ShapeDtypeStruct((T, 128), F32)]
    for d in dils:
        out_specs += [pl.BlockSpec((tm // d, d * w), lambda i: (i, 0)) for w in (HD, 128, 128)]
        out_shape += [jax.ShapeDtypeStruct((T // d, d * HD), BF16), jax.ShapeDtypeStruct((T // d, d * 128), F32),
                      jax.ShapeDtypeStruct((T // d, d * 128), F32)]
    return pl.pallas_call(
        body, name=name, grid=(T // tm,), in_specs=[row, row, nar], out_specs=out_specs, out_shape=out_shape,
        scratch_shapes=[pltpu.VMEM((H, tm, HEAD_DIM), F32)],
        compiler_params=_params("parallel"),
    )(out, dout, lse)


def _attn_bwd(qkv, dout, lse, delta, qg, kg, g, name):
    d = ATTN_DILATIONS[g]
    L = qkv.shape[0]
    HD = qkv.shape[1] // (3 * d)
    H = HD // HEAD_DIM
    nb = L // ATTN_BLOCK
    total = d * nb
    scale = HEAD_DIM ** -0.5

    def body(*refs):
        (q_ref, kc_ref, vc_ref, do_ref, lse_ref, dl_ref, qg_ref, kg_ref,
         out_ref, gg_ref, dq_a, dk_a, dv_a, dq_b, dk_b, dv_b, kk, vv, khh, rkk) = refs
        t = pl.program_id(0)
        n = t % nb
        gq = qg_ref[g:g + 1, :]
        gk = kg_ref[g:g + 1, :]
        _carry_previous(t, [kk, vv, khh, rkk])

        @pl.when(t == 0)
        def _():
            gg_ref[...] = jnp.zeros_like(gg_ref)
            dk_a[...] = jnp.zeros_like(dk_a)
            dv_a[...] = jnp.zeros_like(dv_a)

        @pl.when(t < total)
        def _():
            mask = _band_mask(n)
            vv[ATTN_BLOCK:, :] = vc_ref[...]
            gq_sum = jnp.zeros((1, HEAD_DIM), F32)
            for h in range(H):
                sl = slice(h * HEAD_DIM, (h + 1) * HEAD_DIM)
                qn, qh, rq = _head_norm(q_ref[:, sl], gq)
                kcn, kch, rkc = _head_norm(kc_ref[:, sl], gk)
                kk[ATTN_BLOCK:, sl] = kcn
                khh[ATTN_BLOCK:, sl] = kch
                rkk[ATTN_BLOCK:, sl] = jnp.broadcast_to(rkc, (ATTN_BLOCK, HEAD_DIM))
                kn = kk[:, sl]
                p = jnp.where(mask, jnp.exp(_dot_nt(qn, kn) * scale - lse_ref[:, h:h + 1]), 0.0)
                do = do_ref[:, sl].astype(BF16)
                ds = (p * (_dot_nt(do, vv[:, sl]) - dl_ref[:, h:h + 1]) * scale).astype(BF16)
                dqn = _dot_nn(ds, kn)
                dqh = dqn * gq
                dq_b[:, sl] = rq * (dqh - qh * jnp.mean(dqh * qh, axis=-1, keepdims=True))
                gq_sum = gq_sum + jnp.sum(dqn * qh, axis=0, keepdims=True)
                dkn = _dot_tn(ds, qn)
                dvn = _dot_tn(p.astype(BF16), do)
                dk_a[:, sl] += dkn[0:ATTN_BLOCK]
                dv_a[:, sl] += dvn[0:ATTN_BLOCK]
                dk_b[:, sl] = dkn[ATTN_BLOCK:]
                dv_b[:, sl] = dvn[ATTN_BLOCK:]
            gg_ref[0:1, :] += gq_sum

        @pl.when(t > 0)
        def _():
            out_ref[:, 0:HD] = dq_a[...].astype(BF16)
            gk_sum = jnp.zeros((1, HEAD_DIM), F32)
            for h in range(H):
                sl = slice(h * HEAD_DIM, (h + 1) * HEAD_DIM)
                kh = khh[0:ATTN_BLOCK, sl]
                rk = rkk[0:ATTN_BLOCK, sl]
                dkn = dk_a[:, sl]
                dkh = dkn * gk
                dk = rk * (dkh - kh * jnp.mean(dkh * kh, axis=-1, keepdims=True))
                out_ref[:, HD + h * HEAD_DIM:HD + (h + 1) * HEAD_DIM] = dk.astype(BF16)
                gk_sum = gk_sum + jnp.sum(dkn * kh, axis=0, keepdims=True)
            out_ref[:, 2 * HD:3 * HD] = dv_a[...].astype(BF16)
            gg_ref[1:2, :] += gk_sum

        @pl.when(t < total)
        def _():
            dq_a[...] = dq_b[...]
            dk_a[...] = dk_b[...]
            dv_a[...] = dv_b[...]

    def at(t):
        t = jnp.minimum(t, total - 1)
        return t % nb, t // nb

    def blk(which):
        return pl.BlockSpec((ATTN_BLOCK, HD), lambda t: (at(t)[0], at(t)[1] * 3 + which))

    gains = pl.BlockSpec((3, HEAD_DIM), lambda t: (0, 0))
    nar = pl.BlockSpec((ATTN_BLOCK, 128), lambda t: at(t))
    acc = pltpu.VMEM((ATTN_BLOCK, HD), F32)
    pair16 = pltpu.VMEM((2 * ATTN_BLOCK, HD), BF16)
    pair32 = pltpu.VMEM((2 * ATTN_BLOCK, HD), F32)
    return pl.pallas_call(
        body, name=name, grid=(total + 1,),
        in_specs=[blk(0), blk(1), blk(2), pl.BlockSpec((ATTN_BLOCK, HD), lambda t: at(t)), nar, nar, gains, gains],
        out_specs=[pl.BlockSpec((ATTN_BLOCK, 3 * HD), lambda t: at(jnp.maximum(t - 1, 0))),
                   pl.BlockSpec((8, HEAD_DIM), lambda t: (0, 0))],
        out_shape=[jax.ShapeDtypeStruct((L, d * 3 * HD), BF16), jax.ShapeDtypeStruct((8, HEAD_DIM), F32)],
        scratch_shapes=[acc, acc, acc, acc, acc, acc, pair16, pair16, pair32, pair32],
        compiler_params=_params("arbitrary"),
    )(qkv, qkv, qkv, dout, lse, delta, qg, kg)


def _adamw(w, grad, m, v, name):
    if w.ndim == 2:
        R, C = w.shape
        tr = _row_tile(R, 512)
        blk, steps = pl.BlockSpec((tr, C), lambda i: (i, 0)), R // tr
    else:
        zeros = (0,) * w.ndim
        blk, steps = pl.BlockSpec(w.shape, lambda i: zeros), 1

    def body(w_ref, g_ref, m_ref, v_ref, d_ref, nm_ref, nv_ref):
        gv = g_ref[...]
        nm = ADAM_B1 * m_ref[...] + (1.0 - ADAM_B1) * gv
        nv = ADAM_B2 * v_ref[...] + (1.0 - ADAM_B2) * (gv * gv)
        m_hat = nm / (1.0 - ADAM_B1 ** ADAM_STEP)
        v_hat = nv / (1.0 - ADAM_B2 ** ADAM_STEP)
        d_ref[...] = -ADAM_LR * (m_hat / (jnp.sqrt(v_hat) + ADAM_EPS) + ADAM_WD * w_ref[...])
        nm_ref[...] = nm
        nv_ref[...] = nv

    shp = jax.ShapeDtypeStruct(w.shape, F32)
    return pl.pallas_call(
        body, name=name, grid=(steps,), in_specs=[blk] * 4, out_specs=[blk] * 3, out_shape=[shp] * 3,
        compiler_params=_params("parallel"),
    )(w, grad, m, v)


def _ffn_fwd(x, h, comm, s, tag, tgt=None, next_g=None):
    (win_t, wout), tok = comm.weights(s, h)
    z3, a = _gated_in(h, win_t, None, "swiglu", BF16, f"ffn_in_{tag}", after=tok)
    saved = (x, h, z3, a, win_t, wout)
    if tgt is None:
        return _rows_mm(a, wout, "res", f"ffn_out_{tag}", x=x, scale=0.5, g=next_g), saved
    return _rows_mm(a, wout, "loss", f"ffn_out_loss_{tag}", x=x, scale=0.5, tgt=tgt), saved


def _ffn_bwd(dxo, saved, g, comm, s, tag, after):
    x, h, z3, a, win_t, wout = saved
    dz3 = _swiglu_bwd(dxo, wout, z3, 0.5, f"ffn_out_bwd_{tag}", after=after)
    d_wout = _mm_tn(a, dxo, 0.5, f"ffn_dwout_{tag}")
    d_win_t = _mm_tn(dz3, h, 1.0, f"ffn_dwin_{tag}")
    tok = comm.grads(s, [d_win_t, d_wout])
    dx, dg8 = _rows_mm(dz3, win_t, "rmsbwd", f"ffn_in_bwd_{tag}", x=x, g=g, dxo=dxo, after=tok)
    return dx, dg8, ()


def _local_step(x, tgt, vecs, comm):
    norm_g = vecs["norm_g"]

    def gvec(i):
        return norm_g[i:i + 1]

    h0 = _rmsnorm(x, gvec(0), "rmsnorm_l0a", after=comm.started)
    (x1, h_c), sv_a0 = _ffn_fwd(x, h0, comm, 0, "l0a", next_g=gvec(1))
    (pw1_t, pw2), tok = comm.weights(1, h_c)
    zc3, u = _gated_in(h_c, pw1_t, vecs["b_pw1"], "glu", F32, "conv_pw1", after=tok)
    v, s = _dwconv_fwd(u, vecs["w_dw"], vecs["b_dw"], vecs["cg"], "conv_dw")
    x2, h_b0 = _rows_mm(s, pw2, "res", "conv_pw2", x=x1, bias=vecs["b_pw2"], g=gvec(2))
    (x3, h_a1), sv_b0 = _ffn_fwd(x2, h_b0, comm, 2, "l0b", next_g=gvec(3))
    x4, sv_a1 = _ffn_fwd(x3, h_a1, comm, 3, "l1a")
    h_s = _rmsnorm_streams(x4, gvec(4), "rmsnorm_attn")
    (qkv_t, wo), tok = comm.weights(4, h_s[0])
    hd3 = qkv_t.shape[0] // 3
    qkvs, os, lses = [], [], []
    for g, d in enumerate(ATTN_DILATIONS):
        qkv = _mm_nt(h_s[g], qkv_t, BF16, f"attn_qkv_g{g}", w_row0=g * hd3, n_rows=hd3, streams=d, after=tok)
        o, lse = _attn_fwd(qkv, vecs["q_norm"], vecs["k_norm"], g, f"attn_fwd_g{g}")
        qkvs.append(qkv)
        os.append(o)
        lses.append(lse)
    out_b, out_f, lse_tot = _attn_merge(os, lses, "attn_merge")
    x5, h_b1 = _rows_mm(out_b, wo, "res", "attn_wo", x=x4, g=gvec(5))
    (dy, loss8), sv_b1 = _ffn_fwd(x5, h_b1, comm, 5, "l1b", tgt=tgt)
    loss = 0.5 * jnp.sum(loss8) / x.shape[1]

    dg = [None] * 6
    dx, dg[5], tok = _ffn_bwd(dy, sv_b1, gvec(5), comm, 5, "l1b", ())
    dout = _mm_nt(dx, wo, F32, "attn_wo_bwd", after=tok)
    d_wo = _mm_tn(out_b, dx, 1.0, "attn_dwo")
    delta, *wide = _attn_delta(out_f, dout, lse_tot, "attn_delta")
    per_group = [(dout, lse_tot, delta), tuple(wide[0:3]), tuple(wide[3:6])]
    ggs, d_qkv_t, dhs = [], [], []
    for g, d in enumerate(ATTN_DILATIONS):
        dqkv, gg = _attn_bwd(qkvs[g], *per_group[g], vecs["q_norm"], vecs["k_norm"], g, f"attn_bwd_g{g}")
        ggs.append(gg)
        d_qkv_t.append(_mm_tn(dqkv, h_s[g], 1.0, f"attn_dwqkv_g{g}", streams=d))
        if d > 1:
            dhs.append((d, _rows_mm(dqkv, qkv_t, "plain", f"attn_qkv_bwd_g{g}", b_row0=g * hd3, streams=d)))
        else:
            dqkv0 = dqkv
    tok = comm.grads(4, [jnp.concatenate(d_qkv_t), d_wo])
    dx, dg[4] = _rows_mm(dqkv0, qkv_t, "rmsbwd", "attn_qkv_bwd_g0", x=x4, g=gvec(4), dxo=dx, extras=dhs, after=tok)
    dx, dg[3], tok = _ffn_bwd(dx, sv_a1, gvec(3), comm, 3, "l1a", ())
    dx, dg[2], tok = _ffn_bwd(dx, sv_b0, gvec(2), comm, 2, "l0b", tok)
    ds = _mm_nt(dx, pw2, F32, "conv_pw2_bwd", after=tok)
    d_pw2 = _mm_tn(s, dx, 1.0, "conv_dwpw2")
    dv, dcg8, dbdw8, dbpw2_8 = _conv_post_bwd(ds, v, vecs["cg"], dx, "conv_post_bwd")
    dzc3, d_wdw, db1_16 = _dwconv_bwd(dv, u, vecs["w_dw"], zc3, "conv_dw_bwd")
    d_pw1_t = _mm_tn(dzc3, h_c, 1.0, "conv_dwpw1")
    tok = comm.grads(1, [d_pw1_t, d_pw2])
    dx, dg[1] = _rows_mm(dzc3, pw1_t, "rmsbwd", "conv_pw1_bwd", x=x1, g=gvec(1), dxo=dx, after=tok)
    dx, dg[0], tok = _ffn_bwd(dx, sv_a0, gvec(0), comm, 0, "l0a", ())

    D = x.shape[1]
    small = {
        "norm_g": jnp.stack([p.sum(axis=0) for p in dg]),
        "b_pw1": db1_16.reshape(2, 8, D).sum(axis=1),
        "w_dw": d_wdw[:CONV_WIDTH],
        "b_dw": dbdw8.sum(axis=0, keepdims=True),
        "cg": dcg8.sum(axis=0, keepdims=True),
        "b_pw2": dbpw2_8.sum(axis=0, keepdims=True),
        "q_norm": jnp.stack([gg[0] for gg in ggs]),
        "k_norm": jnp.stack([gg[1] for gg in ggs]),
    }
    return loss, dx, small, tok


def _mesh_pos():
    return lax.axis_index("x"), lax.axis_index("y"), lax.axis_index("c")


def _other_chips(x, y):
    return [(1 - x, y), (x, 1 - y), (1 - x, 1 - y)]


HBM = pl.BlockSpec(memory_space=pltpu.HBM)
SEM = pl.BlockSpec(memory_space=pltpu.SEMAPHORE)
SPLIT_COPY = pltpu.SideEffectType.DATAFLOW_SIDE_EFFECTING


def _hbm(a):
    return pltpu.with_memory_space_constraint(a, pltpu.HBM)


def _gather_copies(j, src, out, send, recv, x, y, c):
    me = 4 * x + 2 * y + c
    peers = [(x, y, 1 - c)] + [(px, py, c) for px, py in _other_chips(x, y)]
    return [pltpu.make_async_remote_copy(src_ref=src, dst_ref=out.at[me], send_sem=send.at[4 * j + k],
                                         recv_sem=recv.at[4 * j + k], device_id=peer, device_id_type=MESH)
            for k, peer in enumerate(peers)]


def _gather_start(shards, me, after, name):
    n = len(shards)
    lands = [lax.dynamic_update_slice(lax.empty((N_DEV,) + s.shape, s.dtype), s[None], (me, 0, 0)) for s in shards]

    def body(*refs):
        ins, outs = refs[:n], refs[2 * n + len(after):3 * n + len(after)]
        send, recv, token = refs[4 * n + len(after):]
        x, y, c = _mesh_pos()
        for j in range(n):
            for cp in _gather_copies(j, ins[j], outs[j], send, recv, x, y, c):
                cp.start()
        token[...] = jnp.zeros_like(token)

    res = pl.pallas_call(
        body, name=name, in_specs=[HBM] * (2 * n) + [ANY] * len(after),
        out_specs=[HBM] * (2 * n) + [SEM, SEM, pl.BlockSpec(memory_space=pltpu.VMEM)],
        out_shape=[pltpu.HBM((N_DEV,) + s.shape, s.dtype) for s in shards] + [pltpu.HBM(s.shape, s.dtype) for s in shards]
        + [pltpu.SemaphoreType.DMA((4 * n,)), pltpu.SemaphoreType.DMA((4 * n,)), jax.ShapeDtypeStruct((8, 128), F32)],
        input_output_aliases={**{i: n + i for i in range(n)}, **{n + i: i for i in range(n)}},
        compiler_params=pltpu.CompilerParams(has_side_effects=SPLIT_COPY),
    )(*[_hbm(s) for s in shards], *[_hbm(l) for l in lands], *after)
    return res[:n], res[n:2 * n], res[2 * n], res[2 * n + 1], res[2 * n + 2]


def _gather_wait(outs, thru, send, recv, after, name):
    n = len(outs)

    def body(*refs):
        out_refs, thru_refs, send_ref, recv_ref = refs[:n], refs[n:2 * n], refs[2 * n], refs[2 * n + 1]
        x, y, c = _mesh_pos()
        for j in range(n):
            for cp in _gather_copies(j, thru_refs[j], out_refs[j], send_ref, recv_ref, x, y, c):
                cp.wait_send()
                cp.wait_recv()

    res = pl.pallas_call(
        body, name=name, in_specs=[HBM] * (2 * n) + [SEM, SEM] + [ANY] * len(after), out_specs=[HBM] * (2 * n),
        out_shape=[pltpu.HBM(a.shape, a.dtype) for a in list(outs) + list(thru)],
        input_output_aliases={i: i for i in range(2 * n)},
        compiler_params=pltpu.CompilerParams(has_side_effects=SPLIT_COPY),
    )(*outs, *thru, send, recv, *after)
    return res[:n]


def _gather_forward(outs, name):
    n = len(outs)

    def body(*refs):
        o = refs[n:2 * n]
        send_sems, recv_sems = refs[2 * n:]
        x, y, c = _mesh_pos()
        copies = []
        for j in range(n):
            for k, (px, py) in enumerate(_other_chips(x, y)):
                blk = o[j].at[4 * px + 2 * py + c]
                cp = pltpu.make_async_remote_copy(src_ref=blk, dst_ref=blk, send_sem=send_sems.at[j, k],
                                                  recv_sem=recv_sems.at[j, k], device_id=(x, y, 1 - c), device_id_type=MESH)
                cp.start()
                copies.append(cp)
        for cp in copies:
            cp.wait()

    return pl.pallas_call(
        body, name=name, in_specs=[ANY] * n, out_specs=[ANY] * n,
        out_shape=[jax.ShapeDtypeStruct(a.shape, a.dtype) for a in outs],
        input_output_aliases={i: i for i in range(n)},
        scratch_shapes=[pltpu.SemaphoreType.DMA((n, 3)), pltpu.SemaphoreType.DMA((n, 3))],
    )(*outs)


def _all_sum(p, name):
    R, C = p.shape

    def body(p_ref, o_ref, buf, send_sems, recv_sems):
        x, y, c = _mesh_pos()
        me = 4 * x + 2 * y + c
        buf[me] = p_ref[...]
        copies = []
        for rel in range(1, N_DEV):
            peer = (1 - x if rel & 4 else x, 1 - y if rel & 2 else y, 1 - c if rel & 1 else c)
            cp = pltpu.make_async_remote_copy(
                src_ref=p_ref, dst_ref=buf.at[me], send_sem=send_sems.at[rel - 1], recv_sem=recv_sems.at[rel - 1],
                device_id=peer, device_id_type=MESH)
            cp.start()
            copies.append(cp)
        for cp in copies:
            cp.wait()
        acc = buf[0]
        for dev in range(1, N_DEV):
            acc = acc + buf[dev]
        o_ref[...] = acc

    vm = pl.BlockSpec(memory_space=pltpu.VMEM)
    return pl.pallas_call(
        body, name=name, in_specs=[vm], out_specs=vm, out_shape=jax.ShapeDtypeStruct((R, C), F32),
        scratch_shapes=[pltpu.VMEM((N_DEV, R, C), F32), pltpu.SemaphoreType.DMA((N_DEV - 1,)),
                        pltpu.SemaphoreType.DMA((N_DEV - 1,))],
    )(p)


def _swap_with_sibling(gs, name):
    n = len(gs)

    def body(*refs):
        ins, outs = refs[:n], refs[n:2 * n]
        send_sems, recv_sems = refs[2 * n:]
        x, y, c = _mesh_pos()
        copies = []
        for i in range(n):
            for k in range(4):
                cp = pltpu.make_async_remote_copy(
                    src_ref=ins[i].at[2 * k + (1 - c)], dst_ref=outs[i].at[k],
                    send_sem=send_sems.at[i, k], recv_sem=recv_sems.at[i, k],
                    device_id=(x, y, 1 - c), device_id_type=MESH)
                cp.start()
                copies.append(cp)
        for cp in copies:
            cp.wait()

    return pl.pallas_call(
        body, name=name, in_specs=[ANY] * n, out_specs=[ANY] * n,
        out_shape=[jax.ShapeDtypeStruct((4,) + g.shape[1:], g.dtype) for g in gs],
        scratch_shapes=[pltpu.SemaphoreType.DMA((n, 4)), pltpu.SemaphoreType.DMA((n, 4))],
    )(*gs)


def _scatter_copies(j, src, land, send, recv, x, y, c):
    return [pltpu.make_async_remote_copy(src_ref=src.at[2 * px + py], dst_ref=land.at[k], send_sem=send.at[3 * j + k],
                                         recv_sem=recv.at[3 * j + k], device_id=(px, py, c), device_id_type=MESH)
            for k, (px, py) in enumerate(_other_chips(x, y))]


def _scatter_start(ps, name):
    n = len(ps)

    def body(*refs):
        ins, lands = refs[:n], refs[2 * n:3 * n]
        send, recv, token = refs[3 * n:]
        x, y, c = _mesh_pos()
        for j in range(n):
            for cp in _scatter_copies(j, ins[j], lands[j], send, recv, x, y, c):
                cp.start()
        token[...] = jnp.zeros_like(token)

    res = pl.pallas_call(
        body, name=name, in_specs=[HBM] * n,
        out_specs=[HBM] * (2 * n) + [SEM, SEM, pl.BlockSpec(memory_space=pltpu.VMEM)],
        out_shape=[pltpu.HBM(p.shape, p.dtype) for p in ps] + [pltpu.HBM((3,) + p.shape[1:], p.dtype) for p in ps]
        + [pltpu.SemaphoreType.DMA((3 * n,)), pltpu.SemaphoreType.DMA((3 * n,)), jax.ShapeDtypeStruct((8, 128), F32)],
        input_output_aliases={i: i for i in range(n)},
        compiler_params=pltpu.CompilerParams(has_side_effects=SPLIT_COPY),
    )(*[_hbm(p) for p in ps])
    return res[:n], res[n:2 * n], res[2 * n], res[2 * n + 1], res[2 * n + 2]


def _scatter_wait(thru, lands, send, recv, after, name):
    n = len(thru)

    def body(*refs):
        thru_refs, land_refs, send_ref, recv_ref = refs[:n], refs[n:2 * n], refs[2 * n], refs[2 * n + 1]
        x, y, c = _mesh_pos()
        for j in range(n):
            for cp in _scatter_copies(j, thru_refs[j], land_refs[j], send_ref, recv_ref, x, y, c):
                cp.wait_send()
                cp.wait_recv()

    res = pl.pallas_call(
        body, name=name, in_specs=[HBM] * (2 * n) + [SEM, SEM, ANY], out_specs=[HBM] * (2 * n),
        out_shape=[pltpu.HBM(a.shape, a.dtype) for a in list(thru) + list(lands)],
        input_output_aliases={i: i for i in range(2 * n)},
        compiler_params=pltpu.CompilerParams(has_side_effects=SPLIT_COPY),
    )(*thru, *lands, send, recv, after)
    return res[:n], res[n:]


def _row_tile(r, pref):
    if r <= pref:
        return r
    for t in range(pref - pref % 16, 0, -16):
        if r % t == 0:
            return t
    return r


def _add_sibling(g, r1, pos, name):
    _, r, D = g.shape
    tr = _row_tile(r, 512)

    def body(pos_ref, g_ref, r_ref, o_ref):
        o_ref[...] = (g_ref[...].astype(F32) + r_ref[...].astype(F32)).astype(BF16)

    return pl.pallas_call(
        body, name=name,
        grid_spec=pltpu.PrefetchScalarGridSpec(
            num_scalar_prefetch=1, grid=(4, r // tr),
            in_specs=[pl.BlockSpec((None, None, tr, D), lambda k, j, pos: (k, pos[0], j, 0)),
                      pl.BlockSpec((None, tr, D), lambda k, j, pos: (k, j, 0))],
            out_specs=pl.BlockSpec((None, tr, D), lambda k, j, pos: (k, j, 0))),
        out_shape=jax.ShapeDtypeStruct((4, r, D), BF16),
        compiler_params=_params("parallel", "parallel"),
    )(pos, g.reshape(4, 2, r, D), r1)


def _add_chips(p, r2, pos, name):
    _, r, D = p.shape
    tr = _row_tile(r, 512)

    def body(pos_ref, p_ref, r_ref, o_ref):
        acc = p_ref[...].astype(F32)
        for j in range(3):
            acc = acc + r_ref[j].astype(F32)
        o_ref[...] = acc

    return pl.pallas_call(
        body, name=name,
        grid_spec=pltpu.PrefetchScalarGridSpec(
            num_scalar_prefetch=1, grid=(r // tr,),
            in_specs=[pl.BlockSpec((None, tr, D), lambda j, pos: (pos[1], j, 0)),
                      pl.BlockSpec((3, tr, D), lambda j, pos: (0, j, 0))],
            out_specs=pl.BlockSpec((tr, D), lambda j, pos: (j, 0))),
        out_shape=jax.ShapeDtypeStruct((r, D), F32),
        compiler_params=_params("parallel"),
    )(pos, p, r2)


def _place_cols(local, me, width):
    R, w = local.shape
    return lax.dynamic_update_slice(jnp.zeros((R, width), F32), local, (0, me * w))


def _pad_rows(a, rows):
    return jnp.pad(a, ((0, rows - a.shape[0]), (0, 0)))


SUBLAYER_SHARDS = ((0, 4), (8, 9), (1, 5), (2, 6), (10, 11), (3, 7))


class _StepComm:
    def __init__(self, shards, pos):
        self.pos = pos
        self.shards = shards
        self.gathers = {}
        self.started = self._start_gather(1, self._start_gather(0, ()))
        self.pending = {}

    def _start_gather(self, s, after):
        outs, thru, send, recv, token = _gather_start([self.shards[i] for i in SUBLAYER_SHARDS[s]],
                                                      2 * self.pos[1] + self.pos[0], after, f"gather_start_{s}")
        self.gathers[s] = (outs, thru, send, recv)
        return (token,)

    def weights(self, s, after):
        outs = _gather_wait(*self.gathers.pop(s), (after,), f"gather_wait_{s}")
        outs = _gather_forward(outs, f"gather_forward_{s}")
        tok = self._start_gather(s + 2, (outs[0],)) if s + 2 < len(SUBLAYER_SHARDS) else ()
        return [w.reshape(-1, w.shape[-1]) for w in outs], tok

    def grads(self, s, mats):
        g3 = [g.reshape(N_DEV, g.shape[0] // N_DEV, g.shape[1]) for g in mats]
        r1 = _swap_with_sibling(g3, f"rs_sibling_{s}")
        ps = [_add_sibling(g, r, self.pos, f"rs_add_sibling_{s}_{i}") for i, (g, r) in enumerate(zip(g3, r1))]
        thru, lands, send, recv, token = _scatter_start(ps, f"rs_start_{s}")
        self.pending[s] = (thru, lands, send, recv)
        return (token,)

    def finish(self, after):
        out = {}
        for s in sorted(self.pending, reverse=True):
            ps, lands = _scatter_wait(*self.pending[s], after, f"rs_wait_{s}")
            for i, p, r2 in zip(SUBLAYER_SHARDS[s], ps, lands):
                out[i] = _add_chips(p, r2, self.pos, f"rs_add_chips_{i}")
        return out


def kernel(x, norm_g, ffn_w_in, ffn_w_out, conv_w_pw1, conv_b_pw1, conv_w_dw, conv_b_dw, conv_norm_g, conv_w_pw2, conv_b_pw2, attn_w_qkv, attn_q_norm, attn_k_norm, attn_w_o, loss_target, m_norm_g, m_ffn_w_in, m_ffn_w_out, m_conv_w_pw1, m_conv_b_pw1, m_conv_w_dw, m_conv_b_dw, m_conv_norm_g, m_conv_w_pw2, m_conv_b_pw2, m_attn_w_qkv, m_attn_q_norm, m_attn_k_norm, m_attn_w_o, v_norm_g, v_ffn_w_in, v_ffn_w_out, v_conv_w_pw1, v_conv_b_pw1, v_conv_w_dw, v_conv_b_dw, v_conv_norm_g, v_conv_w_pw2, v_conv_b_pw2, v_attn_w_qkv, v_attn_q_norm, v_attn_k_norm, v_attn_w_o):
    T, D = x.shape[1], x.shape[2]
    xi, yi, ci = _mesh_pos()
    me = 4 * xi + 2 * yi + ci
    pos = jnp.stack([ci, 2 * xi + yi]).astype(I32)
    lf = [(l, f) for l in range(2) for f in range(2)]

    shards = ([ffn_w_in[l, f].T.astype(BF16) for l, f in lf] + [ffn_w_out[l, f].astype(BF16) for l, f in lf]
              + [conv_w_pw1[0].T.astype(BF16), conv_w_pw2[0].astype(BF16),
                 attn_w_qkv[0].T.astype(BF16), attn_w_o[0].astype(BF16)])
    comm = _StepComm(shards, pos)
    w_cols = norm_g.shape[2]
    vec = _all_sum(_pad_rows(jnp.concatenate([_place_cols(norm_g.reshape(6, w_cols), me, D),
                                              _place_cols(conv_w_dw[0], me, D)]), 40), "gather_vectors")
    vecs = {
        "norm_g": vec[0:6], "w_dw": vec[6:6 + CONV_WIDTH],
        "b_pw1": conv_b_pw1.reshape(2, D), "b_dw": conv_b_dw, "cg": conv_norm_g, "b_pw2": conv_b_pw2,
        "q_norm": attn_q_norm[0], "k_norm": attn_k_norm[0],
    }

    loss_local, dx, small, tok = _local_step(x[0], loss_target[0], vecs, comm)
    loss = lax.psum(loss_local, ("x", "y", "c"))

    mats = comm.finish(dx)
    hd3 = 3 * HEAD_DIM
    packed = jnp.concatenate([
        small["norm_g"], small["b_pw1"], small["w_dw"], small["b_dw"], small["cg"], small["b_pw2"],
        jnp.pad(small["q_norm"].reshape(1, hd3), ((0, 0), (0, D - hd3))),
        jnp.pad(small["k_norm"].reshape(1, hd3), ((0, 0), (0, D - hd3)))])
    red = _all_sum(_pad_rows(packed, 48), "reduce_vectors")
    col0 = me * w_cols
    grads = {
        "norm_g": lax.dynamic_slice(red[0:6], (0, col0), (6, w_cols)),
        "ffn_w_in": jnp.concatenate([mats[i].T for i in range(4)]),
        "ffn_w_out": jnp.concatenate([mats[i] for i in range(4, 8)]),
        "conv_w_pw1": mats[8].T,
        "conv_b_pw1": red[6:8].reshape(1, 2 * D),
        "conv_w_dw": lax.dynamic_slice(red[8:8 + CONV_WIDTH], (0, col0), (CONV_WIDTH, w_cols)),
        "conv_b_dw": red[39:40], "conv_norm_g": red[40:41], "conv_w_pw2": mats[9], "conv_b_pw2": red[41:42],
        "attn_w_qkv": mats[10].T,
        "attn_q_norm": red[42, :hd3].reshape(3, HEAD_DIM), "attn_k_norm": red[43, :hd3].reshape(3, HEAD_DIM),
        "attn_w_o": mats[11],
    }

    names = ["norm_g", "ffn_w_in", "ffn_w_out", "conv_w_pw1", "conv_b_pw1", "conv_w_dw", "conv_b_dw",
             "conv_norm_g", "conv_w_pw2", "conv_b_pw2", "attn_w_qkv", "attn_q_norm", "attn_k_norm", "attn_w_o"]
    ws = [norm_g, ffn_w_in, ffn_w_out, conv_w_pw1, conv_b_pw1, conv_w_dw, conv_b_dw, conv_norm_g, conv_w_pw2,
          conv_b_pw2, attn_w_qkv, attn_q_norm, attn_k_norm, attn_w_o]
    ms = [m_norm_g, m_ffn_w_in, m_ffn_w_out, m_conv_w_pw1, m_conv_b_pw1, m_conv_w_dw, m_conv_b_dw, m_conv_norm_g,
          m_conv_w_pw2, m_conv_b_pw2, m_attn_w_qkv, m_attn_q_norm, m_attn_k_norm, m_attn_w_o]
    vs = [v_norm_g, v_ffn_w_in, v_ffn_w_out, v_conv_w_pw1, v_conv_b_pw1, v_conv_w_dw, v_conv_b_dw, v_conv_norm_g,
          v_conv_w_pw2, v_conv_b_pw2, v_attn_w_qkv, v_attn_q_norm, v_attn_k_norm, v_attn_w_o]
    g_out, d_out, m_out, v_out = [], [], [], []
    for name, w, m, v in zip(names, ws, ms, vs):
        shape2 = (-1, w.shape[-1]) if w.size > SMALL_PARAM else w.shape
        g2 = grads[name].reshape(shape2)
        delta, new_m, new_v = _adamw(w.reshape(shape2), g2, m.reshape(shape2), v.reshape(shape2), f"adamw_{name}")
        g_out.append(g2.reshape(w.shape))
        d_out.append(delta.reshape(w.shape))
        m_out.append(new_m.reshape(w.shape))
        v_out.append(new_v.reshape(w.shape))
    return (loss, dx.reshape(x.shape), *g_out, *d_out, *m_out, *v_out)
```

```python
import functools

import jax
import jax.numpy as jnp
from jax import lax
from jax.experimental import pallas as pl
from jax.experimental.pallas import tpu as pltpu

F32 = jnp.float32
BF16 = jnp.bfloat16
I32 = jnp.int32

NORM_EPS = 1e-6
LANES = 128
SUBLANES = 8
MXU_WIDTH = 256
HEAD_DIM = 128
ATTN_BLOCK = 128
ATTN_DILATIONS = (1, 4, 16)
CONV_WIDTH = 31
CONV_HALO = 32
CONV_CHUNK = 16
WHOLE_K_MAX = 6144
MM_TN_TOKENS = 2048
STREAM_TILE = 512
ADAM_LR, ADAM_B1, ADAM_B2, ADAM_EPS, ADAM_WD, ADAM_STEP = 0.001, 0.9, 0.999, 1e-08, 0.01, 10
N_DEV = 8
SMALL_PARAM = 8192
NEG_BIG = -1e30
V7X_VMEM_BYTES = 64 * 1024 * 1024
VMEM_LIMIT = V7X_VMEM_BYTES - 8 * 1024 * 1024
MESH = pl.DeviceIdType.MESH
ANY = pl.BlockSpec(memory_space=pl.ANY)


def _params(*sem):
    return pltpu.CompilerParams(dimension_semantics=sem or None, vmem_limit_bytes=VMEM_LIMIT)


def _dot_nn(a, b):
    return lax.dot_general(a, b, (((1,), (0,)), ((), ())), preferred_element_type=F32)


def _dot_nt(a, b):
    return lax.dot_general(a, b, (((1,), (1,)), ((), ())), preferred_element_type=F32)


def _dot_tn(a, b):
    return lax.dot_general(a, b, (((0,), (0,)), ((), ())), preferred_element_type=F32)


def _sigmoid(x):
    return 1.0 / (1.0 + jnp.exp(-x))


def _tile(n, pref):
    if n <= pref:
        return n
    for t in range(pref - pref % 128, 0, -128):
        if n % t == 0:
            return t
    return n


def _to_lane_chunks(dst, val):
    for c in range(dst.shape[0]):
        dst[c] = val[:, c * LANES:(c + 1) * LANES]


def _rows8(v):
    tm, c = v.shape
    return v.reshape(tm // 8, 8, c).sum(axis=0)


def _rmsnorm(x, g, name, after=()):
    T, D = x.shape
    tm = min(T, 512)

    def body(x_ref, g_ref, *rest):
        h_ref = rest[-1]
        xv = x_ref[...]
        r = lax.rsqrt(jnp.mean(xv * xv, axis=-1, keepdims=True) + NORM_EPS)
        h_ref[...] = (xv * r * g_ref[...]).astype(BF16)

    return pl.pallas_call(
        body, name=name, grid=(T // tm,),
        in_specs=[pl.BlockSpec((tm, D), lambda i: (i, 0)), pl.BlockSpec((1, D), lambda i: (0, 0))] + [ANY] * len(after),
        out_specs=pl.BlockSpec((tm, D), lambda i: (i, 0)),
        out_shape=jax.ShapeDtypeStruct((T, D), BF16),
        compiler_params=_params("parallel"),
    )(x, g, *after)


def _rmsnorm_streams(x, g, name):
    T, D = x.shape
    tm = min(T, STREAM_TILE)
    dils = [d for d in ATTN_DILATIONS if d > 1]

    def body(x_ref, g_ref, h_ref, *rest):
        outs, hs = rest[:-1], rest[-1]
        xv = x_ref[...]
        r = lax.rsqrt(jnp.mean(xv * xv, axis=-1, keepdims=True) + NORM_EPS)
        hf = xv * r * g_ref[...]
        h_ref[...] = hf.astype(BF16)
        _to_lane_chunks(hs, hf)
        for d, o_ref in zip(dils, outs):
            for s in range(d):
                for c in range(D // LANES):
                    o_ref[s, :, c * LANES:(c + 1) * LANES] = hs.at[c][pl.ds(s, tm // d, stride=d), :].astype(BF16)

    res = pl.pallas_call(
        body, name=name, grid=(T // tm,),
        in_specs=[pl.BlockSpec((tm, D), lambda i: (i, 0)), pl.BlockSpec((1, D), lambda i: (0, 0))],
        out_specs=[pl.BlockSpec((tm, D), lambda i: (i, 0))]
        + [pl.BlockSpec((d, tm // d, D), lambda i: (0, i, 0)) for d in dils],
        out_shape=[jax.ShapeDtypeStruct((T, D), BF16)] + [jax.ShapeDtypeStruct((d, T // d, D), BF16) for d in dils],
        scratch_shapes=[pltpu.VMEM((D // LANES, tm, LANES), F32)],
        compiler_params=_params("parallel"),
    )(x, g)
    return [res[0]] + [a.reshape(T, D) for a in res[1:]]


def _gated_in(h, wt, bias, act, u_dtype, name, after=()):
    T, D = h.shape
    F = wt.shape[0] // 2
    tn = _tile(F, 1408)
    tm = min(T, 1024)
    nn = F // tn

    def body(*refs):
        z_ref, u_ref = refs[-2:]
        if bias is None:
            h_ref, w0_ref, w1_ref = refs[:3]
        else:
            h_ref, w0_ref, w1_ref, b_ref = refs[:4]
        hv = h_ref[...]
        for c0 in range(0, tn, MXU_WIDTH):
            cs = slice(c0, min(c0 + MXU_WIDTH, tn))
            z0 = _dot_nt(hv, w0_ref[cs, :])
            z1 = _dot_nt(hv, w1_ref[cs, :])
            if bias is not None:
                z0 = z0 + b_ref[0:1, cs]
                z1 = z1 + b_ref[1:2, cs]
            z_ref[0, :, cs] = z0.astype(BF16)
            z_ref[1, :, cs] = z1.astype(BF16)
            if act == "swiglu":
                u = z0 * _sigmoid(z0) * z1
            else:
                u = z0 * _sigmoid(z1)
            u_ref[:, cs] = u.astype(u_dtype)

    in_specs = [pl.BlockSpec((tm, D), lambda n, m: (m, 0)),
                pl.BlockSpec((tn, D), lambda n, m: (n, 0)),
                pl.BlockSpec((tn, D), lambda n, m: (n + nn, 0))]
    args = [h, wt, wt]
    if bias is not None:
        in_specs.append(pl.BlockSpec((2, tn), lambda n, m: (0, n)))
        args.append(bias)
    in_specs += [ANY] * len(after)
    args += list(after)
    return pl.pallas_call(
        body, name=name, grid=(nn, T // tm), in_specs=in_specs,
        out_specs=[pl.BlockSpec((2, tm, tn), lambda n, m: (0, m, n)), pl.BlockSpec((tm, tn), lambda n, m: (m, n))],
        out_shape=[jax.ShapeDtypeStruct((2, T, F), BF16), jax.ShapeDtypeStruct((T, F), u_dtype)],
        compiler_params=_params("parallel", "parallel"),
    )(*args)


def _swiglu_bwd(dxo, wout, z3, scale, name, after=()):
    T, D = dxo.shape
    F = wout.shape[0]
    tn = _tile(F, 1408)
    tm = min(T, 512)

    def body(d_ref, w_ref, z_ref, *rest):
        dz_ref = rest[-1]
        dy = (d_ref[...] * scale).astype(BF16)
        da = _dot_nt(dy, w_ref[...])
        gate = z_ref[0].astype(F32)
        up = z_ref[1].astype(F32)
        sg = _sigmoid(gate)
        dz_ref[0] = (da * up * (sg * (1.0 + gate * (1.0 - sg)))).astype(BF16)
        dz_ref[1] = (da * gate * sg).astype(BF16)

    return pl.pallas_call(
        body, name=name, grid=(F // tn, T // tm),
        in_specs=[pl.BlockSpec((tm, D), lambda n, m: (m, 0)),
                  pl.BlockSpec((tn, D), lambda n, m: (n, 0)),
                  pl.BlockSpec((2, tm, tn), lambda n, m: (0, m, n))] + [ANY] * len(after),
        out_specs=pl.BlockSpec((2, tm, tn), lambda n, m: (0, m, n)),
        out_shape=jax.ShapeDtypeStruct((2, T, F), BF16),
        compiler_params=_params("parallel", "parallel"),
    )(dxo, wout, z3, *after)


def _mm_nt(a, wt, out_dtype, name, *, w_row0=0, n_rows=None, streams=1, after=()):
    T, K = a.shape
    N = wt.shape[0] if n_rows is None else n_rows
    L = T // streams
    rows = min(L, STREAM_TILE)
    sp = STREAM_TILE // rows
    mps = L // rows
    assert streams % sp == 0 and w_row0 % N == 0

    def body(a_ref, w_ref, *rest):
        o_ref = rest[-1]
        for s in range(sp):
            a_s = a_ref[s * rows:(s + 1) * rows, :].astype(BF16)
            for c0 in range(0, N, MXU_WIDTH):
                c1 = min(c0 + MXU_WIDTH, N)
                o_ref[:, s * N + c0:s * N + c1] = _dot_nt(a_s, w_ref[c0:c1, :]).astype(out_dtype)

    return pl.pallas_call(
        body, name=name, grid=(T // (rows * sp),),
        in_specs=[pl.BlockSpec((rows * sp, K), lambda m: (m, 0)), pl.BlockSpec((N, K), lambda m: (w_row0 // N, 0))]
        + [ANY] * len(after),
        out_specs=pl.BlockSpec((rows, sp * N), lambda m: (m % mps, m // mps)),
        out_shape=jax.ShapeDtypeStruct((L, streams * N), out_dtype),
        compiler_params=_params("parallel"),
    )(a, wt, *after)


def _mm_tn(a, b, b_scale, name, *, streams=1):
    if a.ndim == 3:
        J, T, F = a.shape
    elif streams > 1:
        J, T, F = 1, a.shape[0] * streams, a.shape[1] // streams
    else:
        (T, F), J = a.shape, 1
    N = b.shape[1]
    tmm = _tile(F, 1408)
    tk = min(T // streams, MM_TN_TOKENS if b.dtype == BF16 else MM_TN_TOKENS // 2)
    npj = F // tmm
    sp = max(1, MM_TN_TOKENS // tk) if streams > 1 and tk == T // streams else 1
    assert streams % sp == 0
    nk = T // (tk * sp)
    kps = nk // streams if sp == 1 else 1

    def body(*refs):
        a_refs, (b_ref, o_ref, acc_ref) = refs[:sp], refs[sp:]
        k = pl.program_id(1)
        part = None
        for s, a_ref in enumerate(a_refs):
            bv = b_ref[s * tk:(s + 1) * tk, :]
            if b_scale != 1.0:
                bv = bv * b_scale
            dot = _dot_tn(a_ref[...].astype(BF16), bv.astype(BF16))
            part = dot if part is None else part + dot

        @pl.when(k == 0)
        def _():
            acc_ref[...] = part

        @pl.when(k > 0)
        def _():
            acc_ref[...] += part

        @pl.when(k == nk - 1)
        def _():
            o_ref[...] = acc_ref[...].astype(BF16)

    if a.ndim == 3:
        a_specs = [pl.BlockSpec((None, tk, tmm), lambda i, k: (i // npj, k, i % npj))]
    elif sp > 1:
        a_specs = [pl.BlockSpec((tk, tmm), lambda i, k, s=s: (0, (k * sp + s) * npj + i)) for s in range(sp)]
    elif streams > 1:
        a_specs = [pl.BlockSpec((tk, tmm), lambda i, k: (k % kps, (k // kps) * npj + i))]
    else:
        a_specs = [pl.BlockSpec((tk, tmm), lambda i, k: (k, i))]
    return pl.pallas_call(
        body, name=name, grid=(J * npj, nk),
        in_specs=a_specs + [pl.BlockSpec((tk * sp, N), lambda i, k: (k, 0))],
        out_specs=pl.BlockSpec((tmm, N), lambda i, k: (i, 0)),
        out_shape=jax.ShapeDtypeStruct((J * F, N), BF16),
        scratch_shapes=[pltpu.VMEM((tmm, N), F32)],
        compiler_params=_params("parallel", "arbitrary"),
    )(*[a] * sp, b)


def _rows_mm(a, b, mode, name, *, x=None, scale=1.0, bias=None, tgt=None, g=None, dxo=None,
             b_row0=0, streams=1, extras=(), after=()):
    if a.ndim == 3:
        J, T, F = a.shape
    elif streams > 1:
        J, T, F = 1, a.shape[0] * streams, a.shape[1] // streams
    else:
        (T, F), J = a.shape, 1
    D = b.shape[1]
    if streams > 1 and T // streams < STREAM_TILE:
        L = T // streams
        sp = STREAM_TILE // L
        assert mode == "plain" and streams % sp == 0 and b_row0 % F == 0 and not after

        def body_short(a_ref, b_ref, o_ref):
            for s in range(sp):
                o_ref[s * L:(s + 1) * L, :] = _dot_nn(a_ref[:, s * F:(s + 1) * F], b_ref[...])

        return pl.pallas_call(
            body_short, name=name, grid=(streams // sp,),
            in_specs=[pl.BlockSpec((L, sp * F), lambda m: (0, m)), pl.BlockSpec((F, D), lambda m: (b_row0 // F, 0))],
            out_specs=pl.BlockSpec((sp * L, D), lambda m: (m, 0)),
            out_shape=jax.ShapeDtypeStruct((T, D), F32),
            compiler_params=_params("parallel"),
        )(a, b)
    tk = _tile(F, 3072)
    kpj = F // tk
    nk = J * kpj
    tm = min(T // streams, STREAM_TILE)
    whole_k = a.ndim == 3 and kpj == 1 and J > 1 and J * F <= WHOLE_K_MAX
    if whole_k:
        nk, tm = 1, min(T, STREAM_TILE // 2)
    nm = T // tm
    mps = nm // streams
    assert b_row0 % tk == 0
    kb0 = b_row0 // tk
    n_ex = len(extras)
    n_ab = J if whole_k else 1

    def body(*refs):
        refs = list(refs)
        a_refs, b_refs = refs[:n_ab], refs[n_ab:2 * n_ab]
        rest = refs[2 * n_ab + len(after):]
        if n_ex:
            tmp_ref = rest.pop()
        acc_ref = rest.pop()
        ex_refs, rest = rest[:n_ex], rest[n_ex:]
        m, k = pl.program_id(0), pl.program_id(1)
        part = _dot_nn(a_refs[0][...], b_refs[0][...])
        for a_ref, b_ref in zip(a_refs[1:], b_refs[1:]):
            part = part + _dot_nn(a_ref[...], b_ref[...])

        @pl.when(k == 0)
        def _():
            acc_ref[...] = part

        @pl.when(k > 0)
        def _():
            acc_ref[...] += part

        @pl.when(k == nk - 1)
        def _():
            acc = acc_ref[...]
            for (d, _), e_ref in zip(extras, ex_refs):
                for s in range(d):
                    for c in range(D // LANES):
                        tmp_ref.at[c][pl.ds(s, tm // d, stride=d), :] = e_ref[s, :, c * LANES:(c + 1) * LANES]
                acc = acc + jnp.concatenate([tmp_ref[c] for c in range(D // LANES)], axis=1)
            if mode == "plain":
                (o_ref,) = rest
                o_ref[...] = acc
            elif mode == "res":
                ins = list(rest)
                x_ref = ins.pop(0)
                if bias is not None:
                    acc = acc + ins.pop(0)[...]
                xn = x_ref[...] + scale * acc
                if g is None:
                    (o_ref,) = ins
                else:
                    g_ref, o_ref, h_ref = ins
                    r = lax.rsqrt(jnp.mean(xn * xn, axis=-1, keepdims=True) + NORM_EPS)
                    h_ref[...] = (xn * r * g_ref[...]).astype(BF16)
                o_ref[...] = xn
            elif mode == "loss":
                x_ref, t_ref, o_ref, l_ref = rest
                e = x_ref[...] + scale * acc - t_ref[...]
                o_ref[...] = e / D

                @pl.when(m == 0)
                def _():
                    l_ref[...] = jnp.zeros_like(l_ref)

                l_ref[...] += _rows8(e * e)
            else:
                x_ref, g_ref, d_ref, o_ref, dg_ref = rest
                xv = x_ref[...]
                r = lax.rsqrt(jnp.mean(xv * xv, axis=-1, keepdims=True) + NORM_EPS)
                xh = xv * r
                dxh = acc * g_ref[...]
                o_ref[...] = d_ref[...] + r * (dxh - xh * jnp.mean(dxh * xh, axis=-1, keepdims=True))

                @pl.when(m == 0)
                def _():
                    dg_ref[...] = jnp.zeros_like(dg_ref)

                dg_ref[...] += _rows8(acc * xh)

    if whole_k:
        a_specs = [pl.BlockSpec((None, tm, F), lambda m, k, j=j: (j, m, 0)) for j in range(J)]
        b_specs = [pl.BlockSpec((F, D), lambda m, k, j=j: (kb0 + j, 0)) for j in range(J)]
    else:
        if a.ndim == 3:
            a_specs = [pl.BlockSpec((None, tm, tk), lambda m, k: (k // kpj, m, k % kpj))]
        elif streams > 1:
            a_specs = [pl.BlockSpec((tm, tk), lambda m, k: (m % mps, (m // mps) * kpj + k))]
        else:
            a_specs = [pl.BlockSpec((tm, tk), lambda m, k: (m, k))]
        b_specs = [pl.BlockSpec((tk, D), lambda m, k: (kb0 + k, 0))]
    row = pl.BlockSpec((tm, D), lambda m, k: (m, 0))
    vec = pl.BlockSpec((1, D), lambda m, k: (0, 0))
    part8 = pl.BlockSpec((8, D), lambda m, k: (0, 0))
    in_specs = a_specs + b_specs + [ANY] * len(after)
    args = [a] * n_ab + [b] * n_ab + list(after)
    for d, e in extras:
        in_specs.append(pl.BlockSpec((d, tm // d, D), lambda m, k: (0, m, 0)))
        args.append(e.reshape(d, T // d, D))
    full = jax.ShapeDtypeStruct((T, D), F32)
    small = jax.ShapeDtypeStruct((8, D), F32)
    if mode == "plain":
        out_specs, out_shape = row, full
    elif mode == "res":
        in_specs.append(row)
        args.append(x)
        if bias is not None:
            in_specs.append(vec)
            args.append(bias)
        if g is None:
            out_specs, out_shape = row, full
        else:
            in_specs.append(vec)
            args.append(g)
            out_specs, out_shape = [row, row], [full, jax.ShapeDtypeStruct((T, D), BF16)]
    elif mode == "loss":
        in_specs += [row, row]
        args += [x, tgt]
        out_specs, out_shape = [row, part8], [full, small]
    else:
        in_specs += [row, vec, row]
        args += [x, g, dxo]
        out_specs, out_shape = [row, part8], [full, small]
    return pl.pallas_call(
        body, name=name, grid=(nm, nk), in_specs=in_specs, out_specs=out_specs, out_shape=out_shape,
        scratch_shapes=[pltpu.VMEM((tm, D), F32)] + ([pltpu.VMEM((D // LANES, tm, LANES), F32)] if n_ex else []),
        compiler_params=_params("arbitrary", "arbitrary"),
    )(*args)


def _sublane_phases(sh):
    rows = sh.shape[1] - SUBLANES
    for p in range(1, SUBLANES):
        sh[p, pl.ds(0, rows), :] = sh[0, pl.ds(p, rows), :]


def _shifted(sh, offset, j, rows=SUBLANES):
    start = pl.multiple_of(j * rows + (offset - offset % SUBLANES), SUBLANES)
    return sh[offset % SUBLANES, pl.ds(start, rows), :]


def _conv_taps(wb, sh, offsets, j, init):
    groups = CONV_CHUNK // SUBLANES
    accs = [init[g * SUBLANES:(g + 1) * SUBLANES] for g in range(groups)]
    for k, off in enumerate(offsets):
        w = wb[k]
        start = pl.multiple_of(j * CONV_CHUNK + (off - off % SUBLANES), SUBLANES)
        for g in range(groups):
            accs[g] = accs[g] + w * sh[off % SUBLANES, pl.ds(start + g * SUBLANES, SUBLANES), :]
    return jnp.concatenate(accs, axis=0)


def _broadcast_rows(dst, src):
    for r in range(dst.shape[0]):
        dst[r] = jnp.zeros(dst.shape[1:], F32) + src[r:r + 1, :]


def _dwconv_fwd(u, w, b_dw, cg, name):
    T, C = u.shape
    tm = min(T, 256)
    hb = tm // CONV_HALO
    pad = CONV_HALO - (CONV_WIDTH - 1)

    def body(u_ref, halo_ref, w_ref, b_ref, g_ref, v_ref, s_ref, sh, wb):
        i = pl.program_id(0)
        sh[0, pl.ds(0, CONV_HALO), :] = jnp.where(i > 0, halo_ref[...], 0.0)
        sh[0, pl.ds(CONV_HALO, tm), :] = u_ref[...]
        _sublane_phases(sh)
        _broadcast_rows(wb, w_ref)
        bias = jnp.zeros((CONV_CHUNK, C), F32) + b_ref[...]

        def chunk(j, carry):
            acc = _conv_taps(wb, sh, [pad + k for k in range(CONV_WIDTH)], j, bias)
            v_ref[pl.ds(pl.multiple_of(j * CONV_CHUNK, CONV_CHUNK), CONV_CHUNK), :] = acc
            return carry

        lax.fori_loop(0, tm // CONV_CHUNK, chunk, 0)
        acc = v_ref[...]
        r = lax.rsqrt(jnp.mean(acc * acc, axis=-1, keepdims=True) + NORM_EPS)
        n = acc * r * g_ref[...]
        s_ref[...] = (n * _sigmoid(n)).astype(BF16)

    row = pl.BlockSpec((tm, C), lambda i: (i, 0))
    vec = pl.BlockSpec((1, C), lambda i: (0, 0))
    return pl.pallas_call(
        body, name=name, grid=(T // tm,),
        in_specs=[row, pl.BlockSpec((CONV_HALO, C), lambda i: (jnp.maximum(i * hb - 1, 0), 0)),
                  pl.BlockSpec((CONV_WIDTH, C), lambda i: (0, 0)), vec, vec],
        out_specs=[row, row],
        out_shape=[jax.ShapeDtypeStruct((T, C), F32), jax.ShapeDtypeStruct((T, C), BF16)],
        scratch_shapes=[pltpu.VMEM((SUBLANES, CONV_HALO + tm, C), F32), pltpu.VMEM((CONV_WIDTH, SUBLANES, C), F32)],
        compiler_params=_params("parallel"),
    )(u, u, w, b_dw, cg)


def _conv_post_bwd(ds, v, cg, dxo, name):
    T, C = v.shape
    tm = min(T, 512)

    def body(ds_ref, v_ref, g_ref, d_ref, dv_ref, dcg_ref, dbdw_ref, dbpw2_ref):
        @pl.when(pl.program_id(0) == 0)
        def _():
            dcg_ref[...] = jnp.zeros_like(dcg_ref)
            dbdw_ref[...] = jnp.zeros_like(dbdw_ref)
            dbpw2_ref[...] = jnp.zeros_like(dbpw2_ref)

        vv = v_ref[...]
        r = lax.rsqrt(jnp.mean(vv * vv, axis=-1, keepdims=True) + NORM_EPS)
        vh = vv * r
        n = vh * g_ref[...]
        sg = _sigmoid(n)
        dn = ds_ref[...] * (sg * (1.0 + n * (1.0 - sg)))
        dvh = dn * g_ref[...]
        dv = r * (dvh - vh * jnp.mean(dvh * vh, axis=-1, keepdims=True))
        dv_ref[...] = dv
        dcg_ref[...] += _rows8(dn * vh)
        dbdw_ref[...] += _rows8(dv)
        dbpw2_ref[...] += _rows8(d_ref[...])

    row = pl.BlockSpec((tm, C), lambda i: (i, 0))
    part8 = pl.BlockSpec((8, C), lambda i: (0, 0))
    small = jax.ShapeDtypeStruct((8, C), F32)
    return pl.pallas_call(
        body, name=name, grid=(T // tm,),
        in_specs=[row, row, pl.BlockSpec((1, C), lambda i: (0, 0)), row],
        out_specs=[row, part8, part8, part8],
        out_shape=[jax.ShapeDtypeStruct((T, C), F32), small, small, small],
        compiler_params=_params("arbitrary"),
    )(ds, v, cg, dxo)


def _dwconv_bwd(dv, u, w, z3, name):
    T, C = u.shape
    tm = min(T, 256)
    hb = tm // CONV_HALO
    nt = T // tm
    pad = CONV_HALO - (CONV_WIDTH - 1)

    taps_at_once = 4
    chunks_at_once = 8

    def body(dv_ref, dvn_ref, u_ref, up_ref, w_ref, z_ref, dz_ref, dw_ref, db_ref, dvsh, ush, wb, du_ref, dwacc):
        i = pl.program_id(0)

        @pl.when(i == 0)
        def _():
            dwacc[...] = jnp.zeros_like(dwacc)
            db_ref[...] = jnp.zeros_like(db_ref)

        dvsh[0, pl.ds(0, tm), :] = dv_ref[...]
        dvsh[0, pl.ds(tm, CONV_HALO), :] = jnp.where(i < nt - 1, dvn_ref[...], 0.0)
        ush[0, pl.ds(0, CONV_HALO), :] = jnp.where(i > 0, up_ref[...], 0.0)
        ush[0, pl.ds(CONV_HALO, tm), :] = u_ref[...]
        _sublane_phases(dvsh)
        _sublane_phases(ush)
        _broadcast_rows(wb, w_ref)

        def du_chunk(j, carry):
            acc = _conv_taps(wb, dvsh, [CONV_WIDTH - 1 - k for k in range(CONV_WIDTH)], j,
                             jnp.zeros((CONV_CHUNK, C), F32))
            du_ref[pl.ds(pl.multiple_of(j * CONV_CHUNK, CONV_CHUNK), CONV_CHUNK), :] = acc
            return carry

        lax.fori_loop(0, tm // CONV_CHUNK, du_chunk, 0)

        for k0 in range(0, CONV_WIDTH, taps_at_once):
            taps = range(k0, min(k0 + taps_at_once, CONV_WIDTH))

            def dw_chunks(jj, carry, taps=taps):
                accs = [jnp.zeros((SUBLANES, C), F32) for _ in taps]
                for u_ in range(chunks_at_once):
                    j = jj * chunks_at_once + u_
                    dvc = dv_ref[pl.ds(pl.multiple_of(j * SUBLANES, SUBLANES), SUBLANES), :]
                    accs = [acc + dvc * _shifted(ush, pad + k, j) for acc, k in zip(accs, taps)]
                for k, acc in zip(taps, accs):
                    dwacc[k] += acc
                return carry

            lax.fori_loop(0, tm // (SUBLANES * chunks_at_once), dw_chunks, 0)

        du = du_ref[...]
        a = z_ref[0].astype(F32)
        gate = z_ref[1].astype(F32)
        sg = _sigmoid(gate)
        da = du * sg
        dgate = du * a * sg * (1.0 - sg)
        dz_ref[0] = da.astype(BF16)
        dz_ref[1] = dgate.astype(BF16)
        db_ref[0:8, :] += _rows8(da)
        db_ref[8:16, :] += _rows8(dgate)

        @pl.when(i == nt - 1)
        def _():
            dw_ref[...] = jnp.zeros_like(dw_ref)
            for k in range(CONV_WIDTH):
                dw_ref[k:k + 1, :] = jnp.sum(dwacc[k], axis=0, keepdims=True)

    row = pl.BlockSpec((tm, C), lambda i: (i, 0))
    last_halo = T // CONV_HALO - 1
    return pl.pallas_call(
        body, name=name, grid=(nt,),
        in_specs=[row, pl.BlockSpec((CONV_HALO, C), lambda i: (jnp.minimum((i + 1) * hb, last_halo), 0)),
                  row, pl.BlockSpec((CONV_HALO, C), lambda i: (jnp.maximum(i * hb - 1, 0), 0)),
                  pl.BlockSpec((CONV_WIDTH, C), lambda i: (0, 0)),
                  pl.BlockSpec((2, tm, C), lambda i: (0, i, 0))],
        out_specs=[pl.BlockSpec((2, tm, C), lambda i: (0, i, 0)),
                   pl.BlockSpec((CONV_HALO, C), lambda i: (0, 0)),
                   pl.BlockSpec((16, C), lambda i: (0, 0))],
        out_shape=[jax.ShapeDtypeStruct((2, T, C), BF16), jax.ShapeDtypeStruct((CONV_HALO, C), F32),
                   jax.ShapeDtypeStruct((16, C), F32)],
        scratch_shapes=[pltpu.VMEM((SUBLANES, tm + CONV_HALO, C), F32), pltpu.VMEM((SUBLANES, CONV_HALO + tm, C), F32),
                        pltpu.VMEM((CONV_WIDTH, SUBLANES, C), F32), pltpu.VMEM((tm, C), F32),
                        pltpu.VMEM((CONV_WIDTH, SUBLANES, C), F32)],
        compiler_params=_params("arbitrary"),
    )(dv, dv, u, u, w, z3)


def _head_norm(raw, gain):
    xv = raw.astype(F32)
    r = lax.rsqrt(jnp.mean(xv * xv, axis=-1, keepdims=True) + NORM_EPS)
    xh = xv * r
    return (xh * gain).astype(BF16), xh, r


def _band_mask(n):
    row = lax.broadcasted_iota(I32, (ATTN_BLOCK, 2 * ATTN_BLOCK), 0)
    col = lax.broadcasted_iota(I32, (ATTN_BLOCK, 2 * ATTN_BLOCK), 1)
    return (col >= row) & (col <= row + ATTN_BLOCK) & ((col >= ATTN_BLOCK) | (n > 0))


def _carry_previous(n, bufs):
    @pl.when(n == 0)
    def _():
        for b in bufs:
            b[0:ATTN_BLOCK, :] = jnp.zeros((ATTN_BLOCK, b.shape[1]), b.dtype)

    @pl.when(n > 0)
    def _():
        for b in bufs:
            b[0:ATTN_BLOCK, :] = b[ATTN_BLOCK:, :]


def _attn_fwd(qkv, qg, kg, g, name):
    d = ATTN_DILATIONS[g]
    L = qkv.shape[0]
    HD = qkv.shape[1] // (3 * d)
    H = HD // HEAD_DIM
    nb = L // ATTN_BLOCK
    scale = HEAD_DIM ** -0.5

    def body(q_ref, kc_ref, vc_ref, qg_ref, kg_ref, o_ref, lse_ref, kk, vv):
        n = pl.program_id(1)
        _carry_previous(n, [kk, vv])
        mask = _band_mask(n)
        lane = lax.broadcasted_iota(I32, (ATTN_BLOCK, 128), 1)
        gq = qg_ref[g:g + 1, :]
        gk = kg_ref[g:g + 1, :]
        vv[ATTN_BLOCK:, :] = vc_ref[...]
        lse_tile = jnp.zeros((ATTN_BLOCK, 128), F32)
        for h in range(H):
            sl = slice(h * HEAD_DIM, (h + 1) * HEAD_DIM)
            q = _head_norm(q_ref[:, sl], gq)[0]
            kk[ATTN_BLOCK:, sl] = _head_norm(kc_ref[:, sl], gk)[0]
            s = jnp.where(mask, _dot_nt(q, kk[:, sl]) * scale, NEG_BIG)
            m = jnp.max(s, axis=-1, keepdims=True)
            p = jnp.exp(s - m)
            l = jnp.sum(p, axis=-1, keepdims=True)
            o_ref[:, sl] = _dot_nn(p.astype(BF16), vv[:, sl]) / l
            lse_tile = jnp.where(lane == h, m + jnp.log(l), lse_tile)
        lse_ref[...] = lse_tile

    def blk(which):
        return pl.BlockSpec((ATTN_BLOCK, HD), lambda r, n: (n, r * 3 + which))

    gains = pl.BlockSpec((3, HEAD_DIM), lambda r, n: (0, 0))
    return pl.pallas_call(
        body, name=name, grid=(d, nb),
        in_specs=[blk(0), blk(1), blk(2), gains, gains],
        out_specs=[pl.BlockSpec((ATTN_BLOCK, HD), lambda r, n: (n, r)),
                   pl.BlockSpec((ATTN_BLOCK, 128), lambda r, n: (n, r))],
        out_shape=[jax.ShapeDtypeStruct((L, d * HD), F32), jax.ShapeDtypeStruct((L, d * 128), F32)],
        scratch_shapes=[pltpu.VMEM((2 * ATTN_BLOCK, HD), BF16), pltpu.VMEM((2 * ATTN_BLOCK, HD), BF16)],
        compiler_params=_params("arbitrary", "arbitrary"),
    )(qkv, qkv, qkv, qg, kg)


def _attn_merge(os, lses, name):
    T = os[0].shape[0]
    HD = os[0].shape[1]
    H = HD // HEAD_DIM
    tm = min(T, STREAM_TILE)
    dils = ATTN_DILATIONS[1:]

    def body(o0, o1, o2, l0, l1, l2, ob_ref, of_ref, lse_ref, n1, n2, m1, m2):
        for d, src, dst, w in ((dils[0], o1, n1, HD), (dils[1], o2, n2, HD), (dils[0], l1, m1, 128), (dils[1], l2, m2, 128)):
            for s in range(d):
                for c in range(w // LANES):
                    dst.at[c][pl.ds(s, tm // d, stride=d), :] = src[:, s * w + c * LANES:s * w + (c + 1) * LANES]
        ls = [l0[...], m1[0], m2[0]]
        m = jnp.maximum(jnp.maximum(ls[0], ls[1]), ls[2])
        tot = m + jnp.log(jnp.exp(ls[0] - m) + jnp.exp(ls[1] - m) + jnp.exp(ls[2] - m))
        lse_ref[...] = tot
        ws = [jnp.exp(l - tot) for l in ls]
        for h in range(H):
            sl = slice(h * HEAD_DIM, (h + 1) * HEAD_DIM)
            out = (ws[0][:, h:h + 1] * o0[:, sl] + ws[1][:, h:h + 1] * n1[h]) + ws[2][:, h:h + 1] * n2[h]
            of_ref[:, sl] = out
            ob_ref[:, sl] = out.astype(BF16)

    def wide(d, w):
        return pl.BlockSpec((tm // d, d * w), lambda i: (i, 0))

    row = pl.BlockSpec((tm, HD), lambda i: (i, 0))
    nar = pl.BlockSpec((tm, 128), lambda i: (i, 0))
    return pl.pallas_call(
        body, name=name, grid=(T // tm,),
        in_specs=[row, wide(dils[0], HD), wide(dils[1], HD), nar, wide(dils[0], 128), wide(dils[1], 128)],
        out_specs=[row, row, nar],
        out_shape=[jax.ShapeDtypeStruct((T, HD), BF16), jax.ShapeDtypeStruct((T, HD), F32),
                   jax.ShapeDtypeStruct((T, 128), F32)],
        scratch_shapes=[pltpu.VMEM((H, tm, HEAD_DIM), F32), pltpu.VMEM((H, tm, HEAD_DIM), F32),
                        pltpu.VMEM((1, tm, 128), F32), pltpu.VMEM((1, tm, 128), F32)],
        compiler_params=_params("parallel"),
    )(*os, *lses)


def _attn_delta(out, dout, lse, name):
    T, HD = out.shape
    H = HD // HEAD_DIM
    tm = min(T, STREAM_TILE)
    dils = ATTN_DILATIONS[1:]

    def body(o_ref, d_ref, l_ref, dl_ref, *rest):
        dch = rest[-1]
        lane = lax.broadcasted_iota(I32, (tm, 128), 1)
        tile = jnp.zeros((tm, 128), F32)
        for h in range(H):
            sl = slice(h * HEAD_DIM, (h + 1) * HEAD_DIM)
            dch[h] = d_ref[:, sl]
            tile = jnp.where(lane == h, jnp.sum(o_ref[:, sl] * d_ref[:, sl], axis=-1, keepdims=True), tile)
        dl_ref[...] = tile
        for i, d in enumerate(dils):
            dw_ref, lw_ref, dlw_ref = rest[3 * i:3 * i + 3]
            for s in range(d):
                rows = pl.ds(s, tm // d, stride=d)
                for h in range(H):
                    dw_ref[:, s * HD + h * HEAD_DIM:s * HD + (h + 1) * HEAD_DIM] = dch.at[h][rows, :].astype(BF16)
                lw_ref[:, s * 128:(s + 1) * 128] = l_ref[rows, :]
                dlw_ref[:, s * 128:(s + 1) * 128] = dl_ref[rows, :]

    row = pl.BlockSpec((tm, HD), lambda i: (i, 0))
    nar = pl.BlockSpec((tm, 128), lambda i: (i, 0))
    out_specs, out_shape = [nar], [jax.ShapeDtypeStruct((T, 128), F32)]
    for d in dils:
        out_specs += [pl.BlockSpec((tm // d, d * w), lambda i: (i, 0)) for w in (HD, 128, 128)]
        out_shape += [jax.ShapeDtypeStruct((T // d, d * HD), BF16), jax.ShapeDtypeStruct((T // d, d * 128), F32),
                      jax.ShapeDtypeStruct((T // d, d * 128), F32)]
    return pl.pallas_call(
        body, name=name, grid=(T // tm,), in_specs=[row, row, nar], out_specs=out_specs, out_shape=out_shape,
        scratch_shapes=[pltpu.VMEM((H, tm, HEAD_DIM), F32)],
        compiler_params=_params("parallel"),
    )(out, dout, lse)


def _attn_bwd(qkv, dout, lse, delta, qg, kg, g, name):
    d = ATTN_DILATIONS[g]
    L = qkv.shape[0]
    HD = qkv.shape[1] // (3 * d)
    H = HD // HEAD_DIM
    nb = L // ATTN_BLOCK
    total = d * nb
    scale = HEAD_DIM ** -0.5

    def body(*refs):
        (q_ref, kc_ref, vc_ref, do_ref, lse_ref, dl_ref, qg_ref, kg_ref,
         out_ref, gg_ref, dq_a, dk_a, dv_a, dq_b, dk_b, dv_b, kk, vv, khh, rkk) = refs
        t = pl.program_id(0)
        n = t % nb
        gq = qg_ref[g:g + 1, :]
        gk = kg_ref[g:g + 1, :]
        _carry_previous(t, [kk, vv, khh, rkk])

        @pl.when(t == 0)
        def _():
            gg_ref[...] = jnp.zeros_like(gg_ref)
            dk_a[...] = jnp.zeros_like(dk_a)
            dv_a[...] = jnp.zeros_like(dv_a)

        @pl.when(t < total)
        def _():
            mask = _band_mask(n)
            vv[ATTN_BLOCK:, :] = vc_ref[...]
            gq_sum = jnp.zeros((1, HEAD_DIM), F32)
            for h in range(H):
                sl = slice(h * HEAD_DIM, (h + 1) * HEAD_DIM)
                qn, qh, rq = _head_norm(q_ref[:, sl], gq)
                kcn, kch, rkc = _head_norm(kc_ref[:, sl], gk)
                kk[ATTN_BLOCK:, sl] = kcn
                khh[ATTN_BLOCK:, sl] = kch
                rkk[ATTN_BLOCK:, sl] = jnp.broadcast_to(rkc, (ATTN_BLOCK, HEAD_DIM))
                kn = kk[:, sl]
                p = jnp.where(mask, jnp.exp(_dot_nt(qn, kn) * scale - lse_ref[:, h:h + 1]), 0.0)
                do = do_ref[:, sl].astype(BF16)
                ds = (p * (_dot_nt(do, vv[:, sl]) - dl_ref[:, h:h + 1]) * scale).astype(BF16)
                dqn = _dot_nn(ds, kn)
                dqh = dqn * gq
                dq_b[:, sl] = rq * (dqh - qh * jnp.mean(dqh * qh, axis=-1, keepdims=True))
                gq_sum = gq_sum + jnp.sum(dqn * qh, axis=0, keepdims=True)
                dkn = _dot_tn(ds, qn)
                dvn = _dot_tn(p.astype(BF16), do)
                dk_a[:, sl] += dkn[0:ATTN_BLOCK]
                dv_a[:, sl] += dvn[0:ATTN_BLOCK]
                dk_b[:, sl] = dkn[ATTN_BLOCK:]
                dv_b[:, sl] = dvn[ATTN_BLOCK:]
            gg_ref[0:1, :] += gq_sum

        @pl.when(t > 0)
        def _():
            out_ref[:, 0:HD] = dq_a[...].astype(BF16)
            gk_sum = jnp.zeros((1, HEAD_DIM), F32)
            for h in range(H):
                sl = slice(h * HEAD_DIM, (h + 1) * HEAD_DIM)
                kh = khh[0:ATTN_BLOCK, sl]
                rk = rkk[0:ATTN_BLOCK, sl]
                dkn = dk_a[:, sl]
                dkh = dkn * gk
                dk = rk * (dkh - kh * jnp.mean(dkh * kh, axis=-1, keepdims=True))
                out_ref[:, HD + h * HEAD_DIM:HD + (h + 1) * HEAD_DIM] = dk.astype(BF16)
                gk_sum = gk_sum + jnp.sum(dkn * kh, axis=0, keepdims=True)
            out_ref[:, 2 * HD:3 * HD] = dv_a[...].astype(BF16)
            gg_ref[1:2, :] += gk_sum

        @pl.when(t < total)
        def _():
            dq_a[...] = dq_b[...]
            dk_a[...] = dk_b[...]
            dv_a[...] = dv_b[...]

    def at(t):
        t = jnp.minimum(t, total - 1)
        return t % nb, t // nb

    def blk(which):
        return pl.BlockSpec((ATTN_BLOCK, HD), lambda t: (at(t)[0], at(t)[1] * 3 + which))

    gains = pl.BlockSpec((3, HEAD_DIM), lambda t: (0, 0))
    nar = pl.BlockSpec((ATTN_BLOCK, 128), lambda t: at(t))
    acc = pltpu.VMEM((ATTN_BLOCK, HD), F32)
    pair16 = pltpu.VMEM((2 * ATTN_BLOCK, HD), BF16)
    pair32 = pltpu.VMEM((2 * ATTN_BLOCK, HD), F32)
    return pl.pallas_call(
        body, name=name, grid=(total + 1,),
        in_specs=[blk(0), blk(1), blk(2), pl.BlockSpec((ATTN_BLOCK, HD), lambda t: at(t)), nar, nar, gains, gains],
        out_specs=[pl.BlockSpec((ATTN_BLOCK, 3 * HD), lambda t: at(jnp.maximum(t - 1, 0))),
                   pl.BlockSpec((8, HEAD_DIM), lambda t: (0, 0))],
        out_shape=[jax.ShapeDtypeStruct((L, d * 3 * HD), BF16), jax.ShapeDtypeStruct((8, HEAD_DIM), F32)],
        scratch_shapes=[acc, acc, acc, acc, acc, acc, pair16, pair16, pair32, pair32],
        compiler_params=_params("arbitrary"),
    )(qkv, qkv, qkv, dout, lse, delta, qg, kg)


def _adamw(w, grad, m, v, name):
    if w.ndim == 2:
        R, C = w.shape
        tr = _row_tile(R, 512)
        blk, steps = pl.BlockSpec((tr, C), lambda i: (i, 0)), R // tr
    else:
        zeros = (0,) * w.ndim
        blk, steps = pl.BlockSpec(w.shape, lambda i: zeros), 1

    def body(w_ref, g_ref, m_ref, v_ref, d_ref, nm_ref, nv_ref):
        gv = g_ref[...]
        nm = ADAM_B1 * m_ref[...] + (1.0 - ADAM_B1) * gv
        nv = ADAM_B2 * v_ref[...] + (1.0 - ADAM_B2) * (gv * gv)
        m_hat = nm / (1.0 - ADAM_B1 ** ADAM_STEP)
        v_hat = nv / (1.0 - ADAM_B2 ** ADAM_STEP)
        d_ref[...] = -ADAM_LR * (m_hat / (jnp.sqrt(v_hat) + ADAM_EPS) + ADAM_WD * w_ref[...])
        nm_ref[...] = nm
        nv_ref[...] = nv

    shp = jax.ShapeDtypeStruct(w.shape, F32)
    return pl.pallas_call(
        body, name=name, grid=(steps,), in_specs=[blk] * 4, out_specs=[blk] * 3, out_shape=[shp] * 3,
        compiler_params=_params("parallel"),
    )(w, grad, m, v)


def _ffn_fwd(x, h, comm, s, tag, tgt=None, next_g=None):
    (win_t, wout), tok = comm.weights(s, h)
    z3, a = _gated_in(h, win_t, None, "swiglu", BF16, f"ffn_in_{tag}", after=tok)
    saved = (x, h, z3, a, win_t, wout)
    if tgt is None:
        return _rows_mm(a, wout, "res", f"ffn_out_{tag}", x=x, scale=0.5, g=next_g), saved
    return _rows_mm(a, wout, "loss", f"ffn_out_loss_{tag}", x=x, scale=0.5, tgt=tgt), saved


def _ffn_bwd(dxo, saved, g, comm, s, tag, after):
    x, h, z3, a, win_t, wout = saved
    dz3 = _swiglu_bwd(dxo, wout, z3, 0.5, f"ffn_out_bwd_{tag}", after=after)
    d_wout = _mm_tn(a, dxo, 0.5, f"ffn_dwout_{tag}")
    d_win_t = _mm_tn(dz3, h, 1.0, f"ffn_dwin_{tag}")
    tok = comm.grads(s, [d_win_t, d_wout])
    dx, dg8 = _rows_mm(dz3, win_t, "rmsbwd", f"ffn_in_bwd_{tag}", x=x, g=g, dxo=dxo, after=tok)
    return dx, dg8, ()


def _local_step(x, tgt, vecs, comm):
    norm_g = vecs["norm_g"]

    def gvec(i):
        return norm_g[i:i + 1]

    h0 = _rmsnorm(x, gvec(0), "rmsnorm_l0a", after=comm.started)
    (x1, h_c), sv_a0 = _ffn_fwd(x, h0, comm, 0, "l0a", next_g=gvec(1))
    (pw1_t, pw2), tok = comm.weights(1, h_c)
    zc3, u = _gated_in(h_c, pw1_t, vecs["b_pw1"], "glu", F32, "conv_pw1", after=tok)
    v, s = _dwconv_fwd(u, vecs["w_dw"], vecs["b_dw"], vecs["cg"], "conv_dw")
    x2, h_b0 = _rows_mm(s, pw2, "res", "conv_pw2", x=x1, bias=vecs["b_pw2"], g=gvec(2))
    (x3, h_a1), sv_b0 = _ffn_fwd(x2, h_b0, comm, 2, "l0b", next_g=gvec(3))
    x4, sv_a1 = _ffn_fwd(x3, h_a1, comm, 3, "l1a")
    h_s = _rmsnorm_streams(x4, gvec(4), "rmsnorm_attn")
    (qkv_t, wo), tok = comm.weights(4, h_s[0])
    hd3 = qkv_t.shape[0] // 3
    qkvs, os, lses = [], [], []
    for g, d in enumerate(ATTN_DILATIONS):
        qkv = _mm_nt(h_s[g], qkv_t, BF16, f"attn_qkv_g{g}", w_row0=g * hd3, n_rows=hd3, streams=d, after=tok)
        o, lse = _attn_fwd(qkv, vecs["q_norm"], vecs["k_norm"], g, f"attn_fwd_g{g}")
        qkvs.append(qkv)
        os.append(o)
        lses.append(lse)
    out_b, out_f, lse_tot = _attn_merge(os, lses, "attn_merge")
    x5, h_b1 = _rows_mm(out_b, wo, "res", "attn_wo", x=x4, g=gvec(5))
    (dy, loss8), sv_b1 = _ffn_fwd(x5, h_b1, comm, 5, "l1b", tgt=tgt)
    loss = 0.5 * jnp.sum(loss8) / x.shape[1]

    dg = [None] * 6
    dx, dg[5], tok = _ffn_bwd(dy, sv_b1, gvec(5), comm, 5, "l1b", ())
    dout = _mm_nt(dx, wo, F32, "attn_wo_bwd", after=tok)
    d_wo = _mm_tn(out_b, dx, 1.0, "attn_dwo")
    delta, *wide = _attn_delta(out_f, dout, lse_tot, "attn_delta")
    per_group = [(dout, lse_tot, delta), tuple(wide[0:3]), tuple(wide[3:6])]
    ggs, d_qkv_t, dhs = [], [], []
    for g, d in enumerate(ATTN_DILATIONS):
        dqkv, gg = _attn_bwd(qkvs[g], *per_group[g], vecs["q_norm"], vecs["k_norm"], g, f"attn_bwd_g{g}")
        ggs.append(gg)
        d_qkv_t.append(_mm_tn(dqkv, h_s[g], 1.0, f"attn_dwqkv_g{g}", streams=d))
        if d > 1:
            dhs.append((d, _rows_mm(dqkv, qkv_t, "plain", f"attn_qkv_bwd_g{g}", b_row0=g * hd3, streams=d)))
        else:
            dqkv0 = dqkv
    tok = comm.grads(4, [jnp.concatenate(d_qkv_t), d_wo])
    dx, dg[4] = _rows_mm(dqkv0, qkv_t, "rmsbwd", "attn_qkv_bwd_g0", x=x4, g=gvec(4), dxo=dx, extras=dhs, after=tok)
    dx, dg[3], tok = _ffn_bwd(dx, sv_a1, gvec(3), comm, 3, "l1a", ())
    dx, dg[2], tok = _ffn_bwd(dx, sv_b0, gvec(2), comm, 2, "l0b", tok)
    ds = _mm_nt(dx, pw2, F32, "conv_pw2_bwd", after=tok)
    d_pw2 = _mm_tn(s, dx, 1.0, "conv_dwpw2")
    dv, dcg8, dbdw8, dbpw2_8 = _conv_post_bwd(ds, v, vecs["cg"], dx, "conv_post_bwd")
    dzc3, d_wdw, db1_16 = _dwconv_bwd(dv, u, vecs["w_dw"], zc3, "conv_dw_bwd")
    d_pw1_t = _mm_tn(dzc3, h_c, 1.0, "conv_dwpw1")
    tok = comm.grads(1, [d_pw1_t, d_pw2])
    dx, dg[1] = _rows_mm(dzc3, pw1_t, "rmsbwd", "conv_pw1_bwd", x=x1, g=gvec(1), dxo=dx, after=tok)
    dx, dg[0], tok = _ffn_bwd(dx, sv_a0, gvec(0), comm, 0, "l0a", ())

    D = x.shape[1]
    small = {
        "norm_g": jnp.stack([p.sum(axis=0) for p in dg]),
        "b_pw1": db1_16.reshape(2, 8, D).sum(axis=1),
        "w_dw": d_wdw[:CONV_WIDTH],
        "b_dw": dbdw8.sum(axis=0, keepdims=True),
        "cg": dcg8.sum(axis=0, keepdims=True),
        "b_pw2": dbpw2_8.sum(axis=0, keepdims=True),
        "q_norm": jnp.stack([gg[0] for gg in ggs]),
        "k_norm": jnp.stack([gg[1] for gg in ggs]),
    }
    return loss, dx, small, tok


def _mesh_pos():
    return lax.axis_index("x"), lax.axis_index("y"), lax.axis_index("c")


def _other_chips(x, y):
    return [(1 - x, y), (x, 1 - y), (1 - x, 1 - y)]


HBM = pl.BlockSpec(memory_space=pltpu.HBM)
SEM = pl.BlockSpec(memory_space=pltpu.SEMAPHORE)
SPLIT_COPY = pltpu.SideEffectType.DATAFLOW_SIDE_EFFECTING


def _hbm(a):
    return pltpu.with_memory_space_constraint(a, pltpu.HBM)


def _gather_copies(j, src, out, send, recv, x, y, c):
    me = 4 * x + 2 * y + c
    peers = [(x, y, 1 - c)] + [(px, py, c) for px, py in _other_chips(x, y)]
    return [pltpu.make_async_remote_copy(src_ref=src, dst_ref=out.at[me], send_sem=send.at[4 * j + k],
                                         recv_sem=recv.at[4 * j + k], device_id=peer, device_id_type=MESH)
            for k, peer in enumerate(peers)]


def _gather_start(shards, me, after, name):
    n = len(shards)
    lands = [lax.dynamic_update_slice(lax.empty((N_DEV,) + s.shape, s.dtype), s[None], (me, 0, 0)) for s in shards]

    def body(*refs):
        ins, outs = refs[:n], refs[2 * n + len(after):3 * n + len(after)]
        send, recv, token = refs[4 * n + len(after):]
        x, y, c = _mesh_pos()
        for j in range(n):
            for cp in _gather_copies(j, ins[j], outs[j], send, recv, x, y, c):
                cp.start()
        token[...] = jnp.zeros_like(token)

    res = pl.pallas_call(
        body, name=name, in_specs=[HBM] * (2 * n) + [ANY] * len(after),
        out_specs=[HBM] * (2 * n) + [SEM, SEM, pl.BlockSpec(memory_space=pltpu.VMEM)],
        out_shape=[pltpu.HBM((N_DEV,) + s.shape, s.dtype) for s in shards] + [pltpu.HBM(s.shape, s.dtype) for s in shards]
        + [pltpu.SemaphoreType.DMA((4 * n,)), pltpu.SemaphoreType.DMA((4 * n,)), jax.ShapeDtypeStruct((8, 128), F32)],
        input_output_aliases={**{i: n + i for i in range(n)}, **{n + i: i for i in range(n)}},
        compiler_params=pltpu.CompilerParams(has_side_effects=SPLIT_COPY),
    )(*[_hbm(s) for s in shards], *[_hbm(l) for l in lands], *after)
    return res[:n], res[n:2 * n], res[2 * n], res[2 * n + 1], res[2 * n + 2]


def _gather_wait(outs, thru, send, recv, after, name):
    n = len(outs)

    def body(*refs):
        out_refs, thru_refs, send_ref, recv_ref = refs[:n], refs[n:2 * n], refs[2 * n], refs[2 * n + 1]
        x, y, c = _mesh_pos()
        for j in range(n):
            for cp in _gather_copies(j, thru_refs[j], out_refs[j], send_ref, recv_ref, x, y, c):
                cp.wait_send()
                cp.wait_recv()

    res = pl.pallas_call(
        body, name=name, in_specs=[HBM] * (2 * n) + [SEM, SEM] + [ANY] * len(after), out_specs=[HBM] * (2 * n),
        out_shape=[pltpu.HBM(a.shape, a.dtype) for a in list(outs) + list(thru)],
        input_output_aliases={i: i for i in range(2 * n)},
        compiler_params=pltpu.CompilerParams(has_side_effects=SPLIT_COPY),
    )(*outs, *thru, send, recv, *after)
    return res[:n]


def _gather_forward(outs, name):
    n = len(outs)

    def body(*refs):
        o = refs[n:2 * n]
        send_sems, recv_sems = refs[2 * n:]
        x, y, c = _mesh_pos()
        copies = []
        for j in range(n):
            for k, (px, py) in enumerate(_other_chips(x, y)):
                blk = o[j].at[4 * px + 2 * py + c]
                cp = pltpu.make_async_remote_copy(src_ref=blk, dst_ref=blk, send_sem=send_sems.at[j, k],
                                                  recv_sem=recv_sems.at[j, k], device_id=(x, y, 1 - c), device_id_type=MESH)
                cp.start()
                copies.append(cp)
        for cp in copies:
            cp.wait()

    return pl.pallas_call(
        body, name=name, in_specs=[ANY] * n, out_specs=[ANY] * n,
        out_shape=[jax.ShapeDtypeStruct(a.shape, a.dtype) for a in outs],
        input_output_aliases={i: i for i in range(n)},
        scratch_shapes=[pltpu.SemaphoreType.DMA((n, 3)), pltpu.SemaphoreType.DMA((n, 3))],
    )(*outs)


def _all_sum(p, name):
    R, C = p.shape

    def body(p_ref, o_ref, buf, send_sems, recv_sems):
        x, y, c = _mesh_pos()
        me = 4 * x + 2 * y + c
        buf[me] = p_ref[...]
        copies = []
        for rel in range(1, N_DEV):
            peer = (1 - x if rel & 4 else x, 1 - y if rel & 2 else y, 1 - c if rel & 1 else c)
            cp = pltpu.make_async_remote_copy(
                src_ref=p_ref, dst_ref=buf.at[me], send_sem=send_sems.at[rel - 1], recv_sem=recv_sems.at[rel - 1],
                device_id=peer, device_id_type=MESH)
            cp.start()
            copies.append(cp)
        for cp in copies:
            cp.wait()
        acc = buf[0]
        for dev in range(1, N_DEV):
            acc = acc + buf[dev]
        o_ref[...] = acc

    vm = pl.BlockSpec(memory_space=pltpu.VMEM)
    return pl.pallas_call(
        body, name=name, in_specs=[vm], out_specs=vm, out_shape=jax.ShapeDtypeStruct((R, C), F32),
        scratch_shapes=[pltpu.VMEM((N_DEV, R, C), F32), pltpu.SemaphoreType.DMA((N_DEV - 1,)),
                        pltpu.SemaphoreType.DMA((N_DEV - 1,))],
    )(p)


def _swap_with_sibling(gs, name):
    n = len(gs)

    def body(*refs):
        ins, outs = refs[:n], refs[n:2 * n]
        send_sems, recv_sems = refs[2 * n:]
        x, y, c = _mesh_pos()
        copies = []
        for i in range(n):
            for k in range(4):
                cp = pltpu.make_async_remote_copy(
                    src_ref=ins[i].at[2 * k + (1 - c)], dst_ref=outs[i].at[k],
                    send_sem=send_sems.at[i, k], recv_sem=recv_sems.at[i, k],
                    device_id=(x, y, 1 - c), device_id_type=MESH)
                cp.start()
                copies.append(cp)
        for cp in copies:
            cp.wait()

    return pl.pallas_call(
        body, name=name, in_specs=[ANY] * n, out_specs=[ANY] * n,
        out_shape=[jax.ShapeDtypeStruct((4,) + g.shape[1:], g.dtype) for g in gs],
        scratch_shapes=[pltpu.SemaphoreType.DMA((n, 4)), pltpu.SemaphoreType.DMA((n, 4))],
    )(*gs)


def _scatter_copies(j, src, land, send, recv, x, y, c):
    return [pltpu.make_async_remote_copy(src_ref=src.at[2 * px + py], dst_ref=land.at[k], send_sem=send.at[3 * j + k],
                                         recv_sem=recv.at[3 * j + k], device_id=(px, py, c), device_id_type=MESH)
            for k, (px, py) in enumerate(_other_chips(x, y))]


def _scatter_start(ps, name):
    n = len(ps)

    def body(*refs):
        ins, lands = refs[:n], refs[2 * n:3 * n]
        send, recv, token = refs[3 * n:]
        x, y, c = _mesh_pos()
        for j in range(n):
            for cp in _scatter_copies(j, ins[j], lands[j], send, recv, x, y, c):
                cp.start()
        token[...] = jnp.zeros_like(token)

    res = pl.pallas_call(
        body, name=name, in_specs=[HBM] * n,
        out_specs=[HBM] * (2 * n) + [SEM, SEM, pl.BlockSpec(memory_space=pltpu.VMEM)],
        out_shape=[pltpu.HBM(p.shape, p.dtype) for p in ps] + [pltpu.HBM((3,) + p.shape[1:], p.dtype) for p in ps]
        + [pltpu.SemaphoreType.DMA((3 * n,)), pltpu.SemaphoreType.DMA((3 * n,)), jax.ShapeDtypeStruct((8, 128), F32)],
        input_output_aliases={i: i for i in range(n)},
        compiler_params=pltpu.CompilerParams(has_side_effects=SPLIT_COPY),
    )(*[_hbm(p) for p in ps])
    return res[:n], res[n:2 * n], res[2 * n], res[2 * n + 1], res[2 * n + 2]


def _scatter_wait(thru, lands, send, recv, after, name):
    n = len(thru)

    def body(*refs):
        thru_refs, land_refs, send_ref, recv_ref = refs[:n], refs[n:2 * n], refs[2 * n], refs[2 * n + 1]
        x, y, c = _mesh_pos()
        for j in range(n):
            for cp in _scatter_copies(j, thru_refs[j], land_refs[j], send_ref, recv_ref, x, y, c):
                cp.wait_send()
                cp.wait_recv()

    res = pl.pallas_call(
        body, name=name, in_specs=[HBM] * (2 * n) + [SEM, SEM, ANY], out_specs=[HBM] * (2 * n),
        out_shape=[pltpu.HBM(a.shape, a.dtype) for a in list(thru) + list(lands)],
        input_output_aliases={i: i for i in range(2 * n)},
        compiler_params=pltpu.CompilerParams(has_side_effects=SPLIT_COPY),
    )(*thru, *lands, send, recv, after)
    return res[:n], res[n:]


def _row_tile(r, pref):
    if r <= pref:
        return r
    for t in range(pref - pref % 16, 0, -16):
        if r % t == 0:
            return t
    return r


def _add_sibling(g, r1, pos, name):
    _, r, D = g.shape
    tr = _row_tile(r, 512)

    def body(pos_ref, g_ref, r_ref, o_ref):
        o_ref[...] = (g_ref[...].astype(F32) + r_ref[...].astype(F32)).astype(BF16)

    return pl.pallas_call(
        body, name=name,
        grid_spec=pltpu.PrefetchScalarGridSpec(
            num_scalar_prefetch=1, grid=(4, r // tr),
            in_specs=[pl.BlockSpec((None, None, tr, D), lambda k, j, pos: (k, pos[0], j, 0)),
                      pl.BlockSpec((None, tr, D), lambda k, j, pos: (k, j, 0))],
            out_specs=pl.BlockSpec((None, tr, D), lambda k, j, pos: (k, j, 0))),
        out_shape=jax.ShapeDtypeStruct((4, r, D), BF16),
        compiler_params=_params("parallel", "parallel"),
    )(pos, g.reshape(4, 2, r, D), r1)


def _add_chips(p, r2, pos, name):
    _, r, D = p.shape
    tr = _row_tile(r, 512)

    def body(pos_ref, p_ref, r_ref, o_ref):
        acc = p_ref[...].astype(F32)
        for j in range(3):
            acc = acc + r_ref[j].astype(F32)
        o_ref[...] = acc

    return pl.pallas_call(
        body, name=name,
        grid_spec=pltpu.PrefetchScalarGridSpec(
            num_scalar_prefetch=1, grid=(r // tr,),
            in_specs=[pl.BlockSpec((None, tr, D), lambda j, pos: (pos[1], j, 0)),
                      pl.BlockSpec((3, tr, D), lambda j, pos: (0, j, 0))],
            out_specs=pl.BlockSpec((tr, D), lambda j, pos: (j, 0))),
        out_shape=jax.ShapeDtypeStruct((r, D), F32),
        compiler_params=_params("parallel"),
    )(pos, p, r2)


def _place_cols(local, me, width):
    R, w = local.shape
    return lax.dynamic_update_slice(jnp.zeros((R, width), F32), local, (0, me * w))


def _pad_rows(a, rows):
    return jnp.pad(a, ((0, rows - a.shape[0]), (0, 0)))


SUBLAYER_SHARDS = ((0, 4), (8, 9), (1, 5), (2, 6), (10, 11), (3, 7))


class _StepComm:
    def __init__(self, shards, pos):
        self.pos = pos
        self.shards = shards
        self.gathers = {}
        self.started = self._start_gather(1, self._start_gather(0, ()))
        self.pending = {}

    def _start_gather(self, s, after):
        outs, thru, send, recv, token = _gather_start([self.shards[i] for i in SUBLAYER_SHARDS[s]],
                                                      2 * self.pos[1] + self.pos[0], after, f"gather_start_{s}")
        self.gathers[s] = (outs, thru, send, recv)
        return (token,)

    def weights(self, s, after):
        outs = _gather_wait(*self.gathers.pop(s), (after,), f"gather_wait_{s}")
        outs = _gather_forward(outs, f"gather_forward_{s}")
        tok = self._start_gather(s + 2, (outs[0],)) if s + 2 < len(SUBLAYER_SHARDS) else ()
        return [w.reshape(-1, w.shape[-1]) for w in outs], tok

    def grads(self, s, mats):
        g3 = [g.reshape(N_DEV, g.shape[0] // N_DEV, g.shape[1]) for g in mats]
        r1 = _swap_with_sibling(g3, f"rs_sibling_{s}")
        ps = [_add_sibling(g, r, self.pos, f"rs_add_sibling_{s}_{i}") for i, (g, r) in enumerate(zip(g3, r1))]
        thru, lands, send, recv, token = _scatter_start(ps, f"rs_start_{s}")
        self.pending[s] = (thru, lands, send, recv)
        return (token,)

    def finish(self, after):
        out = {}
        for s in sorted(self.pending, reverse=True):
            ps, lands = _scatter_wait(*self.pending[s], after, f"rs_wait_{s}")
            for i, p, r2 in zip(SUBLAYER_SHARDS[s], ps, lands):
                out[i] = _add_chips(p, r2, self.pos, f"rs_add_chips_{i}")
        return out


def kernel(x, norm_g, ffn_w_in, ffn_w_out, conv_w_pw1, conv_b_pw1, conv_w_dw, conv_b_dw, conv_norm_g, conv_w_pw2, conv_b_pw2, attn_w_qkv, attn_q_norm, attn_k_norm, attn_w_o, loss_target, m_norm_g, m_ffn_w_in, m_ffn_w_out, m_conv_w_pw1, m_conv_b_pw1, m_conv_w_dw, m_conv_b_dw, m_conv_norm_g, m_conv_w_pw2, m_conv_b_pw2, m_attn_w_qkv, m_attn_q_norm, m_attn_k_norm, m_attn_w_o, v_norm_g, v_ffn_w_in, v_ffn_w_out, v_conv_w_pw1, v_conv_b_pw1, v_conv_w_dw, v_conv_b_dw, v_conv_norm_g, v_conv_w_pw2, v_conv_b_pw2, v_attn_w_qkv, v_attn_q_norm, v_attn_k_norm, v_attn_w_o):
    T, D = x.shape[1], x.shape[2]
    xi, yi, ci = _mesh_pos()
    me = 4 * xi + 2 * yi + ci
    pos = jnp.stack([ci, 2 * xi + yi]).astype(I32)
    lf = [(l, f) for l in range(2) for f in range(2)]

    shards = ([ffn_w_in[l, f].T.astype(BF16) for l, f in lf] + [ffn_w_out[l, f].astype(BF16) for l, f in lf]
              + [conv_w_pw1[0].T.astype(BF16), conv_w_pw2[0].astype(BF16),
                 attn_w_qkv[0].T.astype(BF16), attn_w_o[0].astype(BF16)])
    comm = _StepComm(shards, pos)
    w_cols = norm_g.shape[2]
    vec = _all_sum(_pad_rows(jnp.concatenate([_place_cols(norm_g.reshape(6, w_cols), me, D),
                                              _place_cols(conv_w_dw[0], me, D)]), 40), "gather_vectors")
    vecs = {
        "norm_g": vec[0:6], "w_dw": vec[6:6 + CONV_WIDTH],
        "b_pw1": conv_b_pw1.reshape(2, D), "b_dw": conv_b_dw, "cg": conv_norm_g, "b_pw2": conv_b_pw2,
        "q_norm": attn_q_norm[0], "k_norm": attn_k_norm[0],
    }

    loss_local, dx, small, tok = _local_step(x[0], loss_target[0], vecs, comm)
    loss = lax.psum(loss_local, ("x", "y", "c"))

    mats = comm.finish(dx)
    hd3 = 3 * HEAD_DIM
    packed = jnp.concatenate([
        small["norm_g"], small["b_pw1"], small["w_dw"], small["b_dw"], small["cg"], small["b_pw2"],
        jnp.pad(small["q_norm"].reshape(1, hd3), ((0, 0), (0, D - hd3))),
        jnp.pad(small["k_norm"].reshape(1, hd3), ((0, 0), (0, D - hd3)))])
    red = _all_sum(_pad_rows(packed, 48), "reduce_vectors")
    col0 = me * w_cols
    grads = {
        "norm_g": lax.dynamic_slice(red[0:6], (0, col0), (6, w_cols)),
        "ffn_w_in": jnp.concatenate([mats[i].T for i in range(4)]),
        "ffn_w_out": jnp.concatenate([mats[i] for i in range(4, 8)]),
        "conv_w_pw1": mats[8].T,
        "conv_b_pw1": red[6:8].reshape(1, 2 * D),
        "conv_w_dw": lax.dynamic_slice(red[8:8 + CONV_WIDTH], (0, col0), (CONV_WIDTH, w_cols)),
        "conv_b_dw": red[39:40], "conv_norm_g": red[40:41], "conv_w_pw2": mats[9], "conv_b_pw2": red[41:42],
        "attn_w_qkv": mats[10].T,
        "attn_q_norm": red[42, :hd3].reshape(3, HEAD_DIM), "attn_k_norm": red[43, :hd3].reshape(3, HEAD_DIM),
        "attn_w_o": mats[11],
    }

    names = ["norm_g", "ffn_w_in", "ffn_w_out", "conv_w_pw1", "conv_b_pw1", "conv_w_dw", "conv_b_dw",
             "conv_norm_g", "conv_w_pw2", "conv_b_pw2", "attn_w_qkv", "attn_q_norm", "attn_k_norm", "attn_w_o"]
    ws = [norm_g, ffn_w_in, ffn_w_out, conv_w_pw1, conv_b_pw1, conv_w_dw, conv_b_dw, conv_norm_g, conv_w_pw2,
          conv_b_pw2, attn_w_qkv, attn_q_norm, attn_k_norm, attn_w_o]
    ms = [m_norm_g, m_ffn_w_in, m_ffn_w_out, m_conv_w_pw1, m_conv_b_pw1, m_conv_w_dw, m_conv_b_dw, m_conv_norm_g,
          m_conv_w_pw2, m_conv_b_pw2, m_attn_w_qkv, m_attn_q_norm, m_attn_k_norm, m_attn_w_o]
    vs = [v_norm_g, v_ffn_w_in, v_ffn_w_out, v_conv_w_pw1, v_conv_b_pw1, v_conv_w_dw, v_conv_b_dw, v_conv_norm_g,
          v_conv_w_pw2, v_conv_b_pw2, v_attn_w_qkv, v_attn_q_norm, v_attn_k_norm, v_attn_w_o]
    g_out, d_out, m_out, v_out = [], [], [], []
    for name, w, m, v in zip(names, ws, ms, vs):
        shape2 = (-1, w.shape[-1]) if w.size > SMALL_PARAM else w.shape
        g2 = grads[name].reshape(shape2)
        delta, new_m, new_v = _adamw(w.reshape(shape2), g2, m.reshape(shape2), v.reshape(shape2), f"adamw_{name}")
        g_out.append(g2.reshape(w.shape))
        d_out.append(delta.reshape(w.shape))
        m_out.append(new_m.reshape(w.shape))
        v_out.append(new_v.reshape(w.shape))
    return (loss, dx.reshape(x.shape), *g_out, *d_out, *m_out, *v_out)
```

```python
import functools

import jax
import jax.numpy as jnp
from jax import lax
from jax.experimental import pallas as pl
from jax.experimental.pallas import tpu as pltpu

F32 = jnp.float32
BF16 = jnp.bfloat16
I32 = jnp.int32

NORM_EPS = 1e-6
LANES = 128
SUBLANES = 8
MXU_WIDTH = 256
HEAD_DIM = 128
ATTN_BLOCK = 128
ATTN_DILATIONS = (1, 4, 16)
CONV_WIDTH = 31
CONV_HALO = 32
CONV_CHUNK = 16
WHOLE_K_MAX = 6144
MM_TN_TOKENS = 2048
STREAM_TILE = 512
ADAM_LR, ADAM_B1, ADAM_B2, ADAM_EPS, ADAM_WD, ADAM_STEP = 0.001, 0.9, 0.999, 1e-08, 0.01, 10
N_DEV = 8
SMALL_PARAM = 8192
NEG_BIG = -1e30
V7X_VMEM_BYTES = 64 * 1024 * 1024
VMEM_LIMIT = V7X_VMEM_BYTES - 8 * 1024 * 1024
MESH = pl.DeviceIdType.MESH
ANY = pl.BlockSpec(memory_space=pl.ANY)


def _params(*sem):
    return pltpu.CompilerParams(dimension_semantics=sem or None, vmem_limit_bytes=VMEM_LIMIT)


def _dot_nn(a, b):
    return lax.dot_general(a, b, (((1,), (0,)), ((), ())), preferred_element_type=F32)


def _dot_nt(a, b):
    return lax.dot_general(a, b, (((1,), (1,)), ((), ())), preferred_element_type=F32)


def _dot_tn(a, b):
    return lax.dot_general(a, b, (((0,), (0,)), ((), ())), preferred_element_type=F32)


def _sigmoid(x):
    return 1.0 / (1.0 + jnp.exp(-x))


def _tile(n, pref):
    if n <= pref:
        return n
    for t in range(pref - pref % 128, 0, -128):
        if n % t == 0:
            return t
    return n


def _to_lane_chunks(dst, val):
    for c in range(dst.shape[0]):
        dst[c] = val[:, c * LANES:(c + 1) * LANES]


def _rows8(v):
    tm, c = v.shape
    return v.reshape(tm // 8, 8, c).sum(axis=0)


def _rmsnorm(x, g, name, after=()):
    T, D = x.shape
    tm = min(T, 512)

    def body(x_ref, g_ref, *rest):
        h_ref = rest[-1]
        xv = x_ref[...]
        r = lax.rsqrt(jnp.mean(xv * xv, axis=-1, keepdims=True) + NORM_EPS)
        h_ref[...] = (xv * r * g_ref[...]).astype(BF16)

    return pl.pallas_call(
        body, name=name, grid=(T // tm,),
        in_specs=[pl.BlockSpec((tm, D), lambda i: (i, 0)), pl.BlockSpec((1, D), lambda i: (0, 0))] + [ANY] * len(after),
        out_specs=pl.BlockSpec((tm, D), lambda i: (i, 0)),
        out_shape=jax.ShapeDtypeStruct((T, D), BF16),
        compiler_params=_params("parallel"),
    )(x, g, *after)


def _rmsnorm_streams(x, g, name):
    T, D = x.shape
    tm = min(T, STREAM_TILE)
    dils = [d for d in ATTN_DILATIONS if d > 1]

    def body(x_ref, g_ref, h_ref, *rest):
        outs, hs = rest[:-1], rest[-1]
        xv = x_ref[...]
        r = lax.rsqrt(jnp.mean(xv * xv, axis=-1, keepdims=True) + NORM_EPS)
        hf = xv * r * g_ref[...]
        h_ref[...] = hf.astype(BF16)
        _to_lane_chunks(hs, hf)
        for d, o_ref in zip(dils, outs):
            for s in range(d):
                for c in range(D // LANES):
                    o_ref[s, :, c * LANES:(c + 1) * LANES] = hs.at[c][pl.ds(s, tm // d, stride=d), :].astype(BF16)

    res = pl.pallas_call(
        body, name=name, grid=(T // tm,),
        in_specs=[pl.BlockSpec((tm, D), lambda i: (i, 0)), pl.BlockSpec((1, D), lambda i: (0, 0))],
        out_specs=[pl.BlockSpec((tm, D), lambda i: (i, 0))]
        + [pl.BlockSpec((d, tm // d, D), lambda i: (0, i, 0)) for d in dils],
        out_shape=[jax.ShapeDtypeStruct((T, D), BF16)] + [jax.ShapeDtypeStruct((d, T // d, D), BF16) for d in dils],
        scratch_shapes=[pltpu.VMEM((D // LANES, tm, LANES), F32)],
        compiler_params=_params("parallel"),
    )(x, g)
    return [res[0]] + [a.reshape(T, D) for a in res[1:]]


def _gated_in(h, wt, bias, act, u_dtype, name, after=()):
    T, D = h.shape
    F = wt.shape[0] // 2
    tn = _tile(F, 1408)
    tm = min(T, 1024)
    nn = F // tn

    def body(*refs):
        z_ref, u_ref = refs[-2:]
        if bias is None:
            h_ref, w0_ref, w1_ref = refs[:3]
        else:
            h_ref, w0_ref, w1_ref, b_ref = refs[:4]
        hv = h_ref[...]
        for c0 in range(0, tn, MXU_WIDTH):
            cs = slice(c0, min(c0 + MXU_WIDTH, tn))
            z0 = _dot_nt(hv, w0_ref[cs, :])
            z1 = _dot_nt(hv, w1_ref[cs, :])
            if bias is not None:
                z0 = z0 + b_ref[0:1, cs]
                z1 = z1 + b_ref[1:2, cs]
            z_ref[0, :, cs] = z0.astype(BF16)
            z_ref[1, :, cs] = z1.astype(BF16)
            if act == "swiglu":
                u = z0 * _sigmoid(z0) * z1
            else:
                u = z0 * _sigmoid(z1)
            u_ref[:, cs] = u.astype(u_dtype)

    in_specs = [pl.BlockSpec((tm, D), lambda n, m: (m, 0)),
                pl.BlockSpec((tn, D), lambda n, m: (n, 0)),
                pl.BlockSpec((tn, D), lambda n, m: (n + nn, 0))]
    args = [h, wt, wt]
    if bias is not None:
        in_specs.append(pl.BlockSpec((2, tn), lambda n, m: (0, n)))
        args.append(bias)
    in_specs += [ANY] * len(after)
    args += list(after)
    return pl.pallas_call(
        body, name=name, grid=(nn, T // tm), in_specs=in_specs,
        out_specs=[pl.BlockSpec((2, tm, tn), lambda n, m: (0, m, n)), pl.BlockSpec((tm, tn), lambda n, m: (m, n))],
        out_shape=[jax.ShapeDtypeStruct((2, T, F), BF16), jax.ShapeDtypeStruct((T, F), u_dtype)],
        compiler_params=_params("parallel", "parallel"),
    )(*args)


def _swiglu_bwd(dxo, wout, z3, scale, name, after=()):
    T, D = dxo.shape
    F = wout.shape[0]
    tm = min(T, 512)

    def body(d_ref, w_ref, z_ref, *rest):
        dz_ref = rest[-1]
        dy = (d_ref[...] * scale).astype(BF16)
        for c0 in range(0, F, MXU_WIDTH):
            cs = slice(c0, min(c0 + MXU_WIDTH, F))
            da = _dot_nt(dy, w_ref[cs, :])
            gate = z_ref[0, :, cs].astype(F32)
            up = z_ref[1, :, cs].astype(F32)
            sg = _sigmoid(gate)
            dz_ref[0, :, cs] = (da * up * (sg * (1.0 + gate * (1.0 - sg)))).astype(BF16)
            dz_ref[1, :, cs] = (da * gate * sg).astype(BF16)

    return pl.pallas_call(
        body, name=name, grid=(T // tm,),
        in_specs=[pl.BlockSpec((tm, D), lambda m: (m, 0)),
                  pl.BlockSpec((F, D), lambda m: (0, 0)),
                  pl.BlockSpec((2, tm, F), lambda m: (0, m, 0))] + [ANY] * len(after),
        out_specs=pl.BlockSpec((2, tm, F), lambda m: (0, m, 0)),
        out_shape=jax.ShapeDtypeStruct((2, T, F), BF16),
        compiler_params=_params("parallel"),
    )(dxo, wout, z3, *after)


def _mm_nt(a, wt, out_dtype, name, *, w_row0=0, n_rows=None, streams=1, after=()):
    T, K = a.shape
    N = wt.shape[0] if n_rows is None else n_rows
    L = T // streams
    rows = min(L, STREAM_TILE)
    sp = STREAM_TILE // rows
    mps = L // rows
    assert streams % sp == 0 and w_row0 % N == 0

    def body(a_ref, w_ref, *rest):
        o_ref = rest[-1]
        for s in range(sp):
            a_s = a_ref[s * rows:(s + 1) * rows, :].astype(BF16)
            for c0 in range(0, N, MXU_WIDTH):
                c1 = min(c0 + MXU_WIDTH, N)
                o_ref[:, s * N + c0:s * N + c1] = _dot_nt(a_s, w_ref[c0:c1, :]).astype(out_dtype)

    return pl.pallas_call(
        body, name=name, grid=(T // (rows * sp),),
        in_specs=[pl.BlockSpec((rows * sp, K), lambda m: (m, 0)), pl.BlockSpec((N, K), lambda m: (w_row0 // N, 0))]
        + [ANY] * len(after),
        out_specs=pl.BlockSpec((rows, sp * N), lambda m: (m % mps, m // mps)),
        out_shape=jax.ShapeDtypeStruct((L, streams * N), out_dtype),
        compiler_params=_params("parallel"),
    )(a, wt, *after)


def _mm_tn(a, b, b_scale, name, *, streams=1):
    if a.ndim == 3:
        J, T, F = a.shape
    elif streams > 1:
        J, T, F = 1, a.shape[0] * streams, a.shape[1] // streams
    else:
        (T, F), J = a.shape, 1
    N = b.shape[1]
    tmm = _tile(F, 1408)
    tk = min(T // streams, MM_TN_TOKENS)
    npj = F // tmm
    sp = max(1, MM_TN_TOKENS // tk) if streams > 1 and tk == T // streams else 1
    assert streams % sp == 0
    nk = T // (tk * sp)
    kps = nk // streams if sp == 1 else 1

    def body(*refs):
        a_refs, (b_ref, o_ref, acc_ref) = refs[:sp], refs[sp:]
        k = pl.program_id(1)
        part = None
        for s, a_ref in enumerate(a_refs):
            bv = b_ref[s * tk:(s + 1) * tk, :]
            if b_scale != 1.0:
                bv = bv * b_scale
            dot = _dot_tn(a_ref[...].astype(BF16), bv.astype(BF16))
            part = dot if part is None else part + dot

        @pl.when(k == 0)
        def _():
            acc_ref[...] = part

        @pl.when(k > 0)
        def _():
            acc_ref[...] += part

        @pl.when(k == nk - 1)
        def _():
            o_ref[...] = acc_ref[...].astype(BF16)

    if a.ndim == 3:
        a_specs = [pl.BlockSpec((None, tk, tmm), lambda i, k: (i // npj, k, i % npj))]
    elif sp > 1:
        a_specs = [pl.BlockSpec((tk, tmm), lambda i, k, s=s: (0, (k * sp + s) * npj + i)) for s in range(sp)]
    elif streams > 1:
        a_specs = [pl.BlockSpec((tk, tmm), lambda i, k: (k % kps, (k // kps) * npj + i))]
    else:
        a_specs = [pl.BlockSpec((tk, tmm), lambda i, k: (k, i))]
    return pl.pallas_call(
        body, name=name, grid=(J * npj, nk),
        in_specs=a_specs + [pl.BlockSpec((tk * sp, N), lambda i, k: (k, 0))],
        out_specs=pl.BlockSpec((tmm, N), lambda i, k: (i, 0)),
        out_shape=jax.ShapeDtypeStruct((J * F, N), BF16),
        scratch_shapes=[pltpu.VMEM((tmm, N), F32)],
        compiler_params=_params("parallel", "arbitrary"),
    )(*[a] * sp, b)


def _rows_mm(a, b, mode, name, *, x=None, scale=1.0, bias=None, tgt=None, g=None, dxo=None,
             b_row0=0, streams=1, extras=(), after=()):
    if a.ndim == 3:
        J, T, F = a.shape
    elif streams > 1:
        J, T, F = 1, a.shape[0] * streams, a.shape[1] // streams
    else:
        (T, F), J = a.shape, 1
    D = b.shape[1]
    if streams > 1 and T // streams < STREAM_TILE:
        L = T // streams
        sp = STREAM_TILE // L
        assert mode == "plain" and streams % sp == 0 and b_row0 % F == 0 and not after

        def body_short(a_ref, b_ref, o_ref):
            for s in range(sp):
                o_ref[s * L:(s + 1) * L, :] = _dot_nn(a_ref[:, s * F:(s + 1) * F], b_ref[...])

        return pl.pallas_call(
            body_short, name=name, grid=(streams // sp,),
            in_specs=[pl.BlockSpec((L, sp * F), lambda m: (0, m)), pl.BlockSpec((F, D), lambda m: (b_row0 // F, 0))],
            out_specs=pl.BlockSpec((sp * L, D), lambda m: (m, 0)),
            out_shape=jax.ShapeDtypeStruct((T, D), F32),
            compiler_params=_params("parallel"),
        )(a, b)
    tk = _tile(F, 3072)
    kpj = F // tk
    nk = J * kpj
    tm = min(T // streams, STREAM_TILE)
    whole_k = a.ndim == 3 and kpj == 1 and J > 1 and J * F <= WHOLE_K_MAX
    if whole_k:
        nk, tm = 1, min(T, STREAM_TILE // 2)
    nm = T // tm
    mps = nm // streams
    assert b_row0 % tk == 0
    kb0 = b_row0 // tk
    n_ex = len(extras)
    n_ab = J if whole_k else 1

    def body(*refs):
        refs = list(refs)
        a_refs, b_refs = refs[:n_ab], refs[n_ab:2 * n_ab]
        rest = refs[2 * n_ab + len(after):]
        if n_ex:
            tmp_ref = rest.pop()
        acc_ref = rest.pop()
        ex_refs, rest = rest[:n_ex], rest[n_ex:]
        m, k = pl.program_id(0), pl.program_id(1)
        part = _dot_nn(a_refs[0][...], b_refs[0][...])
        for a_ref, b_ref in zip(a_refs[1:], b_refs[1:]):
            part = part + _dot_nn(a_ref[...], b_ref[...])

        @pl.when(k == 0)
        def _():
            acc_ref[...] = part

        @pl.when(k > 0)
        def _():
            acc_ref[...] += part

        @pl.when(k == nk - 1)
        def _():
            acc = acc_ref[...]
            for (d, _), e_ref in zip(extras, ex_refs):
                for s in range(d):
                    for c in range(D // LANES):
                        tmp_ref.at[c][pl.ds(s, tm // d, stride=d), :] = e_ref[s, :, c * LANES:(c + 1) * LANES]
                acc = acc + jnp.concatenate([tmp_ref[c] for c in range(D // LANES)], axis=1)
            if mode == "plain":
                (o_ref,) = rest
                o_ref[...] = acc
            elif mode == "res":
                ins = list(rest)
                x_ref = ins.pop(0)
                if bias is not None:
                    acc = acc + ins.pop(0)[...]
                xn = x_ref[...] + scale * acc
                if g is None:
                    (o_ref,) = ins
                else:
                    g_ref, o_ref, h_ref = ins
                    r = lax.rsqrt(jnp.mean(xn * xn, axis=-1, keepdims=True) + NORM_EPS)
                    h_ref[...] = (xn * r * g_ref[...]).astype(BF16)
                o_ref[...] = xn
            elif mode == "loss":
                x_ref, t_ref, o_ref, l_ref = rest
                e = x_ref[...] + scale * acc - t_ref[...]
                o_ref[...] = e / D

                @pl.when(m == 0)
                def _():
                    l_ref[...] = jnp.zeros_like(l_ref)

                l_ref[...] += _rows8(e * e)
            else:
                x_ref, g_ref, d_ref, o_ref, dg_ref = rest
                xv = x_ref[...]
                r = lax.rsqrt(jnp.mean(xv * xv, axis=-1, keepdims=True) + NORM_EPS)
                xh = xv * r
                dxh = acc * g_ref[...]
                o_ref[...] = d_ref[...] + r * (dxh - xh * jnp.mean(dxh * xh, axis=-1, keepdims=True))

                @pl.when(m == 0)
                def _():
                    dg_ref[...] = jnp.zeros_like(dg_ref)

                dg_ref[...] += _rows8(acc * xh)

    if whole_k:
        a_specs = [pl.BlockSpec((None, tm, F), lambda m, k, j=j: (j, m, 0)) for j in range(J)]
        b_specs = [pl.BlockSpec((F, D), lambda m, k, j=j: (kb0 + j, 0)) for j in range(J)]
    else:
        if a.ndim == 3:
            a_specs = [pl.BlockSpec((None, tm, tk), lambda m, k: (k // kpj, m, k % kpj))]
        elif streams > 1:
            a_specs = [pl.BlockSpec((tm, tk), lambda m, k: (m % mps, (m // mps) * kpj + k))]
        else:
            a_specs = [pl.BlockSpec((tm, tk), lambda m, k: (m, k))]
        b_specs = [pl.BlockSpec((tk, D), lambda m, k: (kb0 + k, 0))]
    row = pl.BlockSpec((tm, D), lambda m, k: (m, 0))
    vec = pl.BlockSpec((1, D), lambda m, k: (0, 0))
    part8 = pl.BlockSpec((8, D), lambda m, k: (0, 0))
    in_specs = a_specs + b_specs + [ANY] * len(after)
    args = [a] * n_ab + [b] * n_ab + list(after)
    for d, e in extras:
        in_specs.append(pl.BlockSpec((d, tm // d, D), lambda m, k: (0, m, 0)))
        args.append(e.reshape(d, T // d, D))
    full = jax.ShapeDtypeStruct((T, D), F32)
    small = jax.ShapeDtypeStruct((8, D), F32)
    if mode == "plain":
        out_specs, out_shape = row, full
    elif mode == "res":
        in_specs.append(row)
        args.append(x)
        if bias is not None:
            in_specs.append(vec)
            args.append(bias)
        if g is None:
            out_specs, out_shape = row, full
        else:
            in_specs.append(vec)
            args.append(g)
            out_specs, out_shape = [row, row], [full, jax.ShapeDtypeStruct((T, D), BF16)]
    elif mode == "loss":
        in_specs += [row, row]
        args += [x, tgt]
        out_specs, out_shape = [row, part8], [full, small]
    else:
        in_specs += [row, vec, row]
        args += [x, g, dxo]
        out_specs, out_shape = [row, part8], [full, small]
    return pl.pallas_call(
        body, name=name, grid=(nm, nk), in_specs=in_specs, out_specs=out_specs, out_shape=out_shape,
        scratch_shapes=[pltpu.VMEM((tm, D), F32)] + ([pltpu.VMEM((D // LANES, tm, LANES), F32)] if n_ex else []),
        compiler_params=_params("arbitrary", "arbitrary"),
    )(*args)


def _sublane_phases(sh):
    rows = sh.shape[1] - SUBLANES
    for p in range(1, SUBLANES):
        sh[p, pl.ds(0, rows), :] = sh[0, pl.ds(p, rows), :]


def _shifted(sh, offset, j, rows=SUBLANES):
    start = pl.multiple_of(j * rows + (offset - offset % SUBLANES), SUBLANES)
    return sh[offset % SUBLANES, pl.ds(start, rows), :]


def _conv_taps(wb, sh, offsets, j, init):
    groups = CONV_CHUNK // SUBLANES
    accs = [init[g * SUBLANES:(g + 1) * SUBLANES] for g in range(groups)]
    for k, off in enumerate(offsets):
        w = wb[k]
        start = pl.multiple_of(j * CONV_CHUNK + (off - off % SUBLANES), SUBLANES)
        for g in range(groups):
            accs[g] = accs[g] + w * sh[off % SUBLANES, pl.ds(start + g * SUBLANES, SUBLANES), :]
    return jnp.concatenate(accs, axis=0)


def _broadcast_rows(dst, src):
    for r in range(dst.shape[0]):
        dst[r] = jnp.zeros(dst.shape[1:], F32) + src[r:r + 1, :]


def _dwconv_fwd(u, w, b_dw, cg, name):
    T, C = u.shape
    tm = min(T, 256)
    hb = tm // CONV_HALO
    pad = CONV_HALO - (CONV_WIDTH - 1)

    def body(u_ref, halo_ref, w_ref, b_ref, g_ref, v_ref, s_ref, sh, wb):
        i = pl.program_id(0)
        sh[0, pl.ds(0, CONV_HALO), :] = jnp.where(i > 0, halo_ref[...], 0.0)
        sh[0, pl.ds(CONV_HALO, tm), :] = u_ref[...]
        _sublane_phases(sh)
        _broadcast_rows(wb, w_ref)
        bias = jnp.zeros((CONV_CHUNK, C), F32) + b_ref[...]

        def chunk(j, carry):
            acc = _conv_taps(wb, sh, [pad + k for k in range(CONV_WIDTH)], j, bias)
            v_ref[pl.ds(pl.multiple_of(j * CONV_CHUNK, CONV_CHUNK), CONV_CHUNK), :] = acc
            return carry

        lax.fori_loop(0, tm // CONV_CHUNK, chunk, 0)
        acc = v_ref[...]
        r = lax.rsqrt(jnp.mean(acc * acc, axis=-1, keepdims=True) + NORM_EPS)
        n = acc * r * g_ref[...]
        s_ref[...] = (n * _sigmoid(n)).astype(BF16)

    row = pl.BlockSpec((tm, C), lambda i: (i, 0))
    vec = pl.BlockSpec((1, C), lambda i: (0, 0))
    return pl.pallas_call(
        body, name=name, grid=(T // tm,),
        in_specs=[row, pl.BlockSpec((CONV_HALO, C), lambda i: (jnp.maximum(i * hb - 1, 0), 0)),
                  pl.BlockSpec((CONV_WIDTH, C), lambda i: (0, 0)), vec, vec],
        out_specs=[row, row],
        out_shape=[jax.ShapeDtypeStruct((T, C), F32), jax.ShapeDtypeStruct((T, C), BF16)],
        scratch_shapes=[pltpu.VMEM((SUBLANES, CONV_HALO + tm, C), F32), pltpu.VMEM((CONV_WIDTH, SUBLANES, C), F32)],
        compiler_params=_params("parallel"),
    )(u, u, w, b_dw, cg)


def _conv_post_bwd(ds, v, cg, dxo, name):
    T, C = v.shape
    tm = min(T, 512)

    def body(ds_ref, v_ref, g_ref, d_ref, dv_ref, dcg_ref, dbdw_ref, dbpw2_ref):
        @pl.when(pl.program_id(0) == 0)
        def _():
            dcg_ref[...] = jnp.zeros_like(dcg_ref)
            dbdw_ref[...] = jnp.zeros_like(dbdw_ref)
            dbpw2_ref[...] = jnp.zeros_like(dbpw2_ref)

        vv = v_ref[...]
        r = lax.rsqrt(jnp.mean(vv * vv, axis=-1, keepdims=True) + NORM_EPS)
        vh = vv * r
        n = vh * g_ref[...]
        sg = _sigmoid(n)
        dn = ds_ref[...] * (sg * (1.0 + n * (1.0 - sg)))
        dvh = dn * g_ref[...]
        dv = r * (dvh - vh * jnp.mean(dvh * vh, axis=-1, keepdims=True))
        dv_ref[...] = dv
        dcg_ref[...] += _rows8(dn * vh)
        dbdw_ref[...] += _rows8(dv)
        dbpw2_ref[...] += _rows8(d_ref[...])

    row = pl.BlockSpec((tm, C), lambda i: (i, 0))
    part8 = pl.BlockSpec((8, C), lambda i: (0, 0))
    small = jax.ShapeDtypeStruct((8, C), F32)
    return pl.pallas_call(
        body, name=name, grid=(T // tm,),
        in_specs=[row, row, pl.BlockSpec((1, C), lambda i: (0, 0)), row],
        out_specs=[row, part8, part8, part8],
        out_shape=[jax.ShapeDtypeStruct((T, C), F32), small, small, small],
        compiler_params=_params("arbitrary"),
    )(ds, v, cg, dxo)


def _dwconv_bwd(dv, u, w, z3, name):
    T, C = u.shape
    tm = min(T, 256)
    hb = tm // CONV_HALO
    nt = T // tm
    pad = CONV_HALO - (CONV_WIDTH - 1)

    taps_at_once = 4
    chunks_at_once = 8

    def body(dv_ref, dvn_ref, u_ref, up_ref, w_ref, z_ref, dz_ref, dw_ref, db_ref, dvsh, ush, wb, du_ref, dwacc):
        i = pl.program_id(0)

        @pl.when(i == 0)
        def _():
            dwacc[...] = jnp.zeros_like(dwacc)
            db_ref[...] = jnp.zeros_like(db_ref)

        dvsh[0, pl.ds(0, tm), :] = dv_ref[...]
        dvsh[0, pl.ds(tm, CONV_HALO), :] = jnp.where(i < nt - 1, dvn_ref[...], 0.0)
        ush[0, pl.ds(0, CONV_HALO), :] = jnp.where(i > 0, up_ref[...], 0.0)
        ush[0, pl.ds(CONV_HALO, tm), :] = u_ref[...]
        _sublane_phases(dvsh)
        _sublane_phases(ush)
        _broadcast_rows(wb, w_ref)

        def du_chunk(j, carry):
            acc = _conv_taps(wb, dvsh, [CONV_WIDTH - 1 - k for k in range(CONV_WIDTH)], j,
                             jnp.zeros((CONV_CHUNK, C), F32))
            du_ref[pl.ds(pl.multiple_of(j * CONV_CHUNK, CONV_CHUNK), CONV_CHUNK), :] = acc
            return carry

        lax.fori_loop(0, tm // CONV_CHUNK, du_chunk, 0)

        for k0 in range(0, CONV_WIDTH, taps_at_once):
            taps = range(k0, min(k0 + taps_at_once, CONV_WIDTH))

            def dw_chunks(jj, carry, taps=taps):
                accs = [jnp.zeros((SUBLANES, C), F32) for _ in taps]
                for u_ in range(chunks_at_once):
                    j = jj * chunks_at_once + u_
                    dvc = dv_ref[pl.ds(pl.multiple_of(j * SUBLANES, SUBLANES), SUBLANES), :]
                    accs = [acc + dvc * _shifted(ush, pad + k, j) for acc, k in zip(accs, taps)]
                for k, acc in zip(taps, accs):
                    dwacc[k] += acc
                return carry

            lax.fori_loop(0, tm // (SUBLANES * chunks_at_once), dw_chunks, 0)

        du = du_ref[...]
        a = z_ref[0].astype(F32)
        gate = z_ref[1].astype(F32)
        sg = _sigmoid(gate)
        da = du * sg
        dgate = du * a * sg * (1.0 - sg)
        dz_ref[0] = da.astype(BF16)
        dz_ref[1] = dgate.astype(BF16)
        db_ref[0:8, :] += _rows8(da)
        db_ref[8:16, :] += _rows8(dgate)

        @pl.when(i == nt - 1)
        def _():
            dw_ref[...] = jnp.zeros_like(dw_ref)
            for k in range(CONV_WIDTH):
                dw_ref[k:k + 1, :] = jnp.sum(dwacc[k], axis=0, keepdims=True)

    row = pl.BlockSpec((tm, C), lambda i: (i, 0))
    last_halo = T // CONV_HALO - 1
    return pl.pallas_call(
        body, name=name, grid=(nt,),
        in_specs=[row, pl.BlockSpec((CONV_HALO, C), lambda i: (jnp.minimum((i + 1) * hb, last_halo), 0)),
                  row, pl.BlockSpec((CONV_HALO, C), lambda i: (jnp.maximum(i * hb - 1, 0), 0)),
                  pl.BlockSpec((CONV_WIDTH, C), lambda i: (0, 0)),
                  pl.BlockSpec((2, tm, C), lambda i: (0, i, 0))],
        out_specs=[pl.BlockSpec((2, tm, C), lambda i: (0, i, 0)),
                   pl.BlockSpec((CONV_HALO, C), lambda i: (0, 0)),
                   pl.BlockSpec((16, C), lambda i: (0, 0))],
        out_shape=[jax.ShapeDtypeStruct((2, T, C), BF16), jax.ShapeDtypeStruct((CONV_HALO, C), F32),
                   jax.ShapeDtypeStruct((16, C), F32)],
        scratch_shapes=[pltpu.VMEM((SUBLANES, tm + CONV_HALO, C), F32), pltpu.VMEM((SUBLANES, CONV_HALO + tm, C), F32),
                        pltpu.VMEM((CONV_WIDTH, SUBLANES, C), F32), pltpu.VMEM((tm, C), F32),
                        pltpu.VMEM((CONV_WIDTH, SUBLANES, C), F32)],
        compiler_params=_params("arbitrary"),
    )(dv, dv, u, u, w, z3)


def _head_norm(raw, gain):
    xv = raw.astype(F32)
    r = lax.rsqrt(jnp.mean(xv * xv, axis=-1, keepdims=True) + NORM_EPS)
    xh = xv * r
    return (xh * gain).astype(BF16), xh, r


def _band_mask(n):
    row = lax.broadcasted_iota(I32, (ATTN_BLOCK, 2 * ATTN_BLOCK), 0)
    col = lax.broadcasted_iota(I32, (ATTN_BLOCK, 2 * ATTN_BLOCK), 1)
    return (col >= row) & (col <= row + ATTN_BLOCK) & ((col >= ATTN_BLOCK) | (n > 0))


def _carry_previous(n, bufs):
    @pl.when(n == 0)
    def _():
        for b in bufs:
            b[0:ATTN_BLOCK, :] = jnp.zeros((ATTN_BLOCK, b.shape[1]), b.dtype)

    @pl.when(n > 0)
    def _():
        for b in bufs:
            b[0:ATTN_BLOCK, :] = b[ATTN_BLOCK:, :]


def _attn_fwd(qkv, qg, kg, g, name):
    d = ATTN_DILATIONS[g]
    L = qkv.shape[0]
    HD = qkv.shape[1] // (3 * d)
    H = HD // HEAD_DIM
    nb = L // ATTN_BLOCK
    scale = HEAD_DIM ** -0.5

    def body(q_ref, kc_ref, vc_ref, qg_ref, kg_ref, o_ref, lse_ref, kk, vv):
        n = pl.program_id(1)
        _carry_previous(n, [kk, vv])
        mask = _band_mask(n)
        lane = lax.broadcasted_iota(I32, (ATTN_BLOCK, 128), 1)
        gq = qg_ref[g:g + 1, :]
        gk = kg_ref[g:g + 1, :]
        vv[ATTN_BLOCK:, :] = vc_ref[...]
        lse_tile = jnp.zeros((ATTN_BLOCK, 128), F32)
        for h in range(H):
            sl = slice(h * HEAD_DIM, (h + 1) * HEAD_DIM)
            q = _head_norm(q_ref[:, sl], gq)[0]
            kk[ATTN_BLOCK:, sl] = _head_norm(kc_ref[:, sl], gk)[0]
            s = jnp.where(mask, _dot_nt(q, kk[:, sl]) * scale, NEG_BIG)
            m = jnp.max(s, axis=-1, keepdims=True)
            p = jnp.exp(s - m)
            l = jnp.sum(p, axis=-1, keepdims=True)
            o_ref[:, sl] = _dot_nn(p.astype(BF16), vv[:, sl]) / l
            lse_tile = jnp.where(lane == h, m + jnp.log(l), lse_tile)
        lse_ref[...] = lse_tile

    def blk(which):
        return pl.BlockSpec((ATTN_BLOCK, HD), lambda r, n: (n, r * 3 + which))

    gains = pl.BlockSpec((3, HEAD_DIM), lambda r, n: (0, 0))
    return pl.pallas_call(
        body, name=name, grid=(d, nb),
        in_specs=[blk(0), blk(1), blk(2), gains, gains],
        out_specs=[pl.BlockSpec((ATTN_BLOCK, HD), lambda r, n: (n, r)),
                   pl.BlockSpec((ATTN_BLOCK, 128), lambda r, n: (n, r))],
        out_shape=[jax.ShapeDtypeStruct((L, d * HD), F32), jax.ShapeDtypeStruct((L, d * 128), F32)],
        scratch_shapes=[pltpu.VMEM((2 * ATTN_BLOCK, HD), BF16), pltpu.VMEM((2 * ATTN_BLOCK, HD), BF16)],
        compiler_params=_params("arbitrary", "arbitrary"),
    )(qkv, qkv, qkv, qg, kg)


def _attn_merge(os, lses, name):
    T = os[0].shape[0]
    HD = os[0].shape[1]
    H = HD // HEAD_DIM
    tm = min(T, STREAM_TILE)
    dils = ATTN_DILATIONS[1:]

    def body(o0, o1, o2, l0, l1, l2, ob_ref, of_ref, lse_ref, n1, n2, m1, m2):
        for d, src, dst, w in ((dils[0], o1, n1, HD), (dils[1], o2, n2, HD), (dils[0], l1, m1, 128), (dils[1], l2, m2, 128)):
            for s in range(d):
                for c in range(w // LANES):
                    dst.at[c][pl.ds(s, tm // d, stride=d), :] = src[:, s * w + c * LANES:s * w + (c + 1) * LANES]
        ls = [l0[...], m1[0], m2[0]]
        m = jnp.maximum(jnp.maximum(ls[0], ls[1]), ls[2])
        tot = m + jnp.log(jnp.exp(ls[0] - m) + jnp.exp(ls[1] - m) + jnp.exp(ls[2] - m))
        lse_ref[...] = tot
        ws = [jnp.exp(l - tot) for l in ls]
        for h in range(H):
            sl = slice(h * HEAD_DIM, (h + 1) * HEAD_DIM)
            out = (ws[0][:, h:h + 1] * o0[:, sl] + ws[1][:, h:h + 1] * n1[h]) + ws[2][:, h:h + 1] * n2[h]
            of_ref[:, sl] = out
            ob_ref[:, sl] = out.astype(BF16)

    def wide(d, w):
        return pl.BlockSpec((tm // d, d * w), lambda i: (i, 0))

    row = pl.BlockSpec((tm, HD), lambda i: (i, 0))
    nar = pl.BlockSpec((tm, 128), lambda i: (i, 0))
    return pl.pallas_call(
        body, name=name, grid=(T // tm,),
        in_specs=[row, wide(dils[0], HD), wide(dils[1], HD), nar, wide(dils[0], 128), wide(dils[1], 128)],
        out_specs=[row, row, nar],
        out_shape=[jax.ShapeDtypeStruct((T, HD), BF16), jax.ShapeDtypeStruct((T, HD), F32),
                   jax.ShapeDtypeStruct((T, 128), F32)],
        scratch_shapes=[pltpu.VMEM((H, tm, HEAD_DIM), F32), pltpu.VMEM((H, tm, HEAD_DIM), F32),
                        pltpu.VMEM((1, tm, 128), F32), pltpu.VMEM((1, tm, 128), F32)],
        compiler_params=_params("parallel"),
    )(*os, *lses)


def _attn_delta(out, dout, lse, name):
    T, HD = out.shape
    H = HD // HEAD_DIM
    tm = min(T, STREAM_TILE)
    dils = ATTN_DILATIONS[1:]

    def body(o_ref, d_ref, l_ref, dl_ref, *rest):
        dch = rest[-1]
        lane = lax.broadcasted_iota(I32, (tm, 128), 1)
        tile = jnp.zeros((tm, 128), F32)
        for h in range(H):
            sl = slice(h * HEAD_DIM, (h + 1) * HEAD_DIM)
            dch[h] = d_ref[:, sl]
            tile = jnp.where(lane == h, jnp.sum(o_ref[:, sl] * d_ref[:, sl], axis=-1, keepdims=True), tile)
        dl_ref[...] = tile
        for i, d in enumerate(dils):
            dw_ref, lw_ref, dlw_ref = rest[3 * i:3 * i + 3]
            for s in range(d):
                rows = pl.ds(s, tm // d, stride=d)
                for h in range(H):
                    dw_ref[:, s * HD + h * HEAD_DIM:s * HD + (h + 1) * HEAD_DIM] = dch.at[h][rows, :].astype(BF16)
                lw_ref[:, s * 128:(s + 1) * 128] = l_ref[rows, :]
                dlw_ref[:, s * 128:(s + 1) * 128] = dl_ref[rows, :]

    row = pl.BlockSpec((tm, HD), lambda i: (i, 0))
    nar = pl.BlockSpec((tm, 128), lambda i: (i, 0))
    out_specs, out_shape = [nar], [jax.ShapeDtypeStruct((T, 128), F32)]
    for d in dils:
        out_specs += [pl.BlockSpec((tm // d, d * w), lambda i: (i, 0)) for w in (HD, 128, 128)]
        out_shape += [jax.ShapeDtypeStruct((T // d, d * HD), BF16), jax.ShapeDtypeStruct((T // d, d * 128), F32),
                      jax.ShapeDtypeStruct((T // d, d * 128), F32)]
    return pl.pallas_call(
        body, name=name, grid=(T // tm,), in_specs=[row, row, nar], out_specs=out_specs, out_shape=out_shape,
        scratch_shapes=[pltpu.VMEM((H, tm, HEAD_DIM), F32)],
        compiler_params=_params("parallel"),
    )(out, dout, lse)


def _attn_bwd(qkv, dout, lse, delta, qg, kg, g, name):
    d = ATTN_DILATIONS[g]
    L = qkv.shape[0]
    HD = qkv.shape[1] // (3 * d)
    H = HD // HEAD_DIM
    nb = L // ATTN_BLOCK
    total = d * nb
    scale = HEAD_DIM ** -0.5

    def body(*refs):
        (q_ref, kc_ref, vc_ref, do_ref, lse_ref, dl_ref, qg_ref, kg_ref,
         out_ref, gg_ref, dq_a, dk_a, dv_a, dq_b, dk_b, dv_b, kk, vv, khh, rkk) = refs
        t = pl.program_id(0)
        n = t % nb
        gq = qg_ref[g:g + 1, :]
        gk = kg_ref[g:g + 1, :]
        _carry_previous(t, [kk, vv, khh, rkk])

        @pl.when(t == 0)
        def _():
            gg_ref[...] = jnp.zeros_like(gg_ref)
            dk_a[...] = jnp.zeros_like(dk_a)
            dv_a[...] = jnp.zeros_like(dv_a)

        @pl.when(t < total)
        def _():
            mask = _band_mask(n)
            vv[ATTN_BLOCK:, :] = vc_ref[...]
            gq_sum = jnp.zeros((1, HEAD_DIM), F32)
            for h in range(H):
                sl = slice(h * HEAD_DIM, (h + 1) * HEAD_DIM)
                qn, qh, rq = _head_norm(q_ref[:, sl], gq)
                kcn, kch, rkc = _head_norm(kc_ref[:, sl], gk)
                kk[ATTN_BLOCK:, sl] = kcn
                khh[ATTN_BLOCK:, sl] = kch
                rkk[ATTN_BLOCK:, sl] = jnp.broadcast_to(rkc, (ATTN_BLOCK, HEAD_DIM))
                kn = kk[:, sl]
                p = jnp.where(mask, jnp.exp(_dot_nt(qn, kn) * scale - lse_ref[:, h:h + 1]), 0.0)
                do = do_ref[:, sl].astype(BF16)
                ds = (p * (_dot_nt(do, vv[:, sl]) - dl_ref[:, h:h + 1]) * scale).astype(BF16)
                dqn = _dot_nn(ds, kn)
                dqh = dqn * gq
                dq_b[:, sl] = rq * (dqh - qh * jnp.mean(dqh * qh, axis=-1, keepdims=True))
                gq_sum = gq_sum + jnp.sum(dqn * qh, axis=0, keepdims=True)
                dkn = _dot_tn(ds, qn)
                dvn = _dot_tn(p.astype(BF16), do)
                dk_a[:, sl] += dkn[0:ATTN_BLOCK]
                dv_a[:, sl] += dvn[0:ATTN_BLOCK]
                dk_b[:, sl] = dkn[ATTN_BLOCK:]
                dv_b[:, sl] = dvn[ATTN_BLOCK:]
            gg_ref[0:1, :] += gq_sum

        @pl.when(t > 0)
        def _():
            out_ref[:, 0:HD] = dq_a[...].astype(BF16)
            gk_sum = jnp.zeros((1, HEAD_DIM), F32)
            for h in range(H):
                sl = slice(h * HEAD_DIM, (h + 1) * HEAD_DIM)
                kh = khh[0:ATTN_BLOCK, sl]
                rk = rkk[0:ATTN_BLOCK, sl]
                dkn = dk_a[:, sl]
                dkh = dkn * gk
                dk = rk * (dkh - kh * jnp.mean(dkh * kh, axis=-1, keepdims=True))
                out_ref[:, HD + h * HEAD_DIM:HD + (h + 1) * HEAD_DIM] = dk.astype(BF16)
                gk_sum = gk_sum + jnp.sum(dkn * kh, axis=0, keepdims=True)
            out_ref[:, 2 * HD:3 * HD] = dv_a[...].astype(BF16)
            gg_ref[1:2, :] += gk_sum

        @pl.when(t < total)
        def _():
            dq_a[...] = dq_b[...]
            dk_a[...] = dk_b[...]
            dv_a[...] = dv_b[...]

    def at(t):
        t = jnp.minimum(t, total - 1)
        return t % nb, t // nb

    def blk(which):
        return pl.BlockSpec((ATTN_BLOCK, HD), lambda t: (at(t)[0], at(t)[1] * 3 + which))

    gains = pl.BlockSpec((3, HEAD_DIM), lambda t: (0, 0))
    nar = pl.BlockSpec((ATTN_BLOCK, 128), lambda t: at(t))
    acc = pltpu.VMEM((ATTN_BLOCK, HD), F32)
    pair16 = pltpu.VMEM((2 * ATTN_BLOCK, HD), BF16)
    pair32 = pltpu.VMEM((2 * ATTN_BLOCK, HD), F32)
    return pl.pallas_call(
        body, name=name, grid=(total + 1,),
        in_specs=[blk(0), blk(1), blk(2), pl.BlockSpec((ATTN_BLOCK, HD), lambda t: at(t)), nar, nar, gains, gains],
        out_specs=[pl.BlockSpec((ATTN_BLOCK, 3 * HD), lambda t: at(jnp.maximum(t - 1, 0))),
                   pl.BlockSpec((8, HEAD_DIM), lambda t: (0, 0))],
        out_shape=[jax.ShapeDtypeStruct((L, d * 3 * HD), BF16), jax.ShapeDtypeStruct((8, HEAD_DIM), F32)],
        scratch_shapes=[acc, acc, acc, acc, acc, acc, pair16, pair16, pair32, pair32],
        compiler_params=_params("arbitrary"),
    )(qkv, qkv, qkv, dout, lse, delta, qg, kg)


def _adamw(w, grad, m, v, name):
    if w.ndim == 2:
        R, C = w.shape
        tr = _row_tile(R, 512)
        blk, steps = pl.BlockSpec((tr, C), lambda i: (i, 0)), R // tr
    else:
        zeros = (0,) * w.ndim
        blk, steps = pl.BlockSpec(w.shape, lambda i: zeros), 1

    def body(w_ref, g_ref, m_ref, v_ref, d_ref, nm_ref, nv_ref):
        gv = g_ref[...]
        nm = ADAM_B1 * m_ref[...] + (1.0 - ADAM_B1) * gv
        nv = ADAM_B2 * v_ref[...] + (1.0 - ADAM_B2) * (gv * gv)
        m_hat = nm / (1.0 - ADAM_B1 ** ADAM_STEP)
        v_hat = nv / (1.0 - ADAM_B2 ** ADAM_STEP)
        d_ref[...] = -ADAM_LR * (m_hat / (jnp.sqrt(v_hat) + ADAM_EPS) + ADAM_WD * w_ref[...])
        nm_ref[...] = nm
        nv_ref[...] = nv

    shp = jax.ShapeDtypeStruct(w.shape, F32)
    return pl.pallas_call(
        body, name=name, grid=(steps,), in_specs=[blk] * 4, out_specs=[blk] * 3, out_shape=[shp] * 3,
        compiler_params=_params("parallel"),
    )(w, grad, m, v)


def _ffn_fwd(x, h, comm, s, tag, tgt=None, next_g=None):
    (win_t, *rest), tok = comm.weights(s, h)
    z3, a = _gated_in(h, win_t, None, "swiglu", BF16, f"ffn_in_{tag}", after=tok)
    tok = ()
    if not rest:
        rest, tok = comm.second(s, a)
    wout, = rest
    saved = (x, h, z3, a, win_t, wout)
    if tgt is None:
        return _rows_mm(a, wout, "res", f"ffn_out_{tag}", x=x, scale=0.5, g=next_g, after=tok), saved
    return _rows_mm(a, wout, "loss", f"ffn_out_loss_{tag}", x=x, scale=0.5, tgt=tgt), saved


def _ffn_bwd(dxo, saved, g, comm, s, tag, after):
    x, h, z3, a, win_t, wout = saved
    dz3 = _swiglu_bwd(dxo, wout, z3, 0.5, f"ffn_out_bwd_{tag}", after=after)
    d_wout = _mm_tn(a, dxo, 0.5, f"ffn_dwout_{tag}")
    d_win_t = _mm_tn(dz3, h, 1.0, f"ffn_dwin_{tag}")
    tok = comm.grads(s, [d_win_t, d_wout])
    dx, dg8 = _rows_mm(dz3, win_t, "rmsbwd", f"ffn_in_bwd_{tag}", x=x, g=g, dxo=dxo, after=tok)
    return dx, dg8, ()


def _local_step(x, tgt, vecs, comm):
    norm_g = vecs["norm_g"]

    def gvec(i):
        return norm_g[i:i + 1]

    h0 = _rmsnorm(x, gvec(0), "rmsnorm_l0a", after=comm.started)
    (x1, h_c), sv_a0 = _ffn_fwd(x, h0, comm, 0, "l0a", next_g=gvec(1))
    (pw1_t, pw2), tok = comm.weights(1, h_c)
    zc3, u = _gated_in(h_c, pw1_t, vecs["b_pw1"], "glu", F32, "conv_pw1", after=tok)
    v, s = _dwconv_fwd(u, vecs["w_dw"], vecs["b_dw"], vecs["cg"], "conv_dw")
    x2, h_b0 = _rows_mm(s, pw2, "res", "conv_pw2", x=x1, bias=vecs["b_pw2"], g=gvec(2))
    (x3, h_a1), sv_b0 = _ffn_fwd(x2, h_b0, comm, 2, "l0b", next_g=gvec(3))
    x4, sv_a1 = _ffn_fwd(x3, h_a1, comm, 3, "l1a")
    h_s = _rmsnorm_streams(x4, gvec(4), "rmsnorm_attn")
    (qkv_t, wo), tok = comm.weights(4, h_s[0])
    hd3 = qkv_t.shape[0] // 3
    qkvs, os, lses = [], [], []
    for g, d in enumerate(ATTN_DILATIONS):
        qkv = _mm_nt(h_s[g], qkv_t, BF16, f"attn_qkv_g{g}", w_row0=g * hd3, n_rows=hd3, streams=d, after=tok)
        o, lse = _attn_fwd(qkv, vecs["q_norm"], vecs["k_norm"], g, f"attn_fwd_g{g}")
        qkvs.append(qkv)
        os.append(o)
        lses.append(lse)
    out_b, out_f, lse_tot = _attn_merge(os, lses, "attn_merge")
    x5, h_b1 = _rows_mm(out_b, wo, "res", "attn_wo", x=x4, g=gvec(5))
    (dy, loss8), sv_b1 = _ffn_fwd(x5, h_b1, comm, 5, "l1b", tgt=tgt)
    loss = 0.5 * jnp.sum(loss8) / x.shape[1]

    dg = [None] * 6
    dx, dg[5], tok = _ffn_bwd(dy, sv_b1, gvec(5), comm, 5, "l1b", ())
    dout = _mm_nt(dx, wo, F32, "attn_wo_bwd", after=tok)
    d_wo = _mm_tn(out_b, dx, 1.0, "attn_dwo")
    delta, *wide = _attn_delta(out_f, dout, lse_tot, "attn_delta")
    per_group = [(dout, lse_tot, delta), tuple(wide[0:3]), tuple(wide[3:6])]
    ggs, d_qkv_t, dhs = [], [], []
    for g, d in enumerate(ATTN_DILATIONS):
        dqkv, gg = _attn_bwd(qkvs[g], *per_group[g], vecs["q_norm"], vecs["k_norm"], g, f"attn_bwd_g{g}")
        ggs.append(gg)
        d_qkv_t.append(_mm_tn(dqkv, h_s[g], 1.0, f"attn_dwqkv_g{g}", streams=d))
        if d > 1:
            dhs.append((d, _rows_mm(dqkv, qkv_t, "plain", f"attn_qkv_bwd_g{g}", b_row0=g * hd3, streams=d)))
        else:
            dqkv0 = dqkv
    tok = comm.grads(4, [jnp.concatenate(d_qkv_t), d_wo])
    dx, dg[4] = _rows_mm(dqkv0, qkv_t, "rmsbwd", "attn_qkv_bwd_g0", x=x4, g=gvec(4), dxo=dx, extras=dhs, after=tok)
    dx, dg[3], tok = _ffn_bwd(dx, sv_a1, gvec(3), comm, 3, "l1a", ())
    dx, dg[2], tok = _ffn_bwd(dx, sv_b0, gvec(2), comm, 2, "l0b", tok)
    ds = _mm_nt(dx, pw2, F32, "conv_pw2_bwd", after=tok)
    d_pw2 = _mm_tn(s, dx, 1.0, "conv_dwpw2")
    dv, dcg8, dbdw8, dbpw2_8 = _conv_post_bwd(ds, v, vecs["cg"], dx, "conv_post_bwd")
    dzc3, d_wdw, db1_16 = _dwconv_bwd(dv, u, vecs["w_dw"], zc3, "conv_dw_bwd")
    d_pw1_t = _mm_tn(dzc3, h_c, 1.0, "conv_dwpw1")
    tok = comm.grads(1, [d_pw1_t, d_pw2])
    dx, dg[1] = _rows_mm(dzc3, pw1_t, "rmsbwd", "conv_pw1_bwd", x=x1, g=gvec(1), dxo=dx, after=tok)
    dx, dg[0], tok = _ffn_bwd(dx, sv_a0, gvec(0), comm, 0, "l0a", ())

    D = x.shape[1]
    small = {
        "norm_g": jnp.stack([p.sum(axis=0) for p in dg]),
        "b_pw1": db1_16.reshape(2, 8, D).sum(axis=1),
        "w_dw": d_wdw[:CONV_WIDTH],
        "b_dw": dbdw8.sum(axis=0, keepdims=True),
        "cg": dcg8.sum(axis=0, keepdims=True),
        "b_pw2": dbpw2_8.sum(axis=0, keepdims=True),
        "q_norm": jnp.stack([gg[0] for gg in ggs]),
        "k_norm": jnp.stack([gg[1] for gg in ggs]),
    }
    return loss, dx, small, tok


def _mesh_pos():
    return lax.axis_index("x"), lax.axis_index("y"), lax.axis_index("c")


def _other_chips(x, y):
    return [(1 - x, y), (x, 1 - y), (1 - x, 1 - y)]


HBM = pl.BlockSpec(memory_space=pltpu.HBM)
SEM = pl.BlockSpec(memory_space=pltpu.SEMAPHORE)
SPLIT_COPY = pltpu.SideEffectType.DATAFLOW_SIDE_EFFECTING


def _hbm(a):
    return pltpu.with_memory_space_constraint(a, pltpu.HBM)


def _gather_copies(j, src, out, send, recv, x, y, c):
    me = 4 * x + 2 * y + c
    peers = [(x, y, 1 - c)] + [(px, py, c) for px, py in _other_chips(x, y)]
    return [pltpu.make_async_remote_copy(src_ref=src, dst_ref=out.at[me], send_sem=send.at[4 * j + k],
                                         recv_sem=recv.at[4 * j + k], device_id=peer, device_id_type=MESH)
            for k, peer in enumerate(peers)]


def _gather_start(shards, me, after, name):
    n = len(shards)
    lands = [lax.dynamic_update_slice(lax.empty((N_DEV,) + s.shape, s.dtype), s[None], (me, 0, 0)) for s in shards]

    def body(*refs):
        ins, outs = refs[:n], refs[2 * n + len(after):3 * n + len(after)]
        send, recv, token = refs[4 * n + len(after):]
        x, y, c = _mesh_pos()
        for j in range(n):
            for cp in _gather_copies(j, ins[j], outs[j], send, recv, x, y, c):
                cp.start()
        token[...] = jnp.zeros_like(token)

    res = pl.pallas_call(
        body, name=name, in_specs=[HBM] * (2 * n) + [ANY] * len(after),
        out_specs=[HBM] * (2 * n) + [SEM, SEM, pl.BlockSpec(memory_space=pltpu.VMEM)],
        out_shape=[pltpu.HBM((N_DEV,) + s.shape, s.dtype) for s in shards] + [pltpu.HBM(s.shape, s.dtype) for s in shards]
        + [pltpu.SemaphoreType.DMA((4 * n,)), pltpu.SemaphoreType.DMA((4 * n,)), jax.ShapeDtypeStruct((8, 128), F32)],
        input_output_aliases={**{i: n + i for i in range(n)}, **{n + i: i for i in range(n)}},
        compiler_params=pltpu.CompilerParams(has_side_effects=SPLIT_COPY),
    )(*[_hbm(s) for s in shards], *[_hbm(l) for l in lands], *after)
    return res[:n], res[n:2 * n], res[2 * n], res[2 * n + 1], res[2 * n + 2]


def _gather_wait(outs, thru, send, recv, after, name):
    n = len(outs)

    def body(*refs):
        out_refs, thru_refs, send_ref, recv_ref = refs[:n], refs[n:2 * n], refs[2 * n], refs[2 * n + 1]
        x, y, c = _mesh_pos()
        for j in range(n):
            for cp in _gather_copies(j, thru_refs[j], out_refs[j], send_ref, recv_ref, x, y, c):
                cp.wait_send()
                cp.wait_recv()

    res = pl.pallas_call(
        body, name=name, in_specs=[HBM] * (2 * n) + [SEM, SEM] + [ANY] * len(after), out_specs=[HBM] * (2 * n),
        out_shape=[pltpu.HBM(a.shape, a.dtype) for a in list(outs) + list(thru)],
        input_output_aliases={i: i for i in range(2 * n)},
        compiler_params=pltpu.CompilerParams(has_side_effects=SPLIT_COPY),
    )(*outs, *thru, send, recv, *after)
    return res[:n]


def _gather_forward(outs, name):
    n = len(outs)

    def body(*refs):
        o = refs[n:2 * n]
        send_sems, recv_sems = refs[2 * n:]
        x, y, c = _mesh_pos()
        copies = []
        for j in range(n):
            for k, (px, py) in enumerate(_other_chips(x, y)):
                blk = o[j].at[4 * px + 2 * py + c]
                cp = pltpu.make_async_remote_copy(src_ref=blk, dst_ref=blk, send_sem=send_sems.at[j, k],
                                                  recv_sem=recv_sems.at[j, k], device_id=(x, y, 1 - c), device_id_type=MESH)
                cp.start()
                copies.append(cp)
        for cp in copies:
            cp.wait()

    return pl.pallas_call(
        body, name=name, in_specs=[ANY] * n, out_specs=[ANY] * n,
        out_shape=[jax.ShapeDtypeStruct(a.shape, a.dtype) for a in outs],
        input_output_aliases={i: i for i in range(n)},
        scratch_shapes=[pltpu.SemaphoreType.DMA((n, 3)), pltpu.SemaphoreType.DMA((n, 3))],
    )(*outs)


def _all_sum(p, name):
    R, C = p.shape

    def body(p_ref, o_ref, buf, send_sems, recv_sems):
        x, y, c = _mesh_pos()
        me = 4 * x + 2 * y + c
        buf[me] = p_ref[...]
        copies = []
        for rel in range(1, N_DEV):
            peer = (1 - x if rel & 4 else x, 1 - y if rel & 2 else y, 1 - c if rel & 1 else c)
            cp = pltpu.make_async_remote_copy(
                src_ref=p_ref, dst_ref=buf.at[me], send_sem=send_sems.at[rel - 1], recv_sem=recv_sems.at[rel - 1],
                device_id=peer, device_id_type=MESH)
            cp.start()
            copies.append(cp)
        for cp in copies:
            cp.wait()
        acc = buf[0]
        for dev in range(1, N_DEV):
            acc = acc + buf[dev]
        o_ref[...] = acc

    vm = pl.BlockSpec(memory_space=pltpu.VMEM)
    return pl.pallas_call(
        body, name=name, in_specs=[vm], out_specs=vm, out_shape=jax.ShapeDtypeStruct((R, C), F32),
        scratch_shapes=[pltpu.VMEM((N_DEV, R, C), F32), pltpu.SemaphoreType.DMA((N_DEV - 1,)),
                        pltpu.SemaphoreType.DMA((N_DEV - 1,))],
    )(p)


def _swap_with_sibling(gs, name):
    n = len(gs)

    def body(*refs):
        ins, outs = refs[:n], refs[n:2 * n]
        send_sems, recv_sems = refs[2 * n:]
        x, y, c = _mesh_pos()
        copies = []
        for i in range(n):
            for k in range(4):
                cp = pltpu.make_async_remote_copy(
                    src_ref=ins[i].at[2 * k + (1 - c)], dst_ref=outs[i].at[k],
                    send_sem=send_sems.at[i, k], recv_sem=recv_sems.at[i, k],
                    device_id=(x, y, 1 - c), device_id_type=MESH)
                cp.start()
                copies.append(cp)
        for cp in copies:
            cp.wait()

    return pl.pallas_call(
        body, name=name, in_specs=[ANY] * n, out_specs=[ANY] * n,
        out_shape=[jax.ShapeDtypeStruct((4,) + g.shape[1:], g.dtype) for g in gs],
        scratch_shapes=[pltpu.SemaphoreType.DMA((n, 4)), pltpu.SemaphoreType.DMA((n, 4))],
    )(*gs)


def _scatter_copies(j, src, land, send, recv, x, y, c):
    return [pltpu.make_async_remote_copy(src_ref=src.at[2 * px + py], dst_ref=land.at[k], send_sem=send.at[3 * j + k],
                                         recv_sem=recv.at[3 * j + k], device_id=(px, py, c), device_id_type=MESH)
            for k, (px, py) in enumerate(_other_chips(x, y))]


def _scatter_start(ps, name):
    n = len(ps)

    def body(*refs):
        ins, lands = refs[:n], refs[2 * n:3 * n]
        send, recv, token = refs[3 * n:]
        x, y, c = _mesh_pos()
        for j in range(n):
            for cp in _scatter_copies(j, ins[j], lands[j], send, recv, x, y, c):
                cp.start()
        token[...] = jnp.zeros_like(token)

    res = pl.pallas_call(
        body, name=name, in_specs=[HBM] * n,
        out_specs=[HBM] * (2 * n) + [SEM, SEM, pl.BlockSpec(memory_space=pltpu.VMEM)],
        out_shape=[pltpu.HBM(p.shape, p.dtype) for p in ps] + [pltpu.HBM((3,) + p.shape[1:], p.dtype) for p in ps]
        + [pltpu.SemaphoreType.DMA((3 * n,)), pltpu.SemaphoreType.DMA((3 * n,)), jax.ShapeDtypeStruct((8, 128), F32)],
        input_output_aliases={i: i for i in range(n)},
        compiler_params=pltpu.CompilerParams(has_side_effects=SPLIT_COPY),
    )(*[_hbm(p) for p in ps])
    return res[:n], res[n:2 * n], res[2 * n], res[2 * n + 1], res[2 * n + 2]


def _scatter_wait(thru, lands, send, recv, after, name):
    n = len(thru)

    def body(*refs):
        thru_refs, land_refs, send_ref, recv_ref = refs[:n], refs[n:2 * n], refs[2 * n], refs[2 * n + 1]
        x, y, c = _mesh_pos()
        for j in range(n):
            for cp in _scatter_copies(j, thru_refs[j], land_refs[j], send_ref, recv_ref, x, y, c):
                cp.wait_send()
                cp.wait_recv()

    res = pl.pallas_call(
        body, name=name, in_specs=[HBM] * (2 * n) + [SEM, SEM, ANY], out_specs=[HBM] * (2 * n),
        out_shape=[pltpu.HBM(a.shape, a.dtype) for a in list(thru) + list(lands)],
        input_output_aliases={i: i for i in range(2 * n)},
        compiler_params=pltpu.CompilerParams(has_side_effects=SPLIT_COPY),
    )(*thru, *lands, send, recv, after)
    return res[:n], res[n:]


def _row_tile(r, pref):
    if r <= pref:
        return r
    for t in range(pref - pref % 16, 0, -16):
        if r % t == 0:
            return t
    return r


def _add_sibling(g, r1, pos, name):
    _, r, D = g.shape
    tr = _row_tile(r, 512)

    def body(pos_ref, g_ref, r_ref, o_ref):
        o_ref[...] = (g_ref[...].astype(F32) + r_ref[...].astype(F32)).astype(BF16)

    return pl.pallas_call(
        body, name=name,
        grid_spec=pltpu.PrefetchScalarGridSpec(
            num_scalar_prefetch=1, grid=(4, r // tr),
            in_specs=[pl.BlockSpec((None, None, tr, D), lambda k, j, pos: (k, pos[0], j, 0)),
                      pl.BlockSpec((None, tr, D), lambda k, j, pos: (k, j, 0))],
            out_specs=pl.BlockSpec((None, tr, D), lambda k, j, pos: (k, j, 0))),
        out_shape=jax.ShapeDtypeStruct((4, r, D), BF16),
        compiler_params=_params("parallel", "parallel"),
    )(pos, g.reshape(4, 2, r, D), r1)


def _add_chips(p, r2, pos, name):
    _, r, D = p.shape
    tr = _row_tile(r, 512)

    def body(pos_ref, p_ref, r_ref, o_ref):
        acc = p_ref[...].astype(F32)
        for j in range(3):
            acc = acc + r_ref[j].astype(F32)
        o_ref[...] = acc

    return pl.pallas_call(
        body, name=name,
        grid_spec=pltpu.PrefetchScalarGridSpec(
            num_scalar_prefetch=1, grid=(r // tr,),
            in_specs=[pl.BlockSpec((None, tr, D), lambda j, pos: (pos[1], j, 0)),
                      pl.BlockSpec((3, tr, D), lambda j, pos: (0, j, 0))],
            out_specs=pl.BlockSpec((tr, D), lambda j, pos: (j, 0))),
        out_shape=jax.ShapeDtypeStruct((r, D), F32),
        compiler_params=_params("parallel"),
    )(pos, p, r2)


def _place_cols(local, me, width):
    R, w = local.shape
    return lax.dynamic_update_slice(jnp.zeros((R, width), F32), local, (0, me * w))


def _pad_rows(a, rows):
    return jnp.pad(a, ((0, rows - a.shape[0]), (0, 0)))


SUBLAYER_SHARDS = ((0, 4), (8, 9), (1, 5), (2, 6), (10, 11), (3, 7))
GATHER_UNITS = ((0,), (4,), (8, 9), (1, 5), (2, 6), (10, 11), (3, 7))
SUBLAYER_UNITS = ((0, 1), (2,), (3,), (4,), (5,), (6,))
GATHERS_AHEAD = 3


class _StepComm:
    def __init__(self, shards, pos):
        self.pos = pos
        self.shards = shards
        self.gathers = {}
        tok = ()
        for u in range(GATHERS_AHEAD):
            tok = self._start_gather(u, tok)
        self.started = tok
        self.next_unit = GATHERS_AHEAD
        self.pending = {}

    def _start_gather(self, u, after):
        outs, thru, send, recv, token = _gather_start([self.shards[i] for i in GATHER_UNITS[u]],
                                                      2 * self.pos[1] + self.pos[0], after, f"gather_start_{u}")
        self.gathers[u] = (outs, thru, send, recv)
        return (token,)

    def _take(self, u, after):
        outs = _gather_wait(*self.gathers.pop(u), (after,), f"gather_wait_{u}")
        outs = _gather_forward(outs, f"gather_forward_{u}")
        tok = ()
        if self.next_unit < len(GATHER_UNITS):
            tok = self._start_gather(self.next_unit, (outs[0],))
            self.next_unit += 1
        return [w.reshape(-1, w.shape[-1]) for w in outs], tok

    def weights(self, s, after):
        return self._take(SUBLAYER_UNITS[s][0], after)

    def second(self, s, after):
        return self._take(SUBLAYER_UNITS[s][1], after)

    def grads(self, s, mats):
        g3 = [g.reshape(N_DEV, g.shape[0] // N_DEV, g.shape[1]) for g in mats]
        r1 = _swap_with_sibling(g3, f"rs_sibling_{s}")
        ps = [_add_sibling(g, r, self.pos, f"rs_add_sibling_{s}_{i}") for i, (g, r) in enumerate(zip(g3, r1))]
        thru, lands, send, recv, token = _scatter_start(ps, f"rs_start_{s}")
        self.pending[s] = (thru, lands, send, recv)
        return (token,)

    def finish(self, after):
        out = {}
        for s in sorted(self.pending, reverse=True):
            ps, lands = _scatter_wait(*self.pending[s], after, f"rs_wait_{s}")
            for i, p, r2 in zip(SUBLAYER_SHARDS[s], ps, lands):
                out[i] = _add_chips(p, r2, self.pos, f"rs_add_chips_{i}")
        return out


def kernel(x, norm_g, ffn_w_in, ffn_w_out, conv_w_pw1, conv_b_pw1, conv_w_dw, conv_b_dw, conv_norm_g, conv_w_pw2, conv_b_pw2, attn_w_qkv, attn_q_norm, attn_k_norm, attn_w_o, loss_target, m_norm_g, m_ffn_w_in, m_ffn_w_out, m_conv_w_pw1, m_conv_b_pw1, m_conv_w_dw, m_conv_b_dw, m_conv_norm_g, m_conv_w_pw2, m_conv_b_pw2, m_attn_w_qkv, m_attn_q_norm, m_attn_k_norm, m_attn_w_o, v_norm_g, v_ffn_w_in, v_ffn_w_out, v_conv_w_pw1, v_conv_b_pw1, v_conv_w_dw, v_conv_b_dw, v_conv_norm_g, v_conv_w_pw2, v_conv_b_pw2, v_attn_w_qkv, v_attn_q_norm, v_attn_k_norm, v_attn_w_o):
    T, D = x.shape[1], x.shape[2]
    xi, yi, ci = _mesh_pos()
    me = 4 * xi + 2 * yi + ci
    pos = jnp.stack([ci, 2 * xi + yi]).astype(I32)
    lf = [(l, f) for l in range(2) for f in range(2)]

    shards = ([ffn_w_in[l, f].T.astype(BF16) for l, f in lf] + [ffn_w_out[l, f].astype(BF16) for l, f in lf]
              + [conv_w_pw1[0].T.astype(BF16), conv_w_pw2[0].astype(BF16),
                 attn_w_qkv[0].T.astype(BF16), attn_w_o[0].astype(BF16)])
    comm = _StepComm(shards, pos)
    w_cols = norm_g.shape[2]
    vec = _all_sum(_pad_rows(jnp.concatenate([_place_cols(norm_g.reshape(6, w_cols), me, D),
                                              _place_cols(conv_w_dw[0], me, D)]), 40), "gather_vectors")
    vecs = {
        "norm_g": vec[0:6], "w_dw": vec[6:6 + CONV_WIDTH],
        "b_pw1": conv_b_pw1.reshape(2, D), "b_dw": conv_b_dw, "cg": conv_norm_g, "b_pw2": conv_b_pw2,
        "q_norm": attn_q_norm[0], "k_norm": attn_k_norm[0],
    }

    loss_local, dx, small, tok = _local_step(x[0], loss_target[0], vecs, comm)
    loss = lax.psum(loss_local, ("x", "y", "c"))

    mats = comm.finish(dx)
    hd3 = 3 * HEAD_DIM
    packed = jnp.concatenate([
        small["norm_g"], small["b_pw1"], small["w_dw"], small["b_dw"], small["cg"], small["b_pw2"],
        jnp.pad(small["q_norm"].reshape(1, hd3), ((0, 0), (0, D - hd3))),
        jnp.pad(small["k_norm"].reshape(1, hd3), ((0, 0), (0, D - hd3)))])
    red = _all_sum(_pad_rows(packed, 48), "reduce_vectors")
    col0 = me * w_cols
    grads = {
        "norm_g": lax.dynamic_slice(red[0:6], (0, col0), (6, w_cols)),
        "ffn_w_in": jnp.concatenate([mats[i].T for i in range(4)]),
        "ffn_w_out": jnp.concatenate([mats[i] for i in range(4, 8)]),
        "conv_w_pw1": mats[8].T,
        "conv_b_pw1": red[6:8].reshape(1, 2 * D),
        "conv_w_dw": lax.dynamic_slice(red[8:8 + CONV_WIDTH], (0, col0), (CONV_WIDTH, w_cols)),
        "conv_b_dw": red[39:40], "conv_norm_g": red[40:41], "conv_w_pw2": mats[9], "conv_b_pw2": red[41:42],
        "attn_w_qkv": mats[10].T,
        "attn_q_norm": red[42, :hd3].reshape(3, HEAD_DIM), "attn_k_norm": red[43, :hd3].reshape(3, HEAD_DIM),
        "attn_w_o": mats[11],
    }

    names = ["norm_g", "ffn_w_in", "ffn_w_out", "conv_w_pw1", "conv_b_pw1", "conv_w_dw", "conv_b_dw",
             "conv_norm_g", "conv_w_pw2", "conv_b_pw2", "attn_w_qkv", "attn_q_norm", "attn_k_norm", "attn_w_o"]
    ws = [norm_g, ffn_w_in, ffn_w_out, conv_w_pw1, conv_b_pw1, conv_w_dw, conv_b_dw, conv_norm_g, conv_w_pw2,
          conv_b_pw2, attn_w_qkv, attn_q_norm, attn_k_norm, attn_w_o]
    ms = [m_norm_g, m_ffn_w_in, m_ffn_w_out, m_conv_w_pw1, m_conv_b_pw1, m_conv_w_dw, m_conv_b_dw, m_conv_norm_g,
          m_conv_w_pw2, m_conv_b_pw2, m_attn_w_qkv, m_attn_q_norm, m_attn_k_norm, m_attn_w_o]
    vs = [v_norm_g, v_ffn_w_in, v_ffn_w_out, v_conv_w_pw1, v_conv_b_pw1, v_conv_w_dw, v_conv_b_dw, v_conv_norm_g,
          v_conv_w_pw2, v_conv_b_pw2, v_attn_w_qkv, v_attn_q_norm, v_attn_k_norm, v_attn_w_o]
    g_out, d_out, m_out, v_out = [], [], [], []
    for name, w, m, v in zip(names, ws, ms, vs):
        shape2 = (-1, w.shape[-1]) if w.size > SMALL_PARAM else w.shape
        g2 = grads[name].reshape(shape2)
        delta, new_m, new_v = _adamw(w.reshape(shape2), g2, m.reshape(shape2), v.reshape(shape2), f"adamw_{name}")
        g_out.append(g2.reshape(w.shape))
        d_out.append(delta.reshape(w.shape))
        m_out.append(new_m.reshape(w.shape))
        v_out.append(new_v.reshape(w.shape))
    return (loss, dx.reshape(x.shape), *g_out, *d_out, *m_out, *v_out)
```

```python
import functools

import jax
import jax.numpy as jnp
from jax import lax
from jax.experimental import pallas as pl
from jax.experimental.pallas import tpu as pltpu

F32 = jnp.float32
BF16 = jnp.bfloat16
I32 = jnp.int32

NORM_EPS = 1e-6
LANES = 128
SUBLANES = 8
MXU_WIDTH = 256
HEAD_DIM = 128
ATTN_BLOCK = 128
ATTN_DILATIONS = (1, 4, 16)
CONV_WIDTH = 31
CONV_HALO = 32
CONV_CHUNK = 16
WHOLE_K_MAX = 6144
MM_TN_TOKENS = 2048
STREAM_TILE = 512
ADAM_LR, ADAM_B1, ADAM_B2, ADAM_EPS, ADAM_WD, ADAM_STEP = 0.001, 0.9, 0.999, 1e-08, 0.01, 10
N_DEV = 8
SMALL_PARAM = 8192
NEG_BIG = -1e30
V7X_VMEM_BYTES = 64 * 1024 * 1024
VMEM_LIMIT = V7X_VMEM_BYTES - 8 * 1024 * 1024
MESH = pl.DeviceIdType.MESH
ANY = pl.BlockSpec(memory_space=pl.ANY)


def _params(*sem):
    return pltpu.CompilerParams(dimension_semantics=sem or None, vmem_limit_bytes=VMEM_LIMIT)


def _dot_nn(a, b):
    return lax.dot_general(a, b, (((1,), (0,)), ((), ())), preferred_element_type=F32)


def _dot_nt(a, b):
    return lax.dot_general(a, b, (((1,), (1,)), ((), ())), preferred_element_type=F32)


def _dot_tn(a, b):
    return lax.dot_general(a, b, (((0,), (0,)), ((), ())), preferred_element_type=F32)


def _sigmoid(x):
    return 1.0 / (1.0 + jnp.exp(-x))


def _tile(n, pref):
    if n <= pref:
        return n
    for t in range(pref - pref % 128, 0, -128):
        if n % t == 0:
            return t
    return n


def _to_lane_chunks(dst, val):
    for c in range(dst.shape[0]):
        dst[c] = val[:, c * LANES:(c + 1) * LANES]


def _rows8(v):
    tm, c = v.shape
    return v.reshape(tm // 8, 8, c).sum(axis=0)


def _rmsnorm(x, g, name, after=()):
    T, D = x.shape
    tm = min(T, 512)

    def body(x_ref, g_ref, *rest):
        h_ref = rest[-1]
        xv = x_ref[...]
        r = lax.rsqrt(jnp.mean(xv * xv, axis=-1, keepdims=True) + NORM_EPS)
        h_ref[...] = (xv * r * g_ref[...]).astype(BF16)

    return pl.pallas_call(
        body, name=name, grid=(T // tm,),
        in_specs=[pl.BlockSpec((tm, D), lambda i: (i, 0)), pl.BlockSpec((1, D), lambda i: (0, 0))] + [ANY] * len(after),
        out_specs=pl.BlockSpec((tm, D), lambda i: (i, 0)),
        out_shape=jax.ShapeDtypeStruct((T, D), BF16),
        compiler_params=_params("parallel"),
    )(x, g, *after)


def _rmsnorm_streams(x, g, name):
    T, D = x.shape
    tm = min(T, STREAM_TILE)
    dils = [d for d in ATTN_DILATIONS if d > 1]

    def body(x_ref, g_ref, h_ref, *rest):
        outs, hs = rest[:-1], rest[-1]
        xv = x_ref[...]
        r = lax.rsqrt(jnp.mean(xv * xv, axis=-1, keepdims=True) + NORM_EPS)
        hf = xv * r * g_ref[...]
        h_ref[...] = hf.astype(BF16)
        _to_lane_chunks(hs, hf)
        for d, o_ref in zip(dils, outs):
            for s in range(d):
                for c in range(D // LANES):
                    o_ref[s, :, c * LANES:(c + 1) * LANES] = hs.at[c][pl.ds(s, tm // d, stride=d), :].astype(BF16)

    res = pl.pallas_call(
        body, name=name, grid=(T // tm,),
        in_specs=[pl.BlockSpec((tm, D), lambda i: (i, 0)), pl.BlockSpec((1, D), lambda i: (0, 0))],
        out_specs=[pl.BlockSpec((tm, D), lambda i: (i, 0))]
        + [pl.BlockSpec((d, tm // d, D), lambda i: (0, i, 0)) for d in dils],
        out_shape=[jax.ShapeDtypeStruct((T, D), BF16)] + [jax.ShapeDtypeStruct((d, T // d, D), BF16) for d in dils],
        scratch_shapes=[pltpu.VMEM((D // LANES, tm, LANES), F32)],
        compiler_params=_params("parallel"),
    )(x, g)
    return [res[0]] + [a.reshape(T, D) for a in res[1:]]


def _gated_in(h, wt, bias, act, u_dtype, name, after=()):
    T, D = h.shape
    F = wt.shape[0] // 2
    tm = min(T, 512)

    def body(*refs):
        z_ref, u_ref = refs[-2:]
        h_ref, w_ref = refs[:2]
        hv = h_ref[...]
        for c0 in range(0, F, MXU_WIDTH):
            c1 = min(c0 + MXU_WIDTH, F)
            z0 = _dot_nt(hv, w_ref[c0:c1, :])
            z1 = _dot_nt(hv, w_ref[F + c0:F + c1, :])
            if bias is not None:
                z0 = z0 + refs[2][0:1, c0:c1]
                z1 = z1 + refs[2][1:2, c0:c1]
            z_ref[0, :, c0:c1] = z0.astype(BF16)
            z_ref[1, :, c0:c1] = z1.astype(BF16)
            if act == "swiglu":
                u = z0 * _sigmoid(z0) * z1
            else:
                u = z0 * _sigmoid(z1)
            u_ref[:, c0:c1] = u.astype(u_dtype)

    in_specs = [pl.BlockSpec((tm, D), lambda m: (m, 0)), pl.BlockSpec((2 * F, D), lambda m: (0, 0))]
    args = [h, wt]
    if bias is not None:
        in_specs.append(pl.BlockSpec((2, F), lambda m: (0, 0)))
        args.append(bias)
    in_specs += [ANY] * len(after)
    args += list(after)
    return pl.pallas_call(
        body, name=name, grid=(T // tm,), in_specs=in_specs,
        out_specs=[pl.BlockSpec((2, tm, F), lambda m: (0, m, 0)), pl.BlockSpec((tm, F), lambda m: (m, 0))],
        out_shape=[jax.ShapeDtypeStruct((2, T, F), BF16), jax.ShapeDtypeStruct((T, F), u_dtype)],
        compiler_params=_params("parallel"),
    )(*args)


def _swiglu_bwd(dxo, wout, z3, scale, name, after=()):
    T, D = dxo.shape
    F = wout.shape[0]
    tm = min(T, 512)

    def body(d_ref, w_ref, z_ref, *rest):
        dz_ref = rest[-1]
        dy = (d_ref[...] * scale).astype(BF16)
        for c0 in range(0, F, MXU_WIDTH):
            cs = slice(c0, min(c0 + MXU_WIDTH, F))
            da = _dot_nt(dy, w_ref[cs, :])
            gate = z_ref[0, :, cs].astype(F32)
            up = z_ref[1, :, cs].astype(F32)
            sg = _sigmoid(gate)
            dz_ref[0, :, cs] = (da * up * (sg * (1.0 + gate * (1.0 - sg)))).astype(BF16)
            dz_ref[1, :, cs] = (da * gate * sg).astype(BF16)

    return pl.pallas_call(
        body, name=name, grid=(T // tm,),
        in_specs=[pl.BlockSpec((tm, D), lambda m: (m, 0)),
                  pl.BlockSpec((F, D), lambda m: (0, 0)),
                  pl.BlockSpec((2, tm, F), lambda m: (0, m, 0))] + [ANY] * len(after),
        out_specs=pl.BlockSpec((2, tm, F), lambda m: (0, m, 0)),
        out_shape=jax.ShapeDtypeStruct((2, T, F), BF16),
        compiler_params=_params("parallel"),
    )(dxo, wout, z3, *after)


def _mm_nt(a, wt, out_dtype, name, *, w_row0=0, n_rows=None, streams=1, after=()):
    T, K = a.shape
    N = wt.shape[0] if n_rows is None else n_rows
    L = T // streams
    rows = min(L, STREAM_TILE)
    sp = STREAM_TILE // rows
    mps = L // rows
    assert streams % sp == 0 and w_row0 % N == 0

    def body(a_ref, w_ref, *rest):
        o_ref = rest[-1]
        for s in range(sp):
            a_s = a_ref[s * rows:(s + 1) * rows, :].astype(BF16)
            for c0 in range(0, N, MXU_WIDTH):
                c1 = min(c0 + MXU_WIDTH, N)
                o_ref[:, s * N + c0:s * N + c1] = _dot_nt(a_s, w_ref[c0:c1, :]).astype(out_dtype)

    return pl.pallas_call(
        body, name=name, grid=(T // (rows * sp),),
        in_specs=[pl.BlockSpec((rows * sp, K), lambda m: (m, 0)), pl.BlockSpec((N, K), lambda m: (w_row0 // N, 0))]
        + [ANY] * len(after),
        out_specs=pl.BlockSpec((rows, sp * N), lambda m: (m % mps, m // mps)),
        out_shape=jax.ShapeDtypeStruct((L, streams * N), out_dtype),
        compiler_params=_params("parallel"),
    )(a, wt, *after)


def _mm_tn(a, b, b_scale, name, *, streams=1):
    if a.ndim == 3:
        J, T, F = a.shape
    elif streams > 1:
        J, T, F = 1, a.shape[0] * streams, a.shape[1] // streams
    else:
        (T, F), J = a.shape, 1
    N = b.shape[1]
    tmm = _tile(F, 1408)
    tk = min(T // streams, MM_TN_TOKENS)
    npj = F // tmm
    sp = max(1, MM_TN_TOKENS // tk) if streams > 1 and tk == T // streams else 1
    assert streams % sp == 0
    nk = T // (tk * sp)
    kps = nk // streams if sp == 1 else 1

    def body(*refs):
        a_refs, (b_ref, o_ref, acc_ref) = refs[:sp], refs[sp:]
        k = pl.program_id(1)
        part = None
        for s, a_ref in enumerate(a_refs):
            bv = b_ref[s * tk:(s + 1) * tk, :]
            if b_scale != 1.0:
                bv = bv * b_scale
            dot = _dot_tn(a_ref[...].astype(BF16), bv.astype(BF16))
            part = dot if part is None else part + dot

        @pl.when(k == 0)
        def _():
            acc_ref[...] = part

        @pl.when(k > 0)
        def _():
            acc_ref[...] += part

        @pl.when(k == nk - 1)
        def _():
            o_ref[...] = acc_ref[...].astype(BF16)

    if a.ndim == 3:
        a_specs = [pl.BlockSpec((None, tk, tmm), lambda i, k: (i // npj, k, i % npj))]
    elif sp > 1:
        a_specs = [pl.BlockSpec((tk, tmm), lambda i, k, s=s: (0, (k * sp + s) * npj + i)) for s in range(sp)]
    elif streams > 1:
        a_specs = [pl.BlockSpec((tk, tmm), lambda i, k: (k % kps, (k // kps) * npj + i))]
    else:
        a_specs = [pl.BlockSpec((tk, tmm), lambda i, k: (k, i))]
    return pl.pallas_call(
        body, name=name, grid=(J * npj, nk),
        in_specs=a_specs + [pl.BlockSpec((tk * sp, N), lambda i, k: (k, 0))],
        out_specs=pl.BlockSpec((tmm, N), lambda i, k: (i, 0)),
        out_shape=jax.ShapeDtypeStruct((J * F, N), BF16),
        scratch_shapes=[pltpu.VMEM((tmm, N), F32)],
        compiler_params=_params("parallel", "arbitrary"),
    )(*[a] * sp, b)


def _rows_mm(a, b, mode, name, *, x=None, scale=1.0, bias=None, tgt=None, g=None, dxo=None,
             b_row0=0, streams=1, extras=(), after=()):
    if a.ndim == 3:
        J, T, F = a.shape
    elif streams > 1:
        J, T, F = 1, a.shape[0] * streams, a.shape[1] // streams
    else:
        (T, F), J = a.shape, 1
    D = b.shape[1]
    if streams > 1 and T // streams < STREAM_TILE:
        L = T // streams
        sp = STREAM_TILE // L
        assert mode == "plain" and streams % sp == 0 and b_row0 % F == 0 and not after

        def body_short(a_ref, b_ref, o_ref):
            for s in range(sp):
                o_ref[s * L:(s + 1) * L, :] = _dot_nn(a_ref[:, s * F:(s + 1) * F], b_ref[...])

        return pl.pallas_call(
            body_short, name=name, grid=(streams // sp,),
            in_specs=[pl.BlockSpec((L, sp * F), lambda m: (0, m)), pl.BlockSpec((F, D), lambda m: (b_row0 // F, 0))],
            out_specs=pl.BlockSpec((sp * L, D), lambda m: (m, 0)),
            out_shape=jax.ShapeDtypeStruct((T, D), F32),
            compiler_params=_params("parallel"),
        )(a, b)
    tk = _tile(F, 3072)
    kpj = F // tk
    nk = J * kpj
    tm = min(T // streams, STREAM_TILE)
    whole_k = a.ndim == 3 and kpj == 1 and J > 1 and J * F <= WHOLE_K_MAX
    if whole_k:
        nk, tm = 1, min(T, STREAM_TILE // 2)
    nm = T // tm
    mps = nm // streams
    assert b_row0 % tk == 0
    kb0 = b_row0 // tk
    n_ex = len(extras)
    n_ab = J if whole_k else 1

    def body(*refs):
        refs = list(refs)
        a_refs, b_refs = refs[:n_ab], refs[n_ab:2 * n_ab]
        rest = refs[2 * n_ab + len(after):]
        if n_ex:
            tmp_ref = rest.pop()
        acc_ref = rest.pop()
        ex_refs, rest = rest[:n_ex], rest[n_ex:]
        m, k = pl.program_id(0), pl.program_id(1)
        part = _dot_nn(a_refs[0][...], b_refs[0][...])
        for a_ref, b_ref in zip(a_refs[1:], b_refs[1:]):
            part = part + _dot_nn(a_ref[...], b_ref[...])

        @pl.when(k == 0)
        def _():
            acc_ref[...] = part

        @pl.when(k > 0)
        def _():
            acc_ref[...] += part

        @pl.when(k == nk - 1)
        def _():
            acc = acc_ref[...]
            for (d, _), e_ref in zip(extras, ex_refs):
                for s in range(d):
                    for c in range(D // LANES):
                        tmp_ref.at[c][pl.ds(s, tm // d, stride=d), :] = e_ref[s, :, c * LANES:(c + 1) * LANES]
                acc = acc + jnp.concatenate([tmp_ref[c] for c in range(D // LANES)], axis=1)
            if mode == "plain":
                (o_ref,) = rest
                o_ref[...] = acc
            elif mode == "res":
                ins = list(rest)
                x_ref = ins.pop(0)
                if bias is not None:
                    acc = acc + ins.pop(0)[...]
                xn = x_ref[...] + scale * acc
                if g is None:
                    (o_ref,) = ins
                else:
                    g_ref, o_ref, h_ref = ins
                    r = lax.rsqrt(jnp.mean(xn * xn, axis=-1, keepdims=True) + NORM_EPS)
                    h_ref[...] = (xn * r * g_ref[...]).astype(BF16)
                o_ref[...] = xn
            elif mode == "loss":
                x_ref, t_ref, o_ref, l_ref = rest
                e = x_ref[...] + scale * acc - t_ref[...]
                o_ref[...] = e / D

                @pl.when(m == 0)
                def _():
                    l_ref[...] = jnp.zeros_like(l_ref)

                l_ref[...] += _rows8(e * e)
            else:
                x_ref, g_ref, d_ref, o_ref, dg_ref = rest
                xv = x_ref[...]
                r = lax.rsqrt(jnp.mean(xv * xv, axis=-1, keepdims=True) + NORM_EPS)
                xh = xv * r
                dxh = acc * g_ref[...]
                o_ref[...] = d_ref[...] + r * (dxh - xh * jnp.mean(dxh * xh, axis=-1, keepdims=True))

                @pl.when(m == 0)
                def _():
                    dg_ref[...] = jnp.zeros_like(dg_ref)

                dg_ref[...] += _rows8(acc * xh)

    if whole_k:
        a_specs = [pl.BlockSpec((None, tm, F), lambda m, k, j=j: (j, m, 0)) for j in range(J)]
        b_specs = [pl.BlockSpec((F, D), lambda m, k, j=j: (kb0 + j, 0)) for j in range(J)]
    else:
        if a.ndim == 3:
            a_specs = [pl.BlockSpec((None, tm, tk), lambda m, k: (k // kpj, m, k % kpj))]
        elif streams > 1:
            a_specs = [pl.BlockSpec((tm, tk), lambda m, k: (m % mps, (m // mps) * kpj + k))]
        else:
            a_specs = [pl.BlockSpec((tm, tk), lambda m, k: (m, k))]
        b_specs = [pl.BlockSpec((tk, D), lambda m, k: (kb0 + k, 0))]
    row = pl.BlockSpec((tm, D), lambda m, k: (m, 0))
    vec = pl.BlockSpec((1, D), lambda m, k: (0, 0))
    part8 = pl.BlockSpec((8, D), lambda m, k: (0, 0))
    in_specs = a_specs + b_specs + [ANY] * len(after)
    args = [a] * n_ab + [b] * n_ab + list(after)
    for d, e in extras:
        in_specs.append(pl.BlockSpec((d, tm // d, D), lambda m, k: (0, m, 0)))
        args.append(e.reshape(d, T // d, D))
    full = jax.ShapeDtypeStruct((T, D), F32)
    small = jax.ShapeDtypeStruct((8, D), F32)
    if mode == "plain":
        out_specs, out_shape = row, full
    elif mode == "res":
        in_specs.append(row)
        args.append(x)
        if bias is not None:
            in_specs.append(vec)
            args.append(bias)
        if g is None:
            out_specs, out_shape = row, full
        else:
            in_specs.append(vec)
            args.append(g)
            out_specs, out_shape = [row, row], [full, jax.ShapeDtypeStruct((T, D), BF16)]
    elif mode == "loss":
        in_specs += [row, row]
        args += [x, tgt]
        out_specs, out_shape = [row, part8], [full, small]
    else:
        in_specs += [row, vec, row]
        args += [x, g, dxo]
        out_specs, out_shape = [row, part8], [full, small]
    return pl.pallas_call(
        body, name=name, grid=(nm, nk), in_specs=in_specs, out_specs=out_specs, out_shape=out_shape,
        scratch_shapes=[pltpu.VMEM((tm, D), F32)] + ([pltpu.VMEM((D // LANES, tm, LANES), F32)] if n_ex else []),
        compiler_params=_params("arbitrary", "arbitrary"),
    )(*args)


def _sublane_phases(sh):
    rows = sh.shape[1] - SUBLANES
    for p in range(1, SUBLANES):
        sh[p, pl.ds(0, rows), :] = sh[0, pl.ds(p, rows), :]


def _shifted(sh, offset, j, rows=SUBLANES):
    start = pl.multiple_of(j * rows + (offset - offset % SUBLANES), SUBLANES)
    return sh[offset % SUBLANES, pl.ds(start, rows), :]


def _conv_taps(wb, sh, offsets, j, init):
    groups = CONV_CHUNK // SUBLANES
    accs = [init[g * SUBLANES:(g + 1) * SUBLANES] for g in range(groups)]
    for k, off in enumerate(offsets):
        w = wb[k]
        start = pl.multiple_of(j * CONV_CHUNK + (off - off % SUBLANES), SUBLANES)
        for g in range(groups):
            accs[g] = accs[g] + w * sh[off % SUBLANES, pl.ds(start + g * SUBLANES, SUBLANES), :]
    return jnp.concatenate(accs, axis=0)


def _broadcast_rows(dst, src):
    for r in range(dst.shape[0]):
        dst[r] = jnp.zeros(dst.shape[1:], F32) + src[r:r + 1, :]


def _dwconv_fwd(u, w, b_dw, cg, name):
    T, C = u.shape
    tm = min(T, 256)
    hb = tm // CONV_HALO
    pad = CONV_HALO - (CONV_WIDTH - 1)

    def body(u_ref, halo_ref, w_ref, b_ref, g_ref, v_ref, s_ref, sh, wb):
        i = pl.program_id(0)
        sh[0, pl.ds(0, CONV_HALO), :] = jnp.where(i > 0, halo_ref[...], 0.0)
        sh[0, pl.ds(CONV_HALO, tm), :] = u_ref[...]
        _sublane_phases(sh)
        _broadcast_rows(wb, w_ref)
        bias = jnp.zeros((CONV_CHUNK, C), F32) + b_ref[...]

        def chunk(j, carry):
            acc = _conv_taps(wb, sh, [pad + k for k in range(CONV_WIDTH)], j, bias)
            v_ref[pl.ds(pl.multiple_of(j * CONV_CHUNK, CONV_CHUNK), CONV_CHUNK), :] = acc
            return carry

        lax.fori_loop(0, tm // CONV_CHUNK, chunk, 0)
        acc = v_ref[...]
        r = lax.rsqrt(jnp.mean(acc * acc, axis=-1, keepdims=True) + NORM_EPS)
        n = acc * r * g_ref[...]
        s_ref[...] = (n * _sigmoid(n)).astype(BF16)

    row = pl.BlockSpec((tm, C), lambda i: (i, 0))
    vec = pl.BlockSpec((1, C), lambda i: (0, 0))
    return pl.pallas_call(
        body, name=name, grid=(T // tm,),
        in_specs=[row, pl.BlockSpec((CONV_HALO, C), lambda i: (jnp.maximum(i * hb - 1, 0), 0)),
                  pl.BlockSpec((CONV_WIDTH, C), lambda i: (0, 0)), vec, vec],
        out_specs=[row, row],
        out_shape=[jax.ShapeDtypeStruct((T, C), F32), jax.ShapeDtypeStruct((T, C), BF16)],
        scratch_shapes=[pltpu.VMEM((SUBLANES, CONV_HALO + tm, C), F32), pltpu.VMEM((CONV_WIDTH, SUBLANES, C), F32)],
        compiler_params=_params("parallel"),
    )(u, u, w, b_dw, cg)


def _conv_post_bwd(ds, v, cg, dxo, name):
    T, C = v.shape
    tm = min(T, 512)

    def body(ds_ref, v_ref, g_ref, d_ref, dv_ref, dcg_ref, dbdw_ref, dbpw2_ref):
        @pl.when(pl.program_id(0) == 0)
        def _():
            dcg_ref[...] = jnp.zeros_like(dcg_ref)
            dbdw_ref[...] = jnp.zeros_like(dbdw_ref)
            dbpw2_ref[...] = jnp.zeros_like(dbpw2_ref)

        vv = v_ref[...]
        r = lax.rsqrt(jnp.mean(vv * vv, axis=-1, keepdims=True) + NORM_EPS)
        vh = vv * r
        n = vh * g_ref[...]
        sg = _sigmoid(n)
        dn = ds_ref[...] * (sg * (1.0 + n * (1.0 - sg)))
        dvh = dn * g_ref[...]
        dv = r * (dvh - vh * jnp.mean(dvh * vh, axis=-1, keepdims=True))
        dv_ref[...] = dv
        dcg_ref[...] += _rows8(dn * vh)
        dbdw_ref[...] += _rows8(dv)
        dbpw2_ref[...] += _rows8(d_ref[...])

    row = pl.BlockSpec((tm, C), lambda i: (i, 0))
    part8 = pl.BlockSpec((8, C), lambda i: (0, 0))
    small = jax.ShapeDtypeStruct((8, C), F32)
    return pl.pallas_call(
        body, name=name, grid=(T // tm,),
        in_specs=[row, row, pl.BlockSpec((1, C), lambda i: (0, 0)), row],
        out_specs=[row, part8, part8, part8],
        out_shape=[jax.ShapeDtypeStruct((T, C), F32), small, small, small],
        compiler_params=_params("arbitrary"),
    )(ds, v, cg, dxo)


def _dwconv_bwd(dv, u, w, z3, name):
    T, C = u.shape
    tm = min(T, 256)
    hb = tm // CONV_HALO
    nt = T // tm
    pad = CONV_HALO - (CONV_WIDTH - 1)

    taps_at_once = 4
    chunks_at_once = 8

    def body(dv_ref, dvn_ref, u_ref, up_ref, w_ref, z_ref, dz_ref, dw_ref, db_ref, dvsh, ush, wb, du_ref, dwacc):
        i = pl.program_id(0)

        @pl.when(i == 0)
        def _():
            dwacc[...] = jnp.zeros_like(dwacc)
            db_ref[...] = jnp.zeros_like(db_ref)

        dvsh[0, pl.ds(0, tm), :] = dv_ref[...]
        dvsh[0, pl.ds(tm, CONV_HALO), :] = jnp.where(i < nt - 1, dvn_ref[...], 0.0)
        ush[0, pl.ds(0, CONV_HALO), :] = jnp.where(i > 0, up_ref[...], 0.0)
        ush[0, pl.ds(CONV_HALO, tm), :] = u_ref[...]
        _sublane_phases(dvsh)
        _sublane_phases(ush)
        _broadcast_rows(wb, w_ref)

        def du_chunk(j, carry):
            acc = _conv_taps(wb, dvsh, [CONV_WIDTH - 1 - k for k in range(CONV_WIDTH)], j,
                             jnp.zeros((CONV_CHUNK, C), F32))
            du_ref[pl.ds(pl.multiple_of(j * CONV_CHUNK, CONV_CHUNK), CONV_CHUNK), :] = acc
            return carry

        lax.fori_loop(0, tm // CONV_CHUNK, du_chunk, 0)

        for k0 in range(0, CONV_WIDTH, taps_at_once):
            taps = range(k0, min(k0 + taps_at_once, CONV_WIDTH))

            def dw_chunks(jj, carry, taps=taps):
                accs = [jnp.zeros((SUBLANES, C), F32) for _ in taps]
                for u_ in range(chunks_at_once):
                    j = jj * chunks_at_once + u_
                    dvc = dv_ref[pl.ds(pl.multiple_of(j * SUBLANES, SUBLANES), SUBLANES), :]
                    accs = [acc + dvc * _shifted(ush, pad + k, j) for acc, k in zip(accs, taps)]
                for k, acc in zip(taps, accs):
                    dwacc[k] += acc
                return carry

            lax.fori_loop(0, tm // (SUBLANES * chunks_at_once), dw_chunks, 0)

        du = du_ref[...]
        a = z_ref[0].astype(F32)
        gate = z_ref[1].astype(F32)
        sg = _sigmoid(gate)
        da = du * sg
        dgate = du * a * sg * (1.0 - sg)
        dz_ref[0] = da.astype(BF16)
        dz_ref[1] = dgate.astype(BF16)
        db_ref[0:8, :] += _rows8(da)
        db_ref[8:16, :] += _rows8(dgate)

        @pl.when(i == nt - 1)
        def _():
            dw_ref[...] = jnp.zeros_like(dw_ref)
            for k in range(CONV_WIDTH):
                dw_ref[k:k + 1, :] = jnp.sum(dwacc[k], axis=0, keepdims=True)

    row = pl.BlockSpec((tm, C), lambda i: (i, 0))
    last_halo = T // CONV_HALO - 1
    return pl.pallas_call(
        body, name=name, grid=(nt,),
        in_specs=[row, pl.BlockSpec((CONV_HALO, C), lambda i: (jnp.minimum((i + 1) * hb, last_halo), 0)),
                  row, pl.BlockSpec((CONV_HALO, C), lambda i: (jnp.maximum(i * hb - 1, 0), 0)),
                  pl.BlockSpec((CONV_WIDTH, C), lambda i: (0, 0)),
                  pl.BlockSpec((2, tm, C), lambda i: (0, i, 0))],
        out_specs=[pl.BlockSpec((2, tm, C), lambda i: (0, i, 0)),
                   pl.BlockSpec((CONV_HALO, C), lambda i: (0, 0)),
                   pl.BlockSpec((16, C), lambda i: (0, 0))],
        out_shape=[jax.ShapeDtypeStruct((2, T, C), BF16), jax.ShapeDtypeStruct((CONV_HALO, C), F32),
                   jax.ShapeDtypeStruct((16, C), F32)],
        scratch_shapes=[pltpu.VMEM((SUBLANES, tm + CONV_HALO, C), F32), pltpu.VMEM((SUBLANES, CONV_HALO + tm, C), F32),
                        pltpu.VMEM((CONV_WIDTH, SUBLANES, C), F32), pltpu.VMEM((tm, C), F32),
                        pltpu.VMEM((CONV_WIDTH, SUBLANES, C), F32)],
        compiler_params=_params("arbitrary"),
    )(dv, dv, u, u, w, z3)


def _head_norm(raw, gain):
    xv = raw.astype(F32)
    r = lax.rsqrt(jnp.mean(xv * xv, axis=-1, keepdims=True) + NORM_EPS)
    xh = xv * r
    return (xh * gain).astype(BF16), xh, r


def _band_mask(n):
    row = lax.broadcasted_iota(I32, (ATTN_BLOCK, 2 * ATTN_BLOCK), 0)
    col = lax.broadcasted_iota(I32, (ATTN_BLOCK, 2 * ATTN_BLOCK), 1)
    return (col >= row) & (col <= row + ATTN_BLOCK) & ((col >= ATTN_BLOCK) | (n > 0))


def _carry_previous(n, bufs):
    @pl.when(n == 0)
    def _():
        for b in bufs:
            b[0:ATTN_BLOCK, :] = jnp.zeros((ATTN_BLOCK, b.shape[1]), b.dtype)

    @pl.when(n > 0)
    def _():
        for b in bufs:
            b[0:ATTN_BLOCK, :] = b[ATTN_BLOCK:, :]


def _attn_fwd(qkv, qg, kg, g, name):
    d = ATTN_DILATIONS[g]
    L = qkv.shape[0]
    HD = qkv.shape[1] // (3 * d)
    H = HD // HEAD_DIM
    nb = L // ATTN_BLOCK
    scale = HEAD_DIM ** -0.5

    def body(q_ref, kc_ref, vc_ref, qg_ref, kg_ref, o_ref, lse_ref, kk, vv):
        n = pl.program_id(1)
        _carry_previous(n, [kk, vv])
        mask = _band_mask(n)
        lane = lax.broadcasted_iota(I32, (ATTN_BLOCK, 128), 1)
        gq = qg_ref[g:g + 1, :]
        gk = kg_ref[g:g + 1, :]
        vv[ATTN_BLOCK:, :] = vc_ref[...]
        lse_tile = jnp.zeros((ATTN_BLOCK, 128), F32)
        for h in range(H):
            sl = slice(h * HEAD_DIM, (h + 1) * HEAD_DIM)
            q = _head_norm(q_ref[:, sl], gq)[0]
            kk[ATTN_BLOCK:, sl] = _head_norm(kc_ref[:, sl], gk)[0]
            s = jnp.where(mask, _dot_nt(q, kk[:, sl]) * scale, NEG_BIG)
            m = jnp.max(s, axis=-1, keepdims=True)
            p = jnp.exp(s - m)
            l = jnp.sum(p, axis=-1, keepdims=True)
            o_ref[:, sl] = _dot_nn(p.astype(BF16), vv[:, sl]) / l
            lse_tile = jnp.where(lane == h, m + jnp.log(l), lse_tile)
        lse_ref[...] = lse_tile

    def blk(which):
        return pl.BlockSpec((ATTN_BLOCK, HD), lambda r, n: (n, r * 3 + which))

    gains = pl.BlockSpec((3, HEAD_DIM), lambda r, n: (0, 0))
    return pl.pallas_call(
        body, name=name, grid=(d, nb),
        in_specs=[blk(0), blk(1), blk(2), gains, gains],
        out_specs=[pl.BlockSpec((ATTN_BLOCK, HD), lambda r, n: (n, r)),
                   pl.BlockSpec((ATTN_BLOCK, 128), lambda r, n: (n, r))],
        out_shape=[jax.ShapeDtypeStruct((L, d * HD), F32), jax.ShapeDtypeStruct((L, d * 128), F32)],
        scratch_shapes=[pltpu.VMEM((2 * ATTN_BLOCK, HD), BF16), pltpu.VMEM((2 * ATTN_BLOCK, HD), BF16)],
        compiler_params=_params("arbitrary", "arbitrary"),
    )(qkv, qkv, qkv, qg, kg)


def _attn_merge(os, lses, name):
    T = os[0].shape[0]
    HD = os[0].shape[1]
    H = HD // HEAD_DIM
    tm = min(T, STREAM_TILE)
    dils = ATTN_DILATIONS[1:]

    def body(o0, o1, o2, l0, l1, l2, ob_ref, of_ref, lse_ref, n1, n2, m1, m2):
        for d, src, dst, w in ((dils[0], o1, n1, HD), (dils[1], o2, n2, HD), (dils[0], l1, m1, 128), (dils[1], l2, m2, 128)):
            for s in range(d):
                for c in range(w // LANES):
                    dst.at[c][pl.ds(s, tm // d, stride=d), :] = src[:, s * w + c * LANES:s * w + (c + 1) * LANES]
        ls = [l0[...], m1[0], m2[0]]
        m = jnp.maximum(jnp.maximum(ls[0], ls[1]), ls[2])
        tot = m + jnp.log(jnp.exp(ls[0] - m) + jnp.exp(ls[1] - m) + jnp.exp(ls[2] - m))
        lse_ref[...] = tot
        ws = [jnp.exp(l - tot) for l in ls]
        for h in range(H):
            sl = slice(h * HEAD_DIM, (h + 1) * HEAD_DIM)
            out = (ws[0][:, h:h + 1] * o0[:, sl] + ws[1][:, h:h + 1] * n1[h]) + ws[2][:, h:h + 1] * n2[h]
            of_ref[:, sl] = out
            ob_ref[:, sl] = out.astype(BF16)

    def wide(d, w):
        return pl.BlockSpec((tm // d, d * w), lambda i: (i, 0))

    row = pl.BlockSpec((tm, HD), lambda i: (i, 0))
    nar = pl.BlockSpec((tm, 128), lambda i: (i, 0))
    return pl.pallas_call(
        body, name=name, grid=(T // tm,),
        in_specs=[row, wide(dils[0], HD), wide(dils[1], HD), nar, wide(dils[0], 128), wide(dils[1], 128)],
        out_specs=[row, row, nar],
        out_shape=[jax.ShapeDtypeStruct((T, HD), BF16), jax.ShapeDtypeStruct((T, HD), F32),
                   jax.ShapeDtypeStruct((T, 128), F32)],
        scratch_shapes=[pltpu.VMEM((H, tm, HEAD_DIM), F32), pltpu.VMEM((H, tm, HEAD_DIM), F32),
                        pltpu.VMEM((1, tm, 128), F32), pltpu.VMEM((1, tm, 128), F32)],
        compiler_params=_params("parallel"),
    )(*os, *lses)


def _attn_delta(out, dout, lse, name):
    T, HD = out.shape
    H = HD // HEAD_DIM
    tm = min(T, STREAM_TILE)
    dils = ATTN_DILATIONS[1:]

    def body(o_ref, d_ref, l_ref, dl_ref, *rest):
        dch = rest[-1]
        lane = lax.broadcasted_iota(I32, (tm, 128), 1)
        tile = jnp.zeros((tm, 128), F32)
        for h in range(H):
            sl = slice(h * HEAD_DIM, (h + 1) * HEAD_DIM)
            dch[h] = d_ref[:, sl]
            tile = jnp.where(lane == h, jnp.sum(o_ref[:, sl] * d_ref[:, sl], axis=-1, keepdims=True), tile)
        dl_ref[...] = tile
        for i, d in enumerate(dils):
            dw_ref, lw_ref, dlw_ref = rest[3 * i:3 * i + 3]
            for s in range(d):
                rows = pl.ds(s, tm // d, stride=d)
                for h in range(H):
                    dw_ref[:, s * HD + h * HEAD_DIM:s * HD + (h + 1) * HEAD_DIM] = dch.at[h][rows, :].astype(BF16)
                lw_ref[:, s * 128:(s + 1) * 128] = l_ref[rows, :]
                dlw_ref[:, s * 128:(s + 1) * 128] = dl_ref[rows, :]

    row = pl.BlockSpec((tm, HD), lambda i: (i, 0))
    nar = pl.BlockSpec((tm, 128), lambda i: (i, 0))
    out_specs, out_shape = [nar], [jax.ShapeDtypeStruct((T, 128), F32)]
    for d in dils:
        out_specs += [pl.BlockSpec((tm // d, d * w), lambda i: (i, 0)) for w in (HD, 128, 128)]
        out_shape += [jax.ShapeDtypeStruct((T // d, d * HD), BF16), jax.ShapeDtypeStruct((T // d, d * 128), F32),
                      jax.ShapeDtypeStruct((T // d, d * 128), F32)]
    return pl.pallas_call(
        body, name=name, grid=(T // tm,), in_specs=[row, row, nar], out_specs=out_specs, out_shape=out_shape,
        scratch_shapes=[pltpu.VMEM((H, tm, HEAD_DIM), F32)],
        compiler_params=_params("parallel"),
    )(out, dout, lse)


def _attn_bwd(qkv, dout, lse, delta, qg, kg, g, name):
    d = ATTN_DILATIONS[g]
    L = qkv.shape[0]
    HD = qkv.shape[1] // (3 * d)
    H = HD // HEAD_DIM
    nb = L // ATTN_BLOCK
    total = d * nb
    scale = HEAD_DIM ** -0.5

    def body(*refs):
        (q_ref, kc_ref, vc_ref, do_ref, lse_ref, dl_ref, qg_ref, kg_ref,
         out_ref, gg_ref, dq_a, dk_a, dv_a, dq_b, dk_b, dv_b, kk, vv, khh, rkk) = refs
        t = pl.program_id(0)
        n = t % nb
        gq = qg_ref[g:g + 1, :]
        gk = kg_ref[g:g + 1, :]
        _carry_previous(t, [kk, vv, khh, rkk])

        @pl.when(t == 0)
        def _():
            gg_ref[...] = jnp.zeros_like(gg_ref)
            dk_a[...] = jnp.zeros_like(dk_a)
            dv_a[...] = jnp.zeros_like(dv_a)

        @pl.when(t < total)
        def _():
            mask = _band_mask(n)
            vv[ATTN_BLOCK:, :] = vc_ref[...]
            gq_sum = jnp.zeros((1, HEAD_DIM), F32)
            for h in range(H):
                sl = slice(h * HEAD_DIM, (h + 1) * HEAD_DIM)
                qn, qh, rq = _head_norm(q_ref[:, sl], gq)
                kcn, kch, rkc = _head_norm(kc_ref[:, sl], gk)
                kk[ATTN_BLOCK:, sl] = kcn
                khh[ATTN_BLOCK:, sl] = kch
                rkk[ATTN_BLOCK:, sl] = jnp.broadcast_to(rkc, (ATTN_BLOCK, HEAD_DIM))
                kn = kk[:, sl]
                p = jnp.where(mask, jnp.exp(_dot_nt(qn, kn) * scale - lse_ref[:, h:h + 1]), 0.0)
                do = do_ref[:, sl].astype(BF16)
                ds = (p * (_dot_nt(do, vv[:, sl]) - dl_ref[:, h:h + 1]) * scale).astype(BF16)
                dqn = _dot_nn(ds, kn)
                dqh = dqn * gq
                dq_b[:, sl] = rq * (dqh - qh * jnp.mean(dqh * qh, axis=-1, keepdims=True))
                gq_sum = gq_sum + jnp.sum(dqn * qh, axis=0, keepdims=True)
                dkn = _dot_tn(ds, qn)
                dvn = _dot_tn(p.astype(BF16), do)
                dk_a[:, sl] += dkn[0:ATTN_BLOCK]
                dv_a[:, sl] += dvn[0:ATTN_BLOCK]
                dk_b[:, sl] = dkn[ATTN_BLOCK:]
                dv_b[:, sl] = dvn[ATTN_BLOCK:]
            gg_ref[0:1, :] += gq_sum

        @pl.when(t > 0)
        def _():
            out_ref[:, 0:HD] = dq_a[...].astype(BF16)
            gk_sum = jnp.zeros((1, HEAD_DIM), F32)
            for h in range(H):
                sl = slice(h * HEAD_DIM, (h + 1) * HEAD_DIM)
                kh = khh[0:ATTN_BLOCK, sl]
                rk = rkk[0:ATTN_BLOCK, sl]
                dkn = dk_a[:, sl]
                dkh = dkn * gk
                dk = rk * (dkh - kh * jnp.mean(dkh * kh, axis=-1, keepdims=True))
                out_ref[:, HD + h * HEAD_DIM:HD + (h + 1) * HEAD_DIM] = dk.astype(BF16)
                gk_sum = gk_sum + jnp.sum(dkn * kh, axis=0, keepdims=True)
            out_ref[:, 2 * HD:3 * HD] = dv_a[...].astype(BF16)
            gg_ref[1:2, :] += gk_sum

        @pl.when(t < total)
        def _():
            dq_a[...] = dq_b[...]
            dk_a[...] = dk_b[...]
            dv_a[...] = dv_b[...]

    def at(t):
        t = jnp.minimum(t, total - 1)
        return t % nb, t // nb

    def blk(which):
        return pl.BlockSpec((ATTN_BLOCK, HD), lambda t: (at(t)[0], at(t)[1] * 3 + which))

    gains = pl.BlockSpec((3, HEAD_DIM), lambda t: (0, 0))
    nar = pl.BlockSpec((ATTN_BLOCK, 128), lambda t: at(t))
    acc = pltpu.VMEM((ATTN_BLOCK, HD), F32)
    pair16 = pltpu.VMEM((2 * ATTN_BLOCK, HD), BF16)
    pair32 = pltpu.VMEM((2 * ATTN_BLOCK, HD), F32)
    return pl.pallas_call(
        body, name=name, grid=(total + 1,),
        in_specs=[blk(0), blk(1), blk(2), pl.BlockSpec((ATTN_BLOCK, HD), lambda t: at(t)), nar, nar, gains, gains],
        out_specs=[pl.BlockSpec((ATTN_BLOCK, 3 * HD), lambda t: at(jnp.maximum(t - 1, 0))),
                   pl.BlockSpec((8, HEAD_DIM), lambda t: (0, 0))],
        out_shape=[jax.ShapeDtypeStruct((L, d * 3 * HD), BF16), jax.ShapeDtypeStruct((8, HEAD_DIM), F32)],
        scratch_shapes=[acc, acc, acc, acc, acc, acc, pair16, pair16, pair32, pair32],
        compiler_params=_params("arbitrary"),
    )(qkv, qkv, qkv, dout, lse, delta, qg, kg)


def _adamw(w, grad, m, v, name):
    if w.ndim == 2:
        R, C = w.shape
        tr = _row_tile(R, 512)
        blk, steps = pl.BlockSpec((tr, C), lambda i: (i, 0)), R // tr
    else:
        zeros = (0,) * w.ndim
        blk, steps = pl.BlockSpec(w.shape, lambda i: zeros), 1

    def body(w_ref, g_ref, m_ref, v_ref, d_ref, nm_ref, nv_ref):
        gv = g_ref[...]
        nm = ADAM_B1 * m_ref[...] + (1.0 - ADAM_B1) * gv
        nv = ADAM_B2 * v_ref[...] + (1.0 - ADAM_B2) * (gv * gv)
        m_hat = nm / (1.0 - ADAM_B1 ** ADAM_STEP)
        v_hat = nv / (1.0 - ADAM_B2 ** ADAM_STEP)
        d_ref[...] = -ADAM_LR * (m_hat / (jnp.sqrt(v_hat) + ADAM_EPS) + ADAM_WD * w_ref[...])
        nm_ref[...] = nm
        nv_ref[...] = nv

    shp = jax.ShapeDtypeStruct(w.shape, F32)
    return pl.pallas_call(
        body, name=name, grid=(steps,), in_specs=[blk] * 4, out_specs=[blk] * 3, out_shape=[shp] * 3,
        compiler_params=_params("parallel"),
    )(w, grad, m, v)


def _ffn_fwd(x, h, comm, s, tag, tgt=None, next_g=None):
    (win_t, *rest), tok = comm.weights(s, h)
    z3, a = _gated_in(h, win_t, None, "swiglu", BF16, f"ffn_in_{tag}", after=tok)
    tok = ()
    if not rest:
        rest, tok = comm.second(s, a)
    wout, = rest
    saved = (x, h, z3, a, win_t, wout)
    if tgt is None:
        return _rows_mm(a, wout, "res", f"ffn_out_{tag}", x=x, scale=0.5, g=next_g, after=tok), saved
    return _rows_mm(a, wout, "loss", f"ffn_out_loss_{tag}", x=x, scale=0.5, tgt=tgt), saved


def _ffn_bwd(dxo, saved, g, comm, s, tag, after):
    x, h, z3, a, win_t, wout = saved
    dz3 = _swiglu_bwd(dxo, wout, z3, 0.5, f"ffn_out_bwd_{tag}", after=after)
    d_wout = _mm_tn(a, dxo, 0.5, f"ffn_dwout_{tag}")
    d_win_t = _mm_tn(dz3, h, 1.0, f"ffn_dwin_{tag}")
    tok = comm.grads(s, [d_win_t, d_wout])
    dx, dg8 = _rows_mm(dz3, win_t, "rmsbwd", f"ffn_in_bwd_{tag}", x=x, g=g, dxo=dxo, after=tok)
    return dx, dg8, ()


def _local_step(x, tgt, vecs, comm):
    norm_g = vecs["norm_g"]

    def gvec(i):
        return norm_g[i:i + 1]

    h0 = _rmsnorm(x, gvec(0), "rmsnorm_l0a", after=comm.started)
    (x1, h_c), sv_a0 = _ffn_fwd(x, h0, comm, 0, "l0a", next_g=gvec(1))
    (pw1_t, pw2), tok = comm.weights(1, h_c)
    zc3, u = _gated_in(h_c, pw1_t, vecs["b_pw1"], "glu", F32, "conv_pw1", after=tok)
    v, s = _dwconv_fwd(u, vecs["w_dw"], vecs["b_dw"], vecs["cg"], "conv_dw")
    x2, h_b0 = _rows_mm(s, pw2, "res", "conv_pw2", x=x1, bias=vecs["b_pw2"], g=gvec(2))
    (x3, h_a1), sv_b0 = _ffn_fwd(x2, h_b0, comm, 2, "l0b", next_g=gvec(3))
    x4, sv_a1 = _ffn_fwd(x3, h_a1, comm, 3, "l1a")
    h_s = _rmsnorm_streams(x4, gvec(4), "rmsnorm_attn")
    (qkv_t, wo), tok = comm.weights(4, h_s[0])
    hd3 = qkv_t.shape[0] // 3
    qkvs, os, lses = [], [], []
    for g, d in enumerate(ATTN_DILATIONS):
        qkv = _mm_nt(h_s[g], qkv_t, BF16, f"attn_qkv_g{g}", w_row0=g * hd3, n_rows=hd3, streams=d, after=tok)
        o, lse = _attn_fwd(qkv, vecs["q_norm"], vecs["k_norm"], g, f"attn_fwd_g{g}")
        qkvs.append(qkv)
        os.append(o)
        lses.append(lse)
    out_b, out_f, lse_tot = _attn_merge(os, lses, "attn_merge")
    x5, h_b1 = _rows_mm(out_b, wo, "res", "attn_wo", x=x4, g=gvec(5))
    (dy, loss8), sv_b1 = _ffn_fwd(x5, h_b1, comm, 5, "l1b", tgt=tgt)
    loss = 0.5 * jnp.sum(loss8) / x.shape[1]

    dg = [None] * 6
    dx, dg[5], tok = _ffn_bwd(dy, sv_b1, gvec(5), comm, 5, "l1b", ())
    dout = _mm_nt(dx, wo, F32, "attn_wo_bwd", after=tok)
    d_wo = _mm_tn(out_b, dx, 1.0, "attn_dwo")
    delta, *wide = _attn_delta(out_f, dout, lse_tot, "attn_delta")
    per_group = [(dout, lse_tot, delta), tuple(wide[0:3]), tuple(wide[3:6])]
    ggs, d_qkv_t, dhs = [], [], []
    for g, d in enumerate(ATTN_DILATIONS):
        dqkv, gg = _attn_bwd(qkvs[g], *per_group[g], vecs["q_norm"], vecs["k_norm"], g, f"attn_bwd_g{g}")
        ggs.append(gg)
        d_qkv_t.append(_mm_tn(dqkv, h_s[g], 1.0, f"attn_dwqkv_g{g}", streams=d))
        if d > 1:
            dhs.append((d, _rows_mm(dqkv, qkv_t, "plain", f"attn_qkv_bwd_g{g}", b_row0=g * hd3, streams=d)))
        else:
            dqkv0 = dqkv
    tok = comm.grads(4, [jnp.concatenate(d_qkv_t), d_wo])
    dx, dg[4] = _rows_mm(dqkv0, qkv_t, "rmsbwd", "attn_qkv_bwd_g0", x=x4, g=gvec(4), dxo=dx, extras=dhs, after=tok)
    dx, dg[3], tok = _ffn_bwd(dx, sv_a1, gvec(3), comm, 3, "l1a", ())
    dx, dg[2], tok = _ffn_bwd(dx, sv_b0, gvec(2), comm, 2, "l0b", tok)
    ds = _mm_nt(dx, pw2, F32, "conv_pw2_bwd", after=tok)
    d_pw2 = _mm_tn(s, dx, 1.0, "conv_dwpw2")
    dv, dcg8, dbdw8, dbpw2_8 = _conv_post_bwd(ds, v, vecs["cg"], dx, "conv_post_bwd")
    dzc3, d_wdw, db1_16 = _dwconv_bwd(dv, u, vecs["w_dw"], zc3, "conv_dw_bwd")
    d_pw1_t = _mm_tn(dzc3, h_c, 1.0, "conv_dwpw1")
    tok = comm.grads(1, [d_pw1_t, d_pw2])
    dx, dg[1] = _rows_mm(dzc3, pw1_t, "rmsbwd", "conv_pw1_bwd", x=x1, g=gvec(1), dxo=dx, after=tok)
    dx, dg[0], tok = _ffn_bwd(dx, sv_a0, gvec(0), comm, 0, "l0a", ())

    D = x.shape[1]
    small = {
        "norm_g": jnp.stack([p.sum(axis=0) for p in dg]),
        "b_pw1": db1_16.reshape(2, 8, D).sum(axis=1),
        "w_dw": d_wdw[:CONV_WIDTH],
        "b_dw": dbdw8.sum(axis=0, keepdims=True),
        "cg": dcg8.sum(axis=0, keepdims=True),
        "b_pw2": dbpw2_8.sum(axis=0, keepdims=True),
        "q_norm": jnp.stack([gg[0] for gg in ggs]),
        "k_norm": jnp.stack([gg[1] for gg in ggs]),
    }
    return loss, dx, small, tok


def _mesh_pos():
    return lax.axis_index("x"), lax.axis_index("y"), lax.axis_index("c")


def _other_chips(x, y):
    return [(1 - x, y), (x, 1 - y), (1 - x, 1 - y)]


HBM = pl.BlockSpec(memory_space=pltpu.HBM)
SEM = pl.BlockSpec(memory_space=pltpu.SEMAPHORE)
SPLIT_COPY = pltpu.SideEffectType.DATAFLOW_SIDE_EFFECTING


def _hbm(a):
    return pltpu.with_memory_space_constraint(a, pltpu.HBM)


def _gather_copies(j, src, out, send, recv, x, y, c):
    me = 4 * x + 2 * y + c
    peers = [(x, y, 1 - c)] + [(px, py, c) for px, py in _other_chips(x, y)]
    return [pltpu.make_async_remote_copy(src_ref=src, dst_ref=out.at[me], send_sem=send.at[4 * j + k],
                                         recv_sem=recv.at[4 * j + k], device_id=peer, device_id_type=MESH)
            for k, peer in enumerate(peers)]


def _gather_start(shards, me, after, name):
    n = len(shards)
    lands = [lax.dynamic_update_slice(lax.empty((N_DEV,) + s.shape, s.dtype), s[None], (me, 0, 0)) for s in shards]

    def body(*refs):
        ins, outs = refs[:n], refs[2 * n + len(after):3 * n + len(after)]
        send, recv, token = refs[4 * n + len(after):]
        x, y, c = _mesh_pos()
        for j in range(n):
            for cp in _gather_copies(j, ins[j], outs[j], send, recv, x, y, c):
                cp.start()
        token[...] = jnp.zeros_like(token)

    res = pl.pallas_call(
        body, name=name, in_specs=[HBM] * (2 * n) + [ANY] * len(after),
        out_specs=[HBM] * (2 * n) + [SEM, SEM, pl.BlockSpec(memory_space=pltpu.VMEM)],
        out_shape=[pltpu.HBM((N_DEV,) + s.shape, s.dtype) for s in shards] + [pltpu.HBM(s.shape, s.dtype) for s in shards]
        + [pltpu.SemaphoreType.DMA((4 * n,)), pltpu.SemaphoreType.DMA((4 * n,)), jax.ShapeDtypeStruct((8, 128), F32)],
        input_output_aliases={**{i: n + i for i in range(n)}, **{n + i: i for i in range(n)}},
        compiler_params=pltpu.CompilerParams(has_side_effects=SPLIT_COPY),
    )(*[_hbm(s) for s in shards], *[_hbm(l) for l in lands], *after)
    return res[:n], res[n:2 * n], res[2 * n], res[2 * n + 1], res[2 * n + 2]


def _gather_wait(outs, thru, send, recv, after, name):
    n = len(outs)

    def body(*refs):
        out_refs, thru_refs, send_ref, recv_ref = refs[:n], refs[n:2 * n], refs[2 * n], refs[2 * n + 1]
        x, y, c = _mesh_pos()
        for j in range(n):
            for cp in _gather_copies(j, thru_refs[j], out_refs[j], send_ref, recv_ref, x, y, c):
                cp.wait_send()
                cp.wait_recv()

    res = pl.pallas_call(
        body, name=name, in_specs=[HBM] * (2 * n) + [SEM, SEM] + [ANY] * len(after), out_specs=[HBM] * (2 * n),
        out_shape=[pltpu.HBM(a.shape, a.dtype) for a in list(outs) + list(thru)],
        input_output_aliases={i: i for i in range(2 * n)},
        compiler_params=pltpu.CompilerParams(has_side_effects=SPLIT_COPY),
    )(*outs, *thru, send, recv, *after)
    return res[:n]


def _gather_forward(outs, name):
    n = len(outs)

    def body(*refs):
        o = refs[n:2 * n]
        send_sems, recv_sems = refs[2 * n:]
        x, y, c = _mesh_pos()
        copies = []
        for j in range(n):
            for k, (px, py) in enumerate(_other_chips(x, y)):
                blk = o[j].at[4 * px + 2 * py + c]
                cp = pltpu.make_async_remote_copy(src_ref=blk, dst_ref=blk, send_sem=send_sems.at[j, k],
                                                  recv_sem=recv_sems.at[j, k], device_id=(x, y, 1 - c), device_id_type=MESH)
                cp.start()
                copies.append(cp)
        for cp in copies:
            cp.wait()

    return pl.pallas_call(
        body, name=name, in_specs=[ANY] * n, out_specs=[ANY] * n,
        out_shape=[jax.ShapeDtypeStruct(a.shape, a.dtype) for a in outs],
        input_output_aliases={i: i for i in range(n)},
        scratch_shapes=[pltpu.SemaphoreType.DMA((n, 3)), pltpu.SemaphoreType.DMA((n, 3))],
    )(*outs)


def _all_sum(p, name):
    R, C = p.shape

    def body(p_ref, o_ref, buf, send_sems, recv_sems):
        x, y, c = _mesh_pos()
        me = 4 * x + 2 * y + c
        buf[me] = p_ref[...]
        copies = []
        for rel in range(1, N_DEV):
            peer = (1 - x if rel & 4 else x, 1 - y if rel & 2 else y, 1 - c if rel & 1 else c)
            cp = pltpu.make_async_remote_copy(
                src_ref=p_ref, dst_ref=buf.at[me], send_sem=send_sems.at[rel - 1], recv_sem=recv_sems.at[rel - 1],
                device_id=peer, device_id_type=MESH)
            cp.start()
            copies.append(cp)
        for cp in copies:
            cp.wait()
        acc = buf[0]
        for dev in range(1, N_DEV):
            acc = acc + buf[dev]
        o_ref[...] = acc

    vm = pl.BlockSpec(memory_space=pltpu.VMEM)
    return pl.pallas_call(
        body, name=name, in_specs=[vm], out_specs=vm, out_shape=jax.ShapeDtypeStruct((R, C), F32),
        scratch_shapes=[pltpu.VMEM((N_DEV, R, C), F32), pltpu.SemaphoreType.DMA((N_DEV - 1,)),
                        pltpu.SemaphoreType.DMA((N_DEV - 1,))],
    )(p)


def _swap_with_sibling(gs, name):
    n = len(gs)

    def body(*refs):
        ins, outs = refs[:n], refs[n:2 * n]
        send_sems, recv_sems = refs[2 * n:]
        x, y, c = _mesh_pos()
        copies = []
        for i in range(n):
            for k in range(4):
                cp = pltpu.make_async_remote_copy(
                    src_ref=ins[i].at[2 * k + (1 - c)], dst_ref=outs[i].at[k],
                    send_sem=send_sems.at[i, k], recv_sem=recv_sems.at[i, k],
                    device_id=(x, y, 1 - c), device_id_type=MESH)
                cp.start()
                copies.append(cp)
        for cp in copies:
            cp.wait()

    return pl.pallas_call(
        body, name=name, in_specs=[ANY] * n, out_specs=[ANY] * n,
        out_shape=[jax.ShapeDtypeStruct((4,) + g.shape[1:], g.dtype) for g in gs],
        scratch_shapes=[pltpu.SemaphoreType.DMA((n, 4)), pltpu.SemaphoreType.DMA((n, 4))],
    )(*gs)


def _scatter_copies(j, src, land, send, recv, x, y, c):
    return [pltpu.make_async_remote_copy(src_ref=src.at[2 * px + py], dst_ref=land.at[k], send_sem=send.at[3 * j + k],
                                         recv_sem=recv.at[3 * j + k], device_id=(px, py, c), device_id_type=MESH)
            for k, (px, py) in enumerate(_other_chips(x, y))]


def _scatter_start(ps, name):
    n = len(ps)

    def body(*refs):
        ins, lands = refs[:n], refs[2 * n:3 * n]
        send, recv, token = refs[3 * n:]
        x, y, c = _mesh_pos()
        for j in range(n):
            for cp in _scatter_copies(j, ins[j], lands[j], send, recv, x, y, c):
                cp.start()
        token[...] = jnp.zeros_like(token)

    res = pl.pallas_call(
        body, name=name, in_specs=[HBM] * n,
        out_specs=[HBM] * (2 * n) + [SEM, SEM, pl.BlockSpec(memory_space=pltpu.VMEM)],
        out_shape=[pltpu.HBM(p.shape, p.dtype) for p in ps] + [pltpu.HBM((3,) + p.shape[1:], p.dtype) for p in ps]
        + [pltpu.SemaphoreType.DMA((3 * n,)), pltpu.SemaphoreType.DMA((3 * n,)), jax.ShapeDtypeStruct((8, 128), F32)],
        input_output_aliases={i: i for i in range(n)},
        compiler_params=pltpu.CompilerParams(has_side_effects=SPLIT_COPY),
    )(*[_hbm(p) for p in ps])
    return res[:n], res[n:2 * n], res[2 * n], res[2 * n + 1], res[2 * n + 2]


def _scatter_wait(thru, lands, send, recv, after, name):
    n = len(thru)

    def body(*refs):
        thru_refs, land_refs, send_ref, recv_ref = refs[:n], refs[n:2 * n], refs[2 * n], refs[2 * n + 1]
        x, y, c = _mesh_pos()
        for j in range(n):
            for cp in _scatter_copies(j, thru_refs[j], land_refs[j], send_ref, recv_ref, x, y, c):
                cp.wait_send()
                cp.wait_recv()

    res = pl.pallas_call(
        body, name=name, in_specs=[HBM] * (2 * n) + [SEM, SEM, ANY], out_specs=[HBM] * (2 * n),
        out_shape=[pltpu.HBM(a.shape, a.dtype) for a in list(thru) + list(lands)],
        input_output_aliases={i: i for i in range(2 * n)},
        compiler_params=pltpu.CompilerParams(has_side_effects=SPLIT_COPY),
    )(*thru, *lands, send, recv, after)
    return res[:n], res[n:]


def _row_tile(r, pref):
    if r <= pref:
        return r
    for t in range(pref - pref % 16, 0, -16):
        if r % t == 0:
            return t
    return r


def _add_sibling(g, r1, pos, name):
    _, r, D = g.shape
    tr = _row_tile(r, 512)

    def body(pos_ref, g_ref, r_ref, o_ref):
        o_ref[...] = (g_ref[...].astype(F32) + r_ref[...].astype(F32)).astype(BF16)

    return pl.pallas_call(
        body, name=name,
        grid_spec=pltpu.PrefetchScalarGridSpec(
            num_scalar_prefetch=1, grid=(4, r // tr),
            in_specs=[pl.BlockSpec((None, None, tr, D), lambda k, j, pos: (k, pos[0], j, 0)),
                      pl.BlockSpec((None, tr, D), lambda k, j, pos: (k, j, 0))],
            out_specs=pl.BlockSpec((None, tr, D), lambda k, j, pos: (k, j, 0))),
        out_shape=jax.ShapeDtypeStruct((4, r, D), BF16),
        compiler_params=_params("parallel", "parallel"),
    )(pos, g.reshape(4, 2, r, D), r1)


def _add_chips(p, r2, pos, name):
    _, r, D = p.shape
    tr = _row_tile(r, 512)

    def body(pos_ref, p_ref, r_ref, o_ref):
        acc = p_ref[...].astype(F32)
        for j in range(3):
            acc = acc + r_ref[j].astype(F32)
        o_ref[...] = acc

    return pl.pallas_call(
        body, name=name,
        grid_spec=pltpu.PrefetchScalarGridSpec(
            num_scalar_prefetch=1, grid=(r // tr,),
            in_specs=[pl.BlockSpec((None, tr, D), lambda j, pos: (pos[1], j, 0)),
                      pl.BlockSpec((3, tr, D), lambda j, pos: (0, j, 0))],
            out_specs=pl.BlockSpec((tr, D), lambda j, pos: (j, 0))),
        out_shape=jax.ShapeDtypeStruct((r, D), F32),
        compiler_params=_params("parallel"),
    )(pos, p, r2)


def _place_cols(local, me, width):
    R, w = local.shape
    return lax.dynamic_update_slice(jnp.zeros((R, width), F32), local, (0, me * w))


def _pad_rows(a, rows):
    return jnp.pad(a, ((0, rows - a.shape[0]), (0, 0)))


SUBLAYER_SHARDS = ((0, 4), (8, 9), (1, 5), (2, 6), (10, 11), (3, 7))
GATHER_UNITS = ((0,), (4,), (8, 9), (1, 5), (2, 6), (10, 11), (3, 7))
SUBLAYER_UNITS = ((0, 1), (2,), (3,), (4,), (5,), (6,))
GATHERS_AHEAD = 3


class _StepComm:
    def __init__(self, shards, pos):
        self.pos = pos
        self.shards = shards
        self.gathers = {}
        tok = ()
        for u in range(GATHERS_AHEAD):
            tok = self._start_gather(u, tok)
        self.started = tok
        self.next_unit = GATHERS_AHEAD
        self.pending = {}

    def _start_gather(self, u, after):
        outs, thru, send, recv, token = _gather_start([self.shards[i] for i in GATHER_UNITS[u]],
                                                      2 * self.pos[1] + self.pos[0], after, f"gather_start_{u}")
        self.gathers[u] = (outs, thru, send, recv)
        return (token,)

    def _take(self, u, after):
        outs = _gather_wait(*self.gathers.pop(u), (after,), f"gather_wait_{u}")
        outs = _gather_forward(outs, f"gather_forward_{u}")
        tok = ()
        if self.next_unit < len(GATHER_UNITS):
            tok = self._start_gather(self.next_unit, (outs[0],))
            self.next_unit += 1
        return [w.reshape(-1, w.shape[-1]) for w in outs], tok

    def weights(self, s, after):
        return self._take(SUBLAYER_UNITS[s][0], after)

    def second(self, s, after):
        return self._take(SUBLAYER_UNITS[s][1], after)

    def grads(self, s, mats):
        g3 = [g.reshape(N_DEV, g.shape[0] // N_DEV, g.shape[1]) for g in mats]
        r1 = _swap_with_sibling(g3, f"rs_sibling_{s}")
        ps = [_add_sibling(g, r, self.pos, f"rs_add_sibling_{s}_{i}") for i, (g, r) in enumerate(zip(g3, r1))]
        thru, lands, send, recv, token = _scatter_start(ps, f"rs_start_{s}")
        self.pending[s] = (thru, lands, send, recv)
        return (token,)

    def finish(self, after):
        out = {}
        for s in sorted(self.pending, reverse=True):
            ps, lands = _scatter_wait(*self.pending[s], after, f"rs_wait_{s}")
            for i, p, r2 in zip(SUBLAYER_SHARDS[s], ps, lands):
                out[i] = _add_chips(p, r2, self.pos, f"rs_add_chips_{i}")
        return out


def kernel(x, norm_g, ffn_w_in, ffn_w_out, conv_w_pw1, conv_b_pw1, conv_w_dw, conv_b_dw, conv_norm_g, conv_w_pw2, conv_b_pw2, attn_w_qkv, attn_q_norm, attn_k_norm, attn_w_o, loss_target, m_norm_g, m_ffn_w_in, m_ffn_w_out, m_conv_w_pw1, m_conv_b_pw1, m_conv_w_dw, m_conv_b_dw, m_conv_norm_g, m_conv_w_pw2, m_conv_b_pw2, m_attn_w_qkv, m_attn_q_norm, m_attn_k_norm, m_attn_w_o, v_norm_g, v_ffn_w_in, v_ffn_w_out, v_conv_w_pw1, v_conv_b_pw1, v_conv_w_dw, v_conv_b_dw, v_conv_norm_g, v_conv_w_pw2, v_conv_b_pw2, v_attn_w_qkv, v_attn_q_norm, v_attn_k_norm, v_attn_w_o):
    T, D = x.shape[1], x.shape[2]
    xi, yi, ci = _mesh_pos()
    me = 4 * xi + 2 * yi + ci
    pos = jnp.stack([ci, 2 * xi + yi]).astype(I32)
    lf = [(l, f) for l in range(2) for f in range(2)]

    shards = ([ffn_w_in[l, f].T.astype(BF16) for l, f in lf] + [ffn_w_out[l, f].astype(BF16) for l, f in lf]
              + [conv_w_pw1[0].T.astype(BF16), conv_w_pw2[0].astype(BF16),
                 attn_w_qkv[0].T.astype(BF16), attn_w_o[0].astype(BF16)])
    comm = _StepComm(shards, pos)
    w_cols = norm_g.shape[2]
    vec = _all_sum(_pad_rows(jnp.concatenate([_place_cols(norm_g.reshape(6, w_cols), me, D),
                                              _place_cols(conv_w_dw[0], me, D)]), 40), "gather_vectors")
    vecs = {
        "norm_g": vec[0:6], "w_dw": vec[6:6 + CONV_WIDTH],
        "b_pw1": conv_b_pw1.reshape(2, D), "b_dw": conv_b_dw, "cg": conv_norm_g, "b_pw2": conv_b_pw2,
        "q_norm": attn_q_norm[0], "k_norm": attn_k_norm[0],
    }

    loss_local, dx, small, tok = _local_step(x[0], loss_target[0], vecs, comm)

    mats = comm.finish(dx)
    hd3 = 3 * HEAD_DIM
    packed = jnp.concatenate([
        small["norm_g"], small["b_pw1"], small["w_dw"], small["b_dw"], small["cg"], small["b_pw2"],
        jnp.pad(small["q_norm"].reshape(1, hd3), ((0, 0), (0, D - hd3))),
        jnp.pad(small["k_norm"].reshape(1, hd3), ((0, 0), (0, D - hd3))),
        jnp.full((1, D), loss_local, F32)])
    red = _all_sum(_pad_rows(packed, 48), "reduce_vectors")
    loss = red[44, 0]
    col0 = me * w_cols
    grads = {
        "norm_g": lax.dynamic_slice(red[0:6], (0, col0), (6, w_cols)),
        "ffn_w_in": jnp.concatenate([mats[i].T for i in range(4)]),
        "ffn_w_out": jnp.concatenate([mats[i] for i in range(4, 8)]),
        "conv_w_pw1": mats[8].T,
        "conv_b_pw1": red[6:8].reshape(1, 2 * D),
        "conv_w_dw": lax.dynamic_slice(red[8:8 + CONV_WIDTH], (0, col0), (CONV_WIDTH, w_cols)),
        "conv_b_dw": red[39:40], "conv_norm_g": red[40:41], "conv_w_pw2": mats[9], "conv_b_pw2": red[41:42],
        "attn_w_qkv": mats[10].T,
        "attn_q_norm": red[42, :hd3].reshape(3, HEAD_DIM), "attn_k_norm": red[43, :hd3].reshape(3, HEAD_DIM),
        "attn_w_o": mats[11],
    }

    names = ["norm_g", "ffn_w_in", "ffn_w_out", "conv_w_pw1", "conv_b_pw1", "conv_w_dw", "conv_b_dw",
             "conv_norm_g", "conv_w_pw2", "conv_b_pw2", "attn_w_qkv", "attn_q_norm", "attn_k_norm", "attn_w_o"]
    ws = [norm_g, ffn_w_in, ffn_w_out, conv_w_pw1, conv_b_pw1, conv_w_dw, conv_b_dw, conv_norm_g, conv_w_pw2,
          conv_b_pw2, attn_w_qkv, attn_q_norm, attn_k_norm, attn_w_o]
    ms = [m_norm_g, m_ffn_w_in, m_ffn_w_out, m_conv_w_pw1, m_conv_b_pw1, m_conv_w_dw, m_conv_b_dw, m_conv_norm_g,
          m_conv_w_pw2, m_conv_b_pw2, m_attn_w_qkv, m_attn_q_norm, m_attn_k_norm, m_attn_w_o]
    vs = [v_norm_g, v_ffn_w_in, v_ffn_w_out, v_conv_w_pw1, v_conv_b_pw1, v_conv_w_dw, v_conv_b_dw, v_conv_norm_g,
          v_conv_w_pw2, v_conv_b_pw2, v_attn_w_qkv, v_attn_q_norm, v_attn_k_norm, v_attn_w_o]
    g_out, d_out, m_out, v_out = [], [], [], []
    for name, w, m, v in zip(names, ws, ms, vs):
        shape2 = (-1, w.shape[-1]) if w.size > SMALL_PARAM else w.shape
        g2 = grads[name].reshape(shape2)
        delta, new_m, new_v = _adamw(w.reshape(shape2), g2, m.reshape(shape2), v.reshape(shape2), f"adamw_{name}")
        g_out.append(g2.reshape(w.shape))
        d_out.append(delta.reshape(w.shape))
        m_out.append(new_m.reshape(w.shape))
        v_out.append(new_v.reshape(w.shape))
    return (loss, dx.reshape(x.shape), *g_out, *d_out, *m_out, *v_out)
```

```python
import jax
import jax.numpy as jnp
from jax import lax
from jax.experimental import pallas as pl
from jax.experimental.pallas import tpu as pltpu

F32 = jnp.float32
BF16 = jnp.bfloat16
I32 = jnp.int32

NORM_EPS = 1e-6
LANES = 128
SUBLANES = 8
MXU_WIDTH = 256
HEAD_DIM = 128
ATTN_BLOCK = 128
ATTN_DILATIONS = (1, 4, 16)
CONV_WIDTH = 31
CONV_HALO = 32
CONV_CHUNK = 16
WHOLE_K_MAX = 6144
MM_TN_TOKENS = 2048
STREAM_TILE = 512
ADAM_LR, ADAM_B1, ADAM_B2, ADAM_EPS, ADAM_WD, ADAM_STEP = 0.001, 0.9, 0.999, 1e-08, 0.01, 10
N_DEV = 8
SMALL_PARAM = 8192
NEG_BIG = -1e30
V7X_VMEM_BYTES = 64 * 1024 * 1024
VMEM_LIMIT = V7X_VMEM_BYTES - 8 * 1024 * 1024
MESH = pl.DeviceIdType.MESH
ANY = pl.BlockSpec(memory_space=pl.ANY)


def _params(*sem):
    return pltpu.CompilerParams(dimension_semantics=sem or None, vmem_limit_bytes=VMEM_LIMIT)


def _dot_nn(a, b):
    return lax.dot_general(a, b, (((1,), (0,)), ((), ())), preferred_element_type=F32)


def _dot_nt(a, b):
    return lax.dot_general(a, b, (((1,), (1,)), ((), ())), preferred_element_type=F32)


def _dot_tn(a, b):
    return lax.dot_general(a, b, (((0,), (0,)), ((), ())), preferred_element_type=F32)


def _sigmoid(x):
    return 1.0 / (1.0 + jnp.exp(-x))


def _tile(n, pref):
    if n <= pref:
        return n
    for t in range(pref - pref % 128, 0, -128):
        if n % t == 0:
            return t
    return n


def _to_lane_chunks(dst, val):
    for c in range(dst.shape[0]):
        dst[c] = val[:, c * LANES:(c + 1) * LANES]


def _rows8(v):
    tm, c = v.shape
    return v.reshape(tm // 8, 8, c).sum(axis=0)


def _rmsnorm(x, g, name, after=()):
    T, D = x.shape
    tm = min(T, 512)

    def body(x_ref, g_ref, *rest):
        h_ref = rest[-1]
        xv = x_ref[...]
        r = lax.rsqrt(jnp.mean(xv * xv, axis=-1, keepdims=True) + NORM_EPS)
        h_ref[...] = (xv * r * g_ref[...]).astype(BF16)

    return pl.pallas_call(
        body, name=name, grid=(T // tm,),
        in_specs=[pl.BlockSpec((tm, D), lambda i: (i, 0)), pl.BlockSpec((1, D), lambda i: (0, 0))] + [ANY] * len(after),
        out_specs=pl.BlockSpec((tm, D), lambda i: (i, 0)),
        out_shape=jax.ShapeDtypeStruct((T, D), BF16),
        compiler_params=_params("parallel"),
    )(x, g, *after)


def _rmsnorm_streams(x, g, name):
    T, D = x.shape
    tm = min(T, STREAM_TILE)
    dils = [d for d in ATTN_DILATIONS if d > 1]

    def body(x_ref, g_ref, h_ref, *rest):
        outs, hs = rest[:-1], rest[-1]
        xv = x_ref[...]
        r = lax.rsqrt(jnp.mean(xv * xv, axis=-1, keepdims=True) + NORM_EPS)
        hf = xv * r * g_ref[...]
        h_ref[...] = hf.astype(BF16)
        _to_lane_chunks(hs, hf)
        for d, o_ref in zip(dils, outs):
            for s in range(d):
                for c in range(D // LANES):
                    o_ref[s, :, c * LANES:(c + 1) * LANES] = hs.at[c][pl.ds(s, tm // d, stride=d), :].astype(BF16)

    res = pl.pallas_call(
        body, name=name, grid=(T // tm,),
        in_specs=[pl.BlockSpec((tm, D), lambda i: (i, 0)), pl.BlockSpec((1, D), lambda i: (0, 0))],
        out_specs=[pl.BlockSpec((tm, D), lambda i: (i, 0))]
        + [pl.BlockSpec((d, tm // d, D), lambda i: (0, i, 0)) for d in dils],
        out_shape=[jax.ShapeDtypeStruct((T, D), BF16)] + [jax.ShapeDtypeStruct((d, T // d, D), BF16) for d in dils],
        scratch_shapes=[pltpu.VMEM((D // LANES, tm, LANES), F32)],
        compiler_params=_params("parallel"),
    )(x, g)
    return [res[0]] + [a.reshape(T, D) for a in res[1:]]


def _gated_in(h, wt, bias, act, u_dtype, name, after=()):
    T, D = h.shape
    F = wt.shape[0] // 2
    tm = min(T, 512)

    def body(*refs):
        z_ref, u_ref = refs[-2:]
        h_ref, w_ref = refs[:2]
        hv = h_ref[...]
        for c0 in range(0, F, MXU_WIDTH):
            c1 = min(c0 + MXU_WIDTH, F)
            z0 = _dot_nt(hv, w_ref[c0:c1, :])
            z1 = _dot_nt(hv, w_ref[F + c0:F + c1, :])
            if bias is not None:
                z0 = z0 + refs[2][0:1, c0:c1]
                z1 = z1 + refs[2][1:2, c0:c1]
            z_ref[0, :, c0:c1] = z0.astype(BF16)
            z_ref[1, :, c0:c1] = z1.astype(BF16)
            if act == "swiglu":
                u = z0 * _sigmoid(z0) * z1
            else:
                u = z0 * _sigmoid(z1)
            u_ref[:, c0:c1] = u.astype(u_dtype)

    in_specs = [pl.BlockSpec((tm, D), lambda m: (m, 0)), pl.BlockSpec((2 * F, D), lambda m: (0, 0))]
    args = [h, wt]
    if bias is not None:
        in_specs.append(pl.BlockSpec((2, F), lambda m: (0, 0)))
        args.append(bias)
    in_specs += [ANY] * len(after)
    args += list(after)
    return pl.pallas_call(
        body, name=name, grid=(T // tm,), in_specs=in_specs,
        out_specs=[pl.BlockSpec((2, tm, F), lambda m: (0, m, 0)), pl.BlockSpec((tm, F), lambda m: (m, 0))],
        out_shape=[jax.ShapeDtypeStruct((2, T, F), BF16), jax.ShapeDtypeStruct((T, F), u_dtype)],
        compiler_params=_params("parallel"),
    )(*args)


def _swiglu_bwd(dxo, wout, z3, scale, name, after=()):
    T, D = dxo.shape
    F = wout.shape[0]
    tm = min(T, 512)

    def body(d_ref, w_ref, z_ref, *rest):
        dz_ref = rest[-1]
        dy = (d_ref[...] * scale).astype(BF16)
        for c0 in range(0, F, MXU_WIDTH):
            cs = slice(c0, min(c0 + MXU_WIDTH, F))
            da = _dot_nt(dy, w_ref[cs, :])
            gate = z_ref[0, :, cs].astype(F32)
            up = z_ref[1, :, cs].astype(F32)
            sg = _sigmoid(gate)
            dz_ref[0, :, cs] = (da * up * (sg * (1.0 + gate * (1.0 - sg)))).astype(BF16)
            dz_ref[1, :, cs] = (da * gate * sg).astype(BF16)

    return pl.pallas_call(
        body, name=name, grid=(T // tm,),
        in_specs=[pl.BlockSpec((tm, D), lambda m: (m, 0)),
                  pl.BlockSpec((F, D), lambda m: (0, 0)),
                  pl.BlockSpec((2, tm, F), lambda m: (0, m, 0))] + [ANY] * len(after),
        out_specs=pl.BlockSpec((2, tm, F), lambda m: (0, m, 0)),
        out_shape=jax.ShapeDtypeStruct((2, T, F), BF16),
        compiler_params=_params("parallel"),
    )(dxo, wout, z3, *after)


def _mm_nt(a, wt, out_dtype, name, *, w_row0=0, n_rows=None, streams=1, after=()):
    T, K = a.shape
    N = wt.shape[0] if n_rows is None else n_rows
    L = T // streams
    rows = min(L, STREAM_TILE)
    sp = STREAM_TILE // rows
    mps = L // rows
    assert streams % sp == 0 and w_row0 % N == 0

    def body(a_ref, w_ref, *rest):
        o_ref = rest[-1]
        for s in range(sp):
            a_s = a_ref[s * rows:(s + 1) * rows, :].astype(BF16)
            for c0 in range(0, N, MXU_WIDTH):
                c1 = min(c0 + MXU_WIDTH, N)
                o_ref[:, s * N + c0:s * N + c1] = _dot_nt(a_s, w_ref[c0:c1, :]).astype(out_dtype)

    return pl.pallas_call(
        body, name=name, grid=(T // (rows * sp),),
        in_specs=[pl.BlockSpec((rows * sp, K), lambda m: (m, 0)), pl.BlockSpec((N, K), lambda m: (w_row0 // N, 0))]
        + [ANY] * len(after),
        out_specs=pl.BlockSpec((rows, sp * N), lambda m: (m % mps, m // mps)),
        out_shape=jax.ShapeDtypeStruct((L, streams * N), out_dtype),
        compiler_params=_params("parallel"),
    )(a, wt, *after)


def _mm_tn(a, b, b_scale, name, *, streams=1):
    if a.ndim == 3:
        J, T, F = a.shape
    elif streams > 1:
        J, T, F = 1, a.shape[0] * streams, a.shape[1] // streams
    else:
        (T, F), J = a.shape, 1
    N = b.shape[1]
    tmm = _tile(F, 1408)
    tk = min(T // streams, MM_TN_TOKENS)
    npj = F // tmm
    sp = max(1, MM_TN_TOKENS // tk) if streams > 1 and tk == T // streams else 1
    assert streams % sp == 0
    nk = T // (tk * sp)
    kps = nk // streams if sp == 1 else 1

    def body(*refs):
        a_refs, (b_ref, o_ref, acc_ref) = refs[:sp], refs[sp:]
        k = pl.program_id(1)
        part = None
        for s, a_ref in enumerate(a_refs):
            bv = b_ref[s * tk:(s + 1) * tk, :]
            if b_scale != 1.0:
                bv = bv * b_scale
            dot = _dot_tn(a_ref[...].astype(BF16), bv.astype(BF16))
            part = dot if part is None else part + dot

        @pl.when(k == 0)
        def _():
            acc_ref[...] = part

        @pl.when(k > 0)
        def _():
            acc_ref[...] += part

        @pl.when(k == nk - 1)
        def _():
            o_ref[...] = acc_ref[...].astype(BF16)

    if a.ndim == 3:
        a_specs = [pl.BlockSpec((None, tk, tmm), lambda i, k: (i // npj, k, i % npj))]
    elif sp > 1:
        a_specs = [pl.BlockSpec((tk, tmm), lambda i, k, s=s: (0, (k * sp + s) * npj + i)) for s in range(sp)]
    elif streams > 1:
        a_specs = [pl.BlockSpec((tk, tmm), lambda i, k: (k % kps, (k // kps) * npj + i))]
    else:
        a_specs = [pl.BlockSpec((tk, tmm), lambda i, k: (k, i))]
    return pl.pallas_call(
        body, name=name, grid=(J * npj, nk),
        in_specs=a_specs + [pl.BlockSpec((tk * sp, N), lambda i, k: (k, 0))],
        out_specs=pl.BlockSpec((tmm, N), lambda i, k: (i, 0)),
        out_shape=jax.ShapeDtypeStruct((J * F, N), BF16),
        scratch_shapes=[pltpu.VMEM((tmm, N), F32)],
        compiler_params=_params("parallel", "arbitrary"),
    )(*[a] * sp, b)


def _rows_mm(a, b, mode, name, *, x=None, scale=1.0, bias=None, tgt=None, g=None, dxo=None,
             b_row0=0, streams=1, extras=(), after=()):
    if a.ndim == 3:
        J, T, F = a.shape
    elif streams > 1:
        J, T, F = 1, a.shape[0] * streams, a.shape[1] // streams
    else:
        (T, F), J = a.shape, 1
    D = b.shape[1]
    if streams > 1 and T // streams < STREAM_TILE:
        L = T // streams
        sp = STREAM_TILE // L
        assert mode == "plain" and streams % sp == 0 and b_row0 % F == 0 and not after

        def body_short(a_ref, b_ref, o_ref):
            for s in range(sp):
                o_ref[s * L:(s + 1) * L, :] = _dot_nn(a_ref[:, s * F:(s + 1) * F], b_ref[...])

        return pl.pallas_call(
            body_short, name=name, grid=(streams // sp,),
            in_specs=[pl.BlockSpec((L, sp * F), lambda m: (0, m)), pl.BlockSpec((F, D), lambda m: (b_row0 // F, 0))],
            out_specs=pl.BlockSpec((sp * L, D), lambda m: (m, 0)),
            out_shape=jax.ShapeDtypeStruct((T, D), F32),
            compiler_params=_params("parallel"),
        )(a, b)
    tk = _tile(F, 3072)
    kpj = F // tk
    nk = J * kpj
    tm = min(T // streams, STREAM_TILE)
    whole_k = a.ndim == 3 and kpj == 1 and J > 1 and J * F <= WHOLE_K_MAX
    if whole_k:
        nk = 1
    nm = T // tm
    mps = nm // streams
    assert b_row0 % tk == 0
    kb0 = b_row0 // tk
    n_ex = len(extras)
    n_ab = J if whole_k else 1

    def body(*refs):
        refs = list(refs)
        a_refs, b_refs = refs[:n_ab], refs[n_ab:2 * n_ab]
        rest = refs[2 * n_ab + len(after):]
        if n_ex:
            tmp_ref = rest.pop()
        acc_ref = rest.pop()
        ex_refs, rest = rest[:n_ex], rest[n_ex:]
        m, k = pl.program_id(0), pl.program_id(1)
        part = _dot_nn(a_refs[0][...], b_refs[0][...])
        for a_ref, b_ref in zip(a_refs[1:], b_refs[1:]):
            part = part + _dot_nn(a_ref[...], b_ref[...])

        @pl.when(k == 0)
        def _():
            acc_ref[...] = part

        @pl.when(k > 0)
        def _():
            acc_ref[...] += part

        @pl.when(k == nk - 1)
        def _():
            acc = acc_ref[...]
            for (d, _), e_ref in zip(extras, ex_refs):
                for s in range(d):
                    for c in range(D // LANES):
                        tmp_ref.at[c][pl.ds(s, tm // d, stride=d), :] = e_ref[s, :, c * LANES:(c + 1) * LANES]
                acc = acc + jnp.concatenate([tmp_ref[c] for c in range(D // LANES)], axis=1)
            if mode == "plain":
                (o_ref,) = rest
                o_ref[...] = acc
            elif mode == "res":
                ins = list(rest)
                x_ref = ins.pop(0)
                if bias is not None:
                    acc = acc + ins.pop(0)[...]
                xn = x_ref[...] + scale * acc
                if g is None:
                    (o_ref,) = ins
                else:
                    g_ref, o_ref, h_ref = ins
                    r = lax.rsqrt(jnp.mean(xn * xn, axis=-1, keepdims=True) + NORM_EPS)
                    h_ref[...] = (xn * r * g_ref[...]).astype(BF16)
                o_ref[...] = xn
            elif mode == "loss":
                x_ref, t_ref, o_ref, l_ref = rest
                e = x_ref[...] + scale * acc - t_ref[...]
                o_ref[...] = e / D

                @pl.when(m == 0)
                def _():
                    l_ref[...] = jnp.zeros_like(l_ref)

                l_ref[...] += _rows8(e * e)
            else:
                x_ref, g_ref, d_ref, o_ref, dg_ref = rest
                xv = x_ref[...]
                r = lax.rsqrt(jnp.mean(xv * xv, axis=-1, keepdims=True) + NORM_EPS)
                xh = xv * r
                dxh = acc * g_ref[...]
                o_ref[...] = d_ref[...] + r * (dxh - xh * jnp.mean(dxh * xh, axis=-1, keepdims=True))

                @pl.when(m == 0)
                def _():
                    dg_ref[...] = jnp.zeros_like(dg_ref)

                dg_ref[...] += _rows8(acc * xh)

    if whole_k:
        a_specs = [pl.BlockSpec((None, tm, F), lambda m, k, j=j: (j, m, 0)) for j in range(J)]
        b_specs = [pl.BlockSpec((F, D), lambda m, k, j=j: (kb0 + j, 0)) for j in range(J)]
    else:
        if a.ndim == 3:
            a_specs = [pl.BlockSpec((None, tm, tk), lambda m, k: (k // kpj, m, k % kpj))]
        elif streams > 1:
            a_specs = [pl.BlockSpec((tm, tk), lambda m, k: (m % mps, (m // mps) * kpj + k))]
        else:
            a_specs = [pl.BlockSpec((tm, tk), lambda m, k: (m, k))]
        b_specs = [pl.BlockSpec((tk, D), lambda m, k: (kb0 + k, 0))]
    row = pl.BlockSpec((tm, D), lambda m, k: (m, 0))
    vec = pl.BlockSpec((1, D), lambda m, k: (0, 0))
    part8 = pl.BlockSpec((8, D), lambda m, k: (0, 0))
    in_specs = a_specs + b_specs + [ANY] * len(after)
    args = [a] * n_ab + [b] * n_ab + list(after)
    for d, e in extras:
        in_specs.append(pl.BlockSpec((d, tm // d, D), lambda m, k: (0, m, 0)))
        args.append(e.reshape(d, T // d, D))
    full = jax.ShapeDtypeStruct((T, D), F32)
    small = jax.ShapeDtypeStruct((8, D), F32)
    if mode == "plain":
        out_specs, out_shape = row, full
    elif mode == "res":
        in_specs.append(row)
        args.append(x)
        if bias is not None:
            in_specs.append(vec)
            args.append(bias)
        if g is None:
            out_specs, out_shape = row, full
        else:
            in_specs.append(vec)
            args.append(g)
            out_specs, out_shape = [row, row], [full, jax.ShapeDtypeStruct((T, D), BF16)]
    elif mode == "loss":
        in_specs += [row, row]
        args += [x, tgt]
        out_specs, out_shape = [row, part8], [full, small]
    else:
        in_specs += [row, vec, row]
        args += [x, g, dxo]
        out_specs, out_shape = [row, part8], [full, small]
    return pl.pallas_call(
        body, name=name, grid=(nm, nk), in_specs=in_specs, out_specs=out_specs, out_shape=out_shape,
        scratch_shapes=[pltpu.VMEM((tm, D), F32)] + ([pltpu.VMEM((D // LANES, tm, LANES), F32)] if n_ex else []),
        compiler_params=_params("arbitrary", "arbitrary"),
    )(*args)


def _sublane_phases(sh):
    rows = sh.shape[1] - SUBLANES
    for p in range(1, SUBLANES):
        sh[p, pl.ds(0, rows), :] = sh[0, pl.ds(p, rows), :]


def _shifted(sh, offset, j, rows=SUBLANES):
    start = pl.multiple_of(j * rows + (offset - offset % SUBLANES), SUBLANES)
    return sh[offset % SUBLANES, pl.ds(start, rows), :]


def _conv_taps(wb, sh, offsets, j, init):
    groups = CONV_CHUNK // SUBLANES
    accs = [init[g * SUBLANES:(g + 1) * SUBLANES] for g in range(groups)]
    for k, off in enumerate(offsets):
        w = wb[k]
        start = pl.multiple_of(j * CONV_CHUNK + (off - off % SUBLANES), SUBLANES)
        for g in range(groups):
            accs[g] = accs[g] + w * sh[off % SUBLANES, pl.ds(start + g * SUBLANES, SUBLANES), :]
    return jnp.concatenate(accs, axis=0)


def _broadcast_rows(dst, src):
    for r in range(dst.shape[0]):
        dst[r] = jnp.zeros(dst.shape[1:], F32) + src[r:r + 1, :]


def _dwconv_fwd(u, w, b_dw, cg, name):
    T, C = u.shape
    tm = min(T, 256)
    hb = tm // CONV_HALO
    pad = CONV_HALO - (CONV_WIDTH - 1)

    def body(u_ref, halo_ref, w_ref, b_ref, g_ref, v_ref, s_ref, sh, wb):
        i = pl.program_id(0)
        sh[0, pl.ds(0, CONV_HALO), :] = jnp.where(i > 0, halo_ref[...], 0.0)
        sh[0, pl.ds(CONV_HALO, tm), :] = u_ref[...]
        _sublane_phases(sh)
        _broadcast_rows(wb, w_ref)
        bias = jnp.zeros((CONV_CHUNK, C), F32) + b_ref[...]

        def chunk(j, carry):
            acc = _conv_taps(wb, sh, [pad + k for k in range(CONV_WIDTH)], j, bias)
            v_ref[pl.ds(pl.multiple_of(j * CONV_CHUNK, CONV_CHUNK), CONV_CHUNK), :] = acc
            return carry

        lax.fori_loop(0, tm // CONV_CHUNK, chunk, 0)
        acc = v_ref[...]
        r = lax.rsqrt(jnp.mean(acc * acc, axis=-1, keepdims=True) + NORM_EPS)
        n = acc * r * g_ref[...]
        s_ref[...] = (n * _sigmoid(n)).astype(BF16)

    row = pl.BlockSpec((tm, C), lambda i: (i, 0))
    vec = pl.BlockSpec((1, C), lambda i: (0, 0))
    return pl.pallas_call(
        body, name=name, grid=(T // tm,),
        in_specs=[row, pl.BlockSpec((CONV_HALO, C), lambda i: (jnp.maximum(i * hb - 1, 0), 0)),
                  pl.BlockSpec((CONV_WIDTH, C), lambda i: (0, 0)), vec, vec],
        out_specs=[row, row],
        out_shape=[jax.ShapeDtypeStruct((T, C), F32), jax.ShapeDtypeStruct((T, C), BF16)],
        scratch_shapes=[pltpu.VMEM((SUBLANES, CONV_HALO + tm, C), F32), pltpu.VMEM((CONV_WIDTH, SUBLANES, C), F32)],
        compiler_params=_params("parallel"),
    )(u, u, w, b_dw, cg)


def _conv_post_bwd(ds, v, cg, dxo, name):
    T, C = v.shape
    tm = min(T, 512)

    def body(ds_ref, v_ref, g_ref, d_ref, dv_ref, dcg_ref, dbdw_ref, dbpw2_ref):
        @pl.when(pl.program_id(0) == 0)
        def _():
            dcg_ref[...] = jnp.zeros_like(dcg_ref)
            dbdw_ref[...] = jnp.zeros_like(dbdw_ref)
            dbpw2_ref[...] = jnp.zeros_like(dbpw2_ref)

        vv = v_ref[...]
        r = lax.rsqrt(jnp.mean(vv * vv, axis=-1, keepdims=True) + NORM_EPS)
        vh = vv * r
        n = vh * g_ref[...]
        sg = _sigmoid(n)
        dn = ds_ref[...] * (sg * (1.0 + n * (1.0 - sg)))
        dvh = dn * g_ref[...]
        dv = r * (dvh - vh * jnp.mean(dvh * vh, axis=-1, keepdims=True))
        dv_ref[...] = dv
        dcg_ref[...] += _rows8(dn * vh)
        dbdw_ref[...] += _rows8(dv)
        dbpw2_ref[...] += _rows8(d_ref[...])

    row = pl.BlockSpec((tm, C), lambda i: (i, 0))
    part8 = pl.BlockSpec((8, C), lambda i: (0, 0))
    small = jax.ShapeDtypeStruct((8, C), F32)
    return pl.pallas_call(
        body, name=name, grid=(T // tm,),
        in_specs=[row, row, pl.BlockSpec((1, C), lambda i: (0, 0)), row],
        out_specs=[row, part8, part8, part8],
        out_shape=[jax.ShapeDtypeStruct((T, C), F32), small, small, small],
        compiler_params=_params("arbitrary"),
    )(ds, v, cg, dxo)


def _dwconv_bwd(dv, u, w, z3, name):
    T, C = u.shape
    tm = min(T, 256)
    hb = tm // CONV_HALO
    nt = T // tm
    pad = CONV_HALO - (CONV_WIDTH - 1)

    taps_at_once = 4
    chunks_at_once = 8

    def body(dv_ref, dvn_ref, u_ref, up_ref, w_ref, z_ref, dz_ref, dw_ref, db_ref, dvsh, ush, wb, du_ref, dwacc):
        i = pl.program_id(0)

        @pl.when(i == 0)
        def _():
            dwacc[...] = jnp.zeros_like(dwacc)
            db_ref[...] = jnp.zeros_like(db_ref)

        dvsh[0, pl.ds(0, tm), :] = dv_ref[...]
        dvsh[0, pl.ds(tm, CONV_HALO), :] = jnp.where(i < nt - 1, dvn_ref[...], 0.0)
        ush[0, pl.ds(0, CONV_HALO), :] = jnp.where(i > 0, up_ref[...], 0.0)
        ush[0, pl.ds(CONV_HALO, tm), :] = u_ref[...]
        _sublane_phases(dvsh)
        _sublane_phases(ush)
        _broadcast_rows(wb, w_ref)

        def du_chunk(j, carry):
            acc = _conv_taps(wb, dvsh, [CONV_WIDTH - 1 - k for k in range(CONV_WIDTH)], j,
                             jnp.zeros((CONV_CHUNK, C), F32))
            du_ref[pl.ds(pl.multiple_of(j * CONV_CHUNK, CONV_CHUNK), CONV_CHUNK), :] = acc
            return carry

        lax.fori_loop(0, tm // CONV_CHUNK, du_chunk, 0)

        for k0 in range(0, CONV_WIDTH, taps_at_once):
            taps = range(k0, min(k0 + taps_at_once, CONV_WIDTH))

            def dw_chunks(jj, carry, taps=taps):
                accs = [jnp.zeros((SUBLANES, C), F32) for _ in taps]
                for u_ in range(chunks_at_once):
                    j = jj * chunks_at_once + u_
                    dvc = dv_ref[pl.ds(pl.multiple_of(j * SUBLANES, SUBLANES), SUBLANES), :]
                    accs = [acc + dvc * _shifted(ush, pad + k, j) for acc, k in zip(accs, taps)]
                for k, acc in zip(taps, accs):
                    dwacc[k] += acc
                return carry

            lax.fori_loop(0, tm // (SUBLANES * chunks_at_once), dw_chunks, 0)

        du = du_ref[...]
        a = z_ref[0].astype(F32)
        gate = z_ref[1].astype(F32)
        sg = _sigmoid(gate)
        da = du * sg
        dgate = du * a * sg * (1.0 - sg)
        dz_ref[0] = da.astype(BF16)
        dz_ref[1] = dgate.astype(BF16)
        db_ref[0:8, :] += _rows8(da)
        db_ref[8:16, :] += _rows8(dgate)

        @pl.when(i == nt - 1)
        def _():
            dw_ref[...] = jnp.zeros_like(dw_ref)
            for k in range(CONV_WIDTH):
                dw_ref[k:k + 1, :] = jnp.sum(dwacc[k], axis=0, keepdims=True)

    row = pl.BlockSpec((tm, C), lambda i: (i, 0))
    last_halo = T // CONV_HALO - 1
    return pl.pallas_call(
        body, name=name, grid=(nt,),
        in_specs=[row, pl.BlockSpec((CONV_HALO, C), lambda i: (jnp.minimum((i + 1) * hb, last_halo), 0)),
                  row, pl.BlockSpec((CONV_HALO, C), lambda i: (jnp.maximum(i * hb - 1, 0), 0)),
                  pl.BlockSpec((CONV_WIDTH, C), lambda i: (0, 0)),
                  pl.BlockSpec((2, tm, C), lambda i: (0, i, 0))],
        out_specs=[pl.BlockSpec((2, tm, C), lambda i: (0, i, 0)),
                   pl.BlockSpec((CONV_HALO, C), lambda i: (0, 0)),
                   pl.BlockSpec((16, C), lambda i: (0, 0))],
        out_shape=[jax.ShapeDtypeStruct((2, T, C), BF16), jax.ShapeDtypeStruct((CONV_HALO, C), F32),
                   jax.ShapeDtypeStruct((16, C), F32)],
        scratch_shapes=[pltpu.VMEM((SUBLANES, tm + CONV_HALO, C), F32), pltpu.VMEM((SUBLANES, CONV_HALO + tm, C), F32),
                        pltpu.VMEM((CONV_WIDTH, SUBLANES, C), F32), pltpu.VMEM((tm, C), F32),
                        pltpu.VMEM((CONV_WIDTH, SUBLANES, C), F32)],
        compiler_params=_params("arbitrary"),
    )(dv, dv, u, u, w, z3)


def _head_norm(raw, gain):
    xv = raw.astype(F32)
    r = lax.rsqrt(jnp.mean(xv * xv, axis=-1, keepdims=True) + NORM_EPS)
    xh = xv * r
    return (xh * gain).astype(BF16), xh, r


def _band_mask(n):
    row = lax.broadcasted_iota(I32, (ATTN_BLOCK, 2 * ATTN_BLOCK), 0)
    col = lax.broadcasted_iota(I32, (ATTN_BLOCK, 2 * ATTN_BLOCK), 1)
    return (col >= row) & (col <= row + ATTN_BLOCK) & ((col >= ATTN_BLOCK) | (n > 0))


def _carry_previous(n, bufs):
    @pl.when(n == 0)
    def _():
        for b in bufs:
            b[0:ATTN_BLOCK, :] = jnp.zeros((ATTN_BLOCK, b.shape[1]), b.dtype)

    @pl.when(n > 0)
    def _():
        for b in bufs:
            b[0:ATTN_BLOCK, :] = b[ATTN_BLOCK:, :]


def _attn_fwd(qkv, qg, kg, g, name):
    d = ATTN_DILATIONS[g]
    L = qkv.shape[0]
    HD = qkv.shape[1] // (3 * d)
    H = HD // HEAD_DIM
    nb = L // ATTN_BLOCK
    scale = HEAD_DIM ** -0.5

    def body(q_ref, kc_ref, vc_ref, qg_ref, kg_ref, o_ref, lse_ref, kk, vv):
        n = pl.program_id(1)
        _carry_previous(n, [kk, vv])
        mask = _band_mask(n)
        lane = lax.broadcasted_iota(I32, (ATTN_BLOCK, 128), 1)
        gq = qg_ref[g:g + 1, :]
        gk = kg_ref[g:g + 1, :]
        vv[ATTN_BLOCK:, :] = vc_ref[...]
        lse_tile = jnp.zeros((ATTN_BLOCK, 128), F32)
        for h in range(H):
            sl = slice(h * HEAD_DIM, (h + 1) * HEAD_DIM)
            q = _head_norm(q_ref[:, sl], gq)[0]
            kk[ATTN_BLOCK:, sl] = _head_norm(kc_ref[:, sl], gk)[0]
            s = jnp.where(mask, _dot_nt(q, kk[:, sl]) * scale, NEG_BIG)
            m = jnp.max(s, axis=-1, keepdims=True)
            p = jnp.exp(s - m)
            l = jnp.sum(p, axis=-1, keepdims=True)
            o_ref[:, sl] = _dot_nn(p.astype(BF16), vv[:, sl]) / l
            lse_tile = jnp.where(lane == h, m + jnp.log(l), lse_tile)
        lse_ref[...] = lse_tile

    def blk(which):
        return pl.BlockSpec((ATTN_BLOCK, HD), lambda r, n: (n, r * 3 + which))

    gains = pl.BlockSpec((3, HEAD_DIM), lambda r, n: (0, 0))
    return pl.pallas_call(
        body, name=name, grid=(d, nb),
        in_specs=[blk(0), blk(1), blk(2), gains, gains],
        out_specs=[pl.BlockSpec((ATTN_BLOCK, HD), lambda r, n: (n, r)),
                   pl.BlockSpec((ATTN_BLOCK, 128), lambda r, n: (n, r))],
        out_shape=[jax.ShapeDtypeStruct((L, d * HD), F32), jax.ShapeDtypeStruct((L, d * 128), F32)],
        scratch_shapes=[pltpu.VMEM((2 * ATTN_BLOCK, HD), BF16), pltpu.VMEM((2 * ATTN_BLOCK, HD), BF16)],
        compiler_params=_params("arbitrary", "arbitrary"),
    )(qkv, qkv, qkv, qg, kg)


def _attn_merge(os, lses, name):
    T = os[0].shape[0]
    HD = os[0].shape[1]
    H = HD // HEAD_DIM
    tm = min(T, STREAM_TILE)
    dils = ATTN_DILATIONS[1:]

    def body(o0, o1, o2, l0, l1, l2, ob_ref, of_ref, lse_ref, n1, n2, m1, m2):
        for d, src, dst, w in ((dils[0], o1, n1, HD), (dils[1], o2, n2, HD), (dils[0], l1, m1, 128), (dils[1], l2, m2, 128)):
            for s in range(d):
                for c in range(w // LANES):
                    dst.at[c][pl.ds(s, tm // d, stride=d), :] = src[:, s * w + c * LANES:s * w + (c + 1) * LANES]
        ls = [l0[...], m1[0], m2[0]]
        m = jnp.maximum(jnp.maximum(ls[0], ls[1]), ls[2])
        tot = m + jnp.log(jnp.exp(ls[0] - m) + jnp.exp(ls[1] - m) + jnp.exp(ls[2] - m))
        lse_ref[...] = tot
        ws = [jnp.exp(l - tot) for l in ls]
        for h in range(H):
            sl = slice(h * HEAD_DIM, (h + 1) * HEAD_DIM)
            out = (ws[0][:, h:h + 1] * o0[:, sl] + ws[1][:, h:h + 1] * n1[h]) + ws[2][:, h:h + 1] * n2[h]
            of_ref[:, sl] = out
            ob_ref[:, sl] = out.astype(BF16)

    def wide(d, w):
        return pl.BlockSpec((tm // d, d * w), lambda i: (i, 0))

    row = pl.BlockSpec((tm, HD), lambda i: (i, 0))
    nar = pl.BlockSpec((tm, 128), lambda i: (i, 0))
    return pl.pallas_call(
        body, name=name, grid=(T // tm,),
        in_specs=[row, wide(dils[0], HD), wide(dils[1], HD), nar, wide(dils[0], 128), wide(dils[1], 128)],
        out_specs=[row, row, nar],
        out_shape=[jax.ShapeDtypeStruct((T, HD), BF16), jax.ShapeDtypeStruct((T, HD), F32),
                   jax.ShapeDtypeStruct((T, 128), F32)],
        scratch_shapes=[pltpu.VMEM((H, tm, HEAD_DIM), F32), pltpu.VMEM((H, tm, HEAD_DIM), F32),
                        pltpu.VMEM((1, tm, 128), F32), pltpu.VMEM((1, tm, 128), F32)],
        compiler_params=_params("parallel"),
    )(*os, *lses)


def _attn_delta(out, dout, lse, name):
    T, HD = out.shape
    H = HD // HEAD_DIM
    tm = min(T, STREAM_TILE)
    dils = ATTN_DILATIONS[1:]

    def body(o_ref, d_ref, l_ref, dl_ref, *rest):
        dch = rest[-1]
        lane = lax.broadcasted_iota(I32, (tm, 128), 1)
        tile = jnp.zeros((tm, 128), F32)
        for h in range(H):
            sl = slice(h * HEAD_DIM, (h + 1) * HEAD_DIM)
            dch[h] = d_ref[:, sl]
            tile = jnp.where(lane == h, jnp.sum(o_ref[:, sl] * d_ref[:, sl], axis=-1, keepdims=True), tile)
        dl_ref[...] = tile
        for i, d in enumerate(dils):
            dw_ref, lw_ref, dlw_ref = rest[3 * i:3 * i + 3]
            for s in range(d):
                rows = pl.ds(s, tm // d, stride=d)
                for h in range(H):
                    dw_ref[:, s * HD + h * HEAD_DIM:s * HD + (h + 1) * HEAD_DIM] = dch.at[h][rows, :].astype(BF16)
                lw_ref[:, s * 128:(s + 1) * 128] = l_ref[rows, :]
                dlw_ref[:, s * 128:(s + 1) * 128] = dl_ref[rows, :]

    row = pl.BlockSpec((tm, HD), lambda i: (i, 0))
    nar = pl.BlockSpec((tm, 128), lambda i: (i, 0))
    out_specs, out_shape = [nar], [jax.ShapeDtypeStruct((T, 128), F32)]
    for d in dils:
        out_specs += [pl.BlockSpec((tm // d, d * w), lambda i: (i, 0)) for w in (HD, 128, 128)]
        out_shape += [jax.ShapeDtypeStruct((T // d, d * HD), BF16), jax.ShapeDtypeStruct((T // d, d * 128), F32),
                      jax.ShapeDtypeStruct((T // d, d * 128), F32)]
    return pl.pallas_call(
        body, name=name, grid=(T // tm,), in_specs=[row, row, nar], out_specs=out_specs, out_shape=out_shape,
        scratch_shapes=[pltpu.VMEM((H, tm, HEAD_DIM), F32)],
        compiler_params=_params("parallel"),
    )(out, dout, lse)


def _attn_bwd(qkv, dout, lse, delta, qg, kg, g, name):
    d = ATTN_DILATIONS[g]
    L = qkv.shape[0]
    HD = qkv.shape[1] // (3 * d)
    H = HD // HEAD_DIM
    nb = L // ATTN_BLOCK
    total = d * nb
    scale = HEAD_DIM ** -0.5

    def body(*refs):
        (q_ref, kc_ref, vc_ref, do_ref, lse_ref, dl_ref, qg_ref, kg_ref,
         out_ref, gg_ref, dq_a, dk_a, dv_a, dq_b, dk_b, dv_b, kk, vv, khh, rkk) = refs
        t = pl.program_id(0)
        n = t % nb
        gq = qg_ref[g:g + 1, :]
        gk = kg_ref[g:g + 1, :]
        _carry_previous(t, [kk, vv, khh, rkk])

        @pl.when(t == 0)
        def _():
            gg_ref[...] = jnp.zeros_like(gg_ref)
            dk_a[...] = jnp.zeros_like(dk_a)
            dv_a[...] = jnp.zeros_like(dv_a)

        @pl.when(t < total)
        def _():
            mask = _band_mask(n)
            vv[ATTN_BLOCK:, :] = vc_ref[...]
            gq_sum = jnp.zeros((1, HEAD_DIM), F32)
            for h in range(H):
                sl = slice(h * HEAD_DIM, (h + 1) * HEAD_DIM)
                qn, qh, rq = _head_norm(q_ref[:, sl], gq)
                kcn, kch, rkc = _head_norm(kc_ref[:, sl], gk)
                kk[ATTN_BLOCK:, sl] = kcn
                khh[ATTN_BLOCK:, sl] = kch
                rkk[ATTN_BLOCK:, sl] = jnp.broadcast_to(rkc, (ATTN_BLOCK, HEAD_DIM))
                kn = kk[:, sl]
                p = jnp.where(mask, jnp.exp(_dot_nt(qn, kn) * scale - lse_ref[:, h:h + 1]), 0.0)
                do = do_ref[:, sl].astype(BF16)
                ds = (p * (_dot_nt(do, vv[:, sl]) - dl_ref[:, h:h + 1]) * scale).astype(BF16)
                dqn = _dot_nn(ds, kn)
                dqh = dqn * gq
                dq_b[:, sl] = rq * (dqh - qh * jnp.mean(dqh * qh, axis=-1, keepdims=True))
                gq_sum = gq_sum + jnp.sum(dqn * qh, axis=0, keepdims=True)
                dkn = _dot_tn(ds, qn)
                dvn = _dot_tn(p.astype(BF16), do)
                dk_a[:, sl] += dkn[0:ATTN_BLOCK]
                dv_a[:, sl] += dvn[0:ATTN_BLOCK]
                dk_b[:, sl] = dkn[ATTN_BLOCK:]
                dv_b[:, sl] = dvn[ATTN_BLOCK:]
            gg_ref[0:1, :] += gq_sum

        @pl.when(t > 0)
        def _():
            out_ref[:, 0:HD] = dq_a[...].astype(BF16)
            gk_sum = jnp.zeros((1, HEAD_DIM), F32)
            for h in range(H):
                sl = slice(h * HEAD_DIM, (h + 1) * HEAD_DIM)
                kh = khh[0:ATTN_BLOCK, sl]
                rk = rkk[0:ATTN_BLOCK, sl]
                dkn = dk_a[:, sl]
                dkh = dkn * gk
                dk = rk * (dkh - kh * jnp.mean(dkh * kh, axis=-1, keepdims=True))
                out_ref[:, HD + h * HEAD_DIM:HD + (h + 1) * HEAD_DIM] = dk.astype(BF16)
                gk_sum = gk_sum + jnp.sum(dkn * kh, axis=0, keepdims=True)
            out_ref[:, 2 * HD:3 * HD] = dv_a[...].astype(BF16)
            gg_ref[1:2, :] += gk_sum

        @pl.when(t < total)
        def _():
            dq_a[...] = dq_b[...]
            dk_a[...] = dk_b[...]
            dv_a[...] = dv_b[...]

    def at(t):
        t = jnp.minimum(t, total - 1)
        return t % nb, t // nb

    def blk(which):
        return pl.BlockSpec((ATTN_BLOCK, HD), lambda t: (at(t)[0], at(t)[1] * 3 + which))

    gains = pl.BlockSpec((3, HEAD_DIM), lambda t: (0, 0))
    nar = pl.BlockSpec((ATTN_BLOCK, 128), lambda t: at(t))
    acc = pltpu.VMEM((ATTN_BLOCK, HD), F32)
    pair16 = pltpu.VMEM((2 * ATTN_BLOCK, HD), BF16)
    pair32 = pltpu.VMEM((2 * ATTN_BLOCK, HD), F32)
    return pl.pallas_call(
        body, name=name, grid=(total + 1,),
        in_specs=[blk(0), blk(1), blk(2), pl.BlockSpec((ATTN_BLOCK, HD), lambda t: at(t)), nar, nar, gains, gains],
        out_specs=[pl.BlockSpec((ATTN_BLOCK, 3 * HD), lambda t: at(jnp.maximum(t - 1, 0))),
                   pl.BlockSpec((8, HEAD_DIM), lambda t: (0, 0))],
        out_shape=[jax.ShapeDtypeStruct((L, d * 3 * HD), BF16), jax.ShapeDtypeStruct((8, HEAD_DIM), F32)],
        scratch_shapes=[acc, acc, acc, acc, acc, acc, pair16, pair16, pair32, pair32],
        compiler_params=_params("arbitrary"),
    )(qkv, qkv, qkv, dout, lse, delta, qg, kg)


def _adamw(w, grad, m, v, name):
    if w.ndim == 2:
        R, C = w.shape
        tr = _row_tile(R, 512)
        blk, steps = pl.BlockSpec((tr, C), lambda i: (i, 0)), R // tr
    else:
        zeros = (0,) * w.ndim
        blk, steps = pl.BlockSpec(w.shape, lambda i: zeros), 1

    def body(w_ref, g_ref, m_ref, v_ref, d_ref, nm_ref, nv_ref):
        gv = g_ref[...]
        nm = ADAM_B1 * m_ref[...] + (1.0 - ADAM_B1) * gv
        nv = ADAM_B2 * v_ref[...] + (1.0 - ADAM_B2) * (gv * gv)
        m_hat = nm / (1.0 - ADAM_B1 ** ADAM_STEP)
        v_hat = nv / (1.0 - ADAM_B2 ** ADAM_STEP)
        d_ref[...] = -ADAM_LR * (m_hat / (jnp.sqrt(v_hat) + ADAM_EPS) + ADAM_WD * w_ref[...])
        nm_ref[...] = nm
        nv_ref[...] = nv

    shp = jax.ShapeDtypeStruct(w.shape, F32)
    return pl.pallas_call(
        body, name=name, grid=(steps,), in_specs=[blk] * 4, out_specs=[blk] * 3, out_shape=[shp] * 3,
        compiler_params=_params("parallel"),
    )(w, grad, m, v)


def _ffn_fwd(x, h, comm, s, tag, tgt=None, next_g=None):
    (win_t, *rest), tok = comm.weights(s, h)
    z3, a = _gated_in(h, win_t, None, "swiglu", BF16, f"ffn_in_{tag}", after=tok)
    tok = ()
    if not rest:
        rest, tok = comm.second(s, a)
    wout, = rest
    saved = (x, h, z3, a, win_t, wout)
    if tgt is None:
        return _rows_mm(a, wout, "res", f"ffn_out_{tag}", x=x, scale=0.5, g=next_g, after=tok), saved
    return _rows_mm(a, wout, "loss", f"ffn_out_loss_{tag}", x=x, scale=0.5, tgt=tgt), saved


def _ffn_bwd(dxo, saved, g, comm, s, tag, after):
    x, h, z3, a, win_t, wout = saved
    dz3 = _swiglu_bwd(dxo, wout, z3, 0.5, f"ffn_out_bwd_{tag}", after=after)
    d_wout = _mm_tn(a, dxo, 0.5, f"ffn_dwout_{tag}")
    d_win_t = _mm_tn(dz3, h, 1.0, f"ffn_dwin_{tag}")
    tok = comm.grads(s, [d_win_t, d_wout])
    dx, dg8 = _rows_mm(dz3, win_t, "rmsbwd", f"ffn_in_bwd_{tag}", x=x, g=g, dxo=dxo, after=tok)
    return dx, dg8, ()


def _local_step(x, tgt, vecs, comm):
    norm_g = vecs["norm_g"]

    def gvec(i):
        return norm_g[i:i + 1]

    h0 = _rmsnorm(x, gvec(0), "rmsnorm_l0a", after=comm.started)
    (x1, h_c), sv_a0 = _ffn_fwd(x, h0, comm, 0, "l0a", next_g=gvec(1))
    (pw1_t, pw2), tok = comm.weights(1, h_c)
    zc3, u = _gated_in(h_c, pw1_t, vecs["b_pw1"], "glu", F32, "conv_pw1", after=tok)
    v, s = _dwconv_fwd(u, vecs["w_dw"], vecs["b_dw"], vecs["cg"], "conv_dw")
    x2, h_b0 = _rows_mm(s, pw2, "res", "conv_pw2", x=x1, bias=vecs["b_pw2"], g=gvec(2))
    (x3, h_a1), sv_b0 = _ffn_fwd(x2, h_b0, comm, 2, "l0b", next_g=gvec(3))
    x4, sv_a1 = _ffn_fwd(x3, h_a1, comm, 3, "l1a")
    h_s = _rmsnorm_streams(x4, gvec(4), "rmsnorm_attn")
    (qkv_t, wo), tok = comm.weights(4, h_s[0])
    hd3 = qkv_t.shape[0] // 3
    qkvs, os, lses = [], [], []
    for g, d in enumerate(ATTN_DILATIONS):
        qkv = _mm_nt(h_s[g], qkv_t, BF16, f"attn_qkv_g{g}", w_row0=g * hd3, n_rows=hd3, streams=d, after=tok)
        o, lse = _attn_fwd(qkv, vecs["q_norm"], vecs["k_norm"], g, f"attn_fwd_g{g}")
        qkvs.append(qkv)
        os.append(o)
        lses.append(lse)
    out_b, out_f, lse_tot = _attn_merge(os, lses, "attn_merge")
    x5, h_b1 = _rows_mm(out_b, wo, "res", "attn_wo", x=x4, g=gvec(5))
    (dy, loss8), sv_b1 = _ffn_fwd(x5, h_b1, comm, 5, "l1b", tgt=tgt)
    loss = 0.5 * jnp.sum(loss8) / x.shape[1]

    dg = [None] * 6
    dx, dg[5], tok = _ffn_bwd(dy, sv_b1, gvec(5), comm, 5, "l1b", ())
    dout = _mm_nt(dx, wo, F32, "attn_wo_bwd", after=tok)
    d_wo = _mm_tn(out_b, dx, 1.0, "attn_dwo")
    delta, *wide = _attn_delta(out_f, dout, lse_tot, "attn_delta")
    per_group = [(dout, lse_tot, delta), tuple(wide[0:3]), tuple(wide[3:6])]
    ggs, d_qkv_t, dhs = [], [], []
    for g, d in enumerate(ATTN_DILATIONS):
        dqkv, gg = _attn_bwd(qkvs[g], *per_group[g], vecs["q_norm"], vecs["k_norm"], g, f"attn_bwd_g{g}")
        ggs.append(gg)
        d_qkv_t.append(_mm_tn(dqkv, h_s[g], 1.0, f"attn_dwqkv_g{g}", streams=d))
        if d > 1:
            dhs.append((d, _rows_mm(dqkv, qkv_t, "plain", f"attn_qkv_bwd_g{g}", b_row0=g * hd3, streams=d)))
        else:
            dqkv0 = dqkv
    tok = comm.grads(4, [jnp.concatenate(d_qkv_t), d_wo])
    dx, dg[4] = _rows_mm(dqkv0, qkv_t, "rmsbwd", "attn_qkv_bwd_g0", x=x4, g=gvec(4), dxo=dx, extras=dhs, after=tok)
    dx, dg[3], tok = _ffn_bwd(dx, sv_a1, gvec(3), comm, 3, "l1a", ())
    dx, dg[2], tok = _ffn_bwd(dx, sv_b0, gvec(2), comm, 2, "l0b", tok)
    ds = _mm_nt(dx, pw2, F32, "conv_pw2_bwd", after=tok)
    d_pw2 = _mm_tn(s, dx, 1.0, "conv_dwpw2")
    dv, dcg8, dbdw8, dbpw2_8 = _conv_post_bwd(ds, v, vecs["cg"], dx, "conv_post_bwd")
    dzc3, d_wdw, db1_16 = _dwconv_bwd(dv, u, vecs["w_dw"], zc3, "conv_dw_bwd")
    d_pw1_t = _mm_tn(dzc3, h_c, 1.0, "conv_dwpw1")
    tok = comm.grads(1, [d_pw1_t, d_pw2])
    dx, dg[1] = _rows_mm(dzc3, pw1_t, "rmsbwd", "conv_pw1_bwd", x=x1, g=gvec(1), dxo=dx, after=tok)
    dx, dg[0], tok = _ffn_bwd(dx, sv_a0, gvec(0), comm, 0, "l0a", ())

    D = x.shape[1]
    small = {
        "norm_g": jnp.stack([p.sum(axis=0) for p in dg]),
        "b_pw1": db1_16.reshape(2, 8, D).sum(axis=1),
        "w_dw": d_wdw[:CONV_WIDTH],
        "b_dw": dbdw8.sum(axis=0, keepdims=True),
        "cg": dcg8.sum(axis=0, keepdims=True),
        "b_pw2": dbpw2_8.sum(axis=0, keepdims=True),
        "q_norm": jnp.stack([gg[0] for gg in ggs]),
        "k_norm": jnp.stack([gg[1] for gg in ggs]),
    }
    return loss, dx, small, tok


def _mesh_pos():
    return lax.axis_index("x"), lax.axis_index("y"), lax.axis_index("c")


def _other_chips(x, y):
    return [(1 - x, y), (x, 1 - y), (1 - x, 1 - y)]


HBM = pl.BlockSpec(memory_space=pltpu.HBM)
SEM = pl.BlockSpec(memory_space=pltpu.SEMAPHORE)
SPLIT_COPY = pltpu.SideEffectType.DATAFLOW_SIDE_EFFECTING


def _hbm(a):
    return pltpu.with_memory_space_constraint(a, pltpu.HBM)


def _gather_copies(j, src, out, send, recv, x, y, c):
    me = 4 * x + 2 * y + c
    peers = [(x, y, 1 - c)] + [(px, py, c) for px, py in _other_chips(x, y)]
    return [pltpu.make_async_remote_copy(src_ref=src, dst_ref=out.at[me], send_sem=send.at[4 * j + k],
                                         recv_sem=recv.at[4 * j + k], device_id=peer, device_id_type=MESH)
            for k, peer in enumerate(peers)]


def _gather_start(shards, me, after, name):
    n = len(shards)
    lands = [lax.dynamic_update_slice(lax.empty((N_DEV,) + s.shape, s.dtype), s[None], (me, 0, 0)) for s in shards]

    def body(*refs):
        ins, outs = refs[:n], refs[2 * n + len(after):3 * n + len(after)]
        send, recv, token = refs[4 * n + len(after):]
        x, y, c = _mesh_pos()
        for j in range(n):
            for cp in _gather_copies(j, ins[j], outs[j], send, recv, x, y, c):
                cp.start()
        token[...] = jnp.zeros_like(token)

    res = pl.pallas_call(
        body, name=name, in_specs=[HBM] * (2 * n) + [ANY] * len(after),
        out_specs=[HBM] * (2 * n) + [SEM, SEM, pl.BlockSpec(memory_space=pltpu.VMEM)],
        out_shape=[pltpu.HBM((N_DEV,) + s.shape, s.dtype) for s in shards] + [pltpu.HBM(s.shape, s.dtype) for s in shards]
        + [pltpu.SemaphoreType.DMA((4 * n,)), pltpu.SemaphoreType.DMA((4 * n,)), jax.ShapeDtypeStruct((8, 128), F32)],
        input_output_aliases={**{i: n + i for i in range(n)}, **{n + i: i for i in range(n)}},
        compiler_params=pltpu.CompilerParams(has_side_effects=SPLIT_COPY),
    )(*[_hbm(s) for s in shards], *[_hbm(l) for l in lands], *after)
    return res[:n], res[n:2 * n], res[2 * n], res[2 * n + 1], res[2 * n + 2]


def _gather_wait(outs, thru, send, recv, after, name):
    n = len(outs)

    def body(*refs):
        out_refs, thru_refs, send_ref, recv_ref = refs[:n], refs[n:2 * n], refs[2 * n], refs[2 * n + 1]
        x, y, c = _mesh_pos()
        for j in range(n):
            for cp in _gather_copies(j, thru_refs[j], out_refs[j], send_ref, recv_ref, x, y, c):
                cp.wait_send()
                cp.wait_recv()

    res = pl.pallas_call(
        body, name=name, in_specs=[HBM] * (2 * n) + [SEM, SEM] + [ANY] * len(after), out_specs=[HBM] * (2 * n),
        out_shape=[pltpu.HBM(a.shape, a.dtype) for a in list(outs) + list(thru)],
        input_output_aliases={i: i for i in range(2 * n)},
        compiler_params=pltpu.CompilerParams(has_side_effects=SPLIT_COPY),
    )(*outs, *thru, send, recv, *after)
    return res[:n]


def _gather_forward(outs, name):
    n = len(outs)

    def body(*refs):
        o = refs[n:2 * n]
        send_sems, recv_sems = refs[2 * n:]
        x, y, c = _mesh_pos()
        copies = []
        for j in range(n):
            for k, (px, py) in enumerate(_other_chips(x, y)):
                blk = o[j].at[4 * px + 2 * py + c]
                cp = pltpu.make_async_remote_copy(src_ref=blk, dst_ref=blk, send_sem=send_sems.at[j, k],
                                                  recv_sem=recv_sems.at[j, k], device_id=(x, y, 1 - c), device_id_type=MESH)
                cp.start()
                copies.append(cp)
        for cp in copies:
            cp.wait()

    return pl.pallas_call(
        body, name=name, in_specs=[ANY] * n, out_specs=[ANY] * n,
        out_shape=[jax.ShapeDtypeStruct(a.shape, a.dtype) for a in outs],
        input_output_aliases={i: i for i in range(n)},
        scratch_shapes=[pltpu.SemaphoreType.DMA((n, 3)), pltpu.SemaphoreType.DMA((n, 3))],
    )(*outs)


def _all_sum(p, name):
    R, C = p.shape

    def body(p_ref, o_ref, buf, send_sems, recv_sems):
        x, y, c = _mesh_pos()
        me = 4 * x + 2 * y + c
        buf[me] = p_ref[...]
        copies = []
        for rel in range(1, N_DEV):
            peer = (1 - x if rel & 4 else x, 1 - y if rel & 2 else y, 1 - c if rel & 1 else c)
            cp = pltpu.make_async_remote_copy(
                src_ref=p_ref, dst_ref=buf.at[me], send_sem=send_sems.at[rel - 1], recv_sem=recv_sems.at[rel - 1],
                device_id=peer, device_id_type=MESH)
            cp.start()
            copies.append(cp)
        for cp in copies:
            cp.wait()
        acc = buf[0]
        for dev in range(1, N_DEV):
            acc = acc + buf[dev]
        o_ref[...] = acc

    vm = pl.BlockSpec(memory_space=pltpu.VMEM)
    return pl.pallas_call(
        body, name=name, in_specs=[vm], out_specs=vm, out_shape=jax.ShapeDtypeStruct((R, C), F32),
        scratch_shapes=[pltpu.VMEM((N_DEV, R, C), F32), pltpu.SemaphoreType.DMA((N_DEV - 1,)),
                        pltpu.SemaphoreType.DMA((N_DEV - 1,))],
    )(p)


def _swap_with_sibling(gs, name):
    n = len(gs)

    def body(*refs):
        ins, outs = refs[:n], refs[n:2 * n]
        send_sems, recv_sems = refs[2 * n:]
        x, y, c = _mesh_pos()
        copies = []
        for i in range(n):
            for k in range(4):
                cp = pltpu.make_async_remote_copy(
                    src_ref=ins[i].at[2 * k + (1 - c)], dst_ref=outs[i].at[k],
                    send_sem=send_sems.at[i, k], recv_sem=recv_sems.at[i, k],
                    device_id=(x, y, 1 - c), device_id_type=MESH)
                cp.start()
                copies.append(cp)
        for cp in copies:
            cp.wait()

    return pl.pallas_call(
        body, name=name, in_specs=[ANY] * n, out_specs=[ANY] * n,
        out_shape=[jax.ShapeDtypeStruct((4,) + g.shape[1:], g.dtype) for g in gs],
        scratch_shapes=[pltpu.SemaphoreType.DMA((n, 4)), pltpu.SemaphoreType.DMA((n, 4))],
    )(*gs)


def _scatter_copies(j, src, land, send, recv, x, y, c):
    return [pltpu.make_async_remote_copy(src_ref=src.at[2 * px + py], dst_ref=land.at[k], send_sem=send.at[3 * j + k],
                                         recv_sem=recv.at[3 * j + k], device_id=(px, py, c), device_id_type=MESH)
            for k, (px, py) in enumerate(_other_chips(x, y))]


def _scatter_start(ps, name):
    n = len(ps)

    def body(*refs):
        ins, lands = refs[:n], refs[2 * n:3 * n]
        send, recv, token = refs[3 * n:]
        x, y, c = _mesh_pos()
        for j in range(n):
            for cp in _scatter_copies(j, ins[j], lands[j], send, recv, x, y, c):
                cp.start()
        token[...] = jnp.zeros_like(token)

    res = pl.pallas_call(
        body, name=name, in_specs=[HBM] * n,
        out_specs=[HBM] * (2 * n) + [SEM, SEM, pl.BlockSpec(memory_space=pltpu.VMEM)],
        out_shape=[pltpu.HBM(p.shape, p.dtype) for p in ps] + [pltpu.HBM((3,) + p.shape[1:], p.dtype) for p in ps]
        + [pltpu.SemaphoreType.DMA((3 * n,)), pltpu.SemaphoreType.DMA((3 * n,)), jax.ShapeDtypeStruct((8, 128), F32)],
        input_output_aliases={i: i for i in range(n)},
        compiler_params=pltpu.CompilerParams(has_side_effects=SPLIT_COPY),
    )(*[_hbm(p) for p in ps])
    return res[:n], res[n:2 * n], res[2 * n], res[2 * n + 1], res[2 * n + 2]


def _scatter_wait(thru, lands, send, recv, after, name):
    n = len(thru)

    def body(*refs):
        thru_refs, land_refs, send_ref, recv_ref = refs[:n], refs[n:2 * n], refs[2 * n], refs[2 * n + 1]
        x, y, c = _mesh_pos()
        for j in range(n):
            for cp in _scatter_copies(j, thru_refs[j], land_refs[j], send_ref, recv_ref, x, y, c):
                cp.wait_send()
                cp.wait_recv()

    res = pl.pallas_call(
        body, name=name, in_specs=[HBM] * (2 * n) + [SEM, SEM, ANY], out_specs=[HBM] * (2 * n),
        out_shape=[pltpu.HBM(a.shape, a.dtype) for a in list(thru) + list(lands)],
        input_output_aliases={i: i for i in range(2 * n)},
        compiler_params=pltpu.CompilerParams(has_side_effects=SPLIT_COPY),
    )(*thru, *lands, send, recv, after)
    return res[:n], res[n:]


def _row_tile(r, pref):
    if r <= pref:
        return r
    for t in range(pref - pref % 16, 0, -16):
        if r % t == 0:
            return t
    return r


def _add_sibling(g, r1, pos, name):
    _, r, D = g.shape
    tr = _row_tile(r, 512)

    def body(pos_ref, g_ref, r_ref, o_ref):
        o_ref[...] = (g_ref[...].astype(F32) + r_ref[...].astype(F32)).astype(BF16)

    return pl.pallas_call(
        body, name=name,
        grid_spec=pltpu.PrefetchScalarGridSpec(
            num_scalar_prefetch=1, grid=(4, r // tr),
            in_specs=[pl.BlockSpec((None, None, tr, D), lambda k, j, pos: (k, pos[0], j, 0)),
                      pl.BlockSpec((None, tr, D), lambda k, j, pos: (k, j, 0))],
            out_specs=pl.BlockSpec((None, tr, D), lambda k, j, pos: (k, j, 0))),
        out_shape=jax.ShapeDtypeStruct((4, r, D), BF16),
        compiler_params=_params("parallel", "parallel"),
    )(pos, g.reshape(4, 2, r, D), r1)


def _add_chips(p, r2, pos, name):
    _, r, D = p.shape
    tr = _row_tile(r, 512)

    def body(pos_ref, p_ref, r_ref, o_ref):
        acc = p_ref[...].astype(F32)
        for j in range(3):
            acc = acc + r_ref[j].astype(F32)
        o_ref[...] = acc

    return pl.pallas_call(
        body, name=name,
        grid_spec=pltpu.PrefetchScalarGridSpec(
            num_scalar_prefetch=1, grid=(r // tr,),
            in_specs=[pl.BlockSpec((None, tr, D), lambda j, pos: (pos[1], j, 0)),
                      pl.BlockSpec((3, tr, D), lambda j, pos: (0, j, 0))],
            out_specs=pl.BlockSpec((tr, D), lambda j, pos: (j, 0))),
        out_shape=jax.ShapeDtypeStruct((r, D), F32),
        compiler_params=_params("parallel"),
    )(pos, p, r2)


def _place_cols(local, me, width):
    R, w = local.shape
    return lax.dynamic_update_slice(jnp.zeros((R, width), F32), local, (0, me * w))


def _pad_rows(a, rows):
    return jnp.pad(a, ((0, rows - a.shape[0]), (0, 0)))


SUBLAYER_SHARDS = ((0, 4), (8, 9), (1, 5), (2, 6), (10, 11), (3, 7))
GATHER_UNITS = ((0,), (4,), (8, 9), (1, 5), (2, 6), (10, 11), (3, 7))
SUBLAYER_UNITS = ((0, 1), (2,), (3,), (4,), (5,), (6,))
GATHERS_AHEAD = 3


class _StepComm:
    def __init__(self, shards, pos):
        self.pos = pos
        self.shards = shards
        self.gathers = {}
        tok = ()
        for u in range(GATHERS_AHEAD):
            tok = self._start_gather(u, tok)
        self.started = tok
        self.next_unit = GATHERS_AHEAD
        self.pending = {}

    def _start_gather(self, u, after):
        outs, thru, send, recv, token = _gather_start([self.shards[i] for i in GATHER_UNITS[u]],
                                                      2 * self.pos[1] + self.pos[0], after, f"gather_start_{u}")
        self.gathers[u] = (outs, thru, send, recv)
        return (token,)

    def _take(self, u, after):
        outs = _gather_wait(*self.gathers.pop(u), (after,), f"gather_wait_{u}")
        outs = _gather_forward(outs, f"gather_forward_{u}")
        tok = ()
        if self.next_unit < len(GATHER_UNITS):
            tok = self._start_gather(self.next_unit, (outs[0],))
            self.next_unit += 1
        return [w.reshape(-1, w.shape[-1]) for w in outs], tok

    def weights(self, s, after):
        return self._take(SUBLAYER_UNITS[s][0], after)

    def second(self, s, after):
        return self._take(SUBLAYER_UNITS[s][1], after)

    def grads(self, s, mats):
        g3 = [g.reshape(N_DEV, g.shape[0] // N_DEV, g.shape[1]) for g in mats]
        r1 = _swap_with_sibling(g3, f"rs_sibling_{s}")
        ps = [_add_sibling(g, r, self.pos, f"rs_add_sibling_{s}_{i}") for i, (g, r) in enumerate(zip(g3, r1))]
        thru, lands, send, recv, token = _scatter_start(ps, f"rs_start_{s}")
        self.pending[s] = (thru, lands, send, recv)
        return (token,)

    def finish(self, after):
        out = {}
        for s in sorted(self.pending, reverse=True):
            ps, lands = _scatter_wait(*self.pending[s], after, f"rs_wait_{s}")
            for i, p, r2 in zip(SUBLAYER_SHARDS[s], ps, lands):
                out[i] = _add_chips(p, r2, self.pos, f"rs_add_chips_{i}")
        return out


def kernel(x, norm_g, ffn_w_in, ffn_w_out, conv_w_pw1, conv_b_pw1, conv_w_dw, conv_b_dw, conv_norm_g, conv_w_pw2, conv_b_pw2, attn_w_qkv, attn_q_norm, attn_k_norm, attn_w_o, loss_target, m_norm_g, m_ffn_w_in, m_ffn_w_out, m_conv_w_pw1, m_conv_b_pw1, m_conv_w_dw, m_conv_b_dw, m_conv_norm_g, m_conv_w_pw2, m_conv_b_pw2, m_attn_w_qkv, m_attn_q_norm, m_attn_k_norm, m_attn_w_o, v_norm_g, v_ffn_w_in, v_ffn_w_out, v_conv_w_pw1, v_conv_b_pw1, v_conv_w_dw, v_conv_b_dw, v_conv_norm_g, v_conv_w_pw2, v_conv_b_pw2, v_attn_w_qkv, v_attn_q_norm, v_attn_k_norm, v_attn_w_o):
    D = x.shape[2]
    xi, yi, ci = _mesh_pos()
    me = 4 * xi + 2 * yi + ci
    pos = jnp.stack([ci, 2 * xi + yi]).astype(I32)
    lf = [(l, f) for l in range(2) for f in range(2)]

    shards = ([ffn_w_in[l, f].T.astype(BF16) for l, f in lf] + [ffn_w_out[l, f].astype(BF16) for l, f in lf]
              + [conv_w_pw1[0].T.astype(BF16), conv_w_pw2[0].astype(BF16),
                 attn_w_qkv[0].T.astype(BF16), attn_w_o[0].astype(BF16)])
    comm = _StepComm(shards, pos)
    w_cols = norm_g.shape[2]
    vec = _all_sum(_pad_rows(jnp.concatenate([_place_cols(norm_g.reshape(6, w_cols), me, D),
                                              _place_cols(conv_w_dw[0], me, D)]), 40), "gather_vectors")
    vecs = {
        "norm_g": vec[0:6], "w_dw": vec[6:6 + CONV_WIDTH],
        "b_pw1": conv_b_pw1.reshape(2, D), "b_dw": conv_b_dw, "cg": conv_norm_g, "b_pw2": conv_b_pw2,
        "q_norm": attn_q_norm[0], "k_norm": attn_k_norm[0],
    }

    loss_local, dx, small, tok = _local_step(x[0], loss_target[0], vecs, comm)

    mats = comm.finish(dx)
    hd3 = 3 * HEAD_DIM
    packed = jnp.concatenate([
        small["norm_g"], small["b_pw1"], small["w_dw"], small["b_dw"], small["cg"], small["b_pw2"],
        jnp.pad(small["q_norm"].reshape(1, hd3), ((0, 0), (0, D - hd3))),
        jnp.pad(small["k_norm"].reshape(1, hd3), ((0, 0), (0, D - hd3))),
        jnp.full((1, D), loss_local, F32)])
    red = _all_sum(_pad_rows(packed, 48), "reduce_vectors")
    loss = red[44, 0]
    col0 = me * w_cols
    grads = {
        "norm_g": lax.dynamic_slice(red[0:6], (0, col0), (6, w_cols)),
        "ffn_w_in": jnp.concatenate([mats[i].T for i in range(4)]),
        "ffn_w_out": jnp.concatenate([mats[i] for i in range(4, 8)]),
        "conv_w_pw1": mats[8].T,
        "conv_b_pw1": red[6:8].reshape(1, 2 * D),
        "conv_w_dw": lax.dynamic_slice(red[8:8 + CONV_WIDTH], (0, col0), (CONV_WIDTH, w_cols)),
        "conv_b_dw": red[39:40], "conv_norm_g": red[40:41], "conv_w_pw2": mats[9], "conv_b_pw2": red[41:42],
        "attn_w_qkv": mats[10].T,
        "attn_q_norm": red[42, :hd3].reshape(3, HEAD_DIM), "attn_k_norm": red[43, :hd3].reshape(3, HEAD_DIM),
        "attn_w_o": mats[11],
    }

    names = ["norm_g", "ffn_w_in", "ffn_w_out", "conv_w_pw1", "conv_b_pw1", "conv_w_dw", "conv_b_dw",
             "conv_norm_g", "conv_w_pw2", "conv_b_pw2", "attn_w_qkv", "attn_q_norm", "attn_k_norm", "attn_w_o"]
    ws = [norm_g, ffn_w_in, ffn_w_out, conv_w_pw1, conv_b_pw1, conv_w_dw, conv_b_dw, conv_norm_g, conv_w_pw2,
          conv_b_pw2, attn_w_qkv, attn_q_norm, attn_k_norm, attn_w_o]
    ms = [m_norm_g, m_ffn_w_in, m_ffn_w_out, m_conv_w_pw1, m_conv_b_pw1, m_conv_w_dw, m_conv_b_dw, m_conv_norm_g,
          m_conv_w_pw2, m_conv_b_pw2, m_attn_w_qkv, m_attn_q_norm, m_attn_k_norm, m_attn_w_o]
    vs = [v_norm_g, v_ffn_w_in, v_ffn_w_out, v_conv_w_pw1, v_conv_b_pw1, v_conv_w_dw, v_conv_b_dw, v_conv_norm_g,
          v_conv_w_pw2, v_conv_b_pw2, v_attn_w_qkv, v_attn_q_norm, v_attn_k_norm, v_attn_w_o]
    g_out, d_out, m_out, v_out = [], [], [], []
    for name, w, m, v in zip(names, ws, ms, vs):
        shape2 = (-1, w.shape[-1]) if w.size > SMALL_PARAM else w.shape
        g2 = grads[name].reshape(shape2)
        delta, new_m, new_v = _adamw(w.reshape(shape2), g2, m.reshape(shape2), v.reshape(shape2), f"adamw_{name}")
        g_out.append(g2.reshape(w.shape))
        d_out.append(delta.reshape(w.shape))
        m_out.append(new_m.reshape(w.shape))
        v_out.append(new_v.reshape(w.shape))
    return (loss, dx.reshape(x.shape), *g_out, *d_out, *m_out, *v_out)
```

```python
import jax
import jax.numpy as jnp
from jax import lax
from jax.experimental import pallas as pl
from jax.experimental.pallas import tpu as pltpu

F32 = jnp.float32
BF16 = jnp.bfloat16
I32 = jnp.int32

NORM_EPS = 1e-6
LANES = 128
SUBLANES = 8
MXU_WIDTH = 256
HEAD_DIM = 128
ATTN_BLOCK = 128
ATTN_DILATIONS = (1, 4, 16)
CONV_WIDTH = 31
CONV_HALO = 32
CONV_CHUNK = 16
WHOLE_K_MAX = 6144
MM_TN_TOKENS = 2048
STREAM_TILE = 512
ADAM_LR, ADAM_B1, ADAM_B2, ADAM_EPS, ADAM_WD, ADAM_STEP = 0.001, 0.9, 0.999, 1e-08, 0.01, 10
N_DEV = 8
SMALL_PARAM = 8192
NEG_BIG = -1e30
V7X_VMEM_BYTES = 64 * 1024 * 1024
VMEM_LIMIT = V7X_VMEM_BYTES - 8 * 1024 * 1024
MESH = pl.DeviceIdType.MESH
ANY = pl.BlockSpec(memory_space=pl.ANY)


def _params(*sem):
    return pltpu.CompilerParams(dimension_semantics=sem or None, vmem_limit_bytes=VMEM_LIMIT)


def _dot_nn(a, b):
    return lax.dot_general(a, b, (((1,), (0,)), ((), ())), preferred_element_type=F32)


def _dot_nt(a, b):
    return lax.dot_general(a, b, (((1,), (1,)), ((), ())), preferred_element_type=F32)


def _dot_tn(a, b):
    return lax.dot_general(a, b, (((0,), (0,)), ((), ())), preferred_element_type=F32)


def _sigmoid(x):
    return 1.0 / (1.0 + jnp.exp(-x))


def _tile(n, pref):
    if n <= pref:
        return n
    for t in range(pref - pref % 128, 0, -128):
        if n % t == 0:
            return t
    return n


def _to_lane_chunks(dst, val):
    for c in range(dst.shape[0]):
        dst[c] = val[:, c * LANES:(c + 1) * LANES]


def _rows8(v):
    tm, c = v.shape
    return v.reshape(tm // 8, 8, c).sum(axis=0)


def _rmsnorm(x, g, name, after=()):
    T, D = x.shape
    tm = min(T, 512)

    def body(x_ref, g_ref, *rest):
        h_ref = rest[-1]
        xv = x_ref[...]
        r = lax.rsqrt(jnp.mean(xv * xv, axis=-1, keepdims=True) + NORM_EPS)
        h_ref[...] = (xv * r * g_ref[...]).astype(BF16)

    return pl.pallas_call(
        body, name=name, grid=(T // tm,),
        in_specs=[pl.BlockSpec((tm, D), lambda i: (i, 0)), pl.BlockSpec((1, D), lambda i: (0, 0))] + [ANY] * len(after),
        out_specs=pl.BlockSpec((tm, D), lambda i: (i, 0)),
        out_shape=jax.ShapeDtypeStruct((T, D), BF16),
        compiler_params=_params("parallel"),
    )(x, g, *after)


def _rmsnorm_streams(x, g, name):
    T, D = x.shape
    tm = min(T, STREAM_TILE)
    dils = [d for d in ATTN_DILATIONS if d > 1]

    def body(x_ref, g_ref, h_ref, *rest):
        outs, hs = rest[:-1], rest[-1]
        xv = x_ref[...]
        r = lax.rsqrt(jnp.mean(xv * xv, axis=-1, keepdims=True) + NORM_EPS)
        hf = xv * r * g_ref[...]
        h_ref[...] = hf.astype(BF16)
        _to_lane_chunks(hs, hf)
        for d, o_ref in zip(dils, outs):
            for s in range(d):
                for c in range(D // LANES):
                    o_ref[s, :, c * LANES:(c + 1) * LANES] = hs.at[c][pl.ds(s, tm // d, stride=d), :].astype(BF16)

    res = pl.pallas_call(
        body, name=name, grid=(T // tm,),
        in_specs=[pl.BlockSpec((tm, D), lambda i: (i, 0)), pl.BlockSpec((1, D), lambda i: (0, 0))],
        out_specs=[pl.BlockSpec((tm, D), lambda i: (i, 0))]
        + [pl.BlockSpec((d, tm // d, D), lambda i: (0, i, 0)) for d in dils],
        out_shape=[jax.ShapeDtypeStruct((T, D), BF16)] + [jax.ShapeDtypeStruct((d, T // d, D), BF16) for d in dils],
        scratch_shapes=[pltpu.VMEM((D // LANES, tm, LANES), F32)],
        compiler_params=_params("parallel"),
    )(x, g)
    return [res[0]] + [a.reshape(T, D) for a in res[1:]]


def _gated_in(h, wt, bias, act, u_dtype, name, after=()):
    T, D = h.shape
    F = wt.shape[0] // 2
    tm = min(T, 1024)

    def body(*refs):
        z_ref, u_ref = refs[-2:]
        h_ref, w_ref = refs[:2]
        hv = h_ref[...]
        for c0 in range(0, F, MXU_WIDTH):
            c1 = min(c0 + MXU_WIDTH, F)
            z0 = _dot_nt(hv, w_ref[c0:c1, :])
            z1 = _dot_nt(hv, w_ref[F + c0:F + c1, :])
            if bias is not None:
                z0 = z0 + refs[2][0:1, c0:c1]
                z1 = z1 + refs[2][1:2, c0:c1]
            z_ref[0, :, c0:c1] = z0.astype(BF16)
            z_ref[1, :, c0:c1] = z1.astype(BF16)
            if act == "swiglu":
                u = z0 * _sigmoid(z0) * z1
            else:
                u = z0 * _sigmoid(z1)
            u_ref[:, c0:c1] = u.astype(u_dtype)

    in_specs = [pl.BlockSpec((tm, D), lambda m: (m, 0)),
                pl.BlockSpec((2 * F, D), lambda m: (0, 0), pipeline_mode=pl.Buffered(1))]
    args = [h, wt]
    if bias is not None:
        in_specs.append(pl.BlockSpec((2, F), lambda m: (0, 0)))
        args.append(bias)
    in_specs += [ANY] * len(after)
    args += list(after)
    return pl.pallas_call(
        body, name=name, grid=(T // tm,), in_specs=in_specs,
        out_specs=[pl.BlockSpec((2, tm, F), lambda m: (0, m, 0)), pl.BlockSpec((tm, F), lambda m: (m, 0))],
        out_shape=[jax.ShapeDtypeStruct((2, T, F), BF16), jax.ShapeDtypeStruct((T, F), u_dtype)],
        compiler_params=_params("parallel"),
    )(*args)


def _swiglu_bwd(dxo, wout, z3, scale, name, after=()):
    T, D = dxo.shape
    F = wout.shape[0]
    tm = min(T, 512)

    def body(d_ref, w_ref, z_ref, *rest):
        dz_ref = rest[-1]
        dy = (d_ref[...] * scale).astype(BF16)
        for c0 in range(0, F, MXU_WIDTH):
            cs = slice(c0, min(c0 + MXU_WIDTH, F))
            da = _dot_nt(dy, w_ref[cs, :])
            gate = z_ref[0, :, cs].astype(F32)
            up = z_ref[1, :, cs].astype(F32)
            sg = _sigmoid(gate)
            dz_ref[0, :, cs] = (da * up * (sg * (1.0 + gate * (1.0 - sg)))).astype(BF16)
            dz_ref[1, :, cs] = (da * gate * sg).astype(BF16)

    return pl.pallas_call(
        body, name=name, grid=(T // tm,),
        in_specs=[pl.BlockSpec((tm, D), lambda m: (m, 0)),
                  pl.BlockSpec((F, D), lambda m: (0, 0)),
                  pl.BlockSpec((2, tm, F), lambda m: (0, m, 0))] + [ANY] * len(after),
        out_specs=pl.BlockSpec((2, tm, F), lambda m: (0, m, 0)),
        out_shape=jax.ShapeDtypeStruct((2, T, F), BF16),
        compiler_params=_params("parallel"),
    )(dxo, wout, z3, *after)


def _mm_nt(a, wt, out_dtype, name, *, w_row0=0, n_rows=None, streams=1, after=()):
    T, K = a.shape
    N = wt.shape[0] if n_rows is None else n_rows
    L = T // streams
    rows = min(L, STREAM_TILE)
    sp = STREAM_TILE // rows
    mps = L // rows
    assert streams % sp == 0 and w_row0 % N == 0

    def body(a_ref, w_ref, *rest):
        o_ref = rest[-1]
        for s in range(sp):
            a_s = a_ref[s * rows:(s + 1) * rows, :].astype(BF16)
            for c0 in range(0, N, MXU_WIDTH):
                c1 = min(c0 + MXU_WIDTH, N)
                o_ref[:, s * N + c0:s * N + c1] = _dot_nt(a_s, w_ref[c0:c1, :]).astype(out_dtype)

    return pl.pallas_call(
        body, name=name, grid=(T // (rows * sp),),
        in_specs=[pl.BlockSpec((rows * sp, K), lambda m: (m, 0)), pl.BlockSpec((N, K), lambda m: (w_row0 // N, 0))]
        + [ANY] * len(after),
        out_specs=pl.BlockSpec((rows, sp * N), lambda m: (m % mps, m // mps)),
        out_shape=jax.ShapeDtypeStruct((L, streams * N), out_dtype),
        compiler_params=_params("parallel"),
    )(a, wt, *after)


def _mm_tn(a, b, b_scale, name, *, streams=1):
    if a.ndim == 3:
        J, T, F = a.shape
    elif streams > 1:
        J, T, F = 1, a.shape[0] * streams, a.shape[1] // streams
    else:
        (T, F), J = a.shape, 1
    N = b.shape[1]
    tmm = _tile(F, 1408)
    tk = min(T // streams, MM_TN_TOKENS)
    npj = F // tmm
    sp = max(1, MM_TN_TOKENS // tk) if streams > 1 and tk == T // streams else 1
    assert streams % sp == 0
    nk = T // (tk * sp)
    kps = nk // streams if sp == 1 else 1

    def body(*refs):
        a_refs, (b_ref, o_ref, acc_ref) = refs[:sp], refs[sp:]
        k = pl.program_id(1)
        part = None
        for s, a_ref in enumerate(a_refs):
            bv = b_ref[s * tk:(s + 1) * tk, :]
            if b_scale != 1.0:
                bv = bv * b_scale
            dot = _dot_tn(a_ref[...].astype(BF16), bv.astype(BF16))
            part = dot if part is None else part + dot

        @pl.when(k == 0)
        def _():
            acc_ref[...] = part

        @pl.when(k > 0)
        def _():
            acc_ref[...] += part

        @pl.when(k == nk - 1)
        def _():
            o_ref[...] = acc_ref[...].astype(BF16)

    if a.ndim == 3:
        a_specs = [pl.BlockSpec((None, tk, tmm), lambda i, k: (i // npj, k, i % npj))]
    elif sp > 1:
        a_specs = [pl.BlockSpec((tk, tmm), lambda i, k, s=s: (0, (k * sp + s) * npj + i)) for s in range(sp)]
    elif streams > 1:
        a_specs = [pl.BlockSpec((tk, tmm), lambda i, k: (k % kps, (k // kps) * npj + i))]
    else:
        a_specs = [pl.BlockSpec((tk, tmm), lambda i, k: (k, i))]
    return pl.pallas_call(
        body, name=name, grid=(J * npj, nk),
        in_specs=a_specs + [pl.BlockSpec((tk * sp, N), lambda i, k: (k, 0))],
        out_specs=pl.BlockSpec((tmm, N), lambda i, k: (i, 0)),
        out_shape=jax.ShapeDtypeStruct((J * F, N), BF16),
        scratch_shapes=[pltpu.VMEM((tmm, N), F32)],
        compiler_params=_params("parallel", "arbitrary"),
    )(*[a] * sp, b)


def _rows_mm(a, b, mode, name, *, x=None, scale=1.0, bias=None, tgt=None, g=None, dxo=None,
             b_row0=0, streams=1, extras=(), after=()):
    if a.ndim == 3:
        J, T, F = a.shape
    elif streams > 1:
        J, T, F = 1, a.shape[0] * streams, a.shape[1] // streams
    else:
        (T, F), J = a.shape, 1
    D = b.shape[1]
    if streams > 1 and T // streams < STREAM_TILE:
        L = T // streams
        sp = STREAM_TILE // L
        assert mode == "plain" and streams % sp == 0 and b_row0 % F == 0 and not after

        def body_short(a_ref, b_ref, o_ref):
            for s in range(sp):
                o_ref[s * L:(s + 1) * L, :] = _dot_nn(a_ref[:, s * F:(s + 1) * F], b_ref[...])

        return pl.pallas_call(
            body_short, name=name, grid=(streams // sp,),
            in_specs=[pl.BlockSpec((L, sp * F), lambda m: (0, m)), pl.BlockSpec((F, D), lambda m: (b_row0 // F, 0))],
            out_specs=pl.BlockSpec((sp * L, D), lambda m: (m, 0)),
            out_shape=jax.ShapeDtypeStruct((T, D), F32),
            compiler_params=_params("parallel"),
        )(a, b)
    tk = _tile(F, 3072)
    kpj = F // tk
    nk = J * kpj
    tm = min(T // streams, STREAM_TILE)
    whole_k = a.ndim == 3 and kpj == 1 and J > 1 and J * F <= WHOLE_K_MAX
    if whole_k:
        nk = 1
    nm = T // tm
    mps = nm // streams
    assert b_row0 % tk == 0
    kb0 = b_row0 // tk
    n_ex = len(extras)
    n_ab = J if whole_k else 1

    def body(*refs):
        refs = list(refs)
        a_refs, b_refs = refs[:n_ab], refs[n_ab:2 * n_ab]
        rest = refs[2 * n_ab + len(after):]
        if n_ex:
            tmp_ref = rest.pop()
        acc_ref = rest.pop()
        ex_refs, rest = rest[:n_ex], rest[n_ex:]
        m, k = pl.program_id(0), pl.program_id(1)
        part = _dot_nn(a_refs[0][...], b_refs[0][...])
        for a_ref, b_ref in zip(a_refs[1:], b_refs[1:]):
            part = part + _dot_nn(a_ref[...], b_ref[...])

        @pl.when(k == 0)
        def _():
            acc_ref[...] = part

        @pl.when(k > 0)
        def _():
            acc_ref[...] += part

        @pl.when(k == nk - 1)
        def _():
            acc = acc_ref[...]
            for (d, _), e_ref in zip(extras, ex_refs):
                for s in range(d):
                    for c in range(D // LANES):
                        tmp_ref.at[c][pl.ds(s, tm // d, stride=d), :] = e_ref[s, :, c * LANES:(c + 1) * LANES]
                acc = acc + jnp.concatenate([tmp_ref[c] for c in range(D // LANES)], axis=1)
            if mode == "plain":
                (o_ref,) = rest
                o_ref[...] = acc
            elif mode == "res":
                ins = list(rest)
                x_ref = ins.pop(0)
                if bias is not None:
                    acc = acc + ins.pop(0)[...]
                xn = x_ref[...] + scale * acc
                if g is None:
                    (o_ref,) = ins
                else:
                    g_ref, o_ref, h_ref = ins
                    r = lax.rsqrt(jnp.mean(xn * xn, axis=-1, keepdims=True) + NORM_EPS)
                    h_ref[...] = (xn * r * g_ref[...]).astype(BF16)
                o_ref[...] = xn
            elif mode == "loss":
                x_ref, t_ref, o_ref, l_ref = rest
                e = x_ref[...] + scale * acc - t_ref[...]
                o_ref[...] = e / D

                @pl.when(m == 0)
                def _():
                    l_ref[...] = jnp.zeros_like(l_ref)

                l_ref[...] += _rows8(e * e)
            else:
                x_ref, g_ref, d_ref, o_ref, dg_ref = rest
                xv = x_ref[...]
                r = lax.rsqrt(jnp.mean(xv * xv, axis=-1, keepdims=True) + NORM_EPS)
                xh = xv * r
                dxh = acc * g_ref[...]
                o_ref[...] = d_ref[...] + r * (dxh - xh * jnp.mean(dxh * xh, axis=-1, keepdims=True))

                @pl.when(m == 0)
                def _():
                    dg_ref[...] = jnp.zeros_like(dg_ref)

                dg_ref[...] += _rows8(acc * xh)

    if whole_k:
        a_specs = [pl.BlockSpec((None, tm, F), lambda m, k, j=j: (j, m, 0)) for j in range(J)]
        b_specs = [pl.BlockSpec((F, D), lambda m, k, j=j: (kb0 + j, 0)) for j in range(J)]
    else:
        if a.ndim == 3:
            a_specs = [pl.BlockSpec((None, tm, tk), lambda m, k: (k // kpj, m, k % kpj))]
        elif streams > 1:
            a_specs = [pl.BlockSpec((tm, tk), lambda m, k: (m % mps, (m // mps) * kpj + k))]
        else:
            a_specs = [pl.BlockSpec((tm, tk), lambda m, k: (m, k))]
        b_specs = [pl.BlockSpec((tk, D), lambda m, k: (kb0 + k, 0))]
    row = pl.BlockSpec((tm, D), lambda m, k: (m, 0))
    vec = pl.BlockSpec((1, D), lambda m, k: (0, 0))
    part8 = pl.BlockSpec((8, D), lambda m, k: (0, 0))
    in_specs = a_specs + b_specs + [ANY] * len(after)
    args = [a] * n_ab + [b] * n_ab + list(after)
    for d, e in extras:
        in_specs.append(pl.BlockSpec((d, tm // d, D), lambda m, k: (0, m, 0)))
        args.append(e.reshape(d, T // d, D))
    full = jax.ShapeDtypeStruct((T, D), F32)
    small = jax.ShapeDtypeStruct((8, D), F32)
    if mode == "plain":
        out_specs, out_shape = row, full
    elif mode == "res":
        in_specs.append(row)
        args.append(x)
        if bias is not None:
            in_specs.append(vec)
            args.append(bias)
        if g is None:
            out_specs, out_shape = row, full
        else:
            in_specs.append(vec)
            args.append(g)
            out_specs, out_shape = [row, row], [full, jax.ShapeDtypeStruct((T, D), BF16)]
    elif mode == "loss":
        in_specs += [row, row]
        args += [x, tgt]
        out_specs, out_shape = [row, part8], [full, small]
    else:
        in_specs += [row, vec, row]
        args += [x, g, dxo]
        out_specs, out_shape = [row, part8], [full, small]
    return pl.pallas_call(
        body, name=name, grid=(nm, nk), in_specs=in_specs, out_specs=out_specs, out_shape=out_shape,
        scratch_shapes=[pltpu.VMEM((tm, D), F32)] + ([pltpu.VMEM((D // LANES, tm, LANES), F32)] if n_ex else []),
        compiler_params=_params("arbitrary", "arbitrary"),
    )(*args)


def _sublane_phases(sh):
    rows = sh.shape[1] - SUBLANES
    for p in range(1, SUBLANES):
        sh[p, pl.ds(0, rows), :] = sh[0, pl.ds(p, rows), :]


def _shifted(sh, offset, j, rows=SUBLANES):
    start = pl.multiple_of(j * rows + (offset - offset % SUBLANES), SUBLANES)
    return sh[offset % SUBLANES, pl.ds(start, rows), :]


def _conv_taps(wb, sh, offsets, j, init):
    groups = CONV_CHUNK // SUBLANES
    accs = [init[g * SUBLANES:(g + 1) * SUBLANES] for g in range(groups)]
    for k, off in enumerate(offsets):
        w = wb[k]
        start = pl.multiple_of(j * CONV_CHUNK + (off - off % SUBLANES), SUBLANES)
        for g in range(groups):
            accs[g] = accs[g] + w * sh[off % SUBLANES, pl.ds(start + g * SUBLANES, SUBLANES), :]
    return jnp.concatenate(accs, axis=0)


def _broadcast_rows(dst, src):
    for r in range(dst.shape[0]):
        dst[r] = jnp.zeros(dst.shape[1:], F32) + src[r:r + 1, :]


def _dwconv_fwd(u, w, b_dw, cg, name):
    T, C = u.shape
    tm = min(T, 256)
    hb = tm // CONV_HALO
    pad = CONV_HALO - (CONV_WIDTH - 1)

    def body(u_ref, halo_ref, w_ref, b_ref, g_ref, v_ref, s_ref, sh, wb):
        i = pl.program_id(0)
        sh[0, pl.ds(0, CONV_HALO), :] = jnp.where(i > 0, halo_ref[...], 0.0)
        sh[0, pl.ds(CONV_HALO, tm), :] = u_ref[...]
        _sublane_phases(sh)
        _broadcast_rows(wb, w_ref)
        bias = jnp.zeros((CONV_CHUNK, C), F32) + b_ref[...]

        def chunk(j, carry):
            acc = _conv_taps(wb, sh, [pad + k for k in range(CONV_WIDTH)], j, bias)
            v_ref[pl.ds(pl.multiple_of(j * CONV_CHUNK, CONV_CHUNK), CONV_CHUNK), :] = acc
            return carry

        lax.fori_loop(0, tm // CONV_CHUNK, chunk, 0)
        acc = v_ref[...]
        r = lax.rsqrt(jnp.mean(acc * acc, axis=-1, keepdims=True) + NORM_EPS)
        n = acc * r * g_ref[...]
        s_ref[...] = (n * _sigmoid(n)).astype(BF16)

    row = pl.BlockSpec((tm, C), lambda i: (i, 0))
    vec = pl.BlockSpec((1, C), lambda i: (0, 0))
    return pl.pallas_call(
        body, name=name, grid=(T // tm,),
        in_specs=[row, pl.BlockSpec((CONV_HALO, C), lambda i: (jnp.maximum(i * hb - 1, 0), 0)),
                  pl.BlockSpec((CONV_WIDTH, C), lambda i: (0, 0)), vec, vec],
        out_specs=[row, row],
        out_shape=[jax.ShapeDtypeStruct((T, C), F32), jax.ShapeDtypeStruct((T, C), BF16)],
        scratch_shapes=[pltpu.VMEM((SUBLANES, CONV_HALO + tm, C), F32), pltpu.VMEM((CONV_WIDTH, SUBLANES, C), F32)],
        compiler_params=_params("parallel"),
    )(u, u, w, b_dw, cg)


def _conv_post_bwd(ds, v, cg, dxo, name):
    T, C = v.shape
    tm = min(T, 512)

    def body(ds_ref, v_ref, g_ref, d_ref, dv_ref, dcg_ref, dbdw_ref, dbpw2_ref):
        @pl.when(pl.program_id(0) == 0)
        def _():
            dcg_ref[...] = jnp.zeros_like(dcg_ref)
            dbdw_ref[...] = jnp.zeros_like(dbdw_ref)
            dbpw2_ref[...] = jnp.zeros_like(dbpw2_ref)

        vv = v_ref[...]
        r = lax.rsqrt(jnp.mean(vv * vv, axis=-1, keepdims=True) + NORM_EPS)
        vh = vv * r
        n = vh * g_ref[...]
        sg = _sigmoid(n)
        dn = ds_ref[...] * (sg * (1.0 + n * (1.0 - sg)))
        dvh = dn * g_ref[...]
        dv = r * (dvh - vh * jnp.mean(dvh * vh, axis=-1, keepdims=True))
        dv_ref[...] = dv
        dcg_ref[...] += _rows8(dn * vh)
        dbdw_ref[...] += _rows8(dv)
        dbpw2_ref[...] += _rows8(d_ref[...])

    row = pl.BlockSpec((tm, C), lambda i: (i, 0))
    part8 = pl.BlockSpec((8, C), lambda i: (0, 0))
    small = jax.ShapeDtypeStruct((8, C), F32)
    return pl.pallas_call(
        body, name=name, grid=(T // tm,),
        in_specs=[row, row, pl.BlockSpec((1, C), lambda i: (0, 0)), row],
        out_specs=[row, part8, part8, part8],
        out_shape=[jax.ShapeDtypeStruct((T, C), F32), small, small, small],
        compiler_params=_params("arbitrary"),
    )(ds, v, cg, dxo)


def _dwconv_bwd(dv, u, w, z3, name):
    T, C = u.shape
    tm = min(T, 256)
    hb = tm // CONV_HALO
    nt = T // tm
    pad = CONV_HALO - (CONV_WIDTH - 1)

    taps_at_once = 4
    chunks_at_once = 8

    def body(dv_ref, dvn_ref, u_ref, up_ref, w_ref, z_ref, dz_ref, dw_ref, db_ref, dvsh, ush, wb, du_ref, dwacc):
        i = pl.program_id(0)

        @pl.when(i == 0)
        def _():
            dwacc[...] = jnp.zeros_like(dwacc)
            db_ref[...] = jnp.zeros_like(db_ref)

        dvsh[0, pl.ds(0, tm), :] = dv_ref[...]
        dvsh[0, pl.ds(tm, CONV_HALO), :] = jnp.where(i < nt - 1, dvn_ref[...], 0.0)
        ush[0, pl.ds(0, CONV_HALO), :] = jnp.where(i > 0, up_ref[...], 0.0)
        ush[0, pl.ds(CONV_HALO, tm), :] = u_ref[...]
        _sublane_phases(dvsh)
        _sublane_phases(ush)
        _broadcast_rows(wb, w_ref)

        def du_chunk(j, carry):
            acc = _conv_taps(wb, dvsh, [CONV_WIDTH - 1 - k for k in range(CONV_WIDTH)], j,
                             jnp.zeros((CONV_CHUNK, C), F32))
            du_ref[pl.ds(pl.multiple_of(j * CONV_CHUNK, CONV_CHUNK), CONV_CHUNK), :] = acc
            return carry

        lax.fori_loop(0, tm // CONV_CHUNK, du_chunk, 0)

        for k0 in range(0, CONV_WIDTH, taps_at_once):
            taps = range(k0, min(k0 + taps_at_once, CONV_WIDTH))

            def dw_chunks(jj, carry, taps=taps):
                accs = [jnp.zeros((SUBLANES, C), F32) for _ in taps]
                for u_ in range(chunks_at_once):
                    j = jj * chunks_at_once + u_
                    dvc = dv_ref[pl.ds(pl.multiple_of(j * SUBLANES, SUBLANES), SUBLANES), :]
                    accs = [acc + dvc * _shifted(ush, pad + k, j) for acc, k in zip(accs, taps)]
                for k, acc in zip(taps, accs):
                    dwacc[k] += acc
                return carry

            lax.fori_loop(0, tm // (SUBLANES * chunks_at_once), dw_chunks, 0)

        du = du_ref[...]
        a = z_ref[0].astype(F32)
        gate = z_ref[1].astype(F32)
        sg = _sigmoid(gate)
        da = du * sg
        dgate = du * a * sg * (1.0 - sg)
        dz_ref[0] = da.astype(BF16)
        dz_ref[1] = dgate.astype(BF16)
        db_ref[0:8, :] += _rows8(da)
        db_ref[8:16, :] += _rows8(dgate)

        @pl.when(i == nt - 1)
        def _():
            dw_ref[...] = jnp.zeros_like(dw_ref)
            for k in range(CONV_WIDTH):
                dw_ref[k:k + 1, :] = jnp.sum(dwacc[k], axis=0, keepdims=True)

    row = pl.BlockSpec((tm, C), lambda i: (i, 0))
    last_halo = T // CONV_HALO - 1
    return pl.pallas_call(
        body, name=name, grid=(nt,),
        in_specs=[row, pl.BlockSpec((CONV_HALO, C), lambda i: (jnp.minimum((i + 1) * hb, last_halo), 0)),
                  row, pl.BlockSpec((CONV_HALO, C), lambda i: (jnp.maximum(i * hb - 1, 0), 0)),
                  pl.BlockSpec((CONV_WIDTH, C), lambda i: (0, 0)),
                  pl.BlockSpec((2, tm, C), lambda i: (0, i, 0))],
        out_specs=[pl.BlockSpec((2, tm, C), lambda i: (0, i, 0)),
                   pl.BlockSpec((CONV_HALO, C), lambda i: (0, 0)),
                   pl.BlockSpec((16, C), lambda i: (0, 0))],
        out_shape=[jax.ShapeDtypeStruct((2, T, C), BF16), jax.ShapeDtypeStruct((CONV_HALO, C), F32),
                   jax.ShapeDtypeStruct((16, C), F32)],
        scratch_shapes=[pltpu.VMEM((SUBLANES, tm + CONV_HALO, C), F32), pltpu.VMEM((SUBLANES, CONV_HALO + tm, C), F32),
                        pltpu.VMEM((CONV_WIDTH, SUBLANES, C), F32), pltpu.VMEM((tm, C), F32),
                        pltpu.VMEM((CONV_WIDTH, SUBLANES, C), F32)],
        compiler_params=_params("arbitrary"),
    )(dv, dv, u, u, w, z3)


def _head_norm(raw, gain):
    xv = raw.astype(F32)
    r = lax.rsqrt(jnp.mean(xv * xv, axis=-1, keepdims=True) + NORM_EPS)
    xh = xv * r
    return (xh * gain).astype(BF16), xh, r


def _band_mask(n):
    row = lax.broadcasted_iota(I32, (ATTN_BLOCK, 2 * ATTN_BLOCK), 0)
    col = lax.broadcasted_iota(I32, (ATTN_BLOCK, 2 * ATTN_BLOCK), 1)
    return (col >= row) & (col <= row + ATTN_BLOCK) & ((col >= ATTN_BLOCK) | (n > 0))


def _carry_previous(n, bufs):
    @pl.when(n == 0)
    def _():
        for b in bufs:
            b[0:ATTN_BLOCK, :] = jnp.zeros((ATTN_BLOCK, b.shape[1]), b.dtype)

    @pl.when(n > 0)
    def _():
        for b in bufs:
            b[0:ATTN_BLOCK, :] = b[ATTN_BLOCK:, :]


def _attn_fwd(qkv, qg, kg, g, name):
    d = ATTN_DILATIONS[g]
    L = qkv.shape[0]
    HD = qkv.shape[1] // (3 * d)
    H = HD // HEAD_DIM
    nb = L // ATTN_BLOCK
    scale = HEAD_DIM ** -0.5

    def body(q_ref, kc_ref, vc_ref, qg_ref, kg_ref, o_ref, lse_ref, kk, vv):
        n = pl.program_id(1)
        _carry_previous(n, [kk, vv])
        mask = _band_mask(n)
        lane = lax.broadcasted_iota(I32, (ATTN_BLOCK, 128), 1)
        gq = qg_ref[g:g + 1, :]
        gk = kg_ref[g:g + 1, :]
        vv[ATTN_BLOCK:, :] = vc_ref[...]
        lse_tile = jnp.zeros((ATTN_BLOCK, 128), F32)
        for h in range(H):
            sl = slice(h * HEAD_DIM, (h + 1) * HEAD_DIM)
            q = _head_norm(q_ref[:, sl], gq)[0]
            kk[ATTN_BLOCK:, sl] = _head_norm(kc_ref[:, sl], gk)[0]
            s = jnp.where(mask, _dot_nt(q, kk[:, sl]) * scale, NEG_BIG)
            m = jnp.max(s, axis=-1, keepdims=True)
            p = jnp.exp(s - m)
            l = jnp.sum(p, axis=-1, keepdims=True)
            o_ref[:, sl] = _dot_nn(p.astype(BF16), vv[:, sl]) / l
            lse_tile = jnp.where(lane == h, m + jnp.log(l), lse_tile)
        lse_ref[...] = lse_tile

    def blk(which):
        return pl.BlockSpec((ATTN_BLOCK, HD), lambda r, n: (n, r * 3 + which))

    gains = pl.BlockSpec((3, HEAD_DIM), lambda r, n: (0, 0))
    return pl.pallas_call(
        body, name=name, grid=(d, nb),
        in_specs=[blk(0), blk(1), blk(2), gains, gains],
        out_specs=[pl.BlockSpec((ATTN_BLOCK, HD), lambda r, n: (n, r)),
                   pl.BlockSpec((ATTN_BLOCK, 128), lambda r, n: (n, r))],
        out_shape=[jax.ShapeDtypeStruct((L, d * HD), F32), jax.ShapeDtypeStruct((L, d * 128), F32)],
        scratch_shapes=[pltpu.VMEM((2 * ATTN_BLOCK, HD), BF16), pltpu.VMEM((2 * ATTN_BLOCK, HD), BF16)],
        compiler_params=_params("arbitrary", "arbitrary"),
    )(qkv, qkv, qkv, qg, kg)


def _attn_merge(os, lses, name):
    T = os[0].shape[0]
    HD = os[0].shape[1]
    H = HD // HEAD_DIM
    tm = min(T, STREAM_TILE)
    dils = ATTN_DILATIONS[1:]

    def body(o0, o1, o2, l0, l1, l2, ob_ref, of_ref, lse_ref, n1, n2, m1, m2):
        for d, src, dst, w in ((dils[0], o1, n1, HD), (dils[1], o2, n2, HD), (dils[0], l1, m1, 128), (dils[1], l2, m2, 128)):
            for s in range(d):
                for c in range(w // LANES):
                    dst.at[c][pl.ds(s, tm // d, stride=d), :] = src[:, s * w + c * LANES:s * w + (c + 1) * LANES]
        ls = [l0[...], m1[0], m2[0]]
        m = jnp.maximum(jnp.maximum(ls[0], ls[1]), ls[2])
        tot = m + jnp.log(jnp.exp(ls[0] - m) + jnp.exp(ls[1] - m) + jnp.exp(ls[2] - m))
        lse_ref[...] = tot
        ws = [jnp.exp(l - tot) for l in ls]
        for h in range(H):
            sl = slice(h * HEAD_DIM, (h + 1) * HEAD_DIM)
            out = (ws[0][:, h:h + 1] * o0[:, sl] + ws[1][:, h:h + 1] * n1[h]) + ws[2][:, h:h + 1] * n2[h]
            of_ref[:, sl] = out
            ob_ref[:, sl] = out.astype(BF16)

    def wide(d, w):
        return pl.BlockSpec((tm // d, d * w), lambda i: (i, 0))

    row = pl.BlockSpec((tm, HD), lambda i: (i, 0))
    nar = pl.BlockSpec((tm, 128), lambda i: (i, 0))
    return pl.pallas_call(
        body, name=name, grid=(T // tm,),
        in_specs=[row, wide(dils[0], HD), wide(dils[1], HD), nar, wide(dils[0], 128), wide(dils[1], 128)],
        out_specs=[row, row, nar],
        out_shape=[jax.ShapeDtypeStruct((T, HD), BF16), jax.ShapeDtypeStruct((T, HD), F32),
                   jax.ShapeDtypeStruct((T, 128), F32)],
        scratch_shapes=[pltpu.VMEM((H, tm, HEAD_DIM), F32), pltpu.VMEM((H, tm, HEAD_DIM), F32),
                        pltpu.VMEM((1, tm, 128), F32), pltpu.VMEM((1, tm, 128), F32)],
        compiler_params=_params("parallel"),
    )(*os, *lses)


def _attn_delta(out, dout, lse, name):
    T, HD = out.shape
    H = HD // HEAD_DIM
    tm = min(T, STREAM_TILE)
    dils = ATTN_DILATIONS[1:]

    def body(o_ref, d_ref, l_ref, dl_ref, *rest):
        dch = rest[-1]
        lane = lax.broadcasted_iota(I32, (tm, 128), 1)
        tile = jnp.zeros((tm, 128), F32)
        for h in range(H):
            sl = slice(h * HEAD_DIM, (h + 1) * HEAD_DIM)
            dch[h] = d_ref[:, sl]
            tile = jnp.where(lane == h, jnp.sum(o_ref[:, sl] * d_ref[:, sl], axis=-1, keepdims=True), tile)
        dl_ref[...] = tile
        for i, d in enumerate(dils):
            dw_ref, lw_ref, dlw_ref = rest[3 * i:3 * i + 3]
            for s in range(d):
                rows = pl.ds(s, tm // d, stride=d)
                for h in range(H):
                    dw_ref[:, s * HD + h * HEAD_DIM:s * HD + (h + 1) * HEAD_DIM] = dch.at[h][rows, :].astype(BF16)
                lw_ref[:, s * 128:(s + 1) * 128] = l_ref[rows, :]
                dlw_ref[:, s * 128:(s + 1) * 128] = dl_ref[rows, :]

    row = pl.BlockSpec((tm, HD), lambda i: (i, 0))
    nar = pl.BlockSpec((tm, 128), lambda i: (i, 0))
    out_specs, out_shape = [nar], [jax.ShapeDtypeStruct((T, 128), F32)]
    for d in dils:
        out_specs += [pl.BlockSpec((tm // d, d * w), lambda i: (i, 0)) for w in (HD, 128, 128)]
        out_shape += [jax.ShapeDtypeStruct((T // d, d * HD), BF16), jax.ShapeDtypeStruct((T // d, d * 128), F32),
                      jax.ShapeDtypeStruct((T // d, d * 128), F32)]
    return pl.pallas_call(
        body, name=name, grid=(T // tm,), in_specs=[row, row, nar], out_specs=out_specs, out_shape=out_shape,
        scratch_shapes=[pltpu.VMEM((H, tm, HEAD_DIM), F32)],
        compiler_params=_params("parallel"),
    )(out, dout, lse)


def _attn_bwd(qkv, dout, lse, delta, qg, kg, g, name):
    d = ATTN_DILATIONS[g]
    L = qkv.shape[0]
    HD = qkv.shape[1] // (3 * d)
    H = HD // HEAD_DIM
    nb = L // ATTN_BLOCK
    total = d * nb
    scale = HEAD_DIM ** -0.5

    def body(*refs):
        (q_ref, kc_ref, vc_ref, do_ref, lse_ref, dl_ref, qg_ref, kg_ref,
         out_ref, gg_ref, dq_a, dk_a, dv_a, dq_b, dk_b, dv_b, kk, vv, khh, rkk) = refs
        t = pl.program_id(0)
        n = t % nb
        gq = qg_ref[g:g + 1, :]
        gk = kg_ref[g:g + 1, :]
        _carry_previous(t, [kk, vv, khh, rkk])

        @pl.when(t == 0)
        def _():
            gg_ref[...] = jnp.zeros_like(gg_ref)
            dk_a[...] = jnp.zeros_like(dk_a)
            dv_a[...] = jnp.zeros_like(dv_a)

        @pl.when(t < total)
        def _():
            mask = _band_mask(n)
            vv[ATTN_BLOCK:, :] = vc_ref[...]
            gq_sum = jnp.zeros((1, HEAD_DIM), F32)
            for h in range(H):
                sl = slice(h * HEAD_DIM, (h + 1) * HEAD_DIM)
                qn, qh, rq = _head_norm(q_ref[:, sl], gq)
                kcn, kch, rkc = _head_norm(kc_ref[:, sl], gk)
                kk[ATTN_BLOCK:, sl] = kcn
                khh[ATTN_BLOCK:, sl] = kch
                rkk[ATTN_BLOCK:, sl] = jnp.broadcast_to(rkc, (ATTN_BLOCK, HEAD_DIM))
                kn = kk[:, sl]
                p = jnp.where(mask, jnp.exp(_dot_nt(qn, kn) * scale - lse_ref[:, h:h + 1]), 0.0)
                do = do_ref[:, sl].astype(BF16)
                ds = (p * (_dot_nt(do, vv[:, sl]) - dl_ref[:, h:h + 1]) * scale).astype(BF16)
                dqn = _dot_nn(ds, kn)
                dqh = dqn * gq
                dq_b[:, sl] = rq * (dqh - qh * jnp.mean(dqh * qh, axis=-1, keepdims=True))
                gq_sum = gq_sum + jnp.sum(dqn * qh, axis=0, keepdims=True)
                dkn = _dot_tn(ds, qn)
                dvn = _dot_tn(p.astype(BF16), do)
                dk_a[:, sl] += dkn[0:ATTN_BLOCK]
                dv_a[:, sl] += dvn[0:ATTN_BLOCK]
                dk_b[:, sl] = dkn[ATTN_BLOCK:]
                dv_b[:, sl] = dvn[ATTN_BLOCK:]
            gg_ref[0:1, :] += gq_sum

        @pl.when(t > 0)
        def _():
            out_ref[:, 0:HD] = dq_a[...].astype(BF16)
            gk_sum = jnp.zeros((1, HEAD_DIM), F32)
            for h in range(H):
                sl = slice(h * HEAD_DIM, (h + 1) * HEAD_DIM)
                kh = khh[0:ATTN_BLOCK, sl]
                rk = rkk[0:ATTN_BLOCK, sl]
                dkn = dk_a[:, sl]
                dkh = dkn * gk
                dk = rk * (dkh - kh * jnp.mean(dkh * kh, axis=-1, keepdims=True))
                out_ref[:, HD + h * HEAD_DIM:HD + (h + 1) * HEAD_DIM] = dk.astype(BF16)
                gk_sum = gk_sum + jnp.sum(dkn * kh, axis=0, keepdims=True)
            out_ref[:, 2 * HD:3 * HD] = dv_a[...].astype(BF16)
            gg_ref[1:2, :] += gk_sum

        @pl.when(t < total)
        def _():
            dq_a[...] = dq_b[...]
            dk_a[...] = dk_b[...]
            dv_a[...] = dv_b[...]

    def at(t):
        t = jnp.minimum(t, total - 1)
        return t % nb, t // nb

    def blk(which):
        return pl.BlockSpec((ATTN_BLOCK, HD), lambda t: (at(t)[0], at(t)[1] * 3 + which))

    gains = pl.BlockSpec((3, HEAD_DIM), lambda t: (0, 0))
    nar = pl.BlockSpec((ATTN_BLOCK, 128), lambda t: at(t))
    acc = pltpu.VMEM((ATTN_BLOCK, HD), F32)
    pair16 = pltpu.VMEM((2 * ATTN_BLOCK, HD), BF16)
    pair32 = pltpu.VMEM((2 * ATTN_BLOCK, HD), F32)
    return pl.pallas_call(
        body, name=name, grid=(total + 1,),
        in_specs=[blk(0), blk(1), blk(2), pl.BlockSpec((ATTN_BLOCK, HD), lambda t: at(t)), nar, nar, gains, gains],
        out_specs=[pl.BlockSpec((ATTN_BLOCK, 3 * HD), lambda t: at(jnp.maximum(t - 1, 0))),
                   pl.BlockSpec((8, HEAD_DIM), lambda t: (0, 0))],
        out_shape=[jax.ShapeDtypeStruct((L, d * 3 * HD), BF16), jax.ShapeDtypeStruct((8, HEAD_DIM), F32)],
        scratch_shapes=[acc, acc, acc, acc, acc, acc, pair16, pair16, pair32, pair32],
        compiler_params=_params("arbitrary"),
    )(qkv, qkv, qkv, dout, lse, delta, qg, kg)


def _adamw(w, grad, m, v, name):
    if w.ndim == 2:
        R, C = w.shape
        tr = _row_tile(R, 512)
        blk, steps = pl.BlockSpec((tr, C), lambda i: (i, 0)), R // tr
    else:
        zeros = (0,) * w.ndim
        blk, steps = pl.BlockSpec(w.shape, lambda i: zeros), 1

    def body(w_ref, g_ref, m_ref, v_ref, d_ref, nm_ref, nv_ref):
        gv = g_ref[...]
        nm = ADAM_B1 * m_ref[...] + (1.0 - ADAM_B1) * gv
        nv = ADAM_B2 * v_ref[...] + (1.0 - ADAM_B2) * (gv * gv)
        m_hat = nm / (1.0 - ADAM_B1 ** ADAM_STEP)
        v_hat = nv / (1.0 - ADAM_B2 ** ADAM_STEP)
        d_ref[...] = -ADAM_LR * (m_hat / (jnp.sqrt(v_hat) + ADAM_EPS) + ADAM_WD * w_ref[...])
        nm_ref[...] = nm
        nv_ref[...] = nv

    shp = jax.ShapeDtypeStruct(w.shape, F32)
    return pl.pallas_call(
        body, name=name, grid=(steps,), in_specs=[blk] * 4, out_specs=[blk] * 3, out_shape=[shp] * 3,
        compiler_params=_params("parallel"),
    )(w, grad, m, v)


def _ffn_fwd(x, h, comm, s, tag, tgt=None, next_g=None):
    (win_t, *rest), tok = comm.weights(s, h)
    z3, a = _gated_in(h, win_t, None, "swiglu", BF16, f"ffn_in_{tag}", after=tok)
    tok = ()
    if not rest:
        rest, tok = comm.second(s, a)
    wout, = rest
    saved = (x, h, z3, a, win_t, wout)
    if tgt is None:
        return _rows_mm(a, wout, "res", f"ffn_out_{tag}", x=x, scale=0.5, g=next_g, after=tok), saved
    return _rows_mm(a, wout, "loss", f"ffn_out_loss_{tag}", x=x, scale=0.5, tgt=tgt), saved


def _ffn_bwd(dxo, saved, g, comm, s, tag, after):
    x, h, z3, a, win_t, wout = saved
    dz3 = _swiglu_bwd(dxo, wout, z3, 0.5, f"ffn_out_bwd_{tag}", after=after)
    d_wout = _mm_tn(a, dxo, 0.5, f"ffn_dwout_{tag}")
    d_win_t = _mm_tn(dz3, h, 1.0, f"ffn_dwin_{tag}")
    tok = comm.grads(s, [d_win_t, d_wout])
    dx, dg8 = _rows_mm(dz3, win_t, "rmsbwd", f"ffn_in_bwd_{tag}", x=x, g=g, dxo=dxo, after=tok)
    return dx, dg8, ()


def _local_step(x, tgt, vecs, comm):
    norm_g = vecs["norm_g"]

    def gvec(i):
        return norm_g[i:i + 1]

    h0 = _rmsnorm(x, gvec(0), "rmsnorm_l0a", after=comm.started)
    (x1, h_c), sv_a0 = _ffn_fwd(x, h0, comm, 0, "l0a", next_g=gvec(1))
    (pw1_t, pw2), tok = comm.weights(1, h_c)
    zc3, u = _gated_in(h_c, pw1_t, vecs["b_pw1"], "glu", F32, "conv_pw1", after=tok)
    v, s = _dwconv_fwd(u, vecs["w_dw"], vecs["b_dw"], vecs["cg"], "conv_dw")
    x2, h_b0 = _rows_mm(s, pw2, "res", "conv_pw2", x=x1, bias=vecs["b_pw2"], g=gvec(2))
    (x3, h_a1), sv_b0 = _ffn_fwd(x2, h_b0, comm, 2, "l0b", next_g=gvec(3))
    x4, sv_a1 = _ffn_fwd(x3, h_a1, comm, 3, "l1a")
    h_s = _rmsnorm_streams(x4, gvec(4), "rmsnorm_attn")
    (qkv_t, wo), tok = comm.weights(4, h_s[0])
    hd3 = qkv_t.shape[0] // 3
    qkvs, os, lses = [], [], []
    for g, d in enumerate(ATTN_DILATIONS):
        qkv = _mm_nt(h_s[g], qkv_t, BF16, f"attn_qkv_g{g}", w_row0=g * hd3, n_rows=hd3, streams=d, after=tok)
        o, lse = _attn_fwd(qkv, vecs["q_norm"], vecs["k_norm"], g, f"attn_fwd_g{g}")
        qkvs.append(qkv)
        os.append(o)
        lses.append(lse)
    out_b, out_f, lse_tot = _attn_merge(os, lses, "attn_merge")
    x5, h_b1 = _rows_mm(out_b, wo, "res", "attn_wo", x=x4, g=gvec(5))
    (dy, loss8), sv_b1 = _ffn_fwd(x5, h_b1, comm, 5, "l1b", tgt=tgt)
    loss = 0.5 * jnp.sum(loss8) / x.shape[1]

    dg = [None] * 6
    dx, dg[5], tok = _ffn_bwd(dy, sv_b1, gvec(5), comm, 5, "l1b", ())
    dout = _mm_nt(dx, wo, F32, "attn_wo_bwd", after=tok)
    d_wo = _mm_tn(out_b, dx, 1.0, "attn_dwo")
    delta, *wide = _attn_delta(out_f, dout, lse_tot, "attn_delta")
    per_group = [(dout, lse_tot, delta), tuple(wide[0:3]), tuple(wide[3:6])]
    ggs, d_qkv_t, dhs = [], [], []
    for g, d in enumerate(ATTN_DILATIONS):
        dqkv, gg = _attn_bwd(qkvs[g], *per_group[g], vecs["q_norm"], vecs["k_norm"], g, f"attn_bwd_g{g}")
        ggs.append(gg)
        d_qkv_t.append(_mm_tn(dqkv, h_s[g], 1.0, f"attn_dwqkv_g{g}", streams=d))
        if d > 1:
            dhs.append((d, _rows_mm(dqkv, qkv_t, "plain", f"attn_qkv_bwd_g{g}", b_row0=g * hd3, streams=d)))
        else:
            dqkv0 = dqkv
    tok = comm.grads(4, [jnp.concatenate(d_qkv_t), d_wo])
    dx, dg[4] = _rows_mm(dqkv0, qkv_t, "rmsbwd", "attn_qkv_bwd_g0", x=x4, g=gvec(4), dxo=dx, extras=dhs, after=tok)
    dx, dg[3], tok = _ffn_bwd(dx, sv_a1, gvec(3), comm, 3, "l1a", ())
    dx, dg[2], tok = _ffn_bwd(dx, sv_b0, gvec(2), comm, 2, "l0b", tok)
    ds = _mm_nt(dx, pw2, F32, "conv_pw2_bwd", after=tok)
    d_pw2 = _mm_tn(s, dx, 1.0, "conv_dwpw2")
    dv, dcg8, dbdw8, dbpw2_8 = _conv_post_bwd(ds, v, vecs["cg"], dx, "conv_post_bwd")
    dzc3, d_wdw, db1_16 = _dwconv_bwd(dv, u, vecs["w_dw"], zc3, "conv_dw_bwd")
    d_pw1_t = _mm_tn(dzc3, h_c, 1.0, "conv_dwpw1")
    tok = comm.grads(1, [d_pw1_t, d_pw2])
    dx, dg[1] = _rows_mm(dzc3, pw1_t, "rmsbwd", "conv_pw1_bwd", x=x1, g=gvec(1), dxo=dx, after=tok)
    dx, dg[0], tok = _ffn_bwd(dx, sv_a0, gvec(0), comm, 0, "l0a", ())

    D = x.shape[1]
    small = {
        "norm_g": jnp.stack([p.sum(axis=0) for p in dg]),
        "b_pw1": db1_16.reshape(2, 8, D).sum(axis=1),
        "w_dw": d_wdw[:CONV_WIDTH],
        "b_dw": dbdw8.sum(axis=0, keepdims=True),
        "cg": dcg8.sum(axis=0, keepdims=True),
        "b_pw2": dbpw2_8.sum(axis=0, keepdims=True),
        "q_norm": jnp.stack([gg[0] for gg in ggs]),
        "k_norm": jnp.stack([gg[1] for gg in ggs]),
    }
    return loss, dx, small, tok


def _mesh_pos():
    return lax.axis_index("x"), lax.axis_index("y"), lax.axis_index("c")


def _other_chips(x, y):
    return [(1 - x, y), (x, 1 - y), (1 - x, 1 - y)]


HBM = pl.BlockSpec(memory_space=pltpu.HBM)
SEM = pl.BlockSpec(memory_space=pltpu.SEMAPHORE)
SPLIT_COPY = pltpu.SideEffectType.DATAFLOW_SIDE_EFFECTING


def _hbm(a):
    return pltpu.with_memory_space_constraint(a, pltpu.HBM)


def _gather_copies(j, src, out, send, recv, x, y, c):
    me = 4 * x + 2 * y + c
    peers = [(x, y, 1 - c)] + [(px, py, c) for px, py in _other_chips(x, y)]
    return [pltpu.make_async_remote_copy(src_ref=src, dst_ref=out.at[me], send_sem=send.at[4 * j + k],
                                         recv_sem=recv.at[4 * j + k], device_id=peer, device_id_type=MESH)
            for k, peer in enumerate(peers)]


def _gather_start(shards, me, after, name):
    n = len(shards)
    lands = [lax.dynamic_update_slice(lax.empty((N_DEV,) + s.shape, s.dtype), s[None], (me, 0, 0)) for s in shards]

    def body(*refs):
        ins, outs = refs[:n], refs[2 * n + len(after):3 * n + len(after)]
        send, recv, token = refs[4 * n + len(after):]
        x, y, c = _mesh_pos()
        for j in range(n):
            for cp in _gather_copies(j, ins[j], outs[j], send, recv, x, y, c):
                cp.start()
        token[...] = jnp.zeros_like(token)

    res = pl.pallas_call(
        body, name=name, in_specs=[HBM] * (2 * n) + [ANY] * len(after),
        out_specs=[HBM] * (2 * n) + [SEM, SEM, pl.BlockSpec(memory_space=pltpu.VMEM)],
        out_shape=[pltpu.HBM((N_DEV,) + s.shape, s.dtype) for s in shards] + [pltpu.HBM(s.shape, s.dtype) for s in shards]
        + [pltpu.SemaphoreType.DMA((4 * n,)), pltpu.SemaphoreType.DMA((4 * n,)), jax.ShapeDtypeStruct((8, 128), F32)],
        input_output_aliases={**{i: n + i for i in range(n)}, **{n + i: i for i in range(n)}},
        compiler_params=pltpu.CompilerParams(has_side_effects=SPLIT_COPY),
    )(*[_hbm(s) for s in shards], *[_hbm(l) for l in lands], *after)
    return res[:n], res[n:2 * n], res[2 * n], res[2 * n + 1], res[2 * n + 2]


def _gather_wait(outs, thru, send, recv, after, name):
    n = len(outs)

    def body(*refs):
        out_refs, thru_refs, send_ref, recv_ref = refs[:n], refs[n:2 * n], refs[2 * n], refs[2 * n + 1]
        x, y, c = _mesh_pos()
        for j in range(n):
            for cp in _gather_copies(j, thru_refs[j], out_refs[j], send_ref, recv_ref, x, y, c):
                cp.wait_send()
                cp.wait_recv()

    res = pl.pallas_call(
        body, name=name, in_specs=[HBM] * (2 * n) + [SEM, SEM] + [ANY] * len(after), out_specs=[HBM] * (2 * n),
        out_shape=[pltpu.HBM(a.shape, a.dtype) for a in list(outs) + list(thru)],
        input_output_aliases={i: i for i in range(2 * n)},
        compiler_params=pltpu.CompilerParams(has_side_effects=SPLIT_COPY),
    )(*outs, *thru, send, recv, *after)
    return res[:n]


def _gather_forward(outs, name):
    n = len(outs)

    def body(*refs):
        o = refs[n:2 * n]
        send_sems, recv_sems = refs[2 * n:]
        x, y, c = _mesh_pos()
        copies = []
        for j in range(n):
            for k, (px, py) in enumerate(_other_chips(x, y)):
                blk = o[j].at[4 * px + 2 * py + c]
                cp = pltpu.make_async_remote_copy(src_ref=blk, dst_ref=blk, send_sem=send_sems.at[j, k],
                                                  recv_sem=recv_sems.at[j, k], device_id=(x, y, 1 - c), device_id_type=MESH)
                cp.start()
                copies.append(cp)
        for cp in copies:
            cp.wait()

    return pl.pallas_call(
        body, name=name, in_specs=[ANY] * n, out_specs=[ANY] * n,
        out_shape=[jax.ShapeDtypeStruct(a.shape, a.dtype) for a in outs],
        input_output_aliases={i: i for i in range(n)},
        scratch_shapes=[pltpu.SemaphoreType.DMA((n, 3)), pltpu.SemaphoreType.DMA((n, 3))],
    )(*outs)


def _all_sum(p, name):
    R, C = p.shape

    def body(p_ref, o_ref, buf, send_sems, recv_sems):
        x, y, c = _mesh_pos()
        me = 4 * x + 2 * y + c
        buf[me] = p_ref[...]
        copies = []
        for rel in range(1, N_DEV):
            peer = (1 - x if rel & 4 else x, 1 - y if rel & 2 else y, 1 - c if rel & 1 else c)
            cp = pltpu.make_async_remote_copy(
                src_ref=p_ref, dst_ref=buf.at[me], send_sem=send_sems.at[rel - 1], recv_sem=recv_sems.at[rel - 1],
                device_id=peer, device_id_type=MESH)
            cp.start()
            copies.append(cp)
        for cp in copies:
            cp.wait()
        acc = buf[0]
        for dev in range(1, N_DEV):
            acc = acc + buf[dev]
        o_ref[...] = acc

    vm = pl.BlockSpec(memory_space=pltpu.VMEM)
    return pl.pallas_call(
        body, name=name, in_specs=[vm], out_specs=vm, out_shape=jax.ShapeDtypeStruct((R, C), F32),
        scratch_shapes=[pltpu.VMEM((N_DEV, R, C), F32), pltpu.SemaphoreType.DMA((N_DEV - 1,)),
                        pltpu.SemaphoreType.DMA((N_DEV - 1,))],
    )(p)


def _swap_with_sibling(gs, name):
    n = len(gs)

    def body(*refs):
        ins, outs = refs[:n], refs[n:2 * n]
        send_sems, recv_sems = refs[2 * n:]
        x, y, c = _mesh_pos()
        copies = []
        for i in range(n):
            for k in range(4):
                cp = pltpu.make_async_remote_copy(
                    src_ref=ins[i].at[2 * k + (1 - c)], dst_ref=outs[i].at[k],
                    send_sem=send_sems.at[i, k], recv_sem=recv_sems.at[i, k],
                    device_id=(x, y, 1 - c), device_id_type=MESH)
                cp.start()
                copies.append(cp)
        for cp in copies:
            cp.wait()

    return pl.pallas_call(
        body, name=name, in_specs=[ANY] * n, out_specs=[ANY] * n,
        out_shape=[jax.ShapeDtypeStruct((4,) + g.shape[1:], g.dtype) for g in gs],
        scratch_shapes=[pltpu.SemaphoreType.DMA((n, 4)), pltpu.SemaphoreType.DMA((n, 4))],
    )(*gs)


def _scatter_copies(j, src, land, send, recv, x, y, c):
    return [pltpu.make_async_remote_copy(src_ref=src.at[2 * px + py], dst_ref=land.at[k], send_sem=send.at[3 * j + k],
                                         recv_sem=recv.at[3 * j + k], device_id=(px, py, c), device_id_type=MESH)
            for k, (px, py) in enumerate(_other_chips(x, y))]


def _scatter_start(ps, name):
    n = len(ps)

    def body(*refs):
        ins, lands = refs[:n], refs[2 * n:3 * n]
        send, recv, token = refs[3 * n:]
        x, y, c = _mesh_pos()
        for j in range(n):
            for cp in _scatter_copies(j, ins[j], lands[j], send, recv, x, y, c):
                cp.start()
        token[...] = jnp.zeros_like(token)

    res = pl.pallas_call(
        body, name=name, in_specs=[HBM] * n,
        out_specs=[HBM] * (2 * n) + [SEM, SEM, pl.BlockSpec(memory_space=pltpu.VMEM)],
        out_shape=[pltpu.HBM(p.shape, p.dtype) for p in ps] + [pltpu.HBM((3,) + p.shape[1:], p.dtype) for p in ps]
        + [pltpu.SemaphoreType.DMA((3 * n,)), pltpu.SemaphoreType.DMA((3 * n,)), jax.ShapeDtypeStruct((8, 128), F32)],
        input_output_aliases={i: i for i in range(n)},
        compiler_params=pltpu.CompilerParams(has_side_effects=SPLIT_COPY),
    )(*[_hbm(p) for p in ps])
    return res[:n], res[n:2 * n], res[2 * n], res[2 * n + 1], res[2 * n + 2]


def _scatter_wait(thru, lands, send, recv, after, name):
    n = len(thru)

    def body(*refs):
        thru_refs, land_refs, send_ref, recv_ref = refs[:n], refs[n:2 * n], refs[2 * n], refs[2 * n + 1]
        x, y, c = _mesh_pos()
        for j in range(n):
            for cp in _scatter_copies(j, thru_refs[j], land_refs[j], send_ref, recv_ref, x, y, c):
                cp.wait_send()
                cp.wait_recv()

    res = pl.pallas_call(
        body, name=name, in_specs=[HBM] * (2 * n) + [SEM, SEM, ANY], out_specs=[HBM] * (2 * n),
        out_shape=[pltpu.HBM(a.shape, a.dtype) for a in list(thru) + list(lands)],
        input_output_aliases={i: i for i in range(2 * n)},
        compiler_params=pltpu.CompilerParams(has_side_effects=SPLIT_COPY),
    )(*thru, *lands, send, recv, after)
    return res[:n], res[n:]


def _row_tile(r, pref):
    if r <= pref:
        return r
    for t in range(pref - pref % 16, 0, -16):
        if r % t == 0:
            return t
    return r


def _add_sibling(g, r1, pos, name):
    _, r, D = g.shape
    tr = _row_tile(r, 512)

    def body(pos_ref, g_ref, r_ref, o_ref):
        o_ref[...] = (g_ref[...].astype(F32) + r_ref[...].astype(F32)).astype(BF16)

    return pl.pallas_call(
        body, name=name,
        grid_spec=pltpu.PrefetchScalarGridSpec(
            num_scalar_prefetch=1, grid=(4, r // tr),
            in_specs=[pl.BlockSpec((None, None, tr, D), lambda k, j, pos: (k, pos[0], j, 0)),
                      pl.BlockSpec((None, tr, D), lambda k, j, pos: (k, j, 0))],
            out_specs=pl.BlockSpec((None, tr, D), lambda k, j, pos: (k, j, 0))),
        out_shape=jax.ShapeDtypeStruct((4, r, D), BF16),
        compiler_params=_params("parallel", "parallel"),
    )(pos, g.reshape(4, 2, r, D), r1)


def _add_chips(p, r2, pos, name):
    _, r, D = p.shape
    tr = _row_tile(r, 512)

    def body(pos_ref, p_ref, r_ref, o_ref):
        acc = p_ref[...].astype(F32)
        for j in range(3):
            acc = acc + r_ref[j].astype(F32)
        o_ref[...] = acc

    return pl.pallas_call(
        body, name=name,
        grid_spec=pltpu.PrefetchScalarGridSpec(
            num_scalar_prefetch=1, grid=(r // tr,),
            in_specs=[pl.BlockSpec((None, tr, D), lambda j, pos: (pos[1], j, 0)),
                      pl.BlockSpec((3, tr, D), lambda j, pos: (0, j, 0))],
            out_specs=pl.BlockSpec((tr, D), lambda j, pos: (j, 0))),
        out_shape=jax.ShapeDtypeStruct((r, D), F32),
        compiler_params=_params("parallel"),
    )(pos, p, r2)


def _place_cols(local, me, width):
    R, w = local.shape
    return lax.dynamic_update_slice(jnp.zeros((R, width), F32), local, (0, me * w))


def _pad_rows(a, rows):
    return jnp.pad(a, ((0, rows - a.shape[0]), (0, 0)))


SUBLAYER_SHARDS = ((0, 4), (8, 9), (1, 5), (2, 6), (10, 11), (3, 7))
GATHER_UNITS = ((0,), (4,), (8, 9), (1, 5), (2, 6), (10, 11), (3, 7))
SUBLAYER_UNITS = ((0, 1), (2,), (3,), (4,), (5,), (6,))
GATHERS_AHEAD = 3


class _StepComm:
    def __init__(self, shards, pos):
        self.pos = pos
        self.shards = shards
        self.gathers = {}
        tok = ()
        for u in range(GATHERS_AHEAD):
            tok = self._start_gather(u, tok)
        self.started = tok
        self.next_unit = GATHERS_AHEAD
        self.pending = {}

    def _start_gather(self, u, after):
        outs, thru, send, recv, token = _gather_start([self.shards[i] for i in GATHER_UNITS[u]],
                                                      2 * self.pos[1] + self.pos[0], after, f"gather_start_{u}")
        self.gathers[u] = (outs, thru, send, recv)
        return (token,)

    def _take(self, u, after):
        outs = _gather_wait(*self.gathers.pop(u), (after,), f"gather_wait_{u}")
        outs = _gather_forward(outs, f"gather_forward_{u}")
        tok = ()
        if self.next_unit < len(GATHER_UNITS):
            tok = self._start_gather(self.next_unit, (outs[0],))
            self.next_unit += 1
        return [w.reshape(-1, w.shape[-1]) for w in outs], tok

    def weights(self, s, after):
        return self._take(SUBLAYER_UNITS[s][0], after)

    def second(self, s, after):
        return self._take(SUBLAYER_UNITS[s][1], after)

    def grads(self, s, mats):
        g3 = [g.reshape(N_DEV, g.shape[0] // N_DEV, g.shape[1]) for g in mats]
        r1 = _swap_with_sibling(g3, f"rs_sibling_{s}")
        ps = [_add_sibling(g, r, self.pos, f"rs_add_sibling_{s}_{i}") for i, (g, r) in enumerate(zip(g3, r1))]
        thru, lands, send, recv, token = _scatter_start(ps, f"rs_start_{s}")
        self.pending[s] = (thru, lands, send, recv)
        return (token,)

    def finish(self, after):
        out = {}
        for s in sorted(self.pending, reverse=True):
            ps, lands = _scatter_wait(*self.pending[s], after, f"rs_wait_{s}")
            for i, p, r2 in zip(SUBLAYER_SHARDS[s], ps, lands):
                out[i] = _add_chips(p, r2, self.pos, f"rs_add_chips_{i}")
        return out


def kernel(x, norm_g, ffn_w_in, ffn_w_out, conv_w_pw1, conv_b_pw1, conv_w_dw, conv_b_dw, conv_norm_g, conv_w_pw2, conv_b_pw2, attn_w_qkv, attn_q_norm, attn_k_norm, attn_w_o, loss_target, m_norm_g, m_ffn_w_in, m_ffn_w_out, m_conv_w_pw1, m_conv_b_pw1, m_conv_w_dw, m_conv_b_dw, m_conv_norm_g, m_conv_w_pw2, m_conv_b_pw2, m_attn_w_qkv, m_attn_q_norm, m_attn_k_norm, m_attn_w_o, v_norm_g, v_ffn_w_in, v_ffn_w_out, v_conv_w_pw1, v_conv_b_pw1, v_conv_w_dw, v_conv_b_dw, v_conv_norm_g, v_conv_w_pw2, v_conv_b_pw2, v_attn_w_qkv, v_attn_q_norm, v_attn_k_norm, v_attn_w_o):
    D = x.shape[2]
    xi, yi, ci = _mesh_pos()
    me = 4 * xi + 2 * yi + ci
    pos = jnp.stack([ci, 2 * xi + yi]).astype(I32)
    lf = [(l, f) for l in range(2) for f in range(2)]

    shards = ([ffn_w_in[l, f].T.astype(BF16) for l, f in lf] + [ffn_w_out[l, f].astype(BF16) for l, f in lf]
              + [conv_w_pw1[0].T.astype(BF16), conv_w_pw2[0].astype(BF16),
                 attn_w_qkv[0].T.astype(BF16), attn_w_o[0].astype(BF16)])
    comm = _StepComm(shards, pos)
    w_cols = norm_g.shape[2]
    vec = _all_sum(_pad_rows(jnp.concatenate([_place_cols(norm_g.reshape(6, w_cols), me, D),
                                              _place_cols(conv_w_dw[0], me, D)]), 40), "gather_vectors")
    vecs = {
        "norm_g": vec[0:6], "w_dw": vec[6:6 + CONV_WIDTH],
        "b_pw1": conv_b_pw1.reshape(2, D), "b_dw": conv_b_dw, "cg": conv_norm_g, "b_pw2": conv_b_pw2,
        "q_norm": attn_q_norm[0], "k_norm": attn_k_norm[0],
    }

    loss_local, dx, small, tok = _local_step(x[0], loss_target[0], vecs, comm)

    mats = comm.finish(dx)
    hd3 = 3 * HEAD_DIM
    packed = jnp.concatenate([
        small["norm_g"], small["b_pw1"], small["w_dw"], small["b_dw"], small["cg"], small["b_pw2"],
        jnp.pad(small["q_norm"].reshape(1, hd3), ((0, 0), (0, D - hd3))),
        jnp.pad(small["k_norm"].reshape(1, hd3), ((0, 0), (0, D - hd3))),
        jnp.full((1, D), loss_local, F32)])
    red = _all_sum(_pad_rows(packed, 48), "reduce_vectors")
    loss = red[44, 0]
    col0 = me * w_cols
    grads = {
        "norm_g": lax.dynamic_slice(red[0:6], (0, col0), (6, w_cols)),
        "ffn_w_in": jnp.concatenate([mats[i].T for i in range(4)]),
        "ffn_w_out": jnp.concatenate([mats[i] for i in range(4, 8)]),
        "conv_w_pw1": mats[8].T,
        "conv_b_pw1": red[6:8].reshape(1, 2 * D),
        "conv_w_dw": lax.dynamic_slice(red[8:8 + CONV_WIDTH], (0, col0), (CONV_WIDTH, w_cols)),
        "conv_b_dw": red[39:40], "conv_norm_g": red[40:41], "conv_w_pw2": mats[9], "conv_b_pw2": red[41:42],
        "attn_w_qkv": mats[10].T,
        "attn_q_norm": red[42, :hd3].reshape(3, HEAD_DIM), "attn_k_norm": red[43, :hd3].reshape(3, HEAD_DIM),
        "attn_w_o": mats[11],
    }

    names = ["norm_g", "ffn_w_in", "ffn_w_out", "conv_w_pw1", "conv_b_pw1", "conv_w_dw", "conv_b_dw",
             "conv_norm_g", "conv_w_pw2", "conv_b_pw2", "attn_w_qkv", "attn_q_norm", "attn_k_norm", "attn_w_o"]
    ws = [norm_g, ffn_w_in, ffn_w_out, conv_w_pw1, conv_b_pw1, conv_w_dw, conv_b_dw, conv_norm_g, conv_w_pw2,
          conv_b_pw2, attn_w_qkv, attn_q_norm, attn_k_norm, attn_w_o]
    ms = [m_norm_g, m_ffn_w_in, m_ffn_w_out, m_conv_w_pw1, m_conv_b_pw1, m_conv_w_dw, m_conv_b_dw, m_conv_norm_g,
          m_conv_w_pw2, m_conv_b_pw2, m_attn_w_qkv, m_attn_q_norm, m_attn_k_norm, m_attn_w_o]
    vs = [v_norm_g, v_ffn_w_in, v_ffn_w_out, v_conv_w_pw1, v_conv_b_pw1, v_conv_w_dw, v_conv_b_dw, v_conv_norm_g,
          v_conv_w_pw2, v_conv_b_pw2, v_attn_w_qkv, v_attn_q_norm, v_attn_k_norm, v_attn_w_o]
    g_out, d_out, m_out, v_out = [], [], [], []
    for name, w, m, v in zip(names, ws, ms, vs):
        shape2 = (-1, w.shape[-1]) if w.size > SMALL_PARAM else w.shape
        g2 = grads[name].reshape(shape2)
        delta, new_m, new_v = _adamw(w.reshape(shape2), g2, m.reshape(shape2), v.reshape(shape2), f"adamw_{name}")
        g_out.append(g2.reshape(w.shape))
        d_out.append(delta.reshape(w.shape))
        m_out.append(new_m.reshape(w.shape))
        v_out.append(new_v.reshape(w.shape))
    return (loss, dx.reshape(x.shape), *g_out, *d_out, *m_out, *v_out)
```
